```python
import jax, jax.numpy as jnp
from jax import lax
import numpy as np

D_MODEL = 1024
BATCH = 16
SEQ = 256
DEPTH = 2
DEC_BATCH = 2
DEC_SEQ = 1024
PAST_LEN = 512

GRID_W = 64
HEAD_DIM = 64
A_HEADS = 6
A_KV_HEADS = 2
A_GROUP = A_HEADS // A_KV_HEADS
WINDOW = 128
B_HEADS = 4
B_DK = 64
B_DV = 64
CHUNK = 64
F_BIAS = 3.0
C_HEADS = 6
C_Q_RANK = 256
C_KV_RANK = 128
C_NOPE = 64
C_ROPE = 32
C_QK = C_NOPE + C_ROPE
C_V = 64
A_Q = A_HEADS * HEAD_DIM
A_KV = A_KV_HEADS * HEAD_DIM
B_QK = B_HEADS * B_DK
B_V = B_HEADS * B_DV
C_OUT = C_HEADS * C_V
D_MIX = A_Q + B_V + C_OUT
IN_SIZES = (A_Q, A_KV, A_KV, B_QK, B_QK, B_V, B_V, 4 * B_HEADS, C_Q_RANK, C_KV_RANK, C_ROPE)
IN_SPLITS = tuple(sum(IN_SIZES[:i + 1]) for i in range(len(IN_SIZES) - 1))
D_IN = sum(IN_SIZES)
D_FF = 2816
N_EXPERTS = 8
TOP_K = 2
D_FF_EXPERT = 3584
N_DENSE = (DEPTH + 1) // 2
N_MOE = DEPTH // 2
Q_BLOCK = 128
ROPE_THETA = 10000.0
EPS = 1e-6
NEG_INF = -1e30

kernel_name = 'hybrid_diffusion_swa_mlstm_mla_step'


def rms_norm(x, g):
    xf = x.astype(jnp.float32)
    y = xf * lax.rsqrt(jnp.mean(xf * xf, axis=-1, keepdims=True) + EPS)
    return (y * g.astype(jnp.float32)).astype(x.dtype)


def grid_angles(seq_len, rot_dim):
    rows = seq_len // GRID_W
    row = jnp.repeat(jnp.arange(rows), GRID_W).astype(jnp.float32)
    col = jnp.tile(jnp.arange(GRID_W), rows).astype(jnp.float32)
    n_freq = rot_dim // 4
    freq = ROPE_THETA ** (-jnp.arange(n_freq, dtype=jnp.float32) / n_freq)
    ang = jnp.concatenate([row[:, None] * freq, col[:, None] * freq], axis=-1)
    return jnp.cos(ang), jnp.sin(ang)


def apply_rope(x, cos, sin):
    half = x.shape[-1] // 2
    shape = (1, cos.shape[0]) + (1,) * (x.ndim - 3) + (half,)
    c, s = cos.reshape(shape), sin.reshape(shape)
    x1, x2 = x[..., :half], x[..., half:]
    return jnp.concatenate([x1 * c - x2 * s, x1 * s + x2 * c], axis=-1).astype(x.dtype)


def rope_tail(x, cos, sin):
    return jnp.concatenate([x[..., :C_NOPE], apply_rope(x[..., C_NOPE:], cos, sin)], axis=-1)


def attend_block(q, k, v, bias, sink):
    scale = q.shape[-1] ** -0.5
    s = jnp.einsum('bqgrd,bkgd->bgrqk', q, k).astype(jnp.float32) * scale
    if bias is not None:
        s = s + bias
    if sink is not None:
        sk = jnp.broadcast_to(sink.astype(jnp.float32)[None, :, :, None, None], s.shape[:-1] + (1,))
        p = jax.nn.softmax(jnp.concatenate([s, sk], axis=-1), axis=-1)[..., :-1]
    else:
        p = jax.nn.softmax(s, axis=-1)
    return jnp.einsum('bgrqk,bkgd->bqgrd', p.astype(v.dtype), v)


def to_qblocks(q):
    b, s = q.shape[0], q.shape[1]
    return jnp.moveaxis(q.reshape((b, s // Q_BLOCK, Q_BLOCK) + q.shape[2:]), 1, 0)


def from_qblocks(ob, q_shape):
    return jnp.moveaxis(ob, 0, 1).reshape(q_shape[:-1] + (ob.shape[-1],))


def dense_attention(q, k, v, sink):
    ob = lax.map(lambda qi: attend_block(qi, k, v, None, sink), to_qblocks(q))
    return from_qblocks(ob, q.shape)


def window_attention(q, k, v, k_ctx, v_ctx, sink):
    s = q.shape[1]
    nb = s // Q_BLOCK
    band = Q_BLOCK + 2 * WINDOW
    pad = [(0, 0), (WINDOW, WINDOW), (0, 0), (0, 0)]
    kp, vp = jnp.pad(k, pad), jnp.pad(v, pad)
    n_ctx = k_ctx.shape[1]
    ctx_bias = jnp.zeros((Q_BLOCK, n_ctx), jnp.float32)

    def block(args):
        qi, n = args
        start = n * Q_BLOCK
        kb = lax.dynamic_slice_in_dim(kp, start, band, axis=1)
        vb = lax.dynamic_slice_in_dim(vp, start, band, axis=1)
        qpos = start + jnp.arange(Q_BLOCK)
        kpos = start - WINDOW + jnp.arange(band)
        ok = (jnp.abs(qpos[:, None] - kpos[None, :]) <= WINDOW) & ((kpos >= 0) & (kpos < s))[None, :]
        bias = jnp.concatenate([jnp.where(ok, 0.0, NEG_INF).astype(jnp.float32), ctx_bias], axis=1)
        return attend_block(qi, jnp.concatenate([kb, k_ctx.astype(kb.dtype)], axis=1),
                            jnp.concatenate([vb, v_ctx.astype(vb.dtype)], axis=1), bias, sink)

    ob = lax.map(block, (to_qblocks(q), jnp.arange(nb)))
    return from_qblocks(ob, q.shape)


def mlstm_scan(q, k, v, log_i, log_f, C0, n0, m0):
    b, s, h, _ = q.shape
    dv = v.shape[-1]
    nc = s // CHUNK

    def chunks(t):
        return jnp.moveaxis(t.astype(jnp.float32).reshape((b, nc, CHUNK) + t.shape[2:]), 1, 0)

    causal = jnp.tril(jnp.ones((CHUNK, CHUNK), bool))[None, :, :, None]

    def step(carry, xs):
        C, n, m = carry
        qt, kt, vt, li, lf = xs
        bc = jnp.cumsum(lf, axis=1)
        d = jnp.where(causal, bc[:, :, None, :] - bc[:, None, :, :] + li[:, None, :, :], NEG_INF)
        inter = bc + m[:, None, :]
        mt = jnp.maximum(inter, jnp.max(d, axis=2))
        a = jnp.exp(inter - mt)
        w = jnp.exp(d - mt[:, :, None, :]) * jnp.einsum('bthd,bshd->btsh', qt, kt)
        num = a[..., None] * jnp.einsum('bthd,bhde->bthe', qt, C) + jnp.einsum('btsh,bshe->bthe', w, vt)
        den = a * jnp.einsum('bthd,bhd->bth', qt, n) + jnp.sum(w, axis=2)
        ht = num / jnp.maximum(jnp.abs(den), jnp.exp(-mt))[..., None]
        bl = bc[:, -1, :]
        g = bl[:, None, :] - bc + li
        m_new = jnp.maximum(bl + m, jnp.max(g, axis=1))
        ws = jnp.exp(g - m_new[:, None, :])
        decay = jnp.exp(bl + m - m_new)
        C_new = decay[..., None, None] * C + jnp.einsum('bsh,bshd,bshe->bhde', ws, kt, vt)
        n_new = decay[..., None] * n + jnp.einsum('bsh,bshd->bhd', ws, kt)
        return (C_new, n_new, m_new), ht

    carry0 = (C0.astype(jnp.float32), n0.astype(jnp.float32), m0.astype(jnp.float32))
    (C, n, m), hc = lax.scan(step, carry0, (chunks(q), chunks(k), chunks(v), chunks(log_i), chunks(log_f)))
    return jnp.moveaxis(hc, 0, 1).reshape(b, s, h, dv), C, n, m


def mlstm_mixer(bq, bk, bv, bo, bg, gate_b, hn_g, C0, n0, m0):
    b, s = bq.shape[0], bq.shape[1]
    q = bq.reshape(b, s, B_HEADS, B_DK)
    k = bk.reshape(b, s, B_HEADS, B_DK) * (B_DK ** -0.5)
    v = bv.reshape(b, s, B_HEADS, B_DV)
    g = (bg.astype(jnp.float32) + gate_b.astype(jnp.float32)).reshape(b, s, 2, 2, B_HEADS)
    log_i = g[:, :, :, 0]
    log_f = jax.nn.log_sigmoid(g[:, :, :, 1])
    hf, Cf, nf, mf = mlstm_scan(q, k, v, log_i[:, :, 0], log_f[:, :, 0], C0[:, 0], n0[:, 0], m0[:, 0])
    rev = lambda t: jnp.flip(t, axis=1)
    hb, Cb, nb, mb = mlstm_scan(rev(q), rev(k), rev(v), rev(log_i[:, :, 1]), rev(log_f[:, :, 1]),
                                C0[:, 1], n0[:, 1], m0[:, 1])
    hsum = hf + rev(hb)
    out = rms_norm(hsum, hn_g) * jax.nn.sigmoid(bo.astype(jnp.float32).reshape(b, s, B_HEADS, B_DV))
    out = out.astype(bq.dtype).reshape(b, s, B_V)
    return out, jnp.stack([Cf, Cb], axis=1), jnp.stack([nf, nb], axis=1), jnp.stack([mf, mb], axis=1)


def mla_kv(ckv_n, k_rope, wukv, kn_g):
    b, s = ckv_n.shape[0], ckv_n.shape[1]
    kv = (ckv_n @ wukv).reshape(b, s, C_HEADS, C_NOPE + C_V)
    kr = jnp.broadcast_to(k_rope[:, :, None, :], (b, s, C_HEADS, C_ROPE)).astype(kv.dtype)
    k = rms_norm(jnp.concatenate([kv[..., :C_NOPE], kr], axis=-1), kn_g)
    return k, kv[..., C_NOPE:]


def token_mixer(xn, lw, ctx):
    b, s = xn.shape[0], xn.shape[1]
    aq, ak, av, bq, bk, bv, bo, bg, cq, ckv, ckr = jnp.split(xn @ lw['w_in'], IN_SPLITS, axis=-1)
    qa = rms_norm(aq.reshape(b, s, A_KV_HEADS, A_GROUP, HEAD_DIM), lw['a_qn_g'])
    ka = rms_norm(ak.reshape(b, s, A_KV_HEADS, HEAD_DIM), lw['a_kn_g'])
    va = av.reshape(b, s, A_KV_HEADS, HEAD_DIM)
    sink = lw['a_sink'].reshape(A_KV_HEADS, A_GROUP)
    ckv_n = rms_norm(ckv, lw['c_kva_g'])
    qc = rms_norm((rms_norm(cq, lw['c_qa_g']) @ lw['c_wuq']).reshape(b, s, C_HEADS, C_QK), lw['c_qn_g'])
    kc, vc = mla_kv(ckv_n, ckr, lw['c_wukv'], lw['c_kn_g'])
    if ctx is None:
        oa = dense_attention(qa, ka, va, sink)
        zC = jnp.zeros((b, 2, B_HEADS, B_DK, B_DV), jnp.float32)
        zn = jnp.zeros((b, 2, B_HEADS, B_DK), jnp.float32)
        zm = jnp.zeros((b, 2, B_HEADS), jnp.float32)
        ob, C, n, m = mlstm_mixer(bq, bk, bv, bo, bg, lw['b_gate_b'], lw['b_hn_g'], zC, zn, zm)
        oc = dense_attention(qc[:, :, :, None, :], kc, vc, None)
        new = (ka, va, ckv_n, ckr, C, n, m)
    else:
        k_ctx, v_ctx, ckv_ctx, kr_ctx, C0, n0, m0 = ctx
        cos, sin = grid_angles(s, HEAD_DIM)
        oa = window_attention(apply_rope(qa, cos, sin), apply_rope(ka, cos, sin), va, k_ctx, v_ctx, sink)
        ob, _, _, _ = mlstm_mixer(bq, bk, bv, bo, bg, lw['b_gate_b'], lw['b_hn_g'], C0, n0, m0)
        cos_r, sin_r = grid_angles(s, C_ROPE)
        kc_ctx, vc_ctx = mla_kv(ckv_ctx, kr_ctx, lw['c_wukv'], lw['c_kn_g'])
        k_all = jnp.concatenate([kc_ctx.astype(kc.dtype), rope_tail(kc, cos_r, sin_r)], axis=1)
        v_all = jnp.concatenate([vc_ctx.astype(vc.dtype), vc], axis=1)
        oc = dense_attention(rope_tail(qc, cos_r, sin_r)[:, :, :, None, :], k_all, v_all, None)
        new = None
    o = jnp.concatenate([oa.reshape(b, s, A_Q), ob.astype(oa.dtype), oc.reshape(b, s, C_OUT)], axis=-1)
    return o @ lw['w_out'], new


def swiglu(x, w1, w3, w2):
    return (jax.nn.silu(x @ w1) * (x @ w3)) @ w2


def moe_ffn(x, router, w1, w3, w2):
    logits = (x @ router).astype(jnp.float32)
    top_v, top_i = lax.top_k(logits, TOP_K)
    wts = jax.nn.softmax(top_v, axis=-1)
    gates = jnp.sum(jax.nn.one_hot(top_i, N_EXPERTS, dtype=jnp.float32) * wts[..., None], axis=-2)
    out = jnp.zeros(x.shape, jnp.float32)
    for e in range(N_EXPERTS):
        out = out + gates[..., e:e + 1] * swiglu(x, w1[e], w3[e], w2[e])
    return out.astype(x.dtype)


def setup_inputs(seed: int = 0) -> dict:
    key = jax.random.key(seed)
    keys = iter(jax.random.split(key, 40))

    def nrm(shape, scale=1.0):
        return scale * jax.random.normal(next(keys), shape, jnp.float32)

    def gain(shape):
        return 1.0 + nrm(shape, 0.05)

    gate_offset = jnp.tile(jnp.repeat(jnp.array([0.0, F_BIAS], jnp.float32), B_HEADS), 2)
    return {
        'x_prompt': nrm((BATCH, SEQ, D_MODEL)),
        'x_sample': nrm((DEC_BATCH, DEC_SEQ, D_MODEL)),
        'c': nrm((DEC_BATCH, D_MODEL)),
        'cache_swa_k': nrm((DEC_BATCH, DEPTH, PAST_LEN, A_KV_HEADS, HEAD_DIM)),
        'cache_swa_v': nrm((DEC_BATCH, DEPTH, PAST_LEN, A_KV_HEADS, HEAD_DIM)),
        'cache_mla_ckv': nrm((DEC_BATCH, DEPTH, PAST_LEN, C_KV_RANK)),
        'cache_mla_krope': nrm((DEC_BATCH, DEPTH, PAST_LEN, C_ROPE)),
        'state_mlstm_C': nrm((DEC_BATCH, DEPTH, 2, B_HEADS, B_DK, B_DV), 0.5),
        'state_mlstm_n': nrm((DEC_BATCH, DEPTH, 2, B_HEADS, B_DK), 0.5),
        'state_mlstm_m': nrm((DEC_BATCH, DEPTH, 2, B_HEADS), 0.1),
        'c_ctx': nrm((D_MODEL,)),
        'ada_w': nrm((DEPTH, D_MODEL, 6 * D_MODEL), 0.5 * D_MODEL ** -0.5),
        'ada_b': nrm((DEPTH, 6 * D_MODEL), 0.02),
        'norm1_g': gain((DEPTH, D_MODEL)),
        'norm2_g': gain((DEPTH, D_MODEL)),
        'w_in': nrm((DEPTH, D_MODEL, D_IN), D_MODEL ** -0.5),
        'a_qn_g': gain((DEPTH, HEAD_DIM)),
        'a_kn_g': gain((DEPTH, HEAD_DIM)),
        'a_sink': nrm((DEPTH, A_HEADS), 0.5),
        'b_gate_b': nrm((DEPTH, 4 * B_HEADS), 0.1) + gate_offset,
        'b_hn_g': gain((DEPTH, B_DV)),
        'c_qa_g': gain((DEPTH, C_Q_RANK)),
        'c_kva_g': gain((DEPTH, C_KV_RANK)),
        'c_wuq': nrm((DEPTH, C_Q_RANK, C_HEADS * C_QK), C_Q_RANK ** -0.5),
        'c_wukv': nrm((DEPTH, C_KV_RANK, C_HEADS * (C_NOPE + C_V)), C_KV_RANK ** -0.5),
        'c_qn_g': gain((DEPTH, C_QK)),
        'c_kn_g': gain((DEPTH, C_QK)),
        'w_out': nrm((DEPTH, D_MIX, D_MODEL), D_MIX ** -0.5),
        'ffn_w1': nrm((N_DENSE, D_MODEL, D_FF), D_MODEL ** -0.5),
        'ffn_w3': nrm((N_DENSE, D_MODEL, D_FF), D_MODEL ** -0.5),
        'ffn_w2': nrm((N_DENSE, D_FF, D_MODEL), D_FF ** -0.5),
        'moe_router': nrm((N_MOE, D_MODEL, N_EXPERTS), D_MODEL ** -0.5),
        'moe_w1': nrm((N_MOE, N_EXPERTS, D_MODEL, D_FF_EXPERT), D_MODEL ** -0.5),
        'moe_w3': nrm((N_MOE, N_EXPERTS, D_MODEL, D_FF_EXPERT), D_MODEL ** -0.5),
        'moe_w2': nrm((N_MOE, N_EXPERTS, D_FF_EXPERT, D_MODEL), D_FF_EXPERT ** -0.5),
    }


def reference(x_prompt, x_sample, c, cache_swa_k, cache_swa_v, cache_mla_ckv, cache_mla_krope,
              state_mlstm_C, state_mlstm_n, state_mlstm_m, c_ctx, ada_w, ada_b, norm1_g, norm2_g,
              w_in, a_qn_g, a_kn_g, a_sink, b_gate_b, b_hn_g, c_qa_g, c_kva_g, c_wuq, c_wukv,
              c_qn_g, c_kn_g, w_out, ffn_w1, ffn_w3, ffn_w2, moe_router, moe_w1, moe_w3, moe_w2):

    def channel(xn, l):
        i = l // 2
        if l % 2 == 0:
            return swiglu(xn, ffn_w1[i], ffn_w3[i], ffn_w2[i])
        return moe_ffn(xn, moe_router[i], moe_w1[i], moe_w3[i], moe_w2[i])

    def block(x, cvec, l, ctx):
        lw = {'w_in': w_in[l], 'a_qn_g': a_qn_g[l], 'a_kn_g': a_kn_g[l], 'a_sink': a_sink[l],
              'b_gate_b': b_gate_b[l], 'b_hn_g': b_hn_g[l], 'c_qa_g': c_qa_g[l], 'c_kva_g': c_kva_g[l],
              'c_wuq': c_wuq[l], 'c_wukv': c_wukv[l], 'c_qn_g': c_qn_g[l], 'c_kn_g': c_kn_g[l],
              'w_out': w_out[l]}
        mod = (jax.nn.silu(cvec) @ ada_w[l] + ada_b[l])[:, None, :]
        sh1, sc1, g1, sh2, sc2, g2 = jnp.split(mod, 6, axis=-1)
        o, new = token_mixer(rms_norm(x, norm1_g[l]) * (1 + sc1) + sh1, lw, ctx)
        x = x + g1 * o
        x = x + g2 * channel(rms_norm(x, norm2_g[l]) * (1 + sc2) + sh2, l)
        return x, new

    y_prompt, y_sample = x_prompt, x_sample
    news = []
    for l in range(DEPTH):
        y_prompt, new = block(y_prompt, c_ctx[None, :], l, None)
        news.append(new)
        ctx = (cache_swa_k[:, l], cache_swa_v[:, l], cache_mla_ckv[:, l], cache_mla_krope[:, l],
               state_mlstm_C[:, l], state_mlstm_n[:, l], state_mlstm_m[:, l])
        y_sample, _ = block(y_sample, c, l, ctx)

    def stack(j):
        return jnp.stack([nw[j] for nw in news], axis=1)

    new_swa_k = stack(0)
    new_swa_v = stack(1)
    new_mla_ckv = stack(2)
    new_mla_krope = stack(3)
    new_mlstm_C = stack(4)
    new_mlstm_n = stack(5)
    new_mlstm_m = stack(6)
    return (y_prompt, y_sample, new_swa_k, new_swa_v, new_mla_ckv, new_mla_krope, new_mlstm_C, new_mlstm_n, new_mlstm_m)
```

```python
import functools

import jax
import jax.numpy as jnp
from jax import lax
from jax.experimental import pallas as pl
from jax.experimental.pallas import tpu as pltpu

F32 = jnp.float32
BF16 = jnp.bfloat16

D = 1024
NB_P, S_P = 16, 256
NB_S, S_S = 2, 1024
PAST = 512
DEPTH = 2
T_P = NB_P * S_P
T_S = NB_S * S_S
T = T_P + T_S
TM = 256
NT = T // TM
NT_P = T_P // TM
HD = 64
A_HEADS, A_KV = 6, 2
B_HEADS = 4
CHUNK = 64
C_HEADS = 6
C_QRANK, C_KVRANK, C_NOPE, C_ROPE, C_V = 256, 128, 64, 32, 64
C_QK = C_NOPE + C_ROPE
D_FF = 2816
N_EXP = 8
D_FFE = 3584
EPS = 1e-6
NEG = -1e30
LANES = 128
VMEM_LIMIT = 48 * 1024 * 1024

SEG = dict(QA=(0, 384), KA=(384, 256), VA=(640, 256), BQ=(896, 256), BK=(1152, 256),
           BV=(1408, 256), BO=(1664, 256), GI=(1920, 256), GF=(2176, 256), CQ=(2432, 256),
           CKV=(2688, 128), CKR=(2816, 128))
NP_IN = 2944


def _cparams(sem, vmem=VMEM_LIMIT):
    return pltpu.CompilerParams(dimension_semantics=sem, vmem_limit_bytes=vmem)


def _dot(a, b):
    return jnp.dot(a, b, preferred_element_type=F32)


def _dot_nt(a, b):
    return lax.dot_general(a, b, (((1,), (1,)), ((), ())), preferred_element_type=F32)


def _dot_tn(a, b):
    return lax.dot_general(a, b, (((0,), (0,)), ((), ())), preferred_element_type=F32)


def _lane(shape):
    return lax.broadcasted_iota(jnp.int32, shape, len(shape) - 1)


def _mod_row(i):
    return jnp.where(i < NT_P, 0, 1 + (i - NT_P) // (S_S // TM))


def _rope_blk(i):
    return jnp.where((i >= NT_P) & (i < NT), 1 + (i - NT_P) % (S_S // TM), 0)


def _ada_kernel(cv_ref, w_ref, b_ref, o_ref):
    s = cv_ref[...]
    s = s * jax.nn.sigmoid(s)
    o_ref[0] = _dot(s.astype(BF16), w_ref[0].astype(BF16)) + b_ref[0]


def _ada_call(cv, ada_w, ada_b):
    tn = 1536
    return pl.pallas_call(
        _ada_kernel,
        grid=(DEPTH, 6 * D // tn),
        in_specs=[pl.BlockSpec((8, D), lambda l, j: (0, 0)),
                  pl.BlockSpec((1, D, tn), lambda l, j: (l, 0, j)),
                  pl.BlockSpec((1, 1, tn), lambda l, j: (l, 0, j))],
        out_specs=pl.BlockSpec((1, 8, tn), lambda l, j: (l, 0, j)),
        out_shape=jax.ShapeDtypeStruct((DEPTH, 8, 6 * D), F32),
        compiler_params=_cparams(("arbitrary", "arbitrary")),
        name="ada_mod",
    )(cv, ada_w, ada_b.reshape(DEPTH, 1, 6 * D))


def _half_norm(x, g, n):
    outs = []
    for j in range(x.shape[1] // LANES):
        xj = x[:, j * LANES:(j + 1) * LANES]
        lo = _lane(xj.shape) < HD
        s = xj * xj
        s_lo = jnp.sum(jnp.where(lo, s, 0.0), axis=-1, keepdims=True)
        s_hi = jnp.sum(jnp.where(lo, 0.0, s), axis=-1, keepdims=True)
        r = lax.rsqrt(jnp.where(lo, s_lo, s_hi) * (1.0 / n) + EPS)
        outs.append(xj * r)
    return jnp.concatenate(outs, axis=-1) * g


def _group_norm(x, g, n):
    outs = []
    for j in range(x.shape[1] // LANES):
        xj = x[:, j * LANES:(j + 1) * LANES]
        r = lax.rsqrt(jnp.sum(xj * xj, axis=-1, keepdims=True) * (1.0 / n) + EPS)
        outs.append(xj * r)
    return jnp.concatenate(outs, axis=-1) * g


def _rope(x, c, sa, sb, shift):
    outs = []
    for j in range(x.shape[1] // LANES):
        xj = x[:, j * LANES:(j + 1) * LANES]
        outs.append(xj * c + pltpu.roll(xj, LANES - shift, 1) * sa + pltpu.roll(xj, shift, 1) * sb)
    return jnp.concatenate(outs, axis=-1)


def _in_kernel(x_ref, sh_ref, sc_ref, n1_ref, w_ref, gqa_ref, gka_ref, gb_ref, gcq_ref, gckv_ref,
               wuq_ref, gqn_ref, ra_c, ra_a, ra_b, rc_c, rc_a, rc_b,
               qa_o, ka_o, va_o, kaf_o, vaf_o, bq_o, bk_o, bv_o, bo_o, li_o, lf_o, qc_o, ckv_o, ckr_o):
    i = pl.program_id(0)
    x = x_ref[...]
    xn = x * lax.rsqrt(jnp.mean(x * x, axis=-1, keepdims=True) + EPS) * n1_ref[...]
    xn = xn * (1.0 + sc_ref[0]) + sh_ref[0]
    p = _dot(xn.astype(BF16), w_ref[0])

    def seg(name):
        o, n = SEG[name]
        return p[:, o:o + n]

    qa = _half_norm(seg("QA"), gqa_ref[...], HD)
    ka = _half_norm(seg("KA"), gka_ref[...], HD)
    va = seg("VA")

    @pl.when(i < NT_P)
    def _():
        kaf_o[...] = jnp.concatenate([ka[:, 0:HD], ka[:, LANES:LANES + HD]], axis=-1)
        vaf_o[...] = jnp.concatenate([va[:, 0:HD], va[:, LANES:LANES + HD]], axis=-1)

    qa_o[...] = _rope(qa, ra_c[...], ra_a[...], ra_b[...], 32).astype(BF16)
    ka_o[...] = _rope(ka, ra_c[...], ra_a[...], ra_b[...], 32).astype(BF16)
    va_o[...] = va.astype(BF16)

    bq_o[...] = seg("BQ").astype(BF16)
    bk_o[...] = (seg("BK") * (HD ** -0.5)).astype(BF16)
    bv_o[...] = seg("BV").astype(BF16)
    bo_o[...] = seg("BO")
    gb = gb_ref[...]
    li_o[...] = seg("GI") + gb[:, 0:256]
    lf_o[...] = jax.nn.log_sigmoid(seg("GF") + gb[:, 256:512])

    cq = seg("CQ")
    cqn = cq * lax.rsqrt(jnp.mean(cq * cq, axis=-1, keepdims=True) + EPS) * gcq_ref[...]
    qc = _group_norm(_dot(cqn.astype(BF16), wuq_ref[0]), gqn_ref[...], C_QK)
    qc_o[...] = _rope(qc, rc_c[...], rc_a[...], rc_b[...], 16).astype(BF16)

    ckv = seg("CKV")
    ckv_o[...] = ckv * lax.rsqrt(jnp.mean(ckv * ckv, axis=-1, keepdims=True) + EPS) * gckv_ref[...]
    ckr_o[...] = seg("CKR")


def _in_call(x, mod3, l, n1g, w_in_p, gqa, gka, gb, gcq, gckv, wuq_p, gqn, rope_a, rope_c):
    row = lambda k: pl.BlockSpec((1, 1, D), lambda i: ((l * 8 + _mod_row(i)) * 6 + k, 0, 0))
    vec = lambda n: pl.BlockSpec((1, n), lambda i: (0, 0))
    tab = pl.BlockSpec((TM, LANES), lambda i: (_rope_blk(i), 0))
    tok = lambda n: pl.BlockSpec((TM, n), lambda i: (i, 0))
    tokp = pl.BlockSpec((TM, LANES), lambda i: (jnp.minimum(i, NT_P - 1), 0))
    o = lambda n, dt: jax.ShapeDtypeStruct((T, n), dt)
    return pl.pallas_call(
        _in_kernel,
        grid=(NT,),
        in_specs=[tok(D), row(0), row(1), vec(D),
                  pl.BlockSpec((1, D, NP_IN), lambda i: (l, 0, 0)),
                  vec(384), vec(256), vec(512), vec(256), vec(128),
                  pl.BlockSpec((1, C_QRANK, 768), lambda i: (l, 0, 0)), vec(768),
                  tab, tab, tab, tab, tab, tab],
        out_specs=[tok(384), tok(256), tok(256), tokp, tokp, tok(256), tok(256), tok(256), tok(256),
                   tok(256), tok(256), tok(768), tok(128), tok(128)],
        out_shape=[o(384, BF16), o(256, BF16), o(256, BF16),
                   jax.ShapeDtypeStruct((T_P, LANES), F32), jax.ShapeDtypeStruct((T_P, LANES), F32),
                   o(256, BF16), o(256, BF16), o(256, BF16), o(256, F32),
                   o(256, F32), o(256, F32), o(768, BF16), o(128, F32), o(128, F32)],
        compiler_params=_cparams(("arbitrary",)),
        name=f"in_proj_l{l}",
    )(x, mod3, mod3, n1g, w_in_p, gqa, gka, gb, gcq, gckv, wuq_p, gqn, *rope_a, *rope_c)


def _mlakv_kernel(ckv_ref, ckr_ref, wk_ref, wv_ref, gkn_ref, rc_c, rc_a, rc_b, k_o, v_o):
    c = ckv_ref[...].astype(BF16)
    k = _dot(c, wk_ref[0])
    kr = ckr_ref[...]
    k = k + jnp.concatenate([kr] * C_HEADS, axis=-1)
    k = _group_norm(k, gkn_ref[...], C_QK)
    k_o[...] = _rope(k, rc_c[...], rc_a[...], rc_b[...], 16).astype(BF16)
    v_o[...] = _dot(c, wv_ref[0]).astype(BF16)


def _mlakv_call(ckv_all, ckr_all, l, wk_p, wv_p, gkn, rope_c):
    r = ckv_all.shape[0]
    tab = pl.BlockSpec((TM, LANES), lambda i: (_rope_blk(i), 0))
    tok = lambda n: pl.BlockSpec((TM, n), lambda i: (i, 0))
    return pl.pallas_call(
        _mlakv_kernel,
        grid=(r // TM,),
        in_specs=[tok(128), tok(128),
                  pl.BlockSpec((1, C_KVRANK, 768), lambda i: (l, 0, 0)),
                  pl.BlockSpec((1, C_KVRANK, 384), lambda i: (l, 0, 0)),
                  pl.BlockSpec((1, 768), lambda i: (0, 0)), tab, tab, tab],
        out_specs=[tok(768), tok(384)],
        out_shape=[jax.ShapeDtypeStruct((r, 768), BF16), jax.ShapeDtypeStruct((r, 384), BF16)],
        compiler_params=_cparams(("arbitrary",)),
        name=f"mla_kv_l{l}",
    )(ckv_all, ckr_all, wk_p, wv_p, gkn, *rope_c)


def _softmax_pv(scores, values, sink):
    m = scores[0].max(axis=-1, keepdims=True)
    for s in scores[1:]:
        m = jnp.maximum(m, s.max(axis=-1, keepdims=True))
    if sink is not None:
        m = jnp.maximum(m, sink)
    es = [jnp.exp(s - m) for s in scores]
    den = es[0].sum(axis=-1, keepdims=True)
    for e in es[1:]:
        den = den + e.sum(axis=-1, keepdims=True)
    if sink is not None:
        den = den + jnp.exp(sink - m)
    inv = 1.0 / den
    out = None
    for e, v in zip(es, values):
        o = _dot((e * inv).astype(BF16), v)
        out = o if out is None else out + o
    return out


def _half(x, hi):
    lo = _lane(x.shape) < HD
    return jnp.where(lo != hi, x, jnp.zeros_like(x))


def _attn_a_prompt_kernel(sink_ref, q_ref, k_ref, v_ref, o_ref):
    q = q_ref[...]
    k = k_ref[...]
    v = v_ref[...]
    for j in range(A_HEADS // 2):
        acc = None
        for c in range(2):
            h = 2 * j + c
            g = h // (A_HEADS // A_KV)
            qh = _half(q[:, j * LANES:(j + 1) * LANES], c == 1)
            s = _dot_nt(qh, k[:, g * LANES:(g + 1) * LANES]) * (HD ** -0.5)
            vh = _half(v[:, g * LANES:(g + 1) * LANES], c == 1)
            o = _softmax_pv([s], [vh], sink_ref[h])
            acc = o if acc is None else acc + o
        o_ref[:, j * LANES:(j + 1) * LANES] = acc.astype(BF16)


def _attn_a_prompt_call(sink, qa, ka, va):
    return pl.pallas_call(
        _attn_a_prompt_kernel,
        grid_spec=pltpu.PrefetchScalarGridSpec(
            num_scalar_prefetch=1, grid=(NB_P,),
            in_specs=[pl.BlockSpec((S_P, 384), lambda b, s: (b, 0)),
                      pl.BlockSpec((S_P, 256), lambda b, s: (b, 0)),
                      pl.BlockSpec((S_P, 256), lambda b, s: (b, 0))],
            out_specs=pl.BlockSpec((S_P, 384), lambda b, s: (b, 0))),
        out_shape=jax.ShapeDtypeStruct((T_P, 384), BF16),
        compiler_params=_cparams(("arbitrary",)),
        name="attn_a_prompt",
    )(sink, qa, ka, va)


QB = 128
WIN = 128
BAND = QB + 2 * WIN


def _attn_a_sample_kernel(sink_ref, q_ref, k_ref, v_ref, kc_ref, vc_ref, o_ref):
    n = pl.program_id(1)
    ws = pl.multiple_of(jnp.clip(n * QB - WIN, 0, S_S - BAND), QB)
    q = q_ref[...]
    kb = k_ref[pl.ds(ws, BAND), :]
    vb = v_ref[pl.ds(ws, BAND), :]
    kc = kc_ref[0, 0].astype(BF16)
    vc = vc_ref[0, 0].astype(BF16)
    qpos = n * QB + lax.broadcasted_iota(jnp.int32, (QB, BAND), 0)
    kpos = ws + lax.broadcasted_iota(jnp.int32, (QB, BAND), 1)
    ok = jnp.abs(qpos - kpos) <= WIN
    lane = _lane((PAST, LANES))
    for j in range(A_HEADS // 2):
        acc = None
        for c in range(2):
            h = 2 * j + c
            g = h // (A_HEADS // A_KV)
            qh = _half(q[:, j * LANES:(j + 1) * LANES], c == 1)
            s_b = jnp.where(ok, _dot_nt(qh, kb[:, g * LANES:(g + 1) * LANES]) * (HD ** -0.5), NEG)
            kcg = jnp.where((lane < HD) == (g == 0), kc, jnp.zeros_like(kc))
            vcg = jnp.where((lane < HD) == (g == 0), vc, jnp.zeros_like(vc))
            if (g == 1) != (c == 1):
                kcg = pltpu.roll(kcg.astype(F32), HD, 1).astype(BF16)
                vcg = pltpu.roll(vcg.astype(F32), HD, 1).astype(BF16)
            s_c = _dot_nt(qh, kcg) * (HD ** -0.5)
            vh = _half(vb[:, g * LANES:(g + 1) * LANES], c == 1)
            o = _softmax_pv([s_b, s_c], [vh, vcg], sink_ref[h])
            acc = o if acc is None else acc + o
        o_ref[:, j * LANES:(j + 1) * LANES] = acc.astype(BF16)


def _attn_a_sample_call(sink, qa, ka, va, cache_k, cache_v, l):
    nqb = S_S // QB
    off = T_P // S_S
    return pl.pallas_call(
        _attn_a_sample_kernel,
        grid_spec=pltpu.PrefetchScalarGridSpec(
            num_scalar_prefetch=1, grid=(NB_S, nqb),
            in_specs=[pl.BlockSpec((QB, 384), lambda b, n, s: (T_P // QB + b * nqb + n, 0)),
                      pl.BlockSpec((S_S, 256), lambda b, n, s: (off + b, 0)),
                      pl.BlockSpec((S_S, 256), lambda b, n, s: (off + b, 0)),
                      pl.BlockSpec((1, 1, PAST, LANES), lambda b, n, s: (b, l, 0, 0)),
                      pl.BlockSpec((1, 1, PAST, LANES), lambda b, n, s: (b, l, 0, 0))],
            out_specs=pl.BlockSpec((QB, 384), lambda b, n, s: (b * nqb + n, 0))),
        out_shape=jax.ShapeDtypeStruct((T_S, 384), BF16),
        compiler_params=_cparams(("arbitrary", "arbitrary")),
        name=f"attn_a_sample_l{l}",
    )(sink, qa, ka, va, cache_k, cache_v)


def _mla_heads(q, ks, vs, o_ref):
    for j in range(C_HEADS // 2):
        acc = None
        for c in range(2):
            h = 2 * j + c
            qh = q[:, h * LANES:(h + 1) * LANES]
            scores = [_dot_nt(qh, k[:, h * LANES:(h + 1) * LANES]) * (C_QK ** -0.5) for k in ks]
            vals = [_half(v[:, j * LANES:(j + 1) * LANES], c == 1) for v in vs]
            o = _softmax_pv(scores, vals, None)
            acc = o if acc is None else acc + o
        o_ref[:, j * LANES:(j + 1) * LANES] = acc.astype(BF16)


def _mla_prompt_kernel(q_ref, k_ref, v_ref, o_ref):
    _mla_heads(q_ref[...], [k_ref[...]], [v_ref[...]], o_ref)


def _mla_prompt_call(qc, kc, vc):
    return pl.pallas_call(
        _mla_prompt_kernel,
        grid=(NB_P,),
        in_specs=[pl.BlockSpec((S_P, 768), lambda b: (b, 0)),
                  pl.BlockSpec((S_P, 768), lambda b: (b, 0)),
                  pl.BlockSpec((S_P, 384), lambda b: (b, 0))],
        out_specs=pl.BlockSpec((S_P, 384), lambda b: (b, 0)),
        out_shape=jax.ShapeDtypeStruct((T_P, 384), BF16),
        compiler_params=_cparams(("arbitrary",)),
        name="mla_prompt",
    )(qc, kc, vc)


def _mla_sample_kernel(q_ref, kc_ref, vc_ref, kl_ref, vl_ref, o_ref):
    _mla_heads(q_ref[...], [kc_ref[...], kl_ref[...]], [vc_ref[...], vl_ref[...]], o_ref)


def _mla_sample_call(qc, kc, vc):
    tq = 256
    nq = S_S // tq
    return pl.pallas_call(
        _mla_sample_kernel,
        grid=(NB_S, nq),
        in_specs=[pl.BlockSpec((tq, 768), lambda b, n: (T_P // tq + b * nq + n, 0)),
                  pl.BlockSpec((PAST, 768), lambda b, n: (T // PAST + b, 0)),
                  pl.BlockSpec((PAST, 384), lambda b, n: (T // PAST + b, 0)),
                  pl.BlockSpec((S_S, 768), lambda b, n: (T_P // S_S + b, 0)),
                  pl.BlockSpec((S_S, 384), lambda b, n: (T_P // S_S + b, 0))],
        out_specs=pl.BlockSpec((tq, 384), lambda b, n: (b * nq + n, 0)),
        out_shape=jax.ShapeDtypeStruct((T_S, 384), BF16),
        compiler_params=_cparams(("arbitrary", "arbitrary")),
        name="mla_sample",
    )(qc, kc, vc, kc, vc)


def _split3(x):
    x1 = x.astype(BF16)
    r = x - x1.astype(F32)
    x2 = r.astype(BF16)
    x3 = (r - x2.astype(F32)).astype(BF16)
    return x1, x2, x3


def _exact_dot(a01, x):
    x1, x2, x3 = _split3(x)
    return _dot(a01, x1) + _dot(a01, x2) + _dot(a01, x3)


def _col(x, k):
    return x[:, k:k + 1]


def _mlstm_kernel(nc, has_state, *refs):
    if has_state:
        (q_ref, k_ref, v_ref, bo_ref, li_ref, lf_ref, hn_ref, c0_ref, n0_ref, m0_ref,
         ob_ref, h_scr) = refs
    else:
        (q_ref, k_ref, v_ref, bo_ref, li_ref, lf_ref, hn_ref,
         ob_ref, cn_ref, nn_ref, mn_ref, h_scr) = refs
    s_len = nc * CHUNK
    li = li_ref[...]
    lf = lf_ref[...]
    row = lax.broadcasted_iota(jnp.int32, (CHUNK, CHUNK), 0)
    colv = lax.broadcasted_iota(jnp.int32, (CHUNK, CHUNK), 1)
    tri_f = (colv <= row)
    tri_b = (colv >= row)
    tri_f01 = tri_f.astype(BF16)
    tri_b01 = tri_b.astype(BF16)
    lane1 = _lane((1, LANES))
    fwd_lane = lane1 < 2

    bcs, bls, ggs, mgs = [], [], [], []
    for c in range(nc):
        lc = lf[c * CHUNK:(c + 1) * CHUNK]
        bc = jnp.where(fwd_lane, _exact_dot(tri_f01, lc), _exact_dot(tri_b01, lc))
        bl = jnp.where(fwd_lane, bc[CHUNK - 1:CHUNK], bc[0:1])
        gg = bl - bc + li[c * CHUNK:(c + 1) * CHUNK]
        bcs.append(bc)
        bls.append(bl)
        ggs.append(gg)
        mgs.append(gg.max(axis=0, keepdims=True))

    m0 = m0_ref[0] if has_state else jnp.zeros((1, LANES), F32)
    m_prev = [None] * nc
    m_next = [None] * nc
    mf = m0
    mb = m0
    mf_prev, mf_next, mb_prev, mb_next = {}, {}, {}, {}
    for i in range(nc):
        mf_prev[i] = mf
        mf = jnp.maximum(bls[i] + mf, mgs[i])
        mf_next[i] = mf
        cb = nc - 1 - i
        mb_prev[cb] = mb
        mb = jnp.maximum(bls[cb] + mb, mgs[cb])
        mb_next[cb] = mb
    for c in range(nc):
        m_prev[c] = jnp.where(fwd_lane, mf_prev[c], mb_prev[c])
        m_next[c] = jnp.where(fwd_lane, mf_next[c], mb_next[c])

    bc_all = jnp.concatenate(bcs, axis=0)
    ut = (li - bc_all).T

    lane2 = _lane((CHUNK, LANES))
    lo2 = lane2 < HD
    rr = lax.broadcasted_iota(jnp.int32, (LANES, LANES), 0)
    cc = lax.broadcasted_iota(jnp.int32, (LANES, LANES), 1)
    blockdiag = (rr < HD) == (cc < HD)

    for d in range(2):
        if has_state:
            cst = c0_ref[0, d]
            nst = n0_ref[0, d]
        else:
            cst = jnp.zeros((LANES, LANES), F32)
            nst = jnp.zeros((1, LANES), F32)
        order = range(nc) if d == 0 else range(nc - 1, -1, -1)
        causal = tri_f if d == 0 else tri_b
        for c in order:
            rows = slice(c * CHUNK, (c + 1) * CHUNK)
            q2 = q_ref[rows, :]
            k2 = k_ref[rows, :]
            v2 = v_ref[rows, :]
            qf = q2.astype(F32)
            qc_all = _dot(q2, cst.astype(BF16))
            qn_all = qf * nst
            num = None
            a_cols, dn_cols, ws_cols, dec = [], [], [], []
            for e in range(2):
                kk = 2 * d + e
                bc_col = _col(bcs[c], kk)
                d_mat = jnp.where(causal, bc_col + ut[kk:kk + 1, rows], NEG)
                mp = _col(m_prev[c], kk)
                mn = _col(m_next[c], kk)
                inter = bc_col + mp
                mt = jnp.maximum(inter, d_mat.max(axis=-1, keepdims=True))
                a = jnp.exp(inter - mt)
                qe = _half(q2, e == 1)
                w = jnp.exp(d_mat - mt) * _dot_nt(qe, k2)
                o = _dot(w.astype(BF16), _half(v2, e == 1))
                num = o if num is None else num + o
                qn = jnp.sum(jnp.where(lo2 != (e == 1), qn_all, 0.0), axis=-1, keepdims=True)
                den = a * qn + w.sum(axis=-1, keepdims=True)
                a_cols.append(a)
                dn_cols.append(jnp.maximum(jnp.abs(den), jnp.exp(-mt)))
                ws_cols.append(jnp.exp(_col(ggs[c], kk) - mn))
                dec.append(jnp.exp(_col(bls[c], kk) + mp - mn))
            a2 = jnp.where(lo2, a_cols[0], a_cols[1])
            dn2 = jnp.where(lo2, dn_cols[0], dn_cols[1])
            h2 = (a2 * qc_all + num) / dn2
            if d == 0:
                h_scr[rows, :] = h2
            else:
                h_scr[rows, :] = h_scr[rows, :] + h2
            kw = k2.astype(F32) * jnp.where(lo2, ws_cols[0], ws_cols[1])
            dec2 = jnp.where(lane1 < HD, dec[0], dec[1])
            u = _dot_tn(kw.astype(BF16), v2)
            cst = dec2 * cst + jnp.where(blockdiag, u, 0.0)
            nst = dec2 * nst + kw.sum(axis=0, keepdims=True)
        if not has_state:
            cn_ref[0, d] = cst
            nn_ref[0, d] = nst
    if not has_state:
        mn_ref[0] = jnp.where(fwd_lane, mf, mb)

    hs = h_scr[...]
    lo = _lane(hs.shape) < HD
    s = hs * hs
    s_lo = jnp.sum(jnp.where(lo, s, 0.0), axis=-1, keepdims=True)
    s_hi = jnp.sum(jnp.where(lo, 0.0, s), axis=-1, keepdims=True)
    r = lax.rsqrt(jnp.where(lo, s_lo, s_hi) * (1.0 / HD) + EPS)
    ob_ref[...] = (hs * r * hn_ref[...] * jax.nn.sigmoid(bo_ref[...])).astype(BF16)


def _mlstm_call(bq, bk, bv, bo, li, lf, hn2, state, l):
    has_state = state is not None
    if has_state:
        nb, s_len, base = NB_S, S_S, T_P // S_S
    else:
        nb, s_len, base = NB_P, S_P, 0
    nc = s_len // CHUNK
    tokp = pl.BlockSpec((s_len, LANES), lambda b, j: (base + b, j))
    in_specs = [tokp, tokp, tokp, tokp, tokp, tokp, pl.BlockSpec((1, LANES), lambda b, j: (0, 0))]
    args = [bq, bk, bv, bo, li, lf, hn2]
    ob_spec = pl.BlockSpec((s_len, LANES), lambda b, j: (b, j))
    ob_shape = jax.ShapeDtypeStruct((nb * s_len, 256), BF16)
    if has_state:
        c0, n0, m0 = state
        in_specs += [pl.BlockSpec((1, 2, LANES, LANES), lambda b, j: (b * 2 + j, 0, 0, 0)),
                     pl.BlockSpec((1, 2, 1, LANES), lambda b, j: (b * 2 + j, 0, 0, 0)),
                     pl.BlockSpec((1, 1, LANES), lambda b, j: (b * 2 + j, 0, 0))]
        args += [c0, n0, m0]
        out_specs = ob_spec
        out_shape = ob_shape
    else:
        out_specs = [ob_spec,
                     pl.BlockSpec((1, 2, LANES, LANES), lambda b, j: (b * 2 + j, 0, 0, 0)),
                     pl.BlockSpec((1, 2, 1, LANES), lambda b, j: (b * 2 + j, 0, 0, 0)),
                     pl.BlockSpec((1, 1, LANES), lambda b, j: (b * 2 + j, 0, 0))]
        out_shape = [ob_shape,
                     jax.ShapeDtypeStruct((nb * 2, 2, LANES, LANES), F32),
                     jax.ShapeDtypeStruct((nb * 2, 2, 1, LANES), F32),
                     jax.ShapeDtypeStruct((nb * 2, 1, LANES), F32)]
    return pl.pallas_call(
        functools.partial(_mlstm_kernel, nc, has_state),
        grid=(nb, 2),
        in_specs=in_specs,
        out_specs=out_specs,
        out_shape=out_shape,
        scratch_shapes=[pltpu.VMEM((s_len, LANES), F32)],
        compiler_params=_cparams(("arbitrary", "arbitrary")),
        name=f"mlstm_{'sample' if has_state else 'prompt'}_l{l}",
    )(*args)


def _out_kernel(moe, x_ref, oa_ref, ob_ref, oc_ref, w_ref, g1_ref, sh_ref, sc_ref, n2_ref, *rest):
    if moe:
        rh_ref, rl_ref, x1_o, xn_o, gate_o = rest
    else:
        x1_o, xn_o = rest
    o = (_dot(oa_ref[...], w_ref[0, 0:384, :]) + _dot(ob_ref[...], w_ref[0, 384:640, :])
         + _dot(oc_ref[...], w_ref[0, 640:1024, :]))
    x1 = x_ref[...] + g1_ref[0] * o
    x1_o[...] = x1
    xn = x1 * lax.rsqrt(jnp.mean(x1 * x1, axis=-1, keepdims=True) + EPS) * n2_ref[...]
    xn = xn * (1.0 + sc_ref[0]) + sh_ref[0]
    xb = xn.astype(BF16)
    xn_o[...] = xb
    if moe:
        xl = (xn - xb.astype(F32)).astype(BF16)
        logits = _dot(xb, rh_ref[0]) + (_dot(xl, rh_ref[0]) + _dot(xb, rl_ref[0]))
        lane = _lane(logits.shape)
        logits = jnp.where(lane < N_EXP, logits, -jnp.inf)
        m1 = logits.max(axis=-1, keepdims=True)
        i1 = jnp.min(jnp.where(logits == m1, lane, LANES), axis=-1, keepdims=True)
        rest_l = jnp.where(lane == i1, -jnp.inf, logits)
        m2 = rest_l.max(axis=-1, keepdims=True)
        i2 = jnp.min(jnp.where(rest_l == m2, lane, LANES), axis=-1, keepdims=True)
        e2 = jnp.exp(m2 - m1)
        den = 1.0 + e2
        gate_o[...] = jnp.where(lane == i1, 1.0 / den, jnp.where(lane == i2, e2 / den, 0.0))


def _out_call(x, oa, ob, oc, w_out_b, mod3, l, n2g, router):
    moe = router is not None
    row = lambda k: pl.BlockSpec((1, 1, D), lambda i: ((l * 8 + _mod_row(i)) * 6 + k, 0, 0))
    tok = lambda n: pl.BlockSpec((TM, n), lambda i: (i, 0))
    in_specs = [tok(D), tok(384), tok(256), tok(384),
                pl.BlockSpec((1, D, D), lambda i: (l, 0, 0)),
                row(2), row(3), row(4), pl.BlockSpec((1, D), lambda i: (0, 0))]
    args = [x, oa, ob, oc, w_out_b, mod3, mod3, mod3, n2g]
    out_specs = [tok(D), tok(D)]
    out_shape = [jax.ShapeDtypeStruct((T, D), F32), jax.ShapeDtypeStruct((T, D), BF16)]
    if moe:
        rh, rl = router
        in_specs += [pl.BlockSpec((1, D, LANES), lambda i: (0, 0, 0))] * 2
        args += [rh, rl]
        out_specs.append(tok(LANES))
        out_shape.append(jax.ShapeDtypeStruct((T, LANES), F32))
    return pl.pallas_call(
        functools.partial(_out_kernel, moe),
        grid=(NT,),
        in_specs=in_specs, out_specs=out_specs, out_shape=out_shape,
        compiler_params=_cparams(("arbitrary",)),
        name=f"out_proj_l{l}",
    )(*args)


FFN_TM = 1024


def _g2_spec(l):
    def idx(i, *_):
        r = jnp.where(i < T_P // FFN_TM, 0, 1 + (i - T_P // FFN_TM) // (S_S // FFN_TM))
        return ((l * 8 + r) * 6 + 5, 0, 0)
    return pl.BlockSpec((1, 1, D), idx)


def _swiglu_part(x, w1_ref, w3_ref, w2_ref):
    a = _dot(x, w1_ref[...].astype(BF16))
    b = _dot(x, w3_ref[...].astype(BF16))
    h = (a * jax.nn.sigmoid(a) * b).astype(BF16)
    return _dot(h, w2_ref[...].astype(BF16))


def _ffn_kernel(xn_ref, x1_ref, g2_ref, w1_ref, w3_ref, w2_ref, y_ref, acc):
    f = pl.program_id(1)
    part = _swiglu_part(xn_ref[...], w1_ref.at[0], w3_ref.at[0], w2_ref.at[0])

    @pl.when(f == 0)
    def _():
        acc[...] = part

    @pl.when(f > 0)
    def _():
        acc[...] += part

    @pl.when(f == pl.num_programs(1) - 1)
    def _():
        y_ref[...] = x1_ref[...] + g2_ref[0] * acc[...]


def _ffn_call(xn, x1, mod3, l, w1, w3, w2, i_layer):
    tf = 256
    return pl.pallas_call(
        _ffn_kernel,
        grid=(T // FFN_TM, D_FF // tf),
        in_specs=[pl.BlockSpec((FFN_TM, D), lambda i, f: (i, 0)),
                  pl.BlockSpec((FFN_TM, D), lambda i, f: (i, 0)),
                  _g2_spec(l),
                  pl.BlockSpec((1, D, tf), lambda i, f: (i_layer, 0, f)),
                  pl.BlockSpec((1, D, tf), lambda i, f: (i_layer, 0, f)),
                  pl.BlockSpec((1, tf, D), lambda i, f: (i_layer, f, 0))],
        out_specs=pl.BlockSpec((FFN_TM, D), lambda i, f: (i, 0)),
        out_shape=jax.ShapeDtypeStruct((T, D), F32),
        scratch_shapes=[pltpu.VMEM((FFN_TM, D), F32)],
        compiler_params=_cparams(("arbitrary", "arbitrary")),
        name="ffn_dense",
    )(xn, x1, mod3, w1, w3, w2)


def _moe_kernel(xn_ref, x1_ref, g2_ref, gate_ref, w1_ref, w3_ref, w2_ref, y_ref, acc_e, acc):
    e = pl.program_id(1)
    f = pl.program_id(2)
    nf = pl.num_programs(2)
    part = _swiglu_part(xn_ref[...], w1_ref.at[0, 0], w3_ref.at[0, 0], w2_ref.at[0, 0])

    @pl.when(f == 0)
    def _():
        acc_e[...] = part

    @pl.when(f > 0)
    def _():
        acc_e[...] += part

    @pl.when(f == nf - 1)
    def _():
        gates = gate_ref[...]
        gcol = jnp.sum(jnp.where(_lane(gates.shape) == e, gates, 0.0), axis=-1, keepdims=True)
        contrib = gcol * acc_e[...]

        @pl.when(e == 0)
        def _():
            acc[...] = contrib

        @pl.when(e > 0)
        def _():
            acc[...] += contrib

    @pl.when((f == nf - 1) & (e == N_EXP - 1))
    def _():
        y_ref[...] = x1_ref[...] + g2_ref[0] * acc[...]


def _moe_call(xn, x1, mod3, l, gates, w1, w3, w2, i_layer):
    tf = 512
    tm = FFN_TM
    return pl.pallas_call(
        _moe_kernel,
        grid=(T // tm, N_EXP, D_FFE // tf),
        in_specs=[pl.BlockSpec((tm, D), lambda i, e, f: (i, 0)),
                  pl.BlockSpec((tm, D), lambda i, e, f: (i, 0)),
                  _g2_spec(l),
                  pl.BlockSpec((tm, LANES), lambda i, e, f: (i, 0)),
                  pl.BlockSpec((1, 1, D, tf), lambda i, e, f: (i_layer, e, 0, f)),
                  pl.BlockSpec((1, 1, D, tf), lambda i, e, f: (i_layer, e, 0, f)),
                  pl.BlockSpec((1, 1, tf, D), lambda i, e, f: (i_layer, e, f, 0))],
        out_specs=pl.BlockSpec((tm, D), lambda i, e, f: (i, 0)),
        out_shape=jax.ShapeDtypeStruct((T, D), F32),
        scratch_shapes=[pltpu.VMEM((tm, D), F32), pltpu.VMEM((tm, D), F32)],
        compiler_params=_cparams(("arbitrary", "arbitrary", "arbitrary")),
        name="moe_dense",
    )(xn, x1, mod3, gates, w1, w3, w2)


def _pad_cols(w, n):
    return jnp.pad(w, ((0, 0), (0, n - w.shape[1])))


def _relayout_w_in(w):
    aq, ak, av, bq, bk, bv, bo, bg, cq, ckv, ckr = jnp.split(
        w, (384, 512, 640, 896, 1152, 1408, 1664, 1680, 1936, 2064), axis=1)
    dup = lambda m: jnp.concatenate([m[:, 0:HD], m[:, 0:HD], m[:, HD:2 * HD], m[:, HD:2 * HD]], axis=1)
    g = bg.reshape(D, 2, 2, 2, 2)

    def gate_cols(gi):
        cols = []
        for j in range(2):
            cols.append(_pad_cols(g[:, :, gi, j, :].reshape(D, 4), LANES))
        return jnp.concatenate(cols, axis=1)

    ckr_p = jnp.pad(ckr, ((0, 0), (C_NOPE, LANES - C_NOPE - C_ROPE)))
    out = jnp.concatenate([aq, dup(ak), dup(av), bq, bk, bv, bo, gate_cols(0), gate_cols(1),
                           cq, ckv, ckr_p], axis=1)
    return out.astype(BF16)


def _relayout_gate_b(b):
    g = b.reshape(2, 2, 2, 2)

    def cols(gi):
        return jnp.concatenate([jnp.pad(g[:, gi, j, :].reshape(4), (0, LANES - 4)) for j in range(2)])

    return jnp.concatenate([cols(0), cols(1)])[None, :]


def _pad_heads(w, width):
    r = w.shape[0]
    h = w.shape[1] // width
    return jnp.pad(w.reshape(r, h, width), ((0, 0), (0, 0), (0, LANES - width))).reshape(r, h * LANES)


def _rope_tables(half, span_start, period):
    rows = S_S // 64
    r = jnp.repeat(jnp.arange(rows), 64).astype(F32)
    c = jnp.tile(jnp.arange(64), rows).astype(F32)
    n_freq = half // 2
    freq = 10000.0 ** (-jnp.arange(n_freq, dtype=F32) / n_freq)
    ang = jnp.concatenate([r[:, None] * freq, c[:, None] * freq], axis=-1)
    cos, sin = jnp.cos(ang), jnp.sin(ang)
    d = (jnp.arange(LANES) - span_start) % period
    inside = d < 2 * half
    p = jnp.where(inside, d % half, 0)
    first = inside & (d < half)
    second = inside & (d >= half)
    ct = jnp.where(inside[None, :], cos[:, p], 1.0)
    sa = jnp.where(first[None, :], -sin[:, p], 0.0)
    sb = jnp.where(second[None, :], sin[:, p], 0.0)
    ident = (jnp.ones((TM, LANES), F32), jnp.zeros((TM, LANES), F32), jnp.zeros((TM, LANES), F32))
    return tuple(jnp.concatenate([i0, t], axis=0) for i0, t in zip(ident, (ct, sa, sb)))


def kernel(x_prompt, x_sample, c, cache_swa_k, cache_swa_v, cache_mla_ckv, cache_mla_krope, state_mlstm_C, state_mlstm_n, state_mlstm_m, c_ctx, ada_w, ada_b, norm1_g, norm2_g, w_in, a_qn_g, a_kn_g, a_sink, b_gate_b, b_hn_g, c_qa_g, c_kva_g, c_wuq, c_wukv, c_qn_g, c_kn_g, w_out, ffn_w1, ffn_w3, ffn_w2, moe_router, moe_w1, moe_w3, moe_w2):
    x = jnp.concatenate([x_prompt.reshape(T_P, D), x_sample.reshape(T_S, D)], axis=0)
    cv = jnp.concatenate([c_ctx[None, :], c, jnp.zeros((5, D), F32)], axis=0)
    mod = _ada_call(cv, ada_w, ada_b)
    mod3 = mod.reshape(DEPTH * 8 * 6, 1, D)

    rope_a = _rope_tables(32, 0, HD)
    rope_c = _rope_tables(16, C_NOPE, LANES)

    w_in_p = jnp.stack([_relayout_w_in(w_in[l]) for l in range(DEPTH)])
    wuq_p = jnp.stack([_pad_heads(c_wuq[l], C_QK) for l in range(DEPTH)]).astype(BF16)
    wukv = c_wukv.reshape(DEPTH, C_KVRANK, C_HEADS, C_NOPE + C_V)
    wk_p = jnp.pad(wukv[..., :C_NOPE], ((0, 0), (0, 0), (0, 0), (0, LANES - C_NOPE)))
    wk_p = wk_p.reshape(DEPTH, C_KVRANK, C_HEADS * LANES).astype(BF16)
    wv_p = wukv[..., C_NOPE:].reshape(DEPTH, C_KVRANK, C_HEADS * C_V).astype(BF16)
    w_out_b = w_out.astype(BF16)
    cache_k4 = cache_swa_k.reshape(NB_S, DEPTH, PAST, A_KV * HD)
    cache_v4 = cache_swa_v.reshape(NB_S, DEPTH, PAST, A_KV * HD)

    news = []
    for l in range(DEPTH):
        gqa = jnp.tile(a_qn_g[l], A_HEADS)[None, :]
        gka = jnp.tile(a_kn_g[l], 2 * A_KV)[None, :]
        gb = _relayout_gate_b(b_gate_b[l])
        gqn = jnp.tile(jnp.pad(c_qn_g[l], (0, LANES - C_QK)), C_HEADS)[None, :]
        gkn = jnp.tile(jnp.pad(c_kn_g[l], (0, LANES - C_QK)), C_HEADS)[None, :]
        (qa, ka, va, kaf, vaf, bq, bk, bv, bo, li, lf, qc, ckv_n, ckr) = _in_call(
            x, mod3, l, norm1_g[l][None, :], w_in_p, gqa, gka, gb, c_qa_g[l][None, :],
            c_kva_g[l][None, :], wuq_p, gqn, rope_a, rope_c)

        ckr_cache = jnp.pad(cache_mla_krope[:, l].reshape(NB_S * PAST, C_ROPE),
                            ((0, 0), (C_NOPE, LANES - C_NOPE - C_ROPE)))
        ckv_all = jnp.concatenate([ckv_n, cache_mla_ckv[:, l].reshape(NB_S * PAST, C_KVRANK)], axis=0)
        ckr_all = jnp.concatenate([ckr, ckr_cache], axis=0)
        kc, vc = _mlakv_call(ckv_all, ckr_all, l, wk_p, wv_p, gkn, rope_c)

        sink = a_sink[l]
        oa_p = _attn_a_prompt_call(sink, qa, ka, va)
        oa_s = _attn_a_sample_call(sink, qa, ka, va, cache_k4, cache_v4, l)
        oc_p = _mla_prompt_call(qc, kc, vc)
        oc_s = _mla_sample_call(qc, kc, vc)

        hn2 = jnp.tile(b_hn_g[l], 2)[None, :]
        ob_p, cn, nn, mn = _mlstm_call(bq, bk, bv, bo, li, lf, hn2, None, l)
        c_st = state_mlstm_C[:, l].reshape(NB_S, 2, 2, 2, HD, HD)
        c_st = jnp.transpose(c_st, (0, 2, 1, 3, 4, 5))
        c0 = jnp.zeros((NB_S, 2, 2, LANES, LANES), F32)
        c0 = c0.at[..., :HD, :HD].set(c_st[:, :, :, 0]).at[..., HD:, HD:].set(c_st[:, :, :, 1])
        c0 = c0.reshape(NB_S * 2, 2, LANES, LANES)
        n_st = state_mlstm_n[:, l].reshape(NB_S, 2, 2, 1, LANES)
        n0 = jnp.transpose(n_st, (0, 2, 1, 3, 4)).reshape(NB_S * 2, 2, 1, LANES)
        m_st = state_mlstm_m[:, l].reshape(NB_S, 2, 2, 2)
        m0 = jnp.transpose(m_st, (0, 2, 1, 3)).reshape(NB_S * 2, 1, 4)
        m0 = jnp.pad(m0, ((0, 0), (0, 0), (0, LANES - 4)))
        ob_s = _mlstm_call(bq, bk, bv, bo, li, lf, hn2, (c0, n0, m0), l)

        oa = jnp.concatenate([oa_p, oa_s], axis=0)
        ob = jnp.concatenate([ob_p, ob_s], axis=0)
        oc = jnp.concatenate([oc_p, oc_s], axis=0)

        if l % 2 == 0:
            x1, xn = _out_call(x, oa, ob, oc, w_out_b, mod3, l, norm2_g[l][None, :], None)
            x = _ffn_call(xn, x1, mod3, l, ffn_w1, ffn_w3, ffn_w2, l // 2)
        else:
            r = _pad_cols(moe_router[l // 2], LANES)
            rh = r.astype(BF16)
            rl = (r - rh.astype(F32)).astype(BF16)
            x1, xn, gates = _out_call(x, oa, ob, oc, w_out_b, mod3, l, norm2_g[l][None, :],
                                      (rh[None], rl[None]))
            x = _moe_call(xn, x1, mod3, l, gates, moe_w1, moe_w3, moe_w2, l // 2)

        new_k = kaf.reshape(NB_P, S_P, A_KV, HD)
        new_v = vaf.reshape(NB_P, S_P, A_KV, HD)
        new_ckv = ckv_n[:T_P].reshape(NB_P, S_P, C_KVRANK)
        new_kr = ckr[:T_P, C_NOPE:C_NOPE + C_ROPE].reshape(NB_P, S_P, C_ROPE)
        cn6 = cn.reshape(NB_P, 2, 2, LANES, LANES)
        c_e = jnp.stack([cn6[..., :HD, :HD], cn6[..., HD:, HD:]], axis=3)
        new_c = jnp.transpose(c_e, (0, 2, 1, 3, 4, 5)).reshape(NB_P, 2, B_HEADS, HD, HD)
        nn5 = nn.reshape(NB_P, 2, 2, 2, HD)
        new_n = jnp.transpose(nn5, (0, 2, 1, 3, 4)).reshape(NB_P, 2, B_HEADS, HD)
        mn4 = mn.reshape(NB_P, 2, LANES)[:, :, :4].reshape(NB_P, 2, 2, 2)
        new_m = jnp.transpose(mn4, (0, 2, 1, 3)).reshape(NB_P, 2, B_HEADS)
        news.append((new_k, new_v, new_ckv, new_kr, new_c, new_n, new_m))

    y_prompt = x[:T_P].reshape(NB_P, S_P, D)
    y_sample = x[T_P:].reshape(NB_S, S_S, D)
    stacked = tuple(jnp.stack([nw[j] for nw in news], axis=1) for j in range(7))
    return (y_prompt, y_sample) + stacked
```

```python
import functools

import jax
import jax.numpy as jnp
from jax import lax
from jax.experimental import pallas as pl
from jax.experimental.pallas import tpu as pltpu

F32 = jnp.float32
BF16 = jnp.bfloat16

D = 1024
NB_P, S_P = 16, 256
NB_S, S_S = 2, 1024
PAST = 512
DEPTH = 2
T_P = NB_P * S_P
T_S = NB_S * S_S
T = T_P + T_S
TM = 256
NT = T // TM
NT_P = T_P // TM
HD = 64
A_HEADS, A_KV = 6, 2
B_HEADS = 4
CHUNK = 64
C_HEADS = 6
C_QRANK, C_KVRANK, C_NOPE, C_ROPE, C_V = 256, 128, 64, 32, 64
C_QK = C_NOPE + C_ROPE
D_FF = 2816
N_EXP = 8
D_FFE = 3584
EPS = 1e-6
NEG = -1e30
LANES = 128
VMEM_LIMIT = 48 * 1024 * 1024

SEG = dict(QA=(0, 384), KA=(384, 256), VA=(640, 256), BQ=(896, 256), BK=(1152, 256),
           BV=(1408, 256), BO=(1664, 256), GI=(1920, 256), GF=(2176, 256), CQ=(2432, 256),
           CKV=(2688, 128), CKR=(2816, 128))
NP_IN = 2944


def _cparams(sem, vmem=VMEM_LIMIT):
    return pltpu.CompilerParams(dimension_semantics=sem, vmem_limit_bytes=vmem)


def _dot(a, b):
    return jnp.dot(a, b, preferred_element_type=F32)


def _dot_nt(a, b):
    return lax.dot_general(a, b, (((1,), (1,)), ((), ())), preferred_element_type=F32)


def _dot_tn(a, b):
    return lax.dot_general(a, b, (((0,), (0,)), ((), ())), preferred_element_type=F32)


def _lane(shape):
    return lax.broadcasted_iota(jnp.int32, shape, len(shape) - 1)


def _mod_row(i):
    return jnp.where(i < NT_P, 0, 1 + (i - NT_P) // (S_S // TM))


def _rope_blk(i):
    return jnp.where((i >= NT_P) & (i < NT), 1 + (i - NT_P) % (S_S // TM), 0)


def _ada_kernel(cv_ref, w_ref, b_ref, o_ref):
    s = cv_ref[...]
    s = s * jax.nn.sigmoid(s)
    o_ref[0] = _dot(s.astype(BF16), w_ref[0].astype(BF16)) + b_ref[0]


def _ada_call(cv, ada_w, ada_b):
    tn = 1536
    return pl.pallas_call(
        _ada_kernel,
        grid=(DEPTH, 6 * D // tn),
        in_specs=[pl.BlockSpec((8, D), lambda l, j: (0, 0)),
                  pl.BlockSpec((1, D, tn), lambda l, j: (l, 0, j)),
                  pl.BlockSpec((1, 1, tn), lambda l, j: (l, 0, j))],
        out_specs=pl.BlockSpec((1, 8, tn), lambda l, j: (l, 0, j)),
        out_shape=jax.ShapeDtypeStruct((DEPTH, 8, 6 * D), F32),
        compiler_params=_cparams(("arbitrary", "arbitrary")),
        name="ada_mod",
    )(cv, ada_w, ada_b.reshape(DEPTH, 1, 6 * D))


def _half_norm(x, g, n):
    outs = []
    for j in range(x.shape[1] // LANES):
        xj = x[:, j * LANES:(j + 1) * LANES]
        lo = _lane(xj.shape) < HD
        s = xj * xj
        s_lo = jnp.sum(jnp.where(lo, s, 0.0), axis=-1, keepdims=True)
        s_hi = jnp.sum(jnp.where(lo, 0.0, s), axis=-1, keepdims=True)
        r = lax.rsqrt(jnp.where(lo, s_lo, s_hi) * (1.0 / n) + EPS)
        outs.append(xj * r)
    return jnp.concatenate(outs, axis=-1) * g


def _group_norm(x, g, n):
    outs = []
    for j in range(x.shape[1] // LANES):
        xj = x[:, j * LANES:(j + 1) * LANES]
        r = lax.rsqrt(jnp.sum(xj * xj, axis=-1, keepdims=True) * (1.0 / n) + EPS)
        outs.append(xj * r)
    return jnp.concatenate(outs, axis=-1) * g


def _rope(x, c, sa, sb, shift):
    outs = []
    for j in range(x.shape[1] // LANES):
        xj = x[:, j * LANES:(j + 1) * LANES]
        outs.append(xj * c + pltpu.roll(xj, LANES - shift, 1) * sa + pltpu.roll(xj, shift, 1) * sb)
    return jnp.concatenate(outs, axis=-1)


def _in_kernel(x_ref, sh_ref, sc_ref, n1_ref, w_ref, gqa_ref, gka_ref, gb_ref, gcq_ref, gckv_ref,
               wuq_ref, gqn_ref, ra_c, ra_a, ra_b, rc_c, rc_a, rc_b,
               qa_o, ka_o, va_o, kaf_o, vaf_o, bq_o, bk_o, bv_o, bo_o, li_o, lf_o, qc_o, ckv_o, ckr_o):
    i = pl.program_id(0)
    x = x_ref[...]
    xn = x * lax.rsqrt(jnp.mean(x * x, axis=-1, keepdims=True) + EPS) * n1_ref[...]
    xn = xn * (1.0 + sc_ref[0]) + sh_ref[0]
    p = _dot(xn.astype(BF16), w_ref[0])

    def seg(name):
        o, n = SEG[name]
        return p[:, o:o + n]

    qa = _half_norm(seg("QA"), gqa_ref[...], HD)
    ka = _half_norm(seg("KA"), gka_ref[...], HD)
    va = seg("VA")

    @pl.when(i < NT_P)
    def _():
        kaf_o[...] = jnp.concatenate([ka[:, 0:HD], ka[:, LANES:LANES + HD]], axis=-1)
        vaf_o[...] = jnp.concatenate([va[:, 0:HD], va[:, LANES:LANES + HD]], axis=-1)

    qa_o[...] = _rope(qa, ra_c[...], ra_a[...], ra_b[...], 32).astype(BF16)
    ka_o[...] = _rope(ka, ra_c[...], ra_a[...], ra_b[...], 32).astype(BF16)
    va_o[...] = va.astype(BF16)

    bq_o[...] = seg("BQ").astype(BF16)
    bk_o[...] = (seg("BK") * (HD ** -0.5)).astype(BF16)
    bv_o[...] = seg("BV").astype(BF16)
    bo_o[...] = seg("BO")
    gb = gb_ref[...]
    li_o[...] = seg("GI") + gb[:, 0:256]
    lf_o[...] = jax.nn.log_sigmoid(seg("GF") + gb[:, 256:512])

    cq = seg("CQ")
    cqn = cq * lax.rsqrt(jnp.mean(cq * cq, axis=-1, keepdims=True) + EPS) * gcq_ref[...]
    qc = _group_norm(_dot(cqn.astype(BF16), wuq_ref[0]), gqn_ref[...], C_QK)
    qc_o[...] = _rope(qc, rc_c[...], rc_a[...], rc_b[...], 16).astype(BF16)

    ckv = seg("CKV")
    ckv_o[...] = ckv * lax.rsqrt(jnp.mean(ckv * ckv, axis=-1, keepdims=True) + EPS) * gckv_ref[...]
    ckr_o[...] = seg("CKR")


def _in_call(x, mod3, l, n1g, w_in_p, gqa, gka, gb, gcq, gckv, wuq_p, gqn, rope_a, rope_c):
    row = lambda k: pl.BlockSpec((1, 1, D), lambda i: ((l * 8 + _mod_row(i)) * 6 + k, 0, 0))
    vec = lambda n: pl.BlockSpec((1, n), lambda i: (0, 0))
    tab = pl.BlockSpec((TM, LANES), lambda i: (_rope_blk(i), 0))
    tok = lambda n: pl.BlockSpec((TM, n), lambda i: (i, 0))
    tokp = pl.BlockSpec((TM, LANES), lambda i: (jnp.minimum(i, NT_P - 1), 0))
    o = lambda n, dt: jax.ShapeDtypeStruct((T, n), dt)
    return pl.pallas_call(
        _in_kernel,
        grid=(NT,),
        in_specs=[tok(D), row(0), row(1), vec(D),
                  pl.BlockSpec((1, D, NP_IN), lambda i: (l, 0, 0)),
                  vec(384), vec(256), vec(512), vec(256), vec(128),
                  pl.BlockSpec((1, C_QRANK, 768), lambda i: (l, 0, 0)), vec(768),
                  tab, tab, tab, tab, tab, tab],
        out_specs=[tok(384), tok(256), tok(256), tokp, tokp, tok(256), tok(256), tok(256), tok(256),
                   tok(256), tok(256), tok(768), tok(128), tok(128)],
        out_shape=[o(384, BF16), o(256, BF16), o(256, BF16),
                   jax.ShapeDtypeStruct((T_P, LANES), F32), jax.ShapeDtypeStruct((T_P, LANES), F32),
                   o(256, BF16), o(256, BF16), o(256, BF16), o(256, F32),
                   o(256, F32), o(256, F32), o(768, BF16), o(128, F32), o(128, F32)],
        compiler_params=_cparams(("arbitrary",)),
        name=f"in_proj_l{l}",
    )(x, mod3, mod3, n1g, w_in_p, gqa, gka, gb, gcq, gckv, wuq_p, gqn, *rope_a, *rope_c)


def _mlakv_kernel(ckv_ref, ckr_ref, wk_ref, wv_ref, gkn_ref, rc_c, rc_a, rc_b, k_o, v_o):
    c = ckv_ref[...].astype(BF16)
    k = _dot(c, wk_ref[0])
    kr = ckr_ref[...]
    k = k + jnp.concatenate([kr] * C_HEADS, axis=-1)
    k = _group_norm(k, gkn_ref[...], C_QK)
    k_o[...] = _rope(k, rc_c[...], rc_a[...], rc_b[...], 16).astype(BF16)
    v_o[...] = _dot(c, wv_ref[0]).astype(BF16)


def _mlakv_call(ckv_all, ckr_all, l, wk_p, wv_p, gkn, rope_c):
    r = ckv_all.shape[0]
    tab = pl.BlockSpec((TM, LANES), lambda i: (_rope_blk(i), 0))
    tok = lambda n: pl.BlockSpec((TM, n), lambda i: (i, 0))
    return pl.pallas_call(
        _mlakv_kernel,
        grid=(r // TM,),
        in_specs=[tok(128), tok(128),
                  pl.BlockSpec((1, C_KVRANK, 768), lambda i: (l, 0, 0)),
                  pl.BlockSpec((1, C_KVRANK, 384), lambda i: (l, 0, 0)),
                  pl.BlockSpec((1, 768), lambda i: (0, 0)), tab, tab, tab],
        out_specs=[tok(768), tok(384)],
        out_shape=[jax.ShapeDtypeStruct((r, 768), BF16), jax.ShapeDtypeStruct((r, 384), BF16)],
        compiler_params=_cparams(("arbitrary",)),
        name=f"mla_kv_l{l}",
    )(ckv_all, ckr_all, wk_p, wv_p, gkn, *rope_c)


def _softmax_pv(scores, values, sink):
    m = scores[0].max(axis=-1, keepdims=True)
    for s in scores[1:]:
        m = jnp.maximum(m, s.max(axis=-1, keepdims=True))
    if sink is not None:
        m = jnp.maximum(m, sink)
    es = [jnp.exp(s - m) for s in scores]
    den = es[0].sum(axis=-1, keepdims=True)
    for e in es[1:]:
        den = den + e.sum(axis=-1, keepdims=True)
    if sink is not None:
        den = den + jnp.exp(sink - m)
    inv = 1.0 / den
    out = None
    for e, v in zip(es, values):
        o = _dot((e * inv).astype(BF16), v)
        out = o if out is None else out + o
    return out


def _half(x, hi):
    lo = _lane(x.shape) < HD
    return jnp.where(lo != hi, x, jnp.zeros_like(x))


def _attn_a_prompt_kernel(sink_ref, q_ref, k_ref, v_ref, o_ref):
    q = q_ref[...]
    k = k_ref[...]
    v = v_ref[...]
    for j in range(A_HEADS // 2):
        acc = None
        for c in range(2):
            h = 2 * j + c
            g = h // (A_HEADS // A_KV)
            qh = _half(q[:, j * LANES:(j + 1) * LANES], c == 1)
            s = _dot_nt(qh, k[:, g * LANES:(g + 1) * LANES]) * (HD ** -0.5)
            vh = _half(v[:, g * LANES:(g + 1) * LANES], c == 1)
            o = _softmax_pv([s], [vh], sink_ref[h])
            acc = o if acc is None else acc + o
        o_ref[:, j * LANES:(j + 1) * LANES] = acc.astype(BF16)


def _attn_a_prompt_call(sink, qa, ka, va):
    return pl.pallas_call(
        _attn_a_prompt_kernel,
        grid_spec=pltpu.PrefetchScalarGridSpec(
            num_scalar_prefetch=1, grid=(NB_P,),
            in_specs=[pl.BlockSpec((S_P, 384), lambda b, s: (b, 0)),
                      pl.BlockSpec((S_P, 256), lambda b, s: (b, 0)),
                      pl.BlockSpec((S_P, 256), lambda b, s: (b, 0))],
            out_specs=pl.BlockSpec((S_P, 384), lambda b, s: (b, 0))),
        out_shape=jax.ShapeDtypeStruct((T_P, 384), BF16),
        compiler_params=_cparams(("arbitrary",)),
        name="attn_a_prompt",
    )(sink, qa, ka, va)


QB = 128
WIN = 128
BAND = QB + 2 * WIN


def _attn_a_sample_kernel(sink_ref, q_ref, k_ref, v_ref, kc_ref, vc_ref, o_ref):
    n = pl.program_id(1)
    ws = pl.multiple_of(jnp.clip(n * QB - WIN, 0, S_S - BAND), QB)
    q = q_ref[...]
    kb = k_ref[pl.ds(ws, BAND), :]
    vb = v_ref[pl.ds(ws, BAND), :]
    kc = kc_ref[0, 0].astype(BF16)
    vc = vc_ref[0, 0].astype(BF16)
    qpos = n * QB + lax.broadcasted_iota(jnp.int32, (QB, BAND), 0)
    kpos = ws + lax.broadcasted_iota(jnp.int32, (QB, BAND), 1)
    ok = jnp.abs(qpos - kpos) <= WIN
    lane = _lane((PAST, LANES))
    for j in range(A_HEADS // 2):
        acc = None
        for c in range(2):
            h = 2 * j + c
            g = h // (A_HEADS // A_KV)
            qh = _half(q[:, j * LANES:(j + 1) * LANES], c == 1)
            s_b = jnp.where(ok, _dot_nt(qh, kb[:, g * LANES:(g + 1) * LANES]) * (HD ** -0.5), NEG)
            kcg = jnp.where((lane < HD) == (g == 0), kc, jnp.zeros_like(kc))
            vcg = jnp.where((lane < HD) == (g == 0), vc, jnp.zeros_like(vc))
            if (g == 1) != (c == 1):
                kcg = pltpu.roll(kcg.astype(F32), HD, 1).astype(BF16)
                vcg = pltpu.roll(vcg.astype(F32), HD, 1).astype(BF16)
            s_c = _dot_nt(qh, kcg) * (HD ** -0.5)
            vh = _half(vb[:, g * LANES:(g + 1) * LANES], c == 1)
            o = _softmax_pv([s_b, s_c], [vh, vcg], sink_ref[h])
            acc = o if acc is None else acc + o
        o_ref[:, j * LANES:(j + 1) * LANES] = acc.astype(BF16)


def _attn_a_sample_call(sink, qa, ka, va, cache_k, cache_v, l):
    nqb = S_S // QB
    off = T_P // S_S
    return pl.pallas_call(
        _attn_a_sample_kernel,
        grid_spec=pltpu.PrefetchScalarGridSpec(
            num_scalar_prefetch=1, grid=(NB_S, nqb),
            in_specs=[pl.BlockSpec((QB, 384), lambda b, n, s: (T_P // QB + b * nqb + n, 0)),
                      pl.BlockSpec((S_S, 256), lambda b, n, s: (off + b, 0)),
                      pl.BlockSpec((S_S, 256), lambda b, n, s: (off + b, 0)),
                      pl.BlockSpec((1, 1, PAST, LANES), lambda b, n, s: (b, l, 0, 0)),
                      pl.BlockSpec((1, 1, PAST, LANES), lambda b, n, s: (b, l, 0, 0))],
            out_specs=pl.BlockSpec((QB, 384), lambda b, n, s: (b * nqb + n, 0))),
        out_shape=jax.ShapeDtypeStruct((T_S, 384), BF16),
        compiler_params=_cparams(("arbitrary", "arbitrary")),
        name=f"attn_a_sample_l{l}",
    )(sink, qa, ka, va, cache_k, cache_v)


def _mla_heads(q, ks, vs, o_ref):
    for j in range(C_HEADS // 2):
        acc = None
        for c in range(2):
            h = 2 * j + c
            qh = q[:, h * LANES:(h + 1) * LANES]
            scores = [_dot_nt(qh, k[:, h * LANES:(h + 1) * LANES]) * (C_QK ** -0.5) for k in ks]
            vals = [_half(v[:, j * LANES:(j + 1) * LANES], c == 1) for v in vs]
            o = _softmax_pv(scores, vals, None)
            acc = o if acc is None else acc + o
        o_ref[:, j * LANES:(j + 1) * LANES] = acc.astype(BF16)


def _mla_prompt_kernel(q_ref, k_ref, v_ref, o_ref):
    _mla_heads(q_ref[...], [k_ref[...]], [v_ref[...]], o_ref)


def _mla_prompt_call(qc, kc, vc):
    return pl.pallas_call(
        _mla_prompt_kernel,
        grid=(NB_P,),
        in_specs=[pl.BlockSpec((S_P, 768), lambda b: (b, 0)),
                  pl.BlockSpec((S_P, 768), lambda b: (b, 0)),
                  pl.BlockSpec((S_P, 384), lambda b: (b, 0))],
        out_specs=pl.BlockSpec((S_P, 384), lambda b: (b, 0)),
        out_shape=jax.ShapeDtypeStruct((T_P, 384), BF16),
        compiler_params=_cparams(("arbitrary",)),
        name="mla_prompt",
    )(qc, kc, vc)


def _mla_sample_kernel(q_ref, kc_ref, vc_ref, kl_ref, vl_ref, o_ref):
    _mla_heads(q_ref[...], [kc_ref[...], kl_ref[...]], [vc_ref[...], vl_ref[...]], o_ref)


def _mla_sample_call(qc, kc, vc):
    tq = 256
    nq = S_S // tq
    return pl.pallas_call(
        _mla_sample_kernel,
        grid=(NB_S, nq),
        in_specs=[pl.BlockSpec((tq, 768), lambda b, n: (T_P // tq + b * nq + n, 0)),
                  pl.BlockSpec((PAST, 768), lambda b, n: (T // PAST + b, 0)),
                  pl.BlockSpec((PAST, 384), lambda b, n: (T // PAST + b, 0)),
                  pl.BlockSpec((S_S, 768), lambda b, n: (T_P // S_S + b, 0)),
                  pl.BlockSpec((S_S, 384), lambda b, n: (T_P // S_S + b, 0))],
        out_specs=pl.BlockSpec((tq, 384), lambda b, n: (b * nq + n, 0)),
        out_shape=jax.ShapeDtypeStruct((T_S, 384), BF16),
        compiler_params=_cparams(("arbitrary", "arbitrary")),
        name="mla_sample",
    )(qc, kc, vc, kc, vc)


def _split3(x):
    x1 = x.astype(BF16)
    r = x - x1.astype(F32)
    x2 = r.astype(BF16)
    x3 = (r - x2.astype(F32)).astype(BF16)
    return x1, x2, x3


def _exact_dot(a01, x):
    x1, x2, x3 = _split3(x)
    return _dot(a01, x1) + _dot(a01, x2) + _dot(a01, x3)


def _col(x, k):
    return x[:, k:k + 1]


def _mlstm_kernel(nc, has_state, *refs):
    if has_state:
        (q_ref, k_ref, v_ref, bo_ref, li_ref, lf_ref, hn_ref, c0_ref, n0_ref, m0_ref,
         ob_ref, h_scr) = refs
    else:
        (q_ref, k_ref, v_ref, bo_ref, li_ref, lf_ref, hn_ref,
         ob_ref, cn_ref, nn_ref, mn_ref, h_scr) = refs
    s_len = nc * CHUNK
    li = li_ref[...]
    lf = lf_ref[...]
    row = lax.broadcasted_iota(jnp.int32, (CHUNK, CHUNK), 0)
    colv = lax.broadcasted_iota(jnp.int32, (CHUNK, CHUNK), 1)
    tri_f = (colv <= row)
    tri_b = (colv >= row)
    tri_f01 = tri_f.astype(BF16)
    tri_b01 = tri_b.astype(BF16)
    lane1 = _lane((1, LANES))
    fwd_lane = lane1 < 2

    bcs, bls, ggs, mgs = [], [], [], []
    for c in range(nc):
        lc = lf[c * CHUNK:(c + 1) * CHUNK]
        bc = jnp.where(fwd_lane, _exact_dot(tri_f01, lc), _exact_dot(tri_b01, lc))
        bl = jnp.where(fwd_lane, bc[CHUNK - 1:CHUNK], bc[0:1])
        gg = bl - bc + li[c * CHUNK:(c + 1) * CHUNK]
        bcs.append(bc)
        bls.append(bl)
        ggs.append(gg)
        mgs.append(gg.max(axis=0, keepdims=True))

    m0 = m0_ref[0] if has_state else jnp.zeros((1, LANES), F32)
    m_prev = [None] * nc
    m_next = [None] * nc
    mf = m0
    mb = m0
    mf_prev, mf_next, mb_prev, mb_next = {}, {}, {}, {}
    for i in range(nc):
        mf_prev[i] = mf
        mf = jnp.maximum(bls[i] + mf, mgs[i])
        mf_next[i] = mf
        cb = nc - 1 - i
        mb_prev[cb] = mb
        mb = jnp.maximum(bls[cb] + mb, mgs[cb])
        mb_next[cb] = mb
    for c in range(nc):
        m_prev[c] = jnp.where(fwd_lane, mf_prev[c], mb_prev[c])
        m_next[c] = jnp.where(fwd_lane, mf_next[c], mb_next[c])

    bc_all = jnp.concatenate(bcs, axis=0)
    ut = (li - bc_all).T

    lane2 = _lane((CHUNK, LANES))
    lo2 = lane2 < HD
    rr = lax.broadcasted_iota(jnp.int32, (LANES, LANES), 0)
    cc = lax.broadcasted_iota(jnp.int32, (LANES, LANES), 1)
    blockdiag = (rr < HD) == (cc < HD)

    for d in range(2):
        if has_state:
            cst = c0_ref[0, d]
            nst = n0_ref[0, d]
        else:
            cst = jnp.zeros((LANES, LANES), F32)
            nst = jnp.zeros((1, LANES), F32)
        order = range(nc) if d == 0 else range(nc - 1, -1, -1)
        causal = tri_f if d == 0 else tri_b
        for c in order:
            rows = slice(c * CHUNK, (c + 1) * CHUNK)
            q2 = q_ref[rows, :]
            k2 = k_ref[rows, :]
            v2 = v_ref[rows, :]
            qf = q2.astype(F32)
            qc_all = _dot(q2, cst.astype(BF16))
            qn_all = qf * nst
            num = None
            a_cols, dn_cols, ws_cols, dec = [], [], [], []
            for e in range(2):
                kk = 2 * d + e
                bc_col = _col(bcs[c], kk)
                d_mat = jnp.where(causal, bc_col + ut[kk:kk + 1, rows], NEG)
                mp = _col(m_prev[c], kk)
                mn = _col(m_next[c], kk)
                inter = bc_col + mp
                mt = jnp.maximum(inter, d_mat.max(axis=-1, keepdims=True))
                a = jnp.exp(inter - mt)
                qe = _half(q2, e == 1)
                w = jnp.exp(d_mat - mt) * _dot_nt(qe, k2)
                o = _dot(w.astype(BF16), _half(v2, e == 1))
                num = o if num is None else num + o
                qn = jnp.sum(jnp.where(lo2 != (e == 1), qn_all, 0.0), axis=-1, keepdims=True)
                den = a * qn + w.sum(axis=-1, keepdims=True)
                a_cols.append(a)
                dn_cols.append(jnp.maximum(jnp.abs(den), jnp.exp(-mt)))
                ws_cols.append(jnp.exp(_col(ggs[c], kk) - mn))
                dec.append(jnp.exp(_col(bls[c], kk) + mp - mn))
            a2 = jnp.where(lo2, a_cols[0], a_cols[1])
            dn2 = jnp.where(lo2, dn_cols[0], dn_cols[1])
            h2 = (a2 * qc_all + num) / dn2
            if d == 0:
                h_scr[rows, :] = h2
            else:
                h_scr[rows, :] = h_scr[rows, :] + h2
            kw = k2.astype(F32) * jnp.where(lo2, ws_cols[0], ws_cols[1])
            dec2 = jnp.where(lane1 < HD, dec[0], dec[1])
            u = _dot_tn(kw.astype(BF16), v2)
            cst = dec2 * cst + jnp.where(blockdiag, u, 0.0)
            nst = dec2 * nst + kw.sum(axis=0, keepdims=True)
        if not has_state:
            cn_ref[0, d] = cst
            nn_ref[0, d] = nst
    if not has_state:
        mn_ref[0] = jnp.where(fwd_lane, mf, mb)

    hs = h_scr[...]
    lo = _lane(hs.shape) < HD
    s = hs * hs
    s_lo = jnp.sum(jnp.where(lo, s, 0.0), axis=-1, keepdims=True)
    s_hi = jnp.sum(jnp.where(lo, 0.0, s), axis=-1, keepdims=True)
    r = lax.rsqrt(jnp.where(lo, s_lo, s_hi) * (1.0 / HD) + EPS)
    ob_ref[...] = (hs * r * hn_ref[...] * jax.nn.sigmoid(bo_ref[...])).astype(BF16)


def _mlstm_call(bq, bk, bv, bo, li, lf, hn2, state, l):
    has_state = state is not None
    if has_state:
        nb, s_len, base = NB_S, S_S, T_P // S_S
    else:
        nb, s_len, base = NB_P, S_P, 0
    nc = s_len // CHUNK
    tokp = pl.BlockSpec((s_len, LANES), lambda b, j: (base + b, j))
    in_specs = [tokp, tokp, tokp, tokp, tokp, tokp, pl.BlockSpec((1, LANES), lambda b, j: (0, 0))]
    args = [bq, bk, bv, bo, li, lf, hn2]
    ob_spec = pl.BlockSpec((s_len, LANES), lambda b, j: (b, j))
    ob_shape = jax.ShapeDtypeStruct((nb * s_len, 256), BF16)
    if has_state:
        c0, n0, m0 = state
        in_specs += [pl.BlockSpec((1, 2, LANES, LANES), lambda b, j: (b * 2 + j, 0, 0, 0)),
                     pl.BlockSpec((1, 2, 1, LANES), lambda b, j: (b * 2 + j, 0, 0, 0)),
                     pl.BlockSpec((1, 1, LANES), lambda b, j: (b * 2 + j, 0, 0))]
        args += [c0, n0, m0]
        out_specs = ob_spec
        out_shape = ob_shape
    else:
        out_specs = [ob_spec,
                     pl.BlockSpec((1, 2, LANES, LANES), lambda b, j: (b * 2 + j, 0, 0, 0)),
                     pl.BlockSpec((1, 2, 1, LANES), lambda b, j: (b * 2 + j, 0, 0, 0)),
                     pl.BlockSpec((1, 1, LANES), lambda b, j: (b * 2 + j, 0, 0))]
        out_shape = [ob_shape,
                     jax.ShapeDtypeStruct((nb * 2, 2, LANES, LANES), F32),
                     jax.ShapeDtypeStruct((nb * 2, 2, 1, LANES), F32),
                     jax.ShapeDtypeStruct((nb * 2, 1, LANES), F32)]
    return pl.pallas_call(
        functools.partial(_mlstm_kernel, nc, has_state),
        grid=(nb, 2),
        in_specs=in_specs,
        out_specs=out_specs,
        out_shape=out_shape,
        scratch_shapes=[pltpu.VMEM((s_len, LANES), F32)],
        compiler_params=_cparams(("arbitrary", "arbitrary")),
        name=f"mlstm_{'sample' if has_state else 'prompt'}_l{l}",
    )(*args)


def _out_kernel(moe, x_ref, oa_ref, ob_ref, oc_ref, w_ref, g1_ref, sh_ref, sc_ref, n2_ref, *rest):
    if moe:
        rh_ref, rl_ref, x1_o, xn_o, gate_o = rest
    else:
        x1_o, xn_o = rest
    o = (_dot(oa_ref[...], w_ref[0, 0:384, :]) + _dot(ob_ref[...], w_ref[0, 384:640, :])
         + _dot(oc_ref[...], w_ref[0, 640:1024, :]))
    x1 = x_ref[...] + g1_ref[0] * o
    x1_o[...] = x1
    xn = x1 * lax.rsqrt(jnp.mean(x1 * x1, axis=-1, keepdims=True) + EPS) * n2_ref[...]
    xn = xn * (1.0 + sc_ref[0]) + sh_ref[0]
    xb = xn.astype(BF16)
    if not moe:
        xn_o[...] = xb
    else:
        xn_o[...] = xn
        xl = (xn - xb.astype(F32)).astype(BF16)
        logits = _dot(xb, rh_ref[0]) + (_dot(xl, rh_ref[0]) + _dot(xb, rl_ref[0]))
        lane = _lane(logits.shape)
        logits = jnp.where(lane < N_EXP, logits, -jnp.inf)
        m1 = logits.max(axis=-1, keepdims=True)
        i1 = jnp.min(jnp.where(logits == m1, lane, LANES), axis=-1, keepdims=True)
        rest_l = jnp.where(lane == i1, -jnp.inf, logits)
        m2 = rest_l.max(axis=-1, keepdims=True)
        i2 = jnp.min(jnp.where(rest_l == m2, lane, LANES), axis=-1, keepdims=True)
        e2 = jnp.exp(m2 - m1)
        den = 1.0 + e2
        gate_o[...] = jnp.where(lane == i1, 1.0 / den, jnp.where(lane == i2, e2 / den, 0.0))


def _out_call(x, oa, ob, oc, w_out_b, mod3, l, n2g, router):
    moe = router is not None
    row = lambda k: pl.BlockSpec((1, 1, D), lambda i: ((l * 8 + _mod_row(i)) * 6 + k, 0, 0))
    tok = lambda n: pl.BlockSpec((TM, n), lambda i: (i, 0))
    in_specs = [tok(D), tok(384), tok(256), tok(384),
                pl.BlockSpec((1, D, D), lambda i: (l, 0, 0)),
                row(2), row(3), row(4), pl.BlockSpec((1, D), lambda i: (0, 0))]
    args = [x, oa, ob, oc, w_out_b, mod3, mod3, mod3, n2g]
    out_specs = [tok(D), tok(D)]
    out_shape = [jax.ShapeDtypeStruct((T, D), F32), jax.ShapeDtypeStruct((T, D), BF16)]
    if moe:
        out_shape[1] = jax.ShapeDtypeStruct((T, D), F32)
        rh, rl = router
        in_specs += [pl.BlockSpec((1, D, LANES), lambda i: (0, 0, 0))] * 2
        args += [rh, rl]
        out_specs.append(tok(LANES))
        out_shape.append(jax.ShapeDtypeStruct((T, LANES), F32))
    return pl.pallas_call(
        functools.partial(_out_kernel, moe),
        grid=(NT,),
        in_specs=in_specs, out_specs=out_specs, out_shape=out_shape,
        compiler_params=_cparams(("arbitrary",)),
        name=f"out_proj_l{l}",
    )(*args)


FFN_TM = 1024


def _g2_spec(l):
    def idx(i, *_):
        r = jnp.where(i < T_P // FFN_TM, 0, 1 + (i - T_P // FFN_TM) // (S_S // FFN_TM))
        return ((l * 8 + r) * 6 + 5, 0, 0)
    return pl.BlockSpec((1, 1, D), idx)


def _swiglu_part(x, w1_ref, w3_ref, w2_ref):
    a = _dot(x, w1_ref[...].astype(BF16))
    b = _dot(x, w3_ref[...].astype(BF16))
    h = (a * jax.nn.sigmoid(a) * b).astype(BF16)
    return _dot(h, w2_ref[...].astype(BF16))


def _ffn_kernel(xn_ref, x1_ref, g2_ref, w1_ref, w3_ref, w2_ref, y_ref, acc):
    f = pl.program_id(1)
    part = _swiglu_part(xn_ref[...], w1_ref.at[0], w3_ref.at[0], w2_ref.at[0])

    @pl.when(f == 0)
    def _():
        acc[...] = part

    @pl.when(f > 0)
    def _():
        acc[...] += part

    @pl.when(f == pl.num_programs(1) - 1)
    def _():
        y_ref[...] = x1_ref[...] + g2_ref[0] * acc[...]


def _ffn_call(xn, x1, mod3, l, w1, w3, w2, i_layer):
    tf = 256
    return pl.pallas_call(
        _ffn_kernel,
        grid=(T // FFN_TM, D_FF // tf),
        in_specs=[pl.BlockSpec((FFN_TM, D), lambda i, f: (i, 0)),
                  pl.BlockSpec((FFN_TM, D), lambda i, f: (i, 0)),
                  _g2_spec(l),
                  pl.BlockSpec((1, D, tf), lambda i, f: (i_layer, 0, f)),
                  pl.BlockSpec((1, D, tf), lambda i, f: (i_layer, 0, f)),
                  pl.BlockSpec((1, tf, D), lambda i, f: (i_layer, f, 0))],
        out_specs=pl.BlockSpec((FFN_TM, D), lambda i, f: (i, 0)),
        out_shape=jax.ShapeDtypeStruct((T, D), F32),
        scratch_shapes=[pltpu.VMEM((FFN_TM, D), F32)],
        compiler_params=_cparams(("arbitrary", "arbitrary")),
        name="ffn_dense",
    )(xn, x1, mod3, w1, w3, w2)


MOE_BM = 1024
MOE_SUB = 256
MOE_NBLK = 2 * T // MOE_BM + N_EXP
MOE_NR = MOE_NBLK * MOE_BM
MOE_TF = 512
FIN_TM = 512


def _route(gates):
    g8 = gates[:, :N_EXP]
    sel = g8 > 0.0
    cum = jnp.cumsum(sel.astype(jnp.int32), axis=0)
    cnt = cum[-1]
    nb = (cnt + MOE_BM - 1) // MOE_BM
    blk_end = jnp.cumsum(nb)
    blk_off = blk_end - nb
    nvb = blk_end[-1]
    b = jnp.arange(MOE_NBLK, dtype=jnp.int32)
    e_b = jnp.minimum(jnp.sum((blk_end[None, :] <= b[:, None]).astype(jnp.int32), axis=1), N_EXP - 1)
    rows_b = jnp.where(b < nvb, jnp.clip(cnt[e_b] - (b - blk_off[e_b]) * MOE_BM, 0, MOE_BM), 0)
    cum_t = cum.T
    want = jnp.broadcast_to(jnp.arange(1, T + 1, dtype=jnp.int32)[None, :], (N_EXP, T))
    lo = jnp.zeros((N_EXP, T), jnp.int32)
    hi = jnp.full((N_EXP, T), T, jnp.int32)
    for _ in range(T.bit_length()):
        active = lo < hi
        mid = (lo + hi) >> 1
        right = jnp.take_along_axis(cum_t, jnp.minimum(mid, T - 1), axis=1) < want
        lo, hi = jnp.where(active & right, mid + 1, lo), jnp.where(active & ~right, mid, hi)
    tok_of = lo
    r = jnp.arange(MOE_NR, dtype=jnp.int32)
    rb = r // MOE_BM
    e_r = e_b[rb]
    k_r = (rb - blk_off[e_r]) * MOE_BM + r % MOE_BM
    valid = (r % MOE_BM) < rows_b[rb]
    src = jnp.where(valid, tok_of[e_r, jnp.minimum(k_r, T - 1)], 0).astype(jnp.int32)
    gate_r = jnp.where(valid, g8[src, e_r], 0.0)
    return src, gate_r, e_b, rows_b.astype(jnp.int32), nvb.astype(jnp.int32).reshape(1)


def _moe_gather_kernel(src_ref, rows_ref, x_ref, o_ref, rows_scr):
    k = pl.program_id(0)
    base = k * MOE_SUB
    used = rows_ref[k // (MOE_BM // MOE_SUB)] > (k % (MOE_BM // MOE_SUB)) * MOE_SUB

    @pl.when(used)
    def _():
        def body(i, carry):
            for j in range(8):
                r = i * 8 + j
                rows_scr[pl.ds(r, 1), :] = x_ref[pl.ds(src_ref[base + r], 1), :]
            return carry

        lax.fori_loop(0, MOE_SUB // 8, body, 0)
        o_ref[...] = rows_scr[...].astype(BF16)

    @pl.when(jnp.logical_not(used))
    def _():
        o_ref[...] = jnp.zeros(o_ref.shape, o_ref.dtype)


def _moe_gather_call(src, rows_b, xn):
    return pl.pallas_call(
        _moe_gather_kernel,
        grid_spec=pltpu.PrefetchScalarGridSpec(
            num_scalar_prefetch=2, grid=(MOE_NR // MOE_SUB,),
            in_specs=[pl.BlockSpec((T, D), lambda k, s, rw: (0, 0), pipeline_mode=pl.Buffered(1))],
            out_specs=pl.BlockSpec((MOE_SUB, D), lambda k, s, rw: (k, 0)),
            scratch_shapes=[pltpu.VMEM((MOE_SUB, D), F32)]),
        out_shape=jax.ShapeDtypeStruct((MOE_NR, D), BF16),
        compiler_params=_cparams(("arbitrary",)),
        name="moe_gather",
    )(src, rows_b, xn)


def _moe_kernel(eb_ref, rows_ref, nvb_ref, xs_ref, w1_ref, w3_ref, w2_ref, y_ref, w1b, w3b, w2b):
    b = pl.program_id(0)
    f = pl.program_id(1)
    nsub = jnp.where(b < nvb_ref[0], (rows_ref[b] + MOE_SUB - 1) // MOE_SUB, 0)

    @pl.when(nsub > 0)
    def _():
        w1b[...] = w1_ref[0, 0].astype(BF16)
        w3b[...] = w3_ref[0, 0].astype(BF16)
        w2b[...] = w2_ref[0, 0].astype(BF16)

    for k in range(1, MOE_BM // MOE_SUB + 1):
        m = k * MOE_SUB

        @pl.when(nsub == k)
        def _(m=m):
            x = xs_ref[0:m, :]
            a = _dot(x, w1b[...])
            g = _dot(x, w3b[...])
            h = (a * jax.nn.sigmoid(a) * g).astype(BF16)
            part = _dot(h, w2b[...])

            @pl.when(f == 0)
            def _():
                y_ref[0:m, :] = part
                if m < MOE_BM:
                    y_ref[m:MOE_BM, :] = jnp.zeros((MOE_BM - m, D), F32)

            @pl.when(f > 0)
            def _():
                y_ref[0:m, :] += part


def _moe_call(xs, e_b, rows_b, nvb, w1, w3, w2, i_layer):
    nf = D_FFE // MOE_TF

    def blk(b, nvb_ref):
        return jnp.minimum(b, nvb_ref[0] - 1)

    def ff(b, f, nvb_ref):
        return jnp.where(b < nvb_ref[0], f, nf - 1)

    return pl.pallas_call(
        _moe_kernel,
        grid_spec=pltpu.PrefetchScalarGridSpec(
            num_scalar_prefetch=3, grid=(MOE_NBLK, nf),
            in_specs=[pl.BlockSpec((MOE_BM, D), lambda b, f, eb, rw, nv: (blk(b, nv), 0)),
                      pl.BlockSpec((1, 1, D, MOE_TF),
                                   lambda b, f, eb, rw, nv: (i_layer, eb[blk(b, nv)], 0, ff(b, f, nv))),
                      pl.BlockSpec((1, 1, D, MOE_TF),
                                   lambda b, f, eb, rw, nv: (i_layer, eb[blk(b, nv)], 0, ff(b, f, nv))),
                      pl.BlockSpec((1, 1, MOE_TF, D),
                                   lambda b, f, eb, rw, nv: (i_layer, eb[blk(b, nv)], ff(b, f, nv), 0))],
            out_specs=pl.BlockSpec((MOE_BM, D), lambda b, f, eb, rw, nv: (blk(b, nv), 0)),
            scratch_shapes=[pltpu.VMEM((D, MOE_TF), BF16), pltpu.VMEM((D, MOE_TF), BF16),
                            pltpu.VMEM((MOE_TF, D), BF16)]),
        out_shape=jax.ShapeDtypeStruct((MOE_NR, D), F32),
        compiler_params=_cparams(("arbitrary", "arbitrary")),
        name="moe_experts",
    )(e_b, rows_b, nvb, xs, w1, w3, w2)


def _moe_combine_kernel(src_ref, gate_ref, rows_ref, nvb_ref, ys_ref, x1_ref, g2_ref, y_ref, acc):
    i = pl.program_id(0)

    @pl.when(i == 0)
    def _():
        acc[...] = jnp.zeros(acc.shape, F32)

    @pl.when(i < nvb_ref[0])
    def _():
        base = i * MOE_BM
        n = rows_ref[i]

        def add_rows(r0, cnt):
            toks = [src_ref[base + r0 + j] for j in range(cnt)]
            vals = [acc[pl.ds(toks[j], 1), :] + gate_ref[base + r0 + j] * ys_ref[pl.ds(r0 + j, 1), :]
                    for j in range(cnt)]
            for j in range(cnt):
                acc[pl.ds(toks[j], 1), :] = vals[j]

        def body4(q, carry):
            add_rows(q * 4, 4)
            return carry

        lax.fori_loop(0, n // 4, body4, 0)

        def body1(r, carry):
            add_rows(r, 1)
            return carry

        lax.fori_loop((n // 4) * 4, n, body1, 0)

    @pl.when(i >= MOE_NBLK)
    def _():
        t0 = pl.multiple_of((i - MOE_NBLK) * FIN_TM, FIN_TM)
        y_ref[...] = x1_ref[...] + g2_ref[0] * acc[pl.ds(t0, FIN_TM), :]


def _moe_combine_call(src, gate_r, rows_b, nvb, ys, x1, mod3, l):
    nfin = T // FIN_TM

    def g2_idx(i, *_):
        j = jnp.maximum(i - MOE_NBLK, 0)
        r = jnp.where(j < T_P // FIN_TM, 0, 1 + (j - T_P // FIN_TM) // (S_S // FIN_TM))
        return ((l * 8 + r) * 6 + 5, 0, 0)

    return pl.pallas_call(
        _moe_combine_kernel,
        grid_spec=pltpu.PrefetchScalarGridSpec(
            num_scalar_prefetch=4, grid=(MOE_NBLK + nfin,),
            in_specs=[pl.BlockSpec((MOE_BM, D),
                                   lambda i, s, g, rw, nv: (jnp.minimum(jnp.minimum(i, MOE_NBLK - 1), nv[0] - 1), 0)),
                      pl.BlockSpec((FIN_TM, D), lambda i, s, g, rw, nv: (jnp.maximum(i - MOE_NBLK, 0), 0)),
                      pl.BlockSpec((1, 1, D), g2_idx)],
            out_specs=pl.BlockSpec((FIN_TM, D), lambda i, s, g, rw, nv: (jnp.maximum(i - MOE_NBLK, 0), 0)),
            scratch_shapes=[pltpu.VMEM((T, D), F32)]),
        out_shape=jax.ShapeDtypeStruct((T, D), F32),
        compiler_params=_cparams(("arbitrary",), vmem=52 * 1024 * 1024),
        name="moe_combine",
    )(src, gate_r, rows_b, nvb, ys, x1, mod3)


def _moe_layer(xp, x1, gates, mod3, l, w1, w3, w2, i_layer):
    src, gate_r, e_b, rows_b, nvb = _route(gates)
    xs = _moe_gather_call(src, rows_b, xp)
    ys = _moe_call(xs, e_b, rows_b, nvb, w1, w3, w2, i_layer)
    return _moe_combine_call(src, gate_r, rows_b, nvb, ys, x1, mod3, l)


def _pad_cols(w, n):
    return jnp.pad(w, ((0, 0), (0, n - w.shape[1])))


def _relayout_w_in(w):
    aq, ak, av, bq, bk, bv, bo, bg, cq, ckv, ckr = jnp.split(
        w, (384, 512, 640, 896, 1152, 1408, 1664, 1680, 1936, 2064), axis=1)
    dup = lambda m: jnp.concatenate([m[:, 0:HD], m[:, 0:HD], m[:, HD:2 * HD], m[:, HD:2 * HD]], axis=1)
    g = bg.reshape(D, 2, 2, 2, 2)

    def gate_cols(gi):
        cols = []
        for j in range(2):
            cols.append(_pad_cols(g[:, :, gi, j, :].reshape(D, 4), LANES))
        return jnp.concatenate(cols, axis=1)

    ckr_p = jnp.pad(ckr, ((0, 0), (C_NOPE, LANES - C_NOPE - C_ROPE)))
    out = jnp.concatenate([aq, dup(ak), dup(av), bq, bk, bv, bo, gate_cols(0), gate_cols(1),
                           cq, ckv, ckr_p], axis=1)
    return out.astype(BF16)


def _relayout_gate_b(b):
    g = b.reshape(2, 2, 2, 2)

    def cols(gi):
        return jnp.concatenate([jnp.pad(g[:, gi, j, :].reshape(4), (0, LANES - 4)) for j in range(2)])

    return jnp.concatenate([cols(0), cols(1)])[None, :]


def _pad_heads(w, width):
    r = w.shape[0]
    h = w.shape[1] // width
    return jnp.pad(w.reshape(r, h, width), ((0, 0), (0, 0), (0, LANES - width))).reshape(r, h * LANES)


def _rope_tables(half, span_start, period):
    rows = S_S // 64
    r = jnp.repeat(jnp.arange(rows), 64).astype(F32)
    c = jnp.tile(jnp.arange(64), rows).astype(F32)
    n_freq = half // 2
    freq = 10000.0 ** (-jnp.arange(n_freq, dtype=F32) / n_freq)
    ang = jnp.concatenate([r[:, None] * freq, c[:, None] * freq], axis=-1)
    cos, sin = jnp.cos(ang), jnp.sin(ang)
    d = (jnp.arange(LANES) - span_start) % period
    inside = d < 2 * half
    p = jnp.where(inside, d % half, 0)
    first = inside & (d < half)
    second = inside & (d >= half)
    ct = jnp.where(inside[None, :], cos[:, p], 1.0)
    sa = jnp.where(first[None, :], -sin[:, p], 0.0)
    sb = jnp.where(second[None, :], sin[:, p], 0.0)
    ident = (jnp.ones((TM, LANES), F32), jnp.zeros((TM, LANES), F32), jnp.zeros((TM, LANES), F32))
    return tuple(jnp.concatenate([i0, t], axis=0) for i0, t in zip(ident, (ct, sa, sb)))


def kernel(x_prompt, x_sample, c, cache_swa_k, cache_swa_v, cache_mla_ckv, cache_mla_krope, state_mlstm_C, state_mlstm_n, state_mlstm_m, c_ctx, ada_w, ada_b, norm1_g, norm2_g, w_in, a_qn_g, a_kn_g, a_sink, b_gate_b, b_hn_g, c_qa_g, c_kva_g, c_wuq, c_wukv, c_qn_g, c_kn_g, w_out, ffn_w1, ffn_w3, ffn_w2, moe_router, moe_w1, moe_w3, moe_w2):
    x = jnp.concatenate([x_prompt.reshape(T_P, D), x_sample.reshape(T_S, D)], axis=0)
    cv = jnp.concatenate([c_ctx[None, :], c, jnp.zeros((5, D), F32)], axis=0)
    mod = _ada_call(cv, ada_w, ada_b)
    mod3 = mod.reshape(DEPTH * 8 * 6, 1, D)

    rope_a = _rope_tables(32, 0, HD)
    rope_c = _rope_tables(16, C_NOPE, LANES)

    w_in_p = jnp.stack([_relayout_w_in(w_in[l]) for l in range(DEPTH)])
    wuq_p = jnp.stack([_pad_heads(c_wuq[l], C_QK) for l in range(DEPTH)]).astype(BF16)
    wukv = c_wukv.reshape(DEPTH, C_KVRANK, C_HEADS, C_NOPE + C_V)
    wk_p = jnp.pad(wukv[..., :C_NOPE], ((0, 0), (0, 0), (0, 0), (0, LANES - C_NOPE)))
    wk_p = wk_p.reshape(DEPTH, C_KVRANK, C_HEADS * LANES).astype(BF16)
    wv_p = wukv[..., C_NOPE:].reshape(DEPTH, C_KVRANK, C_HEADS * C_V).astype(BF16)
    w_out_b = w_out.astype(BF16)
    cache_k4 = cache_swa_k.reshape(NB_S, DEPTH, PAST, A_KV * HD)
    cache_v4 = cache_swa_v.reshape(NB_S, DEPTH, PAST, A_KV * HD)

    news = []
    for l in range(DEPTH):
        gqa = jnp.tile(a_qn_g[l], A_HEADS)[None, :]
        gka = jnp.tile(a_kn_g[l], 2 * A_KV)[None, :]
        gb = _relayout_gate_b(b_gate_b[l])
        gqn = jnp.tile(jnp.pad(c_qn_g[l], (0, LANES - C_QK)), C_HEADS)[None, :]
        gkn = jnp.tile(jnp.pad(c_kn_g[l], (0, LANES - C_QK)), C_HEADS)[None, :]
        (qa, ka, va, kaf, vaf, bq, bk, bv, bo, li, lf, qc, ckv_n, ckr) = _in_call(
            x, mod3, l, norm1_g[l][None, :], w_in_p, gqa, gka, gb, c_qa_g[l][None, :],
            c_kva_g[l][None, :], wuq_p, gqn, rope_a, rope_c)

        ckr_cache = jnp.pad(cache_mla_krope[:, l].reshape(NB_S * PAST, C_ROPE),
                            ((0, 0), (C_NOPE, LANES - C_NOPE - C_ROPE)))
        ckv_all = jnp.concatenate([ckv_n, cache_mla_ckv[:, l].reshape(NB_S * PAST, C_KVRANK)], axis=0)
        ckr_all = jnp.concatenate([ckr, ckr_cache], axis=0)
        kc, vc = _mlakv_call(ckv_all, ckr_all, l, wk_p, wv_p, gkn, rope_c)

        sink = a_sink[l]
        oa_p = _attn_a_prompt_call(sink, qa, ka, va)
        oa_s = _attn_a_sample_call(sink, qa, ka, va, cache_k4, cache_v4, l)
        oc_p = _mla_prompt_call(qc, kc, vc)
        oc_s = _mla_sample_call(qc, kc, vc)

        hn2 = jnp.tile(b_hn_g[l], 2)[None, :]
        ob_p, cn, nn, mn = _mlstm_call(bq, bk, bv, bo, li, lf, hn2, None, l)
        c_st = state_mlstm_C[:, l].reshape(NB_S, 2, 2, 2, HD, HD)
        c_st = jnp.transpose(c_st, (0, 2, 1, 3, 4, 5))
        c0 = jnp.zeros((NB_S, 2, 2, LANES, LANES), F32)
        c0 = c0.at[..., :HD, :HD].set(c_st[:, :, :, 0]).at[..., HD:, HD:].set(c_st[:, :, :, 1])
        c0 = c0.reshape(NB_S * 2, 2, LANES, LANES)
        n_st = state_mlstm_n[:, l].reshape(NB_S, 2, 2, 1, LANES)
        n0 = jnp.transpose(n_st, (0, 2, 1, 3, 4)).reshape(NB_S * 2, 2, 1, LANES)
        m_st = state_mlstm_m[:, l].reshape(NB_S, 2, 2, 2)
        m0 = jnp.transpose(m_st, (0, 2, 1, 3)).reshape(NB_S * 2, 1, 4)
        m0 = jnp.pad(m0, ((0, 0), (0, 0), (0, LANES - 4)))
        ob_s = _mlstm_call(bq, bk, bv, bo, li, lf, hn2, (c0, n0, m0), l)

        oa = jnp.concatenate([oa_p, oa_s], axis=0)
        ob = jnp.concatenate([ob_p, ob_s], axis=0)
        oc = jnp.concatenate([oc_p, oc_s], axis=0)

        if l % 2 == 0:
            x1, xn = _out_call(x, oa, ob, oc, w_out_b, mod3, l, norm2_g[l][None, :], None)
            x = _ffn_call(xn, x1, mod3, l, ffn_w1, ffn_w3, ffn_w2, l // 2)
        else:
            r = _pad_cols(moe_router[l // 2], LANES)
            rh = r.astype(BF16)
            rl = (r - rh.astype(F32)).astype(BF16)
            x1, xn, gates = _out_call(x, oa, ob, oc, w_out_b, mod3, l, norm2_g[l][None, :],
                                      (rh[None], rl[None]))
            x = _moe_layer(xn, x1, gates, mod3, l, moe_w1, moe_w3, moe_w2, l // 2)

        new_k = kaf.reshape(NB_P, S_P, A_KV, HD)
        new_v = vaf.reshape(NB_P, S_P, A_KV, HD)
        new_ckv = ckv_n[:T_P].reshape(NB_P, S_P, C_KVRANK)
        new_kr = ckr[:T_P, C_NOPE:C_NOPE + C_ROPE].reshape(NB_P, S_P, C_ROPE)
        cn6 = cn.reshape(NB_P, 2, 2, LANES, LANES)
        c_e = jnp.stack([cn6[..., :HD, :HD], cn6[..., HD:, HD:]], axis=3)
        new_c = jnp.transpose(c_e, (0, 2, 1, 3, 4, 5)).reshape(NB_P, 2, B_HEADS, HD, HD)
        nn5 = nn.reshape(NB_P, 2, 2, 2, HD)
        new_n = jnp.transpose(nn5, (0, 2, 1, 3, 4)).reshape(NB_P, 2, B_HEADS, HD)
        mn4 = mn.reshape(NB_P, 2, LANES)[:, :, :4].reshape(NB_P, 2, 2, 2)
        new_m = jnp.transpose(mn4, (0, 2, 1, 3)).reshape(NB_P, 2, B_HEADS)
        news.append((new_k, new_v, new_ckv, new_kr, new_c, new_n, new_m))

    y_prompt = x[:T_P].reshape(NB_P, S_P, D)
    y_sample = x[T_P:].reshape(NB_S, S_S, D)
    stacked = tuple(jnp.stack([nw[j] for nw in news], axis=1) for j in range(7))
    return (y_prompt, y_sample) + stacked
```

```python
import functools

import jax
import jax.numpy as jnp
from jax import lax
from jax.experimental import pallas as pl
from jax.experimental.pallas import tpu as pltpu

F32 = jnp.float32
BF16 = jnp.bfloat16

D = 1024
NB_P, S_P = 16, 256
NB_S, S_S = 2, 1024
PAST = 512
DEPTH = 2
T_P = NB_P * S_P
T_S = NB_S * S_S
T = T_P + T_S
TM = 256
NT = T // TM
NT_P = T_P // TM
HD = 64
A_HEADS, A_KV = 6, 2
B_HEADS = 4
CHUNK = 64
C_HEADS = 6
C_QRANK, C_KVRANK, C_NOPE, C_ROPE, C_V = 256, 128, 64, 32, 64
C_QK = C_NOPE + C_ROPE
D_FF = 2816
N_EXP = 8
D_FFE = 3584
EPS = 1e-6
NEG = -1e30
LANES = 128
VMEM_LIMIT = 48 * 1024 * 1024

SEG = dict(QA=(0, 384), KA=(384, 256), VA=(640, 256), BQ=(896, 256), BK=(1152, 256),
           BV=(1408, 256), BO=(1664, 256), GI=(1920, 256), GF=(2176, 256), CQ=(2432, 256),
           CKV=(2688, 128), CKR=(2816, 128))
NP_IN = 2944


def _cparams(sem, vmem=VMEM_LIMIT):
    return pltpu.CompilerParams(dimension_semantics=sem, vmem_limit_bytes=vmem)


def _dot(a, b):
    return jnp.dot(a, b, preferred_element_type=F32)


def _dot_nt(a, b):
    return lax.dot_general(a, b, (((1,), (1,)), ((), ())), preferred_element_type=F32)


def _dot_tn(a, b):
    return lax.dot_general(a, b, (((0,), (0,)), ((), ())), preferred_element_type=F32)


def _lane(shape):
    return lax.broadcasted_iota(jnp.int32, shape, len(shape) - 1)


def _mod_row(i):
    return jnp.where(i < NT_P, 0, 1 + (i - NT_P) // (S_S // TM))


def _rope_blk(i):
    return jnp.where((i >= NT_P) & (i < NT), 1 + (i - NT_P) % (S_S // TM), 0)


def _ada_kernel(cv_ref, w_ref, b_ref, o_ref):
    s = cv_ref[...]
    s = s * jax.nn.sigmoid(s)
    o_ref[0] = _dot(s.astype(BF16), w_ref[0].astype(BF16)) + b_ref[0]


def _ada_call(cv, ada_w, ada_b):
    tn = 1536
    return pl.pallas_call(
        _ada_kernel,
        grid=(DEPTH, 6 * D // tn),
        in_specs=[pl.BlockSpec((8, D), lambda l, j: (0, 0)),
                  pl.BlockSpec((1, D, tn), lambda l, j: (l, 0, j)),
                  pl.BlockSpec((1, 1, tn), lambda l, j: (l, 0, j))],
        out_specs=pl.BlockSpec((1, 8, tn), lambda l, j: (l, 0, j)),
        out_shape=jax.ShapeDtypeStruct((DEPTH, 8, 6 * D), F32),
        compiler_params=_cparams(("arbitrary", "arbitrary")),
        name="ada_mod",
    )(cv, ada_w, ada_b.reshape(DEPTH, 1, 6 * D))


def _half_norm(x, g, n):
    outs = []
    for j in range(x.shape[1] // LANES):
        xj = x[:, j * LANES:(j + 1) * LANES]
        lo = _lane(xj.shape) < HD
        s = xj * xj
        s_lo = jnp.sum(jnp.where(lo, s, 0.0), axis=-1, keepdims=True)
        s_hi = jnp.sum(jnp.where(lo, 0.0, s), axis=-1, keepdims=True)
        r = lax.rsqrt(jnp.where(lo, s_lo, s_hi) * (1.0 / n) + EPS)
        outs.append(xj * r)
    return jnp.concatenate(outs, axis=-1) * g


def _group_norm(x, g, n):
    outs = []
    for j in range(x.shape[1] // LANES):
        xj = x[:, j * LANES:(j + 1) * LANES]
        r = lax.rsqrt(jnp.sum(xj * xj, axis=-1, keepdims=True) * (1.0 / n) + EPS)
        outs.append(xj * r)
    return jnp.concatenate(outs, axis=-1) * g


def _rope(x, c, sa, sb, shift):
    outs = []
    for j in range(x.shape[1] // LANES):
        xj = x[:, j * LANES:(j + 1) * LANES]
        outs.append(xj * c + pltpu.roll(xj, LANES - shift, 1) * sa + pltpu.roll(xj, shift, 1) * sb)
    return jnp.concatenate(outs, axis=-1)


def _in_kernel(x_ref, sh_ref, sc_ref, n1_ref, w_ref, gqa_ref, gka_ref, gb_ref, gcq_ref, gckv_ref,
               wuq_ref, gqn_ref, ra_c, ra_a, ra_b, rc_c, rc_a, rc_b,
               qa_o, ka_o, va_o, kaf_o, vaf_o, bq_o, bk_o, bv_o, bo_o, li_o, lf_o, qc_o, ckv_o, ckr_o):
    i = pl.program_id(0)
    x = x_ref[...]
    xn = x * lax.rsqrt(jnp.mean(x * x, axis=-1, keepdims=True) + EPS) * n1_ref[...]
    xn = xn * (1.0 + sc_ref[0]) + sh_ref[0]
    p = _dot(xn.astype(BF16), w_ref[0])

    def seg(name):
        o, n = SEG[name]
        return p[:, o:o + n]

    qa = _half_norm(seg("QA"), gqa_ref[...], HD)
    ka = _half_norm(seg("KA"), gka_ref[...], HD)
    va = seg("VA")

    @pl.when(i < NT_P)
    def _():
        kaf_o[...] = jnp.concatenate([ka[:, 0:HD], ka[:, LANES:LANES + HD]], axis=-1)
        vaf_o[...] = jnp.concatenate([va[:, 0:HD], va[:, LANES:LANES + HD]], axis=-1)

    qa_o[...] = _rope(qa, ra_c[...], ra_a[...], ra_b[...], 32).astype(BF16)
    ka_o[...] = _rope(ka, ra_c[...], ra_a[...], ra_b[...], 32).astype(BF16)
    va_o[...] = va.astype(BF16)

    bq_o[...] = seg("BQ").astype(BF16)
    bk_o[...] = (seg("BK") * (HD ** -0.5)).astype(BF16)
    bv_o[...] = seg("BV").astype(BF16)
    bo_o[...] = seg("BO")
    gb = gb_ref[...]
    li_o[...] = seg("GI") + gb[:, 0:256]
    lf_o[...] = jax.nn.log_sigmoid(seg("GF") + gb[:, 256:512])

    cq = seg("CQ")
    cqn = cq * lax.rsqrt(jnp.mean(cq * cq, axis=-1, keepdims=True) + EPS) * gcq_ref[...]
    qc = _group_norm(_dot(cqn.astype(BF16), wuq_ref[0]), gqn_ref[...], C_QK)
    qc_o[...] = _rope(qc, rc_c[...], rc_a[...], rc_b[...], 16).astype(BF16)

    ckv = seg("CKV")
    ckv_o[...] = ckv * lax.rsqrt(jnp.mean(ckv * ckv, axis=-1, keepdims=True) + EPS) * gckv_ref[...]
    ckr_o[...] = seg("CKR")


def _in_call(x, mod3, l, n1g, w_in_p, gqa, gka, gb, gcq, gckv, wuq_p, gqn, rope_a, rope_c):
    row = lambda k: pl.BlockSpec((1, 1, D), lambda i: ((l * 8 + _mod_row(i)) * 6 + k, 0, 0))
    vec = lambda n: pl.BlockSpec((1, n), lambda i: (0, 0))
    tab = pl.BlockSpec((TM, LANES), lambda i: (_rope_blk(i), 0))
    tok = lambda n: pl.BlockSpec((TM, n), lambda i: (i, 0))
    tokp = pl.BlockSpec((TM, LANES), lambda i: (jnp.minimum(i, NT_P - 1), 0))
    o = lambda n, dt: jax.ShapeDtypeStruct((T, n), dt)
    return pl.pallas_call(
        _in_kernel,
        grid=(NT,),
        in_specs=[tok(D), row(0), row(1), vec(D),
                  pl.BlockSpec((1, D, NP_IN), lambda i: (l, 0, 0)),
                  vec(384), vec(256), vec(512), vec(256), vec(128),
                  pl.BlockSpec((1, C_QRANK, 768), lambda i: (l, 0, 0)), vec(768),
                  tab, tab, tab, tab, tab, tab],
        out_specs=[tok(384), tok(256), tok(256), tokp, tokp, tok(256), tok(256), tok(256), tok(256),
                   tok(256), tok(256), tok(768), tok(128), tok(128)],
        out_shape=[o(384, BF16), o(256, BF16), o(256, BF16),
                   jax.ShapeDtypeStruct((T_P, LANES), F32), jax.ShapeDtypeStruct((T_P, LANES), F32),
                   o(256, BF16), o(256, BF16), o(256, BF16), o(256, F32),
                   o(256, F32), o(256, F32), o(768, BF16), o(128, F32), o(128, F32)],
        compiler_params=_cparams(("arbitrary",)),
        name=f"in_proj_l{l}",
    )(x, mod3, mod3, n1g, w_in_p, gqa, gka, gb, gcq, gckv, wuq_p, gqn, *rope_a, *rope_c)


def _mlakv_kernel(ckv_ref, ckr_ref, wk_ref, wv_ref, gkn_ref, rc_c, rc_a, rc_b, k_o, v_o):
    c = ckv_ref[...].astype(BF16)
    k = _dot(c, wk_ref[0])
    kr = ckr_ref[...]
    k = k + jnp.concatenate([kr] * C_HEADS, axis=-1)
    k = _group_norm(k, gkn_ref[...], C_QK)
    k_o[...] = _rope(k, rc_c[...], rc_a[...], rc_b[...], 16).astype(BF16)
    v_o[...] = _dot(c, wv_ref[0]).astype(BF16)


def _mlakv_call(ckv_all, ckr_all, l, wk_p, wv_p, gkn, rope_c):
    r = ckv_all.shape[0]
    tab = pl.BlockSpec((TM, LANES), lambda i: (_rope_blk(i), 0))
    tok = lambda n: pl.BlockSpec((TM, n), lambda i: (i, 0))
    return pl.pallas_call(
        _mlakv_kernel,
        grid=(r // TM,),
        in_specs=[tok(128), tok(128),
                  pl.BlockSpec((1, C_KVRANK, 768), lambda i: (l, 0, 0)),
                  pl.BlockSpec((1, C_KVRANK, 384), lambda i: (l, 0, 0)),
                  pl.BlockSpec((1, 768), lambda i: (0, 0)), tab, tab, tab],
        out_specs=[tok(768), tok(384)],
        out_shape=[jax.ShapeDtypeStruct((r, 768), BF16), jax.ShapeDtypeStruct((r, 384), BF16)],
        compiler_params=_cparams(("arbitrary",)),
        name=f"mla_kv_l{l}",
    )(ckv_all, ckr_all, wk_p, wv_p, gkn, *rope_c)


def _softmax_pv(scores, values, sink):
    m = scores[0].max(axis=-1, keepdims=True)
    for s in scores[1:]:
        m = jnp.maximum(m, s.max(axis=-1, keepdims=True))
    if sink is not None:
        m = jnp.maximum(m, sink)
    es = [jnp.exp(s - m) for s in scores]
    den = es[0].sum(axis=-1, keepdims=True)
    for e in es[1:]:
        den = den + e.sum(axis=-1, keepdims=True)
    if sink is not None:
        den = den + jnp.exp(sink - m)
    inv = 1.0 / den
    out = None
    for e, v in zip(es, values):
        o = _dot((e * inv).astype(BF16), v)
        out = o if out is None else out + o
    return out


def _half(x, hi):
    lo = _lane(x.shape) < HD
    return jnp.where(lo != hi, x, jnp.zeros_like(x))


def _attn_a_prompt_kernel(sink_ref, q_ref, k_ref, v_ref, o_ref):
    q = q_ref[...]
    k = k_ref[...]
    v = v_ref[...]
    for j in range(A_HEADS // 2):
        acc = None
        for c in range(2):
            h = 2 * j + c
            g = h // (A_HEADS // A_KV)
            qh = _half(q[:, j * LANES:(j + 1) * LANES], c == 1)
            s = _dot_nt(qh, k[:, g * LANES:(g + 1) * LANES]) * (HD ** -0.5)
            vh = _half(v[:, g * LANES:(g + 1) * LANES], c == 1)
            o = _softmax_pv([s], [vh], sink_ref[h])
            acc = o if acc is None else acc + o
        o_ref[:, j * LANES:(j + 1) * LANES] = acc.astype(BF16)


def _attn_a_prompt_call(sink, qa, ka, va):
    return pl.pallas_call(
        _attn_a_prompt_kernel,
        grid_spec=pltpu.PrefetchScalarGridSpec(
            num_scalar_prefetch=1, grid=(NB_P,),
            in_specs=[pl.BlockSpec((S_P, 384), lambda b, s: (b, 0)),
                      pl.BlockSpec((S_P, 256), lambda b, s: (b, 0)),
                      pl.BlockSpec((S_P, 256), lambda b, s: (b, 0))],
            out_specs=pl.BlockSpec((S_P, 384), lambda b, s: (b, 0))),
        out_shape=jax.ShapeDtypeStruct((T_P, 384), BF16),
        compiler_params=_cparams(("arbitrary",)),
        name="attn_a_prompt",
    )(sink, qa, ka, va)


QB = 128
WIN = 128
BAND = QB + 2 * WIN


def _attn_a_sample_kernel(sink_ref, q_ref, k_ref, v_ref, kc_ref, vc_ref, o_ref):
    n = pl.program_id(1)
    ws = pl.multiple_of(jnp.clip(n * QB - WIN, 0, S_S - BAND), QB)
    q = q_ref[...]
    kb = k_ref[pl.ds(ws, BAND), :]
    vb = v_ref[pl.ds(ws, BAND), :]
    kc = kc_ref[0, 0].astype(BF16)
    vc = vc_ref[0, 0].astype(BF16)
    qpos = n * QB + lax.broadcasted_iota(jnp.int32, (QB, BAND), 0)
    kpos = ws + lax.broadcasted_iota(jnp.int32, (QB, BAND), 1)
    ok = jnp.abs(qpos - kpos) <= WIN
    lane = _lane((PAST, LANES))
    for j in range(A_HEADS // 2):
        acc = None
        for c in range(2):
            h = 2 * j + c
            g = h // (A_HEADS // A_KV)
            qh = _half(q[:, j * LANES:(j + 1) * LANES], c == 1)
            s_b = jnp.where(ok, _dot_nt(qh, kb[:, g * LANES:(g + 1) * LANES]) * (HD ** -0.5), NEG)
            kcg = jnp.where((lane < HD) == (g == 0), kc, jnp.zeros_like(kc))
            vcg = jnp.where((lane < HD) == (g == 0), vc, jnp.zeros_like(vc))
            if (g == 1) != (c == 1):
                kcg = pltpu.roll(kcg.astype(F32), HD, 1).astype(BF16)
                vcg = pltpu.roll(vcg.astype(F32), HD, 1).astype(BF16)
            s_c = _dot_nt(qh, kcg) * (HD ** -0.5)
            vh = _half(vb[:, g * LANES:(g + 1) * LANES], c == 1)
            o = _softmax_pv([s_b, s_c], [vh, vcg], sink_ref[h])
            acc = o if acc is None else acc + o
        o_ref[:, j * LANES:(j + 1) * LANES] = acc.astype(BF16)


def _attn_a_sample_call(sink, qa, ka, va, cache_k, cache_v, l):
    nqb = S_S // QB
    off = T_P // S_S
    return pl.pallas_call(
        _attn_a_sample_kernel,
        grid_spec=pltpu.PrefetchScalarGridSpec(
            num_scalar_prefetch=1, grid=(NB_S, nqb),
            in_specs=[pl.BlockSpec((QB, 384), lambda b, n, s: (T_P // QB + b * nqb + n, 0)),
                      pl.BlockSpec((S_S, 256), lambda b, n, s: (off + b, 0)),
                      pl.BlockSpec((S_S, 256), lambda b, n, s: (off + b, 0)),
                      pl.BlockSpec((1, 1, PAST, LANES), lambda b, n, s: (b, l, 0, 0)),
                      pl.BlockSpec((1, 1, PAST, LANES), lambda b, n, s: (b, l, 0, 0))],
            out_specs=pl.BlockSpec((QB, 384), lambda b, n, s: (b * nqb + n, 0))),
        out_shape=jax.ShapeDtypeStruct((T_S, 384), BF16),
        compiler_params=_cparams(("arbitrary", "arbitrary")),
        name=f"attn_a_sample_l{l}",
    )(sink, qa, ka, va, cache_k, cache_v)


def _mla_heads(q, ks, vs, o_ref):
    for j in range(C_HEADS // 2):
        acc = None
        for c in range(2):
            h = 2 * j + c
            qh = q[:, h * LANES:(h + 1) * LANES]
            scores = [_dot_nt(qh, k[:, h * LANES:(h + 1) * LANES]) * (C_QK ** -0.5) for k in ks]
            vals = [_half(v[:, j * LANES:(j + 1) * LANES], c == 1) for v in vs]
            o = _softmax_pv(scores, vals, None)
            acc = o if acc is None else acc + o
        o_ref[:, j * LANES:(j + 1) * LANES] = acc.astype(BF16)


def _mla_prompt_kernel(q_ref, k_ref, v_ref, o_ref):
    _mla_heads(q_ref[...], [k_ref[...]], [v_ref[...]], o_ref)


def _mla_prompt_call(qc, kc, vc):
    return pl.pallas_call(
        _mla_prompt_kernel,
        grid=(NB_P,),
        in_specs=[pl.BlockSpec((S_P, 768), lambda b: (b, 0)),
                  pl.BlockSpec((S_P, 768), lambda b: (b, 0)),
                  pl.BlockSpec((S_P, 384), lambda b: (b, 0))],
        out_specs=pl.BlockSpec((S_P, 384), lambda b: (b, 0)),
        out_shape=jax.ShapeDtypeStruct((T_P, 384), BF16),
        compiler_params=_cparams(("arbitrary",)),
        name="mla_prompt",
    )(qc, kc, vc)


def _mla_sample_kernel(q_ref, kc_ref, vc_ref, kl_ref, vl_ref, o_ref):
    _mla_heads(q_ref[...], [kc_ref[...], kl_ref[...]], [vc_ref[...], vl_ref[...]], o_ref)


def _mla_sample_call(qc, kc, vc):
    tq = 256
    nq = S_S // tq
    return pl.pallas_call(
        _mla_sample_kernel,
        grid=(NB_S, nq),
        in_specs=[pl.BlockSpec((tq, 768), lambda b, n: (T_P // tq + b * nq + n, 0)),
                  pl.BlockSpec((PAST, 768), lambda b, n: (T // PAST + b, 0)),
                  pl.BlockSpec((PAST, 384), lambda b, n: (T // PAST + b, 0)),
                  pl.BlockSpec((S_S, 768), lambda b, n: (T_P // S_S + b, 0)),
                  pl.BlockSpec((S_S, 384), lambda b, n: (T_P // S_S + b, 0))],
        out_specs=pl.BlockSpec((tq, 384), lambda b, n: (b * nq + n, 0)),
        out_shape=jax.ShapeDtypeStruct((T_S, 384), BF16),
        compiler_params=_cparams(("arbitrary", "arbitrary")),
        name="mla_sample",
    )(qc, kc, vc, kc, vc)


def _split3(x):
    x1 = x.astype(BF16)
    r = x - x1.astype(F32)
    x2 = r.astype(BF16)
    x3 = (r - x2.astype(F32)).astype(BF16)
    return x1, x2, x3


def _exact_dot(a01, x):
    x1, x2, x3 = _split3(x)
    return _dot(a01, x1) + _dot(a01, x2) + _dot(a01, x3)


def _col(x, k):
    return x[:, k:k + 1]


def _mlstm_kernel(nc, has_state, *refs):
    if has_state:
        (q_ref, k_ref, v_ref, bo_ref, li_ref, lf_ref, hn_ref, c0_ref, n0_ref, m0_ref,
         ob_ref, h_scr) = refs
    else:
        (q_ref, k_ref, v_ref, bo_ref, li_ref, lf_ref, hn_ref,
         ob_ref, cn_ref, nn_ref, mn_ref, h_scr) = refs
    s_len = nc * CHUNK
    li = li_ref[...]
    lf = lf_ref[...]
    row = lax.broadcasted_iota(jnp.int32, (CHUNK, CHUNK), 0)
    colv = lax.broadcasted_iota(jnp.int32, (CHUNK, CHUNK), 1)
    tri_f = (colv <= row)
    tri_b = (colv >= row)
    tri_f01 = tri_f.astype(BF16)
    tri_b01 = tri_b.astype(BF16)
    lane1 = _lane((1, LANES))
    fwd_lane = lane1 < 2

    bcs, bls, ggs, mgs = [], [], [], []
    for c in range(nc):
        lc = lf[c * CHUNK:(c + 1) * CHUNK]
        bc = jnp.where(fwd_lane, _exact_dot(tri_f01, lc), _exact_dot(tri_b01, lc))
        bl = jnp.where(fwd_lane, bc[CHUNK - 1:CHUNK], bc[0:1])
        gg = bl - bc + li[c * CHUNK:(c + 1) * CHUNK]
        bcs.append(bc)
        bls.append(bl)
        ggs.append(gg)
        mgs.append(gg.max(axis=0, keepdims=True))

    m0 = m0_ref[0] if has_state else jnp.zeros((1, LANES), F32)
    m_prev = [None] * nc
    m_next = [None] * nc
    mf = m0
    mb = m0
    mf_prev, mf_next, mb_prev, mb_next = {}, {}, {}, {}
    for i in range(nc):
        mf_prev[i] = mf
        mf = jnp.maximum(bls[i] + mf, mgs[i])
        mf_next[i] = mf
        cb = nc - 1 - i
        mb_prev[cb] = mb
        mb = jnp.maximum(bls[cb] + mb, mgs[cb])
        mb_next[cb] = mb
    for c in range(nc):
        m_prev[c] = jnp.where(fwd_lane, mf_prev[c], mb_prev[c])
        m_next[c] = jnp.where(fwd_lane, mf_next[c], mb_next[c])

    bc_all = jnp.concatenate(bcs, axis=0)
    ut = (li - bc_all).T

    lane2 = _lane((CHUNK, LANES))
    lo2 = lane2 < HD
    rr = lax.broadcasted_iota(jnp.int32, (LANES, LANES), 0)
    cc = lax.broadcasted_iota(jnp.int32, (LANES, LANES), 1)
    blockdiag = (rr < HD) == (cc < HD)

    for d in range(2):
        if has_state:
            cst = c0_ref[0, d]
            nst = n0_ref[0, d]
        else:
            cst = jnp.zeros((LANES, LANES), F32)
            nst = jnp.zeros((1, LANES), F32)
        order = range(nc) if d == 0 else range(nc - 1, -1, -1)
        causal = tri_f if d == 0 else tri_b
        for c in order:
            rows = slice(c * CHUNK, (c + 1) * CHUNK)
            q2 = q_ref[rows, :]
            k2 = k_ref[rows, :]
            v2 = v_ref[rows, :]
            qf = q2.astype(F32)
            qc_all = _dot(q2, cst.astype(BF16))
            qn_all = qf * nst
            num = None
            a_cols, dn_cols, ws_cols, dec = [], [], [], []
            for e in range(2):
                kk = 2 * d + e
                bc_col = _col(bcs[c], kk)
                d_mat = jnp.where(causal, bc_col + ut[kk:kk + 1, rows], NEG)
                mp = _col(m_prev[c], kk)
                mn = _col(m_next[c], kk)
                inter = bc_col + mp
                mt = jnp.maximum(inter, d_mat.max(axis=-1, keepdims=True))
                a = jnp.exp(inter - mt)
                qe = _half(q2, e == 1)
                w = jnp.exp(d_mat - mt) * _dot_nt(qe, k2)
                o = _dot(w.astype(BF16), _half(v2, e == 1))
                num = o if num is None else num + o
                qn = jnp.sum(jnp.where(lo2 != (e == 1), qn_all, 0.0), axis=-1, keepdims=True)
                den = a * qn + w.sum(axis=-1, keepdims=True)
                a_cols.append(a)
                dn_cols.append(jnp.maximum(jnp.abs(den), jnp.exp(-mt)))
                ws_cols.append(jnp.exp(_col(ggs[c], kk) - mn))
                dec.append(jnp.exp(_col(bls[c], kk) + mp - mn))
            a2 = jnp.where(lo2, a_cols[0], a_cols[1])
            dn2 = jnp.where(lo2, dn_cols[0], dn_cols[1])
            h2 = (a2 * qc_all + num) / dn2
            if d == 0:
                h_scr[rows, :] = h2
            else:
                h_scr[rows, :] = h_scr[rows, :] + h2
            kw = k2.astype(F32) * jnp.where(lo2, ws_cols[0], ws_cols[1])
            dec2 = jnp.where(lane1 < HD, dec[0], dec[1])
            u = _dot_tn(kw.astype(BF16), v2)
            cst = dec2 * cst + jnp.where(blockdiag, u, 0.0)
            nst = dec2 * nst + kw.sum(axis=0, keepdims=True)
        if not has_state:
            cn_ref[0, d] = cst
            nn_ref[0, d] = nst
    if not has_state:
        mn_ref[0] = jnp.where(fwd_lane, mf, mb)

    hs = h_scr[...]
    lo = _lane(hs.shape) < HD
    s = hs * hs
    s_lo = jnp.sum(jnp.where(lo, s, 0.0), axis=-1, keepdims=True)
    s_hi = jnp.sum(jnp.where(lo, 0.0, s), axis=-1, keepdims=True)
    r = lax.rsqrt(jnp.where(lo, s_lo, s_hi) * (1.0 / HD) + EPS)
    ob_ref[...] = (hs * r * hn_ref[...] * jax.nn.sigmoid(bo_ref[...])).astype(BF16)


def _mlstm_call(bq, bk, bv, bo, li, lf, hn2, state, l):
    has_state = state is not None
    if has_state:
        nb, s_len, base = NB_S, S_S, T_P // S_S
    else:
        nb, s_len, base = NB_P, S_P, 0
    nc = s_len // CHUNK
    tokp = pl.BlockSpec((s_len, LANES), lambda b, j: (base + b, j))
    in_specs = [tokp, tokp, tokp, tokp, tokp, tokp, pl.BlockSpec((1, LANES), lambda b, j: (0, 0))]
    args = [bq, bk, bv, bo, li, lf, hn2]
    ob_spec = pl.BlockSpec((s_len, LANES), lambda b, j: (b, j))
    ob_shape = jax.ShapeDtypeStruct((nb * s_len, 256), BF16)
    if has_state:
        c0, n0, m0 = state
        in_specs += [pl.BlockSpec((1, 2, LANES, LANES), lambda b, j: (b * 2 + j, 0, 0, 0)),
                     pl.BlockSpec((1, 2, 1, LANES), lambda b, j: (b * 2 + j, 0, 0, 0)),
                     pl.BlockSpec((1, 1, LANES), lambda b, j: (b * 2 + j, 0, 0))]
        args += [c0, n0, m0]
        out_specs = ob_spec
        out_shape = ob_shape
    else:
        out_specs = [ob_spec,
                     pl.BlockSpec((1, 2, LANES, LANES), lambda b, j: (b * 2 + j, 0, 0, 0)),
                     pl.BlockSpec((1, 2, 1, LANES), lambda b, j: (b * 2 + j, 0, 0, 0)),
                     pl.BlockSpec((1, 1, LANES), lambda b, j: (b * 2 + j, 0, 0))]
        out_shape = [ob_shape,
                     jax.ShapeDtypeStruct((nb * 2, 2, LANES, LANES), F32),
                     jax.ShapeDtypeStruct((nb * 2, 2, 1, LANES), F32),
                     jax.ShapeDtypeStruct((nb * 2, 1, LANES), F32)]
    return pl.pallas_call(
        functools.partial(_mlstm_kernel, nc, has_state),
        grid=(nb, 2),
        in_specs=in_specs,
        out_specs=out_specs,
        out_shape=out_shape,
        scratch_shapes=[pltpu.VMEM((s_len, LANES), F32)],
        compiler_params=_cparams(("arbitrary", "arbitrary")),
        name=f"mlstm_{'sample' if has_state else 'prompt'}_l{l}",
    )(*args)


def _out_kernel(moe, x_ref, oa_ref, ob_ref, oc_ref, w_ref, g1_ref, sh_ref, sc_ref, n2_ref, *rest):
    if moe:
        rh_ref, rl_ref, x1_o, xn_o, route_o, cnt_o, run_scr = rest
    else:
        x1_o, xn_o = rest
    o = (_dot(oa_ref[...], w_ref[0, 0:384, :]) + _dot(ob_ref[...], w_ref[0, 384:640, :])
         + _dot(oc_ref[...], w_ref[0, 640:1024, :]))
    x1 = x_ref[...] + g1_ref[0] * o
    x1_o[...] = x1
    xn = x1 * lax.rsqrt(jnp.mean(x1 * x1, axis=-1, keepdims=True) + EPS) * n2_ref[...]
    xn = xn * (1.0 + sc_ref[0]) + sh_ref[0]
    xb = xn.astype(BF16)
    if not moe:
        xn_o[...] = xb
    else:
        xn_o[...] = xn
        xl = (xn - xb.astype(F32)).astype(BF16)
        logits = _dot(xb, rh_ref[0]) + (_dot(xl, rh_ref[0]) + _dot(xb, rl_ref[0]))
        lane = _lane(logits.shape)
        logits = jnp.where(lane < N_EXP, logits, -jnp.inf)
        m1 = logits.max(axis=-1, keepdims=True)
        i1 = jnp.min(jnp.where(logits == m1, lane, LANES), axis=-1, keepdims=True)
        rest_l = jnp.where(lane == i1, -jnp.inf, logits)
        m2 = rest_l.max(axis=-1, keepdims=True)
        i2 = jnp.min(jnp.where(rest_l == m2, lane, LANES), axis=-1, keepdims=True)
        e2 = jnp.exp(m2 - m1)
        den = 1.0 + e2
        w1 = 1.0 / den
        w2 = e2 / den

        @pl.when(pl.program_id(0) == 0)
        def _():
            run_scr[...] = jnp.zeros(run_scr.shape, F32)

        sel = jnp.where(lane == i1, 1.0, jnp.where((lane == i2) & (w2 > 0.0), 1.0, 0.0))
        before = (lax.broadcasted_iota(jnp.int32, (TM, TM), 1)
                  < lax.broadcasted_iota(jnp.int32, (TM, TM), 0)).astype(BF16)
        rank = run_scr[...] + _dot(before, sel.astype(BF16))
        r1 = jnp.sum(jnp.where(lane == i1, rank, 0.0), axis=-1, keepdims=True)
        r2 = jnp.sum(jnp.where(lane == i2, rank, 0.0), axis=-1, keepdims=True)
        run_scr[...] += jnp.sum(sel, axis=0, keepdims=True)
        cnt_o[...] = run_scr[...]
        fields = [i1.astype(F32), i2.astype(F32), r1, r2, w1, w2]
        info = jnp.zeros(lane.shape, F32)
        for k, v in enumerate(fields):
            info = jnp.where(lane == k, v, info)
        route_o[...] = info.T


def _out_call(x, oa, ob, oc, w_out_b, mod3, l, n2g, router):
    moe = router is not None
    row = lambda k: pl.BlockSpec((1, 1, D), lambda i: ((l * 8 + _mod_row(i)) * 6 + k, 0, 0))
    tok = lambda n: pl.BlockSpec((TM, n), lambda i: (i, 0))
    in_specs = [tok(D), tok(384), tok(256), tok(384),
                pl.BlockSpec((1, D, D), lambda i: (l, 0, 0)),
                row(2), row(3), row(4), pl.BlockSpec((1, D), lambda i: (0, 0))]
    args = [x, oa, ob, oc, w_out_b, mod3, mod3, mod3, n2g]
    out_specs = [tok(D), tok(D)]
    out_shape = [jax.ShapeDtypeStruct((T, D), F32), jax.ShapeDtypeStruct((T, D), BF16)]
    if moe:
        out_shape[1] = jax.ShapeDtypeStruct((T, D), F32)
        rh, rl = router
        in_specs += [pl.BlockSpec((1, D, LANES), lambda i: (0, 0, 0))] * 2
        args += [rh, rl]
        out_specs += [pl.BlockSpec((LANES, TM), lambda i: (0, i)), pl.BlockSpec((1, LANES), lambda i: (0, 0))]
        out_shape += [jax.ShapeDtypeStruct((LANES, T), F32), jax.ShapeDtypeStruct((1, LANES), F32)]
    return pl.pallas_call(
        functools.partial(_out_kernel, moe),
        grid=(NT,),
        in_specs=in_specs, out_specs=out_specs, out_shape=out_shape,
        scratch_shapes=[pltpu.VMEM((1, LANES), F32)] if moe else [],
        compiler_params=_cparams(("arbitrary",)),
        name=f"out_proj_l{l}",
    )(*args)


FFN_TM = 1024


def _g2_spec(l):
    def idx(i, *_):
        r = jnp.where(i < T_P // FFN_TM, 0, 1 + (i - T_P // FFN_TM) // (S_S // FFN_TM))
        return ((l * 8 + r) * 6 + 5, 0, 0)
    return pl.BlockSpec((1, 1, D), idx)


def _swiglu_part(x, w1_ref, w3_ref, w2_ref):
    a = _dot(x, w1_ref[...].astype(BF16))
    b = _dot(x, w3_ref[...].astype(BF16))
    h = (a * jax.nn.sigmoid(a) * b).astype(BF16)
    return _dot(h, w2_ref[...].astype(BF16))


def _ffn_kernel(xn_ref, x1_ref, g2_ref, w1_ref, w3_ref, w2_ref, y_ref, acc):
    f = pl.program_id(1)
    part = _swiglu_part(xn_ref[...], w1_ref.at[0], w3_ref.at[0], w2_ref.at[0])

    @pl.when(f == 0)
    def _():
        acc[...] = part

    @pl.when(f > 0)
    def _():
        acc[...] += part

    @pl.when(f == pl.num_programs(1) - 1)
    def _():
        y_ref[...] = x1_ref[...] + g2_ref[0] * acc[...]


def _ffn_call(xn, x1, mod3, l, w1, w3, w2, i_layer):
    tf = 256
    return pl.pallas_call(
        _ffn_kernel,
        grid=(T // FFN_TM, D_FF // tf),
        in_specs=[pl.BlockSpec((FFN_TM, D), lambda i, f: (i, 0)),
                  pl.BlockSpec((FFN_TM, D), lambda i, f: (i, 0)),
                  _g2_spec(l),
                  pl.BlockSpec((1, D, tf), lambda i, f: (i_layer, 0, f)),
                  pl.BlockSpec((1, D, tf), lambda i, f: (i_layer, 0, f)),
                  pl.BlockSpec((1, tf, D), lambda i, f: (i_layer, f, 0))],
        out_specs=pl.BlockSpec((FFN_TM, D), lambda i, f: (i, 0)),
        out_shape=jax.ShapeDtypeStruct((T, D), F32),
        scratch_shapes=[pltpu.VMEM((FFN_TM, D), F32)],
        compiler_params=_cparams(("arbitrary", "arbitrary")),
        name="ffn_dense",
    )(xn, x1, mod3, w1, w3, w2)


MOE_BM = 1024
MOE_SUB = 256
MOE_NBLK = 2 * T // MOE_BM + N_EXP
MOE_NR = MOE_NBLK * MOE_BM
MOE_TF = 512
FIN_TM = 512


def _route_kernel(i1_ref, i2_ref, r1_ref, r2_ref, cnt_ref, w1_ref, w2_ref,
                  src_ref, gate_ref, eb_ref, rows_ref, nvb_ref, off_scr):
    def clear(r, carry):
        src_ref[r] = 0
        gate_ref[r] = 0.0
        return carry

    lax.fori_loop(0, MOE_NR, clear, 0, unroll=8)

    def clear_blk(b, carry):
        eb_ref[b] = 0
        rows_ref[b] = 0
        return carry

    lax.fori_loop(0, MOE_NBLK, clear_blk, 0)

    nblk = jnp.int32(0)
    for e in range(N_EXP):
        c = cnt_ref[e]
        nbe = (c + (MOE_BM - 1)) // MOE_BM
        off_scr[e] = nblk * MOE_BM

        def fill(j, carry, e=e, c=c, nblk=nblk):
            eb_ref[nblk + j] = e
            rows_ref[nblk + j] = jnp.minimum(c - j * MOE_BM, MOE_BM)
            return carry

        lax.fori_loop(0, nbe, fill, 0)
        nblk = nblk + nbe
    nvb_ref[0] = nblk

    def place(t, carry):
        p1 = off_scr[i1_ref[t]] + r1_ref[t]
        src_ref[p1] = t
        gate_ref[p1] = w1_ref[t]
        w2 = w2_ref[t]

        @pl.when(w2 > 0.0)
        def _():
            p2 = off_scr[i2_ref[t]] + r2_ref[t]
            src_ref[p2] = t
            gate_ref[p2] = w2

        return carry

    lax.fori_loop(0, T, place, 0, unroll=4)


def _route_call(route, cnt):
    ints = [route[k].astype(jnp.int32) for k in range(4)]
    cnt8 = cnt[0, :N_EXP].astype(jnp.int32)
    smem = pl.BlockSpec(memory_space=pltpu.SMEM)
    return pl.pallas_call(
        _route_kernel,
        grid_spec=pltpu.PrefetchScalarGridSpec(
            num_scalar_prefetch=5, grid=(1,),
            in_specs=[smem, smem],
            out_specs=[smem, smem, smem, smem, smem],
            scratch_shapes=[pltpu.SMEM((N_EXP,), jnp.int32)]),
        out_shape=[jax.ShapeDtypeStruct((MOE_NR,), jnp.int32), jax.ShapeDtypeStruct((MOE_NR,), F32),
                   jax.ShapeDtypeStruct((MOE_NBLK,), jnp.int32), jax.ShapeDtypeStruct((MOE_NBLK,), jnp.int32),
                   jax.ShapeDtypeStruct((1,), jnp.int32)],
        compiler_params=_cparams(("arbitrary",)),
        name="moe_route",
    )(*ints, cnt8, route[4], route[5])


def _moe_gather_kernel(src_ref, rows_ref, x_ref, o_ref, rows_scr):
    k = pl.program_id(0)
    base = k * MOE_SUB
    used = rows_ref[k // (MOE_BM // MOE_SUB)] > (k % (MOE_BM // MOE_SUB)) * MOE_SUB

    @pl.when(used)
    def _():
        def body(i, carry):
            for j in range(8):
                r = i * 8 + j
                rows_scr[pl.ds(r, 1), :] = x_ref[pl.ds(src_ref[base + r], 1), :]
            return carry

        lax.fori_loop(0, MOE_SUB // 8, body, 0)
        o_ref[...] = rows_scr[...].astype(BF16)

    @pl.when(jnp.logical_not(used))
    def _():
        o_ref[...] = jnp.zeros(o_ref.shape, o_ref.dtype)


def _moe_gather_call(src, rows_b, xn):
    return pl.pallas_call(
        _moe_gather_kernel,
        grid_spec=pltpu.PrefetchScalarGridSpec(
            num_scalar_prefetch=2, grid=(MOE_NR // MOE_SUB,),
            in_specs=[pl.BlockSpec((T, D), lambda k, s, rw: (0, 0), pipeline_mode=pl.Buffered(1))],
            out_specs=pl.BlockSpec((MOE_SUB, D), lambda k, s, rw: (k, 0)),
            scratch_shapes=[pltpu.VMEM((MOE_SUB, D), F32)]),
        out_shape=jax.ShapeDtypeStruct((MOE_NR, D), BF16),
        compiler_params=_cparams(("arbitrary",)),
        name="moe_gather",
    )(src, rows_b, xn)


def _moe_kernel(eb_ref, rows_ref, nvb_ref, xs_ref, w1_ref, w3_ref, w2_ref, y_ref, w1b, w3b, w2b):
    b = pl.program_id(0)
    f = pl.program_id(1)
    nsub = jnp.where(b < nvb_ref[0], (rows_ref[b] + MOE_SUB - 1) // MOE_SUB, 0)

    @pl.when(nsub > 0)
    def _():
        w1b[...] = w1_ref[0, 0].astype(BF16)
        w3b[...] = w3_ref[0, 0].astype(BF16)
        w2b[...] = w2_ref[0, 0].astype(BF16)

    for k in range(1, MOE_BM // MOE_SUB + 1):
        m = k * MOE_SUB

        @pl.when(nsub == k)
        def _(m=m):
            x = xs_ref[0:m, :]
            a = _dot(x, w1b[...])
            g = _dot(x, w3b[...])
            h = (a * jax.nn.sigmoid(a) * g).astype(BF16)
            part = _dot(h, w2b[...])

            @pl.when(f == 0)
            def _():
                y_ref[0:m, :] = part
                if m < MOE_BM:
                    y_ref[m:MOE_BM, :] = jnp.zeros((MOE_BM - m, D), F32)

            @pl.when(f > 0)
            def _():
                y_ref[0:m, :] += part


def _moe_call(xs, e_b, rows_b, nvb, w1, w3, w2, i_layer):
    nf = D_FFE // MOE_TF

    def blk(b, nvb_ref):
        return jnp.minimum(b, nvb_ref[0] - 1)

    def ff(b, f, nvb_ref):
        return jnp.where(b < nvb_ref[0], f, nf - 1)

    return pl.pallas_call(
        _moe_kernel,
        grid_spec=pltpu.PrefetchScalarGridSpec(
            num_scalar_prefetch=3, grid=(MOE_NBLK, nf),
            in_specs=[pl.BlockSpec((MOE_BM, D), lambda b, f, eb, rw, nv: (blk(b, nv), 0)),
                      pl.BlockSpec((1, 1, D, MOE_TF),
                                   lambda b, f, eb, rw, nv: (i_layer, eb[blk(b, nv)], 0, ff(b, f, nv))),
                      pl.BlockSpec((1, 1, D, MOE_TF),
                                   lambda b, f, eb, rw, nv: (i_layer, eb[blk(b, nv)], 0, ff(b, f, nv))),
                      pl.BlockSpec((1, 1, MOE_TF, D),
                                   lambda b, f, eb, rw, nv: (i_layer, eb[blk(b, nv)], ff(b, f, nv), 0))],
            out_specs=pl.BlockSpec((MOE_BM, D), lambda b, f, eb, rw, nv: (blk(b, nv), 0)),
            scratch_shapes=[pltpu.VMEM((D, MOE_TF), BF16), pltpu.VMEM((D, MOE_TF), BF16),
                            pltpu.VMEM((MOE_TF, D), BF16)]),
        out_shape=jax.ShapeDtypeStruct((MOE_NR, D), F32),
        compiler_params=_cparams(("arbitrary", "arbitrary")),
        name="moe_experts",
    )(e_b, rows_b, nvb, xs, w1, w3, w2)


def _moe_combine_kernel(src_ref, gate_ref, rows_ref, nvb_ref, ys_ref, x1_ref, g2_ref, y_ref, acc):
    i = pl.program_id(0)

    @pl.when(i == 0)
    def _():
        acc[...] = jnp.zeros(acc.shape, F32)

    @pl.when(i < nvb_ref[0])
    def _():
        base = i * MOE_BM
        n = rows_ref[i]

        def add_rows(r0, cnt):
            toks = [src_ref[base + r0 + j] for j in range(cnt)]
            vals = [acc[pl.ds(toks[j], 1), :] + gate_ref[base + r0 + j] * ys_ref[pl.ds(r0 + j, 1), :]
                    for j in range(cnt)]
            for j in range(cnt):
                acc[pl.ds(toks[j], 1), :] = vals[j]

        def body4(q, carry):
            add_rows(q * 4, 4)
            return carry

        lax.fori_loop(0, n // 4, body4, 0)

        def body1(r, carry):
            add_rows(r, 1)
            return carry

        lax.fori_loop((n // 4) * 4, n, body1, 0)

    @pl.when(i >= MOE_NBLK)
    def _():
        t0 = pl.multiple_of((i - MOE_NBLK) * FIN_TM, FIN_TM)
        y_ref[...] = x1_ref[...] + g2_ref[0] * acc[pl.ds(t0, FIN_TM), :]


def _moe_combine_call(src, gate_r, rows_b, nvb, ys, x1, mod3, l):
    nfin = T // FIN_TM

    def g2_idx(i, *_):
        j = jnp.maximum(i - MOE_NBLK, 0)
        r = jnp.where(j < T_P // FIN_TM, 0, 1 + (j - T_P // FIN_TM) // (S_S // FIN_TM))
        return ((l * 8 + r) * 6 + 5, 0, 0)

    return pl.pallas_call(
        _moe_combine_kernel,
        grid_spec=pltpu.PrefetchScalarGridSpec(
            num_scalar_prefetch=4, grid=(MOE_NBLK + nfin,),
            in_specs=[pl.BlockSpec((MOE_BM, D),
                                   lambda i, s, g, rw, nv: (jnp.minimum(jnp.minimum(i, MOE_NBLK - 1), nv[0] - 1), 0)),
                      pl.BlockSpec((FIN_TM, D), lambda i, s, g, rw, nv: (jnp.maximum(i - MOE_NBLK, 0), 0)),
                      pl.BlockSpec((1, 1, D), g2_idx)],
            out_specs=pl.BlockSpec((FIN_TM, D), lambda i, s, g, rw, nv: (jnp.maximum(i - MOE_NBLK, 0), 0)),
            scratch_shapes=[pltpu.VMEM((T, D), F32)]),
        out_shape=jax.ShapeDtypeStruct((T, D), F32),
        compiler_params=_cparams(("arbitrary",), vmem=52 * 1024 * 1024),
        name="moe_combine",
    )(src, gate_r, rows_b, nvb, ys, x1, mod3)


def _moe_layer(xp, x1, route, cnt, mod3, l, w1, w3, w2, i_layer):
    src, gate_r, e_b, rows_b, nvb = _route_call(route, cnt)
    xs = _moe_gather_call(src, rows_b, xp)
    ys = _moe_call(xs, e_b, rows_b, nvb, w1, w3, w2, i_layer)
    return _moe_combine_call(src, gate_r, rows_b, nvb, ys, x1, mod3, l)


def _pad_cols(w, n):
    return jnp.pad(w, ((0, 0), (0, n - w.shape[1])))


def _relayout_w_in(w):
    aq, ak, av, bq, bk, bv, bo, bg, cq, ckv, ckr = jnp.split(
        w, (384, 512, 640, 896, 1152, 1408, 1664, 1680, 1936, 2064), axis=1)
    dup = lambda m: jnp.concatenate([m[:, 0:HD], m[:, 0:HD], m[:, HD:2 * HD], m[:, HD:2 * HD]], axis=1)
    g = bg.reshape(D, 2, 2, 2, 2)

    def gate_cols(gi):
        cols = []
        for j in range(2):
            cols.append(_pad_cols(g[:, :, gi, j, :].reshape(D, 4), LANES))
        return jnp.concatenate(cols, axis=1)

    ckr_p = jnp.pad(ckr, ((0, 0), (C_NOPE, LANES - C_NOPE - C_ROPE)))
    out = jnp.concatenate([aq, dup(ak), dup(av), bq, bk, bv, bo, gate_cols(0), gate_cols(1),
                           cq, ckv, ckr_p], axis=1)
    return out.astype(BF16)


def _relayout_gate_b(b):
    g = b.reshape(2, 2, 2, 2)

    def cols(gi):
        return jnp.concatenate([jnp.pad(g[:, gi, j, :].reshape(4), (0, LANES - 4)) for j in range(2)])

    return jnp.concatenate([cols(0), cols(1)])[None, :]


def _pad_heads(w, width):
    r = w.shape[0]
    h = w.shape[1] // width
    return jnp.pad(w.reshape(r, h, width), ((0, 0), (0, 0), (0, LANES - width))).reshape(r, h * LANES)


def _rope_tables(half, span_start, period):
    rows = S_S // 64
    r = jnp.repeat(jnp.arange(rows), 64).astype(F32)
    c = jnp.tile(jnp.arange(64), rows).astype(F32)
    n_freq = half // 2
    freq = 10000.0 ** (-jnp.arange(n_freq, dtype=F32) / n_freq)
    ang = jnp.concatenate([r[:, None] * freq, c[:, None] * freq], axis=-1)
    cos, sin = jnp.cos(ang), jnp.sin(ang)
    d = (jnp.arange(LANES) - span_start) % period
    inside = d < 2 * half
    p = jnp.where(inside, d % half, 0)
    first = inside & (d < half)
    second = inside & (d >= half)
    ct = jnp.where(inside[None, :], cos[:, p], 1.0)
    sa = jnp.where(first[None, :], -sin[:, p], 0.0)
    sb = jnp.where(second[None, :], sin[:, p], 0.0)
    ident = (jnp.ones((TM, LANES), F32), jnp.zeros((TM, LANES), F32), jnp.zeros((TM, LANES), F32))
    return tuple(jnp.concatenate([i0, t], axis=0) for i0, t in zip(ident, (ct, sa, sb)))


def kernel(x_prompt, x_sample, c, cache_swa_k, cache_swa_v, cache_mla_ckv, cache_mla_krope, state_mlstm_C, state_mlstm_n, state_mlstm_m, c_ctx, ada_w, ada_b, norm1_g, norm2_g, w_in, a_qn_g, a_kn_g, a_sink, b_gate_b, b_hn_g, c_qa_g, c_kva_g, c_wuq, c_wukv, c_qn_g, c_kn_g, w_out, ffn_w1, ffn_w3, ffn_w2, moe_router, moe_w1, moe_w3, moe_w2):
    x = jnp.concatenate([x_prompt.reshape(T_P, D), x_sample.reshape(T_S, D)], axis=0)
    cv = jnp.concatenate([c_ctx[None, :], c, jnp.zeros((5, D), F32)], axis=0)
    mod = _ada_call(cv, ada_w, ada_b)
    mod3 = mod.reshape(DEPTH * 8 * 6, 1, D)

    rope_a = _rope_tables(32, 0, HD)
    rope_c = _rope_tables(16, C_NOPE, LANES)

    w_in_p = jnp.stack([_relayout_w_in(w_in[l]) for l in range(DEPTH)])
    wuq_p = jnp.stack([_pad_heads(c_wuq[l], C_QK) for l in range(DEPTH)]).astype(BF16)
    wukv = c_wukv.reshape(DEPTH, C_KVRANK, C_HEADS, C_NOPE + C_V)
    wk_p = jnp.pad(wukv[..., :C_NOPE], ((0, 0), (0, 0), (0, 0), (0, LANES - C_NOPE)))
    wk_p = wk_p.reshape(DEPTH, C_KVRANK, C_HEADS * LANES).astype(BF16)
    wv_p = wukv[..., C_NOPE:].reshape(DEPTH, C_KVRANK, C_HEADS * C_V).astype(BF16)
    w_out_b = w_out.astype(BF16)
    cache_k4 = cache_swa_k.reshape(NB_S, DEPTH, PAST, A_KV * HD)
    cache_v4 = cache_swa_v.reshape(NB_S, DEPTH, PAST, A_KV * HD)

    news = []
    for l in range(DEPTH):
        gqa = jnp.tile(a_qn_g[l], A_HEADS)[None, :]
        gka = jnp.tile(a_kn_g[l], 2 * A_KV)[None, :]
        gb = _relayout_gate_b(b_gate_b[l])
        gqn = jnp.tile(jnp.pad(c_qn_g[l], (0, LANES - C_QK)), C_HEADS)[None, :]
        gkn = jnp.tile(jnp.pad(c_kn_g[l], (0, LANES - C_QK)), C_HEADS)[None, :]
        (qa, ka, va, kaf, vaf, bq, bk, bv, bo, li, lf, qc, ckv_n, ckr) = _in_call(
            x, mod3, l, norm1_g[l][None, :], w_in_p, gqa, gka, gb, c_qa_g[l][None, :],
            c_kva_g[l][None, :], wuq_p, gqn, rope_a, rope_c)

        ckr_cache = jnp.pad(cache_mla_krope[:, l].reshape(NB_S * PAST, C_ROPE),
                            ((0, 0), (C_NOPE, LANES - C_NOPE - C_ROPE)))
        ckv_all = jnp.concatenate([ckv_n, cache_mla_ckv[:, l].reshape(NB_S * PAST, C_KVRANK)], axis=0)
        ckr_all = jnp.concatenate([ckr, ckr_cache], axis=0)
        kc, vc = _mlakv_call(ckv_all, ckr_all, l, wk_p, wv_p, gkn, rope_c)

        sink = a_sink[l]
        oa_p = _attn_a_prompt_call(sink, qa, ka, va)
        oa_s = _attn_a_sample_call(sink, qa, ka, va, cache_k4, cache_v4, l)
        oc_p = _mla_prompt_call(qc, kc, vc)
        oc_s = _mla_sample_call(qc, kc, vc)

        hn2 = jnp.tile(b_hn_g[l], 2)[None, :]
        ob_p, cn, nn, mn = _mlstm_call(bq, bk, bv, bo, li, lf, hn2, None, l)
        c_st = state_mlstm_C[:, l].reshape(NB_S, 2, 2, 2, HD, HD)
        c_st = jnp.transpose(c_st, (0, 2, 1, 3, 4, 5))
        c0 = jnp.zeros((NB_S, 2, 2, LANES, LANES), F32)
        c0 = c0.at[..., :HD, :HD].set(c_st[:, :, :, 0]).at[..., HD:, HD:].set(c_st[:, :, :, 1])
        c0 = c0.reshape(NB_S * 2, 2, LANES, LANES)
        n_st = state_mlstm_n[:, l].reshape(NB_S, 2, 2, 1, LANES)
        n0 = jnp.transpose(n_st, (0, 2, 1, 3, 4)).reshape(NB_S * 2, 2, 1, LANES)
        m_st = state_mlstm_m[:, l].reshape(NB_S, 2, 2, 2)
        m0 = jnp.transpose(m_st, (0, 2, 1, 3)).reshape(NB_S * 2, 1, 4)
        m0 = jnp.pad(m0, ((0, 0), (0, 0), (0, LANES - 4)))
        ob_s = _mlstm_call(bq, bk, bv, bo, li, lf, hn2, (c0, n0, m0), l)

        oa = jnp.concatenate([oa_p, oa_s], axis=0)
        ob = jnp.concatenate([ob_p, ob_s], axis=0)
        oc = jnp.concatenate([oc_p, oc_s], axis=0)

        if l % 2 == 0:
            x1, xn = _out_call(x, oa, ob, oc, w_out_b, mod3, l, norm2_g[l][None, :], None)
            x = _ffn_call(xn, x1, mod3, l, ffn_w1, ffn_w3, ffn_w2, l // 2)
        else:
            r = _pad_cols(moe_router[l // 2], LANES)
            rh = r.astype(BF16)
            rl = (r - rh.astype(F32)).astype(BF16)
            x1, xn, route, cnt = _out_call(x, oa, ob, oc, w_out_b, mod3, l, norm2_g[l][None, :],
                                           (rh[None], rl[None]))
            x = _moe_layer(xn, x1, route, cnt, mod3, l, moe_w1, moe_w3, moe_w2, l // 2)

        new_k = kaf.reshape(NB_P, S_P, A_KV, HD)
        new_v = vaf.reshape(NB_P, S_P, A_KV, HD)
        new_ckv = ckv_n[:T_P].reshape(NB_P, S_P, C_KVRANK)
        new_kr = ckr[:T_P, C_NOPE:C_NOPE + C_ROPE].reshape(NB_P, S_P, C_ROPE)
        cn6 = cn.reshape(NB_P, 2, 2, LANES, LANES)
        c_e = jnp.stack([cn6[..., :HD, :HD], cn6[..., HD:, HD:]], axis=3)
        new_c = jnp.transpose(c_e, (0, 2, 1, 3, 4, 5)).reshape(NB_P, 2, B_HEADS, HD, HD)
        nn5 = nn.reshape(NB_P, 2, 2, 2, HD)
        new_n = jnp.transpose(nn5, (0, 2, 1, 3, 4)).reshape(NB_P, 2, B_HEADS, HD)
        mn4 = mn.reshape(NB_P, 2, LANES)[:, :, :4].reshape(NB_P, 2, 2, 2)
        new_m = jnp.transpose(mn4, (0, 2, 1, 3)).reshape(NB_P, 2, B_HEADS)
        news.append((new_k, new_v, new_ckv, new_kr, new_c, new_n, new_m))

    y_prompt = x[:T_P].reshape(NB_P, S_P, D)
    y_sample = x[T_P:].reshape(NB_S, S_S, D)
    stacked = tuple(jnp.stack([nw[j] for nw in news], axis=1) for j in range(7))
    return (y_prompt, y_sample) + stacked
```

```python
import functools

import jax
import jax.numpy as jnp
from jax import lax
from jax.experimental import pallas as pl
from jax.experimental.pallas import tpu as pltpu

F32 = jnp.float32
BF16 = jnp.bfloat16

D = 1024
NB_P, S_P = 16, 256
NB_S, S_S = 2, 1024
PAST = 512
DEPTH = 2
T_P = NB_P * S_P
T_S = NB_S * S_S
T = T_P + T_S
TM = 256
NT = T // TM
NT_P = T_P // TM
HD = 64
A_HEADS, A_KV = 6, 2
B_HEADS = 4
CHUNK = 64
C_HEADS = 6
C_QRANK, C_KVRANK, C_NOPE, C_ROPE, C_V = 256, 128, 64, 32, 64
C_QK = C_NOPE + C_ROPE
D_FF = 2816
N_EXP = 8
D_FFE = 3584
EPS = 1e-6
NEG = -1e30
LANES = 128
VMEM_LIMIT = 48 * 1024 * 1024

SEG = dict(QA=(0, 384), KA=(384, 256), VA=(640, 256), BQ=(896, 256), BK=(1152, 256),
           BV=(1408, 256), BO=(1664, 256), GI=(1920, 256), GF=(2176, 256), CQ=(2432, 256),
           CKV=(2688, 128), CKR=(2816, 128))
NP_IN = 2944


def _cparams(sem, vmem=VMEM_LIMIT):
    return pltpu.CompilerParams(dimension_semantics=sem, vmem_limit_bytes=vmem)


def _dot(a, b):
    return jnp.dot(a, b, preferred_element_type=F32)


def _dot_nt(a, b):
    return lax.dot_general(a, b, (((1,), (1,)), ((), ())), preferred_element_type=F32)


def _dot_tn(a, b):
    return lax.dot_general(a, b, (((0,), (0,)), ((), ())), preferred_element_type=F32)


def _lane(shape):
    return lax.broadcasted_iota(jnp.int32, shape, len(shape) - 1)


def _mod_row(i):
    return jnp.where(i < NT_P, 0, 1 + (i - NT_P) // (S_S // TM))


def _rope_blk(i):
    return jnp.where((i >= NT_P) & (i < NT), 1 + (i - NT_P) % (S_S // TM), 0)


def _ada_kernel(cv_ref, w_ref, b_ref, o_ref):
    s = cv_ref[...]
    s = s * jax.nn.sigmoid(s)
    o_ref[0] = _dot(s.astype(BF16), w_ref[0].astype(BF16)) + b_ref[0]


def _ada_call(cv, ada_w, ada_b):
    tn = 1536
    return pl.pallas_call(
        _ada_kernel,
        grid=(DEPTH, 6 * D // tn),
        in_specs=[pl.BlockSpec((8, D), lambda l, j: (0, 0)),
                  pl.BlockSpec((1, D, tn), lambda l, j: (l, 0, j)),
                  pl.BlockSpec((1, 1, tn), lambda l, j: (l, 0, j))],
        out_specs=pl.BlockSpec((1, 8, tn), lambda l, j: (l, 0, j)),
        out_shape=jax.ShapeDtypeStruct((DEPTH, 8, 6 * D), F32),
        compiler_params=_cparams(("arbitrary", "arbitrary")),
        name="ada_mod",
    )(cv, ada_w, ada_b.reshape(DEPTH, 1, 6 * D))


def _half_norm(x, g, n):
    outs = []
    for j in range(x.shape[1] // LANES):
        xj = x[:, j * LANES:(j + 1) * LANES]
        lo = _lane(xj.shape) < HD
        s = xj * xj
        s_lo = jnp.sum(jnp.where(lo, s, 0.0), axis=-1, keepdims=True)
        s_hi = jnp.sum(jnp.where(lo, 0.0, s), axis=-1, keepdims=True)
        r = lax.rsqrt(jnp.where(lo, s_lo, s_hi) * (1.0 / n) + EPS)
        outs.append(xj * r)
    return jnp.concatenate(outs, axis=-1) * g


def _group_norm(x, g, n):
    outs = []
    for j in range(x.shape[1] // LANES):
        xj = x[:, j * LANES:(j + 1) * LANES]
        r = lax.rsqrt(jnp.sum(xj * xj, axis=-1, keepdims=True) * (1.0 / n) + EPS)
        outs.append(xj * r)
    return jnp.concatenate(outs, axis=-1) * g


def _rope(x, c, sa, sb, shift):
    outs = []
    for j in range(x.shape[1] // LANES):
        xj = x[:, j * LANES:(j + 1) * LANES]
        outs.append(xj * c + pltpu.roll(xj, LANES - shift, 1) * sa + pltpu.roll(xj, shift, 1) * sb)
    return jnp.concatenate(outs, axis=-1)


def _in_kernel(x_ref, sh_ref, sc_ref, n1_ref, w_ref, gqa_ref, gka_ref, gb_ref, gcq_ref, gckv_ref,
               wuq_ref, gqn_ref, ra_c, ra_a, ra_b, rc_c, rc_a, rc_b,
               qa_o, ka_o, va_o, kaf_o, vaf_o, bq_o, bk_o, bv_o, bo_o, li_o, lf_o, qc_o, ckv_o, ckr_o):
    i = pl.program_id(0)
    x = x_ref[...]
    xn = x * lax.rsqrt(jnp.mean(x * x, axis=-1, keepdims=True) + EPS) * n1_ref[...]
    xn = xn * (1.0 + sc_ref[0]) + sh_ref[0]
    p = _dot(xn.astype(BF16), w_ref[0])

    def seg(name):
        o, n = SEG[name]
        return p[:, o:o + n]

    qa = _half_norm(seg("QA"), gqa_ref[...], HD)
    ka = _half_norm(seg("KA"), gka_ref[...], HD)
    va = seg("VA")

    @pl.when(i < NT_P)
    def _():
        kaf_o[...] = jnp.concatenate([ka[:, 0:HD], ka[:, LANES:LANES + HD]], axis=-1)
        vaf_o[...] = jnp.concatenate([va[:, 0:HD], va[:, LANES:LANES + HD]], axis=-1)

    qa_o[...] = _rope(qa, ra_c[...], ra_a[...], ra_b[...], 32).astype(BF16)
    ka_o[...] = _rope(ka, ra_c[...], ra_a[...], ra_b[...], 32).astype(BF16)
    va_o[...] = va.astype(BF16)

    bq_o[...] = seg("BQ").astype(BF16)
    bk_o[...] = (seg("BK") * (HD ** -0.5)).astype(BF16)
    bv_o[...] = seg("BV").astype(BF16)
    bo_o[...] = seg("BO")
    gb = gb_ref[...]
    li_o[...] = seg("GI") + gb[:, 0:256]
    lf_o[...] = jax.nn.log_sigmoid(seg("GF") + gb[:, 256:512])

    cq = seg("CQ")
    cqn = cq * lax.rsqrt(jnp.mean(cq * cq, axis=-1, keepdims=True) + EPS) * gcq_ref[...]
    qc = _group_norm(_dot(cqn.astype(BF16), wuq_ref[0]), gqn_ref[...], C_QK)
    qc_o[...] = _rope(qc, rc_c[...], rc_a[...], rc_b[...], 16).astype(BF16)

    ckv = seg("CKV")
    ckv_o[...] = ckv * lax.rsqrt(jnp.mean(ckv * ckv, axis=-1, keepdims=True) + EPS) * gckv_ref[...]
    ckr_o[...] = seg("CKR")


def _in_call(x, mod3, l, n1g, w_in_p, gqa, gka, gb, gcq, gckv, wuq_p, gqn, rope_a, rope_c):
    row = lambda k: pl.BlockSpec((1, 1, D), lambda i: ((l * 8 + _mod_row(i)) * 6 + k, 0, 0))
    vec = lambda n: pl.BlockSpec((1, n), lambda i: (0, 0))
    tab = pl.BlockSpec((TM, LANES), lambda i: (_rope_blk(i), 0))
    tok = lambda n: pl.BlockSpec((TM, n), lambda i: (i, 0))
    tokp = pl.BlockSpec((TM, LANES), lambda i: (jnp.minimum(i, NT_P - 1), 0))
    o = lambda n, dt: jax.ShapeDtypeStruct((T, n), dt)
    return pl.pallas_call(
        _in_kernel,
        grid=(NT,),
        in_specs=[tok(D), row(0), row(1), vec(D),
                  pl.BlockSpec((1, D, NP_IN), lambda i: (l, 0, 0)),
                  vec(384), vec(256), vec(512), vec(256), vec(128),
                  pl.BlockSpec((1, C_QRANK, 768), lambda i: (l, 0, 0)), vec(768),
                  tab, tab, tab, tab, tab, tab],
        out_specs=[tok(384), tok(256), tok(256), tokp, tokp, tok(256), tok(256), tok(256), tok(256),
                   tok(256), tok(256), tok(768), tok(128), tok(128)],
        out_shape=[o(384, BF16), o(256, BF16), o(256, BF16),
                   jax.ShapeDtypeStruct((T_P, LANES), F32), jax.ShapeDtypeStruct((T_P, LANES), F32),
                   o(256, BF16), o(256, BF16), o(256, BF16), o(256, F32),
                   o(256, F32), o(256, F32), o(768, BF16), o(128, F32), o(128, F32)],
        compiler_params=_cparams(("arbitrary",)),
        name=f"in_proj_l{l}",
    )(x, mod3, mod3, n1g, w_in_p, gqa, gka, gb, gcq, gckv, wuq_p, gqn, *rope_a, *rope_c)


def _mlakv_kernel(ckv_ref, ckr_ref, wk_ref, wv_ref, gkn_ref, rc_c, rc_a, rc_b, k_o, v_o):
    c = ckv_ref[...].astype(BF16)
    k = _dot(c, wk_ref[0])
    kr = ckr_ref[...]
    k = k + jnp.concatenate([kr] * C_HEADS, axis=-1)
    k = _group_norm(k, gkn_ref[...], C_QK)
    k_o[...] = _rope(k, rc_c[...], rc_a[...], rc_b[...], 16).astype(BF16)
    v_o[...] = _dot(c, wv_ref[0]).astype(BF16)


def _mlakv_call(ckv_all, ckr_all, l, wk_p, wv_p, gkn, rope_c):
    r = ckv_all.shape[0]
    tab = pl.BlockSpec((TM, LANES), lambda i: (_rope_blk(i), 0))
    tok = lambda n: pl.BlockSpec((TM, n), lambda i: (i, 0))
    return pl.pallas_call(
        _mlakv_kernel,
        grid=(r // TM,),
        in_specs=[tok(128), tok(128),
                  pl.BlockSpec((1, C_KVRANK, 768), lambda i: (l, 0, 0)),
                  pl.BlockSpec((1, C_KVRANK, 384), lambda i: (l, 0, 0)),
                  pl.BlockSpec((1, 768), lambda i: (0, 0)), tab, tab, tab],
        out_specs=[tok(768), tok(384)],
        out_shape=[jax.ShapeDtypeStruct((r, 768), BF16), jax.ShapeDtypeStruct((r, 384), BF16)],
        compiler_params=_cparams(("arbitrary",)),
        name=f"mla_kv_l{l}",
    )(ckv_all, ckr_all, wk_p, wv_p, gkn, *rope_c)


def _softmax_pv(scores, values, sink):
    m = scores[0].max(axis=-1, keepdims=True)
    for s in scores[1:]:
        m = jnp.maximum(m, s.max(axis=-1, keepdims=True))
    if sink is not None:
        m = jnp.maximum(m, sink)
    es = [jnp.exp(s - m) for s in scores]
    den = es[0].sum(axis=-1, keepdims=True)
    for e in es[1:]:
        den = den + e.sum(axis=-1, keepdims=True)
    if sink is not None:
        den = den + jnp.exp(sink - m)
    inv = 1.0 / den
    out = None
    for e, v in zip(es, values):
        o = _dot((e * inv).astype(BF16), v)
        out = o if out is None else out + o
    return out


def _half(x, hi):
    lo = _lane(x.shape) < HD
    return jnp.where(lo != hi, x, jnp.zeros_like(x))


def _attn_a_prompt_kernel(sink_ref, q_ref, k_ref, v_ref, o_ref):
    q = q_ref[...]
    k = k_ref[...]
    v = v_ref[...]
    for j in range(A_HEADS // 2):
        acc = None
        for c in range(2):
            h = 2 * j + c
            g = h // (A_HEADS // A_KV)
            qh = _half(q[:, j * LANES:(j + 1) * LANES], c == 1)
            s = _dot_nt(qh, k[:, g * LANES:(g + 1) * LANES]) * (HD ** -0.5)
            vh = _half(v[:, g * LANES:(g + 1) * LANES], c == 1)
            o = _softmax_pv([s], [vh], sink_ref[h])
            acc = o if acc is None else acc + o
        o_ref[:, j * LANES:(j + 1) * LANES] = acc.astype(BF16)


def _attn_a_prompt_call(sink, qa, ka, va):
    return pl.pallas_call(
        _attn_a_prompt_kernel,
        grid_spec=pltpu.PrefetchScalarGridSpec(
            num_scalar_prefetch=1, grid=(NB_P,),
            in_specs=[pl.BlockSpec((S_P, 384), lambda b, s: (b, 0)),
                      pl.BlockSpec((S_P, 256), lambda b, s: (b, 0)),
                      pl.BlockSpec((S_P, 256), lambda b, s: (b, 0))],
            out_specs=pl.BlockSpec((S_P, 384), lambda b, s: (b, 0))),
        out_shape=jax.ShapeDtypeStruct((T_P, 384), BF16),
        compiler_params=_cparams(("arbitrary",)),
        name="attn_a_prompt",
    )(sink, qa, ka, va)


QB = 128
WIN = 128
BAND = QB + 2 * WIN


def _attn_a_sample_kernel(sink_ref, q_ref, k_ref, v_ref, kc_ref, vc_ref, o_ref):
    n = pl.program_id(1)
    ws = pl.multiple_of(jnp.clip(n * QB - WIN, 0, S_S - BAND), QB)
    q = q_ref[...]
    kb = k_ref[pl.ds(ws, BAND), :]
    vb = v_ref[pl.ds(ws, BAND), :]
    kc = kc_ref[0, 0].astype(BF16)
    vc = vc_ref[0, 0].astype(BF16)
    qpos = n * QB + lax.broadcasted_iota(jnp.int32, (QB, BAND), 0)
    kpos = ws + lax.broadcasted_iota(jnp.int32, (QB, BAND), 1)
    ok = jnp.abs(qpos - kpos) <= WIN
    lane = _lane((PAST, LANES))
    for j in range(A_HEADS // 2):
        acc = None
        for c in range(2):
            h = 2 * j + c
            g = h // (A_HEADS // A_KV)
            qh = _half(q[:, j * LANES:(j + 1) * LANES], c == 1)
            s_b = jnp.where(ok, _dot_nt(qh, kb[:, g * LANES:(g + 1) * LANES]) * (HD ** -0.5), NEG)
            kcg = jnp.where((lane < HD) == (g == 0), kc, jnp.zeros_like(kc))
            vcg = jnp.where((lane < HD) == (g == 0), vc, jnp.zeros_like(vc))
            if (g == 1) != (c == 1):
                kcg = pltpu.roll(kcg.astype(F32), HD, 1).astype(BF16)
                vcg = pltpu.roll(vcg.astype(F32), HD, 1).astype(BF16)
            s_c = _dot_nt(qh, kcg) * (HD ** -0.5)
            vh = _half(vb[:, g * LANES:(g + 1) * LANES], c == 1)
            o = _softmax_pv([s_b, s_c], [vh, vcg], sink_ref[h])
            acc = o if acc is None else acc + o
        o_ref[:, j * LANES:(j + 1) * LANES] = acc.astype(BF16)


def _attn_a_sample_call(sink, qa, ka, va, cache_k, cache_v, l):
    nqb = S_S // QB
    off = T_P // S_S
    return pl.pallas_call(
        _attn_a_sample_kernel,
        grid_spec=pltpu.PrefetchScalarGridSpec(
            num_scalar_prefetch=1, grid=(NB_S, nqb),
            in_specs=[pl.BlockSpec((QB, 384), lambda b, n, s: (T_P // QB + b * nqb + n, 0)),
                      pl.BlockSpec((S_S, 256), lambda b, n, s: (off + b, 0)),
                      pl.BlockSpec((S_S, 256), lambda b, n, s: (off + b, 0)),
                      pl.BlockSpec((1, 1, PAST, LANES), lambda b, n, s: (b, l, 0, 0)),
                      pl.BlockSpec((1, 1, PAST, LANES), lambda b, n, s: (b, l, 0, 0))],
            out_specs=pl.BlockSpec((QB, 384), lambda b, n, s: (b * nqb + n, 0))),
        out_shape=jax.ShapeDtypeStruct((T_S, 384), BF16),
        compiler_params=_cparams(("arbitrary", "arbitrary")),
        name=f"attn_a_sample_l{l}",
    )(sink, qa, ka, va, cache_k, cache_v)


def _mla_heads(q, ks, vs, o_ref):
    for j in range(C_HEADS // 2):
        acc = None
        for c in range(2):
            h = 2 * j + c
            qh = q[:, h * LANES:(h + 1) * LANES]
            scores = [_dot_nt(qh, k[:, h * LANES:(h + 1) * LANES]) * (C_QK ** -0.5) for k in ks]
            vals = [_half(v[:, j * LANES:(j + 1) * LANES], c == 1) for v in vs]
            o = _softmax_pv(scores, vals, None)
            acc = o if acc is None else acc + o
        o_ref[:, j * LANES:(j + 1) * LANES] = acc.astype(BF16)


def _mla_prompt_kernel(q_ref, k_ref, v_ref, o_ref):
    _mla_heads(q_ref[...], [k_ref[...]], [v_ref[...]], o_ref)


def _mla_prompt_call(qc, kc, vc):
    return pl.pallas_call(
        _mla_prompt_kernel,
        grid=(NB_P,),
        in_specs=[pl.BlockSpec((S_P, 768), lambda b: (b, 0)),
                  pl.BlockSpec((S_P, 768), lambda b: (b, 0)),
                  pl.BlockSpec((S_P, 384), lambda b: (b, 0))],
        out_specs=pl.BlockSpec((S_P, 384), lambda b: (b, 0)),
        out_shape=jax.ShapeDtypeStruct((T_P, 384), BF16),
        compiler_params=_cparams(("arbitrary",)),
        name="mla_prompt",
    )(qc, kc, vc)


def _mla_sample_kernel(q_ref, kc_ref, vc_ref, kl_ref, vl_ref, o_ref):
    _mla_heads(q_ref[...], [kc_ref[...], kl_ref[...]], [vc_ref[...], vl_ref[...]], o_ref)


def _mla_sample_call(qc, kc, vc):
    tq = 256
    nq = S_S // tq
    return pl.pallas_call(
        _mla_sample_kernel,
        grid=(NB_S, nq),
        in_specs=[pl.BlockSpec((tq, 768), lambda b, n: (T_P // tq + b * nq + n, 0)),
                  pl.BlockSpec((PAST, 768), lambda b, n: (T // PAST + b, 0)),
                  pl.BlockSpec((PAST, 384), lambda b, n: (T // PAST + b, 0)),
                  pl.BlockSpec((S_S, 768), lambda b, n: (T_P // S_S + b, 0)),
                  pl.BlockSpec((S_S, 384), lambda b, n: (T_P // S_S + b, 0))],
        out_specs=pl.BlockSpec((tq, 384), lambda b, n: (b * nq + n, 0)),
        out_shape=jax.ShapeDtypeStruct((T_S, 384), BF16),
        compiler_params=_cparams(("arbitrary", "arbitrary")),
        name="mla_sample",
    )(qc, kc, vc, kc, vc)


def _split3(x):
    x1 = x.astype(BF16)
    r = x - x1.astype(F32)
    x2 = r.astype(BF16)
    x3 = (r - x2.astype(F32)).astype(BF16)
    return x1, x2, x3


def _exact_dot(a01, x):
    x1, x2, x3 = _split3(x)
    return _dot(a01, x1) + _dot(a01, x2) + _dot(a01, x3)


def _col(x, k):
    return x[:, k:k + 1]


def _mlstm_kernel_serial(nc, has_state, *refs):
    if has_state:
        (q_ref, k_ref, v_ref, bo_ref, li_ref, lf_ref, hn_ref, c0_ref, n0_ref, m0_ref,
         ob_ref, h_scr) = refs
    else:
        (q_ref, k_ref, v_ref, bo_ref, li_ref, lf_ref, hn_ref,
         ob_ref, cn_ref, nn_ref, mn_ref, h_scr) = refs
    s_len = nc * CHUNK
    li = li_ref[...]
    lf = lf_ref[...]
    row = lax.broadcasted_iota(jnp.int32, (CHUNK, CHUNK), 0)
    colv = lax.broadcasted_iota(jnp.int32, (CHUNK, CHUNK), 1)
    tri_f = (colv <= row)
    tri_b = (colv >= row)
    tri_f01 = tri_f.astype(BF16)
    tri_b01 = tri_b.astype(BF16)
    lane1 = _lane((1, LANES))
    fwd_lane = lane1 < 2

    bcs, bls, ggs, mgs = [], [], [], []
    for c in range(nc):
        lc = lf[c * CHUNK:(c + 1) * CHUNK]
        bc = jnp.where(fwd_lane, _exact_dot(tri_f01, lc), _exact_dot(tri_b01, lc))
        bl = jnp.where(fwd_lane, bc[CHUNK - 1:CHUNK], bc[0:1])
        gg = bl - bc + li[c * CHUNK:(c + 1) * CHUNK]
        bcs.append(bc)
        bls.append(bl)
        ggs.append(gg)
        mgs.append(gg.max(axis=0, keepdims=True))

    m0 = m0_ref[0] if has_state else jnp.zeros((1, LANES), F32)
    m_prev = [None] * nc
    m_next = [None] * nc
    mf = m0
    mb = m0
    mf_prev, mf_next, mb_prev, mb_next = {}, {}, {}, {}
    for i in range(nc):
        mf_prev[i] = mf
        mf = jnp.maximum(bls[i] + mf, mgs[i])
        mf_next[i] = mf
        cb = nc - 1 - i
        mb_prev[cb] = mb
        mb = jnp.maximum(bls[cb] + mb, mgs[cb])
        mb_next[cb] = mb
    for c in range(nc):
        m_prev[c] = jnp.where(fwd_lane, mf_prev[c], mb_prev[c])
        m_next[c] = jnp.where(fwd_lane, mf_next[c], mb_next[c])

    bc_all = jnp.concatenate(bcs, axis=0)
    ut = (li - bc_all).T

    lane2 = _lane((CHUNK, LANES))
    lo2 = lane2 < HD
    rr = lax.broadcasted_iota(jnp.int32, (LANES, LANES), 0)
    cc = lax.broadcasted_iota(jnp.int32, (LANES, LANES), 1)
    blockdiag = (rr < HD) == (cc < HD)

    for d in range(2):
        if has_state:
            cst = c0_ref[0, d]
            nst = n0_ref[0, d]
        else:
            cst = jnp.zeros((LANES, LANES), F32)
            nst = jnp.zeros((1, LANES), F32)
        order = range(nc) if d == 0 else range(nc - 1, -1, -1)
        causal = tri_f if d == 0 else tri_b
        for c in order:
            rows = slice(c * CHUNK, (c + 1) * CHUNK)
            q2 = q_ref[rows, :]
            k2 = k_ref[rows, :]
            v2 = v_ref[rows, :]
            qf = q2.astype(F32)
            qc_all = _dot(q2, cst.astype(BF16))
            qn_all = qf * nst
            num = None
            a_cols, dn_cols, ws_cols, dec = [], [], [], []
            for e in range(2):
                kk = 2 * d + e
                bc_col = _col(bcs[c], kk)
                d_mat = jnp.where(causal, bc_col + ut[kk:kk + 1, rows], NEG)
                mp = _col(m_prev[c], kk)
                mn = _col(m_next[c], kk)
                inter = bc_col + mp
                mt = jnp.maximum(inter, d_mat.max(axis=-1, keepdims=True))
                a = jnp.exp(inter - mt)
                qe = _half(q2, e == 1)
                w = jnp.exp(d_mat - mt) * _dot_nt(qe, k2)
                o = _dot(w.astype(BF16), _half(v2, e == 1))
                num = o if num is None else num + o
                qn = jnp.sum(jnp.where(lo2 != (e == 1), qn_all, 0.0), axis=-1, keepdims=True)
                den = a * qn + w.sum(axis=-1, keepdims=True)
                a_cols.append(a)
                dn_cols.append(jnp.maximum(jnp.abs(den), jnp.exp(-mt)))
                ws_cols.append(jnp.exp(_col(ggs[c], kk) - mn))
                dec.append(jnp.exp(_col(bls[c], kk) + mp - mn))
            a2 = jnp.where(lo2, a_cols[0], a_cols[1])
            dn2 = jnp.where(lo2, dn_cols[0], dn_cols[1])
            h2 = (a2 * qc_all + num) / dn2
            if d == 0:
                h_scr[rows, :] = h2
            else:
                h_scr[rows, :] = h_scr[rows, :] + h2
            kw = k2.astype(F32) * jnp.where(lo2, ws_cols[0], ws_cols[1])
            dec2 = jnp.where(lane1 < HD, dec[0], dec[1])
            u = _dot_tn(kw.astype(BF16), v2)
            cst = dec2 * cst + jnp.where(blockdiag, u, 0.0)
            nst = dec2 * nst + kw.sum(axis=0, keepdims=True)
        if not has_state:
            cn_ref[0, d] = cst
            nn_ref[0, d] = nst
    if not has_state:
        mn_ref[0] = jnp.where(fwd_lane, mf, mb)

    hs = h_scr[...]
    lo = _lane(hs.shape) < HD
    s = hs * hs
    s_lo = jnp.sum(jnp.where(lo, s, 0.0), axis=-1, keepdims=True)
    s_hi = jnp.sum(jnp.where(lo, 0.0, s), axis=-1, keepdims=True)
    r = lax.rsqrt(jnp.where(lo, s_lo, s_hi) * (1.0 / HD) + EPS)
    ob_ref[...] = (hs * r * hn_ref[...] * jax.nn.sigmoid(bo_ref[...])).astype(BF16)


def _bmm(a, b):
    return lax.dot_general(a, b, (((2,), (1,)), ((0,), (0,))), preferred_element_type=F32)


def _bmm_nt(a, b):
    return lax.dot_general(a, b, (((2,), (2,)), ((0,), (0,))), preferred_element_type=F32)


def _bmm_tn(a, b):
    return lax.dot_general(a, b, (((1,), (1,)), ((0,), (0,))), preferred_element_type=F32)


def _mlstm_kernel(nc, has_state, *refs):
    if has_state:
        (q_ref, k_ref, v_ref, bo_ref, li_ref, lf_ref, hn_ref, c0_ref, n0_ref, m0_ref, ob_ref) = refs
    else:
        (q_ref, k_ref, v_ref, bo_ref, li_ref, lf_ref, hn_ref, ob_ref, cn_ref, nn_ref, mn_ref) = refs
    s_len = nc * CHUNK
    c3 = lambda x: x.reshape(nc, CHUNK, x.shape[-1])
    li3 = c3(li_ref[...])
    lf3 = c3(lf_ref[...])
    row = lax.broadcasted_iota(jnp.int32, (CHUNK, CHUNK), 0)
    colv = lax.broadcasted_iota(jnp.int32, (CHUNK, CHUNK), 1)
    tri_f = colv <= row
    tri_b = colv >= row
    bcast = lambda m: jnp.broadcast_to(m[None], (nc, CHUNK, CHUNK))
    lane1 = _lane((1, LANES))
    fwd_lane = lane1 < 2

    lf_parts = _split3(lf3)
    tf3 = bcast(tri_f.astype(BF16))
    tb3 = bcast(tri_b.astype(BF16))
    bc_f = _bmm(tf3, lf_parts[0]) + _bmm(tf3, lf_parts[1]) + _bmm(tf3, lf_parts[2])
    bc_b = _bmm(tb3, lf_parts[0]) + _bmm(tb3, lf_parts[1]) + _bmm(tb3, lf_parts[2])
    bc3 = jnp.where(fwd_lane, bc_f, bc_b)
    row3 = lax.broadcasted_iota(jnp.int32, (nc, CHUNK, LANES), 1)
    edge = jnp.where(row3 == jnp.where(fwd_lane, CHUNK - 1, 0), bc3, 0.0)
    bl3 = jnp.sum(edge, axis=1, keepdims=True)
    gg3 = bl3 - bc3 + li3
    bl2 = jnp.sum(edge, axis=1)
    mg2 = gg3.max(axis=1)

    m0 = m0_ref[0] if has_state else jnp.zeros((1, LANES), F32)
    mf = m0
    mb = m0
    mf_prev, mf_next, mb_prev, mb_next = {}, {}, {}, {}
    for i in range(nc):
        mf_prev[i] = mf
        mf = jnp.maximum(bl2[i:i + 1] + mf, mg2[i:i + 1])
        mf_next[i] = mf
        cb = nc - 1 - i
        mb_prev[cb] = mb
        mb = jnp.maximum(bl2[cb:cb + 1] + mb, mg2[cb:cb + 1])
        mb_next[cb] = mb
    m_prev = [jnp.where(fwd_lane, mf_prev[c], mb_prev[c]) for c in range(nc)]
    m_next = [jnp.where(fwd_lane, mf_next[c], mb_next[c]) for c in range(nc)]
    m_prev3 = jnp.stack(m_prev)
    m_next3 = jnp.stack(m_next)
    dec2d = jnp.exp(bl2 + jnp.concatenate(m_prev, axis=0) - jnp.concatenate(m_next, axis=0))
    if not has_state:
        mn_ref[0] = jnp.where(fwd_lane, mf, mb)

    u_parts = _split3(li3 - bc3)
    ws3 = jnp.exp(gg3 - m_next3)
    inter3 = bc3 + m_prev3

    q3 = c3(q_ref[...])
    k3 = c3(k_ref[...])
    v3 = c3(v_ref[...])
    q3f = q3.astype(F32)
    k3f = k3.astype(F32)
    lane3 = _lane((nc, CHUNK, LANES))
    lo3 = lane3 < HD
    rr = lax.broadcasted_iota(jnp.int32, (LANES, LANES), 0)
    cc = lax.broadcasted_iota(jnp.int32, (LANES, LANES), 1)
    blockdiag = (rr < HD) == (cc < HD)
    col = lambda x, k: x[:, :, k:k + 1]

    hsum = None
    for d in range(2):
        causal3 = bcast(tri_f if d == 0 else tri_b)
        a_c, ws_sum, mt_c, ws_c, intra = [], [], [], [], None
        for e in range(2):
            kk = 2 * d + e
            pick = jnp.broadcast_to(jnp.where(_lane((CHUNK, LANES)) == kk, 1.0, 0.0).astype(BF16)[None],
                                    (nc, CHUNK, LANES))
            ub = _bmm_nt(pick, u_parts[0]) + _bmm_nt(pick, u_parts[1]) + _bmm_nt(pick, u_parts[2])
            bc_col = col(bc3, kk)
            d_mat = jnp.where(causal3, bc_col + ub, NEG)
            inter = col(inter3, kk)
            mt = jnp.maximum(inter, d_mat.max(axis=-1, keepdims=True))
            a = jnp.exp(inter - mt)
            w = jnp.exp(d_mat - mt) * _bmm_nt(_half(q3, e == 1), k3)
            o = _bmm(w.astype(BF16), _half(v3, e == 1))
            intra = o if intra is None else intra + o
            a_c.append(a)
            mt_c.append(mt)
            ws_sum.append(w.sum(axis=-1, keepdims=True))
            ws_c.append(col(ws3, kk))
        kw3 = k3f * jnp.where(lo3, ws_c[0], ws_c[1])
        dec2 = jnp.where(_lane((nc, LANES)) < HD, dec2d[:, 2 * d:2 * d + 1], dec2d[:, 2 * d + 1:2 * d + 2])
        u_all = jnp.where(blockdiag, _bmm_tn(kw3.astype(BF16), v3), 0.0)
        kwsum = kw3.sum(axis=1)
        if has_state:
            cst = c0_ref[0, d]
            nst = n0_ref[0, d]
        else:
            cst = jnp.zeros((LANES, LANES), F32)
            nst = jnp.zeros((1, LANES), F32)
        cs, ns = [None] * nc, [None] * nc
        for c in (range(nc) if d == 0 else range(nc - 1, -1, -1)):
            cs[c] = cst
            ns[c] = nst
            cst = dec2[c:c + 1] * cst + u_all[c]
            nst = dec2[c:c + 1] * nst + kwsum[c:c + 1]
        if not has_state:
            cn_ref[0, d] = cst
            nn_ref[0, d] = nst
        qc = _bmm(q3, jnp.stack(cs).astype(BF16))
        qn_all = q3f * jnp.stack(ns)
        dn = []
        for e in range(2):
            qn = jnp.sum(jnp.where(lo3 != (e == 1), qn_all, 0.0), axis=-1, keepdims=True)
            den = a_c[e] * qn + ws_sum[e]
            dn.append(jnp.maximum(jnp.abs(den), jnp.exp(-mt_c[e])))
        h2 = (jnp.where(lo3, a_c[0], a_c[1]) * qc + intra) / jnp.where(lo3, dn[0], dn[1])
        hsum = h2 if hsum is None else hsum + h2

    hs = hsum.reshape(s_len, LANES)
    lo = _lane(hs.shape) < HD
    s = hs * hs
    s_lo = jnp.sum(jnp.where(lo, s, 0.0), axis=-1, keepdims=True)
    s_hi = jnp.sum(jnp.where(lo, 0.0, s), axis=-1, keepdims=True)
    r = lax.rsqrt(jnp.where(lo, s_lo, s_hi) * (1.0 / HD) + EPS)
    ob_ref[...] = (hs * r * hn_ref[...] * jax.nn.sigmoid(bo_ref[...])).astype(BF16)


def _mlstm_call(bq, bk, bv, bo, li, lf, hn2, state, l):
    has_state = state is not None
    if has_state:
        nb, s_len, base = NB_S, S_S, T_P // S_S
    else:
        nb, s_len, base = NB_P, S_P, 0
    nc = s_len // CHUNK
    tokp = pl.BlockSpec((s_len, LANES), lambda b, j: (base + b, j))
    in_specs = [tokp, tokp, tokp, tokp, tokp, tokp, pl.BlockSpec((1, LANES), lambda b, j: (0, 0))]
    args = [bq, bk, bv, bo, li, lf, hn2]
    ob_spec = pl.BlockSpec((s_len, LANES), lambda b, j: (b, j))
    ob_shape = jax.ShapeDtypeStruct((nb * s_len, 256), BF16)
    if has_state:
        c0, n0, m0 = state
        in_specs += [pl.BlockSpec((1, 2, LANES, LANES), lambda b, j: (b * 2 + j, 0, 0, 0)),
                     pl.BlockSpec((1, 2, 1, LANES), lambda b, j: (b * 2 + j, 0, 0, 0)),
                     pl.BlockSpec((1, 1, LANES), lambda b, j: (b * 2 + j, 0, 0))]
        args += [c0, n0, m0]
        out_specs = ob_spec
        out_shape = ob_shape
    else:
        out_specs = [ob_spec,
                     pl.BlockSpec((1, 2, LANES, LANES), lambda b, j: (b * 2 + j, 0, 0, 0)),
                     pl.BlockSpec((1, 2, 1, LANES), lambda b, j: (b * 2 + j, 0, 0, 0)),
                     pl.BlockSpec((1, 1, LANES), lambda b, j: (b * 2 + j, 0, 0))]
        out_shape = [ob_shape,
                     jax.ShapeDtypeStruct((nb * 2, 2, LANES, LANES), F32),
                     jax.ShapeDtypeStruct((nb * 2, 2, 1, LANES), F32),
                     jax.ShapeDtypeStruct((nb * 2, 1, LANES), F32)]
    return pl.pallas_call(
        functools.partial(_mlstm_kernel, nc, has_state),
        grid=(nb, 2),
        in_specs=in_specs,
        out_specs=out_specs,
        out_shape=out_shape,
        compiler_params=_cparams(("arbitrary", "arbitrary")),
        name=f"mlstm_{'sample' if has_state else 'prompt'}_l{l}",
    )(*args)


def _out_kernel(moe, x_ref, oa_ref, ob_ref, oc_ref, w_ref, g1_ref, sh_ref, sc_ref, n2_ref, *rest):
    if moe:
        rh_ref, rl_ref, x1_o, xn_o, route_o, cnt_o, run_scr = rest
    else:
        x1_o, xn_o = rest
    o = (_dot(oa_ref[...], w_ref[0, 0:384, :]) + _dot(ob_ref[...], w_ref[0, 384:640, :])
         + _dot(oc_ref[...], w_ref[0, 640:1024, :]))
    x1 = x_ref[...] + g1_ref[0] * o
    x1_o[...] = x1
    xn = x1 * lax.rsqrt(jnp.mean(x1 * x1, axis=-1, keepdims=True) + EPS) * n2_ref[...]
    xn = xn * (1.0 + sc_ref[0]) + sh_ref[0]
    xb = xn.astype(BF16)
    if not moe:
        xn_o[...] = xb
    else:
        xn_o[...] = xn
        xl = (xn - xb.astype(F32)).astype(BF16)
        logits = _dot(xb, rh_ref[0]) + (_dot(xl, rh_ref[0]) + _dot(xb, rl_ref[0]))
        lane = _lane(logits.shape)
        logits = jnp.where(lane < N_EXP, logits, -jnp.inf)
        m1 = logits.max(axis=-1, keepdims=True)
        i1 = jnp.min(jnp.where(logits == m1, lane, LANES), axis=-1, keepdims=True)
        rest_l = jnp.where(lane == i1, -jnp.inf, logits)
        m2 = rest_l.max(axis=-1, keepdims=True)
        i2 = jnp.min(jnp.where(rest_l == m2, lane, LANES), axis=-1, keepdims=True)
        e2 = jnp.exp(m2 - m1)
        den = 1.0 + e2
        w1 = 1.0 / den
        w2 = e2 / den

        @pl.when(pl.program_id(0) == 0)
        def _():
            run_scr[...] = jnp.zeros(run_scr.shape, F32)

        sel = jnp.where(lane == i1, 1.0, jnp.where((lane == i2) & (w2 > 0.0), 1.0, 0.0))
        before = (lax.broadcasted_iota(jnp.int32, (TM, TM), 1)
                  < lax.broadcasted_iota(jnp.int32, (TM, TM), 0)).astype(BF16)
        rank = run_scr[...] + _dot(before, sel.astype(BF16))
        r1 = jnp.sum(jnp.where(lane == i1, rank, 0.0), axis=-1, keepdims=True)
        r2 = jnp.sum(jnp.where(lane == i2, rank, 0.0), axis=-1, keepdims=True)
        run_scr[...] += jnp.sum(sel, axis=0, keepdims=True)
        cnt_o[...] = run_scr[...]
        fields = [i1.astype(F32), i2.astype(F32), r1, r2, w1, w2]
        info = jnp.zeros(lane.shape, F32)
        for k, v in enumerate(fields):
            info = jnp.where(lane == k, v, info)
        route_o[...] = info.T


def _out_call(x, oa, ob, oc, w_out_b, mod3, l, n2g, router):
    moe = router is not None
    row = lambda k: pl.BlockSpec((1, 1, D), lambda i: ((l * 8 + _mod_row(i)) * 6 + k, 0, 0))
    tok = lambda n: pl.BlockSpec((TM, n), lambda i: (i, 0))
    in_specs = [tok(D), tok(384), tok(256), tok(384),
                pl.BlockSpec((1, D, D), lambda i: (l, 0, 0)),
                row(2), row(3), row(4), pl.BlockSpec((1, D), lambda i: (0, 0))]
    args = [x, oa, ob, oc, w_out_b, mod3, mod3, mod3, n2g]
    out_specs = [tok(D), tok(D)]
    out_shape = [jax.ShapeDtypeStruct((T, D), F32), jax.ShapeDtypeStruct((T, D), BF16)]
    if moe:
        out_shape[1] = jax.ShapeDtypeStruct((T, D), F32)
        rh, rl = router
        in_specs += [pl.BlockSpec((1, D, LANES), lambda i: (0, 0, 0))] * 2
        args += [rh, rl]
        out_specs += [pl.BlockSpec((LANES, TM), lambda i: (0, i)), pl.BlockSpec((1, LANES), lambda i: (0, 0))]
        out_shape += [jax.ShapeDtypeStruct((LANES, T), F32), jax.ShapeDtypeStruct((1, LANES), F32)]
    return pl.pallas_call(
        functools.partial(_out_kernel, moe),
        grid=(NT,),
        in_specs=in_specs, out_specs=out_specs, out_shape=out_shape,
        scratch_shapes=[pltpu.VMEM((1, LANES), F32)] if moe else [],
        compiler_params=_cparams(("arbitrary",)),
        name=f"out_proj_l{l}",
    )(*args)


FFN_TM = 1024


def _g2_spec(l):
    def idx(i, *_):
        r = jnp.where(i < T_P // FFN_TM, 0, 1 + (i - T_P // FFN_TM) // (S_S // FFN_TM))
        return ((l * 8 + r) * 6 + 5, 0, 0)
    return pl.BlockSpec((1, 1, D), idx)


def _swiglu_part(x, w1_ref, w3_ref, w2_ref):
    a = _dot(x, w1_ref[...].astype(BF16))
    b = _dot(x, w3_ref[...].astype(BF16))
    h = (a * jax.nn.sigmoid(a) * b).astype(BF16)
    return _dot(h, w2_ref[...].astype(BF16))


def _ffn_kernel(xn_ref, x1_ref, g2_ref, w1_ref, w3_ref, w2_ref, y_ref, acc):
    f = pl.program_id(1)
    part = _swiglu_part(xn_ref[...], w1_ref.at[0], w3_ref.at[0], w2_ref.at[0])

    @pl.when(f == 0)
    def _():
        acc[...] = part

    @pl.when(f > 0)
    def _():
        acc[...] += part

    @pl.when(f == pl.num_programs(1) - 1)
    def _():
        y_ref[...] = x1_ref[...] + g2_ref[0] * acc[...]


def _ffn_call(xn, x1, mod3, l, w1, w3, w2, i_layer):
    tf = 256
    return pl.pallas_call(
        _ffn_kernel,
        grid=(T // FFN_TM, D_FF // tf),
        in_specs=[pl.BlockSpec((FFN_TM, D), lambda i, f: (i, 0)),
                  pl.BlockSpec((FFN_TM, D), lambda i, f: (i, 0)),
                  _g2_spec(l),
                  pl.BlockSpec((1, D, tf), lambda i, f: (i_layer, 0, f)),
                  pl.BlockSpec((1, D, tf), lambda i, f: (i_layer, 0, f)),
                  pl.BlockSpec((1, tf, D), lambda i, f: (i_layer, f, 0))],
        out_specs=pl.BlockSpec((FFN_TM, D), lambda i, f: (i, 0)),
        out_shape=jax.ShapeDtypeStruct((T, D), F32),
        scratch_shapes=[pltpu.VMEM((FFN_TM, D), F32)],
        compiler_params=_cparams(("arbitrary", "arbitrary")),
        name="ffn_dense",
    )(xn, x1, mod3, w1, w3, w2)


MOE_BM = 1024
MOE_SUB = 256
MOE_NBLK = 2 * T // MOE_BM + N_EXP
MOE_NR = MOE_NBLK * MOE_BM
MOE_TF = 512
FIN_TM = 512


def _route_kernel(i1_ref, i2_ref, r1_ref, r2_ref, cnt_ref, w1_ref, w2_ref,
                  src_ref, gate_ref, eb_ref, rows_ref, nvb_ref, off_scr):
    def clear(r, carry):
        src_ref[r] = 0
        gate_ref[r] = 0.0
        return carry

    lax.fori_loop(0, MOE_NR, clear, 0, unroll=8)

    def clear_blk(b, carry):
        eb_ref[b] = 0
        rows_ref[b] = 0
        return carry

    lax.fori_loop(0, MOE_NBLK, clear_blk, 0)

    nblk = jnp.int32(0)
    for e in range(N_EXP):
        c = cnt_ref[e]
        nbe = (c + (MOE_BM - 1)) // MOE_BM
        off_scr[e] = nblk * MOE_BM

        def fill(j, carry, e=e, c=c, nblk=nblk):
            eb_ref[nblk + j] = e
            rows_ref[nblk + j] = jnp.minimum(c - j * MOE_BM, MOE_BM)
            return carry

        lax.fori_loop(0, nbe, fill, 0)
        nblk = nblk + nbe
    nvb_ref[0] = nblk

    def place(t, carry):
        p1 = off_scr[i1_ref[t]] + r1_ref[t]
        src_ref[p1] = t
        gate_ref[p1] = w1_ref[t]
        w2 = w2_ref[t]

        @pl.when(w2 > 0.0)
        def _():
            p2 = off_scr[i2_ref[t]] + r2_ref[t]
            src_ref[p2] = t
            gate_ref[p2] = w2

        return carry

    lax.fori_loop(0, T, place, 0, unroll=4)


def _route_call(route, cnt):
    ints = [route[k].astype(jnp.int32) for k in range(4)]
    cnt8 = cnt[0, :N_EXP].astype(jnp.int32)
    smem = pl.BlockSpec(memory_space=pltpu.SMEM)
    return pl.pallas_call(
        _route_kernel,
        grid_spec=pltpu.PrefetchScalarGridSpec(
            num_scalar_prefetch=5, grid=(1,),
            in_specs=[smem, smem],
            out_specs=[smem, smem, smem, smem, smem],
            scratch_shapes=[pltpu.SMEM((N_EXP,), jnp.int32)]),
        out_shape=[jax.ShapeDtypeStruct((MOE_NR,), jnp.int32), jax.ShapeDtypeStruct((MOE_NR,), F32),
                   jax.ShapeDtypeStruct((MOE_NBLK,), jnp.int32), jax.ShapeDtypeStruct((MOE_NBLK,), jnp.int32),
                   jax.ShapeDtypeStruct((1,), jnp.int32)],
        compiler_params=_cparams(("arbitrary",)),
        name="moe_route",
    )(*ints, cnt8, route[4], route[5])


def _moe_gather_kernel(src_ref, rows_ref, x_ref, o_ref, rows_scr):
    k = pl.program_id(0)
    base = k * MOE_SUB
    used = rows_ref[k // (MOE_BM // MOE_SUB)] > (k % (MOE_BM // MOE_SUB)) * MOE_SUB

    @pl.when(used)
    def _():
        def body(i, carry):
            for j in range(8):
                r = i * 8 + j
                rows_scr[pl.ds(r, 1), :] = x_ref[pl.ds(src_ref[base + r], 1), :]
            return carry

        lax.fori_loop(0, MOE_SUB // 8, body, 0)
        o_ref[...] = rows_scr[...].astype(BF16)

    @pl.when(jnp.logical_not(used))
    def _():
        o_ref[...] = jnp.zeros(o_ref.shape, o_ref.dtype)


def _moe_gather_call(src, rows_b, xn):
    return pl.pallas_call(
        _moe_gather_kernel,
        grid_spec=pltpu.PrefetchScalarGridSpec(
            num_scalar_prefetch=2, grid=(MOE_NR // MOE_SUB,),
            in_specs=[pl.BlockSpec((T, D), lambda k, s, rw: (0, 0), pipeline_mode=pl.Buffered(1))],
            out_specs=pl.BlockSpec((MOE_SUB, D), lambda k, s, rw: (k, 0)),
            scratch_shapes=[pltpu.VMEM((MOE_SUB, D), F32)]),
        out_shape=jax.ShapeDtypeStruct((MOE_NR, D), BF16),
        compiler_params=_cparams(("arbitrary",)),
        name="moe_gather",
    )(src, rows_b, xn)


def _moe_kernel(eb_ref, rows_ref, nvb_ref, xs_ref, w1_ref, w3_ref, w2_ref, y_ref, w1b, w3b, w2b):
    b = pl.program_id(0)
    f = pl.program_id(1)
    nsub = jnp.where(b < nvb_ref[0], (rows_ref[b] + MOE_SUB - 1) // MOE_SUB, 0)

    @pl.when(nsub > 0)
    def _():
        w1b[...] = w1_ref[0, 0].astype(BF16)
        w3b[...] = w3_ref[0, 0].astype(BF16)
        w2b[...] = w2_ref[0, 0].astype(BF16)

    for k in range(1, MOE_BM // MOE_SUB + 1):
        m = k * MOE_SUB

        @pl.when(nsub == k)
        def _(m=m):
            x = xs_ref[0:m, :]
            a = _dot(x, w1b[...])
            g = _dot(x, w3b[...])
            h = (a * jax.nn.sigmoid(a) * g).astype(BF16)
            part = _dot(h, w2b[...])

            @pl.when(f == 0)
            def _():
                y_ref[0:m, :] = part
                if m < MOE_BM:
                    y_ref[m:MOE_BM, :] = jnp.zeros((MOE_BM - m, D), F32)

            @pl.when(f > 0)
            def _():
                y_ref[0:m, :] += part


def _moe_call(xs, e_b, rows_b, nvb, w1, w3, w2, i_layer):
    nf = D_FFE // MOE_TF

    def blk(b, nvb_ref):
        return jnp.minimum(b, nvb_ref[0] - 1)

    def ff(b, f, nvb_ref):
        return jnp.where(b < nvb_ref[0], f, nf - 1)

    return pl.pallas_call(
        _moe_kernel,
        grid_spec=pltpu.PrefetchScalarGridSpec(
            num_scalar_prefetch=3, grid=(MOE_NBLK, nf),
            in_specs=[pl.BlockSpec((MOE_BM, D), lambda b, f, eb, rw, nv: (blk(b, nv), 0)),
                      pl.BlockSpec((1, 1, D, MOE_TF),
                                   lambda b, f, eb, rw, nv: (i_layer, eb[blk(b, nv)], 0, ff(b, f, nv))),
                      pl.BlockSpec((1, 1, D, MOE_TF),
                                   lambda b, f, eb, rw, nv: (i_layer, eb[blk(b, nv)], 0, ff(b, f, nv))),
                      pl.BlockSpec((1, 1, MOE_TF, D),
                                   lambda b, f, eb, rw, nv: (i_layer, eb[blk(b, nv)], ff(b, f, nv), 0))],
            out_specs=pl.BlockSpec((MOE_BM, D), lambda b, f, eb, rw, nv: (blk(b, nv), 0)),
            scratch_shapes=[pltpu.VMEM((D, MOE_TF), BF16), pltpu.VMEM((D, MOE_TF), BF16),
                            pltpu.VMEM((MOE_TF, D), BF16)]),
        out_shape=jax.ShapeDtypeStruct((MOE_NR, D), F32),
        compiler_params=_cparams(("arbitrary", "arbitrary")),
        name="moe_experts",
    )(e_b, rows_b, nvb, xs, w1, w3, w2)


def _moe_combine_kernel(src_ref, gate_ref, rows_ref, nvb_ref, ys_ref, x1_ref, g2_ref, y_ref, acc):
    i = pl.program_id(0)

    @pl.when(i == 0)
    def _():
        acc[...] = jnp.zeros(acc.shape, F32)

    @pl.when(i < nvb_ref[0])
    def _():
        base = i * MOE_BM
        n = rows_ref[i]

        def add_rows(r0, cnt):
            toks = [src_ref[base + r0 + j] for j in range(cnt)]
            vals = [acc[pl.ds(toks[j], 1), :] + gate_ref[base + r0 + j] * ys_ref[pl.ds(r0 + j, 1), :]
                    for j in range(cnt)]
            for j in range(cnt):
                acc[pl.ds(toks[j], 1), :] = vals[j]

        def body4(q, carry):
            add_rows(q * 4, 4)
            return carry

        lax.fori_loop(0, n // 4, body4, 0)

        def body1(r, carry):
            add_rows(r, 1)
            return carry

        lax.fori_loop((n // 4) * 4, n, body1, 0)

    @pl.when(i >= MOE_NBLK)
    def _():
        t0 = pl.multiple_of((i - MOE_NBLK) * FIN_TM, FIN_TM)
        y_ref[...] = x1_ref[...] + g2_ref[0] * acc[pl.ds(t0, FIN_TM), :]


def _moe_combine_call(src, gate_r, rows_b, nvb, ys, x1, mod3, l):
    nfin = T // FIN_TM

    def g2_idx(i, *_):
        j = jnp.maximum(i - MOE_NBLK, 0)
        r = jnp.where(j < T_P // FIN_TM, 0, 1 + (j - T_P // FIN_TM) // (S_S // FIN_TM))
        return ((l * 8 + r) * 6 + 5, 0, 0)

    return pl.pallas_call(
        _moe_combine_kernel,
        grid_spec=pltpu.PrefetchScalarGridSpec(
            num_scalar_prefetch=4, grid=(MOE_NBLK + nfin,),
            in_specs=[pl.BlockSpec((MOE_BM, D),
                                   lambda i, s, g, rw, nv: (jnp.minimum(jnp.minimum(i, MOE_NBLK - 1), nv[0] - 1), 0)),
                      pl.BlockSpec((FIN_TM, D), lambda i, s, g, rw, nv: (jnp.maximum(i - MOE_NBLK, 0), 0)),
                      pl.BlockSpec((1, 1, D), g2_idx)],
            out_specs=pl.BlockSpec((FIN_TM, D), lambda i, s, g, rw, nv: (jnp.maximum(i - MOE_NBLK, 0), 0)),
            scratch_shapes=[pltpu.VMEM((T, D), F32)]),
        out_shape=jax.ShapeDtypeStruct((T, D), F32),
        compiler_params=_cparams(("arbitrary",), vmem=52 * 1024 * 1024),
        name="moe_combine",
    )(src, gate_r, rows_b, nvb, ys, x1, mod3)


def _moe_layer(xp, x1, route, cnt, mod3, l, w1, w3, w2, i_layer):
    src, gate_r, e_b, rows_b, nvb = _route_call(route, cnt)
    xs = _moe_gather_call(src, rows_b, xp)
    ys = _moe_call(xs, e_b, rows_b, nvb, w1, w3, w2, i_layer)
    return _moe_combine_call(src, gate_r, rows_b, nvb, ys, x1, mod3, l)


def _pad_cols(w, n):
    return jnp.pad(w, ((0, 0), (0, n - w.shape[1])))


def _relayout_w_in(w):
    aq, ak, av, bq, bk, bv, bo, bg, cq, ckv, ckr = jnp.split(
        w, (384, 512, 640, 896, 1152, 1408, 1664, 1680, 1936, 2064), axis=1)
    dup = lambda m: jnp.concatenate([m[:, 0:HD], m[:, 0:HD], m[:, HD:2 * HD], m[:, HD:2 * HD]], axis=1)
    g = bg.reshape(D, 2, 2, 2, 2)

    def gate_cols(gi):
        cols = []
        for j in range(2):
            cols.append(_pad_cols(g[:, :, gi, j, :].reshape(D, 4), LANES))
        return jnp.concatenate(cols, axis=1)

    ckr_p = jnp.pad(ckr, ((0, 0), (C_NOPE, LANES - C_NOPE - C_ROPE)))
    out = jnp.concatenate([aq, dup(ak), dup(av), bq, bk, bv, bo, gate_cols(0), gate_cols(1),
                           cq, ckv, ckr_p], axis=1)
    return out.astype(BF16)


def _relayout_gate_b(b):
    g = b.reshape(2, 2, 2, 2)

    def cols(gi):
        return jnp.concatenate([jnp.pad(g[:, gi, j, :].reshape(4), (0, LANES - 4)) for j in range(2)])

    return jnp.concatenate([cols(0), cols(1)])[None, :]


def _pad_heads(w, width):
    r = w.shape[0]
    h = w.shape[1] // width
    return jnp.pad(w.reshape(r, h, width), ((0, 0), (0, 0), (0, LANES - width))).reshape(r, h * LANES)


def _rope_tables(half, span_start, period):
    rows = S_S // 64
    r = jnp.repeat(jnp.arange(rows), 64).astype(F32)
    c = jnp.tile(jnp.arange(64), rows).astype(F32)
    n_freq = half // 2
    freq = 10000.0 ** (-jnp.arange(n_freq, dtype=F32) / n_freq)
    ang = jnp.concatenate([r[:, None] * freq, c[:, None] * freq], axis=-1)
    cos, sin = jnp.cos(ang), jnp.sin(ang)
    d = (jnp.arange(LANES) - span_start) % period
    inside = d < 2 * half
    p = jnp.where(inside, d % half, 0)
    first = inside & (d < half)
    second = inside & (d >= half)
    ct = jnp.where(inside[None, :], cos[:, p], 1.0)
    sa = jnp.where(first[None, :], -sin[:, p], 0.0)
    sb = jnp.where(second[None, :], sin[:, p], 0.0)
    ident = (jnp.ones((TM, LANES), F32), jnp.zeros((TM, LANES), F32), jnp.zeros((TM, LANES), F32))
    return tuple(jnp.concatenate([i0, t], axis=0) for i0, t in zip(ident, (ct, sa, sb)))


def kernel(x_prompt, x_sample, c, cache_swa_k, cache_swa_v, cache_mla_ckv, cache_mla_krope, state_mlstm_C, state_mlstm_n, state_mlstm_m, c_ctx, ada_w, ada_b, norm1_g, norm2_g, w_in, a_qn_g, a_kn_g, a_sink, b_gate_b, b_hn_g, c_qa_g, c_kva_g, c_wuq, c_wukv, c_qn_g, c_kn_g, w_out, ffn_w1, ffn_w3, ffn_w2, moe_router, moe_w1, moe_w3, moe_w2):
    x = jnp.concatenate([x_prompt.reshape(T_P, D), x_sample.reshape(T_S, D)], axis=0)
    cv = jnp.concatenate([c_ctx[None, :], c, jnp.zeros((5, D), F32)], axis=0)
    mod = _ada_call(cv, ada_w, ada_b)
    mod3 = mod.reshape(DEPTH * 8 * 6, 1, D)

    rope_a = _rope_tables(32, 0, HD)
    rope_c = _rope_tables(16, C_NOPE, LANES)

    w_in_p = jnp.stack([_relayout_w_in(w_in[l]) for l in range(DEPTH)])
    wuq_p = jnp.stack([_pad_heads(c_wuq[l], C_QK) for l in range(DEPTH)]).astype(BF16)
    wukv = c_wukv.reshape(DEPTH, C_KVRANK, C_HEADS, C_NOPE + C_V)
    wk_p = jnp.pad(wukv[..., :C_NOPE], ((0, 0), (0, 0), (0, 0), (0, LANES - C_NOPE)))
    wk_p = wk_p.reshape(DEPTH, C_KVRANK, C_HEADS * LANES).astype(BF16)
    wv_p = wukv[..., C_NOPE:].reshape(DEPTH, C_KVRANK, C_HEADS * C_V).astype(BF16)
    w_out_b = w_out.astype(BF16)
    cache_k4 = cache_swa_k.reshape(NB_S, DEPTH, PAST, A_KV * HD)
    cache_v4 = cache_swa_v.reshape(NB_S, DEPTH, PAST, A_KV * HD)

    news = []
    for l in range(DEPTH):
        gqa = jnp.tile(a_qn_g[l], A_HEADS)[None, :]
        gka = jnp.tile(a_kn_g[l], 2 * A_KV)[None, :]
        gb = _relayout_gate_b(b_gate_b[l])
        gqn = jnp.tile(jnp.pad(c_qn_g[l], (0, LANES - C_QK)), C_HEADS)[None, :]
        gkn = jnp.tile(jnp.pad(c_kn_g[l], (0, LANES - C_QK)), C_HEADS)[None, :]
        (qa, ka, va, kaf, vaf, bq, bk, bv, bo, li, lf, qc, ckv_n, ckr) = _in_call(
            x, mod3, l, norm1_g[l][None, :], w_in_p, gqa, gka, gb, c_qa_g[l][None, :],
            c_kva_g[l][None, :], wuq_p, gqn, rope_a, rope_c)

        ckr_cache = jnp.pad(cache_mla_krope[:, l].reshape(NB_S * PAST, C_ROPE),
                            ((0, 0), (C_NOPE, LANES - C_NOPE - C_ROPE)))
        ckv_all = jnp.concatenate([ckv_n, cache_mla_ckv[:, l].reshape(NB_S * PAST, C_KVRANK)], axis=0)
        ckr_all = jnp.concatenate([ckr, ckr_cache], axis=0)
        kc, vc = _mlakv_call(ckv_all, ckr_all, l, wk_p, wv_p, gkn, rope_c)

        sink = a_sink[l]
        oa_p = _attn_a_prompt_call(sink, qa, ka, va)
        oa_s = _attn_a_sample_call(sink, qa, ka, va, cache_k4, cache_v4, l)
        oc_p = _mla_prompt_call(qc, kc, vc)
        oc_s = _mla_sample_call(qc, kc, vc)

        hn2 = jnp.tile(b_hn_g[l], 2)[None, :]
        ob_p, cn, nn, mn = _mlstm_call(bq, bk, bv, bo, li, lf, hn2, None, l)
        c_st = state_mlstm_C[:, l].reshape(NB_S, 2, 2, 2, HD, HD)
        c_st = jnp.transpose(c_st, (0, 2, 1, 3, 4, 5))
        c0 = jnp.zeros((NB_S, 2, 2, LANES, LANES), F32)
        c0 = c0.at[..., :HD, :HD].set(c_st[:, :, :, 0]).at[..., HD:, HD:].set(c_st[:, :, :, 1])
        c0 = c0.reshape(NB_S * 2, 2, LANES, LANES)
        n_st = state_mlstm_n[:, l].reshape(NB_S, 2, 2, 1, LANES)
        n0 = jnp.transpose(n_st, (0, 2, 1, 3, 4)).reshape(NB_S * 2, 2, 1, LANES)
        m_st = state_mlstm_m[:, l].reshape(NB_S, 2, 2, 2)
        m0 = jnp.transpose(m_st, (0, 2, 1, 3)).reshape(NB_S * 2, 1, 4)
        m0 = jnp.pad(m0, ((0, 0), (0, 0), (0, LANES - 4)))
        ob_s = _mlstm_call(bq, bk, bv, bo, li, lf, hn2, (c0, n0, m0), l)

        oa = jnp.concatenate([oa_p, oa_s], axis=0)
        ob = jnp.concatenate([ob_p, ob_s], axis=0)
        oc = jnp.concatenate([oc_p, oc_s], axis=0)

        if l % 2 == 0:
            x1, xn = _out_call(x, oa, ob, oc, w_out_b, mod3, l, norm2_g[l][None, :], None)
            x = _ffn_call(xn, x1, mod3, l, ffn_w1, ffn_w3, ffn_w2, l // 2)
        else:
            r = _pad_cols(moe_router[l // 2], LANES)
            rh = r.astype(BF16)
            rl = (r - rh.astype(F32)).astype(BF16)
            x1, xn, route, cnt = _out_call(x, oa, ob, oc, w_out_b, mod3, l, norm2_g[l][None, :],
                                           (rh[None], rl[None]))
            x = _moe_layer(xn, x1, route, cnt, mod3, l, moe_w1, moe_w3, moe_w2, l // 2)

        new_k = kaf.reshape(NB_P, S_P, A_KV, HD)
        new_v = vaf.reshape(NB_P, S_P, A_KV, HD)
        new_ckv = ckv_n[:T_P].reshape(NB_P, S_P, C_KVRANK)
        new_kr = ckr[:T_P, C_NOPE:C_NOPE + C_ROPE].reshape(NB_P, S_P, C_ROPE)
        cn6 = cn.reshape(NB_P, 2, 2, LANES, LANES)
        c_e = jnp.stack([cn6[..., :HD, :HD], cn6[..., HD:, HD:]], axis=3)
        new_c = jnp.transpose(c_e, (0, 2, 1, 3, 4, 5)).reshape(NB_P, 2, B_HEADS, HD, HD)
        nn5 = nn.reshape(NB_P, 2, 2, 2, HD)
        new_n = jnp.transpose(nn5, (0, 2, 1, 3, 4)).reshape(NB_P, 2, B_HEADS, HD)
        mn4 = mn.reshape(NB_P, 2, LANES)[:, :, :4].reshape(NB_P, 2, 2, 2)
        new_m = jnp.transpose(mn4, (0, 2, 1, 3)).reshape(NB_P, 2, B_HEADS)
        news.append((new_k, new_v, new_ckv, new_kr, new_c, new_n, new_m))

    y_prompt = x[:T_P].reshape(NB_P, S_P, D)
    y_sample = x[T_P:].reshape(NB_S, S_S, D)
    stacked = tuple(jnp.stack([nw[j] for nw in news], axis=1) for j in range(7))
    return (y_prompt, y_sample) + stacked
```

```python
import functools

import jax
import jax.numpy as jnp
from jax import lax
from jax.experimental import pallas as pl
from jax.experimental.pallas import tpu as pltpu

F32 = jnp.float32
BF16 = jnp.bfloat16

D = 1024
NB_P, S_P = 16, 256
NB_S, S_S = 2, 1024
PAST = 512
DEPTH = 2
T_P = NB_P * S_P
T_S = NB_S * S_S
T = T_P + T_S
TM = 256
NT = T // TM
NT_P = T_P // TM
HD = 64
A_HEADS, A_KV = 6, 2
B_HEADS = 4
CHUNK = 64
C_HEADS = 6
C_QRANK, C_KVRANK, C_NOPE, C_ROPE, C_V = 256, 128, 64, 32, 64
C_QK = C_NOPE + C_ROPE
D_FF = 2816
N_EXP = 8
D_FFE = 3584
EPS = 1e-6
NEG = -1e30
LANES = 128
VMEM_LIMIT = 48 * 1024 * 1024

SEG = dict(QA=(0, 384), KA=(384, 256), VA=(640, 256), BQ=(896, 256), BK=(1152, 256),
           BV=(1408, 256), BO=(1664, 256), GI=(1920, 256), GF=(2176, 256), CQ=(2432, 256),
           CKV=(2688, 128), CKR=(2816, 128))
NP_IN = 2944


def _cparams(sem, vmem=VMEM_LIMIT):
    return pltpu.CompilerParams(dimension_semantics=sem, vmem_limit_bytes=vmem)


def _dot(a, b):
    return jnp.dot(a, b, preferred_element_type=F32)


def _dot_nt(a, b):
    return lax.dot_general(a, b, (((1,), (1,)), ((), ())), preferred_element_type=F32)


def _lane(shape):
    return lax.broadcasted_iota(jnp.int32, shape, len(shape) - 1)


def _mod_row(i):
    return jnp.where(i < NT_P, 0, 1 + (i - NT_P) // (S_S // TM))


def _pair_specs(n, sample_off):
    return [pl.BlockSpec((TM, n), lambda i: (jnp.minimum(i, NT_P - 1), 0)),
            pl.BlockSpec((TM, n), lambda i: (sample_off + jnp.maximum(i - NT_P, 0), 0))]


def _pick(i, p_ref, s_ref):
    return jnp.where(i < NT_P, p_ref[...], s_ref[...])


def _rope_blk(i):
    return jnp.where((i >= NT_P) & (i < NT), 1 + (i - NT_P) % (S_S // TM), 0)


def _ada_kernel(cv_ref, w_ref, b_ref, o_ref):
    s = cv_ref[...]
    s = s * jax.nn.sigmoid(s)
    o_ref[0] = _dot(s.astype(BF16), w_ref[0].astype(BF16)) + b_ref[0]


def _ada_call(cv, ada_w, ada_b):
    tn = 1536
    return pl.pallas_call(
        _ada_kernel,
        grid=(DEPTH, 6 * D // tn),
        in_specs=[pl.BlockSpec((8, D), lambda l, j: (0, 0)),
                  pl.BlockSpec((1, D, tn), lambda l, j: (l, 0, j)),
                  pl.BlockSpec((1, 1, tn), lambda l, j: (l, 0, j))],
        out_specs=pl.BlockSpec((1, 8, tn), lambda l, j: (l, 0, j)),
        out_shape=jax.ShapeDtypeStruct((DEPTH, 8, 6 * D), F32),
        compiler_params=_cparams(("arbitrary", "arbitrary")),
        name="ada_mod",
    )(cv, ada_w, ada_b.reshape(DEPTH, 1, 6 * D))


def _half_norm(x, g, n):
    outs = []
    for j in range(x.shape[1] // LANES):
        xj = x[:, j * LANES:(j + 1) * LANES]
        lo = _lane(xj.shape) < HD
        s = xj * xj
        s_lo = jnp.sum(jnp.where(lo, s, 0.0), axis=-1, keepdims=True)
        s_hi = jnp.sum(jnp.where(lo, 0.0, s), axis=-1, keepdims=True)
        r = lax.rsqrt(jnp.where(lo, s_lo, s_hi) * (1.0 / n) + EPS)
        outs.append(xj * r)
    return jnp.concatenate(outs, axis=-1) * g


def _group_norm(x, g, n):
    outs = []
    for j in range(x.shape[1] // LANES):
        xj = x[:, j * LANES:(j + 1) * LANES]
        r = lax.rsqrt(jnp.sum(xj * xj, axis=-1, keepdims=True) * (1.0 / n) + EPS)
        outs.append(xj * r)
    return jnp.concatenate(outs, axis=-1) * g


def _rope(x, c, sa, sb, shift):
    outs = []
    for j in range(x.shape[1] // LANES):
        xj = x[:, j * LANES:(j + 1) * LANES]
        outs.append(xj * c + pltpu.roll(xj, LANES - shift, 1) * sa + pltpu.roll(xj, shift, 1) * sb)
    return jnp.concatenate(outs, axis=-1)


def _in_kernel(xp_ref, xs_ref, sh_ref, sc_ref, n1_ref, w_ref, gqa_ref, gka_ref, gb_ref, gcq_ref, gckv_ref,
               wuq_ref, gqn_ref, ra_c, ra_a, ra_b, rc_c, rc_a, rc_b,
               qa_o, ka_o, va_o, kaf_o, vaf_o, bq_o, bk_o, bv_o, bo_o, li_o, lf_o, qc_o, ckv_o, ckr_o):
    i = pl.program_id(0)
    x = _pick(i, xp_ref, xs_ref)
    xn = x * lax.rsqrt(jnp.mean(x * x, axis=-1, keepdims=True) + EPS) * n1_ref[...]
    xn = xn * (1.0 + sc_ref[0]) + sh_ref[0]
    p = _dot(xn.astype(BF16), w_ref[0])

    def seg(name):
        o, n = SEG[name]
        return p[:, o:o + n]

    qa = _half_norm(seg("QA"), gqa_ref[...], HD)
    ka = _half_norm(seg("KA"), gka_ref[...], HD)
    va = seg("VA")

    @pl.when(i < NT_P)
    def _():
        kaf_o[...] = jnp.concatenate([ka[:, 0:HD], ka[:, LANES:LANES + HD]], axis=-1)
        vaf_o[...] = jnp.concatenate([va[:, 0:HD], va[:, LANES:LANES + HD]], axis=-1)

    qa_o[...] = _rope(qa, ra_c[...], ra_a[...], ra_b[...], 32).astype(BF16)
    ka_o[...] = _rope(ka, ra_c[...], ra_a[...], ra_b[...], 32).astype(BF16)
    va_o[...] = va.astype(BF16)

    bq_o[...] = seg("BQ").astype(BF16)
    bk_o[...] = (seg("BK") * (HD ** -0.5)).astype(BF16)
    bv_o[...] = seg("BV").astype(BF16)
    bo_o[...] = seg("BO")
    gb = gb_ref[...]
    li_o[...] = seg("GI") + gb[:, 0:256]
    lf_o[...] = jax.nn.log_sigmoid(seg("GF") + gb[:, 256:512])

    cq = seg("CQ")
    cqn = cq * lax.rsqrt(jnp.mean(cq * cq, axis=-1, keepdims=True) + EPS) * gcq_ref[...]
    qc = _group_norm(_dot(cqn.astype(BF16), wuq_ref[0]), gqn_ref[...], C_QK)
    qc_o[...] = _rope(qc, rc_c[...], rc_a[...], rc_b[...], 16).astype(BF16)

    ckv = seg("CKV")
    ckv_o[...] = ckv * lax.rsqrt(jnp.mean(ckv * ckv, axis=-1, keepdims=True) + EPS) * gckv_ref[...]
    ckr_o[...] = seg("CKR")


def _in_call(x_p, x_s, sample_off, mod3, l, n1g, w_in_p, gqa, gka, gb, gcq, gckv, wuq_p, gqn, rope_a, rope_c):
    row = lambda k: pl.BlockSpec((1, 1, D), lambda i: ((l * 8 + _mod_row(i)) * 6 + k, 0, 0))
    vec = lambda n: pl.BlockSpec((1, n), lambda i: (0, 0))
    tab = pl.BlockSpec((TM, LANES), lambda i: (_rope_blk(i), 0))
    tok = lambda n: pl.BlockSpec((TM, n), lambda i: (i, 0))
    tokp = pl.BlockSpec((TM, LANES), lambda i: (jnp.minimum(i, NT_P - 1), 0))
    o = lambda n, dt: jax.ShapeDtypeStruct((T, n), dt)
    return pl.pallas_call(
        _in_kernel,
        grid=(NT,),
        in_specs=_pair_specs(D, sample_off) + [row(0), row(1), vec(D),
                  pl.BlockSpec((1, D, NP_IN), lambda i: (l, 0, 0)),
                  vec(384), vec(256), vec(512), vec(256), vec(128),
                  pl.BlockSpec((1, C_QRANK, 768), lambda i: (l, 0, 0)), vec(768),
                  tab, tab, tab, tab, tab, tab],
        out_specs=[tok(384), tok(256), tok(256), tokp, tokp, tok(256), tok(256), tok(256), tok(256),
                   tok(256), tok(256), tok(768), tok(128), tok(128)],
        out_shape=[o(384, BF16), o(256, BF16), o(256, BF16),
                   jax.ShapeDtypeStruct((T_P, LANES), F32), jax.ShapeDtypeStruct((T_P, LANES), F32),
                   o(256, BF16), o(256, BF16), o(256, BF16), o(256, F32),
                   o(256, F32), o(256, F32), o(768, BF16), o(128, F32), o(128, F32)],
        compiler_params=_cparams(("arbitrary",)),
        name=f"in_proj_l{l}",
    )(x_p, x_s, mod3, mod3, n1g, w_in_p, gqa, gka, gb, gcq, gckv, wuq_p, gqn, *rope_a, *rope_c)


def _mlakv_kernel(ckv_ref, ckvc_ref, ckr_ref, ckrc_ref, wk_ref, wv_ref, gkn_ref, rc_c, rc_a, rc_b, k_o, v_o):
    tok = pl.program_id(0) < NT
    c = jnp.where(tok, ckv_ref[...], ckvc_ref[...]).astype(BF16)
    k = _dot(c, wk_ref[0])
    kr = jnp.where(tok, ckr_ref[...], ckrc_ref[...])
    k = k + jnp.concatenate([kr] * C_HEADS, axis=-1)
    k = _group_norm(k, gkn_ref[...], C_QK)
    k_o[...] = _rope(k, rc_c[...], rc_a[...], rc_b[...], 16).astype(BF16)
    v_o[...] = _dot(c, wv_ref[0]).astype(BF16)


def _mlakv_call(ckv_n, ckv_cache, ckr, ckr_cache, l, wk_p, wv_p, gkn, rope_c):
    r = T + NB_S * PAST
    tab = pl.BlockSpec((TM, LANES), lambda i: (_rope_blk(i), 0))
    tok = lambda n: pl.BlockSpec((TM, n), lambda i: (i, 0))
    tokens = pl.BlockSpec((TM, LANES), lambda i: (jnp.minimum(i, NT - 1), 0))
    cached = pl.BlockSpec((TM, LANES), lambda i: (jnp.maximum(i - NT, 0), 0))
    return pl.pallas_call(
        _mlakv_kernel,
        grid=(r // TM,),
        in_specs=[tokens, cached, tokens, cached,
                  pl.BlockSpec((1, C_KVRANK, 768), lambda i: (l, 0, 0)),
                  pl.BlockSpec((1, C_KVRANK, 384), lambda i: (l, 0, 0)),
                  pl.BlockSpec((1, 768), lambda i: (0, 0)), tab, tab, tab],
        out_specs=[tok(768), tok(384)],
        out_shape=[jax.ShapeDtypeStruct((r, 768), BF16), jax.ShapeDtypeStruct((r, 384), BF16)],
        compiler_params=_cparams(("arbitrary",)),
        name=f"mla_kv_l{l}",
    )(ckv_n, ckv_cache, ckr, ckr_cache, wk_p, wv_p, gkn, *rope_c)


def _softmax_pv(scores, values, sink):
    m = scores[0].max(axis=-1, keepdims=True)
    for s in scores[1:]:
        m = jnp.maximum(m, s.max(axis=-1, keepdims=True))
    if sink is not None:
        m = jnp.maximum(m, sink)
    es = [jnp.exp(s - m) for s in scores]
    den = es[0].sum(axis=-1, keepdims=True)
    for e in es[1:]:
        den = den + e.sum(axis=-1, keepdims=True)
    if sink is not None:
        den = den + jnp.exp(sink - m)
    inv = 1.0 / den
    out = None
    for e, v in zip(es, values):
        o = _dot((e * inv).astype(BF16), v)
        out = o if out is None else out + o
    return out


def _half(x, hi):
    lo = _lane(x.shape) < HD
    return jnp.where(lo != hi, x, jnp.zeros_like(x))


def _attn_a_prompt_kernel(sink_ref, q_ref, k_ref, v_ref, o_ref):
    q = q_ref[...]
    k = k_ref[...]
    v = v_ref[...]
    for j in range(A_HEADS // 2):
        acc = None
        for c in range(2):
            h = 2 * j + c
            g = h // (A_HEADS // A_KV)
            qh = _half(q[:, j * LANES:(j + 1) * LANES], c == 1)
            s = _dot_nt(qh, k[:, g * LANES:(g + 1) * LANES]) * (HD ** -0.5)
            vh = _half(v[:, g * LANES:(g + 1) * LANES], c == 1)
            o = _softmax_pv([s], [vh], sink_ref[h])
            acc = o if acc is None else acc + o
        o_ref[:, j * LANES:(j + 1) * LANES] = acc.astype(BF16)


def _attn_a_prompt_call(sink, qa, ka, va):
    return pl.pallas_call(
        _attn_a_prompt_kernel,
        grid_spec=pltpu.PrefetchScalarGridSpec(
            num_scalar_prefetch=1, grid=(NB_P,),
            in_specs=[pl.BlockSpec((S_P, 384), lambda b, s: (b, 0)),
                      pl.BlockSpec((S_P, 256), lambda b, s: (b, 0)),
                      pl.BlockSpec((S_P, 256), lambda b, s: (b, 0))],
            out_specs=pl.BlockSpec((S_P, 384), lambda b, s: (b, 0))),
        out_shape=jax.ShapeDtypeStruct((T_P, 384), BF16),
        compiler_params=_cparams(("arbitrary",)),
        name="attn_a_prompt",
    )(sink, qa, ka, va)


QB = 128
WIN = 128
BAND = QB + 2 * WIN


def _attn_a_sample_kernel(sink_ref, q_ref, k_ref, v_ref, kc_ref, vc_ref, o_ref):
    n = pl.program_id(1)
    ws = pl.multiple_of(jnp.clip(n * QB - WIN, 0, S_S - BAND), QB)
    q = q_ref[...]
    kb = k_ref[pl.ds(ws, BAND), :]
    vb = v_ref[pl.ds(ws, BAND), :]
    kc = kc_ref[0, 0].astype(BF16)
    vc = vc_ref[0, 0].astype(BF16)
    qpos = n * QB + lax.broadcasted_iota(jnp.int32, (QB, BAND), 0)
    kpos = ws + lax.broadcasted_iota(jnp.int32, (QB, BAND), 1)
    ok = jnp.abs(qpos - kpos) <= WIN
    lane = _lane((PAST, LANES))
    for j in range(A_HEADS // 2):
        acc = None
        for c in range(2):
            h = 2 * j + c
            g = h // (A_HEADS // A_KV)
            qh = _half(q[:, j * LANES:(j + 1) * LANES], c == 1)
            s_b = jnp.where(ok, _dot_nt(qh, kb[:, g * LANES:(g + 1) * LANES]) * (HD ** -0.5), NEG)
            kcg = jnp.where((lane < HD) == (g == 0), kc, jnp.zeros_like(kc))
            vcg = jnp.where((lane < HD) == (g == 0), vc, jnp.zeros_like(vc))
            if (g == 1) != (c == 1):
                kcg = pltpu.roll(kcg.astype(F32), HD, 1).astype(BF16)
                vcg = pltpu.roll(vcg.astype(F32), HD, 1).astype(BF16)
            s_c = _dot_nt(qh, kcg) * (HD ** -0.5)
            vh = _half(vb[:, g * LANES:(g + 1) * LANES], c == 1)
            o = _softmax_pv([s_b, s_c], [vh, vcg], sink_ref[h])
            acc = o if acc is None else acc + o
        o_ref[:, j * LANES:(j + 1) * LANES] = acc.astype(BF16)


def _attn_a_sample_call(sink, qa, ka, va, cache_k, cache_v, l):
    nqb = S_S // QB
    off = T_P // S_S
    return pl.pallas_call(
        _attn_a_sample_kernel,
        grid_spec=pltpu.PrefetchScalarGridSpec(
            num_scalar_prefetch=1, grid=(NB_S, nqb),
            in_specs=[pl.BlockSpec((QB, 384), lambda b, n, s: (T_P // QB + b * nqb + n, 0)),
                      pl.BlockSpec((S_S, 256), lambda b, n, s: (off + b, 0)),
                      pl.BlockSpec((S_S, 256), lambda b, n, s: (off + b, 0)),
                      pl.BlockSpec((1, 1, PAST, LANES), lambda b, n, s: (b, l, 0, 0)),
                      pl.BlockSpec((1, 1, PAST, LANES), lambda b, n, s: (b, l, 0, 0))],
            out_specs=pl.BlockSpec((QB, 384), lambda b, n, s: (b * nqb + n, 0))),
        out_shape=jax.ShapeDtypeStruct((T_S, 384), BF16),
        compiler_params=_cparams(("arbitrary", "arbitrary")),
        name=f"attn_a_sample_l{l}",
    )(sink, qa, ka, va, cache_k, cache_v)


def _mla_heads(q, ks, vs, o_ref):
    for j in range(C_HEADS // 2):
        acc = None
        for c in range(2):
            h = 2 * j + c
            qh = q[:, h * LANES:(h + 1) * LANES]
            scores = [_dot_nt(qh, k[:, h * LANES:(h + 1) * LANES]) * (C_QK ** -0.5) for k in ks]
            vals = [_half(v[:, j * LANES:(j + 1) * LANES], c == 1) for v in vs]
            o = _softmax_pv(scores, vals, None)
            acc = o if acc is None else acc + o
        o_ref[:, j * LANES:(j + 1) * LANES] = acc.astype(BF16)


def _mla_prompt_kernel(q_ref, k_ref, v_ref, o_ref):
    _mla_heads(q_ref[...], [k_ref[...]], [v_ref[...]], o_ref)


def _mla_prompt_call(qc, kc, vc):
    return pl.pallas_call(
        _mla_prompt_kernel,
        grid=(NB_P,),
        in_specs=[pl.BlockSpec((S_P, 768), lambda b: (b, 0)),
                  pl.BlockSpec((S_P, 768), lambda b: (b, 0)),
                  pl.BlockSpec((S_P, 384), lambda b: (b, 0))],
        out_specs=pl.BlockSpec((S_P, 384), lambda b: (b, 0)),
        out_shape=jax.ShapeDtypeStruct((T_P, 384), BF16),
        compiler_params=_cparams(("arbitrary",)),
        name="mla_prompt",
    )(qc, kc, vc)


def _mla_sample_kernel(q_ref, kc_ref, vc_ref, kl_ref, vl_ref, o_ref):
    _mla_heads(q_ref[...], [kc_ref[...], kl_ref[...]], [vc_ref[...], vl_ref[...]], o_ref)


def _mla_sample_call(qc, kc, vc):
    tq = 256
    nq = S_S // tq
    return pl.pallas_call(
        _mla_sample_kernel,
        grid=(NB_S, nq),
        in_specs=[pl.BlockSpec((tq, 768), lambda b, n: (T_P // tq + b * nq + n, 0)),
                  pl.BlockSpec((PAST, 768), lambda b, n: (T // PAST + b, 0)),
                  pl.BlockSpec((PAST, 384), lambda b, n: (T // PAST + b, 0)),
                  pl.BlockSpec((S_S, 768), lambda b, n: (T_P // S_S + b, 0)),
                  pl.BlockSpec((S_S, 384), lambda b, n: (T_P // S_S + b, 0))],
        out_specs=pl.BlockSpec((tq, 384), lambda b, n: (b * nq + n, 0)),
        out_shape=jax.ShapeDtypeStruct((T_S, 384), BF16),
        compiler_params=_cparams(("arbitrary", "arbitrary")),
        name="mla_sample",
    )(qc, kc, vc, kc, vc)


def _split3(x):
    x1 = x.astype(BF16)
    r = x - x1.astype(F32)
    x2 = r.astype(BF16)
    x3 = (r - x2.astype(F32)).astype(BF16)
    return x1, x2, x3


def _bmm(a, b):
    return lax.dot_general(a, b, (((2,), (1,)), ((0,), (0,))), preferred_element_type=F32)


def _bmm_nt(a, b):
    return lax.dot_general(a, b, (((2,), (2,)), ((0,), (0,))), preferred_element_type=F32)


def _bmm_tn(a, b):
    return lax.dot_general(a, b, (((1,), (1,)), ((0,), (0,))), preferred_element_type=F32)


def _mlstm_kernel(nc, has_state, *refs):
    if has_state:
        (q_ref, k_ref, v_ref, bo_ref, li_ref, lf_ref, hn_ref, c0_ref, n0_ref, m0_ref, ob_ref) = refs
    else:
        (q_ref, k_ref, v_ref, bo_ref, li_ref, lf_ref, hn_ref, ob_ref, cn_ref, nn_ref, mn_ref) = refs
    s_len = nc * CHUNK
    c3 = lambda x: x.reshape(nc, CHUNK, x.shape[-1])
    li3 = c3(li_ref[...])
    lf3 = c3(lf_ref[...])
    row = lax.broadcasted_iota(jnp.int32, (CHUNK, CHUNK), 0)
    colv = lax.broadcasted_iota(jnp.int32, (CHUNK, CHUNK), 1)
    tri_f = colv <= row
    tri_b = colv >= row
    bcast = lambda m: jnp.broadcast_to(m[None], (nc, CHUNK, CHUNK))
    lane1 = _lane((1, LANES))
    fwd_lane = lane1 < 2

    lf_parts = _split3(lf3)
    tf3 = bcast(tri_f.astype(BF16))
    tb3 = bcast(tri_b.astype(BF16))
    bc_f = _bmm(tf3, lf_parts[0]) + _bmm(tf3, lf_parts[1]) + _bmm(tf3, lf_parts[2])
    bc_b = _bmm(tb3, lf_parts[0]) + _bmm(tb3, lf_parts[1]) + _bmm(tb3, lf_parts[2])
    bc3 = jnp.where(fwd_lane, bc_f, bc_b)
    row3 = lax.broadcasted_iota(jnp.int32, (nc, CHUNK, LANES), 1)
    edge = jnp.where(row3 == jnp.where(fwd_lane, CHUNK - 1, 0), bc3, 0.0)
    bl3 = jnp.sum(edge, axis=1, keepdims=True)
    gg3 = bl3 - bc3 + li3
    bl2 = jnp.sum(edge, axis=1)
    mg2 = gg3.max(axis=1)

    m0 = m0_ref[0] if has_state else jnp.zeros((1, LANES), F32)
    mf = m0
    mb = m0
    mf_prev, mf_next, mb_prev, mb_next = {}, {}, {}, {}
    for i in range(nc):
        mf_prev[i] = mf
        mf = jnp.maximum(bl2[i:i + 1] + mf, mg2[i:i + 1])
        mf_next[i] = mf
        cb = nc - 1 - i
        mb_prev[cb] = mb
        mb = jnp.maximum(bl2[cb:cb + 1] + mb, mg2[cb:cb + 1])
        mb_next[cb] = mb
    m_prev = [jnp.where(fwd_lane, mf_prev[c], mb_prev[c]) for c in range(nc)]
    m_next = [jnp.where(fwd_lane, mf_next[c], mb_next[c]) for c in range(nc)]
    m_prev3 = jnp.stack(m_prev)
    m_next3 = jnp.stack(m_next)
    dec2d = jnp.exp(bl2 + jnp.concatenate(m_prev, axis=0) - jnp.concatenate(m_next, axis=0))
    if not has_state:
        mn_ref[0] = jnp.where(fwd_lane, mf, mb)

    u_parts = _split3(li3 - bc3)
    ws3 = jnp.exp(gg3 - m_next3)
    inter3 = bc3 + m_prev3

    q3 = c3(q_ref[...])
    k3 = c3(k_ref[...])
    v3 = c3(v_ref[...])
    q3f = q3.astype(F32)
    k3f = k3.astype(F32)
    lane3 = _lane((nc, CHUNK, LANES))
    lo3 = lane3 < HD
    rr = lax.broadcasted_iota(jnp.int32, (LANES, LANES), 0)
    cc = lax.broadcasted_iota(jnp.int32, (LANES, LANES), 1)
    blockdiag = (rr < HD) == (cc < HD)
    col = lambda x, k: x[:, :, k:k + 1]

    hsum = None
    for d in range(2):
        causal3 = bcast(tri_f if d == 0 else tri_b)
        a_c, ws_sum, mt_c, ws_c, intra = [], [], [], [], None
        for e in range(2):
            kk = 2 * d + e
            pick = jnp.broadcast_to(jnp.where(_lane((CHUNK, LANES)) == kk, 1.0, 0.0).astype(BF16)[None],
                                    (nc, CHUNK, LANES))
            ub = _bmm_nt(pick, u_parts[0]) + _bmm_nt(pick, u_parts[1]) + _bmm_nt(pick, u_parts[2])
            bc_col = col(bc3, kk)
            d_mat = jnp.where(causal3, bc_col + ub, NEG)
            inter = col(inter3, kk)
            mt = jnp.maximum(inter, d_mat.max(axis=-1, keepdims=True))
            a = jnp.exp(inter - mt)
            w = jnp.exp(d_mat - mt) * _bmm_nt(_half(q3, e == 1), k3)
            o = _bmm(w.astype(BF16), _half(v3, e == 1))
            intra = o if intra is None else intra + o
            a_c.append(a)
            mt_c.append(mt)
            ws_sum.append(w.sum(axis=-1, keepdims=True))
            ws_c.append(col(ws3, kk))
        kw3 = k3f * jnp.where(lo3, ws_c[0], ws_c[1])
        dec2 = jnp.where(_lane((nc, LANES)) < HD, dec2d[:, 2 * d:2 * d + 1], dec2d[:, 2 * d + 1:2 * d + 2])
        u_all = jnp.where(blockdiag, _bmm_tn(kw3.astype(BF16), v3), 0.0)
        kwsum = kw3.sum(axis=1)
        if has_state:
            cst = c0_ref[0, d]
            nst = n0_ref[0, d]
        else:
            cst = jnp.zeros((LANES, LANES), F32)
            nst = jnp.zeros((1, LANES), F32)
        cs, ns = [None] * nc, [None] * nc
        for c in (range(nc) if d == 0 else range(nc - 1, -1, -1)):
            cs[c] = cst
            ns[c] = nst
            cst = dec2[c:c + 1] * cst + u_all[c]
            nst = dec2[c:c + 1] * nst + kwsum[c:c + 1]
        if not has_state:
            cn_ref[0, d] = cst
            nn_ref[0, d] = nst
        qc = _bmm(q3, jnp.stack(cs).astype(BF16))
        qn_all = q3f * jnp.stack(ns)
        dn = []
        for e in range(2):
            qn = jnp.sum(jnp.where(lo3 != (e == 1), qn_all, 0.0), axis=-1, keepdims=True)
            den = a_c[e] * qn + ws_sum[e]
            dn.append(jnp.maximum(jnp.abs(den), jnp.exp(-mt_c[e])))
        h2 = (jnp.where(lo3, a_c[0], a_c[1]) * qc + intra) / jnp.where(lo3, dn[0], dn[1])
        hsum = h2 if hsum is None else hsum + h2

    hs = hsum.reshape(s_len, LANES)
    lo = _lane(hs.shape) < HD
    s = hs * hs
    s_lo = jnp.sum(jnp.where(lo, s, 0.0), axis=-1, keepdims=True)
    s_hi = jnp.sum(jnp.where(lo, 0.0, s), axis=-1, keepdims=True)
    r = lax.rsqrt(jnp.where(lo, s_lo, s_hi) * (1.0 / HD) + EPS)
    ob_ref[...] = (hs * r * hn_ref[...] * jax.nn.sigmoid(bo_ref[...])).astype(BF16)


def _mlstm_call(bq, bk, bv, bo, li, lf, hn2, state, l):
    has_state = state is not None
    if has_state:
        nb, s_len, base = NB_S, S_S, T_P // S_S
    else:
        nb, s_len, base = NB_P, S_P, 0
    nc = s_len // CHUNK
    tokp = pl.BlockSpec((s_len, LANES), lambda b, j: (base + b, j))
    in_specs = [tokp, tokp, tokp, tokp, tokp, tokp, pl.BlockSpec((1, LANES), lambda b, j: (0, 0))]
    args = [bq, bk, bv, bo, li, lf, hn2]
    ob_spec = pl.BlockSpec((s_len, LANES), lambda b, j: (b, j))
    ob_shape = jax.ShapeDtypeStruct((nb * s_len, 256), BF16)
    if has_state:
        c0, n0, m0 = state
        in_specs += [pl.BlockSpec((1, 2, LANES, LANES), lambda b, j: (b * 2 + j, 0, 0, 0)),
                     pl.BlockSpec((1, 2, 1, LANES), lambda b, j: (b * 2 + j, 0, 0, 0)),
                     pl.BlockSpec((1, 1, LANES), lambda b, j: (b * 2 + j, 0, 0))]
        args += [c0, n0, m0]
        out_specs = ob_spec
        out_shape = ob_shape
    else:
        out_specs = [ob_spec,
                     pl.BlockSpec((1, 2, LANES, LANES), lambda b, j: (b * 2 + j, 0, 0, 0)),
                     pl.BlockSpec((1, 2, 1, LANES), lambda b, j: (b * 2 + j, 0, 0, 0)),
                     pl.BlockSpec((1, 1, LANES), lambda b, j: (b * 2 + j, 0, 0))]
        out_shape = [ob_shape,
                     jax.ShapeDtypeStruct((nb * 2, 2, LANES, LANES), F32),
                     jax.ShapeDtypeStruct((nb * 2, 2, 1, LANES), F32),
                     jax.ShapeDtypeStruct((nb * 2, 1, LANES), F32)]
    return pl.pallas_call(
        functools.partial(_mlstm_kernel, nc, has_state),
        grid=(nb, 2),
        in_specs=in_specs,
        out_specs=out_specs,
        out_shape=out_shape,
        compiler_params=_cparams(("arbitrary", "arbitrary")),
        name=f"mlstm_{'sample' if has_state else 'prompt'}_l{l}",
    )(*args)


def _out_kernel(moe, xp_ref, xs_ref, oap_ref, oas_ref, obp_ref, obs_ref, ocp_ref, ocs_ref,
                w_ref, g1_ref, sh_ref, sc_ref, n2_ref, *rest):
    if moe:
        rh_ref, rl_ref, x1_o, xn_o, route_o, cnt_o, run_scr = rest
    else:
        x1_o, xn_o = rest
    i = pl.program_id(0)
    o = (_dot(_pick(i, oap_ref, oas_ref), w_ref[0, 0:384, :])
         + _dot(_pick(i, obp_ref, obs_ref), w_ref[0, 384:640, :])
         + _dot(_pick(i, ocp_ref, ocs_ref), w_ref[0, 640:1024, :]))
    x1 = _pick(i, xp_ref, xs_ref) + g1_ref[0] * o
    x1_o[...] = x1
    xn = x1 * lax.rsqrt(jnp.mean(x1 * x1, axis=-1, keepdims=True) + EPS) * n2_ref[...]
    xn = xn * (1.0 + sc_ref[0]) + sh_ref[0]
    xb = xn.astype(BF16)
    if not moe:
        xn_o[...] = xb
    else:
        xn_o[...] = xn
        xl = (xn - xb.astype(F32)).astype(BF16)
        logits = _dot(xb, rh_ref[0]) + (_dot(xl, rh_ref[0]) + _dot(xb, rl_ref[0]))
        lane = _lane(logits.shape)
        logits = jnp.where(lane < N_EXP, logits, -jnp.inf)
        m1 = logits.max(axis=-1, keepdims=True)
        i1 = jnp.min(jnp.where(logits == m1, lane, LANES), axis=-1, keepdims=True)
        rest_l = jnp.where(lane == i1, -jnp.inf, logits)
        m2 = rest_l.max(axis=-1, keepdims=True)
        i2 = jnp.min(jnp.where(rest_l == m2, lane, LANES), axis=-1, keepdims=True)
        e2 = jnp.exp(m2 - m1)
        den = 1.0 + e2
        w1 = 1.0 / den
        w2 = e2 / den

        @pl.when(pl.program_id(0) == 0)
        def _():
            run_scr[...] = jnp.zeros(run_scr.shape, F32)

        sel = jnp.where(lane == i1, 1.0, jnp.where((lane == i2) & (w2 > 0.0), 1.0, 0.0))
        before = (lax.broadcasted_iota(jnp.int32, (TM, TM), 1)
                  < lax.broadcasted_iota(jnp.int32, (TM, TM), 0)).astype(BF16)
        rank = run_scr[...] + _dot(before, sel.astype(BF16))
        r1 = jnp.sum(jnp.where(lane == i1, rank, 0.0), axis=-1, keepdims=True)
        r2 = jnp.sum(jnp.where(lane == i2, rank, 0.0), axis=-1, keepdims=True)
        run_scr[...] += jnp.sum(sel, axis=0, keepdims=True)
        cnt_o[...] = run_scr[...]
        fields = [i1.astype(F32), i2.astype(F32), r1, r2, w1, w2]
        info = jnp.zeros(lane.shape, F32)
        for k, v in enumerate(fields):
            info = jnp.where(lane == k, v, info)
        route_o[...] = info.T


def _out_call(x_p, x_s, sample_off, oa, ob, oc, w_out_b, mod3, l, n2g, router):
    moe = router is not None
    row = lambda k: pl.BlockSpec((1, 1, D), lambda i: ((l * 8 + _mod_row(i)) * 6 + k, 0, 0))
    tok = lambda n: pl.BlockSpec((TM, n), lambda i: (i, 0))
    in_specs = (_pair_specs(D, sample_off) + _pair_specs(384, 0) + _pair_specs(256, 0) + _pair_specs(384, 0)
                + [pl.BlockSpec((1, D, D), lambda i: (l, 0, 0)),
                   row(2), row(3), row(4), pl.BlockSpec((1, D), lambda i: (0, 0))])
    args = [x_p, x_s, *oa, *ob, *oc, w_out_b, mod3, mod3, mod3, n2g]
    out_specs = [tok(D), tok(D)]
    out_shape = [jax.ShapeDtypeStruct((T, D), F32), jax.ShapeDtypeStruct((T, D), BF16)]
    if moe:
        out_shape[1] = jax.ShapeDtypeStruct((T, D), F32)
        rh, rl = router
        in_specs += [pl.BlockSpec((1, D, LANES), lambda i: (0, 0, 0))] * 2
        args += [rh, rl]
        out_specs += [pl.BlockSpec((LANES, TM), lambda i: (0, i)), pl.BlockSpec((1, LANES), lambda i: (0, 0))]
        out_shape += [jax.ShapeDtypeStruct((LANES, T), F32), jax.ShapeDtypeStruct((1, LANES), F32)]
    return pl.pallas_call(
        functools.partial(_out_kernel, moe),
        grid=(NT,),
        in_specs=in_specs, out_specs=out_specs, out_shape=out_shape,
        scratch_shapes=[pltpu.VMEM((1, LANES), F32)] if moe else [],
        compiler_params=_cparams(("arbitrary",)),
        name=f"out_proj_l{l}",
    )(*args)


FFN_TM = 1024
FFN_TF = 256
FFN_TN = 256


def _hidden_tile(x, w1_ref, w3_ref):
    a = _dot(x, w1_ref[...].astype(BF16))
    b = _dot(x, w3_ref[...].astype(BF16))
    return (a * jax.nn.sigmoid(a) * b).astype(BF16)


def _down_tile(h_scr, w2b, m, tf):
    out = None
    for f in range(h_scr.shape[0]):
        o = _dot(h_scr[f, 0:m, :], w2b[f * tf:(f + 1) * tf, :])
        out = o if out is None else out + o
    return out


def _ffn_kernel(nf, xn_ref, x1_ref, g2_ref, w1_ref, w3_ref, w2_ref, y_ref, h_scr):
    s = pl.program_id(1)

    @pl.when(s < nf)
    def _():
        h_scr[s] = _hidden_tile(xn_ref[...], w1_ref.at[0], w3_ref.at[0])

    @pl.when(s >= nf)
    def _():
        y = _down_tile(h_scr, w2_ref[0].astype(BF16), FFN_TM, FFN_TF)
        y_ref[...] = x1_ref[...] + g2_ref[0] * y


def _ffn_call(xn, x1, mod3, l, w1, w3, w2, i_layer):
    nf = D_FF // FFN_TF
    nn = D // FFN_TN
    hid = lambda i, s: (i_layer, 0, jnp.minimum(s, nf - 1))
    out = lambda s: jnp.maximum(s - nf, 0)

    def g2_idx(i, s):
        r = jnp.where(i < T_P // FFN_TM, 0, 1 + (i - T_P // FFN_TM) // (S_S // FFN_TM))
        return ((l * 8 + r) * 6 + 5, 0, out(s))

    return pl.pallas_call(
        functools.partial(_ffn_kernel, nf),
        grid=(T // FFN_TM, nf + nn),
        in_specs=[pl.BlockSpec((FFN_TM, D), lambda i, s: (i, 0)),
                  pl.BlockSpec((FFN_TM, FFN_TN), lambda i, s: (i, out(s))),
                  pl.BlockSpec((1, 1, FFN_TN), g2_idx),
                  pl.BlockSpec((1, D, FFN_TF), hid),
                  pl.BlockSpec((1, D, FFN_TF), hid),
                  pl.BlockSpec((1, D_FF, FFN_TN), lambda i, s: (i_layer, 0, out(s)))],
        out_specs=pl.BlockSpec((FFN_TM, FFN_TN), lambda i, s: (i, out(s))),
        out_shape=jax.ShapeDtypeStruct((T, D), F32),
        scratch_shapes=[pltpu.VMEM((nf, FFN_TM, FFN_TF), BF16)],
        compiler_params=_cparams(("arbitrary", "arbitrary")),
        name="ffn_dense",
    )(xn, x1, mod3, w1, w3, w2)


MOE_BM = 1024
MOE_SUB = 256
MOE_NBLK = 2 * T // MOE_BM + N_EXP
MOE_NR = MOE_NBLK * MOE_BM
MOE_TF = 512
MOE_TN = 256
FIN_TM = 512


def _route_kernel(i1_ref, i2_ref, r1_ref, r2_ref, cnt_ref, w1_ref, w2_ref,
                  src_ref, gate_ref, eb_ref, rows_ref, nvb_ref, off_scr):
    def clear(r, carry):
        src_ref[r] = 0
        gate_ref[r] = 0.0
        return carry

    lax.fori_loop(0, MOE_NR, clear, 0, unroll=8)

    def clear_blk(b, carry):
        eb_ref[b] = 0
        rows_ref[b] = 0
        return carry

    lax.fori_loop(0, MOE_NBLK, clear_blk, 0)

    nblk = jnp.int32(0)
    for e in range(N_EXP):
        c = cnt_ref[e]
        nbe = (c + (MOE_BM - 1)) // MOE_BM
        off_scr[e] = nblk * MOE_BM

        def fill(j, carry, e=e, c=c, nblk=nblk):
            eb_ref[nblk + j] = e
            rows_ref[nblk + j] = jnp.minimum(c - j * MOE_BM, MOE_BM)
            return carry

        lax.fori_loop(0, nbe, fill, 0)
        nblk = nblk + nbe
    nvb_ref[0] = nblk

    def place(t, carry):
        p1 = off_scr[i1_ref[t]] + r1_ref[t]
        src_ref[p1] = t
        gate_ref[p1] = w1_ref[t]
        w2 = w2_ref[t]

        @pl.when(w2 > 0.0)
        def _():
            p2 = off_scr[i2_ref[t]] + r2_ref[t]
            src_ref[p2] = t
            gate_ref[p2] = w2

        return carry

    lax.fori_loop(0, T, place, 0, unroll=4)


def _route_call(route, cnt):
    ints = [route[k].astype(jnp.int32) for k in range(4)]
    cnt8 = cnt[0, :N_EXP].astype(jnp.int32)
    smem = pl.BlockSpec(memory_space=pltpu.SMEM)
    return pl.pallas_call(
        _route_kernel,
        grid_spec=pltpu.PrefetchScalarGridSpec(
            num_scalar_prefetch=5, grid=(1,),
            in_specs=[smem, smem],
            out_specs=[smem, smem, smem, smem, smem],
            scratch_shapes=[pltpu.SMEM((N_EXP,), jnp.int32)]),
        out_shape=[jax.ShapeDtypeStruct((MOE_NR,), jnp.int32), jax.ShapeDtypeStruct((MOE_NR,), F32),
                   jax.ShapeDtypeStruct((MOE_NBLK,), jnp.int32), jax.ShapeDtypeStruct((MOE_NBLK,), jnp.int32),
                   jax.ShapeDtypeStruct((1,), jnp.int32)],
        compiler_params=_cparams(("arbitrary",)),
        name="moe_route",
    )(*ints, cnt8, route[4], route[5])


def _moe_gather_kernel(src_ref, rows_ref, x_ref, o_ref, rows_scr):
    k = pl.program_id(0)
    base = k * MOE_SUB
    used = rows_ref[k // (MOE_BM // MOE_SUB)] > (k % (MOE_BM // MOE_SUB)) * MOE_SUB

    @pl.when(used)
    def _():
        def body(i, carry):
            for j in range(8):
                r = i * 8 + j
                rows_scr[pl.ds(r, 1), :] = x_ref[pl.ds(src_ref[base + r], 1), :]
            return carry

        lax.fori_loop(0, MOE_SUB // 8, body, 0)
        o_ref[...] = rows_scr[...].astype(BF16)

    @pl.when(jnp.logical_not(used))
    def _():
        o_ref[...] = jnp.zeros(o_ref.shape, o_ref.dtype)


def _moe_gather_call(src, rows_b, xn):
    return pl.pallas_call(
        _moe_gather_kernel,
        grid_spec=pltpu.PrefetchScalarGridSpec(
            num_scalar_prefetch=2, grid=(MOE_NR // MOE_SUB,),
            in_specs=[pl.BlockSpec((T, D), lambda k, s, rw: (0, 0), pipeline_mode=pl.Buffered(1))],
            out_specs=pl.BlockSpec((MOE_SUB, D), lambda k, s, rw: (k, 0)),
            scratch_shapes=[pltpu.VMEM((MOE_SUB, D), F32)]),
        out_shape=jax.ShapeDtypeStruct((MOE_NR, D), BF16),
        compiler_params=_cparams(("arbitrary",)),
        name="moe_gather",
    )(src, rows_b, xn)


def _moe_kernel(nf, eb_ref, rows_ref, nvb_ref, xs_ref, w1_ref, w3_ref, w2_ref, y_ref, h_scr, w1b, w3b, w2b):
    b = pl.program_id(0)
    s = pl.program_id(1)
    nsub = jnp.where(b < nvb_ref[0], (rows_ref[b] + MOE_SUB - 1) // MOE_SUB, 0)
    sizes = [k * MOE_SUB for k in range(1, MOE_BM // MOE_SUB + 1)]

    @pl.when((s < nf) & (nsub > 0))
    def _():
        w1b[...] = w1_ref[0, 0].astype(BF16)
        w3b[...] = w3_ref[0, 0].astype(BF16)
        for k, m in enumerate(sizes, start=1):
            @pl.when(nsub == k)
            def _(m=m):
                h_scr[s, 0:m, :] = _hidden_tile(xs_ref[0:m, :], w1b, w3b)

    @pl.when((s >= nf) & (nsub > 0))
    def _():
        w2b[...] = w2_ref[0, 0].astype(BF16)
        for k, m in enumerate(sizes, start=1):
            @pl.when(nsub == k)
            def _(m=m):
                y_ref[0:m, :] = _down_tile(h_scr, w2b, m, MOE_TF)
                if m < MOE_BM:
                    y_ref[m:MOE_BM, :] = jnp.zeros((MOE_BM - m, MOE_TN), F32)

    @pl.when((s >= nf) & (nsub == 0))
    def _():
        y_ref[...] = jnp.zeros(y_ref.shape, F32)


def _moe_call(xs, e_b, rows_b, nvb, w1, w3, w2, i_layer):
    nf = D_FFE // MOE_TF
    nn = D // MOE_TN

    def blk(b, nv):
        return jnp.minimum(b, nv[0] - 1)

    def hid(b, s, eb, rw, nv):
        return (i_layer, eb[blk(b, nv)], 0, jnp.where(b < nv[0], jnp.minimum(s, nf - 1), nf - 1))

    def down(b, s, eb, rw, nv):
        return (i_layer, eb[blk(b, nv)], 0, jnp.where(b < nv[0], jnp.maximum(s - nf, 0), nn - 1))

    return pl.pallas_call(
        functools.partial(_moe_kernel, nf),
        grid_spec=pltpu.PrefetchScalarGridSpec(
            num_scalar_prefetch=3, grid=(MOE_NBLK, nf + nn),
            in_specs=[pl.BlockSpec((MOE_BM, D), lambda b, s, eb, rw, nv: (blk(b, nv), 0)),
                      pl.BlockSpec((1, 1, D, MOE_TF), hid),
                      pl.BlockSpec((1, 1, D, MOE_TF), hid),
                      pl.BlockSpec((1, 1, D_FFE, MOE_TN), down)],
            out_specs=pl.BlockSpec((MOE_BM, MOE_TN), lambda b, s, eb, rw, nv: (b, jnp.maximum(s - nf, 0))),
            scratch_shapes=[pltpu.VMEM((nf, MOE_BM, MOE_TF), BF16),
                            pltpu.VMEM((D, MOE_TF), BF16), pltpu.VMEM((D, MOE_TF), BF16),
                            pltpu.VMEM((D_FFE, MOE_TN), BF16)]),
        out_shape=jax.ShapeDtypeStruct((MOE_NR, D), F32),
        compiler_params=_cparams(("arbitrary", "arbitrary")),
        name="moe_experts",
    )(e_b, rows_b, nvb, xs, w1, w3, w2)


def _moe_combine_kernel(src_ref, gate_ref, rows_ref, nvb_ref, ys_ref, x1_ref, g2_ref, yp_ref, ys_out_ref, acc):
    i = pl.program_id(0)

    @pl.when(i == 0)
    def _():
        acc[...] = jnp.zeros(acc.shape, F32)

    @pl.when(i < nvb_ref[0])
    def _():
        base = i * MOE_BM
        n = rows_ref[i]

        def add_rows(r0, cnt):
            toks = [src_ref[base + r0 + j] for j in range(cnt)]
            vals = [acc[pl.ds(toks[j], 1), :] + gate_ref[base + r0 + j] * ys_ref[pl.ds(r0 + j, 1), :]
                    for j in range(cnt)]
            for j in range(cnt):
                acc[pl.ds(toks[j], 1), :] = vals[j]

        def body4(q, carry):
            add_rows(q * 4, 4)
            return carry

        lax.fori_loop(0, n // 4, body4, 0)

        def body1(r, carry):
            add_rows(r, 1)
            return carry

        lax.fori_loop((n // 4) * 4, n, body1, 0)

    @pl.when(i >= MOE_NBLK)
    def _():
        t0 = pl.multiple_of((i - MOE_NBLK) * FIN_TM, FIN_TM)
        y = x1_ref[...] + g2_ref[0] * acc[pl.ds(t0, FIN_TM), :]

        @pl.when(i - MOE_NBLK < T_P // FIN_TM)
        def _():
            yp_ref[...] = y

        @pl.when(i - MOE_NBLK >= T_P // FIN_TM)
        def _():
            ys_out_ref[...] = y


def _moe_combine_call(src, gate_r, rows_b, nvb, ys, x1, mod3, l):
    nfin = T // FIN_TM

    def g2_idx(i, *_):
        j = jnp.maximum(i - MOE_NBLK, 0)
        r = jnp.where(j < T_P // FIN_TM, 0, 1 + (j - T_P // FIN_TM) // (S_S // FIN_TM))
        return ((l * 8 + r) * 6 + 5, 0, 0)

    return pl.pallas_call(
        _moe_combine_kernel,
        grid_spec=pltpu.PrefetchScalarGridSpec(
            num_scalar_prefetch=4, grid=(MOE_NBLK + nfin,),
            in_specs=[pl.BlockSpec((MOE_BM, D),
                                   lambda i, s, g, rw, nv: (jnp.minimum(jnp.minimum(i, MOE_NBLK - 1), nv[0] - 1), 0)),
                      pl.BlockSpec((FIN_TM, D), lambda i, s, g, rw, nv: (jnp.maximum(i - MOE_NBLK, 0), 0)),
                      pl.BlockSpec((1, 1, D), g2_idx)],
            out_specs=[pl.BlockSpec((FIN_TM, D),
                                    lambda i, s, g, rw, nv: (jnp.clip(i - MOE_NBLK, 0, T_P // FIN_TM - 1), 0)),
                       pl.BlockSpec((FIN_TM, D),
                                    lambda i, s, g, rw, nv: (jnp.maximum(i - MOE_NBLK - T_P // FIN_TM, 0), 0))],
            scratch_shapes=[pltpu.VMEM((T, D), F32)]),
        out_shape=[jax.ShapeDtypeStruct((T_P, D), F32), jax.ShapeDtypeStruct((T_S, D), F32)],
        compiler_params=_cparams(("arbitrary",), vmem=52 * 1024 * 1024),
        name="moe_combine",
    )(src, gate_r, rows_b, nvb, ys, x1, mod3)


def _moe_layer(xp, x1, route, cnt, mod3, l, w1, w3, w2, i_layer):
    src, gate_r, e_b, rows_b, nvb = _route_call(route, cnt)
    xs = _moe_gather_call(src, rows_b, xp)
    ys = _moe_call(xs, e_b, rows_b, nvb, w1, w3, w2, i_layer)
    return _moe_combine_call(src, gate_r, rows_b, nvb, ys, x1, mod3, l)


def _pad_cols(w, n):
    return jnp.pad(w, ((0, 0), (0, n - w.shape[1])))


def _relayout_w_in(w):
    aq, ak, av, bq, bk, bv, bo, bg, cq, ckv, ckr = jnp.split(
        w, (384, 512, 640, 896, 1152, 1408, 1664, 1680, 1936, 2064), axis=1)
    dup = lambda m: jnp.concatenate([m[:, 0:HD], m[:, 0:HD], m[:, HD:2 * HD], m[:, HD:2 * HD]], axis=1)
    g = bg.reshape(D, 2, 2, 2, 2)

    def gate_cols(gi):
        cols = []
        for j in range(2):
            cols.append(_pad_cols(g[:, :, gi, j, :].reshape(D, 4), LANES))
        return jnp.concatenate(cols, axis=1)

    ckr_p = jnp.pad(ckr, ((0, 0), (C_NOPE, LANES - C_NOPE - C_ROPE)))
    out = jnp.concatenate([aq, dup(ak), dup(av), bq, bk, bv, bo, gate_cols(0), gate_cols(1),
                           cq, ckv, ckr_p], axis=1)
    return out.astype(BF16)


def _relayout_gate_b(b):
    g = b.reshape(2, 2, 2, 2)

    def cols(gi):
        return jnp.concatenate([jnp.pad(g[:, gi, j, :].reshape(4), (0, LANES - 4)) for j in range(2)])

    return jnp.concatenate([cols(0), cols(1)])[None, :]


def _pad_heads(w, width):
    r = w.shape[0]
    h = w.shape[1] // width
    return jnp.pad(w.reshape(r, h, width), ((0, 0), (0, 0), (0, LANES - width))).reshape(r, h * LANES)


def _rope_tables(half, span_start, period):
    rows = S_S // 64
    r = jnp.repeat(jnp.arange(rows), 64).astype(F32)
    c = jnp.tile(jnp.arange(64), rows).astype(F32)
    n_freq = half // 2
    freq = 10000.0 ** (-jnp.arange(n_freq, dtype=F32) / n_freq)
    ang = jnp.concatenate([r[:, None] * freq, c[:, None] * freq], axis=-1)
    cos, sin = jnp.cos(ang), jnp.sin(ang)
    d = (jnp.arange(LANES) - span_start) % period
    inside = d < 2 * half
    p = jnp.where(inside, d % half, 0)
    first = inside & (d < half)
    second = inside & (d >= half)
    ct = jnp.where(inside[None, :], cos[:, p], 1.0)
    sa = jnp.where(first[None, :], -sin[:, p], 0.0)
    sb = jnp.where(second[None, :], sin[:, p], 0.0)
    ident = (jnp.ones((TM, LANES), F32), jnp.zeros((TM, LANES), F32), jnp.zeros((TM, LANES), F32))
    return tuple(jnp.concatenate([i0, t], axis=0) for i0, t in zip(ident, (ct, sa, sb)))


def kernel(x_prompt, x_sample, c, cache_swa_k, cache_swa_v, cache_mla_ckv, cache_mla_krope, state_mlstm_C, state_mlstm_n, state_mlstm_m, c_ctx, ada_w, ada_b, norm1_g, norm2_g, w_in, a_qn_g, a_kn_g, a_sink, b_gate_b, b_hn_g, c_qa_g, c_kva_g, c_wuq, c_wukv, c_qn_g, c_kn_g, w_out, ffn_w1, ffn_w3, ffn_w2, moe_router, moe_w1, moe_w3, moe_w2):
    x_p, x_s, sample_off = x_prompt.reshape(T_P, D), x_sample.reshape(T_S, D), 0
    cv = jnp.concatenate([c_ctx[None, :], c, jnp.zeros((5, D), F32)], axis=0)
    mod = _ada_call(cv, ada_w, ada_b)
    mod3 = mod.reshape(DEPTH * 8 * 6, 1, D)

    rope_a = _rope_tables(32, 0, HD)
    rope_c = _rope_tables(16, C_NOPE, LANES)

    w_in_p = jnp.stack([_relayout_w_in(w_in[l]) for l in range(DEPTH)])
    wuq_p = jnp.stack([_pad_heads(c_wuq[l], C_QK) for l in range(DEPTH)]).astype(BF16)
    wukv = c_wukv.reshape(DEPTH, C_KVRANK, C_HEADS, C_NOPE + C_V)
    wk_p = jnp.pad(wukv[..., :C_NOPE], ((0, 0), (0, 0), (0, 0), (0, LANES - C_NOPE)))
    wk_p = wk_p.reshape(DEPTH, C_KVRANK, C_HEADS * LANES).astype(BF16)
    wv_p = wukv[..., C_NOPE:].reshape(DEPTH, C_KVRANK, C_HEADS * C_V).astype(BF16)
    w_out_b = w_out.astype(BF16)
    cache_k4 = cache_swa_k.reshape(NB_S, DEPTH, PAST, A_KV * HD)
    cache_v4 = cache_swa_v.reshape(NB_S, DEPTH, PAST, A_KV * HD)

    news = []
    for l in range(DEPTH):
        gqa = jnp.tile(a_qn_g[l], A_HEADS)[None, :]
        gka = jnp.tile(a_kn_g[l], 2 * A_KV)[None, :]
        gb = _relayout_gate_b(b_gate_b[l])
        gqn = jnp.tile(jnp.pad(c_qn_g[l], (0, LANES - C_QK)), C_HEADS)[None, :]
        gkn = jnp.tile(jnp.pad(c_kn_g[l], (0, LANES - C_QK)), C_HEADS)[None, :]
        (qa, ka, va, kaf, vaf, bq, bk, bv, bo, li, lf, qc, ckv_n, ckr) = _in_call(
            x_p, x_s, sample_off, mod3, l, norm1_g[l][None, :], w_in_p, gqa, gka, gb, c_qa_g[l][None, :],
            c_kva_g[l][None, :], wuq_p, gqn, rope_a, rope_c)

        ckr_cache = jnp.pad(cache_mla_krope[:, l].reshape(NB_S * PAST, C_ROPE),
                            ((0, 0), (C_NOPE, LANES - C_NOPE - C_ROPE)))
        kc, vc = _mlakv_call(ckv_n, cache_mla_ckv[:, l].reshape(NB_S * PAST, C_KVRANK), ckr, ckr_cache,
                             l, wk_p, wv_p, gkn, rope_c)

        sink = a_sink[l]
        oa_p = _attn_a_prompt_call(sink, qa, ka, va)
        oa_s = _attn_a_sample_call(sink, qa, ka, va, cache_k4, cache_v4, l)
        oc_p = _mla_prompt_call(qc, kc, vc)
        oc_s = _mla_sample_call(qc, kc, vc)

        hn2 = jnp.tile(b_hn_g[l], 2)[None, :]
        ob_p, cn, nn, mn = _mlstm_call(bq, bk, bv, bo, li, lf, hn2, None, l)
        c_st = state_mlstm_C[:, l].reshape(NB_S, 2, 2, 2, HD, HD)
        c_st = jnp.transpose(c_st, (0, 2, 1, 3, 4, 5))
        c0 = jnp.zeros((NB_S, 2, 2, LANES, LANES), F32)
        c0 = c0.at[..., :HD, :HD].set(c_st[:, :, :, 0]).at[..., HD:, HD:].set(c_st[:, :, :, 1])
        c0 = c0.reshape(NB_S * 2, 2, LANES, LANES)
        n_st = state_mlstm_n[:, l].reshape(NB_S, 2, 2, 1, LANES)
        n0 = jnp.transpose(n_st, (0, 2, 1, 3, 4)).reshape(NB_S * 2, 2, 1, LANES)
        m_st = state_mlstm_m[:, l].reshape(NB_S, 2, 2, 2)
        m0 = jnp.transpose(m_st, (0, 2, 1, 3)).reshape(NB_S * 2, 1, 4)
        m0 = jnp.pad(m0, ((0, 0), (0, 0), (0, LANES - 4)))
        ob_s = _mlstm_call(bq, bk, bv, bo, li, lf, hn2, (c0, n0, m0), l)

        oa, ob, oc = (oa_p, oa_s), (ob_p, ob_s), (oc_p, oc_s)

        if l % 2 == 0:
            x1, xn = _out_call(x_p, x_s, sample_off, oa, ob, oc, w_out_b, mod3, l, norm2_g[l][None, :], None)
            x = _ffn_call(xn, x1, mod3, l, ffn_w1, ffn_w3, ffn_w2, l // 2)
            x_p, x_s, sample_off = x, x, NT_P
        else:
            r = _pad_cols(moe_router[l // 2], LANES)
            rh = r.astype(BF16)
            rl = (r - rh.astype(F32)).astype(BF16)
            x1, xn, route, cnt = _out_call(x_p, x_s, sample_off, oa, ob, oc, w_out_b, mod3, l,
                                           norm2_g[l][None, :], (rh[None], rl[None]))
            x_p, x_s = _moe_layer(xn, x1, route, cnt, mod3, l, moe_w1, moe_w3, moe_w2, l // 2)
            sample_off = 0

        new_k = kaf.reshape(NB_P, S_P, A_KV, HD)
        new_v = vaf.reshape(NB_P, S_P, A_KV, HD)
        new_ckv = ckv_n[:T_P].reshape(NB_P, S_P, C_KVRANK)
        new_kr = ckr[:T_P, C_NOPE:C_NOPE + C_ROPE].reshape(NB_P, S_P, C_ROPE)
        cn6 = cn.reshape(NB_P, 2, 2, LANES, LANES)
        c_e = jnp.stack([cn6[..., :HD, :HD], cn6[..., HD:, HD:]], axis=3)
        new_c = jnp.transpose(c_e, (0, 2, 1, 3, 4, 5)).reshape(NB_P, 2, B_HEADS, HD, HD)
        nn5 = nn.reshape(NB_P, 2, 2, 2, HD)
        new_n = jnp.transpose(nn5, (0, 2, 1, 3, 4)).reshape(NB_P, 2, B_HEADS, HD)
        mn4 = mn.reshape(NB_P, 2, LANES)[:, :, :4].reshape(NB_P, 2, 2, 2)
        new_m = jnp.transpose(mn4, (0, 2, 1, 3)).reshape(NB_P, 2, B_HEADS)
        news.append((new_k, new_v, new_ckv, new_kr, new_c, new_n, new_m))

    y_prompt = x_p[:T_P].reshape(NB_P, S_P, D)
    y_sample = x_s[sample_off * TM:sample_off * TM + T_S].reshape(NB_S, S_S, D)
    stacked = tuple(jnp.stack([nw[j] for nw in news], axis=1) for j in range(7))
    return (y_prompt, y_sample) + stacked
```

```python
import functools

import jax
import jax.numpy as jnp
from jax import lax
from jax.experimental import pallas as pl
from jax.experimental.pallas import tpu as pltpu

F32 = jnp.float32
BF16 = jnp.bfloat16

D = 1024
NB_P, S_P = 16, 256
NB_S, S_S = 2, 1024
PAST = 512
DEPTH = 2
T_P = NB_P * S_P
T_S = NB_S * S_S
T = T_P + T_S
TM = 256
NT = T // TM
NT_P = T_P // TM
HD = 64
A_HEADS, A_KV = 6, 2
B_HEADS = 4
CHUNK = 64
C_HEADS = 6
C_QRANK, C_KVRANK, C_NOPE, C_ROPE, C_V = 256, 128, 64, 32, 64
C_QK = C_NOPE + C_ROPE
D_FF = 2816
N_EXP = 8
D_FFE = 3584
EPS = 1e-6
NEG = -1e30
LANES = 128
VMEM_LIMIT = 48 * 1024 * 1024

SEG = dict(QA=(0, 384), KA=(384, 256), VA=(640, 256), BQ=(896, 256), BK=(1152, 256),
           BV=(1408, 256), BO=(1664, 256), GI=(1920, 256), GF=(2176, 256), CQ=(2432, 256),
           CKV=(2688, 128), CKR=(2816, 128))
NP_IN = 2944


def _cparams(sem, vmem=VMEM_LIMIT):
    return pltpu.CompilerParams(dimension_semantics=sem, vmem_limit_bytes=vmem)


def _dot(a, b):
    return jnp.dot(a, b, preferred_element_type=F32)


def _dot_nt(a, b):
    return lax.dot_general(a, b, (((1,), (1,)), ((), ())), preferred_element_type=F32)


def _lane(shape):
    return lax.broadcasted_iota(jnp.int32, shape, len(shape) - 1)


def _mod_row(i):
    return jnp.where(i < NT_P, 0, 1 + (i - NT_P) // (S_S // TM))


def _pair_specs(n, sample_off):
    return [pl.BlockSpec((TM, n), lambda i: (jnp.minimum(i, NT_P - 1), 0)),
            pl.BlockSpec((TM, n), lambda i: (sample_off + jnp.maximum(i - NT_P, 0), 0))]


def _pick(i, p_ref, s_ref):
    return jnp.where(i < NT_P, p_ref[...], s_ref[...])


def _rope_blk(i):
    return jnp.where((i >= NT_P) & (i < NT), 1 + (i - NT_P) % (S_S // TM), 0)


def _ada_kernel(cv_ref, w_ref, b_ref, o_ref):
    s = cv_ref[...]
    s = s * jax.nn.sigmoid(s)
    o_ref[0] = _dot(s.astype(BF16), w_ref[0].astype(BF16)) + b_ref[0]


def _ada_call(cv, ada_w, ada_b):
    tn = 1536
    return pl.pallas_call(
        _ada_kernel,
        grid=(DEPTH, 6 * D // tn),
        in_specs=[pl.BlockSpec((8, D), lambda l, j: (0, 0)),
                  pl.BlockSpec((1, D, tn), lambda l, j: (l, 0, j)),
                  pl.BlockSpec((1, 1, tn), lambda l, j: (l, 0, j))],
        out_specs=pl.BlockSpec((1, 8, tn), lambda l, j: (l, 0, j)),
        out_shape=jax.ShapeDtypeStruct((DEPTH, 8, 6 * D), F32),
        compiler_params=_cparams(("arbitrary", "arbitrary")),
        name="ada_mod",
    )(cv, ada_w, ada_b.reshape(DEPTH, 1, 6 * D))


def _half_norm(x, g, n):
    outs = []
    for j in range(x.shape[1] // LANES):
        xj = x[:, j * LANES:(j + 1) * LANES]
        lo = _lane(xj.shape) < HD
        s = xj * xj
        s_lo = jnp.sum(jnp.where(lo, s, 0.0), axis=-1, keepdims=True)
        s_hi = jnp.sum(jnp.where(lo, 0.0, s), axis=-1, keepdims=True)
        r = lax.rsqrt(jnp.where(lo, s_lo, s_hi) * (1.0 / n) + EPS)
        outs.append(xj * r)
    return jnp.concatenate(outs, axis=-1) * g


def _group_norm(x, g, n):
    outs = []
    for j in range(x.shape[1] // LANES):
        xj = x[:, j * LANES:(j + 1) * LANES]
        r = lax.rsqrt(jnp.sum(xj * xj, axis=-1, keepdims=True) * (1.0 / n) + EPS)
        outs.append(xj * r)
    return jnp.concatenate(outs, axis=-1) * g


def _rope(x, c, sa, sb, shift):
    outs = []
    for j in range(x.shape[1] // LANES):
        xj = x[:, j * LANES:(j + 1) * LANES]
        outs.append(xj * c + pltpu.roll(xj, LANES - shift, 1) * sa + pltpu.roll(xj, shift, 1) * sb)
    return jnp.concatenate(outs, axis=-1)


def _in_kernel(xp_ref, xs_ref, sh_ref, sc_ref, n1_ref, w_ref, gqa_ref, gka_ref, gb_ref, gcq_ref, gckv_ref,
               wuq_ref, gqn_ref, ra_c, ra_a, ra_b, rc_c, rc_a, rc_b,
               qa_o, ka_o, va_o, kaf_o, vaf_o, bq_o, bk_o, bv_o, bo_o, li_o, lf_o, qc_o, ckv_o, ckr_o):
    i = pl.program_id(0)
    x = _pick(i, xp_ref, xs_ref)
    xn = x * lax.rsqrt(jnp.mean(x * x, axis=-1, keepdims=True) + EPS) * n1_ref[...]
    xn = xn * (1.0 + sc_ref[0]) + sh_ref[0]
    p = _dot(xn.astype(BF16), w_ref[0])

    def seg(name):
        o, n = SEG[name]
        return p[:, o:o + n]

    qa = _half_norm(seg("QA"), gqa_ref[...], HD)
    ka = _half_norm(seg("KA"), gka_ref[...], HD)
    va = seg("VA")

    @pl.when(i < NT_P)
    def _():
        kaf_o[...] = jnp.concatenate([ka[:, 0:HD], ka[:, LANES:LANES + HD]], axis=-1)
        vaf_o[...] = jnp.concatenate([va[:, 0:HD], va[:, LANES:LANES + HD]], axis=-1)

    va_o[...] = va.astype(BF16)

    bq_o[...] = seg("BQ").astype(BF16)
    bk_o[...] = (seg("BK") * (HD ** -0.5)).astype(BF16)
    bv_o[...] = seg("BV").astype(BF16)
    bo_o[...] = seg("BO")
    gb = gb_ref[...]
    li_o[...] = seg("GI") + gb[:, 0:256]
    lf_o[...] = jax.nn.log_sigmoid(seg("GF") + gb[:, 256:512])

    cq = seg("CQ")
    cqn = cq * lax.rsqrt(jnp.mean(cq * cq, axis=-1, keepdims=True) + EPS) * gcq_ref[...]
    qc = _group_norm(_dot(cqn.astype(BF16), wuq_ref[0]), gqn_ref[...], C_QK)

    @pl.when(i < NT_P)
    def _():
        qa_o[...] = qa.astype(BF16)
        ka_o[...] = ka.astype(BF16)
        qc_o[...] = qc.astype(BF16)

    @pl.when(i >= NT_P)
    def _():
        qa_o[...] = _rope(qa, ra_c[...], ra_a[...], ra_b[...], 32).astype(BF16)
        ka_o[...] = _rope(ka, ra_c[...], ra_a[...], ra_b[...], 32).astype(BF16)
        qc_o[...] = _rope(qc, rc_c[...], rc_a[...], rc_b[...], 16).astype(BF16)

    ckv = seg("CKV")
    ckv_o[...] = ckv * lax.rsqrt(jnp.mean(ckv * ckv, axis=-1, keepdims=True) + EPS) * gckv_ref[...]
    ckr_o[...] = seg("CKR")


def _in_call(x_p, x_s, sample_off, mod3, l, n1g, w_in_p, gqa, gka, gb, gcq, gckv, wuq_p, gqn, rope_a, rope_c):
    row = lambda k: pl.BlockSpec((1, 1, D), lambda i: ((l * 8 + _mod_row(i)) * 6 + k, 0, 0))
    vec = lambda n: pl.BlockSpec((1, n), lambda i: (0, 0))
    tab = pl.BlockSpec((TM, LANES), lambda i: (_rope_blk(i), 0))
    tok = lambda n: pl.BlockSpec((TM, n), lambda i: (i, 0))
    tokp = pl.BlockSpec((TM, LANES), lambda i: (jnp.minimum(i, NT_P - 1), 0))
    o = lambda n, dt: jax.ShapeDtypeStruct((T, n), dt)
    return pl.pallas_call(
        _in_kernel,
        grid=(NT,),
        in_specs=_pair_specs(D, sample_off) + [row(0), row(1), vec(D),
                  pl.BlockSpec((1, D, NP_IN), lambda i: (l, 0, 0)),
                  vec(384), vec(256), vec(512), vec(256), vec(128),
                  pl.BlockSpec((1, C_QRANK, 768), lambda i: (l, 0, 0)), vec(768),
                  tab, tab, tab, tab, tab, tab],
        out_specs=[tok(384), tok(256), tok(256), tokp, tokp, tok(256), tok(256), tok(256), tok(256),
                   tok(256), tok(256), tok(768), tok(128), tok(128)],
        out_shape=[o(384, BF16), o(256, BF16), o(256, BF16),
                   jax.ShapeDtypeStruct((T_P, LANES), F32), jax.ShapeDtypeStruct((T_P, LANES), F32),
                   o(256, BF16), o(256, BF16), o(256, BF16), o(256, F32),
                   o(256, F32), o(256, F32), o(768, BF16), o(128, F32), o(128, F32)],
        compiler_params=_cparams(("arbitrary",)),
        name=f"in_proj_l{l}",
    )(x_p, x_s, mod3, mod3, n1g, w_in_p, gqa, gka, gb, gcq, gckv, wuq_p, gqn, *rope_a, *rope_c)


def _mlakv_kernel(ckv_ref, ckvc_ref, ckr_ref, ckrc_ref, wk_ref, wv_ref, gkn_ref, rc_c, rc_a, rc_b, k_o, v_o):
    i = pl.program_id(0)
    tok = i < NT
    c = jnp.where(tok, ckv_ref[...], ckvc_ref[...]).astype(BF16)
    k = _dot(c, wk_ref[0])
    kr = jnp.where(tok, ckr_ref[...], ckrc_ref[...])
    k = k + jnp.concatenate([kr] * C_HEADS, axis=-1)
    k = _group_norm(k, gkn_ref[...], C_QK)
    sample = (i >= NT_P) & tok

    @pl.when(sample)
    def _():
        k_o[...] = _rope(k, rc_c[...], rc_a[...], rc_b[...], 16).astype(BF16)

    @pl.when(jnp.logical_not(sample))
    def _():
        k_o[...] = k.astype(BF16)

    v_o[...] = _dot(c, wv_ref[0]).astype(BF16)


def _mlakv_call(ckv_n, ckv_cache, ckr, ckr_cache, l, wk_p, wv_p, gkn, rope_c):
    r = T + NB_S * PAST
    tab = pl.BlockSpec((TM, LANES), lambda i: (_rope_blk(i), 0))
    tok = lambda n: pl.BlockSpec((TM, n), lambda i: (i, 0))
    tokens = pl.BlockSpec((TM, LANES), lambda i: (jnp.minimum(i, NT - 1), 0))
    cached = pl.BlockSpec((TM, LANES), lambda i: (jnp.maximum(i - NT, 0), 0))
    return pl.pallas_call(
        _mlakv_kernel,
        grid=(r // TM,),
        in_specs=[tokens, cached, tokens, cached,
                  pl.BlockSpec((1, C_KVRANK, 768), lambda i: (l, 0, 0)),
                  pl.BlockSpec((1, C_KVRANK, 384), lambda i: (l, 0, 0)),
                  pl.BlockSpec((1, 768), lambda i: (0, 0)), tab, tab, tab],
        out_specs=[tok(768), tok(384)],
        out_shape=[jax.ShapeDtypeStruct((r, 768), BF16), jax.ShapeDtypeStruct((r, 384), BF16)],
        compiler_params=_cparams(("arbitrary",)),
        name=f"mla_kv_l{l}",
    )(ckv_n, ckv_cache, ckr, ckr_cache, wk_p, wv_p, gkn, *rope_c)


def _softmax_pv(scores, values, sink):
    m = scores[0].max(axis=-1, keepdims=True)
    for s in scores[1:]:
        m = jnp.maximum(m, s.max(axis=-1, keepdims=True))
    if sink is not None:
        m = jnp.maximum(m, sink)
    es = [jnp.exp(s - m) for s in scores]
    den = es[0].sum(axis=-1, keepdims=True)
    for e in es[1:]:
        den = den + e.sum(axis=-1, keepdims=True)
    if sink is not None:
        den = den + jnp.exp(sink - m)
    inv = 1.0 / den
    out = None
    for e, v in zip(es, values):
        o = _dot((e * inv).astype(BF16), v)
        out = o if out is None else out + o
    return out


def _half(x, hi):
    lo = _lane(x.shape) < HD
    return jnp.where(lo != hi, x, jnp.zeros_like(x))


def _attn_a_prompt_kernel(sink_ref, q_ref, k_ref, v_ref, o_ref):
    q = q_ref[...]
    k = k_ref[...]
    v = v_ref[...]
    for j in range(A_HEADS // 2):
        acc = None
        for c in range(2):
            h = 2 * j + c
            g = h // (A_HEADS // A_KV)
            qh = _half(q[:, j * LANES:(j + 1) * LANES], c == 1)
            s = _dot_nt(qh, k[:, g * LANES:(g + 1) * LANES]) * (HD ** -0.5)
            vh = _half(v[:, g * LANES:(g + 1) * LANES], c == 1)
            o = _softmax_pv([s], [vh], sink_ref[h])
            acc = o if acc is None else acc + o
        o_ref[:, j * LANES:(j + 1) * LANES] = acc.astype(BF16)


def _attn_a_prompt_call(sink, qa, ka, va):
    return pl.pallas_call(
        _attn_a_prompt_kernel,
        grid_spec=pltpu.PrefetchScalarGridSpec(
            num_scalar_prefetch=1, grid=(NB_P,),
            in_specs=[pl.BlockSpec((S_P, 384), lambda b, s: (b, 0)),
                      pl.BlockSpec((S_P, 256), lambda b, s: (b, 0)),
                      pl.BlockSpec((S_P, 256), lambda b, s: (b, 0))],
            out_specs=pl.BlockSpec((S_P, 384), lambda b, s: (b, 0))),
        out_shape=jax.ShapeDtypeStruct((T_P, 384), BF16),
        compiler_params=_cparams(("arbitrary",)),
        name="attn_a_prompt",
    )(sink, qa, ka, va)


QB = 128
WIN = 128
BAND = QB + 2 * WIN


def _attn_a_sample_kernel(sink_ref, q_ref, k_ref, v_ref, kc_ref, vc_ref, o_ref):
    n = pl.program_id(1)
    ws = pl.multiple_of(jnp.clip(n * QB - WIN, 0, S_S - BAND), QB)
    q = q_ref[...]
    kb = k_ref[pl.ds(ws, BAND), :]
    vb = v_ref[pl.ds(ws, BAND), :]
    kc = kc_ref[0, 0].astype(BF16)
    vc = vc_ref[0, 0].astype(BF16)
    qpos = n * QB + lax.broadcasted_iota(jnp.int32, (QB, BAND), 0)
    kpos = ws + lax.broadcasted_iota(jnp.int32, (QB, BAND), 1)
    ok = jnp.abs(qpos - kpos) <= WIN
    lane = _lane((PAST, LANES))
    for j in range(A_HEADS // 2):
        acc = None
        for c in range(2):
            h = 2 * j + c
            g = h // (A_HEADS // A_KV)
            qh = _half(q[:, j * LANES:(j + 1) * LANES], c == 1)
            s_b = jnp.where(ok, _dot_nt(qh, kb[:, g * LANES:(g + 1) * LANES]) * (HD ** -0.5), NEG)
            kcg = jnp.where((lane < HD) == (g == 0), kc, jnp.zeros_like(kc))
            vcg = jnp.where((lane < HD) == (g == 0), vc, jnp.zeros_like(vc))
            if (g == 1) != (c == 1):
                kcg = pltpu.roll(kcg.astype(F32), HD, 1).astype(BF16)
                vcg = pltpu.roll(vcg.astype(F32), HD, 1).astype(BF16)
            s_c = _dot_nt(qh, kcg) * (HD ** -0.5)
            vh = _half(vb[:, g * LANES:(g + 1) * LANES], c == 1)
            o = _softmax_pv([s_b, s_c], [vh, vcg], sink_ref[h])
            acc = o if acc is None else acc + o
        o_ref[:, j * LANES:(j + 1) * LANES] = acc.astype(BF16)


def _attn_a_sample_call(sink, qa, ka, va, cache_k, cache_v, l):
    nqb = S_S // QB
    off = T_P // S_S
    return pl.pallas_call(
        _attn_a_sample_kernel,
        grid_spec=pltpu.PrefetchScalarGridSpec(
            num_scalar_prefetch=1, grid=(NB_S, nqb),
            in_specs=[pl.BlockSpec((QB, 384), lambda b, n, s: (T_P // QB + b * nqb + n, 0)),
                      pl.BlockSpec((S_S, 256), lambda b, n, s: (off + b, 0)),
                      pl.BlockSpec((S_S, 256), lambda b, n, s: (off + b, 0)),
                      pl.BlockSpec((1, 1, PAST, LANES), lambda b, n, s: (b, l, 0, 0)),
                      pl.BlockSpec((1, 1, PAST, LANES), lambda b, n, s: (b, l, 0, 0))],
            out_specs=pl.BlockSpec((QB, 384), lambda b, n, s: (b * nqb + n, 0))),
        out_shape=jax.ShapeDtypeStruct((T_S, 384), BF16),
        compiler_params=_cparams(("arbitrary", "arbitrary")),
        name=f"attn_a_sample_l{l}",
    )(sink, qa, ka, va, cache_k, cache_v)


def _mla_heads(q, ks, vs, o_ref):
    for j in range(C_HEADS // 2):
        acc = None
        for c in range(2):
            h = 2 * j + c
            qh = q[:, h * LANES:(h + 1) * LANES]
            scores = [_dot_nt(qh, k[:, h * LANES:(h + 1) * LANES]) * (C_QK ** -0.5) for k in ks]
            vals = [_half(v[:, j * LANES:(j + 1) * LANES], c == 1) for v in vs]
            o = _softmax_pv(scores, vals, None)
            acc = o if acc is None else acc + o
        o_ref[:, j * LANES:(j + 1) * LANES] = acc.astype(BF16)


def _mla_prompt_kernel(q_ref, k_ref, v_ref, o_ref):
    _mla_heads(q_ref[...], [k_ref[...]], [v_ref[...]], o_ref)


def _mla_prompt_call(qc, kc, vc):
    return pl.pallas_call(
        _mla_prompt_kernel,
        grid=(NB_P,),
        in_specs=[pl.BlockSpec((S_P, 768), lambda b: (b, 0)),
                  pl.BlockSpec((S_P, 768), lambda b: (b, 0)),
                  pl.BlockSpec((S_P, 384), lambda b: (b, 0))],
        out_specs=pl.BlockSpec((S_P, 384), lambda b: (b, 0)),
        out_shape=jax.ShapeDtypeStruct((T_P, 384), BF16),
        compiler_params=_cparams(("arbitrary",)),
        name="mla_prompt",
    )(qc, kc, vc)


def _mla_sample_kernel(q_ref, kc_ref, vc_ref, kl_ref, vl_ref, o_ref):
    _mla_heads(q_ref[...], [kc_ref[...], kl_ref[...]], [vc_ref[...], vl_ref[...]], o_ref)


def _mla_sample_call(qc, kc, vc):
    tq = 256
    nq = S_S // tq
    return pl.pallas_call(
        _mla_sample_kernel,
        grid=(NB_S, nq),
        in_specs=[pl.BlockSpec((tq, 768), lambda b, n: (T_P // tq + b * nq + n, 0)),
                  pl.BlockSpec((PAST, 768), lambda b, n: (T // PAST + b, 0)),
                  pl.BlockSpec((PAST, 384), lambda b, n: (T // PAST + b, 0)),
                  pl.BlockSpec((S_S, 768), lambda b, n: (T_P // S_S + b, 0)),
                  pl.BlockSpec((S_S, 384), lambda b, n: (T_P // S_S + b, 0))],
        out_specs=pl.BlockSpec((tq, 384), lambda b, n: (b * nq + n, 0)),
        out_shape=jax.ShapeDtypeStruct((T_S, 384), BF16),
        compiler_params=_cparams(("arbitrary", "arbitrary")),
        name="mla_sample",
    )(qc, kc, vc, kc, vc)


def _split3(x):
    x1 = x.astype(BF16)
    r = x - x1.astype(F32)
    x2 = r.astype(BF16)
    x3 = (r - x2.astype(F32)).astype(BF16)
    return x1, x2, x3


def _bmm(a, b):
    return lax.dot_general(a, b, (((2,), (1,)), ((0,), (0,))), preferred_element_type=F32)


def _bmm_nt(a, b):
    return lax.dot_general(a, b, (((2,), (2,)), ((0,), (0,))), preferred_element_type=F32)


def _bmm_tn(a, b):
    return lax.dot_general(a, b, (((1,), (1,)), ((0,), (0,))), preferred_element_type=F32)


def _mlstm_kernel(nc, has_state, *refs):
    if has_state:
        (q_ref, k_ref, v_ref, bo_ref, li_ref, lf_ref, hn_ref, c0_ref, n0_ref, m0_ref, ob_ref) = refs
    else:
        (q_ref, k_ref, v_ref, bo_ref, li_ref, lf_ref, hn_ref, ob_ref, cn_ref, nn_ref, mn_ref) = refs
    s_len = nc * CHUNK
    c3 = lambda x: x.reshape(nc, CHUNK, x.shape[-1])
    li3 = c3(li_ref[...])
    lf3 = c3(lf_ref[...])
    row = lax.broadcasted_iota(jnp.int32, (CHUNK, CHUNK), 0)
    colv = lax.broadcasted_iota(jnp.int32, (CHUNK, CHUNK), 1)
    tri_f = colv <= row
    tri_b = colv >= row
    bcast = lambda m: jnp.broadcast_to(m[None], (nc, CHUNK, CHUNK))
    lane1 = _lane((1, LANES))
    fwd_lane = lane1 < 2

    lf_parts = _split3(lf3)
    tf3 = bcast(tri_f.astype(BF16))
    tb3 = bcast(tri_b.astype(BF16))
    bc_f = _bmm(tf3, lf_parts[0]) + _bmm(tf3, lf_parts[1]) + _bmm(tf3, lf_parts[2])
    bc_b = _bmm(tb3, lf_parts[0]) + _bmm(tb3, lf_parts[1]) + _bmm(tb3, lf_parts[2])
    bc3 = jnp.where(fwd_lane, bc_f, bc_b)
    row3 = lax.broadcasted_iota(jnp.int32, (nc, CHUNK, LANES), 1)
    edge = jnp.where(row3 == jnp.where(fwd_lane, CHUNK - 1, 0), bc3, 0.0)
    bl3 = jnp.sum(edge, axis=1, keepdims=True)
    gg3 = bl3 - bc3 + li3
    bl2 = jnp.sum(edge, axis=1)
    mg2 = gg3.max(axis=1)

    m0 = m0_ref[0] if has_state else jnp.zeros((1, LANES), F32)
    mf = m0
    mb = m0
    mf_prev, mf_next, mb_prev, mb_next = {}, {}, {}, {}
    for i in range(nc):
        mf_prev[i] = mf
        mf = jnp.maximum(bl2[i:i + 1] + mf, mg2[i:i + 1])
        mf_next[i] = mf
        cb = nc - 1 - i
        mb_prev[cb] = mb
        mb = jnp.maximum(bl2[cb:cb + 1] + mb, mg2[cb:cb + 1])
        mb_next[cb] = mb
    m_prev = [jnp.where(fwd_lane, mf_prev[c], mb_prev[c]) for c in range(nc)]
    m_next = [jnp.where(fwd_lane, mf_next[c], mb_next[c]) for c in range(nc)]
    m_prev3 = jnp.stack(m_prev)
    m_next3 = jnp.stack(m_next)
    dec2d = jnp.exp(bl2 + jnp.concatenate(m_prev, axis=0) - jnp.concatenate(m_next, axis=0))
    if not has_state:
        mn_ref[0] = jnp.where(fwd_lane, mf, mb)

    u_parts = _split3(li3 - bc3)
    ws3 = jnp.exp(gg3 - m_next3)
    inter3 = bc3 + m_prev3

    q3 = c3(q_ref[...])
    k3 = c3(k_ref[...])
    v3 = c3(v_ref[...])
    q3f = q3.astype(F32)
    k3f = k3.astype(F32)
    lane3 = _lane((nc, CHUNK, LANES))
    lo3 = lane3 < HD
    rr = lax.broadcasted_iota(jnp.int32, (LANES, LANES), 0)
    cc = lax.broadcasted_iota(jnp.int32, (LANES, LANES), 1)
    blockdiag = (rr < HD) == (cc < HD)
    col = lambda x, k: x[:, :, k:k + 1]

    hsum = None
    for d in range(2):
        causal3 = bcast(tri_f if d == 0 else tri_b)
        a_c, ws_sum, mt_c, ws_c, intra = [], [], [], [], None
        for e in range(2):
            kk = 2 * d + e
            pick = jnp.broadcast_to(jnp.where(_lane((CHUNK, LANES)) == kk, 1.0, 0.0).astype(BF16)[None],
                                    (nc, CHUNK, LANES))
            ub = _bmm_nt(pick, u_parts[0]) + _bmm_nt(pick, u_parts[1]) + _bmm_nt(pick, u_parts[2])
            bc_col = col(bc3, kk)
            d_mat = jnp.where(causal3, bc_col + ub, NEG)
            inter = col(inter3, kk)
            mt = jnp.maximum(inter, d_mat.max(axis=-1, keepdims=True))
            a = jnp.exp(inter - mt)
            w = jnp.exp(d_mat - mt) * _bmm_nt(_half(q3, e == 1), k3)
            o = _bmm(w.astype(BF16), _half(v3, e == 1))
            intra = o if intra is None else intra + o
            a_c.append(a)
            mt_c.append(mt)
            ws_sum.append(w.sum(axis=-1, keepdims=True))
            ws_c.append(col(ws3, kk))
        kw3 = k3f * jnp.where(lo3, ws_c[0], ws_c[1])
        dec2 = jnp.where(_lane((nc, LANES)) < HD, dec2d[:, 2 * d:2 * d + 1], dec2d[:, 2 * d + 1:2 * d + 2])
        u_all = jnp.where(blockdiag, _bmm_tn(kw3.astype(BF16), v3), 0.0)
        kwsum = kw3.sum(axis=1)
        if has_state:
            cst = c0_ref[0, d]
            nst = n0_ref[0, d]
        else:
            cst = jnp.zeros((LANES, LANES), F32)
            nst = jnp.zeros((1, LANES), F32)
        cs, ns = [None] * nc, [None] * nc
        for c in (range(nc) if d == 0 else range(nc - 1, -1, -1)):
            cs[c] = cst
            ns[c] = nst
            cst = dec2[c:c + 1] * cst + u_all[c]
            nst = dec2[c:c + 1] * nst + kwsum[c:c + 1]
        if not has_state:
            cn_ref[0, d] = cst
            nn_ref[0, d] = nst
        qc = _bmm(q3, jnp.stack(cs).astype(BF16))
        qn_all = q3f * jnp.stack(ns)
        dn = []
        for e in range(2):
            qn = jnp.sum(jnp.where(lo3 != (e == 1), qn_all, 0.0), axis=-1, keepdims=True)
            den = a_c[e] * qn + ws_sum[e]
            dn.append(jnp.maximum(jnp.abs(den), jnp.exp(-mt_c[e])))
        h2 = (jnp.where(lo3, a_c[0], a_c[1]) * qc + intra) / jnp.where(lo3, dn[0], dn[1])
        hsum = h2 if hsum is None else hsum + h2

    hs = hsum.reshape(s_len, LANES)
    lo = _lane(hs.shape) < HD
    s = hs * hs
    s_lo = jnp.sum(jnp.where(lo, s, 0.0), axis=-1, keepdims=True)
    s_hi = jnp.sum(jnp.where(lo, 0.0, s), axis=-1, keepdims=True)
    r = lax.rsqrt(jnp.where(lo, s_lo, s_hi) * (1.0 / HD) + EPS)
    ob_ref[...] = (hs * r * hn_ref[...] * jax.nn.sigmoid(bo_ref[...])).astype(BF16)


def _mlstm_call(bq, bk, bv, bo, li, lf, hn2, state, l):
    has_state = state is not None
    if has_state:
        nb, s_len, base = NB_S, S_S, T_P // S_S
    else:
        nb, s_len, base = NB_P, S_P, 0
    nc = s_len // CHUNK
    tokp = pl.BlockSpec((s_len, LANES), lambda b, j: (base + b, j))
    in_specs = [tokp, tokp, tokp, tokp, tokp, tokp, pl.BlockSpec((1, LANES), lambda b, j: (0, 0))]
    args = [bq, bk, bv, bo, li, lf, hn2]
    ob_spec = pl.BlockSpec((s_len, LANES), lambda b, j: (b, j))
    ob_shape = jax.ShapeDtypeStruct((nb * s_len, 256), BF16)
    if has_state:
        c0, n0, m0 = state
        in_specs += [pl.BlockSpec((1, 2, LANES, LANES), lambda b, j: (b * 2 + j, 0, 0, 0)),
                     pl.BlockSpec((1, 2, 1, LANES), lambda b, j: (b * 2 + j, 0, 0, 0)),
                     pl.BlockSpec((1, 1, LANES), lambda b, j: (b * 2 + j, 0, 0))]
        args += [c0, n0, m0]
        out_specs = ob_spec
        out_shape = ob_shape
    else:
        out_specs = [ob_spec,
                     pl.BlockSpec((1, 2, LANES, LANES), lambda b, j: (b * 2 + j, 0, 0, 0)),
                     pl.BlockSpec((1, 2, 1, LANES), lambda b, j: (b * 2 + j, 0, 0, 0)),
                     pl.BlockSpec((1, 1, LANES), lambda b, j: (b * 2 + j, 0, 0))]
        out_shape = [ob_shape,
                     jax.ShapeDtypeStruct((nb * 2, 2, LANES, LANES), F32),
                     jax.ShapeDtypeStruct((nb * 2, 2, 1, LANES), F32),
                     jax.ShapeDtypeStruct((nb * 2, 1, LANES), F32)]
    return pl.pallas_call(
        functools.partial(_mlstm_kernel, nc, has_state),
        grid=(nb, 2),
        in_specs=in_specs,
        out_specs=out_specs,
        out_shape=out_shape,
        compiler_params=_cparams(("arbitrary", "arbitrary")),
        name=f"mlstm_{'sample' if has_state else 'prompt'}_l{l}",
    )(*args)


def _out_kernel(moe, xp_ref, xs_ref, oap_ref, oas_ref, obp_ref, obs_ref, ocp_ref, ocs_ref,
                w_ref, g1_ref, sh_ref, sc_ref, n2_ref, *rest):
    if moe:
        rh_ref, rl_ref, x1_o, xn_o, route_o, cnt_o, run_scr = rest
    else:
        x1_o, xn_o = rest
    i = pl.program_id(0)
    o = (_dot(_pick(i, oap_ref, oas_ref), w_ref[0, 0:384, :])
         + _dot(_pick(i, obp_ref, obs_ref), w_ref[0, 384:640, :])
         + _dot(_pick(i, ocp_ref, ocs_ref), w_ref[0, 640:1024, :]))
    x1 = _pick(i, xp_ref, xs_ref) + g1_ref[0] * o
    x1_o[...] = x1
    xn = x1 * lax.rsqrt(jnp.mean(x1 * x1, axis=-1, keepdims=True) + EPS) * n2_ref[...]
    xn = xn * (1.0 + sc_ref[0]) + sh_ref[0]
    xb = xn.astype(BF16)
    if not moe:
        xn_o[...] = xb
    else:
        xn_o[...] = xn
        xl = (xn - xb.astype(F32)).astype(BF16)
        logits = _dot(xb, rh_ref[0]) + (_dot(xl, rh_ref[0]) + _dot(xb, rl_ref[0]))
        lane = _lane(logits.shape)
        logits = jnp.where(lane < N_EXP, logits, -jnp.inf)
        m1 = logits.max(axis=-1, keepdims=True)
        i1 = jnp.min(jnp.where(logits == m1, lane, LANES), axis=-1, keepdims=True)
        rest_l = jnp.where(lane == i1, -jnp.inf, logits)
        m2 = rest_l.max(axis=-1, keepdims=True)
        i2 = jnp.min(jnp.where(rest_l == m2, lane, LANES), axis=-1, keepdims=True)
        e2 = jnp.exp(m2 - m1)
        den = 1.0 + e2
        w1 = 1.0 / den
        w2 = e2 / den

        @pl.when(pl.program_id(0) == 0)
        def _():
            run_scr[...] = jnp.zeros(run_scr.shape, F32)

        sel = jnp.where(lane == i1, 1.0, jnp.where((lane == i2) & (w2 > 0.0), 1.0, 0.0))
        before = (lax.broadcasted_iota(jnp.int32, (TM, TM), 1)
                  < lax.broadcasted_iota(jnp.int32, (TM, TM), 0)).astype(BF16)
        rank = run_scr[...] + _dot(before, sel.astype(BF16))
        r1 = jnp.sum(jnp.where(lane == i1, rank, 0.0), axis=-1, keepdims=True)
        r2 = jnp.sum(jnp.where(lane == i2, rank, 0.0), axis=-1, keepdims=True)
        run_scr[...] += jnp.sum(sel, axis=0, keepdims=True)
        cnt_o[...] = run_scr[...]
        fields = [i1.astype(F32), i2.astype(F32), r1, r2, w1, w2]
        info = jnp.zeros(lane.shape, F32)
        for k, v in enumerate(fields):
            info = jnp.where(lane == k, v, info)
        route_o[...] = info.T


def _out_call(x_p, x_s, sample_off, oa, ob, oc, w_out_b, mod3, l, n2g, router):
    moe = router is not None
    row = lambda k: pl.BlockSpec((1, 1, D), lambda i: ((l * 8 + _mod_row(i)) * 6 + k, 0, 0))
    tok = lambda n: pl.BlockSpec((TM, n), lambda i: (i, 0))
    in_specs = (_pair_specs(D, sample_off) + _pair_specs(384, 0) + _pair_specs(256, 0) + _pair_specs(384, 0)
                + [pl.BlockSpec((1, D, D), lambda i: (l, 0, 0)),
                   row(2), row(3), row(4), pl.BlockSpec((1, D), lambda i: (0, 0))])
    args = [x_p, x_s, *oa, *ob, *oc, w_out_b, mod3, mod3, mod3, n2g]
    out_specs = [tok(D), tok(D)]
    out_shape = [jax.ShapeDtypeStruct((T, D), F32), jax.ShapeDtypeStruct((T, D), BF16)]
    if moe:
        out_shape[1] = jax.ShapeDtypeStruct((T, D), F32)
        rh, rl = router
        in_specs += [pl.BlockSpec((1, D, LANES), lambda i: (0, 0, 0))] * 2
        args += [rh, rl]
        out_specs += [pl.BlockSpec((LANES, TM), lambda i: (0, i)), pl.BlockSpec((1, LANES), lambda i: (0, 0))]
        out_shape += [jax.ShapeDtypeStruct((LANES, T), F32), jax.ShapeDtypeStruct((1, LANES), F32)]
    return pl.pallas_call(
        functools.partial(_out_kernel, moe),
        grid=(NT,),
        in_specs=in_specs, out_specs=out_specs, out_shape=out_shape,
        scratch_shapes=[pltpu.VMEM((1, LANES), F32)] if moe else [],
        compiler_params=_cparams(("arbitrary",)),
        name=f"out_proj_l{l}",
    )(*args)


FFN_TM = 1024
FFN_TF = 256
FFN_TN = 256


def _hidden_tile(x, w1_ref, w3_ref):
    a = _dot(x, w1_ref[...].astype(BF16))
    b = _dot(x, w3_ref[...].astype(BF16))
    return (a * jax.nn.sigmoid(a) * b).astype(BF16)


def _down_tile(h_scr, w2n, m):
    out = None
    for f in range(h_scr.shape[0]):
        o = _dot(h_scr[f, 0:m, :], w2n[f])
        out = o if out is None else out + o
    return out


def _keep_weights(s, w1_ref, w3_ref, w2_ref, w1b, w3b, w2b, tn):
    w1b[s] = w1_ref[...].astype(BF16)
    w3b[s] = w3_ref[...].astype(BF16)
    w2 = w2_ref[...].astype(BF16)
    for n in range(w2b.shape[0]):
        w2b[n, s] = w2[:, n * tn:(n + 1) * tn]


def _ffn_kernel(nf, xn_ref, x1_ref, g2_ref, w1_ref, w3_ref, w2_ref, y_ref, h_scr, w1b, w3b, w2b):
    i = pl.program_id(0)
    s = pl.program_id(1)

    @pl.when((i == 0) & (s < nf))
    def _():
        _keep_weights(s, w1_ref.at[0], w3_ref.at[0], w2_ref.at[0], w1b, w3b, w2b, FFN_TN)

    @pl.when(s < nf)
    def _():
        h_scr[s] = _hidden_tile(xn_ref[...], w1b.at[s], w3b.at[s])

    @pl.when(s >= nf)
    def _():
        y_ref[...] = x1_ref[...] + g2_ref[0] * _down_tile(h_scr, w2b.at[s - nf], FFN_TM)


def _ffn_call(xn, x1, mod3, l, w1, w3, w2, i_layer):
    nf = D_FF // FFN_TF
    nn = D // FFN_TN
    tile = lambda i, s: jnp.where(i == 0, jnp.minimum(s, nf - 1), nf - 1)
    out = lambda s: jnp.maximum(s - nf, 0)

    def g2_idx(i, s):
        r = jnp.where(i < T_P // FFN_TM, 0, 1 + (i - T_P // FFN_TM) // (S_S // FFN_TM))
        return ((l * 8 + r) * 6 + 5, 0, out(s))

    return pl.pallas_call(
        functools.partial(_ffn_kernel, nf),
        grid=(T // FFN_TM, nf + nn),
        in_specs=[pl.BlockSpec((FFN_TM, D), lambda i, s: (i, 0)),
                  pl.BlockSpec((FFN_TM, FFN_TN), lambda i, s: (i, out(s))),
                  pl.BlockSpec((1, 1, FFN_TN), g2_idx),
                  pl.BlockSpec((1, D, FFN_TF), lambda i, s: (i_layer, 0, tile(i, s))),
                  pl.BlockSpec((1, D, FFN_TF), lambda i, s: (i_layer, 0, tile(i, s))),
                  pl.BlockSpec((1, FFN_TF, D), lambda i, s: (i_layer, tile(i, s), 0))],
        out_specs=pl.BlockSpec((FFN_TM, FFN_TN), lambda i, s: (i, out(s))),
        out_shape=jax.ShapeDtypeStruct((T, D), F32),
        scratch_shapes=[pltpu.VMEM((nf, FFN_TM, FFN_TF), BF16),
                        pltpu.VMEM((nf, D, FFN_TF), BF16), pltpu.VMEM((nf, D, FFN_TF), BF16),
                        pltpu.VMEM((nn, nf, FFN_TF, FFN_TN), BF16)],
        compiler_params=_cparams(("arbitrary", "arbitrary")),
        name="ffn_dense",
    )(xn, x1, mod3, w1, w3, w2)


MOE_BM = 1024
MOE_SUB = 256
MOE_NBLK = 2 * T // MOE_BM + N_EXP
MOE_NR = MOE_NBLK * MOE_BM
MOE_TF = 512
MOE_TN = 256
FIN_TM = 512


def _route_kernel(i1_ref, i2_ref, r1_ref, r2_ref, cnt_ref, w2_ref,
                  src_ref, eb_ref, rows_ref, nvb_ref, off_scr):
    def clear(r, carry):
        src_ref[r] = 0
        return carry

    lax.fori_loop(0, MOE_NR, clear, 0, unroll=8)

    def clear_blk(b, carry):
        eb_ref[b] = 0
        rows_ref[b] = 0
        return carry

    lax.fori_loop(0, MOE_NBLK, clear_blk, 0)

    nblk = jnp.int32(0)
    for e in range(N_EXP):
        c = cnt_ref[e]
        nbe = (c + (MOE_BM - 1)) // MOE_BM
        off_scr[e] = nblk * MOE_BM

        def fill(j, carry, e=e, c=c, nblk=nblk):
            eb_ref[nblk + j] = e
            rows_ref[nblk + j] = jnp.minimum(c - j * MOE_BM, MOE_BM)
            return carry

        lax.fori_loop(0, nbe, fill, 0)
        nblk = nblk + nbe
    nvb_ref[0] = nblk

    def place(t, carry):
        src_ref[off_scr[i1_ref[t]] + r1_ref[t]] = t

        @pl.when(w2_ref[t] > 0.0)
        def _():
            src_ref[off_scr[i2_ref[t]] + r2_ref[t]] = t

        return carry

    lax.fori_loop(0, T, place, 0, unroll=4)


def _route_call(route, cnt):
    ints = [route[k].astype(jnp.int32) for k in range(4)]
    cnt8 = cnt[0, :N_EXP].astype(jnp.int32)
    smem = pl.BlockSpec(memory_space=pltpu.SMEM)
    outs = pl.pallas_call(
        _route_kernel,
        grid_spec=pltpu.PrefetchScalarGridSpec(
            num_scalar_prefetch=5, grid=(1,),
            in_specs=[smem],
            out_specs=[smem, smem, smem, smem],
            scratch_shapes=[pltpu.SMEM((N_EXP,), jnp.int32)]),
        out_shape=[jax.ShapeDtypeStruct((MOE_NR,), jnp.int32),
                   jax.ShapeDtypeStruct((MOE_NBLK,), jnp.int32), jax.ShapeDtypeStruct((MOE_NBLK,), jnp.int32),
                   jax.ShapeDtypeStruct((1,), jnp.int32)],
        compiler_params=_cparams(("arbitrary",)),
        name="moe_route",
    )(*ints, cnt8, route[5])
    return tuple(outs) + (ints[0], route[4], route[5])


def _moe_gather_kernel(src_ref, rows_ref, x_ref, o_ref, rows_scr):
    k = pl.program_id(0)
    base = k * MOE_SUB
    used = rows_ref[k // (MOE_BM // MOE_SUB)] > (k % (MOE_BM // MOE_SUB)) * MOE_SUB

    @pl.when(used)
    def _():
        def body(i, carry):
            for j in range(8):
                r = i * 8 + j
                rows_scr[pl.ds(r, 1), :] = x_ref[pl.ds(src_ref[base + r], 1), :]
            return carry

        lax.fori_loop(0, MOE_SUB // 8, body, 0)
        o_ref[...] = rows_scr[...].astype(BF16)

    @pl.when(jnp.logical_not(used))
    def _():
        o_ref[...] = jnp.zeros(o_ref.shape, o_ref.dtype)


def _moe_gather_call(src, rows_b, xn):
    return pl.pallas_call(
        _moe_gather_kernel,
        grid_spec=pltpu.PrefetchScalarGridSpec(
            num_scalar_prefetch=2, grid=(MOE_NR // MOE_SUB,),
            in_specs=[pl.BlockSpec((T, D), lambda k, s, rw: (0, 0), pipeline_mode=pl.Buffered(1))],
            out_specs=pl.BlockSpec((MOE_SUB, D), lambda k, s, rw: (k, 0)),
            scratch_shapes=[pltpu.VMEM((MOE_SUB, D), F32)]),
        out_shape=jax.ShapeDtypeStruct((MOE_NR, D), BF16),
        compiler_params=_cparams(("arbitrary",)),
        name="moe_gather",
    )(src, rows_b, xn)


def _moe_first(b, eb_ref):
    return (b == 0) | (eb_ref[b] != eb_ref[jnp.maximum(b - 1, 0)])


def _moe_kernel(nf, eb_ref, rows_ref, nvb_ref, xs_ref, w1_ref, w3_ref, w2_ref, y_ref, h_scr, w1b, w3b, w2b):
    b = pl.program_id(0)
    s = pl.program_id(1)
    nsub = jnp.where(b < nvb_ref[0], (rows_ref[b] + MOE_SUB - 1) // MOE_SUB, 0)
    sizes = [k * MOE_SUB for k in range(1, MOE_BM // MOE_SUB + 1)]

    @pl.when((s < nf) & (nsub > 0) & _moe_first(b, eb_ref))
    def _():
        _keep_weights(s, w1_ref.at[0, 0], w3_ref.at[0, 0], w2_ref.at[0, 0], w1b, w3b, w2b, MOE_TN)

    @pl.when((s < nf) & (nsub > 0))
    def _():
        for k, m in enumerate(sizes, start=1):
            @pl.when(nsub == k)
            def _(m=m):
                h_scr[s, 0:m, :] = _hidden_tile(xs_ref[0:m, :], w1b.at[s], w3b.at[s])

    @pl.when((s >= nf) & (nsub > 0))
    def _():
        for k, m in enumerate(sizes, start=1):
            @pl.when(nsub == k)
            def _(m=m):
                y_ref[0:m, :] = _down_tile(h_scr, w2b.at[s - nf], m)
                if m < MOE_BM:
                    y_ref[m:MOE_BM, :] = jnp.zeros((MOE_BM - m, MOE_TN), F32)

    @pl.when((s >= nf) & (nsub == 0))
    def _():
        y_ref[...] = jnp.zeros(y_ref.shape, F32)


def _moe_call(xs, e_b, rows_b, nvb, w1, w3, w2, i_layer):
    nf = D_FFE // MOE_TF
    nn = D // MOE_TN

    def blk(b, nv):
        return jnp.minimum(b, nv[0] - 1)

    def tile(b, s, eb, nv):
        stream = (b < nv[0]) & _moe_first(b, eb)
        return jnp.where(stream, jnp.minimum(s, nf - 1), nf - 1)

    return pl.pallas_call(
        functools.partial(_moe_kernel, nf),
        grid_spec=pltpu.PrefetchScalarGridSpec(
            num_scalar_prefetch=3, grid=(MOE_NBLK, nf + nn),
            in_specs=[pl.BlockSpec((MOE_BM, D), lambda b, s, eb, rw, nv: (blk(b, nv), 0)),
                      pl.BlockSpec((1, 1, D, MOE_TF),
                                   lambda b, s, eb, rw, nv: (i_layer, eb[blk(b, nv)], 0, tile(b, s, eb, nv))),
                      pl.BlockSpec((1, 1, D, MOE_TF),
                                   lambda b, s, eb, rw, nv: (i_layer, eb[blk(b, nv)], 0, tile(b, s, eb, nv))),
                      pl.BlockSpec((1, 1, MOE_TF, D),
                                   lambda b, s, eb, rw, nv: (i_layer, eb[blk(b, nv)], tile(b, s, eb, nv), 0))],
            out_specs=pl.BlockSpec((MOE_BM, MOE_TN), lambda b, s, eb, rw, nv: (b, jnp.maximum(s - nf, 0))),
            scratch_shapes=[pltpu.VMEM((nf, MOE_BM, MOE_TF), BF16),
                            pltpu.VMEM((nf, D, MOE_TF), BF16), pltpu.VMEM((nf, D, MOE_TF), BF16),
                            pltpu.VMEM((nn, nf, MOE_TF, MOE_TN), BF16)]),
        out_shape=jax.ShapeDtypeStruct((MOE_NR, D), F32),
        compiler_params=_cparams(("arbitrary", "arbitrary"), vmem=56 * 1024 * 1024),
        name="moe_experts",
    )(e_b, rows_b, nvb, xs, w1, w3, w2)


def _moe_combine_kernel(src_ref, eb_ref, rows_ref, nvb_ref, i1_ref, w1_ref, w2_ref,
                        ys_ref, x1_ref, g2_ref, yp_ref, ys_out_ref, acc):
    i = pl.program_id(0)

    @pl.when(i == 0)
    def _():
        acc[...] = jnp.zeros(acc.shape, F32)

    @pl.when(i < nvb_ref[0])
    def _():
        base = i * MOE_BM
        n = rows_ref[i]
        e = eb_ref[i]

        def gate(t):
            return jnp.where(i1_ref[t] == e, w1_ref[t], w2_ref[t])

        def add_rows(r0, cnt):
            toks = [src_ref[base + r0 + j] for j in range(cnt)]
            vals = [acc[pl.ds(toks[j], 1), :] + gate(toks[j]) * ys_ref[pl.ds(r0 + j, 1), :]
                    for j in range(cnt)]
            for j in range(cnt):
                acc[pl.ds(toks[j], 1), :] = vals[j]

        def body4(q, carry):
            add_rows(q * 4, 4)
            return carry

        lax.fori_loop(0, n // 4, body4, 0)

        def body1(r, carry):
            add_rows(r, 1)
            return carry

        lax.fori_loop((n // 4) * 4, n, body1, 0)

    @pl.when(i >= MOE_NBLK)
    def _():
        t0 = pl.multiple_of((i - MOE_NBLK) * FIN_TM, FIN_TM)
        y = x1_ref[...] + g2_ref[0] * acc[pl.ds(t0, FIN_TM), :]

        @pl.when(i - MOE_NBLK < T_P // FIN_TM)
        def _():
            yp_ref[...] = y

        @pl.when(i - MOE_NBLK >= T_P // FIN_TM)
        def _():
            ys_out_ref[...] = y


def _moe_combine_call(src, e_b, rows_b, nvb, i1, w1, w2, ys, x1, mod3, l):
    nfin = T // FIN_TM
    smem = pl.BlockSpec(memory_space=pltpu.SMEM)

    def g2_idx(i, *_):
        j = jnp.maximum(i - MOE_NBLK, 0)
        r = jnp.where(j < T_P // FIN_TM, 0, 1 + (j - T_P // FIN_TM) // (S_S // FIN_TM))
        return ((l * 8 + r) * 6 + 5, 0, 0)

    return pl.pallas_call(
        _moe_combine_kernel,
        grid_spec=pltpu.PrefetchScalarGridSpec(
            num_scalar_prefetch=5, grid=(MOE_NBLK + nfin,),
            in_specs=[smem, smem,
                      pl.BlockSpec((MOE_BM, D),
                                   lambda i, s, eb, rw, nv, t1: (jnp.minimum(jnp.minimum(i, MOE_NBLK - 1), nv[0] - 1), 0)),
                      pl.BlockSpec((FIN_TM, D), lambda i, *_: (jnp.maximum(i - MOE_NBLK, 0), 0)),
                      pl.BlockSpec((1, 1, D), g2_idx)],
            out_specs=[pl.BlockSpec((FIN_TM, D),
                                    lambda i, *_: (jnp.clip(i - MOE_NBLK, 0, T_P // FIN_TM - 1), 0)),
                       pl.BlockSpec((FIN_TM, D),
                                    lambda i, *_: (jnp.maximum(i - MOE_NBLK - T_P // FIN_TM, 0), 0))],
            scratch_shapes=[pltpu.VMEM((T, D), F32)]),
        out_shape=[jax.ShapeDtypeStruct((T_P, D), F32), jax.ShapeDtypeStruct((T_S, D), F32)],
        compiler_params=_cparams(("arbitrary",), vmem=52 * 1024 * 1024),
        name="moe_combine",
    )(src, e_b, rows_b, nvb, i1, w1, w2, ys, x1, mod3)


def _moe_layer(xp, x1, route, cnt, mod3, l, w1, w3, w2, i_layer):
    src, e_b, rows_b, nvb, i1, g1, g2 = _route_call(route, cnt)
    xs = _moe_gather_call(src, rows_b, xp)
    ys = _moe_call(xs, e_b, rows_b, nvb, w1, w3, w2, i_layer)
    return _moe_combine_call(src, e_b, rows_b, nvb, i1, g1, g2, ys, x1, mod3, l)


def _pad_cols(w, n):
    return jnp.pad(w, ((0, 0), (0, n - w.shape[1])))


def _relayout_w_in(w):
    aq, ak, av, bq, bk, bv, bo, bg, cq, ckv, ckr = jnp.split(
        w, (384, 512, 640, 896, 1152, 1408, 1664, 1680, 1936, 2064), axis=1)
    dup = lambda m: jnp.concatenate([m[:, 0:HD], m[:, 0:HD], m[:, HD:2 * HD], m[:, HD:2 * HD]], axis=1)
    g = bg.reshape(D, 2, 2, 2, 2)

    def gate_cols(gi):
        cols = []
        for j in range(2):
            cols.append(_pad_cols(g[:, :, gi, j, :].reshape(D, 4), LANES))
        return jnp.concatenate(cols, axis=1)

    ckr_p = jnp.pad(ckr, ((0, 0), (C_NOPE, LANES - C_NOPE - C_ROPE)))
    out = jnp.concatenate([aq, dup(ak), dup(av), bq, bk, bv, bo, gate_cols(0), gate_cols(1),
                           cq, ckv, ckr_p], axis=1)
    return out.astype(BF16)


def _relayout_gate_b(b):
    g = b.reshape(2, 2, 2, 2)

    def cols(gi):
        return jnp.concatenate([jnp.pad(g[:, gi, j, :].reshape(4), (0, LANES - 4)) for j in range(2)])

    return jnp.concatenate([cols(0), cols(1)])[None, :]


def _pad_heads(w, width):
    r = w.shape[0]
    h = w.shape[1] // width
    return jnp.pad(w.reshape(r, h, width), ((0, 0), (0, 0), (0, LANES - width))).reshape(r, h * LANES)


def _rope_tables(half, span_start, period):
    rows = S_S // 64
    r = jnp.repeat(jnp.arange(rows), 64).astype(F32)
    c = jnp.tile(jnp.arange(64), rows).astype(F32)
    n_freq = half // 2
    freq = 10000.0 ** (-jnp.arange(n_freq, dtype=F32) / n_freq)
    ang = jnp.concatenate([r[:, None] * freq, c[:, None] * freq], axis=-1)
    cos, sin = jnp.cos(ang), jnp.sin(ang)
    d = (jnp.arange(LANES) - span_start) % period
    inside = d < 2 * half
    p = jnp.where(inside, d % half, 0)
    first = inside & (d < half)
    second = inside & (d >= half)
    ct = jnp.where(inside[None, :], cos[:, p], 1.0)
    sa = jnp.where(first[None, :], -sin[:, p], 0.0)
    sb = jnp.where(second[None, :], sin[:, p], 0.0)
    ident = (jnp.ones((TM, LANES), F32), jnp.zeros((TM, LANES), F32), jnp.zeros((TM, LANES), F32))
    return tuple(jnp.concatenate([i0, t], axis=0) for i0, t in zip(ident, (ct, sa, sb)))


def kernel(x_prompt, x_sample, c, cache_swa_k, cache_swa_v, cache_mla_ckv, cache_mla_krope, state_mlstm_C, state_mlstm_n, state_mlstm_m, c_ctx, ada_w, ada_b, norm1_g, norm2_g, w_in, a_qn_g, a_kn_g, a_sink, b_gate_b, b_hn_g, c_qa_g, c_kva_g, c_wuq, c_wukv, c_qn_g, c_kn_g, w_out, ffn_w1, ffn_w3, ffn_w2, moe_router, moe_w1, moe_w3, moe_w2):
    x_p, x_s, sample_off = x_prompt.reshape(T_P, D), x_sample.reshape(T_S, D), 0
    cv = jnp.concatenate([c_ctx[None, :], c, jnp.zeros((5, D), F32)], axis=0)
    mod = _ada_call(cv, ada_w, ada_b)
    mod3 = mod.reshape(DEPTH * 8 * 6, 1, D)

    rope_a = _rope_tables(32, 0, HD)
    rope_c = _rope_tables(16, C_NOPE, LANES)

    w_in_p = jnp.stack([_relayout_w_in(w_in[l]) for l in range(DEPTH)])
    wuq_p = jnp.stack([_pad_heads(c_wuq[l], C_QK) for l in range(DEPTH)]).astype(BF16)
    wukv = c_wukv.reshape(DEPTH, C_KVRANK, C_HEADS, C_NOPE + C_V)
    wk_p = jnp.pad(wukv[..., :C_NOPE], ((0, 0), (0, 0), (0, 0), (0, LANES - C_NOPE)))
    wk_p = wk_p.reshape(DEPTH, C_KVRANK, C_HEADS * LANES).astype(BF16)
    wv_p = wukv[..., C_NOPE:].reshape(DEPTH, C_KVRANK, C_HEADS * C_V).astype(BF16)
    w_out_b = w_out.astype(BF16)
    cache_k4 = cache_swa_k.reshape(NB_S, DEPTH, PAST, A_KV * HD)
    cache_v4 = cache_swa_v.reshape(NB_S, DEPTH, PAST, A_KV * HD)

    news = []
    for l in range(DEPTH):
        gqa = jnp.tile(a_qn_g[l], A_HEADS)[None, :]
        gka = jnp.tile(a_kn_g[l], 2 * A_KV)[None, :]
        gb = _relayout_gate_b(b_gate_b[l])
        gqn = jnp.tile(jnp.pad(c_qn_g[l], (0, LANES - C_QK)), C_HEADS)[None, :]
        gkn = jnp.tile(jnp.pad(c_kn_g[l], (0, LANES - C_QK)), C_HEADS)[None, :]
        (qa, ka, va, kaf, vaf, bq, bk, bv, bo, li, lf, qc, ckv_n, ckr) = _in_call(
            x_p, x_s, sample_off, mod3, l, norm1_g[l][None, :], w_in_p, gqa, gka, gb, c_qa_g[l][None, :],
            c_kva_g[l][None, :], wuq_p, gqn, rope_a, rope_c)

        ckr_cache = jnp.pad(cache_mla_krope[:, l].reshape(NB_S * PAST, C_ROPE),
                            ((0, 0), (C_NOPE, LANES - C_NOPE - C_ROPE)))
        kc, vc = _mlakv_call(ckv_n, cache_mla_ckv[:, l].reshape(NB_S * PAST, C_KVRANK), ckr, ckr_cache,
                             l, wk_p, wv_p, gkn, rope_c)

        sink = a_sink[l]
        oa_p = _attn_a_prompt_call(sink, qa, ka, va)
        oa_s = _attn_a_sample_call(sink, qa, ka, va, cache_k4, cache_v4, l)
        oc_p = _mla_prompt_call(qc, kc, vc)
        oc_s = _mla_sample_call(qc, kc, vc)

        hn2 = jnp.tile(b_hn_g[l], 2)[None, :]
        ob_p, cn, nn, mn = _mlstm_call(bq, bk, bv, bo, li, lf, hn2, None, l)
        c_st = state_mlstm_C[:, l].reshape(NB_S, 2, 2, 2, HD, HD)
        c_st = jnp.transpose(c_st, (0, 2, 1, 3, 4, 5))
        c0 = jnp.zeros((NB_S, 2, 2, LANES, LANES), F32)
        c0 = c0.at[..., :HD, :HD].set(c_st[:, :, :, 0]).at[..., HD:, HD:].set(c_st[:, :, :, 1])
        c0 = c0.reshape(NB_S * 2, 2, LANES, LANES)
        n_st = state_mlstm_n[:, l].reshape(NB_S, 2, 2, 1, LANES)
        n0 = jnp.transpose(n_st, (0, 2, 1, 3, 4)).reshape(NB_S * 2, 2, 1, LANES)
        m_st = state_mlstm_m[:, l].reshape(NB_S, 2, 2, 2)
        m0 = jnp.transpose(m_st, (0, 2, 1, 3)).reshape(NB_S * 2, 1, 4)
        m0 = jnp.pad(m0, ((0, 0), (0, 0), (0, LANES - 4)))
        ob_s = _mlstm_call(bq, bk, bv, bo, li, lf, hn2, (c0, n0, m0), l)

        oa, ob, oc = (oa_p, oa_s), (ob_p, ob_s), (oc_p, oc_s)

        if l % 2 == 0:
            x1, xn = _out_call(x_p, x_s, sample_off, oa, ob, oc, w_out_b, mod3, l, norm2_g[l][None, :], None)
            x = _ffn_call(xn, x1, mod3, l, ffn_w1, ffn_w3, ffn_w2, l // 2)
            x_p, x_s, sample_off = x, x, NT_P
        else:
            r = _pad_cols(moe_router[l // 2], LANES)
            rh = r.astype(BF16)
            rl = (r - rh.astype(F32)).astype(BF16)
            x1, xn, route, cnt = _out_call(x_p, x_s, sample_off, oa, ob, oc, w_out_b, mod3, l,
                                           norm2_g[l][None, :], (rh[None], rl[None]))
            x_p, x_s = _moe_layer(xn, x1, route, cnt, mod3, l, moe_w1, moe_w3, moe_w2, l // 2)
            sample_off = 0

        new_k = kaf.reshape(NB_P, S_P, A_KV, HD)
        new_v = vaf.reshape(NB_P, S_P, A_KV, HD)
        new_ckv = ckv_n[:T_P].reshape(NB_P, S_P, C_KVRANK)
        new_kr = ckr[:T_P, C_NOPE:C_NOPE + C_ROPE].reshape(NB_P, S_P, C_ROPE)
        cn6 = cn.reshape(NB_P, 2, 2, LANES, LANES)
        c_e = jnp.stack([cn6[..., :HD, :HD], cn6[..., HD:, HD:]], axis=3)
        new_c = jnp.transpose(c_e, (0, 2, 1, 3, 4, 5)).reshape(NB_P, 2, B_HEADS, HD, HD)
        nn5 = nn.reshape(NB_P, 2, 2, 2, HD)
        new_n = jnp.transpose(nn5, (0, 2, 1, 3, 4)).reshape(NB_P, 2, B_HEADS, HD)
        mn4 = mn.reshape(NB_P, 2, LANES)[:, :, :4].reshape(NB_P, 2, 2, 2)
        new_m = jnp.transpose(mn4, (0, 2, 1, 3)).reshape(NB_P, 2, B_HEADS)
        news.append((new_k, new_v, new_ckv, new_kr, new_c, new_n, new_m))

    y_prompt = x_p[:T_P].reshape(NB_P, S_P, D)
    y_sample = x_s[sample_off * TM:sample_off * TM + T_S].reshape(NB_S, S_S, D)
    stacked = tuple(jnp.stack([nw[j] for nw in news], axis=1) for j in range(7))
    return (y_prompt, y_sample) + stacked
```

```python
import functools

import jax
import jax.numpy as jnp
from jax import lax
from jax.experimental import pallas as pl
from jax.experimental.pallas import tpu as pltpu

F32 = jnp.float32
BF16 = jnp.bfloat16

D = 1024
NB_P, S_P = 16, 256
NB_S, S_S = 2, 1024
PAST = 512
DEPTH = 2
T_P = NB_P * S_P
T_S = NB_S * S_S
T = T_P + T_S
TM = 512
NT = T // TM
NT_P = T_P // TM
HD = 64
A_HEADS, A_KV = 6, 2
B_HEADS = 4
CHUNK = 64
C_HEADS = 6
C_QRANK, C_KVRANK, C_NOPE, C_ROPE, C_V = 256, 128, 64, 32, 64
C_QK = C_NOPE + C_ROPE
D_FF = 2816
N_EXP = 8
D_FFE = 3584
EPS = 1e-6
NEG = -1e30
LANES = 128
VMEM_LIMIT = 48 * 1024 * 1024

SEG = dict(QA=(0, 384), KA=(384, 256), VA=(640, 256), BQ=(896, 256), BK=(1152, 256),
           BV=(1408, 256), BO=(1664, 256), GI=(1920, 256), GF=(2176, 256), CQ=(2432, 256),
           CKV=(2688, 128), CKR=(2816, 128))
NP_IN = 2944


def _cparams(sem, vmem=VMEM_LIMIT):
    return pltpu.CompilerParams(dimension_semantics=sem, vmem_limit_bytes=vmem)


def _dot(a, b):
    return jnp.dot(a, b, preferred_element_type=F32)


def _dot_nt(a, b):
    return lax.dot_general(a, b, (((1,), (1,)), ((), ())), preferred_element_type=F32)


def _lane(shape):
    return lax.broadcasted_iota(jnp.int32, shape, len(shape) - 1)


def _mod_row(i):
    return jnp.where(i < NT_P, 0, 1 + (i - NT_P) // (S_S // TM))


def _pair_specs(n, sample_off):
    return [pl.BlockSpec((TM, n), lambda i: (jnp.minimum(i, NT_P - 1), 0)),
            pl.BlockSpec((TM, n), lambda i: (sample_off + jnp.maximum(i - NT_P, 0), 0))]


def _pick(i, p_ref, s_ref):
    return jnp.where(i < NT_P, p_ref[...], s_ref[...])


def _rope_blk(i):
    return jnp.where((i >= NT_P) & (i < NT), 1 + (i - NT_P) % (S_S // TM), 0)


def _ada_kernel(cv_ref, w_ref, b_ref, o_ref):
    s = cv_ref[...]
    s = s * jax.nn.sigmoid(s)
    o_ref[0] = _dot(s.astype(BF16), w_ref[0].astype(BF16)) + b_ref[0]


def _ada_call(cv, ada_w, ada_b):
    tn = 1536
    return pl.pallas_call(
        _ada_kernel,
        grid=(DEPTH, 6 * D // tn),
        in_specs=[pl.BlockSpec((8, D), lambda l, j: (0, 0)),
                  pl.BlockSpec((1, D, tn), lambda l, j: (l, 0, j)),
                  pl.BlockSpec((1, 1, tn), lambda l, j: (l, 0, j))],
        out_specs=pl.BlockSpec((1, 8, tn), lambda l, j: (l, 0, j)),
        out_shape=jax.ShapeDtypeStruct((DEPTH, 8, 6 * D), F32),
        compiler_params=_cparams(("arbitrary", "arbitrary")),
        name="ada_mod",
    )(cv, ada_w, ada_b.reshape(DEPTH, 1, 6 * D))


def _half_norm(x, g, n):
    outs = []
    for j in range(x.shape[1] // LANES):
        xj = x[:, j * LANES:(j + 1) * LANES]
        lo = _lane(xj.shape) < HD
        s = xj * xj
        s_lo = jnp.sum(jnp.where(lo, s, 0.0), axis=-1, keepdims=True)
        s_hi = jnp.sum(jnp.where(lo, 0.0, s), axis=-1, keepdims=True)
        r = lax.rsqrt(jnp.where(lo, s_lo, s_hi) * (1.0 / n) + EPS)
        outs.append(xj * r)
    return jnp.concatenate(outs, axis=-1) * g


def _group_norm(x, g, n):
    outs = []
    for j in range(x.shape[1] // LANES):
        xj = x[:, j * LANES:(j + 1) * LANES]
        r = lax.rsqrt(jnp.sum(xj * xj, axis=-1, keepdims=True) * (1.0 / n) + EPS)
        outs.append(xj * r)
    return jnp.concatenate(outs, axis=-1) * g


def _rope(x, c, sa, sb, shift):
    outs = []
    for j in range(x.shape[1] // LANES):
        xj = x[:, j * LANES:(j + 1) * LANES]
        outs.append(xj * c + pltpu.roll(xj, LANES - shift, 1) * sa + pltpu.roll(xj, shift, 1) * sb)
    return jnp.concatenate(outs, axis=-1)


def _in_kernel(xp_ref, xs_ref, sh_ref, sc_ref, n1_ref, w_ref, gqa_ref, gka_ref, gb_ref, gcq_ref, gckv_ref,
               wuq_ref, gqn_ref, ra_c, ra_a, ra_b, rc_c, rc_a, rc_b,
               qa_o, ka_o, va_o, kaf_o, vaf_o, bq_o, bk_o, bv_o, bo_o, li_o, lf_o, qc_o, ckv_o, ckr_o):
    i = pl.program_id(0)
    x = _pick(i, xp_ref, xs_ref)
    xn = x * lax.rsqrt(jnp.mean(x * x, axis=-1, keepdims=True) + EPS) * n1_ref[...]
    xn = xn * (1.0 + sc_ref[0]) + sh_ref[0]
    p = _dot(xn.astype(BF16), w_ref[0])

    def seg(name):
        o, n = SEG[name]
        return p[:, o:o + n]

    qa = _half_norm(seg("QA"), gqa_ref[...], HD)
    ka = _half_norm(seg("KA"), gka_ref[...], HD)
    va = seg("VA")

    @pl.when(i < NT_P)
    def _():
        kaf_o[...] = jnp.concatenate([ka[:, 0:HD], ka[:, LANES:LANES + HD]], axis=-1)
        vaf_o[...] = jnp.concatenate([va[:, 0:HD], va[:, LANES:LANES + HD]], axis=-1)

    va_o[...] = va.astype(BF16)

    bq_o[...] = seg("BQ").astype(BF16)
    bk_o[...] = (seg("BK") * (HD ** -0.5)).astype(BF16)
    bv_o[...] = seg("BV").astype(BF16)
    bo_o[...] = seg("BO")
    gb = gb_ref[...]
    li_o[...] = seg("GI") + gb[:, 0:256]
    lf_o[...] = jax.nn.log_sigmoid(seg("GF") + gb[:, 256:512])

    cq = seg("CQ")
    cqn = cq * lax.rsqrt(jnp.mean(cq * cq, axis=-1, keepdims=True) + EPS) * gcq_ref[...]
    qc = _group_norm(_dot(cqn.astype(BF16), wuq_ref[0]), gqn_ref[...], C_QK)

    @pl.when(i < NT_P)
    def _():
        qa_o[...] = qa.astype(BF16)
        ka_o[...] = ka.astype(BF16)
        qc_o[...] = qc.astype(BF16)

    @pl.when(i >= NT_P)
    def _():
        qa_o[...] = _rope(qa, ra_c[...], ra_a[...], ra_b[...], 32).astype(BF16)
        ka_o[...] = _rope(ka, ra_c[...], ra_a[...], ra_b[...], 32).astype(BF16)
        qc_o[...] = _rope(qc, rc_c[...], rc_a[...], rc_b[...], 16).astype(BF16)

    ckv = seg("CKV")
    ckv_o[...] = ckv * lax.rsqrt(jnp.mean(ckv * ckv, axis=-1, keepdims=True) + EPS) * gckv_ref[...]
    ckr_o[...] = seg("CKR")


def _in_call(x_p, x_s, sample_off, mod3, l, n1g, w_in_p, gqa, gka, gb, gcq, gckv, wuq_p, gqn, rope_a, rope_c):
    row = lambda k: pl.BlockSpec((1, 1, D), lambda i: ((l * 8 + _mod_row(i)) * 6 + k, 0, 0))
    vec = lambda n: pl.BlockSpec((1, n), lambda i: (0, 0))
    tab = pl.BlockSpec((TM, LANES), lambda i: (_rope_blk(i), 0))
    tok = lambda n: pl.BlockSpec((TM, n), lambda i: (i, 0))
    tokp = pl.BlockSpec((TM, LANES), lambda i: (jnp.minimum(i, NT_P - 1), 0))
    o = lambda n, dt: jax.ShapeDtypeStruct((T, n), dt)
    return pl.pallas_call(
        _in_kernel,
        grid=(NT,),
        in_specs=_pair_specs(D, sample_off) + [row(0), row(1), vec(D),
                  pl.BlockSpec((1, D, NP_IN), lambda i: (l, 0, 0)),
                  vec(384), vec(256), vec(512), vec(256), vec(128),
                  pl.BlockSpec((1, C_QRANK, 768), lambda i: (l, 0, 0)), vec(768),
                  tab, tab, tab, tab, tab, tab],
        out_specs=[tok(384), tok(256), tok(256), tokp, tokp, tok(256), tok(256), tok(256), tok(256),
                   tok(256), tok(256), tok(768), tok(128), tok(128)],
        out_shape=[o(384, BF16), o(256, BF16), o(256, BF16),
                   jax.ShapeDtypeStruct((T_P, LANES), F32), jax.ShapeDtypeStruct((T_P, LANES), F32),
                   o(256, BF16), o(256, BF16), o(256, BF16), o(256, F32),
                   o(256, F32), o(256, F32), o(768, BF16), o(128, F32), o(128, F32)],
        compiler_params=_cparams(("arbitrary",)),
        name=f"in_proj_l{l}",
    )(x_p, x_s, mod3, mod3, n1g, w_in_p, gqa, gka, gb, gcq, gckv, wuq_p, gqn, *rope_a, *rope_c)


def _mlakv_kernel(ckv_ref, ckvc_ref, ckr_ref, ckrc_ref, wk_ref, wv_ref, gkn_ref, rc_c, rc_a, rc_b, k_o, v_o):
    i = pl.program_id(0)
    tok = i < NT
    c = jnp.where(tok, ckv_ref[...], ckvc_ref[...]).astype(BF16)
    k = _dot(c, wk_ref[0])
    kr = jnp.where(tok, ckr_ref[...], ckrc_ref[...])
    k = k + jnp.concatenate([kr] * C_HEADS, axis=-1)
    k = _group_norm(k, gkn_ref[...], C_QK)
    sample = (i >= NT_P) & tok

    @pl.when(sample)
    def _():
        k_o[...] = _rope(k, rc_c[...], rc_a[...], rc_b[...], 16).astype(BF16)

    @pl.when(jnp.logical_not(sample))
    def _():
        k_o[...] = k.astype(BF16)

    v_o[...] = _dot(c, wv_ref[0]).astype(BF16)


def _mlakv_call(ckv_n, ckv_cache, ckr, ckr_cache, l, wk_p, wv_p, gkn, rope_c):
    r = T + NB_S * PAST
    tab = pl.BlockSpec((TM, LANES), lambda i: (_rope_blk(i), 0))
    tok = lambda n: pl.BlockSpec((TM, n), lambda i: (i, 0))
    tokens = pl.BlockSpec((TM, LANES), lambda i: (jnp.minimum(i, NT - 1), 0))
    cached = pl.BlockSpec((TM, LANES), lambda i: (jnp.maximum(i - NT, 0), 0))
    return pl.pallas_call(
        _mlakv_kernel,
        grid=(r // TM,),
        in_specs=[tokens, cached, tokens, cached,
                  pl.BlockSpec((1, C_KVRANK, 768), lambda i: (l, 0, 0)),
                  pl.BlockSpec((1, C_KVRANK, 384), lambda i: (l, 0, 0)),
                  pl.BlockSpec((1, 768), lambda i: (0, 0)), tab, tab, tab],
        out_specs=[tok(768), tok(384)],
        out_shape=[jax.ShapeDtypeStruct((r, 768), BF16), jax.ShapeDtypeStruct((r, 384), BF16)],
        compiler_params=_cparams(("arbitrary",)),
        name=f"mla_kv_l{l}",
    )(ckv_n, ckv_cache, ckr, ckr_cache, wk_p, wv_p, gkn, *rope_c)


def _softmax_pv(scores, values, sink):
    m = scores[0].max(axis=-1, keepdims=True)
    for s in scores[1:]:
        m = jnp.maximum(m, s.max(axis=-1, keepdims=True))
    if sink is not None:
        m = jnp.maximum(m, sink)
    es = [jnp.exp(s - m) for s in scores]
    den = es[0].sum(axis=-1, keepdims=True)
    for e in es[1:]:
        den = den + e.sum(axis=-1, keepdims=True)
    if sink is not None:
        den = den + jnp.exp(sink - m)
    inv = 1.0 / den
    out = None
    for e, v in zip(es, values):
        o = _dot((e * inv).astype(BF16), v)
        out = o if out is None else out + o
    return out


def _half(x, hi):
    lo = _lane(x.shape) < HD
    return jnp.where(lo != hi, x, jnp.zeros_like(x))


def _attn_a_prompt_kernel(sink_ref, q_ref, k_ref, v_ref, o_ref):
    q = q_ref[...]
    k = k_ref[...]
    v = v_ref[...]
    for j in range(A_HEADS // 2):
        acc = None
        for c in range(2):
            h = 2 * j + c
            g = h // (A_HEADS // A_KV)
            qh = _half(q[:, j * LANES:(j + 1) * LANES], c == 1)
            s = _dot_nt(qh, k[:, g * LANES:(g + 1) * LANES]) * (HD ** -0.5)
            vh = _half(v[:, g * LANES:(g + 1) * LANES], c == 1)
            o = _softmax_pv([s], [vh], sink_ref[h])
            acc = o if acc is None else acc + o
        o_ref[:, j * LANES:(j + 1) * LANES] = acc.astype(BF16)


def _attn_a_prompt_call(sink, qa, ka, va):
    return pl.pallas_call(
        _attn_a_prompt_kernel,
        grid_spec=pltpu.PrefetchScalarGridSpec(
            num_scalar_prefetch=1, grid=(NB_P,),
            in_specs=[pl.BlockSpec((S_P, 384), lambda b, s: (b, 0)),
                      pl.BlockSpec((S_P, 256), lambda b, s: (b, 0)),
                      pl.BlockSpec((S_P, 256), lambda b, s: (b, 0))],
            out_specs=pl.BlockSpec((S_P, 384), lambda b, s: (b, 0))),
        out_shape=jax.ShapeDtypeStruct((T_P, 384), BF16),
        compiler_params=_cparams(("arbitrary",)),
        name="attn_a_prompt",
    )(sink, qa, ka, va)


QB = 128
WIN = 128
BAND = QB + 2 * WIN


def _attn_a_sample_kernel(sink_ref, q_ref, k_ref, v_ref, kc_ref, vc_ref, o_ref):
    n = pl.program_id(1)
    ws = pl.multiple_of(jnp.clip(n * QB - WIN, 0, S_S - BAND), QB)
    q = q_ref[...]
    kb = k_ref[pl.ds(ws, BAND), :]
    vb = v_ref[pl.ds(ws, BAND), :]
    kc = kc_ref[0, 0].astype(BF16)
    vc = vc_ref[0, 0].astype(BF16)
    qpos = n * QB + lax.broadcasted_iota(jnp.int32, (QB, BAND), 0)
    kpos = ws + lax.broadcasted_iota(jnp.int32, (QB, BAND), 1)
    ok = jnp.abs(qpos - kpos) <= WIN
    lane = _lane((PAST, LANES))
    for j in range(A_HEADS // 2):
        acc = None
        for c in range(2):
            h = 2 * j + c
            g = h // (A_HEADS // A_KV)
            qh = _half(q[:, j * LANES:(j + 1) * LANES], c == 1)
            s_b = jnp.where(ok, _dot_nt(qh, kb[:, g * LANES:(g + 1) * LANES]) * (HD ** -0.5), NEG)
            kcg = jnp.where((lane < HD) == (g == 0), kc, jnp.zeros_like(kc))
            vcg = jnp.where((lane < HD) == (g == 0), vc, jnp.zeros_like(vc))
            if (g == 1) != (c == 1):
                kcg = pltpu.roll(kcg.astype(F32), HD, 1).astype(BF16)
                vcg = pltpu.roll(vcg.astype(F32), HD, 1).astype(BF16)
            s_c = _dot_nt(qh, kcg) * (HD ** -0.5)
            vh = _half(vb[:, g * LANES:(g + 1) * LANES], c == 1)
            o = _softmax_pv([s_b, s_c], [vh, vcg], sink_ref[h])
            acc = o if acc is None else acc + o
        o_ref[:, j * LANES:(j + 1) * LANES] = acc.astype(BF16)


def _attn_a_sample_call(sink, qa, ka, va, cache_k, cache_v, l):
    nqb = S_S // QB
    off = T_P // S_S
    return pl.pallas_call(
        _attn_a_sample_kernel,
        grid_spec=pltpu.PrefetchScalarGridSpec(
            num_scalar_prefetch=1, grid=(NB_S, nqb),
            in_specs=[pl.BlockSpec((QB, 384), lambda b, n, s: (T_P // QB + b * nqb + n, 0)),
                      pl.BlockSpec((S_S, 256), lambda b, n, s: (off + b, 0)),
                      pl.BlockSpec((S_S, 256), lambda b, n, s: (off + b, 0)),
                      pl.BlockSpec((1, 1, PAST, LANES), lambda b, n, s: (b, l, 0, 0)),
                      pl.BlockSpec((1, 1, PAST, LANES), lambda b, n, s: (b, l, 0, 0))],
            out_specs=pl.BlockSpec((QB, 384), lambda b, n, s: (b * nqb + n, 0))),
        out_shape=jax.ShapeDtypeStruct((T_S, 384), BF16),
        compiler_params=_cparams(("arbitrary", "arbitrary")),
        name=f"attn_a_sample_l{l}",
    )(sink, qa, ka, va, cache_k, cache_v)


def _mla_heads(q, ks, vs, o_ref):
    for j in range(C_HEADS // 2):
        acc = None
        for c in range(2):
            h = 2 * j + c
            qh = q[:, h * LANES:(h + 1) * LANES]
            scores = [_dot_nt(qh, k[:, h * LANES:(h + 1) * LANES]) * (C_QK ** -0.5) for k in ks]
            vals = [_half(v[:, j * LANES:(j + 1) * LANES], c == 1) for v in vs]
            o = _softmax_pv(scores, vals, None)
            acc = o if acc is None else acc + o
        o_ref[:, j * LANES:(j + 1) * LANES] = acc.astype(BF16)


def _mla_prompt_kernel(q_ref, k_ref, v_ref, o_ref):
    _mla_heads(q_ref[...], [k_ref[...]], [v_ref[...]], o_ref)


def _mla_prompt_call(qc, kc, vc):
    return pl.pallas_call(
        _mla_prompt_kernel,
        grid=(NB_P,),
        in_specs=[pl.BlockSpec((S_P, 768), lambda b: (b, 0)),
                  pl.BlockSpec((S_P, 768), lambda b: (b, 0)),
                  pl.BlockSpec((S_P, 384), lambda b: (b, 0))],
        out_specs=pl.BlockSpec((S_P, 384), lambda b: (b, 0)),
        out_shape=jax.ShapeDtypeStruct((T_P, 384), BF16),
        compiler_params=_cparams(("arbitrary",)),
        name="mla_prompt",
    )(qc, kc, vc)


def _mla_sample_kernel(q_ref, kc_ref, vc_ref, kl_ref, vl_ref, o_ref):
    _mla_heads(q_ref[...], [kc_ref[...], kl_ref[...]], [vc_ref[...], vl_ref[...]], o_ref)


def _mla_sample_call(qc, kc, vc):
    tq = 256
    nq = S_S // tq
    return pl.pallas_call(
        _mla_sample_kernel,
        grid=(NB_S, nq),
        in_specs=[pl.BlockSpec((tq, 768), lambda b, n: (T_P // tq + b * nq + n, 0)),
                  pl.BlockSpec((PAST, 768), lambda b, n: (T // PAST + b, 0)),
                  pl.BlockSpec((PAST, 384), lambda b, n: (T // PAST + b, 0)),
                  pl.BlockSpec((S_S, 768), lambda b, n: (T_P // S_S + b, 0)),
                  pl.BlockSpec((S_S, 384), lambda b, n: (T_P // S_S + b, 0))],
        out_specs=pl.BlockSpec((tq, 384), lambda b, n: (b * nq + n, 0)),
        out_shape=jax.ShapeDtypeStruct((T_S, 384), BF16),
        compiler_params=_cparams(("arbitrary", "arbitrary")),
        name="mla_sample",
    )(qc, kc, vc, kc, vc)


def _split3(x):
    x1 = x.astype(BF16)
    r = x - x1.astype(F32)
    x2 = r.astype(BF16)
    x3 = (r - x2.astype(F32)).astype(BF16)
    return x1, x2, x3


def _bmm(a, b):
    return lax.dot_general(a, b, (((2,), (1,)), ((0,), (0,))), preferred_element_type=F32)


def _bmm_nt(a, b):
    return lax.dot_general(a, b, (((2,), (2,)), ((0,), (0,))), preferred_element_type=F32)


def _bmm_tn(a, b):
    return lax.dot_general(a, b, (((1,), (1,)), ((0,), (0,))), preferred_element_type=F32)


def _mlstm_kernel(nc, has_state, *refs):
    for j in range(B_HEADS // 2):
        _mlstm_pair(nc, has_state, j, *refs)


def _mlstm_pair(nc, has_state, j, *refs):
    if has_state:
        (q_ref, k_ref, v_ref, bo_ref, li_ref, lf_ref, hn_ref, c0_ref, n0_ref, m0_ref, ob_ref) = refs
    else:
        (q_ref, k_ref, v_ref, bo_ref, li_ref, lf_ref, hn_ref, ob_ref, cn_ref, nn_ref, mn_ref) = refs
    s_len = nc * CHUNK
    pair = slice(j * LANES, (j + 1) * LANES)
    c3 = lambda x: x.reshape(nc, CHUNK, x.shape[-1])
    li3 = c3(li_ref[:, pair])
    lf3 = c3(lf_ref[:, pair])
    row = lax.broadcasted_iota(jnp.int32, (CHUNK, CHUNK), 0)
    colv = lax.broadcasted_iota(jnp.int32, (CHUNK, CHUNK), 1)
    tri_f = colv <= row
    tri_b = colv >= row
    bcast = lambda m: jnp.broadcast_to(m[None], (nc, CHUNK, CHUNK))
    lane1 = _lane((1, LANES))
    fwd_lane = lane1 < 2

    lf_parts = _split3(lf3)
    tf3 = bcast(tri_f.astype(BF16))
    tb3 = bcast(tri_b.astype(BF16))
    bc_f = _bmm(tf3, lf_parts[0]) + _bmm(tf3, lf_parts[1]) + _bmm(tf3, lf_parts[2])
    bc_b = _bmm(tb3, lf_parts[0]) + _bmm(tb3, lf_parts[1]) + _bmm(tb3, lf_parts[2])
    bc3 = jnp.where(fwd_lane, bc_f, bc_b)
    row3 = lax.broadcasted_iota(jnp.int32, (nc, CHUNK, LANES), 1)
    edge = jnp.where(row3 == jnp.where(fwd_lane, CHUNK - 1, 0), bc3, 0.0)
    bl3 = jnp.sum(edge, axis=1, keepdims=True)
    gg3 = bl3 - bc3 + li3
    bl2 = jnp.sum(edge, axis=1)
    mg2 = gg3.max(axis=1)

    m0 = m0_ref[j] if has_state else jnp.zeros((1, LANES), F32)
    mf = m0
    mb = m0
    mf_prev, mf_next, mb_prev, mb_next = {}, {}, {}, {}
    for i in range(nc):
        mf_prev[i] = mf
        mf = jnp.maximum(bl2[i:i + 1] + mf, mg2[i:i + 1])
        mf_next[i] = mf
        cb = nc - 1 - i
        mb_prev[cb] = mb
        mb = jnp.maximum(bl2[cb:cb + 1] + mb, mg2[cb:cb + 1])
        mb_next[cb] = mb
    m_prev = [jnp.where(fwd_lane, mf_prev[c], mb_prev[c]) for c in range(nc)]
    m_next = [jnp.where(fwd_lane, mf_next[c], mb_next[c]) for c in range(nc)]
    m_prev3 = jnp.stack(m_prev)
    m_next3 = jnp.stack(m_next)
    dec2d = jnp.exp(bl2 + jnp.concatenate(m_prev, axis=0) - jnp.concatenate(m_next, axis=0))
    if not has_state:
        mn_ref[j] = jnp.where(fwd_lane, mf, mb)

    u_parts = _split3(li3 - bc3)
    ws3 = jnp.exp(gg3 - m_next3)
    inter3 = bc3 + m_prev3

    q3 = c3(q_ref[:, pair])
    k3 = c3(k_ref[:, pair])
    v3 = c3(v_ref[:, pair])
    q3f = q3.astype(F32)
    k3f = k3.astype(F32)
    lane3 = _lane((nc, CHUNK, LANES))
    lo3 = lane3 < HD
    rr = lax.broadcasted_iota(jnp.int32, (LANES, LANES), 0)
    cc = lax.broadcasted_iota(jnp.int32, (LANES, LANES), 1)
    blockdiag = (rr < HD) == (cc < HD)
    col = lambda x, k: x[:, :, k:k + 1]

    hsum = None
    for d in range(2):
        causal3 = bcast(tri_f if d == 0 else tri_b)
        a_c, ws_sum, mt_c, ws_c, intra = [], [], [], [], None
        for e in range(2):
            kk = 2 * d + e
            pick = jnp.broadcast_to(jnp.where(_lane((CHUNK, LANES)) == kk, 1.0, 0.0).astype(BF16)[None],
                                    (nc, CHUNK, LANES))
            ub = _bmm_nt(pick, u_parts[0]) + _bmm_nt(pick, u_parts[1]) + _bmm_nt(pick, u_parts[2])
            bc_col = col(bc3, kk)
            d_mat = jnp.where(causal3, bc_col + ub, NEG)
            inter = col(inter3, kk)
            mt = jnp.maximum(inter, d_mat.max(axis=-1, keepdims=True))
            a = jnp.exp(inter - mt)
            w = jnp.exp(d_mat - mt) * _bmm_nt(_half(q3, e == 1), k3)
            o = _bmm(w.astype(BF16), _half(v3, e == 1))
            intra = o if intra is None else intra + o
            a_c.append(a)
            mt_c.append(mt)
            ws_sum.append(w.sum(axis=-1, keepdims=True))
            ws_c.append(col(ws3, kk))
        kw3 = k3f * jnp.where(lo3, ws_c[0], ws_c[1])
        dec2 = jnp.where(_lane((nc, LANES)) < HD, dec2d[:, 2 * d:2 * d + 1], dec2d[:, 2 * d + 1:2 * d + 2])
        u_all = jnp.where(blockdiag, _bmm_tn(kw3.astype(BF16), v3), 0.0)
        kwsum = kw3.sum(axis=1)
        if has_state:
            cst = c0_ref[j, d]
            nst = n0_ref[j, d]
        else:
            cst = jnp.zeros((LANES, LANES), F32)
            nst = jnp.zeros((1, LANES), F32)
        cs, ns = [None] * nc, [None] * nc
        for c in (range(nc) if d == 0 else range(nc - 1, -1, -1)):
            cs[c] = cst
            ns[c] = nst
            cst = dec2[c:c + 1] * cst + u_all[c]
            nst = dec2[c:c + 1] * nst + kwsum[c:c + 1]
        if not has_state:
            cn_ref[j, d] = cst
            nn_ref[j, d] = nst
        qc = _bmm(q3, jnp.stack(cs).astype(BF16))
        qn_all = q3f * jnp.stack(ns)
        dn = []
        for e in range(2):
            qn = jnp.sum(jnp.where(lo3 != (e == 1), qn_all, 0.0), axis=-1, keepdims=True)
            den = a_c[e] * qn + ws_sum[e]
            dn.append(jnp.maximum(jnp.abs(den), jnp.exp(-mt_c[e])))
        h2 = (jnp.where(lo3, a_c[0], a_c[1]) * qc + intra) / jnp.where(lo3, dn[0], dn[1])
        hsum = h2 if hsum is None else hsum + h2

    hs = hsum.reshape(s_len, LANES)
    lo = _lane(hs.shape) < HD
    s = hs * hs
    s_lo = jnp.sum(jnp.where(lo, s, 0.0), axis=-1, keepdims=True)
    s_hi = jnp.sum(jnp.where(lo, 0.0, s), axis=-1, keepdims=True)
    r = lax.rsqrt(jnp.where(lo, s_lo, s_hi) * (1.0 / HD) + EPS)
    ob_ref[:, pair] = (hs * r * hn_ref[...] * jax.nn.sigmoid(bo_ref[:, pair])).astype(BF16)


def _mlstm_call(bq, bk, bv, bo, li, lf, hn2, state, l):
    has_state = state is not None
    if has_state:
        nb, s_len, base = NB_S, S_S, T_P // S_S
    else:
        nb, s_len, base = NB_P, S_P, 0
    nc = s_len // CHUNK
    tokp = pl.BlockSpec((s_len, 2 * LANES), lambda b: (base + b, 0))
    in_specs = [tokp, tokp, tokp, tokp, tokp, tokp, pl.BlockSpec((1, LANES), lambda b: (0, 0))]
    args = [bq, bk, bv, bo, li, lf, hn2]
    ob_spec = pl.BlockSpec((s_len, 2 * LANES), lambda b: (b, 0))
    ob_shape = jax.ShapeDtypeStruct((nb * s_len, 256), BF16)
    state_specs = [pl.BlockSpec((2, 2, LANES, LANES), lambda b: (b, 0, 0, 0)),
                   pl.BlockSpec((2, 2, 1, LANES), lambda b: (b, 0, 0, 0)),
                   pl.BlockSpec((2, 1, LANES), lambda b: (b, 0, 0))]
    if has_state:
        c0, n0, m0 = state
        in_specs += state_specs
        args += [c0, n0, m0]
        out_specs = ob_spec
        out_shape = ob_shape
    else:
        out_specs = [ob_spec] + state_specs
        out_shape = [ob_shape,
                     jax.ShapeDtypeStruct((nb * 2, 2, LANES, LANES), F32),
                     jax.ShapeDtypeStruct((nb * 2, 2, 1, LANES), F32),
                     jax.ShapeDtypeStruct((nb * 2, 1, LANES), F32)]
    return pl.pallas_call(
        functools.partial(_mlstm_kernel, nc, has_state),
        grid=(nb,),
        in_specs=in_specs,
        out_specs=out_specs,
        out_shape=out_shape,
        compiler_params=_cparams(("arbitrary",)),
        name=f"mlstm_{'sample' if has_state else 'prompt'}_l{l}",
    )(*args)


def _out_kernel(moe, xp_ref, xs_ref, oap_ref, oas_ref, obp_ref, obs_ref, ocp_ref, ocs_ref,
                w_ref, g1_ref, sh_ref, sc_ref, n2_ref, *rest):
    if moe:
        rh_ref, rl_ref, x1_o, xn_o, route_o, cnt_o, run_scr = rest
    else:
        x1_o, xn_o = rest
    i = pl.program_id(0)
    o = (_dot(_pick(i, oap_ref, oas_ref), w_ref[0, 0:384, :])
         + _dot(_pick(i, obp_ref, obs_ref), w_ref[0, 384:640, :])
         + _dot(_pick(i, ocp_ref, ocs_ref), w_ref[0, 640:1024, :]))
    x1 = _pick(i, xp_ref, xs_ref) + g1_ref[0] * o
    x1_o[...] = x1
    xn = x1 * lax.rsqrt(jnp.mean(x1 * x1, axis=-1, keepdims=True) + EPS) * n2_ref[...]
    xn = xn * (1.0 + sc_ref[0]) + sh_ref[0]
    xb = xn.astype(BF16)
    if not moe:
        xn_o[...] = xb
    else:
        xn_o[...] = xn
        xl = (xn - xb.astype(F32)).astype(BF16)
        logits = _dot(xb, rh_ref[0]) + (_dot(xl, rh_ref[0]) + _dot(xb, rl_ref[0]))
        lane = _lane(logits.shape)
        logits = jnp.where(lane < N_EXP, logits, -jnp.inf)
        m1 = logits.max(axis=-1, keepdims=True)
        i1 = jnp.min(jnp.where(logits == m1, lane, LANES), axis=-1, keepdims=True)
        rest_l = jnp.where(lane == i1, -jnp.inf, logits)
        m2 = rest_l.max(axis=-1, keepdims=True)
        i2 = jnp.min(jnp.where(rest_l == m2, lane, LANES), axis=-1, keepdims=True)
        e2 = jnp.exp(m2 - m1)
        den = 1.0 + e2
        w1 = 1.0 / den
        w2 = e2 / den

        @pl.when(pl.program_id(0) == 0)
        def _():
            run_scr[...] = jnp.zeros(run_scr.shape, F32)

        sel = jnp.where(lane == i1, 1.0, jnp.where((lane == i2) & (w2 > 0.0), 1.0, 0.0))
        rb = 256
        before = (lax.broadcasted_iota(jnp.int32, (rb, rb), 1)
                  < lax.broadcasted_iota(jnp.int32, (rb, rb), 0)).astype(BF16)
        run = run_scr[...]
        ranks = []
        for k in range(TM // rb):
            part = sel[k * rb:(k + 1) * rb]
            ranks.append(run + _dot(before, part.astype(BF16)))
            run = run + jnp.sum(part, axis=0, keepdims=True)
        rank = jnp.concatenate(ranks, axis=0)
        r1 = jnp.sum(jnp.where(lane == i1, rank, 0.0), axis=-1, keepdims=True)
        r2 = jnp.sum(jnp.where(lane == i2, rank, 0.0), axis=-1, keepdims=True)
        run_scr[...] = run
        cnt_o[...] = run_scr[...]
        fields = [i1.astype(F32), i2.astype(F32), r1, r2, w1, w2]
        info = jnp.zeros(lane.shape, F32)
        for k, v in enumerate(fields):
            info = jnp.where(lane == k, v, info)
        route_o[...] = info.T


def _out_call(x_p, x_s, sample_off, oa, ob, oc, w_out_b, mod3, l, n2g, router):
    moe = router is not None
    row = lambda k: pl.BlockSpec((1, 1, D), lambda i: ((l * 8 + _mod_row(i)) * 6 + k, 0, 0))
    tok = lambda n: pl.BlockSpec((TM, n), lambda i: (i, 0))
    in_specs = (_pair_specs(D, sample_off) + _pair_specs(384, 0) + _pair_specs(256, 0) + _pair_specs(384, 0)
                + [pl.BlockSpec((1, D, D), lambda i: (l, 0, 0)),
                   row(2), row(3), row(4), pl.BlockSpec((1, D), lambda i: (0, 0))])
    args = [x_p, x_s, *oa, *ob, *oc, w_out_b, mod3, mod3, mod3, n2g]
    out_specs = [tok(D), tok(D)]
    out_shape = [jax.ShapeDtypeStruct((T, D), F32), jax.ShapeDtypeStruct((T, D), BF16)]
    if moe:
        out_shape[1] = jax.ShapeDtypeStruct((T, D), F32)
        rh, rl = router
        in_specs += [pl.BlockSpec((1, D, LANES), lambda i: (0, 0, 0))] * 2
        args += [rh, rl]
        out_specs += [pl.BlockSpec((LANES, TM), lambda i: (0, i)), pl.BlockSpec((1, LANES), lambda i: (0, 0))]
        out_shape += [jax.ShapeDtypeStruct((LANES, T), F32), jax.ShapeDtypeStruct((1, LANES), F32)]
    return pl.pallas_call(
        functools.partial(_out_kernel, moe),
        grid=(NT,),
        in_specs=in_specs, out_specs=out_specs, out_shape=out_shape,
        scratch_shapes=[pltpu.VMEM((1, LANES), F32)] if moe else [],
        compiler_params=_cparams(("arbitrary",)),
        name=f"out_proj_l{l}",
    )(*args)


FFN_TM = 1024
FFN_TF = 256
FFN_TN = 256


def _hidden_tile(x, w1_ref, w3_ref):
    a = _dot(x, w1_ref[...].astype(BF16))
    b = _dot(x, w3_ref[...].astype(BF16))
    return (a * jax.nn.sigmoid(a) * b).astype(BF16)


def _down_tile(h_scr, w2n, m):
    out = None
    for f in range(h_scr.shape[0]):
        o = _dot(h_scr[f, 0:m, :], w2n[f])
        out = o if out is None else out + o
    return out


def _keep_weights(s, w1_ref, w3_ref, w2_ref, w1b, w3b, w2b, tn):
    w1b[s] = w1_ref[...].astype(BF16)
    w3b[s] = w3_ref[...].astype(BF16)
    w2 = w2_ref[...].astype(BF16)
    for n in range(w2b.shape[0]):
        w2b[n, s] = w2[:, n * tn:(n + 1) * tn]


def _ffn_kernel(nf, xn_ref, x1_ref, g2_ref, w1_ref, w3_ref, w2_ref, y_ref, h_scr, w1b, w3b, w2b):
    i = pl.program_id(0)
    s = pl.program_id(1)

    @pl.when((i == 0) & (s < nf))
    def _():
        _keep_weights(s, w1_ref.at[0], w3_ref.at[0], w2_ref.at[0], w1b, w3b, w2b, FFN_TN)

    @pl.when(s < nf)
    def _():
        h_scr[s] = _hidden_tile(xn_ref[...], w1b.at[s], w3b.at[s])

    @pl.when(s >= nf)
    def _():
        y_ref[...] = x1_ref[...] + g2_ref[0] * _down_tile(h_scr, w2b.at[s - nf], FFN_TM)


def _ffn_call(xn, x1, mod3, l, w1, w3, w2, i_layer):
    nf = D_FF // FFN_TF
    nn = D // FFN_TN
    tile = lambda i, s: jnp.where(i == 0, jnp.minimum(s, nf - 1), nf - 1)
    out = lambda s: jnp.maximum(s - nf, 0)

    def g2_idx(i, s):
        r = jnp.where(i < T_P // FFN_TM, 0, 1 + (i - T_P // FFN_TM) // (S_S // FFN_TM))
        return ((l * 8 + r) * 6 + 5, 0, out(s))

    return pl.pallas_call(
        functools.partial(_ffn_kernel, nf),
        grid=(T // FFN_TM, nf + nn),
        in_specs=[pl.BlockSpec((FFN_TM, D), lambda i, s: (i, 0)),
                  pl.BlockSpec((FFN_TM, FFN_TN), lambda i, s: (i, out(s))),
                  pl.BlockSpec((1, 1, FFN_TN), g2_idx),
                  pl.BlockSpec((1, D, FFN_TF), lambda i, s: (i_layer, 0, tile(i, s))),
                  pl.BlockSpec((1, D, FFN_TF), lambda i, s: (i_layer, 0, tile(i, s))),
                  pl.BlockSpec((1, FFN_TF, D), lambda i, s: (i_layer, tile(i, s), 0))],
        out_specs=pl.BlockSpec((FFN_TM, FFN_TN), lambda i, s: (i, out(s))),
        out_shape=jax.ShapeDtypeStruct((T, D), F32),
        scratch_shapes=[pltpu.VMEM((nf, FFN_TM, FFN_TF), BF16),
                        pltpu.VMEM((nf, D, FFN_TF), BF16), pltpu.VMEM((nf, D, FFN_TF), BF16),
                        pltpu.VMEM((nn, nf, FFN_TF, FFN_TN), BF16)],
        compiler_params=_cparams(("arbitrary", "arbitrary")),
        name="ffn_dense",
    )(xn, x1, mod3, w1, w3, w2)


MOE_BM = 1024
MOE_SUB = 256
MOE_NBLK = 2 * T // MOE_BM + N_EXP
MOE_NR = MOE_NBLK * MOE_BM
MOE_TF = 512
MOE_TN = 256
FIN_TM = 512


def _route_kernel(i1_ref, i2_ref, r1_ref, r2_ref, cnt_ref, w2_ref,
                  src_ref, eb_ref, rows_ref, nvb_ref, off_scr):
    def clear(r, carry):
        src_ref[r] = 0
        return carry

    lax.fori_loop(0, MOE_NR, clear, 0, unroll=8)

    def clear_blk(b, carry):
        eb_ref[b] = 0
        rows_ref[b] = 0
        return carry

    lax.fori_loop(0, MOE_NBLK, clear_blk, 0)

    nblk = jnp.int32(0)
    for e in range(N_EXP):
        c = cnt_ref[e]
        nbe = (c + (MOE_BM - 1)) // MOE_BM
        off_scr[e] = nblk * MOE_BM

        def fill(j, carry, e=e, c=c, nblk=nblk):
            eb_ref[nblk + j] = e
            rows_ref[nblk + j] = jnp.minimum(c - j * MOE_BM, MOE_BM)
            return carry

        lax.fori_loop(0, nbe, fill, 0)
        nblk = nblk + nbe
    nvb_ref[0] = nblk

    def place(t, carry):
        src_ref[off_scr[i1_ref[t]] + r1_ref[t]] = t

        @pl.when(w2_ref[t] > 0.0)
        def _():
            src_ref[off_scr[i2_ref[t]] + r2_ref[t]] = t

        return carry

    lax.fori_loop(0, T, place, 0, unroll=4)


def _route_call(route, cnt):
    ints = [route[k].astype(jnp.int32) for k in range(4)]
    cnt8 = cnt[0, :N_EXP].astype(jnp.int32)
    smem = pl.BlockSpec(memory_space=pltpu.SMEM)
    outs = pl.pallas_call(
        _route_kernel,
        grid_spec=pltpu.PrefetchScalarGridSpec(
            num_scalar_prefetch=5, grid=(1,),
            in_specs=[smem],
            out_specs=[smem, smem, smem, smem],
            scratch_shapes=[pltpu.SMEM((N_EXP,), jnp.int32)]),
        out_shape=[jax.ShapeDtypeStruct((MOE_NR,), jnp.int32),
                   jax.ShapeDtypeStruct((MOE_NBLK,), jnp.int32), jax.ShapeDtypeStruct((MOE_NBLK,), jnp.int32),
                   jax.ShapeDtypeStruct((1,), jnp.int32)],
        compiler_params=_cparams(("arbitrary",)),
        name="moe_route",
    )(*ints, cnt8, route[5])
    return tuple(outs) + (ints[0], route[4], route[5])


def _moe_gather_kernel(src_ref, rows_ref, x_ref, o_ref, rows_scr):
    k = pl.program_id(0)
    base = k * MOE_SUB
    used = rows_ref[k // (MOE_BM // MOE_SUB)] > (k % (MOE_BM // MOE_SUB)) * MOE_SUB

    @pl.when(used)
    def _():
        def body(i, carry):
            for j in range(8):
                r = i * 8 + j
                rows_scr[pl.ds(r, 1), :] = x_ref[pl.ds(src_ref[base + r], 1), :]
            return carry

        lax.fori_loop(0, MOE_SUB // 8, body, 0)
        o_ref[...] = rows_scr[...].astype(BF16)

    @pl.when(jnp.logical_not(used))
    def _():
        o_ref[...] = jnp.zeros(o_ref.shape, o_ref.dtype)


def _moe_gather_call(src, rows_b, xn):
    return pl.pallas_call(
        _moe_gather_kernel,
        grid_spec=pltpu.PrefetchScalarGridSpec(
            num_scalar_prefetch=2, grid=(MOE_NR // MOE_SUB,),
            in_specs=[pl.BlockSpec((T, D), lambda k, s, rw: (0, 0), pipeline_mode=pl.Buffered(1))],
            out_specs=pl.BlockSpec((MOE_SUB, D), lambda k, s, rw: (k, 0)),
            scratch_shapes=[pltpu.VMEM((MOE_SUB, D), F32)]),
        out_shape=jax.ShapeDtypeStruct((MOE_NR, D), BF16),
        compiler_params=_cparams(("arbitrary",)),
        name="moe_gather",
    )(src, rows_b, xn)


def _moe_first(b, eb_ref):
    return (b == 0) | (eb_ref[b] != eb_ref[jnp.maximum(b - 1, 0)])


def _moe_kernel(nf, eb_ref, rows_ref, nvb_ref, xs_ref, w1_ref, w3_ref, w2_ref, y_ref, h_scr, w1b, w3b, w2b):
    b = pl.program_id(0)
    s = pl.program_id(1)
    nsub = jnp.where(b < nvb_ref[0], (rows_ref[b] + MOE_SUB - 1) // MOE_SUB, 0)
    sizes = [k * MOE_SUB for k in range(1, MOE_BM // MOE_SUB + 1)]

    @pl.when((s < nf) & (nsub > 0) & _moe_first(b, eb_ref))
    def _():
        _keep_weights(s, w1_ref.at[0, 0], w3_ref.at[0, 0], w2_ref.at[0, 0], w1b, w3b, w2b, MOE_TN)

    @pl.when((s < nf) & (nsub > 0))
    def _():
        for k, m in enumerate(sizes, start=1):
            @pl.when(nsub == k)
            def _(m=m):
                h_scr[s, 0:m, :] = _hidden_tile(xs_ref[0:m, :], w1b.at[s], w3b.at[s])

    @pl.when((s >= nf) & (nsub > 0))
    def _():
        for k, m in enumerate(sizes, start=1):
            @pl.when(nsub == k)
            def _(m=m):
                y_ref[0:m, :] = _down_tile(h_scr, w2b.at[s - nf], m)
                if m < MOE_BM:
                    y_ref[m:MOE_BM, :] = jnp.zeros((MOE_BM - m, MOE_TN), F32)

    @pl.when((s >= nf) & (nsub == 0))
    def _():
        y_ref[...] = jnp.zeros(y_ref.shape, F32)


def _moe_call(xs, e_b, rows_b, nvb, w1, w3, w2, i_layer):
    nf = D_FFE // MOE_TF
    nn = D // MOE_TN

    def blk(b, nv):
        return jnp.minimum(b, nv[0] - 1)

    def tile(b, s, eb, nv):
        stream = (b < nv[0]) & _moe_first(b, eb)
        return jnp.where(stream, jnp.minimum(s, nf - 1), nf - 1)

    return pl.pallas_call(
        functools.partial(_moe_kernel, nf),
        grid_spec=pltpu.PrefetchScalarGridSpec(
            num_scalar_prefetch=3, grid=(MOE_NBLK, nf + nn),
            in_specs=[pl.BlockSpec((MOE_BM, D), lambda b, s, eb, rw, nv: (blk(b, nv), 0)),
                      pl.BlockSpec((1, 1, D, MOE_TF),
                                   lambda b, s, eb, rw, nv: (i_layer, eb[blk(b, nv)], 0, tile(b, s, eb, nv))),
                      pl.BlockSpec((1, 1, D, MOE_TF),
                                   lambda b, s, eb, rw, nv: (i_layer, eb[blk(b, nv)], 0, tile(b, s, eb, nv))),
                      pl.BlockSpec((1, 1, MOE_TF, D),
                                   lambda b, s, eb, rw, nv: (i_layer, eb[blk(b, nv)], tile(b, s, eb, nv), 0))],
            out_specs=pl.BlockSpec((MOE_BM, MOE_TN), lambda b, s, eb, rw, nv: (b, jnp.maximum(s - nf, 0))),
            scratch_shapes=[pltpu.VMEM((nf, MOE_BM, MOE_TF), BF16),
                            pltpu.VMEM((nf, D, MOE_TF), BF16), pltpu.VMEM((nf, D, MOE_TF), BF16),
                            pltpu.VMEM((nn, nf, MOE_TF, MOE_TN), BF16)]),
        out_shape=jax.ShapeDtypeStruct((MOE_NR, D), F32),
        compiler_params=_cparams(("arbitrary", "arbitrary"), vmem=56 * 1024 * 1024),
        name="moe_experts",
    )(e_b, rows_b, nvb, xs, w1, w3, w2)


def _moe_combine_kernel(src_ref, eb_ref, rows_ref, nvb_ref, i1_ref, w1_ref, w2_ref,
                        ys_ref, x1_ref, g2_ref, yp_ref, ys_out_ref, acc):
    i = pl.program_id(0)

    @pl.when(i == 0)
    def _():
        acc[...] = jnp.zeros(acc.shape, F32)

    @pl.when(i < nvb_ref[0])
    def _():
        base = i * MOE_BM
        n = rows_ref[i]
        e = eb_ref[i]

        def gate(t):
            return jnp.where(i1_ref[t] == e, w1_ref[t], w2_ref[t])

        def add_rows(r0, cnt):
            toks = [src_ref[base + r0 + j] for j in range(cnt)]
            vals = [acc[pl.ds(toks[j], 1), :] + gate(toks[j]) * ys_ref[pl.ds(r0 + j, 1), :]
                    for j in range(cnt)]
            for j in range(cnt):
                acc[pl.ds(toks[j], 1), :] = vals[j]

        def body4(q, carry):
            add_rows(q * 4, 4)
            return carry

        lax.fori_loop(0, n // 4, body4, 0)

        def body1(r, carry):
            add_rows(r, 1)
            return carry

        lax.fori_loop((n // 4) * 4, n, body1, 0)

    @pl.when(i >= MOE_NBLK)
    def _():
        t0 = pl.multiple_of((i - MOE_NBLK) * FIN_TM, FIN_TM)
        y = x1_ref[...] + g2_ref[0] * acc[pl.ds(t0, FIN_TM), :]

        @pl.when(i - MOE_NBLK < T_P // FIN_TM)
        def _():
            yp_ref[...] = y

        @pl.when(i - MOE_NBLK >= T_P // FIN_TM)
        def _():
            ys_out_ref[...] = y


def _moe_combine_call(src, e_b, rows_b, nvb, i1, w1, w2, ys, x1, mod3, l):
    nfin = T // FIN_TM
    smem = pl.BlockSpec(memory_space=pltpu.SMEM)

    def g2_idx(i, *_):
        j = jnp.maximum(i - MOE_NBLK, 0)
        r = jnp.where(j < T_P // FIN_TM, 0, 1 + (j - T_P // FIN_TM) // (S_S // FIN_TM))
        return ((l * 8 + r) * 6 + 5, 0, 0)

    return pl.pallas_call(
        _moe_combine_kernel,
        grid_spec=pltpu.PrefetchScalarGridSpec(
            num_scalar_prefetch=5, grid=(MOE_NBLK + nfin,),
            in_specs=[smem, smem,
                      pl.BlockSpec((MOE_BM, D),
                                   lambda i, s, eb, rw, nv, t1: (jnp.minimum(jnp.minimum(i, MOE_NBLK - 1), nv[0] - 1), 0)),
                      pl.BlockSpec((FIN_TM, D), lambda i, *_: (jnp.maximum(i - MOE_NBLK, 0), 0)),
                      pl.BlockSpec((1, 1, D), g2_idx)],
            out_specs=[pl.BlockSpec((FIN_TM, D),
                                    lambda i, *_: (jnp.clip(i - MOE_NBLK, 0, T_P // FIN_TM - 1), 0)),
                       pl.BlockSpec((FIN_TM, D),
                                    lambda i, *_: (jnp.maximum(i - MOE_NBLK - T_P // FIN_TM, 0), 0))],
            scratch_shapes=[pltpu.VMEM((T, D), F32)]),
        out_shape=[jax.ShapeDtypeStruct((T_P, D), F32), jax.ShapeDtypeStruct((T_S, D), F32)],
        compiler_params=_cparams(("arbitrary",), vmem=52 * 1024 * 1024),
        name="moe_combine",
    )(src, e_b, rows_b, nvb, i1, w1, w2, ys, x1, mod3)


def _moe_layer(xp, x1, route, cnt, mod3, l, w1, w3, w2, i_layer):
    src, e_b, rows_b, nvb, i1, g1, g2 = _route_call(route, cnt)
    xs = _moe_gather_call(src, rows_b, xp)
    ys = _moe_call(xs, e_b, rows_b, nvb, w1, w3, w2, i_layer)
    return _moe_combine_call(src, e_b, rows_b, nvb, i1, g1, g2, ys, x1, mod3, l)


def _pad_cols(w, n):
    return jnp.pad(w, ((0, 0), (0, n - w.shape[1])))


def _relayout_w_in(w):
    aq, ak, av, bq, bk, bv, bo, bg, cq, ckv, ckr = jnp.split(
        w, (384, 512, 640, 896, 1152, 1408, 1664, 1680, 1936, 2064), axis=1)
    dup = lambda m: jnp.concatenate([m[:, 0:HD], m[:, 0:HD], m[:, HD:2 * HD], m[:, HD:2 * HD]], axis=1)
    g = bg.reshape(D, 2, 2, 2, 2)

    def gate_cols(gi):
        cols = []
        for j in range(2):
            cols.append(_pad_cols(g[:, :, gi, j, :].reshape(D, 4), LANES))
        return jnp.concatenate(cols, axis=1)

    ckr_p = jnp.pad(ckr, ((0, 0), (C_NOPE, LANES - C_NOPE - C_ROPE)))
    out = jnp.concatenate([aq, dup(ak), dup(av), bq, bk, bv, bo, gate_cols(0), gate_cols(1),
                           cq, ckv, ckr_p], axis=1)
    return out.astype(BF16)


def _relayout_gate_b(b):
    g = b.reshape(2, 2, 2, 2)

    def cols(gi):
        return jnp.concatenate([jnp.pad(g[:, gi, j, :].reshape(4), (0, LANES - 4)) for j in range(2)])

    return jnp.concatenate([cols(0), cols(1)])[None, :]


def _pad_heads(w, width):
    r = w.shape[0]
    h = w.shape[1] // width
    return jnp.pad(w.reshape(r, h, width), ((0, 0), (0, 0), (0, LANES - width))).reshape(r, h * LANES)


def _rope_tables(half, span_start, period):
    rows = S_S // 64
    r = jnp.repeat(jnp.arange(rows), 64).astype(F32)
    c = jnp.tile(jnp.arange(64), rows).astype(F32)
    n_freq = half // 2
    freq = 10000.0 ** (-jnp.arange(n_freq, dtype=F32) / n_freq)
    ang = jnp.concatenate([r[:, None] * freq, c[:, None] * freq], axis=-1)
    cos, sin = jnp.cos(ang), jnp.sin(ang)
    d = (jnp.arange(LANES) - span_start) % period
    inside = d < 2 * half
    p = jnp.where(inside, d % half, 0)
    first = inside & (d < half)
    second = inside & (d >= half)
    ct = jnp.where(inside[None, :], cos[:, p], 1.0)
    sa = jnp.where(first[None, :], -sin[:, p], 0.0)
    sb = jnp.where(second[None, :], sin[:, p], 0.0)
    ident = (jnp.ones((TM, LANES), F32), jnp.zeros((TM, LANES), F32), jnp.zeros((TM, LANES), F32))
    return tuple(jnp.concatenate([i0, t], axis=0) for i0, t in zip(ident, (ct, sa, sb)))


def kernel(x_prompt, x_sample, c, cache_swa_k, cache_swa_v, cache_mla_ckv, cache_mla_krope, state_mlstm_C, state_mlstm_n, state_mlstm_m, c_ctx, ada_w, ada_b, norm1_g, norm2_g, w_in, a_qn_g, a_kn_g, a_sink, b_gate_b, b_hn_g, c_qa_g, c_kva_g, c_wuq, c_wukv, c_qn_g, c_kn_g, w_out, ffn_w1, ffn_w3, ffn_w2, moe_router, moe_w1, moe_w3, moe_w2):
    x_p, x_s, sample_off = x_prompt.reshape(T_P, D), x_sample.reshape(T_S, D), 0
    cv = jnp.concatenate([c_ctx[None, :], c, jnp.zeros((5, D), F32)], axis=0)
    mod = _ada_call(cv, ada_w, ada_b)
    mod3 = mod.reshape(DEPTH * 8 * 6, 1, D)

    rope_a = _rope_tables(32, 0, HD)
    rope_c = _rope_tables(16, C_NOPE, LANES)

    w_in_p = jnp.stack([_relayout_w_in(w_in[l]) for l in range(DEPTH)])
    wuq_p = jnp.stack([_pad_heads(c_wuq[l], C_QK) for l in range(DEPTH)]).astype(BF16)
    wukv = c_wukv.reshape(DEPTH, C_KVRANK, C_HEADS, C_NOPE + C_V)
    wk_p = jnp.pad(wukv[..., :C_NOPE], ((0, 0), (0, 0), (0, 0), (0, LANES - C_NOPE)))
    wk_p = wk_p.reshape(DEPTH, C_KVRANK, C_HEADS * LANES).astype(BF16)
    wv_p = wukv[..., C_NOPE:].reshape(DEPTH, C_KVRANK, C_HEADS * C_V).astype(BF16)
    w_out_b = w_out.astype(BF16)
    cache_k4 = cache_swa_k.reshape(NB_S, DEPTH, PAST, A_KV * HD)
    cache_v4 = cache_swa_v.reshape(NB_S, DEPTH, PAST, A_KV * HD)

    news = []
    for l in range(DEPTH):
        gqa = jnp.tile(a_qn_g[l], A_HEADS)[None, :]
        gka = jnp.tile(a_kn_g[l], 2 * A_KV)[None, :]
        gb = _relayout_gate_b(b_gate_b[l])
        gqn = jnp.tile(jnp.pad(c_qn_g[l], (0, LANES - C_QK)), C_HEADS)[None, :]
        gkn = jnp.tile(jnp.pad(c_kn_g[l], (0, LANES - C_QK)), C_HEADS)[None, :]
        (qa, ka, va, kaf, vaf, bq, bk, bv, bo, li, lf, qc, ckv_n, ckr) = _in_call(
            x_p, x_s, sample_off, mod3, l, norm1_g[l][None, :], w_in_p, gqa, gka, gb, c_qa_g[l][None, :],
            c_kva_g[l][None, :], wuq_p, gqn, rope_a, rope_c)

        ckr_cache = jnp.pad(cache_mla_krope[:, l].reshape(NB_S * PAST, C_ROPE),
                            ((0, 0), (C_NOPE, LANES - C_NOPE - C_ROPE)))
        kc, vc = _mlakv_call(ckv_n, cache_mla_ckv[:, l].reshape(NB_S * PAST, C_KVRANK), ckr, ckr_cache,
                             l, wk_p, wv_p, gkn, rope_c)

        sink = a_sink[l]
        oa_p = _attn_a_prompt_call(sink, qa, ka, va)
        oa_s = _attn_a_sample_call(sink, qa, ka, va, cache_k4, cache_v4, l)
        oc_p = _mla_prompt_call(qc, kc, vc)
        oc_s = _mla_sample_call(qc, kc, vc)

        hn2 = jnp.tile(b_hn_g[l], 2)[None, :]
        ob_p, cn, nn, mn = _mlstm_call(bq, bk, bv, bo, li, lf, hn2, None, l)
        c_st = state_mlstm_C[:, l].reshape(NB_S, 2, 2, 2, HD, HD)
        c_st = jnp.transpose(c_st, (0, 2, 1, 3, 4, 5))
        c0 = jnp.zeros((NB_S, 2, 2, LANES, LANES), F32)
        c0 = c0.at[..., :HD, :HD].set(c_st[:, :, :, 0]).at[..., HD:, HD:].set(c_st[:, :, :, 1])
        c0 = c0.reshape(NB_S * 2, 2, LANES, LANES)
        n_st = state_mlstm_n[:, l].reshape(NB_S, 2, 2, 1, LANES)
        n0 = jnp.transpose(n_st, (0, 2, 1, 3, 4)).reshape(NB_S * 2, 2, 1, LANES)
        m_st = state_mlstm_m[:, l].reshape(NB_S, 2, 2, 2)
        m0 = jnp.transpose(m_st, (0, 2, 1, 3)).reshape(NB_S * 2, 1, 4)
        m0 = jnp.pad(m0, ((0, 0), (0, 0), (0, LANES - 4)))
        ob_s = _mlstm_call(bq, bk, bv, bo, li, lf, hn2, (c0, n0, m0), l)

        oa, ob, oc = (oa_p, oa_s), (ob_p, ob_s), (oc_p, oc_s)

        if l % 2 == 0:
            x1, xn = _out_call(x_p, x_s, sample_off, oa, ob, oc, w_out_b, mod3, l, norm2_g[l][None, :], None)
            x = _ffn_call(xn, x1, mod3, l, ffn_w1, ffn_w3, ffn_w2, l // 2)
            x_p, x_s, sample_off = x, x, NT_P
        else:
            r = _pad_cols(moe_router[l // 2], LANES)
            rh = r.astype(BF16)
            rl = (r - rh.astype(F32)).astype(BF16)
            x1, xn, route, cnt = _out_call(x_p, x_s, sample_off, oa, ob, oc, w_out_b, mod3, l,
                                           norm2_g[l][None, :], (rh[None], rl[None]))
            x_p, x_s = _moe_layer(xn, x1, route, cnt, mod3, l, moe_w1, moe_w3, moe_w2, l // 2)
            sample_off = 0

        new_k = kaf.reshape(NB_P, S_P, A_KV, HD)
        new_v = vaf.reshape(NB_P, S_P, A_KV, HD)
        new_ckv = ckv_n[:T_P].reshape(NB_P, S_P, C_KVRANK)
        new_kr = ckr[:T_P, C_NOPE:C_NOPE + C_ROPE].reshape(NB_P, S_P, C_ROPE)
        cn6 = cn.reshape(NB_P, 2, 2, LANES, LANES)
        c_e = jnp.stack([cn6[..., :HD, :HD], cn6[..., HD:, HD:]], axis=3)
        new_c = jnp.transpose(c_e, (0, 2, 1, 3, 4, 5)).reshape(NB_P, 2, B_HEADS, HD, HD)
        nn5 = nn.reshape(NB_P, 2, 2, 2, HD)
        new_n = jnp.transpose(nn5, (0, 2, 1, 3, 4)).reshape(NB_P, 2, B_HEADS, HD)
        mn4 = mn.reshape(NB_P, 2, LANES)[:, :, :4].reshape(NB_P, 2, 2, 2)
        new_m = jnp.transpose(mn4, (0, 2, 1, 3)).reshape(NB_P, 2, B_HEADS)
        news.append((new_k, new_v, new_ckv, new_kr, new_c, new_n, new_m))

    y_prompt = x_p[:T_P].reshape(NB_P, S_P, D)
    y_sample = x_s[sample_off * TM:sample_off * TM + T_S].reshape(NB_S, S_S, D)
    stacked = tuple(jnp.stack([nw[j] for nw in news], axis=1) for j in range(7))
    return (y_prompt, y_sample) + stacked
```

```python
import functools

import jax
import jax.numpy as jnp
from jax import lax
from jax.experimental import pallas as pl
from jax.experimental.pallas import tpu as pltpu

F32 = jnp.float32
BF16 = jnp.bfloat16

D = 1024
NB_P, S_P = 16, 256
NB_S, S_S = 2, 1024
PAST = 512
DEPTH = 2
T_P = NB_P * S_P
T_S = NB_S * S_S
T = T_P + T_S
TM = 512
NT = T // TM
NT_P = T_P // TM
HD = 64
A_HEADS, A_KV = 6, 2
B_HEADS = 4
CHUNK = 64
C_HEADS = 6
C_QRANK, C_KVRANK, C_NOPE, C_ROPE, C_V = 256, 128, 64, 32, 64
C_QK = C_NOPE + C_ROPE
D_FF = 2816
N_EXP = 8
D_FFE = 3584
EPS = 1e-6
NEG = -1e30
LANES = 128
VMEM_LIMIT = 48 * 1024 * 1024

SEG = dict(QA=(0, 384), KA=(384, 256), VA=(640, 256), BQ=(896, 256), BK=(1152, 256),
           BV=(1408, 256), BO=(1664, 256), GI=(1920, 256), GF=(2176, 256), CQ=(2432, 256),
           CKV=(2688, 128), CKR=(2816, 128))
NP_IN = 2944


def _cparams(sem, vmem=VMEM_LIMIT):
    return pltpu.CompilerParams(dimension_semantics=sem, vmem_limit_bytes=vmem)


def _dot(a, b):
    return jnp.dot(a, b, preferred_element_type=F32)


def _dot_nt(a, b):
    return lax.dot_general(a, b, (((1,), (1,)), ((), ())), preferred_element_type=F32)


def _lane(shape):
    return lax.broadcasted_iota(jnp.int32, shape, len(shape) - 1)


def _mod_row(i):
    return jnp.where(i < NT_P, 0, 1 + (i - NT_P) // (S_S // TM))


def _pair_specs(n, sample_off):
    return [pl.BlockSpec((TM, n), lambda i: (jnp.minimum(i, NT_P - 1), 0)),
            pl.BlockSpec((TM, n), lambda i: (sample_off + jnp.maximum(i - NT_P, 0), 0))]


def _pick(i, p_ref, s_ref):
    return jnp.where(i < NT_P, p_ref[...], s_ref[...])


def _rope_blk(i):
    return jnp.where((i >= NT_P) & (i < NT), 1 + (i - NT_P) % (S_S // TM), 0)


def _ada_kernel(cv_ref, w_ref, b_ref, o_ref):
    s = cv_ref[...]
    s = s * jax.nn.sigmoid(s)
    o_ref[0] = _dot(s.astype(BF16), w_ref[0].astype(BF16)) + b_ref[0]


def _ada_call(cv, ada_w, ada_b):
    tn = 1536
    return pl.pallas_call(
        _ada_kernel,
        grid=(DEPTH, 6 * D // tn),
        in_specs=[pl.BlockSpec((8, D), lambda l, j: (0, 0)),
                  pl.BlockSpec((1, D, tn), lambda l, j: (l, 0, j)),
                  pl.BlockSpec((1, 1, tn), lambda l, j: (l, 0, j))],
        out_specs=pl.BlockSpec((1, 8, tn), lambda l, j: (l, 0, j)),
        out_shape=jax.ShapeDtypeStruct((DEPTH, 8, 6 * D), F32),
        compiler_params=_cparams(("arbitrary", "arbitrary")),
        name="ada_mod",
    )(cv, ada_w, ada_b.reshape(DEPTH, 1, 6 * D))


def _half_norm(x, g, n):
    outs = []
    for j in range(x.shape[1] // LANES):
        xj = x[:, j * LANES:(j + 1) * LANES]
        lo = _lane(xj.shape) < HD
        s = xj * xj
        s_lo = jnp.sum(jnp.where(lo, s, 0.0), axis=-1, keepdims=True)
        s_hi = jnp.sum(jnp.where(lo, 0.0, s), axis=-1, keepdims=True)
        r = lax.rsqrt(jnp.where(lo, s_lo, s_hi) * (1.0 / n) + EPS)
        outs.append(xj * r)
    return jnp.concatenate(outs, axis=-1) * g


def _group_norm(x, g, n):
    outs = []
    for j in range(x.shape[1] // LANES):
        xj = x[:, j * LANES:(j + 1) * LANES]
        r = lax.rsqrt(jnp.sum(xj * xj, axis=-1, keepdims=True) * (1.0 / n) + EPS)
        outs.append(xj * r)
    return jnp.concatenate(outs, axis=-1) * g


def _rope(x, c, sa, sb, shift):
    outs = []
    for j in range(x.shape[1] // LANES):
        xj = x[:, j * LANES:(j + 1) * LANES]
        outs.append(xj * c + pltpu.roll(xj, LANES - shift, 1) * sa + pltpu.roll(xj, shift, 1) * sb)
    return jnp.concatenate(outs, axis=-1)


def _in_kernel(xp_ref, xs_ref, sh_ref, sc_ref, n1_ref, w_ref, gqa_ref, gka_ref, gb_ref, gcq_ref, gckv_ref,
               wuq_ref, gqn_ref, ra_c, ra_a, ra_b, rc_c, rc_a, rc_b,
               qa_o, ka_o, va_o, kaf_o, vaf_o, bq_o, bk_o, bv_o, bo_o, li_o, lf_o, qc_o, ckv_o, ckr_o):
    i = pl.program_id(0)
    x = _pick(i, xp_ref, xs_ref)
    xn = x * lax.rsqrt(jnp.mean(x * x, axis=-1, keepdims=True) + EPS) * n1_ref[...]
    xn = xn * (1.0 + sc_ref[0]) + sh_ref[0]
    p = _dot(xn.astype(BF16), w_ref[0])

    def seg(name):
        o, n = SEG[name]
        return p[:, o:o + n]

    qa = _half_norm(seg("QA"), gqa_ref[...], HD)
    ka = _half_norm(seg("KA"), gka_ref[...], HD)
    va = seg("VA")

    @pl.when(i < NT_P)
    def _():
        kaf_o[...] = jnp.concatenate([ka[:, 0:HD], ka[:, LANES:LANES + HD]], axis=-1)
        vaf_o[...] = jnp.concatenate([va[:, 0:HD], va[:, LANES:LANES + HD]], axis=-1)

    va_o[...] = va.astype(BF16)

    bq_o[...] = seg("BQ").astype(BF16)
    bk_o[...] = (seg("BK") * (HD ** -0.5)).astype(BF16)
    bv_o[...] = seg("BV").astype(BF16)
    bo_o[...] = seg("BO")
    gb = gb_ref[...]
    li_o[...] = seg("GI") + gb[:, 0:256]
    lf_o[...] = jax.nn.log_sigmoid(seg("GF") + gb[:, 256:512])

    cq = seg("CQ")
    cqn = cq * lax.rsqrt(jnp.mean(cq * cq, axis=-1, keepdims=True) + EPS) * gcq_ref[...]
    qc = _group_norm(_dot(cqn.astype(BF16), wuq_ref[0]), gqn_ref[...], C_QK)

    @pl.when(i < NT_P)
    def _():
        qa_o[...] = qa.astype(BF16)
        ka_o[...] = ka.astype(BF16)
        qc_o[...] = qc.astype(BF16)

    @pl.when(i >= NT_P)
    def _():
        qa_o[...] = _rope(qa, ra_c[...], ra_a[...], ra_b[...], 32).astype(BF16)
        ka_o[...] = _rope(ka, ra_c[...], ra_a[...], ra_b[...], 32).astype(BF16)
        qc_o[...] = _rope(qc, rc_c[...], rc_a[...], rc_b[...], 16).astype(BF16)

    ckv = seg("CKV")
    ckv_o[...] = ckv * lax.rsqrt(jnp.mean(ckv * ckv, axis=-1, keepdims=True) + EPS) * gckv_ref[...]
    ckr_o[...] = seg("CKR")


def _in_call(x_p, x_s, sample_off, mod3, l, n1g, w_in_p, gqa, gka, gb, gcq, gckv, wuq_p, gqn, rope_a, rope_c):
    row = lambda k: pl.BlockSpec((1, 1, D), lambda i: ((l * 8 + _mod_row(i)) * 6 + k, 0, 0))
    vec = lambda n: pl.BlockSpec((1, n), lambda i: (0, 0))
    tab = pl.BlockSpec((TM, LANES), lambda i: (_rope_blk(i), 0))
    tok = lambda n: pl.BlockSpec((TM, n), lambda i: (i, 0))
    tokp = pl.BlockSpec((TM, LANES), lambda i: (jnp.minimum(i, NT_P - 1), 0))
    o = lambda n, dt: jax.ShapeDtypeStruct((T, n), dt)
    return pl.pallas_call(
        _in_kernel,
        grid=(NT,),
        in_specs=_pair_specs(D, sample_off) + [row(0), row(1), vec(D),
                  pl.BlockSpec((1, D, NP_IN), lambda i: (l, 0, 0)),
                  vec(384), vec(256), vec(512), vec(256), vec(128),
                  pl.BlockSpec((1, C_QRANK, 768), lambda i: (l, 0, 0)), vec(768),
                  tab, tab, tab, tab, tab, tab],
        out_specs=[tok(384), tok(256), tok(256), tokp, tokp, tok(256), tok(256), tok(256), tok(256),
                   tok(256), tok(256), tok(768), tok(128), tok(128)],
        out_shape=[o(384, BF16), o(256, BF16), o(256, BF16),
                   jax.ShapeDtypeStruct((T_P, LANES), F32), jax.ShapeDtypeStruct((T_P, LANES), F32),
                   o(256, BF16), o(256, BF16), o(256, BF16), o(256, F32),
                   o(256, F32), o(256, F32), o(768, BF16), o(128, F32), o(128, F32)],
        compiler_params=_cparams(("arbitrary",)),
        name=f"in_proj_l{l}",
    )(x_p, x_s, mod3, mod3, n1g, w_in_p, gqa, gka, gb, gcq, gckv, wuq_p, gqn, *rope_a, *rope_c)


def _mlakv_kernel(ckv_ref, ckvc_ref, ckr_ref, ckrc_ref, wk_ref, wv_ref, gkn_ref, rc_c, rc_a, rc_b, k_o, v_o):
    i = pl.program_id(0)
    tok = i < NT
    c = jnp.where(tok, ckv_ref[...], ckvc_ref[...]).astype(BF16)
    k = _dot(c, wk_ref[0])
    kr = jnp.where(tok, ckr_ref[...], ckrc_ref[...])
    k = k + jnp.concatenate([kr] * C_HEADS, axis=-1)
    k = _group_norm(k, gkn_ref[...], C_QK)
    sample = (i >= NT_P) & tok

    @pl.when(sample)
    def _():
        k_o[...] = _rope(k, rc_c[...], rc_a[...], rc_b[...], 16).astype(BF16)

    @pl.when(jnp.logical_not(sample))
    def _():
        k_o[...] = k.astype(BF16)

    v_o[...] = _dot(c, wv_ref[0]).astype(BF16)


def _mlakv_call(ckv_n, ckv_cache, ckr, ckr_cache, l, wk_p, wv_p, gkn, rope_c):
    r = T + NB_S * PAST
    tab = pl.BlockSpec((TM, LANES), lambda i: (_rope_blk(i), 0))
    tok = lambda n: pl.BlockSpec((TM, n), lambda i: (i, 0))
    tokens = pl.BlockSpec((TM, LANES), lambda i: (jnp.minimum(i, NT - 1), 0))
    cached = pl.BlockSpec((TM, LANES), lambda i: (jnp.maximum(i - NT, 0), 0))
    return pl.pallas_call(
        _mlakv_kernel,
        grid=(r // TM,),
        in_specs=[tokens, cached, tokens, cached,
                  pl.BlockSpec((1, C_KVRANK, 768), lambda i: (l, 0, 0)),
                  pl.BlockSpec((1, C_KVRANK, 384), lambda i: (l, 0, 0)),
                  pl.BlockSpec((1, 768), lambda i: (0, 0)), tab, tab, tab],
        out_specs=[tok(768), tok(384)],
        out_shape=[jax.ShapeDtypeStruct((r, 768), BF16), jax.ShapeDtypeStruct((r, 384), BF16)],
        compiler_params=_cparams(("arbitrary",)),
        name=f"mla_kv_l{l}",
    )(ckv_n, ckv_cache, ckr, ckr_cache, wk_p, wv_p, gkn, *rope_c)


def _bmm(a, b):
    return lax.dot_general(a, b, (((2,), (1,)), ((0,), (0,))), preferred_element_type=F32)


def _bmm_nt(a, b):
    return lax.dot_general(a, b, (((2,), (2,)), ((0,), (0,))), preferred_element_type=F32)


def _bmm_tn(a, b):
    return lax.dot_general(a, b, (((1,), (1,)), ((0,), (0,))), preferred_element_type=F32)


def _softmax_pv(scores, values, sink):
    batched = scores[0].ndim == 3
    heads = range(scores[0].shape[0]) if batched else ()
    m = scores[0].max(axis=-1, keepdims=True)
    for s in scores[1:]:
        m = jnp.maximum(m, s.max(axis=-1, keepdims=True))
    if sink is not None:
        m = jnp.stack([jnp.maximum(m[h], sink[h]) for h in heads]) if batched else jnp.maximum(m, sink)
    es = [jnp.exp(s - m) for s in scores]
    den = es[0].sum(axis=-1, keepdims=True)
    for e in es[1:]:
        den = den + e.sum(axis=-1, keepdims=True)
    if sink is not None:
        den = (jnp.stack([den[h] + jnp.exp(sink[h] - m[h]) for h in heads]) if batched
               else den + jnp.exp(sink - m))
    inv = 1.0 / den
    out = None
    for e, v in zip(es, values):
        o = (_bmm if batched else _dot)((e * inv).astype(BF16), v)
        out = o if out is None else out + o
    return out


def _scores(q3, k3, scale):
    return jnp.stack([_dot_nt(q3[h], k3[h]) for h in range(q3.shape[0])]) * scale


def _half(x, hi):
    lo = _lane(x.shape) < HD
    return jnp.where(lo != hi, x, jnp.zeros_like(x))


def _pair_sum_store(o3, o_ref):
    for j in range(o3.shape[0] // 2):
        o_ref[:, j * LANES:(j + 1) * LANES] = (o3[2 * j] + o3[2 * j + 1]).astype(BF16)


def _a_heads(q, kdups, vdups):
    q3 = jnp.stack([_half(q[:, (h // 2) * LANES:(h // 2 + 1) * LANES], h % 2 == 1) for h in range(A_HEADS)])
    k3 = [kdups[h // (A_HEADS // A_KV)] for h in range(A_HEADS)]
    v3 = jnp.stack([_half(vdups[h // (A_HEADS // A_KV)], h % 2 == 1) for h in range(A_HEADS)])
    return q3, k3, v3


def _sink3(sink_ref):
    return [sink_ref[h] for h in range(A_HEADS)]


def _attn_a_prompt_kernel(sink_ref, q_ref, k_ref, v_ref, o_ref):
    q = q_ref[...]
    k = k_ref[...]
    v = v_ref[...]
    for j in range(A_HEADS // 2):
        acc = None
        for c in range(2):
            h = 2 * j + c
            g = h // (A_HEADS // A_KV)
            qh = _half(q[:, j * LANES:(j + 1) * LANES], c == 1)
            s = _dot_nt(qh, k[:, g * LANES:(g + 1) * LANES]) * (HD ** -0.5)
            vh = _half(v[:, g * LANES:(g + 1) * LANES], c == 1)
            o = _softmax_pv([s], [vh], sink_ref[h])
            acc = o if acc is None else acc + o
        o_ref[:, j * LANES:(j + 1) * LANES] = acc.astype(BF16)


def _attn_a_prompt_call(sink, qa, ka, va):
    return pl.pallas_call(
        _attn_a_prompt_kernel,
        grid_spec=pltpu.PrefetchScalarGridSpec(
            num_scalar_prefetch=1, grid=(NB_P,),
            in_specs=[pl.BlockSpec((S_P, 384), lambda b, s: (b, 0)),
                      pl.BlockSpec((S_P, 256), lambda b, s: (b, 0)),
                      pl.BlockSpec((S_P, 256), lambda b, s: (b, 0))],
            out_specs=pl.BlockSpec((S_P, 384), lambda b, s: (b, 0))),
        out_shape=jax.ShapeDtypeStruct((T_P, 384), BF16),
        compiler_params=_cparams(("arbitrary",)),
        name="attn_a_prompt",
    )(sink, qa, ka, va)


QB = 128
WIN = 128
BAND = QB + 2 * WIN


def _attn_a_sample_kernel(sink_ref, q_ref, k_ref, v_ref, kc_ref, vc_ref, o_ref):
    n = pl.program_id(1)
    ws = pl.multiple_of(jnp.clip(n * QB - WIN, 0, S_S - BAND), QB)
    kb = k_ref[pl.ds(ws, BAND), :]
    vb = v_ref[pl.ds(ws, BAND), :]
    qpos = n * QB + lax.broadcasted_iota(jnp.int32, (QB, BAND), 0)
    kpos = ws + lax.broadcasted_iota(jnp.int32, (QB, BAND), 1)
    ok = (jnp.abs(qpos - kpos) <= WIN)[None]
    groups = range(A_KV)
    q3, k3, v3 = _a_heads(q_ref[...], [kb[:, g * LANES:(g + 1) * LANES] for g in groups],
                          [vb[:, g * LANES:(g + 1) * LANES] for g in groups])

    def dup(x, g):
        xg = _half(x, g == 1)
        return (xg + pltpu.roll(xg, HD, 1)).astype(BF16)

    kc = kc_ref[0, 0]
    vc = vc_ref[0, 0]
    _, kc3, vc3 = _a_heads(q_ref[...], [dup(kc, g) for g in groups], [dup(vc, g) for g in groups])
    s_b = jnp.where(ok, _scores(q3, k3, HD ** -0.5), NEG)
    s_c = _scores(q3, kc3, HD ** -0.5)
    _pair_sum_store(_softmax_pv([s_b, s_c], [v3, vc3], _sink3(sink_ref)), o_ref)


def _attn_a_sample_call(sink, qa, ka, va, cache_k, cache_v, l):
    nqb = S_S // QB
    off = T_P // S_S
    return pl.pallas_call(
        _attn_a_sample_kernel,
        grid_spec=pltpu.PrefetchScalarGridSpec(
            num_scalar_prefetch=1, grid=(NB_S, nqb),
            in_specs=[pl.BlockSpec((QB, 384), lambda b, n, s: (T_P // QB + b * nqb + n, 0)),
                      pl.BlockSpec((S_S, 256), lambda b, n, s: (off + b, 0)),
                      pl.BlockSpec((S_S, 256), lambda b, n, s: (off + b, 0)),
                      pl.BlockSpec((1, 1, PAST, LANES), lambda b, n, s: (b, l, 0, 0)),
                      pl.BlockSpec((1, 1, PAST, LANES), lambda b, n, s: (b, l, 0, 0))],
            out_specs=pl.BlockSpec((QB, 384), lambda b, n, s: (b * nqb + n, 0))),
        out_shape=jax.ShapeDtypeStruct((T_S, 384), BF16),
        compiler_params=_cparams(("arbitrary", "arbitrary")),
        name=f"attn_a_sample_l{l}",
    )(sink, qa, ka, va, cache_k, cache_v)


def _mla_heads(q, ks, vs, o_ref):
    heads = range(C_HEADS)
    q3 = jnp.stack([q[:, h * LANES:(h + 1) * LANES] for h in heads])
    scores = [_scores(q3, [k[:, h * LANES:(h + 1) * LANES] for h in heads], C_QK ** -0.5) for k in ks]
    vals = [jnp.stack([_half(v[:, (h // 2) * LANES:(h // 2 + 1) * LANES], h % 2 == 1) for h in heads])
            for v in vs]
    _pair_sum_store(_softmax_pv(scores, vals, None), o_ref)


def _mla_prompt_kernel(q_ref, k_ref, v_ref, o_ref):
    _mla_heads(q_ref[...], [k_ref[...]], [v_ref[...]], o_ref)


def _mla_prompt_call(qc, kc, vc):
    return pl.pallas_call(
        _mla_prompt_kernel,
        grid=(NB_P,),
        in_specs=[pl.BlockSpec((S_P, 768), lambda b: (b, 0)),
                  pl.BlockSpec((S_P, 768), lambda b: (b, 0)),
                  pl.BlockSpec((S_P, 384), lambda b: (b, 0))],
        out_specs=pl.BlockSpec((S_P, 384), lambda b: (b, 0)),
        out_shape=jax.ShapeDtypeStruct((T_P, 384), BF16),
        compiler_params=_cparams(("arbitrary",)),
        name="mla_prompt",
    )(qc, kc, vc)


def _mla_sample_kernel(q_ref, kc_ref, vc_ref, kl_ref, vl_ref, o_ref):
    _mla_heads(q_ref[...], [kc_ref[...], kl_ref[...]], [vc_ref[...], vl_ref[...]], o_ref)


def _mla_sample_call(qc, kc, vc):
    tq = 256
    nq = S_S // tq
    return pl.pallas_call(
        _mla_sample_kernel,
        grid=(NB_S, nq),
        in_specs=[pl.BlockSpec((tq, 768), lambda b, n: (T_P // tq + b * nq + n, 0)),
                  pl.BlockSpec((PAST, 768), lambda b, n: (T // PAST + b, 0)),
                  pl.BlockSpec((PAST, 384), lambda b, n: (T // PAST + b, 0)),
                  pl.BlockSpec((S_S, 768), lambda b, n: (T_P // S_S + b, 0)),
                  pl.BlockSpec((S_S, 384), lambda b, n: (T_P // S_S + b, 0))],
        out_specs=pl.BlockSpec((tq, 384), lambda b, n: (b * nq + n, 0)),
        out_shape=jax.ShapeDtypeStruct((T_S, 384), BF16),
        compiler_params=_cparams(("arbitrary", "arbitrary")),
        name="mla_sample",
    )(qc, kc, vc, kc, vc)


def _split3(x):
    x1 = x.astype(BF16)
    r = x - x1.astype(F32)
    x2 = r.astype(BF16)
    x3 = (r - x2.astype(F32)).astype(BF16)
    return x1, x2, x3


def _mlstm_kernel(nc, has_state, *refs):
    for j in range(B_HEADS // 2):
        _mlstm_pair(nc, has_state, j, *refs)


def _mlstm_pair(nc, has_state, j, *refs):
    if has_state:
        (q_ref, k_ref, v_ref, bo_ref, li_ref, lf_ref, hn_ref, c0_ref, n0_ref, m0_ref, ob_ref) = refs
    else:
        (q_ref, k_ref, v_ref, bo_ref, li_ref, lf_ref, hn_ref, ob_ref, cn_ref, nn_ref, mn_ref) = refs
    s_len = nc * CHUNK
    pair = slice(j * LANES, (j + 1) * LANES)
    c3 = lambda x: x.reshape(nc, CHUNK, x.shape[-1])
    li3 = c3(li_ref[:, pair])
    lf3 = c3(lf_ref[:, pair])
    row = lax.broadcasted_iota(jnp.int32, (CHUNK, CHUNK), 0)
    colv = lax.broadcasted_iota(jnp.int32, (CHUNK, CHUNK), 1)
    tri_f = colv <= row
    tri_b = colv >= row
    bcast = lambda m: jnp.broadcast_to(m[None], (nc, CHUNK, CHUNK))
    lane1 = _lane((1, LANES))
    fwd_lane = lane1 < 2

    lf_parts = _split3(lf3)
    tf3 = bcast(tri_f.astype(BF16))
    tb3 = bcast(tri_b.astype(BF16))
    bc_f = _bmm(tf3, lf_parts[0]) + _bmm(tf3, lf_parts[1]) + _bmm(tf3, lf_parts[2])
    bc_b = _bmm(tb3, lf_parts[0]) + _bmm(tb3, lf_parts[1]) + _bmm(tb3, lf_parts[2])
    bc3 = jnp.where(fwd_lane, bc_f, bc_b)
    row3 = lax.broadcasted_iota(jnp.int32, (nc, CHUNK, LANES), 1)
    edge = jnp.where(row3 == jnp.where(fwd_lane, CHUNK - 1, 0), bc3, 0.0)
    bl3 = jnp.sum(edge, axis=1, keepdims=True)
    gg3 = bl3 - bc3 + li3
    bl2 = jnp.sum(edge, axis=1)
    mg2 = gg3.max(axis=1)

    m0 = m0_ref[j] if has_state else jnp.zeros((1, LANES), F32)
    mf = m0
    mb = m0
    mf_prev, mf_next, mb_prev, mb_next = {}, {}, {}, {}
    for i in range(nc):
        mf_prev[i] = mf
        mf = jnp.maximum(bl2[i:i + 1] + mf, mg2[i:i + 1])
        mf_next[i] = mf
        cb = nc - 1 - i
        mb_prev[cb] = mb
        mb = jnp.maximum(bl2[cb:cb + 1] + mb, mg2[cb:cb + 1])
        mb_next[cb] = mb
    m_prev = [jnp.where(fwd_lane, mf_prev[c], mb_prev[c]) for c in range(nc)]
    m_next = [jnp.where(fwd_lane, mf_next[c], mb_next[c]) for c in range(nc)]
    m_prev3 = jnp.stack(m_prev)
    m_next3 = jnp.stack(m_next)
    dec2d = jnp.exp(bl2 + jnp.concatenate(m_prev, axis=0) - jnp.concatenate(m_next, axis=0))
    if not has_state:
        mn_ref[j] = jnp.where(fwd_lane, mf, mb)

    u_parts = _split3(li3 - bc3)
    ws3 = jnp.exp(gg3 - m_next3)
    inter3 = bc3 + m_prev3

    q3 = c3(q_ref[:, pair])
    k3 = c3(k_ref[:, pair])
    v3 = c3(v_ref[:, pair])
    q3f = q3.astype(F32)
    k3f = k3.astype(F32)
    lane3 = _lane((nc, CHUNK, LANES))
    lo3 = lane3 < HD
    rr = lax.broadcasted_iota(jnp.int32, (LANES, LANES), 0)
    cc = lax.broadcasted_iota(jnp.int32, (LANES, LANES), 1)
    blockdiag = (rr < HD) == (cc < HD)
    col = lambda x, k: x[:, :, k:k + 1]

    hsum = None
    for d in range(2):
        causal3 = bcast(tri_f if d == 0 else tri_b)
        a_c, ws_sum, mt_c, ws_c, intra = [], [], [], [], None
        for e in range(2):
            kk = 2 * d + e
            pick = jnp.broadcast_to(jnp.where(_lane((CHUNK, LANES)) == kk, 1.0, 0.0).astype(BF16)[None],
                                    (nc, CHUNK, LANES))
            ub = _bmm_nt(pick, u_parts[0]) + _bmm_nt(pick, u_parts[1]) + _bmm_nt(pick, u_parts[2])
            bc_col = col(bc3, kk)
            d_mat = jnp.where(causal3, bc_col + ub, NEG)
            inter = col(inter3, kk)
            mt = jnp.maximum(inter, d_mat.max(axis=-1, keepdims=True))
            a = jnp.exp(inter - mt)
            w = jnp.exp(d_mat - mt) * _bmm_nt(_half(q3, e == 1), k3)
            o = _bmm(w.astype(BF16), _half(v3, e == 1))
            intra = o if intra is None else intra + o
            a_c.append(a)
            mt_c.append(mt)
            ws_sum.append(w.sum(axis=-1, keepdims=True))
            ws_c.append(col(ws3, kk))
        kw3 = k3f * jnp.where(lo3, ws_c[0], ws_c[1])
        dec2 = jnp.where(_lane((nc, LANES)) < HD, dec2d[:, 2 * d:2 * d + 1], dec2d[:, 2 * d + 1:2 * d + 2])
        u_all = jnp.where(blockdiag, _bmm_tn(kw3.astype(BF16), v3), 0.0)
        kwsum = kw3.sum(axis=1)
        if has_state:
            cst = c0_ref[j, d]
            nst = n0_ref[j, d]
        else:
            cst = jnp.zeros((LANES, LANES), F32)
            nst = jnp.zeros((1, LANES), F32)
        cs, ns = [None] * nc, [None] * nc
        for c in (range(nc) if d == 0 else range(nc - 1, -1, -1)):
            cs[c] = cst
            ns[c] = nst
            cst = dec2[c:c + 1] * cst + u_all[c]
            nst = dec2[c:c + 1] * nst + kwsum[c:c + 1]
        if not has_state:
            cn_ref[j, d] = cst
            nn_ref[j, d] = nst
        qc = _bmm(q3, jnp.stack(cs).astype(BF16))
        qn_all = q3f * jnp.stack(ns)
        dn = []
        for e in range(2):
            qn = jnp.sum(jnp.where(lo3 != (e == 1), qn_all, 0.0), axis=-1, keepdims=True)
            den = a_c[e] * qn + ws_sum[e]
            dn.append(jnp.maximum(jnp.abs(den), jnp.exp(-mt_c[e])))
        h2 = (jnp.where(lo3, a_c[0], a_c[1]) * qc + intra) / jnp.where(lo3, dn[0], dn[1])
        hsum = h2 if hsum is None else hsum + h2

    hs = hsum.reshape(s_len, LANES)
    lo = _lane(hs.shape) < HD
    s = hs * hs
    s_lo = jnp.sum(jnp.where(lo, s, 0.0), axis=-1, keepdims=True)
    s_hi = jnp.sum(jnp.where(lo, 0.0, s), axis=-1, keepdims=True)
    r = lax.rsqrt(jnp.where(lo, s_lo, s_hi) * (1.0 / HD) + EPS)
    ob_ref[:, pair] = (hs * r * hn_ref[...] * jax.nn.sigmoid(bo_ref[:, pair])).astype(BF16)


def _mlstm_call(bq, bk, bv, bo, li, lf, hn2, state, l):
    has_state = state is not None
    if has_state:
        nb, s_len, base = NB_S, S_S, T_P // S_S
    else:
        nb, s_len, base = NB_P, S_P, 0
    nc = s_len // CHUNK
    tokp = pl.BlockSpec((s_len, 2 * LANES), lambda b: (base + b, 0))
    in_specs = [tokp, tokp, tokp, tokp, tokp, tokp, pl.BlockSpec((1, LANES), lambda b: (0, 0))]
    args = [bq, bk, bv, bo, li, lf, hn2]
    ob_spec = pl.BlockSpec((s_len, 2 * LANES), lambda b: (b, 0))
    ob_shape = jax.ShapeDtypeStruct((nb * s_len, 256), BF16)
    state_specs = [pl.BlockSpec((2, 2, LANES, LANES), lambda b: (b, 0, 0, 0)),
                   pl.BlockSpec((2, 2, 1, LANES), lambda b: (b, 0, 0, 0)),
                   pl.BlockSpec((2, 1, LANES), lambda b: (b, 0, 0))]
    if has_state:
        c0, n0, m0 = state
        in_specs += state_specs
        args += [c0, n0, m0]
        out_specs = ob_spec
        out_shape = ob_shape
    else:
        out_specs = [ob_spec] + state_specs
        out_shape = [ob_shape,
                     jax.ShapeDtypeStruct((nb * 2, 2, LANES, LANES), F32),
                     jax.ShapeDtypeStruct((nb * 2, 2, 1, LANES), F32),
                     jax.ShapeDtypeStruct((nb * 2, 1, LANES), F32)]
    return pl.pallas_call(
        functools.partial(_mlstm_kernel, nc, has_state),
        grid=(nb,),
        in_specs=in_specs,
        out_specs=out_specs,
        out_shape=out_shape,
        compiler_params=_cparams(("arbitrary",)),
        name=f"mlstm_{'sample' if has_state else 'prompt'}_l{l}",
    )(*args)


def _out_kernel(moe, xp_ref, xs_ref, oap_ref, oas_ref, obp_ref, obs_ref, ocp_ref, ocs_ref,
                w_ref, g1_ref, sh_ref, sc_ref, n2_ref, *rest):
    if moe:
        rh_ref, rl_ref, x1_o, xn_o, route_o, cnt_o, run_scr = rest
    else:
        x1_o, xn_o = rest
    i = pl.program_id(0)
    o = (_dot(_pick(i, oap_ref, oas_ref), w_ref[0, 0:384, :])
         + _dot(_pick(i, obp_ref, obs_ref), w_ref[0, 384:640, :])
         + _dot(_pick(i, ocp_ref, ocs_ref), w_ref[0, 640:1024, :]))
    x1 = _pick(i, xp_ref, xs_ref) + g1_ref[0] * o
    x1_o[...] = x1
    xn = x1 * lax.rsqrt(jnp.mean(x1 * x1, axis=-1, keepdims=True) + EPS) * n2_ref[...]
    xn = xn * (1.0 + sc_ref[0]) + sh_ref[0]
    xb = xn.astype(BF16)
    if not moe:
        xn_o[...] = xb
    else:
        xn_o[...] = xn
        xl = (xn - xb.astype(F32)).astype(BF16)
        logits = _dot(xb, rh_ref[0]) + (_dot(xl, rh_ref[0]) + _dot(xb, rl_ref[0]))
        lane = _lane(logits.shape)
        logits = jnp.where(lane < N_EXP, logits, -jnp.inf)
        m1 = logits.max(axis=-1, keepdims=True)
        i1 = jnp.min(jnp.where(logits == m1, lane, LANES), axis=-1, keepdims=True)
        rest_l = jnp.where(lane == i1, -jnp.inf, logits)
        m2 = rest_l.max(axis=-1, keepdims=True)
        i2 = jnp.min(jnp.where(rest_l == m2, lane, LANES), axis=-1, keepdims=True)
        e2 = jnp.exp(m2 - m1)
        den = 1.0 + e2
        w1 = 1.0 / den
        w2 = e2 / den

        @pl.when(pl.program_id(0) == 0)
        def _():
            run_scr[...] = jnp.zeros(run_scr.shape, F32)

        sel = jnp.where(lane == i1, 1.0, jnp.where((lane == i2) & (w2 > 0.0), 1.0, 0.0))
        rb = 256
        before = (lax.broadcasted_iota(jnp.int32, (rb, rb), 1)
                  < lax.broadcasted_iota(jnp.int32, (rb, rb), 0)).astype(BF16)
        run = run_scr[...]
        ranks = []
        for k in range(TM // rb):
            part = sel[k * rb:(k + 1) * rb]
            ranks.append(run + _dot(before, part.astype(BF16)))
            run = run + jnp.sum(part, axis=0, keepdims=True)
        rank = jnp.concatenate(ranks, axis=0)
        r1 = jnp.sum(jnp.where(lane == i1, rank, 0.0), axis=-1, keepdims=True)
        r2 = jnp.sum(jnp.where(lane == i2, rank, 0.0), axis=-1, keepdims=True)
        run_scr[...] = run
        cnt_o[...] = run_scr[...]
        fields = [i1.astype(F32), i2.astype(F32), r1, r2, w1, w2]
        info = jnp.zeros(lane.shape, F32)
        for k, v in enumerate(fields):
            info = jnp.where(lane == k, v, info)
        route_o[...] = info.T


def _out_call(x_p, x_s, sample_off, oa, ob, oc, w_out_b, mod3, l, n2g, router):
    moe = router is not None
    row = lambda k: pl.BlockSpec((1, 1, D), lambda i: ((l * 8 + _mod_row(i)) * 6 + k, 0, 0))
    tok = lambda n: pl.BlockSpec((TM, n), lambda i: (i, 0))
    in_specs = (_pair_specs(D, sample_off) + _pair_specs(384, 0) + _pair_specs(256, 0) + _pair_specs(384, 0)
                + [pl.BlockSpec((1, D, D), lambda i: (l, 0, 0)),
                   row(2), row(3), row(4), pl.BlockSpec((1, D), lambda i: (0, 0))])
    args = [x_p, x_s, *oa, *ob, *oc, w_out_b, mod3, mod3, mod3, n2g]
    out_specs = [tok(D), tok(D)]
    out_shape = [jax.ShapeDtypeStruct((T, D), F32), jax.ShapeDtypeStruct((T, D), BF16)]
    if moe:
        out_shape[1] = jax.ShapeDtypeStruct((T, D), F32)
        rh, rl = router
        in_specs += [pl.BlockSpec((1, D, LANES), lambda i: (0, 0, 0))] * 2
        args += [rh, rl]
        out_specs += [pl.BlockSpec((LANES, TM), lambda i: (0, i)), pl.BlockSpec((1, LANES), lambda i: (0, 0))]
        out_shape += [jax.ShapeDtypeStruct((LANES, T), F32), jax.ShapeDtypeStruct((1, LANES), F32)]
    return pl.pallas_call(
        functools.partial(_out_kernel, moe),
        grid=(NT,),
        in_specs=in_specs, out_specs=out_specs, out_shape=out_shape,
        scratch_shapes=[pltpu.VMEM((1, LANES), F32)] if moe else [],
        compiler_params=_cparams(("arbitrary",)),
        name=f"out_proj_l{l}",
    )(*args)


FFN_TM = 1024
FFN_TF = 256
FFN_TN = 256


def _hidden_tile(x, w1_ref, w3_ref):
    a = _dot(x, w1_ref[...].astype(BF16))
    b = _dot(x, w3_ref[...].astype(BF16))
    return (a * jax.nn.sigmoid(a) * b).astype(BF16)


def _down_tile(h_scr, w2n, m):
    out = None
    for f in range(h_scr.shape[0]):
        o = _dot(h_scr[f, 0:m, :], w2n[f])
        out = o if out is None else out + o
    return out


def _keep_weights(s, w1_ref, w3_ref, w2_ref, w1b, w3b, w2b, tn):
    w1b[s] = w1_ref[...].astype(BF16)
    w3b[s] = w3_ref[...].astype(BF16)
    w2 = w2_ref[...].astype(BF16)
    for n in range(w2b.shape[0]):
        w2b[n, s] = w2[:, n * tn:(n + 1) * tn]


def _ffn_kernel(nf, xn_ref, x1_ref, g2_ref, w1_ref, w3_ref, w2_ref, y_ref, h_scr, w1b, w3b, w2b):
    i = pl.program_id(0)
    s = pl.program_id(1)

    @pl.when((i == 0) & (s < nf))
    def _():
        _keep_weights(s, w1_ref.at[0], w3_ref.at[0], w2_ref.at[0], w1b, w3b, w2b, FFN_TN)

    @pl.when(s < nf)
    def _():
        h_scr[s] = _hidden_tile(xn_ref[...], w1b.at[s], w3b.at[s])

    @pl.when(s >= nf)
    def _():
        y_ref[...] = x1_ref[...] + g2_ref[0] * _down_tile(h_scr, w2b.at[s - nf], FFN_TM)


def _ffn_call(xn, x1, mod3, l, w1, w3, w2, i_layer):
    nf = D_FF // FFN_TF
    nn = D // FFN_TN
    tile = lambda i, s: jnp.where(i == 0, jnp.minimum(s, nf - 1), nf - 1)
    out = lambda s: jnp.maximum(s - nf, 0)

    def g2_idx(i, s):
        r = jnp.where(i < T_P // FFN_TM, 0, 1 + (i - T_P // FFN_TM) // (S_S // FFN_TM))
        return ((l * 8 + r) * 6 + 5, 0, out(s))

    return pl.pallas_call(
        functools.partial(_ffn_kernel, nf),
        grid=(T // FFN_TM, nf + nn),
        in_specs=[pl.BlockSpec((FFN_TM, D), lambda i, s: (i, 0)),
                  pl.BlockSpec((FFN_TM, FFN_TN), lambda i, s: (i, out(s))),
                  pl.BlockSpec((1, 1, FFN_TN), g2_idx),
                  pl.BlockSpec((1, D, FFN_TF), lambda i, s: (i_layer, 0, tile(i, s))),
                  pl.BlockSpec((1, D, FFN_TF), lambda i, s: (i_layer, 0, tile(i, s))),
                  pl.BlockSpec((1, FFN_TF, D), lambda i, s: (i_layer, tile(i, s), 0))],
        out_specs=pl.BlockSpec((FFN_TM, FFN_TN), lambda i, s: (i, out(s))),
        out_shape=jax.ShapeDtypeStruct((T, D), F32),
        scratch_shapes=[pltpu.VMEM((nf, FFN_TM, FFN_TF), BF16),
                        pltpu.VMEM((nf, D, FFN_TF), BF16), pltpu.VMEM((nf, D, FFN_TF), BF16),
                        pltpu.VMEM((nn, nf, FFN_TF, FFN_TN), BF16)],
        compiler_params=_cparams(("arbitrary", "arbitrary")),
        name="ffn_dense",
    )(xn, x1, mod3, w1, w3, w2)


MOE_BM = 1024
MOE_SUB = 256
MOE_NBLK = 2 * T // MOE_BM + N_EXP
MOE_NR = MOE_NBLK * MOE_BM
MOE_TF = 512
MOE_TN = 256
FIN_TM = 512


def _route_kernel(i1_ref, i2_ref, r1_ref, r2_ref, cnt_ref, w2_ref,
                  src_ref, eb_ref, rows_ref, nvb_ref, off_scr):
    def clear(r, carry):
        src_ref[r] = 0
        return carry

    lax.fori_loop(0, MOE_NR, clear, 0, unroll=8)

    def clear_blk(b, carry):
        eb_ref[b] = 0
        rows_ref[b] = 0
        return carry

    lax.fori_loop(0, MOE_NBLK, clear_blk, 0)

    nblk = jnp.int32(0)
    for e in range(N_EXP):
        c = cnt_ref[e]
        nbe = (c + (MOE_BM - 1)) // MOE_BM
        off_scr[e] = nblk * MOE_BM

        def fill(j, carry, e=e, c=c, nblk=nblk):
            eb_ref[nblk + j] = e
            rows_ref[nblk + j] = jnp.minimum(c - j * MOE_BM, MOE_BM)
            return carry

        lax.fori_loop(0, nbe, fill, 0)
        nblk = nblk + nbe
    nvb_ref[0] = nblk

    def place(t, carry):
        src_ref[off_scr[i1_ref[t]] + r1_ref[t]] = t

        @pl.when(w2_ref[t] > 0.0)
        def _():
            src_ref[off_scr[i2_ref[t]] + r2_ref[t]] = t

        return carry

    lax.fori_loop(0, T, place, 0, unroll=4)


def _route_call(route, cnt):
    ints = [route[k].astype(jnp.int32) for k in range(4)]
    cnt8 = cnt[0, :N_EXP].astype(jnp.int32)
    smem = pl.BlockSpec(memory_space=pltpu.SMEM)
    outs = pl.pallas_call(
        _route_kernel,
        grid_spec=pltpu.PrefetchScalarGridSpec(
            num_scalar_prefetch=5, grid=(1,),
            in_specs=[smem],
            out_specs=[smem, smem, smem, smem],
            scratch_shapes=[pltpu.SMEM((N_EXP,), jnp.int32)]),
        out_shape=[jax.ShapeDtypeStruct((MOE_NR,), jnp.int32),
                   jax.ShapeDtypeStruct((MOE_NBLK,), jnp.int32), jax.ShapeDtypeStruct((MOE_NBLK,), jnp.int32),
                   jax.ShapeDtypeStruct((1,), jnp.int32)],
        compiler_params=_cparams(("arbitrary",)),
        name="moe_route",
    )(*ints, cnt8, route[5])
    return tuple(outs) + (ints[0], route[4], route[5])


def _moe_gather_kernel(src_ref, rows_ref, x_ref, o_ref, rows_scr):
    k = pl.program_id(0)
    base = k * MOE_SUB
    used = rows_ref[k // (MOE_BM // MOE_SUB)] > (k % (MOE_BM // MOE_SUB)) * MOE_SUB

    @pl.when(used)
    def _():
        def body(i, carry):
            for j in range(8):
                r = i * 8 + j
                rows_scr[pl.ds(r, 1), :] = x_ref[pl.ds(src_ref[base + r], 1), :]
            return carry

        lax.fori_loop(0, MOE_SUB // 8, body, 0)
        o_ref[...] = rows_scr[...].astype(BF16)

    @pl.when(jnp.logical_not(used))
    def _():
        o_ref[...] = jnp.zeros(o_ref.shape, o_ref.dtype)


def _moe_gather_call(src, rows_b, xn):
    return pl.pallas_call(
        _moe_gather_kernel,
        grid_spec=pltpu.PrefetchScalarGridSpec(
            num_scalar_prefetch=2, grid=(MOE_NR // MOE_SUB,),
            in_specs=[pl.BlockSpec((T, D), lambda k, s, rw: (0, 0), pipeline_mode=pl.Buffered(1))],
            out_specs=pl.BlockSpec((MOE_SUB, D), lambda k, s, rw: (k, 0)),
            scratch_shapes=[pltpu.VMEM((MOE_SUB, D), F32)]),
        out_shape=jax.ShapeDtypeStruct((MOE_NR, D), BF16),
        compiler_params=_cparams(("arbitrary",)),
        name="moe_gather",
    )(src, rows_b, xn)


def _moe_first(b, eb_ref):
    return (b == 0) | (eb_ref[b] != eb_ref[jnp.maximum(b - 1, 0)])


def _moe_kernel(nf, eb_ref, rows_ref, nvb_ref, xs_ref, w1_ref, w3_ref, w2_ref, y_ref, h_scr, w1b, w3b, w2b):
    b = pl.program_id(0)
    s = pl.program_id(1)
    nsub = jnp.where(b < nvb_ref[0], (rows_ref[b] + MOE_SUB - 1) // MOE_SUB, 0)
    sizes = [k * MOE_SUB for k in range(1, MOE_BM // MOE_SUB + 1)]

    @pl.when((s < nf) & (nsub > 0) & _moe_first(b, eb_ref))
    def _():
        _keep_weights(s, w1_ref.at[0, 0], w3_ref.at[0, 0], w2_ref.at[0, 0], w1b, w3b, w2b, MOE_TN)

    @pl.when((s < nf) & (nsub > 0))
    def _():
        for k, m in enumerate(sizes, start=1):
            @pl.when(nsub == k)
            def _(m=m):
                h_scr[s, 0:m, :] = _hidden_tile(xs_ref[0:m, :], w1b.at[s], w3b.at[s])

    @pl.when((s >= nf) & (nsub > 0))
    def _():
        for k, m in enumerate(sizes, start=1):
            @pl.when(nsub == k)
            def _(m=m):
                y_ref[0:m, :] = _down_tile(h_scr, w2b.at[s - nf], m)
                if m < MOE_BM:
                    y_ref[m:MOE_BM, :] = jnp.zeros((MOE_BM - m, MOE_TN), F32)

    @pl.when((s >= nf) & (nsub == 0))
    def _():
        y_ref[...] = jnp.zeros(y_ref.shape, F32)


def _moe_call(xs, e_b, rows_b, nvb, w1, w3, w2, i_layer):
    nf = D_FFE // MOE_TF
    nn = D // MOE_TN

    def blk(b, nv):
        return jnp.minimum(b, nv[0] - 1)

    def tile(b, s, eb, nv):
        stream = (b < nv[0]) & _moe_first(b, eb)
        return jnp.where(stream, jnp.minimum(s, nf - 1), nf - 1)

    return pl.pallas_call(
        functools.partial(_moe_kernel, nf),
        grid_spec=pltpu.PrefetchScalarGridSpec(
            num_scalar_prefetch=3, grid=(MOE_NBLK, nf + nn),
            in_specs=[pl.BlockSpec((MOE_BM, D), lambda b, s, eb, rw, nv: (blk(b, nv), 0)),
                      pl.BlockSpec((1, 1, D, MOE_TF),
                                   lambda b, s, eb, rw, nv: (i_layer, eb[blk(b, nv)], 0, tile(b, s, eb, nv))),
                      pl.BlockSpec((1, 1, D, MOE_TF),
                                   lambda b, s, eb, rw, nv: (i_layer, eb[blk(b, nv)], 0, tile(b, s, eb, nv))),
                      pl.BlockSpec((1, 1, MOE_TF, D),
                                   lambda b, s, eb, rw, nv: (i_layer, eb[blk(b, nv)], tile(b, s, eb, nv), 0))],
            out_specs=pl.BlockSpec((MOE_BM, MOE_TN), lambda b, s, eb, rw, nv: (b, jnp.maximum(s - nf, 0))),
            scratch_shapes=[pltpu.VMEM((nf, MOE_BM, MOE_TF), BF16),
                            pltpu.VMEM((nf, D, MOE_TF), BF16), pltpu.VMEM((nf, D, MOE_TF), BF16),
                            pltpu.VMEM((nn, nf, MOE_TF, MOE_TN), BF16)]),
        out_shape=jax.ShapeDtypeStruct((MOE_NR, D), F32),
        compiler_params=_cparams(("arbitrary", "arbitrary"), vmem=56 * 1024 * 1024),
        name="moe_experts",
    )(e_b, rows_b, nvb, xs, w1, w3, w2)


def _moe_combine_kernel(src_ref, eb_ref, rows_ref, nvb_ref, i1_ref, w1_ref, w2_ref,
                        ys_ref, x1_ref, g2_ref, yp_ref, ys_out_ref, acc):
    i = pl.program_id(0)

    @pl.when(i == 0)
    def _():
        acc[...] = jnp.zeros(acc.shape, F32)

    @pl.when(i < nvb_ref[0])
    def _():
        base = i * MOE_BM
        n = rows_ref[i]
        e = eb_ref[i]

        def gate(t):
            return jnp.where(i1_ref[t] == e, w1_ref[t], w2_ref[t])

        def add_rows(r0, cnt):
            toks = [src_ref[base + r0 + j] for j in range(cnt)]
            vals = [acc[pl.ds(toks[j], 1), :] + gate(toks[j]) * ys_ref[pl.ds(r0 + j, 1), :]
                    for j in range(cnt)]
            for j in range(cnt):
                acc[pl.ds(toks[j], 1), :] = vals[j]

        def body4(q, carry):
            add_rows(q * 4, 4)
            return carry

        lax.fori_loop(0, n // 4, body4, 0)

        def body1(r, carry):
            add_rows(r, 1)
            return carry

        lax.fori_loop((n // 4) * 4, n, body1, 0)

    @pl.when(i >= MOE_NBLK)
    def _():
        t0 = pl.multiple_of((i - MOE_NBLK) * FIN_TM, FIN_TM)
        y = x1_ref[...] + g2_ref[0] * acc[pl.ds(t0, FIN_TM), :]

        @pl.when(i - MOE_NBLK < T_P // FIN_TM)
        def _():
            yp_ref[...] = y

        @pl.when(i - MOE_NBLK >= T_P // FIN_TM)
        def _():
            ys_out_ref[...] = y


def _moe_combine_call(src, e_b, rows_b, nvb, i1, w1, w2, ys, x1, mod3, l):
    nfin = T // FIN_TM
    smem = pl.BlockSpec(memory_space=pltpu.SMEM)

    def g2_idx(i, *_):
        j = jnp.maximum(i - MOE_NBLK, 0)
        r = jnp.where(j < T_P // FIN_TM, 0, 1 + (j - T_P // FIN_TM) // (S_S // FIN_TM))
        return ((l * 8 + r) * 6 + 5, 0, 0)

    return pl.pallas_call(
        _moe_combine_kernel,
        grid_spec=pltpu.PrefetchScalarGridSpec(
            num_scalar_prefetch=5, grid=(MOE_NBLK + nfin,),
            in_specs=[smem, smem,
                      pl.BlockSpec((MOE_BM, D),
                                   lambda i, s, eb, rw, nv, t1: (jnp.minimum(jnp.minimum(i, MOE_NBLK - 1), nv[0] - 1), 0)),
                      pl.BlockSpec((FIN_TM, D), lambda i, *_: (jnp.maximum(i - MOE_NBLK, 0), 0)),
                      pl.BlockSpec((1, 1, D), g2_idx)],
            out_specs=[pl.BlockSpec((FIN_TM, D),
                                    lambda i, *_: (jnp.clip(i - MOE_NBLK, 0, T_P // FIN_TM - 1), 0)),
                       pl.BlockSpec((FIN_TM, D),
                                    lambda i, *_: (jnp.maximum(i - MOE_NBLK - T_P // FIN_TM, 0), 0))],
            scratch_shapes=[pltpu.VMEM((T, D), F32)]),
        out_shape=[jax.ShapeDtypeStruct((T_P, D), F32), jax.ShapeDtypeStruct((T_S, D), F32)],
        compiler_params=_cparams(("arbitrary",), vmem=52 * 1024 * 1024),
        name="moe_combine",
    )(src, e_b, rows_b, nvb, i1, w1, w2, ys, x1, mod3)


def _moe_layer(xp, x1, route, cnt, mod3, l, w1, w3, w2, i_layer):
    src, e_b, rows_b, nvb, i1, g1, g2 = _route_call(route, cnt)
    xs = _moe_gather_call(src, rows_b, xp)
    ys = _moe_call(xs, e_b, rows_b, nvb, w1, w3, w2, i_layer)
    return _moe_combine_call(src, e_b, rows_b, nvb, i1, g1, g2, ys, x1, mod3, l)


def _pad_cols(w, n):
    return jnp.pad(w, ((0, 0), (0, n - w.shape[1])))


def _relayout_w_in(w):
    aq, ak, av, bq, bk, bv, bo, bg, cq, ckv, ckr = jnp.split(
        w, (384, 512, 640, 896, 1152, 1408, 1664, 1680, 1936, 2064), axis=1)
    dup = lambda m: jnp.concatenate([m[:, 0:HD], m[:, 0:HD], m[:, HD:2 * HD], m[:, HD:2 * HD]], axis=1)
    g = bg.reshape(D, 2, 2, 2, 2)

    def gate_cols(gi):
        cols = []
        for j in range(2):
            cols.append(_pad_cols(g[:, :, gi, j, :].reshape(D, 4), LANES))
        return jnp.concatenate(cols, axis=1)

    ckr_p = jnp.pad(ckr, ((0, 0), (C_NOPE, LANES - C_NOPE - C_ROPE)))
    out = jnp.concatenate([aq, dup(ak), dup(av), bq, bk, bv, bo, gate_cols(0), gate_cols(1),
                           cq, ckv, ckr_p], axis=1)
    return out.astype(BF16)


def _relayout_gate_b(b):
    g = b.reshape(2, 2, 2, 2)

    def cols(gi):
        return jnp.concatenate([jnp.pad(g[:, gi, j, :].reshape(4), (0, LANES - 4)) for j in range(2)])

    return jnp.concatenate([cols(0), cols(1)])[None, :]


def _pad_heads(w, width):
    r = w.shape[0]
    h = w.shape[1] // width
    return jnp.pad(w.reshape(r, h, width), ((0, 0), (0, 0), (0, LANES - width))).reshape(r, h * LANES)


def _rope_tables(half, span_start, period):
    rows = S_S // 64
    r = jnp.repeat(jnp.arange(rows), 64).astype(F32)
    c = jnp.tile(jnp.arange(64), rows).astype(F32)
    n_freq = half // 2
    freq = 10000.0 ** (-jnp.arange(n_freq, dtype=F32) / n_freq)
    ang = jnp.concatenate([r[:, None] * freq, c[:, None] * freq], axis=-1)
    cos, sin = jnp.cos(ang), jnp.sin(ang)
    d = (jnp.arange(LANES) - span_start) % period
    inside = d < 2 * half
    p = jnp.where(inside, d % half, 0)
    first = inside & (d < half)
    second = inside & (d >= half)
    ct = jnp.where(inside[None, :], cos[:, p], 1.0)
    sa = jnp.where(first[None, :], -sin[:, p], 0.0)
    sb = jnp.where(second[None, :], sin[:, p], 0.0)
    ident = (jnp.ones((TM, LANES), F32), jnp.zeros((TM, LANES), F32), jnp.zeros((TM, LANES), F32))
    return tuple(jnp.concatenate([i0, t], axis=0) for i0, t in zip(ident, (ct, sa, sb)))


def kernel(x_prompt, x_sample, c, cache_swa_k, cache_swa_v, cache_mla_ckv, cache_mla_krope, state_mlstm_C, state_mlstm_n, state_mlstm_m, c_ctx, ada_w, ada_b, norm1_g, norm2_g, w_in, a_qn_g, a_kn_g, a_sink, b_gate_b, b_hn_g, c_qa_g, c_kva_g, c_wuq, c_wukv, c_qn_g, c_kn_g, w_out, ffn_w1, ffn_w3, ffn_w2, moe_router, moe_w1, moe_w3, moe_w2):
    x_p, x_s, sample_off = x_prompt.reshape(T_P, D), x_sample.reshape(T_S, D), 0
    cv = jnp.concatenate([c_ctx[None, :], c, jnp.zeros((5, D), F32)], axis=0)
    mod = _ada_call(cv, ada_w, ada_b)
    mod3 = mod.reshape(DEPTH * 8 * 6, 1, D)

    rope_a = _rope_tables(32, 0, HD)
    rope_c = _rope_tables(16, C_NOPE, LANES)

    w_in_p = jnp.stack([_relayout_w_in(w_in[l]) for l in range(DEPTH)])
    wuq_p = jnp.stack([_pad_heads(c_wuq[l], C_QK) for l in range(DEPTH)]).astype(BF16)
    wukv = c_wukv.reshape(DEPTH, C_KVRANK, C_HEADS, C_NOPE + C_V)
    wk_p = jnp.pad(wukv[..., :C_NOPE], ((0, 0), (0, 0), (0, 0), (0, LANES - C_NOPE)))
    wk_p = wk_p.reshape(DEPTH, C_KVRANK, C_HEADS * LANES).astype(BF16)
    wv_p = wukv[..., C_NOPE:].reshape(DEPTH, C_KVRANK, C_HEADS * C_V).astype(BF16)
    w_out_b = w_out.astype(BF16)
    cache_k4 = cache_swa_k.reshape(NB_S, DEPTH, PAST, A_KV * HD)
    cache_v4 = cache_swa_v.reshape(NB_S, DEPTH, PAST, A_KV * HD)

    news = []
    for l in range(DEPTH):
        gqa = jnp.tile(a_qn_g[l], A_HEADS)[None, :]
        gka = jnp.tile(a_kn_g[l], 2 * A_KV)[None, :]
        gb = _relayout_gate_b(b_gate_b[l])
        gqn = jnp.tile(jnp.pad(c_qn_g[l], (0, LANES - C_QK)), C_HEADS)[None, :]
        gkn = jnp.tile(jnp.pad(c_kn_g[l], (0, LANES - C_QK)), C_HEADS)[None, :]
        (qa, ka, va, kaf, vaf, bq, bk, bv, bo, li, lf, qc, ckv_n, ckr) = _in_call(
            x_p, x_s, sample_off, mod3, l, norm1_g[l][None, :], w_in_p, gqa, gka, gb, c_qa_g[l][None, :],
            c_kva_g[l][None, :], wuq_p, gqn, rope_a, rope_c)

        ckr_cache = jnp.pad(cache_mla_krope[:, l].reshape(NB_S * PAST, C_ROPE),
                            ((0, 0), (C_NOPE, LANES - C_NOPE - C_ROPE)))
        kc, vc = _mlakv_call(ckv_n, cache_mla_ckv[:, l].reshape(NB_S * PAST, C_KVRANK), ckr, ckr_cache,
                             l, wk_p, wv_p, gkn, rope_c)

        sink = a_sink[l]
        oa_p = _attn_a_prompt_call(sink, qa, ka, va)
        oa_s = _attn_a_sample_call(sink, qa, ka, va, cache_k4, cache_v4, l)
        oc_p = _mla_prompt_call(qc, kc, vc)
        oc_s = _mla_sample_call(qc, kc, vc)

        hn2 = jnp.tile(b_hn_g[l], 2)[None, :]
        ob_p, cn, nn, mn = _mlstm_call(bq, bk, bv, bo, li, lf, hn2, None, l)
        c_st = state_mlstm_C[:, l].reshape(NB_S, 2, 2, 2, HD, HD)
        c_st = jnp.transpose(c_st, (0, 2, 1, 3, 4, 5))
        c0 = jnp.zeros((NB_S, 2, 2, LANES, LANES), F32)
        c0 = c0.at[..., :HD, :HD].set(c_st[:, :, :, 0]).at[..., HD:, HD:].set(c_st[:, :, :, 1])
        c0 = c0.reshape(NB_S * 2, 2, LANES, LANES)
        n_st = state_mlstm_n[:, l].reshape(NB_S, 2, 2, 1, LANES)
        n0 = jnp.transpose(n_st, (0, 2, 1, 3, 4)).reshape(NB_S * 2, 2, 1, LANES)
        m_st = state_mlstm_m[:, l].reshape(NB_S, 2, 2, 2)
        m0 = jnp.transpose(m_st, (0, 2, 1, 3)).reshape(NB_S * 2, 1, 4)
        m0 = jnp.pad(m0, ((0, 0), (0, 0), (0, LANES - 4)))
        ob_s = _mlstm_call(bq, bk, bv, bo, li, lf, hn2, (c0, n0, m0), l)

        oa, ob, oc = (oa_p, oa_s), (ob_p, ob_s), (oc_p, oc_s)

        if l % 2 == 0:
            x1, xn = _out_call(x_p, x_s, sample_off, oa, ob, oc, w_out_b, mod3, l, norm2_g[l][None, :], None)
            x = _ffn_call(xn, x1, mod3, l, ffn_w1, ffn_w3, ffn_w2, l // 2)
            x_p, x_s, sample_off = x, x, NT_P
        else:
            r = _pad_cols(moe_router[l // 2], LANES)
            rh = r.astype(BF16)
            rl = (r - rh.astype(F32)).astype(BF16)
            x1, xn, route, cnt = _out_call(x_p, x_s, sample_off, oa, ob, oc, w_out_b, mod3, l,
                                           norm2_g[l][None, :], (rh[None], rl[None]))
            x_p, x_s = _moe_layer(xn, x1, route, cnt, mod3, l, moe_w1, moe_w3, moe_w2, l // 2)
            sample_off = 0

        new_k = kaf.reshape(NB_P, S_P, A_KV, HD)
        new_v = vaf.reshape(NB_P, S_P, A_KV, HD)
        new_ckv = ckv_n[:T_P].reshape(NB_P, S_P, C_KVRANK)
        new_kr = ckr[:T_P, C_NOPE:C_NOPE + C_ROPE].reshape(NB_P, S_P, C_ROPE)
        cn6 = cn.reshape(NB_P, 2, 2, LANES, LANES)
        c_e = jnp.stack([cn6[..., :HD, :HD], cn6[..., HD:, HD:]], axis=3)
        new_c = jnp.transpose(c_e, (0, 2, 1, 3, 4, 5)).reshape(NB_P, 2, B_HEADS, HD, HD)
        nn5 = nn.reshape(NB_P, 2, 2, 2, HD)
        new_n = jnp.transpose(nn5, (0, 2, 1, 3, 4)).reshape(NB_P, 2, B_HEADS, HD)
        mn4 = mn.reshape(NB_P, 2, LANES)[:, :, :4].reshape(NB_P, 2, 2, 2)
        new_m = jnp.transpose(mn4, (0, 2, 1, 3)).reshape(NB_P, 2, B_HEADS)
        news.append((new_k, new_v, new_ckv, new_kr, new_c, new_n, new_m))

    y_prompt = x_p[:T_P].reshape(NB_P, S_P, D)
    y_sample = x_s[sample_off * TM:sample_off * TM + T_S].reshape(NB_S, S_S, D)
    stacked = tuple(jnp.stack([nw[j] for nw in news], axis=1) for j in range(7))
    return (y_prompt, y_sample) + stacked
```

```python
import functools

import jax
import jax.numpy as jnp
from jax import lax
from jax.experimental import pallas as pl
from jax.experimental.pallas import tpu as pltpu

F32 = jnp.float32
BF16 = jnp.bfloat16

D = 1024
NB_P, S_P = 16, 256
NB_S, S_S = 2, 1024
PAST = 512
DEPTH = 2
T_P = NB_P * S_P
T_S = NB_S * S_S
T = T_P + T_S
TM = 512
NT = T // TM
NT_P = T_P // TM
HD = 64
A_HEADS, A_KV = 6, 2
B_HEADS = 4
CHUNK = 64
C_HEADS = 6
C_QRANK, C_KVRANK, C_NOPE, C_ROPE, C_V = 256, 128, 64, 32, 64
C_QK = C_NOPE + C_ROPE
D_FF = 2816
N_EXP = 8
D_FFE = 3584
EPS = 1e-6
NEG = -1e30
LANES = 128
VMEM_LIMIT = 48 * 1024 * 1024

SEG = dict(QA=(0, 384), KA=(384, 256), VA=(640, 256), BQ=(896, 256), BK=(1152, 256),
           BV=(1408, 256), BO=(1664, 256), GI=(1920, 256), GF=(2176, 256), CQ=(2432, 256),
           CKV=(2688, 128), CKR=(2816, 128))
NP_IN = 2944


def _cparams(sem, vmem=VMEM_LIMIT):
    return pltpu.CompilerParams(dimension_semantics=sem, vmem_limit_bytes=vmem)


def _dot(a, b):
    return jnp.dot(a, b, preferred_element_type=F32)


def _dot_nt(a, b):
    return lax.dot_general(a, b, (((1,), (1,)), ((), ())), preferred_element_type=F32)


def _lane(shape):
    return lax.broadcasted_iota(jnp.int32, shape, len(shape) - 1)


def _mod_row(i):
    return jnp.where(i < NT_P, 0, 1 + (i - NT_P) // (S_S // TM))


def _pair_specs(n, sample_off):
    return [pl.BlockSpec((TM, n), lambda i: (jnp.minimum(i, NT_P - 1), 0)),
            pl.BlockSpec((TM, n), lambda i: (sample_off + jnp.maximum(i - NT_P, 0), 0))]


def _pick(i, p_ref, s_ref):
    return jnp.where(i < NT_P, p_ref[...], s_ref[...])


def _rope_blk(i):
    return jnp.where((i >= NT_P) & (i < NT), 1 + (i - NT_P) % (S_S // TM), 0)


def _ada_kernel(cv_ref, w_ref, b_ref, o_ref):
    s = cv_ref[...]
    s = s * jax.nn.sigmoid(s)
    o_ref[0] = _dot(s.astype(BF16), w_ref[0].astype(BF16)) + b_ref[0]


def _ada_call(cv, ada_w, ada_b):
    tn = 1536
    return pl.pallas_call(
        _ada_kernel,
        grid=(DEPTH, 6 * D // tn),
        in_specs=[pl.BlockSpec((8, D), lambda l, j: (0, 0)),
                  pl.BlockSpec((1, D, tn), lambda l, j: (l, 0, j)),
                  pl.BlockSpec((1, 1, tn), lambda l, j: (l, 0, j))],
        out_specs=pl.BlockSpec((1, 8, tn), lambda l, j: (l, 0, j)),
        out_shape=jax.ShapeDtypeStruct((DEPTH, 8, 6 * D), F32),
        compiler_params=_cparams(("arbitrary", "arbitrary")),
        name="ada_mod",
    )(cv, ada_w, ada_b.reshape(DEPTH, 1, 6 * D))


def _half_norm(x, g, n):
    outs = []
    for j in range(x.shape[1] // LANES):
        xj = x[:, j * LANES:(j + 1) * LANES]
        lo = _lane(xj.shape) < HD
        s = xj * xj
        s_lo = jnp.sum(jnp.where(lo, s, 0.0), axis=-1, keepdims=True)
        s_hi = jnp.sum(jnp.where(lo, 0.0, s), axis=-1, keepdims=True)
        r = lax.rsqrt(jnp.where(lo, s_lo, s_hi) * (1.0 / n) + EPS)
        outs.append(xj * r)
    return jnp.concatenate(outs, axis=-1) * g


def _group_norm(x, g, n):
    outs = []
    for j in range(x.shape[1] // LANES):
        xj = x[:, j * LANES:(j + 1) * LANES]
        r = lax.rsqrt(jnp.sum(xj * xj, axis=-1, keepdims=True) * (1.0 / n) + EPS)
        outs.append(xj * r)
    return jnp.concatenate(outs, axis=-1) * g


def _rope(x, c, sa, sb, shift):
    outs = []
    for j in range(x.shape[1] // LANES):
        xj = x[:, j * LANES:(j + 1) * LANES]
        outs.append(xj * c + pltpu.roll(xj, LANES - shift, 1) * sa + pltpu.roll(xj, shift, 1) * sb)
    return jnp.concatenate(outs, axis=-1)


def _in_kernel(xp_ref, xs_ref, sh_ref, sc_ref, n1_ref, w_ref, gqa_ref, gka_ref, gb_ref, gcq_ref, gckv_ref,
               wuq_ref, gqn_ref, ra_c, ra_a, ra_b, rc_c, rc_a, rc_b,
               qa_o, ka_o, va_o, kaf_o, vaf_o, bq_o, bk_o, bv_o, bo_o, li_o, lf_o, qc_o, ckv_o, ckr_o):
    i = pl.program_id(0)
    x = _pick(i, xp_ref, xs_ref)
    xn = x * lax.rsqrt(jnp.mean(x * x, axis=-1, keepdims=True) + EPS) * n1_ref[...]
    xn = xn * (1.0 + sc_ref[0]) + sh_ref[0]
    p = _dot(xn.astype(BF16), w_ref[0])

    def seg(name):
        o, n = SEG[name]
        return p[:, o:o + n]

    qa = _half_norm(seg("QA"), gqa_ref[...], HD)
    ka = _half_norm(seg("KA"), gka_ref[...], HD)
    va = seg("VA")

    @pl.when(i < NT_P)
    def _():
        kaf_o[...] = jnp.concatenate([ka[:, 0:HD], ka[:, LANES:LANES + HD]], axis=-1)
        vaf_o[...] = jnp.concatenate([va[:, 0:HD], va[:, LANES:LANES + HD]], axis=-1)

    va_o[...] = va.astype(BF16)

    bq_o[...] = seg("BQ").astype(BF16)
    bk_o[...] = (seg("BK") * (HD ** -0.5)).astype(BF16)
    bv_o[...] = seg("BV").astype(BF16)
    bo_o[...] = seg("BO")
    gb = gb_ref[...]
    li_o[...] = seg("GI") + gb[:, 0:256]
    lf_o[...] = jax.nn.log_sigmoid(seg("GF") + gb[:, 256:512])

    cq = seg("CQ")
    cqn = cq * lax.rsqrt(jnp.mean(cq * cq, axis=-1, keepdims=True) + EPS) * gcq_ref[...]
    qc = _group_norm(_dot(cqn.astype(BF16), wuq_ref[0]), gqn_ref[...], C_QK)

    @pl.when(i < NT_P)
    def _():
        qa_o[...] = qa.astype(BF16)
        ka_o[...] = ka.astype(BF16)
        qc_o[...] = qc.astype(BF16)

    @pl.when(i >= NT_P)
    def _():
        qa_o[...] = _rope(qa, ra_c[...], ra_a[...], ra_b[...], 32).astype(BF16)
        ka_o[...] = _rope(ka, ra_c[...], ra_a[...], ra_b[...], 32).astype(BF16)
        qc_o[...] = _rope(qc, rc_c[...], rc_a[...], rc_b[...], 16).astype(BF16)

    ckv = seg("CKV")
    ckv_o[...] = ckv * lax.rsqrt(jnp.mean(ckv * ckv, axis=-1, keepdims=True) + EPS) * gckv_ref[...]
    ckr_o[...] = seg("CKR")


def _in_call(x_p, x_s, sample_off, mod3, l, n1g, w_in_p, gqa, gka, gb, gcq, gckv, wuq_p, gqn, rope_a, rope_c):
    row = lambda k: pl.BlockSpec((1, 1, D), lambda i: ((l * 8 + _mod_row(i)) * 6 + k, 0, 0))
    vec = lambda n: pl.BlockSpec((1, n), lambda i: (0, 0))
    tab = pl.BlockSpec((TM, LANES), lambda i: (_rope_blk(i), 0))
    tok = lambda n: pl.BlockSpec((TM, n), lambda i: (i, 0))
    tokp = pl.BlockSpec((TM, LANES), lambda i: (jnp.minimum(i, NT_P - 1), 0))
    o = lambda n, dt: jax.ShapeDtypeStruct((T, n), dt)
    return pl.pallas_call(
        _in_kernel,
        grid=(NT,),
        in_specs=_pair_specs(D, sample_off) + [row(0), row(1), vec(D),
                  pl.BlockSpec((1, D, NP_IN), lambda i: (0, 0, 0)),
                  vec(384), vec(256), vec(512), vec(256), vec(128),
                  pl.BlockSpec((1, C_QRANK, 768), lambda i: (0, 0, 0)), vec(768),
                  tab, tab, tab, tab, tab, tab],
        out_specs=[tok(384), tok(256), tok(256), tokp, tokp, tok(256), tok(256), tok(256), tok(256),
                   tok(256), tok(256), tok(768), tok(128), tok(128)],
        out_shape=[o(384, BF16), o(256, BF16), o(256, BF16),
                   jax.ShapeDtypeStruct((T_P, LANES), F32), jax.ShapeDtypeStruct((T_P, LANES), F32),
                   o(256, BF16), o(256, BF16), o(256, BF16), o(256, F32),
                   o(256, F32), o(256, F32), o(768, BF16), o(128, F32), o(128, F32)],
        compiler_params=_cparams(("arbitrary",)),
        name=f"in_proj_l{l}",
    )(x_p, x_s, mod3, mod3, n1g, w_in_p, gqa, gka, gb, gcq, gckv, wuq_p, gqn, *rope_a, *rope_c)


def _mlakv_kernel(ckv_ref, ckvc_ref, ckr_ref, ckrc_ref, wk_ref, wv_ref, gkn_ref, rc_c, rc_a, rc_b, k_o, v_o):
    i = pl.program_id(0)
    tok = i < NT
    c = jnp.where(tok, ckv_ref[...], ckvc_ref[...]).astype(BF16)
    k = _dot(c, wk_ref[0])
    kr = jnp.where(tok, ckr_ref[...], ckrc_ref[...])
    k = k + jnp.concatenate([kr] * C_HEADS, axis=-1)
    k = _group_norm(k, gkn_ref[...], C_QK)
    sample = (i >= NT_P) & tok

    @pl.when(sample)
    def _():
        k_o[...] = _rope(k, rc_c[...], rc_a[...], rc_b[...], 16).astype(BF16)

    @pl.when(jnp.logical_not(sample))
    def _():
        k_o[...] = k.astype(BF16)

    v_o[...] = _dot(c, wv_ref[0]).astype(BF16)


def _mlakv_call(ckv_n, ckv_cache, ckr, ckr_cache, l, wk_p, wv_p, gkn, rope_c):
    r = T + NB_S * PAST
    tab = pl.BlockSpec((TM, LANES), lambda i: (_rope_blk(i), 0))
    tok = lambda n: pl.BlockSpec((TM, n), lambda i: (i, 0))
    tokens = pl.BlockSpec((TM, LANES), lambda i: (jnp.minimum(i, NT - 1), 0))
    cached = pl.BlockSpec((TM, LANES), lambda i: (jnp.maximum(i - NT, 0), 0))
    return pl.pallas_call(
        _mlakv_kernel,
        grid=(r // TM,),
        in_specs=[tokens, cached, tokens, cached,
                  pl.BlockSpec((1, C_KVRANK, 768), lambda i: (0, 0, 0)),
                  pl.BlockSpec((1, C_KVRANK, 384), lambda i: (0, 0, 0)),
                  pl.BlockSpec((1, 768), lambda i: (0, 0)), tab, tab, tab],
        out_specs=[tok(768), tok(384)],
        out_shape=[jax.ShapeDtypeStruct((r, 768), BF16), jax.ShapeDtypeStruct((r, 384), BF16)],
        compiler_params=_cparams(("arbitrary",)),
        name=f"mla_kv_l{l}",
    )(ckv_n, ckv_cache, ckr, ckr_cache, wk_p, wv_p, gkn, *rope_c)


def _bmm(a, b):
    return lax.dot_general(a, b, (((2,), (1,)), ((0,), (0,))), preferred_element_type=F32)


def _bmm_nt(a, b):
    return lax.dot_general(a, b, (((2,), (2,)), ((0,), (0,))), preferred_element_type=F32)


def _bmm_tn(a, b):
    return lax.dot_general(a, b, (((1,), (1,)), ((0,), (0,))), preferred_element_type=F32)


def _softmax_pv(scores, values, sink):
    batched = scores[0].ndim == 3
    heads = range(scores[0].shape[0]) if batched else ()
    m = scores[0].max(axis=-1, keepdims=True)
    for s in scores[1:]:
        m = jnp.maximum(m, s.max(axis=-1, keepdims=True))
    if sink is not None:
        m = jnp.stack([jnp.maximum(m[h], sink[h]) for h in heads]) if batched else jnp.maximum(m, sink)
    es = [jnp.exp(s - m) for s in scores]
    den = es[0].sum(axis=-1, keepdims=True)
    for e in es[1:]:
        den = den + e.sum(axis=-1, keepdims=True)
    if sink is not None:
        den = (jnp.stack([den[h] + jnp.exp(sink[h] - m[h]) for h in heads]) if batched
               else den + jnp.exp(sink - m))
    inv = 1.0 / den
    out = None
    for e, v in zip(es, values):
        o = (_bmm if batched else _dot)((e * inv).astype(BF16), v)
        out = o if out is None else out + o
    return out


def _scores(q3, k3, scale):
    return jnp.stack([_dot_nt(q3[h], k3[h]) for h in range(q3.shape[0])]) * scale


def _half(x, hi):
    lo = _lane(x.shape) < HD
    return jnp.where(lo != hi, x, jnp.zeros_like(x))


def _pair_sum_store(o3, o_ref):
    for j in range(o3.shape[0] // 2):
        o_ref[:, j * LANES:(j + 1) * LANES] = (o3[2 * j] + o3[2 * j + 1]).astype(BF16)


def _a_heads(q, kdups, vdups):
    q3 = jnp.stack([_half(q[:, (h // 2) * LANES:(h // 2 + 1) * LANES], h % 2 == 1) for h in range(A_HEADS)])
    k3 = [kdups[h // (A_HEADS // A_KV)] for h in range(A_HEADS)]
    v3 = jnp.stack([_half(vdups[h // (A_HEADS // A_KV)], h % 2 == 1) for h in range(A_HEADS)])
    return q3, k3, v3


def _sink3(sink_ref):
    return [sink_ref[h] for h in range(A_HEADS)]


def _attn_a_prompt_kernel(sink_ref, q_ref, k_ref, v_ref, o_ref):
    q = q_ref[...]
    k = k_ref[...]
    v = v_ref[...]
    for j in range(A_HEADS // 2):
        acc = None
        for c in range(2):
            h = 2 * j + c
            g = h // (A_HEADS // A_KV)
            qh = _half(q[:, j * LANES:(j + 1) * LANES], c == 1)
            s = _dot_nt(qh, k[:, g * LANES:(g + 1) * LANES]) * (HD ** -0.5)
            vh = _half(v[:, g * LANES:(g + 1) * LANES], c == 1)
            o = _softmax_pv([s], [vh], sink_ref[h])
            acc = o if acc is None else acc + o
        o_ref[:, j * LANES:(j + 1) * LANES] = acc.astype(BF16)


def _attn_a_prompt_call(sink, qa, ka, va):
    return pl.pallas_call(
        _attn_a_prompt_kernel,
        grid_spec=pltpu.PrefetchScalarGridSpec(
            num_scalar_prefetch=1, grid=(NB_P,),
            in_specs=[pl.BlockSpec((S_P, 384), lambda b, s: (b, 0)),
                      pl.BlockSpec((S_P, 256), lambda b, s: (b, 0)),
                      pl.BlockSpec((S_P, 256), lambda b, s: (b, 0))],
            out_specs=pl.BlockSpec((S_P, 384), lambda b, s: (b, 0))),
        out_shape=jax.ShapeDtypeStruct((T_P, 384), BF16),
        compiler_params=_cparams(("arbitrary",)),
        name="attn_a_prompt",
    )(sink, qa, ka, va)


QB = 128
WIN = 128
BAND = QB + 2 * WIN


def _attn_a_sample_kernel(sink_ref, q_ref, k_ref, v_ref, kc_ref, vc_ref, o_ref):
    n = pl.program_id(1)
    ws = pl.multiple_of(jnp.clip(n * QB - WIN, 0, S_S - BAND), QB)
    kb = k_ref[pl.ds(ws, BAND), :]
    vb = v_ref[pl.ds(ws, BAND), :]
    qpos = n * QB + lax.broadcasted_iota(jnp.int32, (QB, BAND), 0)
    kpos = ws + lax.broadcasted_iota(jnp.int32, (QB, BAND), 1)
    ok = (jnp.abs(qpos - kpos) <= WIN)[None]
    groups = range(A_KV)
    q3, k3, v3 = _a_heads(q_ref[...], [kb[:, g * LANES:(g + 1) * LANES] for g in groups],
                          [vb[:, g * LANES:(g + 1) * LANES] for g in groups])

    def dup(x, g):
        xg = _half(x, g == 1)
        return (xg + pltpu.roll(xg, HD, 1)).astype(BF16)

    kc = kc_ref[0, 0]
    vc = vc_ref[0, 0]
    _, kc3, vc3 = _a_heads(q_ref[...], [dup(kc, g) for g in groups], [dup(vc, g) for g in groups])
    s_b = jnp.where(ok, _scores(q3, k3, HD ** -0.5), NEG)
    s_c = _scores(q3, kc3, HD ** -0.5)
    _pair_sum_store(_softmax_pv([s_b, s_c], [v3, vc3], _sink3(sink_ref)), o_ref)


def _attn_a_sample_call(sink, qa, ka, va, cache_k, cache_v, l):
    nqb = S_S // QB
    off = T_P // S_S
    return pl.pallas_call(
        _attn_a_sample_kernel,
        grid_spec=pltpu.PrefetchScalarGridSpec(
            num_scalar_prefetch=1, grid=(NB_S, nqb),
            in_specs=[pl.BlockSpec((QB, 384), lambda b, n, s: (T_P // QB + b * nqb + n, 0)),
                      pl.BlockSpec((S_S, 256), lambda b, n, s: (off + b, 0)),
                      pl.BlockSpec((S_S, 256), lambda b, n, s: (off + b, 0)),
                      pl.BlockSpec((1, 1, PAST, LANES), lambda b, n, s: (b, l, 0, 0)),
                      pl.BlockSpec((1, 1, PAST, LANES), lambda b, n, s: (b, l, 0, 0))],
            out_specs=pl.BlockSpec((QB, 384), lambda b, n, s: (b * nqb + n, 0))),
        out_shape=jax.ShapeDtypeStruct((T_S, 384), BF16),
        compiler_params=_cparams(("arbitrary", "arbitrary")),
        name=f"attn_a_sample_l{l}",
    )(sink, qa, ka, va, cache_k, cache_v)


def _mla_heads(q, ks, vs, o_ref):
    heads = range(C_HEADS)
    q3 = jnp.stack([q[:, h * LANES:(h + 1) * LANES] for h in heads])
    scores = [_scores(q3, [k[:, h * LANES:(h + 1) * LANES] for h in heads], C_QK ** -0.5) for k in ks]
    vals = [jnp.stack([_half(v[:, (h // 2) * LANES:(h // 2 + 1) * LANES], h % 2 == 1) for h in heads])
            for v in vs]
    _pair_sum_store(_softmax_pv(scores, vals, None), o_ref)


def _mla_prompt_kernel(q_ref, k_ref, v_ref, o_ref):
    _mla_heads(q_ref[...], [k_ref[...]], [v_ref[...]], o_ref)


def _mla_prompt_call(qc, kc, vc):
    return pl.pallas_call(
        _mla_prompt_kernel,
        grid=(NB_P,),
        in_specs=[pl.BlockSpec((S_P, 768), lambda b: (b, 0)),
                  pl.BlockSpec((S_P, 768), lambda b: (b, 0)),
                  pl.BlockSpec((S_P, 384), lambda b: (b, 0))],
        out_specs=pl.BlockSpec((S_P, 384), lambda b: (b, 0)),
        out_shape=jax.ShapeDtypeStruct((T_P, 384), BF16),
        compiler_params=_cparams(("arbitrary",)),
        name="mla_prompt",
    )(qc, kc, vc)


def _mla_sample_kernel(q_ref, kc_ref, vc_ref, kl_ref, vl_ref, o_ref):
    _mla_heads(q_ref[...], [kc_ref[...], kl_ref[...]], [vc_ref[...], vl_ref[...]], o_ref)


def _mla_sample_call(qc, kc, vc):
    tq = 256
    nq = S_S // tq
    return pl.pallas_call(
        _mla_sample_kernel,
        grid=(NB_S, nq),
        in_specs=[pl.BlockSpec((tq, 768), lambda b, n: (T_P // tq + b * nq + n, 0)),
                  pl.BlockSpec((PAST, 768), lambda b, n: (T // PAST + b, 0)),
                  pl.BlockSpec((PAST, 384), lambda b, n: (T // PAST + b, 0)),
                  pl.BlockSpec((S_S, 768), lambda b, n: (T_P // S_S + b, 0)),
                  pl.BlockSpec((S_S, 384), lambda b, n: (T_P // S_S + b, 0))],
        out_specs=pl.BlockSpec((tq, 384), lambda b, n: (b * nq + n, 0)),
        out_shape=jax.ShapeDtypeStruct((T_S, 384), BF16),
        compiler_params=_cparams(("arbitrary", "arbitrary")),
        name="mla_sample",
    )(qc, kc, vc, kc, vc)


def _split3(x):
    x1 = x.astype(BF16)
    r = x - x1.astype(F32)
    x2 = r.astype(BF16)
    x3 = (r - x2.astype(F32)).astype(BF16)
    return x1, x2, x3


def _mlstm_kernel(nc, has_state, *refs):
    for j in range(B_HEADS // 2):
        _mlstm_pair(nc, has_state, j, *refs)


def _mlstm_pair(nc, has_state, j, *refs):
    if has_state:
        (q_ref, k_ref, v_ref, bo_ref, li_ref, lf_ref, hn_ref, c0_ref, n0_ref, m0_ref, ob_ref) = refs
    else:
        (q_ref, k_ref, v_ref, bo_ref, li_ref, lf_ref, hn_ref, ob_ref, cn_ref, nn_ref, mn_ref) = refs
    s_len = nc * CHUNK
    pair = slice(j * LANES, (j + 1) * LANES)
    c3 = lambda x: x.reshape(nc, CHUNK, x.shape[-1])
    li3 = c3(li_ref[:, pair])
    lf3 = c3(lf_ref[:, pair])
    row = lax.broadcasted_iota(jnp.int32, (CHUNK, CHUNK), 0)
    colv = lax.broadcasted_iota(jnp.int32, (CHUNK, CHUNK), 1)
    tri_f = colv <= row
    tri_b = colv >= row
    bcast = lambda m: jnp.broadcast_to(m[None], (nc, CHUNK, CHUNK))
    lane1 = _lane((1, LANES))
    fwd_lane = lane1 < 2

    lf_parts = _split3(lf3)
    tf3 = bcast(tri_f.astype(BF16))
    tb3 = bcast(tri_b.astype(BF16))
    bc_f = _bmm(tf3, lf_parts[0]) + _bmm(tf3, lf_parts[1]) + _bmm(tf3, lf_parts[2])
    bc_b = _bmm(tb3, lf_parts[0]) + _bmm(tb3, lf_parts[1]) + _bmm(tb3, lf_parts[2])
    bc3 = jnp.where(fwd_lane, bc_f, bc_b)
    row3 = lax.broadcasted_iota(jnp.int32, (nc, CHUNK, LANES), 1)
    edge = jnp.where(row3 == jnp.where(fwd_lane, CHUNK - 1, 0), bc3, 0.0)
    bl3 = jnp.sum(edge, axis=1, keepdims=True)
    gg3 = bl3 - bc3 + li3
    bl2 = jnp.sum(edge, axis=1)
    mg2 = gg3.max(axis=1)

    m0 = m0_ref[j] if has_state else jnp.zeros((1, LANES), F32)
    mf = m0
    mb = m0
    mf_prev, mf_next, mb_prev, mb_next = {}, {}, {}, {}
    for i in range(nc):
        mf_prev[i] = mf
        mf = jnp.maximum(bl2[i:i + 1] + mf, mg2[i:i + 1])
        mf_next[i] = mf
        cb = nc - 1 - i
        mb_prev[cb] = mb
        mb = jnp.maximum(bl2[cb:cb + 1] + mb, mg2[cb:cb + 1])
        mb_next[cb] = mb
    m_prev = [jnp.where(fwd_lane, mf_prev[c], mb_prev[c]) for c in range(nc)]
    m_next = [jnp.where(fwd_lane, mf_next[c], mb_next[c]) for c in range(nc)]
    m_prev3 = jnp.stack(m_prev)
    m_next3 = jnp.stack(m_next)
    dec2d = jnp.exp(bl2 + jnp.concatenate(m_prev, axis=0) - jnp.concatenate(m_next, axis=0))
    if not has_state:
        mn_ref[j] = jnp.where(fwd_lane, mf, mb)

    u_parts = _split3(li3 - bc3)
    ws3 = jnp.exp(gg3 - m_next3)
    inter3 = bc3 + m_prev3

    q3 = c3(q_ref[:, pair])
    k3 = c3(k_ref[:, pair])
    v3 = c3(v_ref[:, pair])
    q3f = q3.astype(F32)
    k3f = k3.astype(F32)
    lane3 = _lane((nc, CHUNK, LANES))
    lo3 = lane3 < HD
    rr = lax.broadcasted_iota(jnp.int32, (LANES, LANES), 0)
    cc = lax.broadcasted_iota(jnp.int32, (LANES, LANES), 1)
    blockdiag = (rr < HD) == (cc < HD)
    col = lambda x, k: x[:, :, k:k + 1]

    hsum = None
    for d in range(2):
        causal3 = bcast(tri_f if d == 0 else tri_b)
        a_c, ws_sum, mt_c, ws_c, intra = [], [], [], [], None
        for e in range(2):
            kk = 2 * d + e
            pick = jnp.broadcast_to(jnp.where(_lane((CHUNK, LANES)) == kk, 1.0, 0.0).astype(BF16)[None],
                                    (nc, CHUNK, LANES))
            ub = _bmm_nt(pick, u_parts[0]) + _bmm_nt(pick, u_parts[1]) + _bmm_nt(pick, u_parts[2])
            bc_col = col(bc3, kk)
            d_mat = jnp.where(causal3, bc_col + ub, NEG)
            inter = col(inter3, kk)
            mt = jnp.maximum(inter, d_mat.max(axis=-1, keepdims=True))
            a = jnp.exp(inter - mt)
            w = jnp.exp(d_mat - mt) * _bmm_nt(_half(q3, e == 1), k3)
            o = _bmm(w.astype(BF16), _half(v3, e == 1))
            intra = o if intra is None else intra + o
            a_c.append(a)
            mt_c.append(mt)
            ws_sum.append(w.sum(axis=-1, keepdims=True))
            ws_c.append(col(ws3, kk))
        kw3 = k3f * jnp.where(lo3, ws_c[0], ws_c[1])
        dec2 = jnp.where(_lane((nc, LANES)) < HD, dec2d[:, 2 * d:2 * d + 1], dec2d[:, 2 * d + 1:2 * d + 2])
        u_all = jnp.where(blockdiag, _bmm_tn(kw3.astype(BF16), v3), 0.0)
        kwsum = kw3.sum(axis=1)
        if has_state:
            cst = c0_ref[j, d]
            nst = n0_ref[j, d]
        else:
            cst = jnp.zeros((LANES, LANES), F32)
            nst = jnp.zeros((1, LANES), F32)
        cs, ns = [None] * nc, [None] * nc
        for c in (range(nc) if d == 0 else range(nc - 1, -1, -1)):
            cs[c] = cst
            ns[c] = nst
            cst = dec2[c:c + 1] * cst + u_all[c]
            nst = dec2[c:c + 1] * nst + kwsum[c:c + 1]
        if not has_state:
            cn_ref[j, d] = cst
            nn_ref[j, d] = nst
        qc = _bmm(q3, jnp.stack(cs).astype(BF16))
        qn_all = q3f * jnp.stack(ns)
        dn = []
        for e in range(2):
            qn = jnp.sum(jnp.where(lo3 != (e == 1), qn_all, 0.0), axis=-1, keepdims=True)
            den = a_c[e] * qn + ws_sum[e]
            dn.append(jnp.maximum(jnp.abs(den), jnp.exp(-mt_c[e])))
        h2 = (jnp.where(lo3, a_c[0], a_c[1]) * qc + intra) / jnp.where(lo3, dn[0], dn[1])
        hsum = h2 if hsum is None else hsum + h2

    hs = hsum.reshape(s_len, LANES)
    lo = _lane(hs.shape) < HD
    s = hs * hs
    s_lo = jnp.sum(jnp.where(lo, s, 0.0), axis=-1, keepdims=True)
    s_hi = jnp.sum(jnp.where(lo, 0.0, s), axis=-1, keepdims=True)
    r = lax.rsqrt(jnp.where(lo, s_lo, s_hi) * (1.0 / HD) + EPS)
    ob_ref[:, pair] = (hs * r * hn_ref[...] * jax.nn.sigmoid(bo_ref[:, pair])).astype(BF16)


def _mlstm_call(bq, bk, bv, bo, li, lf, hn2, state, l):
    has_state = state is not None
    if has_state:
        nb, s_len, base = NB_S, S_S, T_P // S_S
    else:
        nb, s_len, base = NB_P, S_P, 0
    nc = s_len // CHUNK
    tokp = pl.BlockSpec((s_len, 2 * LANES), lambda b: (base + b, 0))
    in_specs = [tokp, tokp, tokp, tokp, tokp, tokp, pl.BlockSpec((1, LANES), lambda b: (0, 0))]
    args = [bq, bk, bv, bo, li, lf, hn2]
    ob_spec = pl.BlockSpec((s_len, 2 * LANES), lambda b: (b, 0))
    ob_shape = jax.ShapeDtypeStruct((nb * s_len, 256), BF16)
    state_specs = [pl.BlockSpec((2, 2, LANES, LANES), lambda b: (b, 0, 0, 0)),
                   pl.BlockSpec((2, 2, 1, LANES), lambda b: (b, 0, 0, 0)),
                   pl.BlockSpec((2, 1, LANES), lambda b: (b, 0, 0))]
    if has_state:
        c0, n0, m0 = state
        in_specs += state_specs
        args += [c0, n0, m0]
        out_specs = ob_spec
        out_shape = ob_shape
    else:
        out_specs = [ob_spec] + state_specs
        out_shape = [ob_shape,
                     jax.ShapeDtypeStruct((nb * 2, 2, LANES, LANES), F32),
                     jax.ShapeDtypeStruct((nb * 2, 2, 1, LANES), F32),
                     jax.ShapeDtypeStruct((nb * 2, 1, LANES), F32)]
    return pl.pallas_call(
        functools.partial(_mlstm_kernel, nc, has_state),
        grid=(nb,),
        in_specs=in_specs,
        out_specs=out_specs,
        out_shape=out_shape,
        compiler_params=_cparams(("arbitrary",)),
        name=f"mlstm_{'sample' if has_state else 'prompt'}_l{l}",
    )(*args)


def _out_kernel(moe, xp_ref, xs_ref, oap_ref, oas_ref, obp_ref, obs_ref, ocp_ref, ocs_ref,
                w_ref, g1_ref, sh_ref, sc_ref, n2_ref, *rest):
    if moe:
        rh_ref, rl_ref, x1_o, xn_o, route_o, cnt_o, run_scr = rest
    else:
        x1_o, xn_o = rest
    i = pl.program_id(0)
    o = (_dot(_pick(i, oap_ref, oas_ref), w_ref[0, 0:384, :])
         + _dot(_pick(i, obp_ref, obs_ref), w_ref[0, 384:640, :])
         + _dot(_pick(i, ocp_ref, ocs_ref), w_ref[0, 640:1024, :]))
    x1 = _pick(i, xp_ref, xs_ref) + g1_ref[0] * o
    x1_o[...] = x1
    xn = x1 * lax.rsqrt(jnp.mean(x1 * x1, axis=-1, keepdims=True) + EPS) * n2_ref[...]
    xn = xn * (1.0 + sc_ref[0]) + sh_ref[0]
    xb = xn.astype(BF16)
    if not moe:
        xn_o[...] = xb
    else:
        xn_o[...] = xn
        xl = (xn - xb.astype(F32)).astype(BF16)
        logits = _dot(xb, rh_ref[0]) + (_dot(xl, rh_ref[0]) + _dot(xb, rl_ref[0]))
        lane = _lane(logits.shape)
        logits = jnp.where(lane < N_EXP, logits, -jnp.inf)
        m1 = logits.max(axis=-1, keepdims=True)
        i1 = jnp.min(jnp.where(logits == m1, lane, LANES), axis=-1, keepdims=True)
        rest_l = jnp.where(lane == i1, -jnp.inf, logits)
        m2 = rest_l.max(axis=-1, keepdims=True)
        i2 = jnp.min(jnp.where(rest_l == m2, lane, LANES), axis=-1, keepdims=True)
        e2 = jnp.exp(m2 - m1)
        den = 1.0 + e2
        w1 = 1.0 / den
        w2 = e2 / den

        @pl.when(pl.program_id(0) == 0)
        def _():
            run_scr[...] = jnp.zeros(run_scr.shape, F32)

        sel = jnp.where(lane == i1, 1.0, jnp.where((lane == i2) & (w2 > 0.0), 1.0, 0.0))
        rb = 256
        before = (lax.broadcasted_iota(jnp.int32, (rb, rb), 1)
                  < lax.broadcasted_iota(jnp.int32, (rb, rb), 0)).astype(BF16)
        run = run_scr[...]
        ranks = []
        for k in range(TM // rb):
            part = sel[k * rb:(k + 1) * rb]
            ranks.append(run + _dot(before, part.astype(BF16)))
            run = run + jnp.sum(part, axis=0, keepdims=True)
        rank = jnp.concatenate(ranks, axis=0)
        r1 = jnp.sum(jnp.where(lane == i1, rank, 0.0), axis=-1, keepdims=True)
        r2 = jnp.sum(jnp.where(lane == i2, rank, 0.0), axis=-1, keepdims=True)
        run_scr[...] = run
        cnt_o[...] = run_scr[...]
        fields = [i1.astype(F32), i2.astype(F32), r1, r2, w1, w2]
        info = jnp.zeros(lane.shape, F32)
        for k, v in enumerate(fields):
            info = jnp.where(lane == k, v, info)
        route_o[...] = info.T


def _out_call(x_p, x_s, sample_off, oa, ob, oc, w_out_b, mod3, l, n2g, router):
    moe = router is not None
    row = lambda k: pl.BlockSpec((1, 1, D), lambda i: ((l * 8 + _mod_row(i)) * 6 + k, 0, 0))
    tok = lambda n: pl.BlockSpec((TM, n), lambda i: (i, 0))
    in_specs = (_pair_specs(D, sample_off) + _pair_specs(384, 0) + _pair_specs(256, 0) + _pair_specs(384, 0)
                + [pl.BlockSpec((1, D, D), lambda i: (0, 0, 0)),
                   row(2), row(3), row(4), pl.BlockSpec((1, D), lambda i: (0, 0))])
    args = [x_p, x_s, *oa, *ob, *oc, w_out_b, mod3, mod3, mod3, n2g]
    out_specs = [tok(D), tok(D)]
    out_shape = [jax.ShapeDtypeStruct((T, D), F32), jax.ShapeDtypeStruct((T, D), BF16)]
    if moe:
        out_shape[1] = jax.ShapeDtypeStruct((T, D), F32)
        rh, rl = router
        in_specs += [pl.BlockSpec((1, D, LANES), lambda i: (0, 0, 0))] * 2
        args += [rh, rl]
        out_specs += [pl.BlockSpec((LANES, TM), lambda i: (0, i)), pl.BlockSpec((1, LANES), lambda i: (0, 0))]
        out_shape += [jax.ShapeDtypeStruct((LANES, T), F32), jax.ShapeDtypeStruct((1, LANES), F32)]
    return pl.pallas_call(
        functools.partial(_out_kernel, moe),
        grid=(NT,),
        in_specs=in_specs, out_specs=out_specs, out_shape=out_shape,
        scratch_shapes=[pltpu.VMEM((1, LANES), F32)] if moe else [],
        compiler_params=_cparams(("arbitrary",)),
        name=f"out_proj_l{l}",
    )(*args)


FFN_TM = 1024
FFN_TF = 256
FFN_TN = 256


def _hidden_tile(x, w1_ref, w3_ref):
    a = _dot(x, w1_ref[...].astype(BF16))
    b = _dot(x, w3_ref[...].astype(BF16))
    return (a * jax.nn.sigmoid(a) * b).astype(BF16)


def _down_tile(h_scr, w2n, m):
    out = None
    for f in range(h_scr.shape[0]):
        o = _dot(h_scr[f, 0:m, :], w2n[f])
        out = o if out is None else out + o
    return out


def _keep_weights(s, w1_ref, w3_ref, w2_ref, w1b, w3b, w2b, tn):
    w1b[s] = w1_ref[...].astype(BF16)
    w3b[s] = w3_ref[...].astype(BF16)
    w2 = w2_ref[...].astype(BF16)
    for n in range(w2b.shape[0]):
        w2b[n, s] = w2[:, n * tn:(n + 1) * tn]


def _ffn_kernel(nf, xn_ref, x1_ref, g2_ref, w1_ref, w3_ref, w2_ref, y_ref, h_scr, w1b, w3b, w2b):
    i = pl.program_id(0)
    s = pl.program_id(1)

    @pl.when((i == 0) & (s < nf))
    def _():
        _keep_weights(s, w1_ref.at[0], w3_ref.at[0], w2_ref.at[0], w1b, w3b, w2b, FFN_TN)

    @pl.when(s < nf)
    def _():
        h_scr[s] = _hidden_tile(xn_ref[...], w1b.at[s], w3b.at[s])

    @pl.when(s >= nf)
    def _():
        y_ref[...] = x1_ref[...] + g2_ref[0] * _down_tile(h_scr, w2b.at[s - nf], FFN_TM)


def _ffn_call(xn, x1, mod3, l, w1, w3, w2, i_layer):
    nf = D_FF // FFN_TF
    nn = D // FFN_TN
    tile = lambda i, s: jnp.where(i == 0, jnp.minimum(s, nf - 1), nf - 1)
    out = lambda s: jnp.maximum(s - nf, 0)

    def g2_idx(i, s):
        r = jnp.where(i < T_P // FFN_TM, 0, 1 + (i - T_P // FFN_TM) // (S_S // FFN_TM))
        return ((l * 8 + r) * 6 + 5, 0, out(s))

    return pl.pallas_call(
        functools.partial(_ffn_kernel, nf),
        grid=(T // FFN_TM, nf + nn),
        in_specs=[pl.BlockSpec((FFN_TM, D), lambda i, s: (i, 0)),
                  pl.BlockSpec((FFN_TM, FFN_TN), lambda i, s: (i, out(s))),
                  pl.BlockSpec((1, 1, FFN_TN), g2_idx),
                  pl.BlockSpec((1, D, FFN_TF), lambda i, s: (i_layer, 0, tile(i, s))),
                  pl.BlockSpec((1, D, FFN_TF), lambda i, s: (i_layer, 0, tile(i, s))),
                  pl.BlockSpec((1, FFN_TF, D), lambda i, s: (i_layer, tile(i, s), 0))],
        out_specs=pl.BlockSpec((FFN_TM, FFN_TN), lambda i, s: (i, out(s))),
        out_shape=jax.ShapeDtypeStruct((T, D), F32),
        scratch_shapes=[pltpu.VMEM((nf, FFN_TM, FFN_TF), BF16),
                        pltpu.VMEM((nf, D, FFN_TF), BF16), pltpu.VMEM((nf, D, FFN_TF), BF16),
                        pltpu.VMEM((nn, nf, FFN_TF, FFN_TN), BF16)],
        compiler_params=_cparams(("arbitrary", "arbitrary")),
        name="ffn_dense",
    )(xn, x1, mod3, w1, w3, w2)


MOE_BM = 1024
MOE_SUB = 256
MOE_NBLK = 2 * T // MOE_BM + N_EXP
MOE_NR = MOE_NBLK * MOE_BM
MOE_TF = 512
MOE_TN = 256
FIN_TM = 512


def _route_kernel(i1_ref, i2_ref, r1_ref, r2_ref, cnt_ref, w2_ref,
                  src_ref, eb_ref, rows_ref, nvb_ref, off_scr):
    def clear(r, carry):
        src_ref[r] = 0
        return carry

    lax.fori_loop(0, MOE_NR + 8, clear, 0, unroll=8)

    def clear_blk(b, carry):
        eb_ref[b] = 0
        rows_ref[b] = 0
        return carry

    lax.fori_loop(0, MOE_NBLK, clear_blk, 0)

    nblk = jnp.int32(0)
    for e in range(N_EXP):
        c = cnt_ref[e]
        nbe = (c + (MOE_BM - 1)) // MOE_BM
        off_scr[e] = nblk * MOE_BM

        def fill(j, carry, e=e, c=c, nblk=nblk):
            eb_ref[nblk + j] = e
            rows_ref[nblk + j] = jnp.minimum(c - j * MOE_BM, MOE_BM)
            return carry

        lax.fori_loop(0, nbe, fill, 0)
        nblk = nblk + nbe
    nvb_ref[0] = nblk

    def place(t, carry):
        src_ref[off_scr[i1_ref[t]] + r1_ref[t]] = t
        src_ref[jnp.where(w2_ref[t] > 0.0, off_scr[i2_ref[t]] + r2_ref[t], MOE_NR)] = t
        return carry

    lax.fori_loop(0, T, place, 0, unroll=4)


def _route_call(route, cnt):
    ints = [route[k].astype(jnp.int32) for k in range(4)]
    cnt8 = cnt[0, :N_EXP].astype(jnp.int32)
    smem = pl.BlockSpec(memory_space=pltpu.SMEM)
    outs = pl.pallas_call(
        _route_kernel,
        grid_spec=pltpu.PrefetchScalarGridSpec(
            num_scalar_prefetch=5, grid=(1,),
            in_specs=[smem],
            out_specs=[smem, smem, smem, smem],
            scratch_shapes=[pltpu.SMEM((N_EXP,), jnp.int32)]),
        out_shape=[jax.ShapeDtypeStruct((MOE_NR + 8,), jnp.int32),
                   jax.ShapeDtypeStruct((MOE_NBLK,), jnp.int32), jax.ShapeDtypeStruct((MOE_NBLK,), jnp.int32),
                   jax.ShapeDtypeStruct((1,), jnp.int32)],
        compiler_params=_cparams(("arbitrary",)),
        name="moe_route",
    )(*ints, cnt8, route[5])
    return tuple(outs) + (ints[0], route[4], route[5])


def _moe_gather_kernel(src_ref, rows_ref, x_ref, o_ref, rows_scr):
    k = pl.program_id(0)
    base = k * MOE_SUB
    used = rows_ref[k // (MOE_BM // MOE_SUB)] > (k % (MOE_BM // MOE_SUB)) * MOE_SUB

    @pl.when(used)
    def _():
        def body(i, carry):
            for j in range(8):
                r = i * 8 + j
                rows_scr[pl.ds(r, 1), :] = x_ref[pl.ds(src_ref[base + r], 1), :]
            return carry

        lax.fori_loop(0, MOE_SUB // 8, body, 0)
        o_ref[...] = rows_scr[...].astype(BF16)

    @pl.when(jnp.logical_not(used))
    def _():
        o_ref[...] = jnp.zeros(o_ref.shape, o_ref.dtype)


def _moe_gather_call(src, rows_b, xn):
    return pl.pallas_call(
        _moe_gather_kernel,
        grid_spec=pltpu.PrefetchScalarGridSpec(
            num_scalar_prefetch=2, grid=(MOE_NR // MOE_SUB,),
            in_specs=[pl.BlockSpec((T, D), lambda k, s, rw: (0, 0), pipeline_mode=pl.Buffered(1))],
            out_specs=pl.BlockSpec((MOE_SUB, D), lambda k, s, rw: (k, 0)),
            scratch_shapes=[pltpu.VMEM((MOE_SUB, D), F32)]),
        out_shape=jax.ShapeDtypeStruct((MOE_NR, D), BF16),
        compiler_params=_cparams(("arbitrary",)),
        name="moe_gather",
    )(src, rows_b, xn)


def _moe_first(b, eb_ref):
    return (b == 0) | (eb_ref[b] != eb_ref[jnp.maximum(b - 1, 0)])


def _moe_kernel(nf, eb_ref, rows_ref, nvb_ref, xs_ref, w1_ref, w3_ref, w2_ref, y_ref, h_scr, w1b, w3b, w2b):
    b = pl.program_id(0)
    s = pl.program_id(1)
    nsub = jnp.where(b < nvb_ref[0], (rows_ref[b] + MOE_SUB - 1) // MOE_SUB, 0)
    sizes = [k * MOE_SUB for k in range(1, MOE_BM // MOE_SUB + 1)]

    @pl.when((s < nf) & (nsub > 0) & _moe_first(b, eb_ref))
    def _():
        _keep_weights(s, w1_ref.at[0, 0], w3_ref.at[0, 0], w2_ref.at[0, 0], w1b, w3b, w2b, MOE_TN)

    @pl.when((s < nf) & (nsub > 0))
    def _():
        for k, m in enumerate(sizes, start=1):
            @pl.when(nsub == k)
            def _(m=m):
                h_scr[s, 0:m, :] = _hidden_tile(xs_ref[0:m, :], w1b.at[s], w3b.at[s])

    @pl.when((s >= nf) & (nsub > 0))
    def _():
        for k, m in enumerate(sizes, start=1):
            @pl.when(nsub == k)
            def _(m=m):
                y_ref[0:m, :] = _down_tile(h_scr, w2b.at[s - nf], m)
                if m < MOE_BM:
                    y_ref[m:MOE_BM, :] = jnp.zeros((MOE_BM - m, MOE_TN), F32)

    @pl.when((s >= nf) & (nsub == 0))
    def _():
        y_ref[...] = jnp.zeros(y_ref.shape, F32)


def _moe_call(xs, e_b, rows_b, nvb, w1, w3, w2, i_layer):
    nf = D_FFE // MOE_TF
    nn = D // MOE_TN

    def blk(b, nv):
        return jnp.minimum(b, nv[0] - 1)

    def tile(b, s, eb, nv):
        stream = (b < nv[0]) & _moe_first(b, eb)
        return jnp.where(stream, jnp.minimum(s, nf - 1), nf - 1)

    return pl.pallas_call(
        functools.partial(_moe_kernel, nf),
        grid_spec=pltpu.PrefetchScalarGridSpec(
            num_scalar_prefetch=3, grid=(MOE_NBLK, nf + nn),
            in_specs=[pl.BlockSpec((MOE_BM, D), lambda b, s, eb, rw, nv: (blk(b, nv), 0)),
                      pl.BlockSpec((1, 1, D, MOE_TF),
                                   lambda b, s, eb, rw, nv: (i_layer, eb[blk(b, nv)], 0, tile(b, s, eb, nv))),
                      pl.BlockSpec((1, 1, D, MOE_TF),
                                   lambda b, s, eb, rw, nv: (i_layer, eb[blk(b, nv)], 0, tile(b, s, eb, nv))),
                      pl.BlockSpec((1, 1, MOE_TF, D),
                                   lambda b, s, eb, rw, nv: (i_layer, eb[blk(b, nv)], tile(b, s, eb, nv), 0))],
            out_specs=pl.BlockSpec((MOE_BM, MOE_TN), lambda b, s, eb, rw, nv: (b, jnp.maximum(s - nf, 0))),
            scratch_shapes=[pltpu.VMEM((nf, MOE_BM, MOE_TF), BF16),
                            pltpu.VMEM((nf, D, MOE_TF), BF16), pltpu.VMEM((nf, D, MOE_TF), BF16),
                            pltpu.VMEM((nn, nf, MOE_TF, MOE_TN), BF16)]),
        out_shape=jax.ShapeDtypeStruct((MOE_NR, D), F32),
        compiler_params=_cparams(("arbitrary", "arbitrary"), vmem=56 * 1024 * 1024),
        name="moe_experts",
    )(e_b, rows_b, nvb, xs, w1, w3, w2)


def _moe_combine_kernel(src_ref, eb_ref, rows_ref, nvb_ref, i1_ref, w1_ref, w2_ref,
                        ys_ref, x1_ref, g2_ref, yp_ref, ys_out_ref, acc):
    i = pl.program_id(0)

    @pl.when(i == 0)
    def _():
        acc[...] = jnp.zeros(acc.shape, F32)

    @pl.when(i < nvb_ref[0])
    def _():
        base = i * MOE_BM
        n = rows_ref[i]
        e = eb_ref[i]

        def gate(t):
            return jnp.where(i1_ref[t] == e, w1_ref[t], w2_ref[t])

        def add_rows(r0, cnt):
            toks = [src_ref[base + r0 + j] for j in range(cnt)]
            vals = [acc[pl.ds(toks[j], 1), :] + gate(toks[j]) * ys_ref[pl.ds(r0 + j, 1), :]
                    for j in range(cnt)]
            for j in range(cnt):
                acc[pl.ds(toks[j], 1), :] = vals[j]

        def body4(q, carry):
            add_rows(q * 4, 4)
            return carry

        lax.fori_loop(0, n // 4, body4, 0)

        def body1(r, carry):
            add_rows(r, 1)
            return carry

        lax.fori_loop((n // 4) * 4, n, body1, 0)

    @pl.when(i >= MOE_NBLK)
    def _():
        t0 = pl.multiple_of((i - MOE_NBLK) * FIN_TM, FIN_TM)
        y = x1_ref[...] + g2_ref[0] * acc[pl.ds(t0, FIN_TM), :]

        @pl.when(i - MOE_NBLK < T_P // FIN_TM)
        def _():
            yp_ref[...] = y

        @pl.when(i - MOE_NBLK >= T_P // FIN_TM)
        def _():
            ys_out_ref[...] = y


def _moe_combine_call(src, e_b, rows_b, nvb, i1, w1, w2, ys, x1, mod3, l):
    nfin = T // FIN_TM
    smem = pl.BlockSpec(memory_space=pltpu.SMEM)

    def g2_idx(i, *_):
        j = jnp.maximum(i - MOE_NBLK, 0)
        r = jnp.where(j < T_P // FIN_TM, 0, 1 + (j - T_P // FIN_TM) // (S_S // FIN_TM))
        return ((l * 8 + r) * 6 + 5, 0, 0)

    return pl.pallas_call(
        _moe_combine_kernel,
        grid_spec=pltpu.PrefetchScalarGridSpec(
            num_scalar_prefetch=5, grid=(MOE_NBLK + nfin,),
            in_specs=[smem, smem,
                      pl.BlockSpec((MOE_BM, D),
                                   lambda i, s, eb, rw, nv, t1: (jnp.minimum(jnp.minimum(i, MOE_NBLK - 1), nv[0] - 1), 0)),
                      pl.BlockSpec((FIN_TM, D), lambda i, *_: (jnp.maximum(i - MOE_NBLK, 0), 0)),
                      pl.BlockSpec((1, 1, D), g2_idx)],
            out_specs=[pl.BlockSpec((FIN_TM, D),
                                    lambda i, *_: (jnp.clip(i - MOE_NBLK, 0, T_P // FIN_TM - 1), 0)),
                       pl.BlockSpec((FIN_TM, D),
                                    lambda i, *_: (jnp.maximum(i - MOE_NBLK - T_P // FIN_TM, 0), 0))],
            scratch_shapes=[pltpu.VMEM((T, D), F32)]),
        out_shape=[jax.ShapeDtypeStruct((T_P, D), F32), jax.ShapeDtypeStruct((T_S, D), F32)],
        compiler_params=_cparams(("arbitrary",), vmem=52 * 1024 * 1024),
        name="moe_combine",
    )(src, e_b, rows_b, nvb, i1, w1, w2, ys, x1, mod3)


def _moe_layer(xp, x1, route, cnt, mod3, l, w1, w3, w2, i_layer):
    src, e_b, rows_b, nvb, i1, g1, g2 = _route_call(route, cnt)
    xs = _moe_gather_call(src, rows_b, xp)
    ys = _moe_call(xs, e_b, rows_b, nvb, w1, w3, w2, i_layer)
    return _moe_combine_call(src, e_b, rows_b, nvb, i1, g1, g2, ys, x1, mod3, l)


def _pad_cols(w, n):
    return jnp.pad(w, ((0, 0), (0, n - w.shape[1])))


def _relayout_w_in(w):
    aq, ak, av, bq, bk, bv, bo, bg, cq, ckv, ckr = jnp.split(
        w, (384, 512, 640, 896, 1152, 1408, 1664, 1680, 1936, 2064), axis=1)
    dup = lambda m: jnp.concatenate([m[:, 0:HD], m[:, 0:HD], m[:, HD:2 * HD], m[:, HD:2 * HD]], axis=1)
    g = bg.reshape(D, 2, 2, 2, 2)

    def gate_cols(gi):
        cols = []
        for j in range(2):
            cols.append(_pad_cols(g[:, :, gi, j, :].reshape(D, 4), LANES))
        return jnp.concatenate(cols, axis=1)

    ckr_p = jnp.pad(ckr, ((0, 0), (C_NOPE, LANES - C_NOPE - C_ROPE)))
    out = jnp.concatenate([aq, dup(ak), dup(av), bq, bk, bv, bo, gate_cols(0), gate_cols(1),
                           cq, ckv, ckr_p], axis=1)
    return out.astype(BF16)


def _relayout_gate_b(b):
    g = b.reshape(2, 2, 2, 2)

    def cols(gi):
        return jnp.concatenate([jnp.pad(g[:, gi, j, :].reshape(4), (0, LANES - 4)) for j in range(2)])

    return jnp.concatenate([cols(0), cols(1)])[None, :]


def _pad_heads(w, width):
    r = w.shape[0]
    h = w.shape[1] // width
    return jnp.pad(w.reshape(r, h, width), ((0, 0), (0, 0), (0, LANES - width))).reshape(r, h * LANES)


def _rope_tables(half, span_start, period):
    rows = S_S // 64
    r = jnp.repeat(jnp.arange(rows), 64).astype(F32)
    c = jnp.tile(jnp.arange(64), rows).astype(F32)
    n_freq = half // 2
    freq = 10000.0 ** (-jnp.arange(n_freq, dtype=F32) / n_freq)
    ang = jnp.concatenate([r[:, None] * freq, c[:, None] * freq], axis=-1)
    cos, sin = jnp.cos(ang), jnp.sin(ang)
    d = (jnp.arange(LANES) - span_start) % period
    inside = d < 2 * half
    p = jnp.where(inside, d % half, 0)
    first = inside & (d < half)
    second = inside & (d >= half)
    ct = jnp.where(inside[None, :], cos[:, p], 1.0)
    sa = jnp.where(first[None, :], -sin[:, p], 0.0)
    sb = jnp.where(second[None, :], sin[:, p], 0.0)
    ident = (jnp.ones((TM, LANES), F32), jnp.zeros((TM, LANES), F32), jnp.zeros((TM, LANES), F32))
    return tuple(jnp.concatenate([i0, t], axis=0) for i0, t in zip(ident, (ct, sa, sb)))


def kernel(x_prompt, x_sample, c, cache_swa_k, cache_swa_v, cache_mla_ckv, cache_mla_krope, state_mlstm_C, state_mlstm_n, state_mlstm_m, c_ctx, ada_w, ada_b, norm1_g, norm2_g, w_in, a_qn_g, a_kn_g, a_sink, b_gate_b, b_hn_g, c_qa_g, c_kva_g, c_wuq, c_wukv, c_qn_g, c_kn_g, w_out, ffn_w1, ffn_w3, ffn_w2, moe_router, moe_w1, moe_w3, moe_w2):
    x_p, x_s, sample_off = x_prompt.reshape(T_P, D), x_sample.reshape(T_S, D), 0
    cv = jnp.concatenate([c_ctx[None, :], c, jnp.zeros((5, D), F32)], axis=0)
    mod = _ada_call(cv, ada_w, ada_b)
    mod3 = mod.reshape(DEPTH * 8 * 6, 1, D)

    rope_a = _rope_tables(32, 0, HD)
    rope_c = _rope_tables(16, C_NOPE, LANES)

    cache_k4 = cache_swa_k.reshape(NB_S, DEPTH, PAST, A_KV * HD)
    cache_v4 = cache_swa_v.reshape(NB_S, DEPTH, PAST, A_KV * HD)

    news = []
    for l in range(DEPTH):
        w_in_p = _relayout_w_in(w_in[l])[None]
        wuq_p = _pad_heads(c_wuq[l], C_QK).astype(BF16)[None]
        wukv = c_wukv[l].reshape(C_KVRANK, C_HEADS, C_NOPE + C_V)
        wk_p = jnp.pad(wukv[..., :C_NOPE], ((0, 0), (0, 0), (0, LANES - C_NOPE)))
        wk_p = wk_p.reshape(1, C_KVRANK, C_HEADS * LANES).astype(BF16)
        wv_p = wukv[..., C_NOPE:].reshape(1, C_KVRANK, C_HEADS * C_V).astype(BF16)
        w_out_b = w_out[l].astype(BF16)[None]
        gqa = jnp.tile(a_qn_g[l], A_HEADS)[None, :]
        gka = jnp.tile(a_kn_g[l], 2 * A_KV)[None, :]
        gb = _relayout_gate_b(b_gate_b[l])
        gqn = jnp.tile(jnp.pad(c_qn_g[l], (0, LANES - C_QK)), C_HEADS)[None, :]
        gkn = jnp.tile(jnp.pad(c_kn_g[l], (0, LANES - C_QK)), C_HEADS)[None, :]
        (qa, ka, va, kaf, vaf, bq, bk, bv, bo, li, lf, qc, ckv_n, ckr) = _in_call(
            x_p, x_s, sample_off, mod3, l, norm1_g[l][None, :], w_in_p, gqa, gka, gb, c_qa_g[l][None, :],
            c_kva_g[l][None, :], wuq_p, gqn, rope_a, rope_c)

        ckr_cache = jnp.pad(cache_mla_krope[:, l].reshape(NB_S * PAST, C_ROPE),
                            ((0, 0), (C_NOPE, LANES - C_NOPE - C_ROPE)))
        kc, vc = _mlakv_call(ckv_n, cache_mla_ckv[:, l].reshape(NB_S * PAST, C_KVRANK), ckr, ckr_cache,
                             l, wk_p, wv_p, gkn, rope_c)

        sink = a_sink[l]
        oa_p = _attn_a_prompt_call(sink, qa, ka, va)
        oa_s = _attn_a_sample_call(sink, qa, ka, va, cache_k4, cache_v4, l)
        oc_p = _mla_prompt_call(qc, kc, vc)
        oc_s = _mla_sample_call(qc, kc, vc)

        hn2 = jnp.tile(b_hn_g[l], 2)[None, :]
        ob_p, cn, nn, mn = _mlstm_call(bq, bk, bv, bo, li, lf, hn2, None, l)
        c_st = state_mlstm_C[:, l].reshape(NB_S, 2, 2, 2, HD, HD)
        c_st = jnp.transpose(c_st, (0, 2, 1, 3, 4, 5))
        c0 = jnp.zeros((NB_S, 2, 2, LANES, LANES), F32)
        c0 = c0.at[..., :HD, :HD].set(c_st[:, :, :, 0]).at[..., HD:, HD:].set(c_st[:, :, :, 1])
        c0 = c0.reshape(NB_S * 2, 2, LANES, LANES)
        n_st = state_mlstm_n[:, l].reshape(NB_S, 2, 2, 1, LANES)
        n0 = jnp.transpose(n_st, (0, 2, 1, 3, 4)).reshape(NB_S * 2, 2, 1, LANES)
        m_st = state_mlstm_m[:, l].reshape(NB_S, 2, 2, 2)
        m0 = jnp.transpose(m_st, (0, 2, 1, 3)).reshape(NB_S * 2, 1, 4)
        m0 = jnp.pad(m0, ((0, 0), (0, 0), (0, LANES - 4)))
        ob_s = _mlstm_call(bq, bk, bv, bo, li, lf, hn2, (c0, n0, m0), l)

        oa, ob, oc = (oa_p, oa_s), (ob_p, ob_s), (oc_p, oc_s)

        if l % 2 == 0:
            x1, xn = _out_call(x_p, x_s, sample_off, oa, ob, oc, w_out_b, mod3, l, norm2_g[l][None, :], None)
            x = _ffn_call(xn, x1, mod3, l, ffn_w1, ffn_w3, ffn_w2, l // 2)
            x_p, x_s, sample_off = x, x, NT_P
        else:
            r = _pad_cols(moe_router[l // 2], LANES)
            rh = r.astype(BF16)
            rl = (r - rh.astype(F32)).astype(BF16)
            x1, xn, route, cnt = _out_call(x_p, x_s, sample_off, oa, ob, oc, w_out_b, mod3, l,
                                           norm2_g[l][None, :], (rh[None], rl[None]))
            x_p, x_s = _moe_layer(xn, x1, route, cnt, mod3, l, moe_w1, moe_w3, moe_w2, l // 2)
            sample_off = 0

        new_k = kaf.reshape(NB_P, S_P, A_KV, HD)
        new_v = vaf.reshape(NB_P, S_P, A_KV, HD)
        new_ckv = ckv_n[:T_P].reshape(NB_P, S_P, C_KVRANK)
        new_kr = ckr[:T_P, C_NOPE:C_NOPE + C_ROPE].reshape(NB_P, S_P, C_ROPE)
        cn6 = cn.reshape(NB_P, 2, 2, LANES, LANES)
        c_e = jnp.stack([cn6[..., :HD, :HD], cn6[..., HD:, HD:]], axis=3)
        new_c = jnp.transpose(c_e, (0, 2, 1, 3, 4, 5)).reshape(NB_P, 2, B_HEADS, HD, HD)
        nn5 = nn.reshape(NB_P, 2, 2, 2, HD)
        new_n = jnp.transpose(nn5, (0, 2, 1, 3, 4)).reshape(NB_P, 2, B_HEADS, HD)
        mn4 = mn.reshape(NB_P, 2, LANES)[:, :, :4].reshape(NB_P, 2, 2, 2)
        new_m = jnp.transpose(mn4, (0, 2, 1, 3)).reshape(NB_P, 2, B_HEADS)
        news.append((new_k, new_v, new_ckv, new_kr, new_c, new_n, new_m))

    y_prompt = x_p[:T_P].reshape(NB_P, S_P, D)
    y_sample = x_s[sample_off * TM:sample_off * TM + T_S].reshape(NB_S, S_S, D)
    stacked = tuple(jnp.stack([nw[j] for nw in news], axis=1) for j in range(7))
    return (y_prompt, y_sample) + stacked
```

```python
import functools

import jax
import jax.numpy as jnp
import numpy as np
from jax import lax
from jax.experimental import pallas as pl
from jax.experimental.pallas import tpu as pltpu

F32 = jnp.float32
BF16 = jnp.bfloat16

D = 1024
NB_P, S_P = 16, 256
NB_S, S_S = 2, 1024
PAST = 512
DEPTH = 2
T_P = NB_P * S_P
T_S = NB_S * S_S
T = T_P + T_S
TM = 512
NT = T // TM
NT_P = T_P // TM
HD = 64
A_HEADS, A_KV = 6, 2
B_HEADS = 4
CHUNK = 64
C_HEADS = 6
C_QRANK, C_KVRANK, C_NOPE, C_ROPE, C_V = 256, 128, 64, 32, 64
C_QK = C_NOPE + C_ROPE
D_FF = 2816
N_EXP = 8
D_FFE = 3584
EPS = 1e-6
NEG = -1e30
LANES = 128
VMEM_LIMIT = 48 * 1024 * 1024

IN_MAIN = 1664
SEG = dict(QA=(0, 384), KA=(384, 128), VA=(512, 128), BQ=(640, 256), BK=(896, 256), BV=(1152, 256),
           BO=(1408, 256), GI=(1664, 256), GF=(1920, 256), CQ=(2176, 256), CKV=(2432, 128), CKR=(2560, 128))
NP_IN = 2688


def _cparams(sem, vmem=VMEM_LIMIT):
    return pltpu.CompilerParams(dimension_semantics=sem, vmem_limit_bytes=vmem)


def _dot(a, b):
    return jnp.dot(a, b, preferred_element_type=F32)


def _dot_nt(a, b):
    return lax.dot_general(a, b, (((1,), (1,)), ((), ())), preferred_element_type=F32)


def _lane(shape):
    return lax.broadcasted_iota(jnp.int32, shape, len(shape) - 1)


def _mod_row(i):
    return jnp.where(i < NT_P, 0, 1 + (i - NT_P) // (S_S // TM))


def _pair_specs(n, sample_off):
    return [pl.BlockSpec((TM, n), lambda i: (jnp.minimum(i, NT_P - 1), 0)),
            pl.BlockSpec((TM, n), lambda i: (sample_off + jnp.maximum(i - NT_P, 0), 0))]


def _pick(i, p_ref, s_ref):
    return jnp.where(i < NT_P, p_ref[...], s_ref[...])


def _rope_blk(i):
    return jnp.where((i >= NT_P) & (i < NT), 1 + (i - NT_P) % (S_S // TM), 0)


def _ada_kernel(cv_ref, w_ref, b_ref, o_ref):
    s = cv_ref[...]
    s = s * jax.nn.sigmoid(s)
    o_ref[0] = _dot(s.astype(BF16), w_ref[0].astype(BF16)) + b_ref[0]


def _ada_call(cv, ada_w, ada_b):
    tn = 1536
    return pl.pallas_call(
        _ada_kernel,
        grid=(DEPTH, 6 * D // tn),
        in_specs=[pl.BlockSpec((8, D), lambda l, j: (0, 0)),
                  pl.BlockSpec((1, D, tn), lambda l, j: (l, 0, j)),
                  pl.BlockSpec((1, 1, tn), lambda l, j: (l, 0, j))],
        out_specs=pl.BlockSpec((1, 8, tn), lambda l, j: (l, 0, j)),
        out_shape=jax.ShapeDtypeStruct((DEPTH, 8, 6 * D), F32),
        compiler_params=_cparams(("arbitrary", "arbitrary")),
        name="ada_mod",
    )(cv, ada_w, ada_b.reshape(DEPTH, 1, 6 * D))


def _half_norm(x, g, n):
    outs = []
    for j in range(x.shape[1] // LANES):
        xj = x[:, j * LANES:(j + 1) * LANES]
        lo = _lane(xj.shape) < HD
        s = xj * xj
        s_lo = jnp.sum(jnp.where(lo, s, 0.0), axis=-1, keepdims=True)
        s_hi = jnp.sum(jnp.where(lo, 0.0, s), axis=-1, keepdims=True)
        r = lax.rsqrt(jnp.where(lo, s_lo, s_hi) * (1.0 / n) + EPS)
        outs.append(xj * r)
    return jnp.concatenate(outs, axis=-1) * g


def _group_norm(x, g, n):
    outs = []
    for j in range(x.shape[1] // LANES):
        xj = x[:, j * LANES:(j + 1) * LANES]
        r = lax.rsqrt(jnp.sum(xj * xj, axis=-1, keepdims=True) * (1.0 / n) + EPS)
        outs.append(xj * r)
    return jnp.concatenate(outs, axis=-1) * g


def _rope(x, c, sa, sb, shift):
    outs = []
    for j in range(x.shape[1] // LANES):
        xj = x[:, j * LANES:(j + 1) * LANES]
        outs.append(xj * c + pltpu.roll(xj, LANES - shift, 1) * sa + pltpu.roll(xj, shift, 1) * sb)
    return jnp.concatenate(outs, axis=-1)


def _in_kernel(xp_ref, xs_ref, sh_ref, sc_ref, n1_ref, w_ref, gqa_ref, gka_ref, gb_ref, gcq_ref, gckv_ref,
               wuq_ref, gqn_ref, ra_c, ra_a, ra_b, rc_c, rc_a, rc_b,
               qa_o, ka_o, va_o, kaf_o, vaf_o, bq_o, bk_o, bv_o, bo_o, li_o, lf_o, qc_o, ckv_o, ckr_o):
    i = pl.program_id(0)
    x = _pick(i, xp_ref, xs_ref)
    xn = x * lax.rsqrt(jnp.mean(x * x, axis=-1, keepdims=True) + EPS) * n1_ref[...]
    xn = xn * (1.0 + sc_ref[0]) + sh_ref[0]
    p = _dot(xn.astype(BF16), w_ref[0])

    def seg(name):
        o, n = SEG[name]
        return p[:, o:o + n]

    qa = _half_norm(seg("QA"), gqa_ref[...], HD)
    ka = _half_norm(seg("KA"), gka_ref[...], HD)
    va = seg("VA")

    @pl.when(i < NT_P)
    def _():
        kaf_o[...] = ka
        vaf_o[...] = va

    def dup(x):
        lo, hi = _half(x, False), _half(x, True)
        return jnp.concatenate([lo + pltpu.roll(lo, HD, 1), hi + pltpu.roll(hi, HD, 1)], axis=-1)

    va_o[...] = dup(va).astype(BF16)

    bq_o[...] = seg("BQ").astype(BF16)
    bk_o[...] = (seg("BK") * (HD ** -0.5)).astype(BF16)
    bv_o[...] = seg("BV").astype(BF16)
    bo_o[...] = seg("BO")
    gb = gb_ref[...]
    li_o[...] = seg("GI") + gb[:, 0:256]
    lf_o[...] = jax.nn.log_sigmoid(seg("GF") + gb[:, 256:512])

    cq = seg("CQ")
    cqn = cq * lax.rsqrt(jnp.mean(cq * cq, axis=-1, keepdims=True) + EPS) * gcq_ref[...]
    qc = _group_norm(_dot(cqn.astype(BF16), wuq_ref[0]), gqn_ref[...], C_QK)

    @pl.when(i < NT_P)
    def _():
        qa_o[...] = qa.astype(BF16)
        ka_o[...] = dup(ka).astype(BF16)
        qc_o[...] = qc.astype(BF16)

    @pl.when(i >= NT_P)
    def _():
        qa_o[...] = _rope(qa, ra_c[...], ra_a[...], ra_b[...], 32).astype(BF16)
        ka_o[...] = dup(_rope(ka, ra_c[...], ra_a[...], ra_b[...], 32)).astype(BF16)
        qc_o[...] = _rope(qc, rc_c[...], rc_a[...], rc_b[...], 16).astype(BF16)

    ckv = seg("CKV")
    ckv_o[...] = ckv * lax.rsqrt(jnp.mean(ckv * ckv, axis=-1, keepdims=True) + EPS) * gckv_ref[...]
    ckr_o[...] = seg("CKR")


def _in_call(x_p, x_s, sample_off, mod3, l, n1g, w_in_p, gqa, gka, gb, gcq, gckv, wuq_p, gqn, rope_a, rope_c):
    row = lambda k: pl.BlockSpec((1, 1, D), lambda i: ((l * 8 + _mod_row(i)) * 6 + k, 0, 0))
    vec = lambda n: pl.BlockSpec((1, n), lambda i: (0, 0))
    tab = pl.BlockSpec((TM, LANES), lambda i: (_rope_blk(i), 0))
    tok = lambda n: pl.BlockSpec((TM, n), lambda i: (i, 0))
    tokp = pl.BlockSpec((TM, LANES), lambda i: (jnp.minimum(i, NT_P - 1), 0))
    o = lambda n, dt: jax.ShapeDtypeStruct((T, n), dt)
    return pl.pallas_call(
        _in_kernel,
        grid=(NT,),
        in_specs=_pair_specs(D, sample_off) + [row(0), row(1), vec(D),
                  pl.BlockSpec((1, D, NP_IN), lambda i: (0, 0, 0)),
                  vec(384), vec(128), vec(512), vec(256), vec(128),
                  pl.BlockSpec((1, C_QRANK, 768), lambda i: (0, 0, 0)), vec(768),
                  tab, tab, tab, tab, tab, tab],
        out_specs=[tok(384), tok(256), tok(256), tokp, tokp, tok(256), tok(256), tok(256), tok(256),
                   tok(256), tok(256), tok(768), tok(128), tok(128)],
        out_shape=[o(384, BF16), o(256, BF16), o(256, BF16),
                   jax.ShapeDtypeStruct((T_P, LANES), F32), jax.ShapeDtypeStruct((T_P, LANES), F32),
                   o(256, BF16), o(256, BF16), o(256, BF16), o(256, F32),
                   o(256, F32), o(256, F32), o(768, BF16), o(128, F32), o(128, F32)],
        compiler_params=_cparams(("arbitrary",)),
        name=f"in_proj_l{l}",
    )(x_p, x_s, mod3, mod3, n1g, w_in_p, gqa, gka, gb, gcq, gckv, wuq_p, gqn, *rope_a, *rope_c)


def _mlakv_kernel(ckv_ref, ckvc_ref, ckr_ref, ckrc_ref, wk_ref, wv_ref, gkn_ref, rc_c, rc_a, rc_b, k_o, v_o):
    i = pl.program_id(0)
    tok = i < NT
    c = jnp.where(tok, ckv_ref[...], ckvc_ref[...]).astype(BF16)
    k = _dot(c, wk_ref[0])
    kr = jnp.where(tok, ckr_ref[...], ckrc_ref[...])
    k = k + jnp.concatenate([kr] * C_HEADS, axis=-1)
    k = _group_norm(k, gkn_ref[...], C_QK)
    sample = (i >= NT_P) & tok

    @pl.when(sample)
    def _():
        k_o[...] = _rope(k, rc_c[...], rc_a[...], rc_b[...], 16).astype(BF16)

    @pl.when(jnp.logical_not(sample))
    def _():
        k_o[...] = k.astype(BF16)

    v_o[...] = _dot(c, wv_ref[0]).astype(BF16)


def _mlakv_call(ckv_n, ckv_cache, ckr, ckr_cache, l, wk_p, wv_p, gkn, rope_c):
    r = T + NB_S * PAST
    tab = pl.BlockSpec((TM, LANES), lambda i: (_rope_blk(i), 0))
    tok = lambda n: pl.BlockSpec((TM, n), lambda i: (i, 0))
    tokens = pl.BlockSpec((TM, LANES), lambda i: (jnp.minimum(i, NT - 1), 0))
    cached = pl.BlockSpec((TM, LANES), lambda i: (jnp.maximum(i - NT, 0), 0))
    return pl.pallas_call(
        _mlakv_kernel,
        grid=(r // TM,),
        in_specs=[tokens, cached, tokens, cached,
                  pl.BlockSpec((1, C_KVRANK, 768), lambda i: (0, 0, 0)),
                  pl.BlockSpec((1, C_KVRANK, 384), lambda i: (0, 0, 0)),
                  pl.BlockSpec((1, 768), lambda i: (0, 0)), tab, tab, tab],
        out_specs=[tok(768), tok(384)],
        out_shape=[jax.ShapeDtypeStruct((r, 768), BF16), jax.ShapeDtypeStruct((r, 384), BF16)],
        compiler_params=_cparams(("arbitrary",)),
        name=f"mla_kv_l{l}",
    )(ckv_n, ckv_cache, ckr, ckr_cache, wk_p, wv_p, gkn, *rope_c)


def _bmm(a, b):
    return lax.dot_general(a, b, (((2,), (1,)), ((0,), (0,))), preferred_element_type=F32)


def _bmm_nt(a, b):
    return lax.dot_general(a, b, (((2,), (2,)), ((0,), (0,))), preferred_element_type=F32)


def _bmm_tn(a, b):
    return lax.dot_general(a, b, (((1,), (1,)), ((0,), (0,))), preferred_element_type=F32)


def _softmax_pv(scores, values, sink):
    batched = scores[0].ndim == 3
    heads = range(scores[0].shape[0]) if batched else ()
    m = scores[0].max(axis=-1, keepdims=True)
    for s in scores[1:]:
        m = jnp.maximum(m, s.max(axis=-1, keepdims=True))
    if sink is not None:
        m = jnp.stack([jnp.maximum(m[h], sink[h]) for h in heads]) if batched else jnp.maximum(m, sink)
    es = [jnp.exp(s - m) for s in scores]
    den = es[0].sum(axis=-1, keepdims=True)
    for e in es[1:]:
        den = den + e.sum(axis=-1, keepdims=True)
    if sink is not None:
        den = (jnp.stack([den[h] + jnp.exp(sink[h] - m[h]) for h in heads]) if batched
               else den + jnp.exp(sink - m))
    inv = 1.0 / den
    out = None
    for e, v in zip(es, values):
        o = (_bmm if batched else _dot)((e * inv).astype(BF16), v)
        out = o if out is None else out + o
    return out


def _scores(q3, k3, scale):
    return jnp.stack([_dot_nt(q3[h], k3[h]) for h in range(q3.shape[0])]) * scale


def _half(x, hi):
    lo = _lane(x.shape) < HD
    return jnp.where(lo != hi, x, jnp.zeros_like(x))


def _pair_sum_store(o3, o_ref):
    for j in range(o3.shape[0] // 2):
        o_ref[:, j * LANES:(j + 1) * LANES] = (o3[2 * j] + o3[2 * j + 1]).astype(BF16)


def _a_heads(q, kdups, vdups):
    q3 = jnp.stack([_half(q[:, (h // 2) * LANES:(h // 2 + 1) * LANES], h % 2 == 1) for h in range(A_HEADS)])
    k3 = [kdups[h // (A_HEADS // A_KV)] for h in range(A_HEADS)]
    v3 = jnp.stack([_half(vdups[h // (A_HEADS // A_KV)], h % 2 == 1) for h in range(A_HEADS)])
    return q3, k3, v3


def _sink3(sink_ref):
    return [sink_ref[h] for h in range(A_HEADS)]


def _attn_a_prompt_kernel(sink_ref, q_ref, k_ref, v_ref, o_ref):
    q = q_ref[...]
    k = k_ref[...]
    v = v_ref[...]
    for j in range(A_HEADS // 2):
        acc = None
        for c in range(2):
            h = 2 * j + c
            g = h // (A_HEADS // A_KV)
            qh = _half(q[:, j * LANES:(j + 1) * LANES], c == 1)
            s = _dot_nt(qh, k[:, g * LANES:(g + 1) * LANES]) * (HD ** -0.5)
            vh = _half(v[:, g * LANES:(g + 1) * LANES], c == 1)
            o = _softmax_pv([s], [vh], sink_ref[h])
            acc = o if acc is None else acc + o
        o_ref[:, j * LANES:(j + 1) * LANES] = acc.astype(BF16)


def _attn_a_prompt_call(sink, qa, ka, va):
    return pl.pallas_call(
        _attn_a_prompt_kernel,
        grid_spec=pltpu.PrefetchScalarGridSpec(
            num_scalar_prefetch=1, grid=(NB_P,),
            in_specs=[pl.BlockSpec((S_P, 384), lambda b, s: (b, 0)),
                      pl.BlockSpec((S_P, 256), lambda b, s: (b, 0)),
                      pl.BlockSpec((S_P, 256), lambda b, s: (b, 0))],
            out_specs=pl.BlockSpec((S_P, 384), lambda b, s: (b, 0))),
        out_shape=jax.ShapeDtypeStruct((T_P, 384), BF16),
        compiler_params=_cparams(("arbitrary",)),
        name="attn_a_prompt",
    )(sink, qa, ka, va)


QB = 128
WIN = 128
BAND = QB + 2 * WIN


def _attn_a_sample_kernel(sink_ref, q_ref, k_ref, v_ref, kc_ref, vc_ref, o_ref):
    n = pl.program_id(1)
    ws = pl.multiple_of(jnp.clip(n * QB - WIN, 0, S_S - BAND), QB)
    kb = k_ref[pl.ds(ws, BAND), :]
    vb = v_ref[pl.ds(ws, BAND), :]
    qpos = n * QB + lax.broadcasted_iota(jnp.int32, (QB, BAND), 0)
    kpos = ws + lax.broadcasted_iota(jnp.int32, (QB, BAND), 1)
    ok = (jnp.abs(qpos - kpos) <= WIN)[None]
    groups = range(A_KV)
    q3, k3, v3 = _a_heads(q_ref[...], [kb[:, g * LANES:(g + 1) * LANES] for g in groups],
                          [vb[:, g * LANES:(g + 1) * LANES] for g in groups])

    def dup(x, g):
        xg = _half(x, g == 1)
        return (xg + pltpu.roll(xg, HD, 1)).astype(BF16)

    kc = kc_ref[0, 0]
    vc = vc_ref[0, 0]
    _, kc3, vc3 = _a_heads(q_ref[...], [dup(kc, g) for g in groups], [dup(vc, g) for g in groups])
    s_b = jnp.where(ok, _scores(q3, k3, HD ** -0.5), NEG)
    s_c = _scores(q3, kc3, HD ** -0.5)
    _pair_sum_store(_softmax_pv([s_b, s_c], [v3, vc3], _sink3(sink_ref)), o_ref)


def _attn_a_sample_call(sink, qa, ka, va, cache_k, cache_v, l):
    nqb = S_S // QB
    off = T_P // S_S
    return pl.pallas_call(
        _attn_a_sample_kernel,
        grid_spec=pltpu.PrefetchScalarGridSpec(
            num_scalar_prefetch=1, grid=(NB_S, nqb),
            in_specs=[pl.BlockSpec((QB, 384), lambda b, n, s: (T_P // QB + b * nqb + n, 0)),
                      pl.BlockSpec((S_S, 256), lambda b, n, s: (off + b, 0)),
                      pl.BlockSpec((S_S, 256), lambda b, n, s: (off + b, 0)),
                      pl.BlockSpec((1, 1, PAST, LANES), lambda b, n, s: (b, l, 0, 0)),
                      pl.BlockSpec((1, 1, PAST, LANES), lambda b, n, s: (b, l, 0, 0))],
            out_specs=pl.BlockSpec((QB, 384), lambda b, n, s: (b * nqb + n, 0))),
        out_shape=jax.ShapeDtypeStruct((T_S, 384), BF16),
        compiler_params=_cparams(("arbitrary", "arbitrary")),
        name=f"attn_a_sample_l{l}",
    )(sink, qa, ka, va, cache_k, cache_v)


def _mla_heads(q, ks, vs, o_ref):
    heads = range(C_HEADS)
    q3 = jnp.stack([q[:, h * LANES:(h + 1) * LANES] for h in heads])
    scores = [_scores(q3, [k[:, h * LANES:(h + 1) * LANES] for h in heads], C_QK ** -0.5) for k in ks]
    vals = [jnp.stack([_half(v[:, (h // 2) * LANES:(h // 2 + 1) * LANES], h % 2 == 1) for h in heads])
            for v in vs]
    _pair_sum_store(_softmax_pv(scores, vals, None), o_ref)


def _mla_prompt_kernel(q_ref, k_ref, v_ref, o_ref):
    _mla_heads(q_ref[...], [k_ref[...]], [v_ref[...]], o_ref)


def _mla_prompt_call(qc, kc, vc):
    return pl.pallas_call(
        _mla_prompt_kernel,
        grid=(NB_P,),
        in_specs=[pl.BlockSpec((S_P, 768), lambda b: (b, 0)),
                  pl.BlockSpec((S_P, 768), lambda b: (b, 0)),
                  pl.BlockSpec((S_P, 384), lambda b: (b, 0))],
        out_specs=pl.BlockSpec((S_P, 384), lambda b: (b, 0)),
        out_shape=jax.ShapeDtypeStruct((T_P, 384), BF16),
        compiler_params=_cparams(("arbitrary",)),
        name="mla_prompt",
    )(qc, kc, vc)


def _mla_sample_kernel(q_ref, kc_ref, vc_ref, kl_ref, vl_ref, o_ref):
    _mla_heads(q_ref[...], [kc_ref[...], kl_ref[...]], [vc_ref[...], vl_ref[...]], o_ref)


def _mla_sample_call(qc, kc, vc):
    tq = 256
    nq = S_S // tq
    return pl.pallas_call(
        _mla_sample_kernel,
        grid=(NB_S, nq),
        in_specs=[pl.BlockSpec((tq, 768), lambda b, n: (T_P // tq + b * nq + n, 0)),
                  pl.BlockSpec((PAST, 768), lambda b, n: (T // PAST + b, 0)),
                  pl.BlockSpec((PAST, 384), lambda b, n: (T // PAST + b, 0)),
                  pl.BlockSpec((S_S, 768), lambda b, n: (T_P // S_S + b, 0)),
                  pl.BlockSpec((S_S, 384), lambda b, n: (T_P // S_S + b, 0))],
        out_specs=pl.BlockSpec((tq, 384), lambda b, n: (b * nq + n, 0)),
        out_shape=jax.ShapeDtypeStruct((T_S, 384), BF16),
        compiler_params=_cparams(("arbitrary", "arbitrary")),
        name="mla_sample",
    )(qc, kc, vc, kc, vc)


def _split3(x):
    x1 = x.astype(BF16)
    r = x - x1.astype(F32)
    x2 = r.astype(BF16)
    x3 = (r - x2.astype(F32)).astype(BF16)
    return x1, x2, x3


def _mlstm_kernel(nc, has_state, *refs):
    for j in range(B_HEADS // 2):
        _mlstm_pair(nc, has_state, j, *refs)


def _mlstm_pair(nc, has_state, j, *refs):
    if has_state:
        (q_ref, k_ref, v_ref, bo_ref, li_ref, lf_ref, hn_ref, c0_ref, n0_ref, m0_ref, ob_ref) = refs
    else:
        (q_ref, k_ref, v_ref, bo_ref, li_ref, lf_ref, hn_ref, ob_ref, cn_ref, nn_ref, mn_ref) = refs
    s_len = nc * CHUNK
    pair = slice(j * LANES, (j + 1) * LANES)
    c3 = lambda x: x.reshape(nc, CHUNK, x.shape[-1])
    li3 = c3(li_ref[:, pair])
    lf3 = c3(lf_ref[:, pair])
    row = lax.broadcasted_iota(jnp.int32, (CHUNK, CHUNK), 0)
    colv = lax.broadcasted_iota(jnp.int32, (CHUNK, CHUNK), 1)
    tri_f = colv <= row
    tri_b = colv >= row
    bcast = lambda m: jnp.broadcast_to(m[None], (nc, CHUNK, CHUNK))
    lane1 = _lane((1, LANES))
    fwd_lane = lane1 < 2

    lf_parts = _split3(lf3)
    tf3 = bcast(tri_f.astype(BF16))
    tb3 = bcast(tri_b.astype(BF16))
    bc_f = _bmm(tf3, lf_parts[0]) + _bmm(tf3, lf_parts[1]) + _bmm(tf3, lf_parts[2])
    bc_b = _bmm(tb3, lf_parts[0]) + _bmm(tb3, lf_parts[1]) + _bmm(tb3, lf_parts[2])
    bc3 = jnp.where(fwd_lane, bc_f, bc_b)
    row3 = lax.broadcasted_iota(jnp.int32, (nc, CHUNK, LANES), 1)
    edge = jnp.where(row3 == jnp.where(fwd_lane, CHUNK - 1, 0), bc3, 0.0)
    bl3 = jnp.sum(edge, axis=1, keepdims=True)
    gg3 = bl3 - bc3 + li3
    bl2 = jnp.sum(edge, axis=1)
    mg2 = gg3.max(axis=1)

    m0 = m0_ref[j] if has_state else jnp.zeros((1, LANES), F32)
    mf = m0
    mb = m0
    mf_prev, mf_next, mb_prev, mb_next = {}, {}, {}, {}
    for i in range(nc):
        mf_prev[i] = mf
        mf = jnp.maximum(bl2[i:i + 1] + mf, mg2[i:i + 1])
        mf_next[i] = mf
        cb = nc - 1 - i
        mb_prev[cb] = mb
        mb = jnp.maximum(bl2[cb:cb + 1] + mb, mg2[cb:cb + 1])
        mb_next[cb] = mb
    m_prev = [jnp.where(fwd_lane, mf_prev[c], mb_prev[c]) for c in range(nc)]
    m_next = [jnp.where(fwd_lane, mf_next[c], mb_next[c]) for c in range(nc)]
    m_prev3 = jnp.stack(m_prev)
    m_next3 = jnp.stack(m_next)
    dec2d = jnp.exp(bl2 + jnp.concatenate(m_prev, axis=0) - jnp.concatenate(m_next, axis=0))
    if not has_state:
        mn_ref[j] = jnp.where(fwd_lane, mf, mb)

    u_parts = _split3(li3 - bc3)
    ws3 = jnp.exp(gg3 - m_next3)
    inter3 = bc3 + m_prev3

    q3 = c3(q_ref[:, pair])
    k3 = c3(k_ref[:, pair])
    v3 = c3(v_ref[:, pair])
    q3f = q3.astype(F32)
    k3f = k3.astype(F32)
    lane3 = _lane((nc, CHUNK, LANES))
    lo3 = lane3 < HD
    rr = lax.broadcasted_iota(jnp.int32, (LANES, LANES), 0)
    cc = lax.broadcasted_iota(jnp.int32, (LANES, LANES), 1)
    blockdiag = (rr < HD) == (cc < HD)
    col = lambda x, k: x[:, :, k:k + 1]

    hsum = None
    for d in range(2):
        causal3 = bcast(tri_f if d == 0 else tri_b)
        a_c, ws_sum, mt_c, ws_c, intra = [], [], [], [], None
        for e in range(2):
            kk = 2 * d + e
            pick = jnp.broadcast_to(jnp.where(_lane((CHUNK, LANES)) == kk, 1.0, 0.0).astype(BF16)[None],
                                    (nc, CHUNK, LANES))
            ub = _bmm_nt(pick, u_parts[0]) + _bmm_nt(pick, u_parts[1]) + _bmm_nt(pick, u_parts[2])
            bc_col = col(bc3, kk)
            d_mat = jnp.where(causal3, bc_col + ub, NEG)
            inter = col(inter3, kk)
            mt = jnp.maximum(inter, d_mat.max(axis=-1, keepdims=True))
            a = jnp.exp(inter - mt)
            w = jnp.exp(d_mat - mt) * _bmm_nt(_half(q3, e == 1), k3)
            o = _bmm(w.astype(BF16), _half(v3, e == 1))
            intra = o if intra is None else intra + o
            a_c.append(a)
            mt_c.append(mt)
            ws_sum.append(w.sum(axis=-1, keepdims=True))
            ws_c.append(col(ws3, kk))
        kw3 = k3f * jnp.where(lo3, ws_c[0], ws_c[1])
        dec2 = jnp.where(_lane((nc, LANES)) < HD, dec2d[:, 2 * d:2 * d + 1], dec2d[:, 2 * d + 1:2 * d + 2])
        u_all = jnp.where(blockdiag, _bmm_tn(kw3.astype(BF16), v3), 0.0)
        kwsum = kw3.sum(axis=1)
        if has_state:
            cst = c0_ref[j, d]
            nst = n0_ref[j, d]
        else:
            cst = jnp.zeros((LANES, LANES), F32)
            nst = jnp.zeros((1, LANES), F32)
        cs, ns = [None] * nc, [None] * nc
        for c in (range(nc) if d == 0 else range(nc - 1, -1, -1)):
            cs[c] = cst
            ns[c] = nst
            cst = dec2[c:c + 1] * cst + u_all[c]
            nst = dec2[c:c + 1] * nst + kwsum[c:c + 1]
        if not has_state:
            for e in range(2):
                half = slice(e * HD, (e + 1) * HD)
                cn_ref[0, d, 2 * j + e] = cst[half, half]
                nn_ref[0, d, 2 * j + e:2 * j + e + 1, :] = nst[:, half]
        qc = _bmm(q3, jnp.stack(cs).astype(BF16))
        qn_all = q3f * jnp.stack(ns)
        dn = []
        for e in range(2):
            qn = jnp.sum(jnp.where(lo3 != (e == 1), qn_all, 0.0), axis=-1, keepdims=True)
            den = a_c[e] * qn + ws_sum[e]
            dn.append(jnp.maximum(jnp.abs(den), jnp.exp(-mt_c[e])))
        h2 = (jnp.where(lo3, a_c[0], a_c[1]) * qc + intra) / jnp.where(lo3, dn[0], dn[1])
        hsum = h2 if hsum is None else hsum + h2

    hs = hsum.reshape(s_len, LANES)
    lo = _lane(hs.shape) < HD
    s = hs * hs
    s_lo = jnp.sum(jnp.where(lo, s, 0.0), axis=-1, keepdims=True)
    s_hi = jnp.sum(jnp.where(lo, 0.0, s), axis=-1, keepdims=True)
    r = lax.rsqrt(jnp.where(lo, s_lo, s_hi) * (1.0 / HD) + EPS)
    ob_ref[:, pair] = (hs * r * hn_ref[...] * jax.nn.sigmoid(bo_ref[:, pair])).astype(BF16)


def _mlstm_call(bq, bk, bv, bo, li, lf, hn2, state, l):
    has_state = state is not None
    if has_state:
        nb, s_len, base = NB_S, S_S, T_P // S_S
    else:
        nb, s_len, base = NB_P, S_P, 0
    nc = s_len // CHUNK
    tokp = pl.BlockSpec((s_len, 2 * LANES), lambda b: (base + b, 0))
    in_specs = [tokp, tokp, tokp, tokp, tokp, tokp, pl.BlockSpec((1, LANES), lambda b: (0, 0))]
    args = [bq, bk, bv, bo, li, lf, hn2]
    ob_spec = pl.BlockSpec((s_len, 2 * LANES), lambda b: (b, 0))
    ob_shape = jax.ShapeDtypeStruct((nb * s_len, 256), BF16)
    state_specs = [pl.BlockSpec((2, 2, LANES, LANES), lambda b: (b, 0, 0, 0)),
                   pl.BlockSpec((2, 2, 1, LANES), lambda b: (b, 0, 0, 0)),
                   pl.BlockSpec((2, 1, LANES), lambda b: (b, 0, 0))]
    if has_state:
        c0, n0, m0 = state
        in_specs += state_specs
        args += [c0, n0, m0]
        out_specs = ob_spec
        out_shape = ob_shape
    else:
        out_specs = [ob_spec,
                     pl.BlockSpec((1, 2, B_HEADS, HD, HD), lambda b: (b, 0, 0, 0, 0)),
                     pl.BlockSpec((1, 2, B_HEADS, HD), lambda b: (b, 0, 0, 0)),
                     state_specs[2]]
        out_shape = [ob_shape,
                     jax.ShapeDtypeStruct((nb, 2, B_HEADS, HD, HD), F32),
                     jax.ShapeDtypeStruct((nb, 2, B_HEADS, HD), F32),
                     jax.ShapeDtypeStruct((nb * 2, 1, LANES), F32)]
    return pl.pallas_call(
        functools.partial(_mlstm_kernel, nc, has_state),
        grid=(nb,),
        in_specs=in_specs,
        out_specs=out_specs,
        out_shape=out_shape,
        compiler_params=_cparams(("arbitrary",)),
        name=f"mlstm_{'sample' if has_state else 'prompt'}_l{l}",
    )(*args)


def _out_kernel(moe, xp_ref, xs_ref, oap_ref, oas_ref, obp_ref, obs_ref, ocp_ref, ocs_ref,
                w_ref, g1_ref, sh_ref, sc_ref, n2_ref, *rest):
    if moe:
        rh_ref, rl_ref, x1_o, xn_o, route_o, cnt_o, run_scr = rest
    else:
        x1_o, xn_o = rest
    i = pl.program_id(0)
    o = (_dot(_pick(i, oap_ref, oas_ref), w_ref[0, 0:384, :])
         + _dot(_pick(i, obp_ref, obs_ref), w_ref[0, 384:640, :])
         + _dot(_pick(i, ocp_ref, ocs_ref), w_ref[0, 640:1024, :]))
    x1 = _pick(i, xp_ref, xs_ref) + g1_ref[0] * o
    x1_o[...] = x1
    xn = x1 * lax.rsqrt(jnp.mean(x1 * x1, axis=-1, keepdims=True) + EPS) * n2_ref[...]
    xn = xn * (1.0 + sc_ref[0]) + sh_ref[0]
    xb = xn.astype(BF16)
    if not moe:
        xn_o[...] = xb
    else:
        xn_o[...] = xn
        xl = (xn - xb.astype(F32)).astype(BF16)
        logits = _dot(xb, rh_ref[0]) + (_dot(xl, rh_ref[0]) + _dot(xb, rl_ref[0]))
        lane = _lane(logits.shape)
        logits = jnp.where(lane < N_EXP, logits, -jnp.inf)
        m1 = logits.max(axis=-1, keepdims=True)
        i1 = jnp.min(jnp.where(logits == m1, lane, LANES), axis=-1, keepdims=True)
        rest_l = jnp.where(lane == i1, -jnp.inf, logits)
        m2 = rest_l.max(axis=-1, keepdims=True)
        i2 = jnp.min(jnp.where(rest_l == m2, lane, LANES), axis=-1, keepdims=True)
        e2 = jnp.exp(m2 - m1)
        den = 1.0 + e2
        w1 = 1.0 / den
        w2 = e2 / den

        @pl.when(pl.program_id(0) == 0)
        def _():
            run_scr[...] = jnp.zeros(run_scr.shape, F32)

        sel = jnp.where(lane == i1, 1.0, jnp.where((lane == i2) & (w2 > 0.0), 1.0, 0.0))
        rb = 256
        before = (lax.broadcasted_iota(jnp.int32, (rb, rb), 1)
                  < lax.broadcasted_iota(jnp.int32, (rb, rb), 0)).astype(BF16)
        run = run_scr[...]
        ranks = []
        for k in range(TM // rb):
            part = sel[k * rb:(k + 1) * rb]
            ranks.append(run + _dot(before, part.astype(BF16)))
            run = run + jnp.sum(part, axis=0, keepdims=True)
        rank = jnp.concatenate(ranks, axis=0)
        r1 = jnp.sum(jnp.where(lane == i1, rank, 0.0), axis=-1, keepdims=True)
        r2 = jnp.sum(jnp.where(lane == i2, rank, 0.0), axis=-1, keepdims=True)
        run_scr[...] = run
        cnt_o[...] = run_scr[...]
        fields = [i1.astype(F32), i2.astype(F32), r1, r2, w1, w2]
        info = jnp.zeros(lane.shape, F32)
        for k, v in enumerate(fields):
            info = jnp.where(lane == k, v, info)
        route_o[...] = info.T


def _out_call(x_p, x_s, sample_off, oa, ob, oc, w_out_b, mod3, l, n2g, router):
    moe = router is not None
    row = lambda k: pl.BlockSpec((1, 1, D), lambda i: ((l * 8 + _mod_row(i)) * 6 + k, 0, 0))
    tok = lambda n: pl.BlockSpec((TM, n), lambda i: (i, 0))
    in_specs = (_pair_specs(D, sample_off) + _pair_specs(384, 0) + _pair_specs(256, 0) + _pair_specs(384, 0)
                + [pl.BlockSpec((1, D, D), lambda i: (0, 0, 0)),
                   row(2), row(3), row(4), pl.BlockSpec((1, D), lambda i: (0, 0))])
    args = [x_p, x_s, *oa, *ob, *oc, w_out_b, mod3, mod3, mod3, n2g]
    out_specs = [tok(D), tok(D)]
    out_shape = [jax.ShapeDtypeStruct((T, D), F32), jax.ShapeDtypeStruct((T, D), BF16)]
    if moe:
        out_shape[1] = jax.ShapeDtypeStruct((T, D), F32)
        rh, rl = router
        in_specs += [pl.BlockSpec((1, D, LANES), lambda i: (0, 0, 0))] * 2
        args += [rh, rl]
        out_specs += [pl.BlockSpec((LANES, TM), lambda i: (0, i)), pl.BlockSpec((1, LANES), lambda i: (0, 0))]
        out_shape += [jax.ShapeDtypeStruct((LANES, T), F32), jax.ShapeDtypeStruct((1, LANES), F32)]
    return pl.pallas_call(
        functools.partial(_out_kernel, moe),
        grid=(NT,),
        in_specs=in_specs, out_specs=out_specs, out_shape=out_shape,
        scratch_shapes=[pltpu.VMEM((1, LANES), F32)] if moe else [],
        compiler_params=_cparams(("arbitrary",)),
        name=f"out_proj_l{l}",
    )(*args)


FFN_TM = 1024
FFN_TF = 256
FFN_TN = 256


def _hidden_tile(x, w1_ref, w3_ref):
    a = _dot(x, w1_ref[...].astype(BF16))
    b = _dot(x, w3_ref[...].astype(BF16))
    return (a * jax.nn.sigmoid(a) * b).astype(BF16)


def _down_tile(h_scr, w2n, m):
    out = None
    for f in range(h_scr.shape[0]):
        o = _dot(h_scr[f, 0:m, :], w2n[f])
        out = o if out is None else out + o
    return out


def _keep_weights(s, w1_ref, w3_ref, w2_ref, w1b, w3b, w2b, tn):
    w1b[s] = w1_ref[...].astype(BF16)
    w3b[s] = w3_ref[...].astype(BF16)
    w2 = w2_ref[...].astype(BF16)
    for n in range(w2b.shape[0]):
        w2b[n, s] = w2[:, n * tn:(n + 1) * tn]


def _ffn_kernel(nf, xn_ref, x1_ref, g2_ref, w1_ref, w3_ref, w2_ref, y_ref, h_scr, w1b, w3b, w2b):
    i = pl.program_id(0)
    s = pl.program_id(1)

    @pl.when((i == 0) & (s < nf))
    def _():
        _keep_weights(s, w1_ref.at[0], w3_ref.at[0], w2_ref.at[0], w1b, w3b, w2b, FFN_TN)

    @pl.when(s < nf)
    def _():
        h_scr[s] = _hidden_tile(xn_ref[...], w1b.at[s], w3b.at[s])

    @pl.when(s >= nf)
    def _():
        y_ref[...] = x1_ref[...] + g2_ref[0] * _down_tile(h_scr, w2b.at[s - nf], FFN_TM)


def _ffn_call(xn, x1, mod3, l, w1, w3, w2, i_layer):
    nf = D_FF // FFN_TF
    nn = D // FFN_TN
    tile = lambda i, s: jnp.where(i == 0, jnp.minimum(s, nf - 1), nf - 1)
    out = lambda s: jnp.maximum(s - nf, 0)

    def g2_idx(i, s):
        r = jnp.where(i < T_P // FFN_TM, 0, 1 + (i - T_P // FFN_TM) // (S_S // FFN_TM))
        return ((l * 8 + r) * 6 + 5, 0, out(s))

    return pl.pallas_call(
        functools.partial(_ffn_kernel, nf),
        grid=(T // FFN_TM, nf + nn),
        in_specs=[pl.BlockSpec((FFN_TM, D), lambda i, s: (i, 0)),
                  pl.BlockSpec((FFN_TM, FFN_TN), lambda i, s: (i, out(s))),
                  pl.BlockSpec((1, 1, FFN_TN), g2_idx),
                  pl.BlockSpec((1, D, FFN_TF), lambda i, s: (i_layer, 0, tile(i, s))),
                  pl.BlockSpec((1, D, FFN_TF), lambda i, s: (i_layer, 0, tile(i, s))),
                  pl.BlockSpec((1, FFN_TF, D), lambda i, s: (i_layer, tile(i, s), 0))],
        out_specs=pl.BlockSpec((FFN_TM, FFN_TN), lambda i, s: (i, out(s))),
        out_shape=jax.ShapeDtypeStruct((T, D), F32),
        scratch_shapes=[pltpu.VMEM((nf, FFN_TM, FFN_TF), BF16),
                        pltpu.VMEM((nf, D, FFN_TF), BF16), pltpu.VMEM((nf, D, FFN_TF), BF16),
                        pltpu.VMEM((nn, nf, FFN_TF, FFN_TN), BF16)],
        compiler_params=_cparams(("arbitrary", "arbitrary")),
        name="ffn_dense",
    )(xn, x1, mod3, w1, w3, w2)


MOE_BM = 1024
MOE_SUB = 256
MOE_NBLK = 2 * T // MOE_BM + N_EXP
MOE_NR = MOE_NBLK * MOE_BM
MOE_TF = 512
MOE_TN = 256
FIN_TM = 512


def _route_kernel(i1_ref, i2_ref, r1_ref, r2_ref, cnt_ref, w2_ref,
                  src_ref, eb_ref, rows_ref, nvb_ref, off_scr):
    def clear(r, carry):
        src_ref[r] = 0
        return carry

    lax.fori_loop(0, MOE_NR + 8, clear, 0, unroll=8)

    def clear_blk(b, carry):
        eb_ref[b] = 0
        rows_ref[b] = 0
        return carry

    lax.fori_loop(0, MOE_NBLK, clear_blk, 0)

    nblk = jnp.int32(0)
    for e in range(N_EXP):
        c = cnt_ref[e]
        nbe = (c + (MOE_BM - 1)) // MOE_BM
        off_scr[e] = nblk * MOE_BM

        def fill(j, carry, e=e, c=c, nblk=nblk):
            eb_ref[nblk + j] = e
            rows_ref[nblk + j] = jnp.minimum(c - j * MOE_BM, MOE_BM)
            return carry

        lax.fori_loop(0, nbe, fill, 0)
        nblk = nblk + nbe
    nvb_ref[0] = nblk

    def place(t, carry):
        src_ref[off_scr[i1_ref[t]] + r1_ref[t]] = t
        src_ref[jnp.where(w2_ref[t] > 0.0, off_scr[i2_ref[t]] + r2_ref[t], MOE_NR)] = t
        return carry

    lax.fori_loop(0, T, place, 0, unroll=4)


def _route_call(route, cnt):
    ints = [route[k].astype(jnp.int32) for k in range(4)]
    cnt8 = cnt[0, :N_EXP].astype(jnp.int32)
    smem = pl.BlockSpec(memory_space=pltpu.SMEM)
    outs = pl.pallas_call(
        _route_kernel,
        grid_spec=pltpu.PrefetchScalarGridSpec(
            num_scalar_prefetch=5, grid=(1,),
            in_specs=[smem],
            out_specs=[smem, smem, smem, smem],
            scratch_shapes=[pltpu.SMEM((N_EXP,), jnp.int32)]),
        out_shape=[jax.ShapeDtypeStruct((MOE_NR + 8,), jnp.int32),
                   jax.ShapeDtypeStruct((MOE_NBLK,), jnp.int32), jax.ShapeDtypeStruct((MOE_NBLK,), jnp.int32),
                   jax.ShapeDtypeStruct((1,), jnp.int32)],
        compiler_params=_cparams(("arbitrary",)),
        name="moe_route",
    )(*ints, cnt8, route[5])
    return tuple(outs) + (ints[0], route[4], route[5])


def _moe_gather_kernel(src_ref, rows_ref, x_ref, o_ref, rows_scr):
    k = pl.program_id(0)
    base = k * MOE_SUB
    used = rows_ref[k // (MOE_BM // MOE_SUB)] > (k % (MOE_BM // MOE_SUB)) * MOE_SUB

    @pl.when(used)
    def _():
        def body(i, carry):
            for j in range(8):
                r = i * 8 + j
                rows_scr[pl.ds(r, 1), :] = x_ref[pl.ds(src_ref[base + r], 1), :]
            return carry

        lax.fori_loop(0, MOE_SUB // 8, body, 0)
        o_ref[...] = rows_scr[...].astype(BF16)

    @pl.when(jnp.logical_not(used))
    def _():
        o_ref[...] = jnp.zeros(o_ref.shape, o_ref.dtype)


def _moe_gather_call(src, rows_b, xn):
    return pl.pallas_call(
        _moe_gather_kernel,
        grid_spec=pltpu.PrefetchScalarGridSpec(
            num_scalar_prefetch=2, grid=(MOE_NR // MOE_SUB,),
            in_specs=[pl.BlockSpec((T, D), lambda k, s, rw: (0, 0), pipeline_mode=pl.Buffered(1))],
            out_specs=pl.BlockSpec((MOE_SUB, D), lambda k, s, rw: (k, 0)),
            scratch_shapes=[pltpu.VMEM((MOE_SUB, D), F32)]),
        out_shape=jax.ShapeDtypeStruct((MOE_NR, D), BF16),
        compiler_params=_cparams(("arbitrary",)),
        name="moe_gather",
    )(src, rows_b, xn)


def _moe_first(b, eb_ref):
    return (b == 0) | (eb_ref[b] != eb_ref[jnp.maximum(b - 1, 0)])


def _moe_kernel(nf, eb_ref, rows_ref, nvb_ref, xs_ref, w1_ref, w3_ref, w2_ref, y_ref, h_scr, w1b, w3b, w2b):
    b = pl.program_id(0)
    s = pl.program_id(1)
    nsub = jnp.where(b < nvb_ref[0], (rows_ref[b] + MOE_SUB - 1) // MOE_SUB, 0)
    sizes = [k * MOE_SUB for k in range(1, MOE_BM // MOE_SUB + 1)]

    @pl.when((s < nf) & (nsub > 0) & _moe_first(b, eb_ref))
    def _():
        _keep_weights(s, w1_ref.at[0, 0], w3_ref.at[0, 0], w2_ref.at[0, 0], w1b, w3b, w2b, MOE_TN)

    @pl.when((s < nf) & (nsub > 0))
    def _():
        for k, m in enumerate(sizes, start=1):
            @pl.when(nsub == k)
            def _(m=m):
                h_scr[s, 0:m, :] = _hidden_tile(xs_ref[0:m, :], w1b.at[s], w3b.at[s])

    @pl.when((s >= nf) & (nsub > 0))
    def _():
        for k, m in enumerate(sizes, start=1):
            @pl.when(nsub == k)
            def _(m=m):
                y_ref[0:m, :] = _down_tile(h_scr, w2b.at[s - nf], m)
                if m < MOE_BM:
                    y_ref[m:MOE_BM, :] = jnp.zeros((MOE_BM - m, MOE_TN), F32)

    @pl.when((s >= nf) & (nsub == 0))
    def _():
        y_ref[...] = jnp.zeros(y_ref.shape, F32)


def _moe_call(xs, e_b, rows_b, nvb, w1, w3, w2, i_layer):
    nf = D_FFE // MOE_TF
    nn = D // MOE_TN

    def blk(b, nv):
        return jnp.minimum(b, nv[0] - 1)

    def tile(b, s, eb, nv):
        stream = (b < nv[0]) & _moe_first(b, eb)
        return jnp.where(stream, jnp.minimum(s, nf - 1), nf - 1)

    return pl.pallas_call(
        functools.partial(_moe_kernel, nf),
        grid_spec=pltpu.PrefetchScalarGridSpec(
            num_scalar_prefetch=3, grid=(MOE_NBLK, nf + nn),
            in_specs=[pl.BlockSpec((MOE_BM, D), lambda b, s, eb, rw, nv: (blk(b, nv), 0)),
                      pl.BlockSpec((1, 1, D, MOE_TF),
                                   lambda b, s, eb, rw, nv: (i_layer, eb[blk(b, nv)], 0, tile(b, s, eb, nv))),
                      pl.BlockSpec((1, 1, D, MOE_TF),
                                   lambda b, s, eb, rw, nv: (i_layer, eb[blk(b, nv)], 0, tile(b, s, eb, nv))),
                      pl.BlockSpec((1, 1, MOE_TF, D),
                                   lambda b, s, eb, rw, nv: (i_layer, eb[blk(b, nv)], tile(b, s, eb, nv), 0))],
            out_specs=pl.BlockSpec((MOE_BM, MOE_TN), lambda b, s, eb, rw, nv: (b, jnp.maximum(s - nf, 0))),
            scratch_shapes=[pltpu.VMEM((nf, MOE_BM, MOE_TF), BF16),
                            pltpu.VMEM((nf, D, MOE_TF), BF16), pltpu.VMEM((nf, D, MOE_TF), BF16),
                            pltpu.VMEM((nn, nf, MOE_TF, MOE_TN), BF16)]),
        out_shape=jax.ShapeDtypeStruct((MOE_NR, D), F32),
        compiler_params=_cparams(("arbitrary", "arbitrary"), vmem=56 * 1024 * 1024),
        name="moe_experts",
    )(e_b, rows_b, nvb, xs, w1, w3, w2)


def _moe_combine_kernel(src_ref, eb_ref, rows_ref, nvb_ref, i1_ref, w1_ref, w2_ref,
                        ys_ref, x1_ref, g2_ref, yp_ref, ys_out_ref, acc):
    i = pl.program_id(0)

    @pl.when(i == 0)
    def _():
        acc[...] = jnp.zeros(acc.shape, F32)

    @pl.when(i < nvb_ref[0])
    def _():
        base = i * MOE_BM
        n = rows_ref[i]
        e = eb_ref[i]

        def gate(t):
            return jnp.where(i1_ref[t] == e, w1_ref[t], w2_ref[t])

        def add_rows(r0, cnt):
            toks = [src_ref[base + r0 + j] for j in range(cnt)]
            vals = [acc[pl.ds(toks[j], 1), :] + gate(toks[j]) * ys_ref[pl.ds(r0 + j, 1), :]
                    for j in range(cnt)]
            for j in range(cnt):
                acc[pl.ds(toks[j], 1), :] = vals[j]

        def body4(q, carry):
            add_rows(q * 4, 4)
            return carry

        lax.fori_loop(0, n // 4, body4, 0)

        def body1(r, carry):
            add_rows(r, 1)
            return carry

        lax.fori_loop((n // 4) * 4, n, body1, 0)

    @pl.when(i >= MOE_NBLK)
    def _():
        t0 = pl.multiple_of((i - MOE_NBLK) * FIN_TM, FIN_TM)
        y = x1_ref[...] + g2_ref[0] * acc[pl.ds(t0, FIN_TM), :]

        @pl.when(i - MOE_NBLK < T_P // FIN_TM)
        def _():
            yp_ref[...] = y

        @pl.when(i - MOE_NBLK >= T_P // FIN_TM)
        def _():
            ys_out_ref[...] = y


def _moe_combine_call(src, e_b, rows_b, nvb, i1, w1, w2, ys, x1, mod3, l):
    nfin = T // FIN_TM
    smem = pl.BlockSpec(memory_space=pltpu.SMEM)

    def g2_idx(i, *_):
        j = jnp.maximum(i - MOE_NBLK, 0)
        r = jnp.where(j < T_P // FIN_TM, 0, 1 + (j - T_P // FIN_TM) // (S_S // FIN_TM))
        return ((l * 8 + r) * 6 + 5, 0, 0)

    return pl.pallas_call(
        _moe_combine_kernel,
        grid_spec=pltpu.PrefetchScalarGridSpec(
            num_scalar_prefetch=5, grid=(MOE_NBLK + nfin,),
            in_specs=[smem, smem,
                      pl.BlockSpec((MOE_BM, D),
                                   lambda i, s, eb, rw, nv, t1: (jnp.minimum(jnp.minimum(i, MOE_NBLK - 1), nv[0] - 1), 0)),
                      pl.BlockSpec((FIN_TM, D), lambda i, *_: (jnp.maximum(i - MOE_NBLK, 0), 0)),
                      pl.BlockSpec((1, 1, D), g2_idx)],
            out_specs=[pl.BlockSpec((FIN_TM, D),
                                    lambda i, *_: (jnp.clip(i - MOE_NBLK, 0, T_P // FIN_TM - 1), 0)),
                       pl.BlockSpec((FIN_TM, D),
                                    lambda i, *_: (jnp.maximum(i - MOE_NBLK - T_P // FIN_TM, 0), 0))],
            scratch_shapes=[pltpu.VMEM((T, D), F32)]),
        out_shape=[jax.ShapeDtypeStruct((T_P, D), F32), jax.ShapeDtypeStruct((T_S, D), F32)],
        compiler_params=_cparams(("arbitrary",), vmem=52 * 1024 * 1024),
        name="moe_combine",
    )(src, e_b, rows_b, nvb, i1, w1, w2, ys, x1, mod3)


def _moe_layer(xp, x1, route, cnt, mod3, l, w1, w3, w2, i_layer):
    src, e_b, rows_b, nvb, i1, g1, g2 = _route_call(route, cnt)
    xs = _moe_gather_call(src, rows_b, xp)
    ys = _moe_call(xs, e_b, rows_b, nvb, w1, w3, w2, i_layer)
    return _moe_combine_call(src, e_b, rows_b, nvb, i1, g1, g2, ys, x1, mod3, l)


def _pad_cols(w, n):
    return jnp.pad(w, ((0, 0), (0, n - w.shape[1])))


def _tail_selector():
    sel = np.zeros((432, 1024), np.float32)
    for d in range(2):
        for gate in range(2):
            for head in range(B_HEADS):
                sel[d * 8 + gate * 4 + head, gate * 256 + (head // 2) * LANES + d * 2 + head % 2] = 1.0
    for c in range(C_QRANK):
        sel[16 + c, 512 + c] = 1.0
    for c in range(C_KVRANK):
        sel[16 + C_QRANK + c, 768 + c] = 1.0
    for c in range(C_ROPE):
        sel[16 + C_QRANK + C_KVRANK + c, 896 + C_NOPE + c] = 1.0
    return sel


def _relayout_w_in(w):
    sel = jnp.asarray(_tail_selector(), BF16)
    tail = jnp.dot(w[:, IN_MAIN:].astype(BF16), sel, preferred_element_type=BF16)
    return jnp.concatenate([w[:, :IN_MAIN].astype(BF16), tail], axis=1)


def _relayout_gate_b(b):
    return jnp.dot(b[None, :], jnp.asarray(_tail_selector()[:16, :512]), precision=lax.Precision.HIGHEST)


def _pad_heads(w, width):
    r = w.shape[0]
    h = w.shape[1] // width
    return jnp.pad(w.reshape(r, h, width), ((0, 0), (0, 0), (0, LANES - width))).reshape(r, h * LANES)


def _rope_tables(half, span_start, period):
    rows = S_S // 64
    r = jnp.repeat(jnp.arange(rows), 64).astype(F32)
    c = jnp.tile(jnp.arange(64), rows).astype(F32)
    n_freq = half // 2
    freq = 10000.0 ** (-jnp.arange(n_freq, dtype=F32) / n_freq)
    ang = jnp.concatenate([r[:, None] * freq, c[:, None] * freq], axis=-1)
    cos, sin = jnp.cos(ang), jnp.sin(ang)
    d = (jnp.arange(LANES) - span_start) % period
    inside = d < 2 * half
    p = jnp.where(inside, d % half, 0)
    first = inside & (d < half)
    second = inside & (d >= half)
    ct = jnp.where(inside[None, :], cos[:, p], 1.0)
    sa = jnp.where(first[None, :], -sin[:, p], 0.0)
    sb = jnp.where(second[None, :], sin[:, p], 0.0)
    ident = (jnp.ones((TM, LANES), F32), jnp.zeros((TM, LANES), F32), jnp.zeros((TM, LANES), F32))
    return tuple(jnp.concatenate([i0, t], axis=0) for i0, t in zip(ident, (ct, sa, sb)))


def kernel(x_prompt, x_sample, c, cache_swa_k, cache_swa_v, cache_mla_ckv, cache_mla_krope, state_mlstm_C, state_mlstm_n, state_mlstm_m, c_ctx, ada_w, ada_b, norm1_g, norm2_g, w_in, a_qn_g, a_kn_g, a_sink, b_gate_b, b_hn_g, c_qa_g, c_kva_g, c_wuq, c_wukv, c_qn_g, c_kn_g, w_out, ffn_w1, ffn_w3, ffn_w2, moe_router, moe_w1, moe_w3, moe_w2):
    x_p, x_s, sample_off = x_prompt.reshape(T_P, D), x_sample.reshape(T_S, D), 0
    cv = jnp.concatenate([c_ctx[None, :], c, jnp.zeros((5, D), F32)], axis=0)
    mod = _ada_call(cv, ada_w, ada_b)
    mod3 = mod.reshape(DEPTH * 8 * 6, 1, D)

    rope_a = _rope_tables(32, 0, HD)
    rope_c = _rope_tables(16, C_NOPE, LANES)

    cache_k4 = cache_swa_k.reshape(NB_S, DEPTH, PAST, A_KV * HD)
    cache_v4 = cache_swa_v.reshape(NB_S, DEPTH, PAST, A_KV * HD)

    news = []
    for l in range(DEPTH):
        w_in_p = _relayout_w_in(w_in[l])[None]
        wuq_p = _pad_heads(c_wuq[l], C_QK).astype(BF16)[None]
        wukv = c_wukv[l].reshape(C_KVRANK, C_HEADS, C_NOPE + C_V)
        wk_p = jnp.pad(wukv[..., :C_NOPE], ((0, 0), (0, 0), (0, LANES - C_NOPE)))
        wk_p = wk_p.reshape(1, C_KVRANK, C_HEADS * LANES).astype(BF16)
        wv_p = wukv[..., C_NOPE:].reshape(1, C_KVRANK, C_HEADS * C_V).astype(BF16)
        w_out_b = w_out[l].astype(BF16)[None]
        gqa = jnp.tile(a_qn_g[l], A_HEADS)[None, :]
        gka = jnp.tile(a_kn_g[l], A_KV)[None, :]
        gb = _relayout_gate_b(b_gate_b[l])
        gqn = jnp.tile(jnp.pad(c_qn_g[l], (0, LANES - C_QK)), C_HEADS)[None, :]
        gkn = jnp.tile(jnp.pad(c_kn_g[l], (0, LANES - C_QK)), C_HEADS)[None, :]
        (qa, ka, va, kaf, vaf, bq, bk, bv, bo, li, lf, qc, ckv_n, ckr) = _in_call(
            x_p, x_s, sample_off, mod3, l, norm1_g[l][None, :], w_in_p, gqa, gka, gb, c_qa_g[l][None, :],
            c_kva_g[l][None, :], wuq_p, gqn, rope_a, rope_c)

        ckr_cache = jnp.pad(cache_mla_krope[:, l].reshape(NB_S * PAST, C_ROPE),
                            ((0, 0), (C_NOPE, LANES - C_NOPE - C_ROPE)))
        kc, vc = _mlakv_call(ckv_n, cache_mla_ckv[:, l].reshape(NB_S * PAST, C_KVRANK), ckr, ckr_cache,
                             l, wk_p, wv_p, gkn, rope_c)

        sink = a_sink[l]
        oa_p = _attn_a_prompt_call(sink, qa, ka, va)
        oa_s = _attn_a_sample_call(sink, qa, ka, va, cache_k4, cache_v4, l)
        oc_p = _mla_prompt_call(qc, kc, vc)
        oc_s = _mla_sample_call(qc, kc, vc)

        hn2 = jnp.tile(b_hn_g[l], 2)[None, :]
        ob_p, cn, nn, mn = _mlstm_call(bq, bk, bv, bo, li, lf, hn2, None, l)
        c_st = state_mlstm_C[:, l].reshape(NB_S, 2, 2, 2, HD, HD)
        c_st = jnp.transpose(c_st, (0, 2, 1, 3, 4, 5))
        c0 = jnp.zeros((NB_S, 2, 2, LANES, LANES), F32)
        c0 = c0.at[..., :HD, :HD].set(c_st[:, :, :, 0]).at[..., HD:, HD:].set(c_st[:, :, :, 1])
        c0 = c0.reshape(NB_S * 2, 2, LANES, LANES)
        n_st = state_mlstm_n[:, l].reshape(NB_S, 2, 2, 1, LANES)
        n0 = jnp.transpose(n_st, (0, 2, 1, 3, 4)).reshape(NB_S * 2, 2, 1, LANES)
        m_st = state_mlstm_m[:, l].reshape(NB_S, 2, 2, 2)
        m0 = jnp.transpose(m_st, (0, 2, 1, 3)).reshape(NB_S * 2, 1, 4)
        m0 = jnp.pad(m0, ((0, 0), (0, 0), (0, LANES - 4)))
        ob_s = _mlstm_call(bq, bk, bv, bo, li, lf, hn2, (c0, n0, m0), l)

        oa, ob, oc = (oa_p, oa_s), (ob_p, ob_s), (oc_p, oc_s)

        if l % 2 == 0:
            x1, xn = _out_call(x_p, x_s, sample_off, oa, ob, oc, w_out_b, mod3, l, norm2_g[l][None, :], None)
            x = _ffn_call(xn, x1, mod3, l, ffn_w1, ffn_w3, ffn_w2, l // 2)
            x_p, x_s, sample_off = x, x, NT_P
        else:
            r = _pad_cols(moe_router[l // 2], LANES)
            rh = r.astype(BF16)
            rl = (r - rh.astype(F32)).astype(BF16)
            x1, xn, route, cnt = _out_call(x_p, x_s, sample_off, oa, ob, oc, w_out_b, mod3, l,
                                           norm2_g[l][None, :], (rh[None], rl[None]))
            x_p, x_s = _moe_layer(xn, x1, route, cnt, mod3, l, moe_w1, moe_w3, moe_w2, l // 2)
            sample_off = 0

        new_k = kaf.reshape(NB_P, S_P, A_KV, HD)
        new_v = vaf.reshape(NB_P, S_P, A_KV, HD)
        new_ckv = ckv_n[:T_P].reshape(NB_P, S_P, C_KVRANK)
        new_kr = ckr[:T_P, C_NOPE:C_NOPE + C_ROPE].reshape(NB_P, S_P, C_ROPE)
        new_c, new_n = cn, nn
        mn4 = mn.reshape(NB_P, 2, LANES)[:, :, :4].reshape(NB_P, 2, 2, 2)
        new_m = jnp.transpose(mn4, (0, 2, 1, 3)).reshape(NB_P, 2, B_HEADS)
        news.append((new_k, new_v, new_ckv, new_kr, new_c, new_n, new_m))

    y_prompt = x_p[:T_P].reshape(NB_P, S_P, D)
    y_sample = x_s[sample_off * TM:sample_off * TM + T_S].reshape(NB_S, S_S, D)
    stacked = tuple(jnp.stack([nw[j] for nw in news], axis=1) for j in range(7))
    return (y_prompt, y_sample) + stacked
```

```python
import functools

import jax
import jax.numpy as jnp
import numpy as np
from jax import lax
from jax.experimental import pallas as pl
from jax.experimental.pallas import tpu as pltpu

F32 = jnp.float32
BF16 = jnp.bfloat16

D = 1024
NB_P, S_P = 16, 256
NB_S, S_S = 2, 1024
PAST = 512
DEPTH = 2
T_P = NB_P * S_P
T_S = NB_S * S_S
T = T_P + T_S
TM = 512
NT = T // TM
NT_P = T_P // TM
HD = 64
A_HEADS, A_KV = 6, 2
B_HEADS = 4
CHUNK = 64
C_HEADS = 6
C_QRANK, C_KVRANK, C_NOPE, C_ROPE, C_V = 256, 128, 64, 32, 64
C_QK = C_NOPE + C_ROPE
D_FF = 2816
N_EXP = 8
D_FFE = 3584
EPS = 1e-6
NEG = -1e30
LANES = 128
VMEM_LIMIT = 48 * 1024 * 1024

IN_MAIN = 1664
SEG = dict(QA=(0, 384), KA=(384, 128), VA=(512, 128), BQ=(640, 256), BK=(896, 256), BV=(1152, 256),
           BO=(1408, 256), GI=(1664, 256), GF=(1920, 256), CQ=(2176, 256), CKV=(2432, 128), CKR=(2560, 128))
NP_IN = 2688


def _cparams(sem, vmem=VMEM_LIMIT):
    return pltpu.CompilerParams(dimension_semantics=sem, vmem_limit_bytes=vmem)


def _dot(a, b):
    return jnp.dot(a, b, preferred_element_type=F32)


def _dot_nt(a, b):
    return lax.dot_general(a, b, (((1,), (1,)), ((), ())), preferred_element_type=F32)


def _lane(shape):
    return lax.broadcasted_iota(jnp.int32, shape, len(shape) - 1)


def _mod_row(i):
    return jnp.where(i < NT_P, 0, 1 + (i - NT_P) // (S_S // TM))


def _pair_specs(n, sample_off):
    return [pl.BlockSpec((TM, n), lambda i: (jnp.minimum(i, NT_P - 1), 0)),
            pl.BlockSpec((TM, n), lambda i: (sample_off + jnp.maximum(i - NT_P, 0), 0))]


def _pick(i, p_ref, s_ref):
    return jnp.where(i < NT_P, p_ref[...], s_ref[...])


def _rope_blk(i):
    return jnp.where((i >= NT_P) & (i < NT), 1 + (i - NT_P) % (S_S // TM), 0)


def _ada_kernel(cv_ref, w_ref, b_ref, o_ref):
    s = cv_ref[...]
    s = s * jax.nn.sigmoid(s)
    o_ref[0] = _dot(s.astype(BF16), w_ref[0].astype(BF16)) + b_ref[0]


def _ada_call(cv, ada_w, ada_b):
    tn = 1536
    return pl.pallas_call(
        _ada_kernel,
        grid=(DEPTH, 6 * D // tn),
        in_specs=[pl.BlockSpec((8, D), lambda l, j: (0, 0)),
                  pl.BlockSpec((1, D, tn), lambda l, j: (l, 0, j)),
                  pl.BlockSpec((1, 1, tn), lambda l, j: (l, 0, j))],
        out_specs=pl.BlockSpec((1, 8, tn), lambda l, j: (l, 0, j)),
        out_shape=jax.ShapeDtypeStruct((DEPTH, 8, 6 * D), F32),
        compiler_params=_cparams(("arbitrary", "arbitrary")),
        name="ada_mod",
    )(cv, ada_w, ada_b.reshape(DEPTH, 1, 6 * D))


def _half_norm(x, g, n):
    outs = []
    for j in range(x.shape[1] // LANES):
        xj = x[:, j * LANES:(j + 1) * LANES]
        lo = _lane(xj.shape) < HD
        s = xj * xj
        s_lo = jnp.sum(jnp.where(lo, s, 0.0), axis=-1, keepdims=True)
        s_hi = jnp.sum(jnp.where(lo, 0.0, s), axis=-1, keepdims=True)
        r = lax.rsqrt(jnp.where(lo, s_lo, s_hi) * (1.0 / n) + EPS)
        outs.append(xj * r)
    return jnp.concatenate(outs, axis=-1) * g


def _group_norm(x, g, n):
    outs = []
    for j in range(x.shape[1] // LANES):
        xj = x[:, j * LANES:(j + 1) * LANES]
        r = lax.rsqrt(jnp.sum(xj * xj, axis=-1, keepdims=True) * (1.0 / n) + EPS)
        outs.append(xj * r)
    return jnp.concatenate(outs, axis=-1) * g


def _rope(x, c, sa, sb, shift):
    outs = []
    for j in range(x.shape[1] // LANES):
        xj = x[:, j * LANES:(j + 1) * LANES]
        outs.append(xj * c + pltpu.roll(xj, LANES - shift, 1) * sa + pltpu.roll(xj, shift, 1) * sb)
    return jnp.concatenate(outs, axis=-1)


def _in_kernel(xp_ref, xs_ref, sh_ref, sc_ref, n1_ref, w_ref, wt_ref, gqa_ref, gka_ref, gb_ref, gcq_ref, gckv_ref,
               wuq_ref, gqn_ref, ra_c, ra_a, ra_b, rc_c, rc_a, rc_b,
               qa_o, ka_o, va_o, kaf_o, vaf_o, bq_o, bk_o, bv_o, bo_o, li_o, lf_o, qc_o, ckv_o, ckr_o, wb_scr):
    i = pl.program_id(0)

    @pl.when(i == 0)
    def _():
        wb_scr[...] = w_ref[0].astype(BF16)

    x = _pick(i, xp_ref, xs_ref)
    xn = x * lax.rsqrt(jnp.mean(x * x, axis=-1, keepdims=True) + EPS) * n1_ref[...]
    xn = xn * (1.0 + sc_ref[0]) + sh_ref[0]
    xb = xn.astype(BF16)
    p_main = _dot(xb, wb_scr[...])
    p_tail = _dot(xb, wt_ref[...])

    def seg(name):
        o, n = SEG[name]
        return p_main[:, o:o + n] if o < IN_MAIN else p_tail[:, o - IN_MAIN:o - IN_MAIN + n]

    qa = _half_norm(seg("QA"), gqa_ref[...], HD)
    ka = _half_norm(seg("KA"), gka_ref[...], HD)
    va = seg("VA")

    @pl.when(i < NT_P)
    def _():
        kaf_o[...] = ka
        vaf_o[...] = va

    def dup(x):
        lo, hi = _half(x, False), _half(x, True)
        return jnp.concatenate([lo + pltpu.roll(lo, HD, 1), hi + pltpu.roll(hi, HD, 1)], axis=-1)

    va_o[...] = dup(va).astype(BF16)

    bq_o[...] = seg("BQ").astype(BF16)
    bk_o[...] = (seg("BK") * (HD ** -0.5)).astype(BF16)
    bv_o[...] = seg("BV").astype(BF16)
    bo_o[...] = seg("BO")
    gb = gb_ref[...]
    li_o[...] = seg("GI") + gb[:, 0:256]
    lf_o[...] = jax.nn.log_sigmoid(seg("GF") + gb[:, 256:512])

    cq = seg("CQ")
    cqn = cq * lax.rsqrt(jnp.mean(cq * cq, axis=-1, keepdims=True) + EPS) * gcq_ref[...]
    qc = _group_norm(_dot(cqn.astype(BF16), wuq_ref[0]), gqn_ref[...], C_QK)

    @pl.when(i < NT_P)
    def _():
        qa_o[...] = qa.astype(BF16)
        ka_o[...] = dup(ka).astype(BF16)
        qc_o[...] = qc.astype(BF16)

    @pl.when(i >= NT_P)
    def _():
        qa_o[...] = _rope(qa, ra_c[...], ra_a[...], ra_b[...], 32).astype(BF16)
        ka_o[...] = dup(_rope(ka, ra_c[...], ra_a[...], ra_b[...], 32)).astype(BF16)
        qc_o[...] = _rope(qc, rc_c[...], rc_a[...], rc_b[...], 16).astype(BF16)

    ckv = seg("CKV")
    ckv_o[...] = ckv * lax.rsqrt(jnp.mean(ckv * ckv, axis=-1, keepdims=True) + EPS) * gckv_ref[...]
    ckr_o[...] = seg("CKR")


def _in_call(x_p, x_s, sample_off, mod3, l, n1g, w_in, w_tail, gqa, gka, gb, gcq, gckv, wuq_p, gqn, rope_a, rope_c):
    row = lambda k: pl.BlockSpec((1, 1, D), lambda i: ((l * 8 + _mod_row(i)) * 6 + k, 0, 0))
    vec = lambda n: pl.BlockSpec((1, n), lambda i: (0, 0))
    tab = pl.BlockSpec((TM, LANES), lambda i: (_rope_blk(i), 0))
    tok = lambda n: pl.BlockSpec((TM, n), lambda i: (i, 0))
    tokp = pl.BlockSpec((TM, LANES), lambda i: (jnp.minimum(i, NT_P - 1), 0))
    o = lambda n, dt: jax.ShapeDtypeStruct((T, n), dt)
    return pl.pallas_call(
        _in_kernel,
        grid=(NT,),
        in_specs=_pair_specs(D, sample_off) + [row(0), row(1), vec(D),
                  pl.BlockSpec((1, D, IN_MAIN), lambda i: (l, 0, 0), pipeline_mode=pl.Buffered(1)),
                  pl.BlockSpec((D, NP_IN - IN_MAIN), lambda i: (0, 0)),
                  vec(384), vec(128), vec(512), vec(256), vec(128),
                  pl.BlockSpec((1, C_QRANK, 768), lambda i: (0, 0, 0)), vec(768),
                  tab, tab, tab, tab, tab, tab],
        out_specs=[tok(384), tok(256), tok(256), tokp, tokp, tok(256), tok(256), tok(256), tok(256),
                   tok(256), tok(256), tok(768), tok(128), tok(128)],
        out_shape=[o(384, BF16), o(256, BF16), o(256, BF16),
                   jax.ShapeDtypeStruct((T_P, LANES), F32), jax.ShapeDtypeStruct((T_P, LANES), F32),
                   o(256, BF16), o(256, BF16), o(256, BF16), o(256, F32),
                   o(256, F32), o(256, F32), o(768, BF16), o(128, F32), o(128, F32)],
        scratch_shapes=[pltpu.VMEM((D, IN_MAIN), BF16)],
        compiler_params=_cparams(("arbitrary",)),
        name=f"in_proj_l{l}",
    )(x_p, x_s, mod3, mod3, n1g, w_in, w_tail, gqa, gka, gb, gcq, gckv, wuq_p, gqn, *rope_a, *rope_c)


def _mlakv_kernel(ckv_ref, ckvc_ref, ckr_ref, ckrc_ref, wk_ref, wv_ref, gkn_ref, rc_c, rc_a, rc_b, k_o, v_o):
    i = pl.program_id(0)
    tok = i < NT
    c = jnp.where(tok, ckv_ref[...], ckvc_ref[...]).astype(BF16)
    k = _dot(c, wk_ref[0])
    kr = jnp.where(tok, ckr_ref[...], ckrc_ref[...])
    k = k + jnp.concatenate([kr] * C_HEADS, axis=-1)
    k = _group_norm(k, gkn_ref[...], C_QK)
    sample = (i >= NT_P) & tok

    @pl.when(sample)
    def _():
        k_o[...] = _rope(k, rc_c[...], rc_a[...], rc_b[...], 16).astype(BF16)

    @pl.when(jnp.logical_not(sample))
    def _():
        k_o[...] = k.astype(BF16)

    v_o[...] = _dot(c, wv_ref[0]).astype(BF16)


def _mlakv_call(ckv_n, ckv_cache, ckr, ckr_cache, l, wk_p, wv_p, gkn, rope_c):
    r = T + NB_S * PAST
    tab = pl.BlockSpec((TM, LANES), lambda i: (_rope_blk(i), 0))
    tok = lambda n: pl.BlockSpec((TM, n), lambda i: (i, 0))
    tokens = pl.BlockSpec((TM, LANES), lambda i: (jnp.minimum(i, NT - 1), 0))
    cached = pl.BlockSpec((TM, LANES), lambda i: (jnp.maximum(i - NT, 0), 0))
    return pl.pallas_call(
        _mlakv_kernel,
        grid=(r // TM,),
        in_specs=[tokens, cached, tokens, cached,
                  pl.BlockSpec((1, C_KVRANK, 768), lambda i: (0, 0, 0)),
                  pl.BlockSpec((1, C_KVRANK, 384), lambda i: (0, 0, 0)),
                  pl.BlockSpec((1, 768), lambda i: (0, 0)), tab, tab, tab],
        out_specs=[tok(768), tok(384)],
        out_shape=[jax.ShapeDtypeStruct((r, 768), BF16), jax.ShapeDtypeStruct((r, 384), BF16)],
        compiler_params=_cparams(("arbitrary",)),
        name=f"mla_kv_l{l}",
    )(ckv_n, ckv_cache, ckr, ckr_cache, wk_p, wv_p, gkn, *rope_c)


def _bmm(a, b):
    return lax.dot_general(a, b, (((2,), (1,)), ((0,), (0,))), preferred_element_type=F32)


def _bmm_nt(a, b):
    return lax.dot_general(a, b, (((2,), (2,)), ((0,), (0,))), preferred_element_type=F32)


def _bmm_tn(a, b):
    return lax.dot_general(a, b, (((1,), (1,)), ((0,), (0,))), preferred_element_type=F32)


def _softmax_pv(scores, values, sink):
    batched = scores[0].ndim == 3
    heads = range(scores[0].shape[0]) if batched else ()
    m = scores[0].max(axis=-1, keepdims=True)
    for s in scores[1:]:
        m = jnp.maximum(m, s.max(axis=-1, keepdims=True))
    if sink is not None:
        m = jnp.stack([jnp.maximum(m[h], sink[h]) for h in heads]) if batched else jnp.maximum(m, sink)
    es = [jnp.exp(s - m) for s in scores]
    den = es[0].sum(axis=-1, keepdims=True)
    for e in es[1:]:
        den = den + e.sum(axis=-1, keepdims=True)
    if sink is not None:
        den = (jnp.stack([den[h] + jnp.exp(sink[h] - m[h]) for h in heads]) if batched
               else den + jnp.exp(sink - m))
    inv = 1.0 / den
    out = None
    for e, v in zip(es, values):
        o = (_bmm if batched else _dot)((e * inv).astype(BF16), v)
        out = o if out is None else out + o
    return out


def _scores(q3, k3, scale):
    return jnp.stack([_dot_nt(q3[h], k3[h]) for h in range(q3.shape[0])]) * scale


def _half(x, hi):
    lo = _lane(x.shape) < HD
    return jnp.where(lo != hi, x, jnp.zeros_like(x))


def _pair_sum_store(o3, o_ref):
    for j in range(o3.shape[0] // 2):
        o_ref[:, j * LANES:(j + 1) * LANES] = (o3[2 * j] + o3[2 * j + 1]).astype(BF16)


def _a_heads(q, kdups, vdups):
    q3 = jnp.stack([_half(q[:, (h // 2) * LANES:(h // 2 + 1) * LANES], h % 2 == 1) for h in range(A_HEADS)])
    k3 = [kdups[h // (A_HEADS // A_KV)] for h in range(A_HEADS)]
    v3 = jnp.stack([_half(vdups[h // (A_HEADS // A_KV)], h % 2 == 1) for h in range(A_HEADS)])
    return q3, k3, v3


def _sink3(sink_ref):
    return [sink_ref[h] for h in range(A_HEADS)]


def _attn_a_prompt_kernel(sink_ref, q_ref, k_ref, v_ref, o_ref):
    q = q_ref[...]
    k = k_ref[...]
    v = v_ref[...]
    for j in range(A_HEADS // 2):
        acc = None
        for c in range(2):
            h = 2 * j + c
            g = h // (A_HEADS // A_KV)
            qh = _half(q[:, j * LANES:(j + 1) * LANES], c == 1)
            s = _dot_nt(qh, k[:, g * LANES:(g + 1) * LANES]) * (HD ** -0.5)
            vh = _half(v[:, g * LANES:(g + 1) * LANES], c == 1)
            o = _softmax_pv([s], [vh], sink_ref[h])
            acc = o if acc is None else acc + o
        o_ref[:, j * LANES:(j + 1) * LANES] = acc.astype(BF16)


def _attn_a_prompt_call(sink, qa, ka, va):
    return pl.pallas_call(
        _attn_a_prompt_kernel,
        grid_spec=pltpu.PrefetchScalarGridSpec(
            num_scalar_prefetch=1, grid=(NB_P,),
            in_specs=[pl.BlockSpec((S_P, 384), lambda b, s: (b, 0)),
                      pl.BlockSpec((S_P, 256), lambda b, s: (b, 0)),
                      pl.BlockSpec((S_P, 256), lambda b, s: (b, 0))],
            out_specs=pl.BlockSpec((S_P, 384), lambda b, s: (b, 0))),
        out_shape=jax.ShapeDtypeStruct((T_P, 384), BF16),
        compiler_params=_cparams(("arbitrary",)),
        name="attn_a_prompt",
    )(sink, qa, ka, va)


QB = 128
WIN = 128
BAND = QB + 2 * WIN


def _attn_a_sample_kernel(sink_ref, q_ref, k_ref, v_ref, kc_ref, vc_ref, o_ref):
    n = pl.program_id(1)
    ws = pl.multiple_of(jnp.clip(n * QB - WIN, 0, S_S - BAND), QB)
    kb = k_ref[pl.ds(ws, BAND), :]
    vb = v_ref[pl.ds(ws, BAND), :]
    qpos = n * QB + lax.broadcasted_iota(jnp.int32, (QB, BAND), 0)
    kpos = ws + lax.broadcasted_iota(jnp.int32, (QB, BAND), 1)
    ok = (jnp.abs(qpos - kpos) <= WIN)[None]
    groups = range(A_KV)
    q3, k3, v3 = _a_heads(q_ref[...], [kb[:, g * LANES:(g + 1) * LANES] for g in groups],
                          [vb[:, g * LANES:(g + 1) * LANES] for g in groups])

    def dup(x, g):
        xg = _half(x, g == 1)
        return (xg + pltpu.roll(xg, HD, 1)).astype(BF16)

    kc = kc_ref[0, 0]
    vc = vc_ref[0, 0]
    _, kc3, vc3 = _a_heads(q_ref[...], [dup(kc, g) for g in groups], [dup(vc, g) for g in groups])
    s_b = jnp.where(ok, _scores(q3, k3, HD ** -0.5), NEG)
    s_c = _scores(q3, kc3, HD ** -0.5)
    _pair_sum_store(_softmax_pv([s_b, s_c], [v3, vc3], _sink3(sink_ref)), o_ref)


def _attn_a_sample_call(sink, qa, ka, va, cache_k, cache_v, l):
    nqb = S_S // QB
    off = T_P // S_S
    return pl.pallas_call(
        _attn_a_sample_kernel,
        grid_spec=pltpu.PrefetchScalarGridSpec(
            num_scalar_prefetch=1, grid=(NB_S, nqb),
            in_specs=[pl.BlockSpec((QB, 384), lambda b, n, s: (T_P // QB + b * nqb + n, 0)),
                      pl.BlockSpec((S_S, 256), lambda b, n, s: (off + b, 0)),
                      pl.BlockSpec((S_S, 256), lambda b, n, s: (off + b, 0)),
                      pl.BlockSpec((1, 1, PAST, LANES), lambda b, n, s: (b, l, 0, 0)),
                      pl.BlockSpec((1, 1, PAST, LANES), lambda b, n, s: (b, l, 0, 0))],
            out_specs=pl.BlockSpec((QB, 384), lambda b, n, s: (b * nqb + n, 0))),
        out_shape=jax.ShapeDtypeStruct((T_S, 384), BF16),
        compiler_params=_cparams(("arbitrary", "arbitrary")),
        name=f"attn_a_sample_l{l}",
    )(sink, qa, ka, va, cache_k, cache_v)


def _mla_heads(q, ks, vs, o_ref):
    heads = range(C_HEADS)
    q3 = jnp.stack([q[:, h * LANES:(h + 1) * LANES] for h in heads])
    scores = [_scores(q3, [k[:, h * LANES:(h + 1) * LANES] for h in heads], C_QK ** -0.5) for k in ks]
    vals = [jnp.stack([_half(v[:, (h // 2) * LANES:(h // 2 + 1) * LANES], h % 2 == 1) for h in heads])
            for v in vs]
    _pair_sum_store(_softmax_pv(scores, vals, None), o_ref)


def _mla_prompt_kernel(q_ref, k_ref, v_ref, o_ref):
    _mla_heads(q_ref[...], [k_ref[...]], [v_ref[...]], o_ref)


def _mla_prompt_call(qc, kc, vc):
    return pl.pallas_call(
        _mla_prompt_kernel,
        grid=(NB_P,),
        in_specs=[pl.BlockSpec((S_P, 768), lambda b: (b, 0)),
                  pl.BlockSpec((S_P, 768), lambda b: (b, 0)),
                  pl.BlockSpec((S_P, 384), lambda b: (b, 0))],
        out_specs=pl.BlockSpec((S_P, 384), lambda b: (b, 0)),
        out_shape=jax.ShapeDtypeStruct((T_P, 384), BF16),
        compiler_params=_cparams(("arbitrary",)),
        name="mla_prompt",
    )(qc, kc, vc)


def _mla_sample_kernel(q_ref, kc_ref, vc_ref, kl_ref, vl_ref, o_ref):
    _mla_heads(q_ref[...], [kc_ref[...], kl_ref[...]], [vc_ref[...], vl_ref[...]], o_ref)


def _mla_sample_call(qc, kc, vc):
    tq = 256
    nq = S_S // tq
    return pl.pallas_call(
        _mla_sample_kernel,
        grid=(NB_S, nq),
        in_specs=[pl.BlockSpec((tq, 768), lambda b, n: (T_P // tq + b * nq + n, 0)),
                  pl.BlockSpec((PAST, 768), lambda b, n: (T // PAST + b, 0)),
                  pl.BlockSpec((PAST, 384), lambda b, n: (T // PAST + b, 0)),
                  pl.BlockSpec((S_S, 768), lambda b, n: (T_P // S_S + b, 0)),
                  pl.BlockSpec((S_S, 384), lambda b, n: (T_P // S_S + b, 0))],
        out_specs=pl.BlockSpec((tq, 384), lambda b, n: (b * nq + n, 0)),
        out_shape=jax.ShapeDtypeStruct((T_S, 384), BF16),
        compiler_params=_cparams(("arbitrary", "arbitrary")),
        name="mla_sample",
    )(qc, kc, vc, kc, vc)


def _split3(x):
    x1 = x.astype(BF16)
    r = x - x1.astype(F32)
    x2 = r.astype(BF16)
    x3 = (r - x2.astype(F32)).astype(BF16)
    return x1, x2, x3


def _mlstm_kernel(nc, has_state, *refs):
    for j in range(B_HEADS // 2):
        _mlstm_pair(nc, has_state, j, *refs)


def _mlstm_pair(nc, has_state, j, *refs):
    if has_state:
        (q_ref, k_ref, v_ref, bo_ref, li_ref, lf_ref, hn_ref, c0_ref, n0_ref, m0_ref, ob_ref) = refs
    else:
        (q_ref, k_ref, v_ref, bo_ref, li_ref, lf_ref, hn_ref, ob_ref, cn_ref, nn_ref, mn_ref) = refs
    s_len = nc * CHUNK
    pair = slice(j * LANES, (j + 1) * LANES)
    c3 = lambda x: x.reshape(nc, CHUNK, x.shape[-1])
    li3 = c3(li_ref[:, pair])
    lf3 = c3(lf_ref[:, pair])
    row = lax.broadcasted_iota(jnp.int32, (CHUNK, CHUNK), 0)
    colv = lax.broadcasted_iota(jnp.int32, (CHUNK, CHUNK), 1)
    tri_f = colv <= row
    tri_b = colv >= row
    bcast = lambda m: jnp.broadcast_to(m[None], (nc, CHUNK, CHUNK))
    lane1 = _lane((1, LANES))
    fwd_lane = lane1 < 2

    lf_parts = _split3(lf3)
    tf3 = bcast(tri_f.astype(BF16))
    tb3 = bcast(tri_b.astype(BF16))
    bc_f = _bmm(tf3, lf_parts[0]) + _bmm(tf3, lf_parts[1]) + _bmm(tf3, lf_parts[2])
    bc_b = _bmm(tb3, lf_parts[0]) + _bmm(tb3, lf_parts[1]) + _bmm(tb3, lf_parts[2])
    bc3 = jnp.where(fwd_lane, bc_f, bc_b)
    row3 = lax.broadcasted_iota(jnp.int32, (nc, CHUNK, LANES), 1)
    edge = jnp.where(row3 == jnp.where(fwd_lane, CHUNK - 1, 0), bc3, 0.0)
    bl3 = jnp.sum(edge, axis=1, keepdims=True)
    gg3 = bl3 - bc3 + li3
    bl2 = jnp.sum(edge, axis=1)
    mg2 = gg3.max(axis=1)

    m0 = m0_ref[j] if has_state else jnp.zeros((1, LANES), F32)
    mf = m0
    mb = m0
    mf_prev, mf_next, mb_prev, mb_next = {}, {}, {}, {}
    for i in range(nc):
        mf_prev[i] = mf
        mf = jnp.maximum(bl2[i:i + 1] + mf, mg2[i:i + 1])
        mf_next[i] = mf
        cb = nc - 1 - i
        mb_prev[cb] = mb
        mb = jnp.maximum(bl2[cb:cb + 1] + mb, mg2[cb:cb + 1])
        mb_next[cb] = mb
    m_prev = [jnp.where(fwd_lane, mf_prev[c], mb_prev[c]) for c in range(nc)]
    m_next = [jnp.where(fwd_lane, mf_next[c], mb_next[c]) for c in range(nc)]
    m_prev3 = jnp.stack(m_prev)
    m_next3 = jnp.stack(m_next)
    dec2d = jnp.exp(bl2 + jnp.concatenate(m_prev, axis=0) - jnp.concatenate(m_next, axis=0))
    if not has_state:
        mn_ref[j] = jnp.where(fwd_lane, mf, mb)

    u_parts = _split3(li3 - bc3)
    ws3 = jnp.exp(gg3 - m_next3)
    inter3 = bc3 + m_prev3

    q3 = c3(q_ref[:, pair])
    k3 = c3(k_ref[:, pair])
    v3 = c3(v_ref[:, pair])
    q3f = q3.astype(F32)
    k3f = k3.astype(F32)
    lane3 = _lane((nc, CHUNK, LANES))
    lo3 = lane3 < HD
    rr = lax.broadcasted_iota(jnp.int32, (LANES, LANES), 0)
    cc = lax.broadcasted_iota(jnp.int32, (LANES, LANES), 1)
    blockdiag = (rr < HD) == (cc < HD)
    col = lambda x, k: x[:, :, k:k + 1]

    hsum = None
    for d in range(2):
        causal3 = bcast(tri_f if d == 0 else tri_b)
        a_c, ws_sum, mt_c, ws_c, intra = [], [], [], [], None
        for e in range(2):
            kk = 2 * d + e
            pick = jnp.broadcast_to(jnp.where(_lane((CHUNK, LANES)) == kk, 1.0, 0.0).astype(BF16)[None],
                                    (nc, CHUNK, LANES))
            ub = _bmm_nt(pick, u_parts[0]) + _bmm_nt(pick, u_parts[1]) + _bmm_nt(pick, u_parts[2])
            bc_col = col(bc3, kk)
            d_mat = jnp.where(causal3, bc_col + ub, NEG)
            inter = col(inter3, kk)
            mt = jnp.maximum(inter, d_mat.max(axis=-1, keepdims=True))
            a = jnp.exp(inter - mt)
            w = jnp.exp(d_mat - mt) * _bmm_nt(_half(q3, e == 1), k3)
            o = _bmm(w.astype(BF16), _half(v3, e == 1))
            intra = o if intra is None else intra + o
            a_c.append(a)
            mt_c.append(mt)
            ws_sum.append(w.sum(axis=-1, keepdims=True))
            ws_c.append(col(ws3, kk))
        kw3 = k3f * jnp.where(lo3, ws_c[0], ws_c[1])
        dec2 = jnp.where(_lane((nc, LANES)) < HD, dec2d[:, 2 * d:2 * d + 1], dec2d[:, 2 * d + 1:2 * d + 2])
        u_all = jnp.where(blockdiag, _bmm_tn(kw3.astype(BF16), v3), 0.0)
        kwsum = kw3.sum(axis=1)
        if has_state:
            zero = jnp.zeros((HD, HD), F32)
            cst = jnp.concatenate([jnp.concatenate([c0_ref[0, d, 2 * j], zero], axis=1),
                                   jnp.concatenate([zero, c0_ref[0, d, 2 * j + 1]], axis=1)], axis=0)
            nst = jnp.concatenate([n0_ref[0, d, 2 * j:2 * j + 1, :], n0_ref[0, d, 2 * j + 1:2 * j + 2, :]],
                                  axis=1)
        else:
            cst = jnp.zeros((LANES, LANES), F32)
            nst = jnp.zeros((1, LANES), F32)
        cs, ns = [None] * nc, [None] * nc
        for c in (range(nc) if d == 0 else range(nc - 1, -1, -1)):
            cs[c] = cst
            ns[c] = nst
            cst = dec2[c:c + 1] * cst + u_all[c]
            nst = dec2[c:c + 1] * nst + kwsum[c:c + 1]
        if not has_state:
            for e in range(2):
                half = slice(e * HD, (e + 1) * HD)
                cn_ref[0, d, 2 * j + e] = cst[half, half]
                nn_ref[0, d, 2 * j + e:2 * j + e + 1, :] = nst[:, half]
        qc = _bmm(q3, jnp.stack(cs).astype(BF16))
        qn_all = q3f * jnp.stack(ns)
        dn = []
        for e in range(2):
            qn = jnp.sum(jnp.where(lo3 != (e == 1), qn_all, 0.0), axis=-1, keepdims=True)
            den = a_c[e] * qn + ws_sum[e]
            dn.append(jnp.maximum(jnp.abs(den), jnp.exp(-mt_c[e])))
        h2 = (jnp.where(lo3, a_c[0], a_c[1]) * qc + intra) / jnp.where(lo3, dn[0], dn[1])
        hsum = h2 if hsum is None else hsum + h2

    hs = hsum.reshape(s_len, LANES)
    lo = _lane(hs.shape) < HD
    s = hs * hs
    s_lo = jnp.sum(jnp.where(lo, s, 0.0), axis=-1, keepdims=True)
    s_hi = jnp.sum(jnp.where(lo, 0.0, s), axis=-1, keepdims=True)
    r = lax.rsqrt(jnp.where(lo, s_lo, s_hi) * (1.0 / HD) + EPS)
    ob_ref[:, pair] = (hs * r * hn_ref[...] * jax.nn.sigmoid(bo_ref[:, pair])).astype(BF16)


def _mlstm_call(bq, bk, bv, bo, li, lf, hn2, state, l):
    has_state = state is not None
    if has_state:
        nb, s_len, base = NB_S, S_S, T_P // S_S
    else:
        nb, s_len, base = NB_P, S_P, 0
    nc = s_len // CHUNK
    tokp = pl.BlockSpec((s_len, 2 * LANES), lambda b: (base + b, 0))
    in_specs = [tokp, tokp, tokp, tokp, tokp, tokp, pl.BlockSpec((1, LANES), lambda b: (0, 0))]
    args = [bq, bk, bv, bo, li, lf, hn2]
    ob_spec = pl.BlockSpec((s_len, 2 * LANES), lambda b: (b, 0))
    ob_shape = jax.ShapeDtypeStruct((nb * s_len, 256), BF16)
    state_specs = [pl.BlockSpec((1, 2, B_HEADS, HD, HD), lambda b: (b, 0, 0, 0, 0)),
                   pl.BlockSpec((1, 2, B_HEADS, HD), lambda b: (b, 0, 0, 0)),
                   pl.BlockSpec((2, 1, LANES), lambda b: (b, 0, 0))]
    if has_state:
        in_specs += state_specs
        args += list(state)
        out_specs = ob_spec
        out_shape = ob_shape
    else:
        out_specs = [ob_spec] + state_specs
        out_shape = [ob_shape,
                     jax.ShapeDtypeStruct((nb, 2, B_HEADS, HD, HD), F32),
                     jax.ShapeDtypeStruct((nb, 2, B_HEADS, HD), F32),
                     jax.ShapeDtypeStruct((nb * 2, 1, LANES), F32)]
    return pl.pallas_call(
        functools.partial(_mlstm_kernel, nc, has_state),
        grid=(nb,),
        in_specs=in_specs,
        out_specs=out_specs,
        out_shape=out_shape,
        compiler_params=_cparams(("arbitrary",)),
        name=f"mlstm_{'sample' if has_state else 'prompt'}_l{l}",
    )(*args)


def _out_kernel(moe, xp_ref, xs_ref, oap_ref, oas_ref, obp_ref, obs_ref, ocp_ref, ocs_ref,
                w_ref, g1_ref, sh_ref, sc_ref, n2_ref, *rest):
    if moe:
        rh_ref, rl_ref, x1_o, xn_o, route_o, cnt_o, run_scr = rest
    else:
        x1_o, xn_o = rest
    i = pl.program_id(0)
    o = (_dot(_pick(i, oap_ref, oas_ref), w_ref[0, 0:384, :])
         + _dot(_pick(i, obp_ref, obs_ref), w_ref[0, 384:640, :])
         + _dot(_pick(i, ocp_ref, ocs_ref), w_ref[0, 640:1024, :]))
    x1 = _pick(i, xp_ref, xs_ref) + g1_ref[0] * o
    x1_o[...] = x1
    xn = x1 * lax.rsqrt(jnp.mean(x1 * x1, axis=-1, keepdims=True) + EPS) * n2_ref[...]
    xn = xn * (1.0 + sc_ref[0]) + sh_ref[0]
    xb = xn.astype(BF16)
    if not moe:
        xn_o[...] = xb
    else:
        xn_o[...] = xn
        xl = (xn - xb.astype(F32)).astype(BF16)
        logits = _dot(xb, rh_ref[0]) + (_dot(xl, rh_ref[0]) + _dot(xb, rl_ref[0]))
        lane = _lane(logits.shape)
        logits = jnp.where(lane < N_EXP, logits, -jnp.inf)
        m1 = logits.max(axis=-1, keepdims=True)
        i1 = jnp.min(jnp.where(logits == m1, lane, LANES), axis=-1, keepdims=True)
        rest_l = jnp.where(lane == i1, -jnp.inf, logits)
        m2 = rest_l.max(axis=-1, keepdims=True)
        i2 = jnp.min(jnp.where(rest_l == m2, lane, LANES), axis=-1, keepdims=True)
        e2 = jnp.exp(m2 - m1)
        den = 1.0 + e2
        w1 = 1.0 / den
        w2 = e2 / den

        @pl.when(pl.program_id(0) == 0)
        def _():
            run_scr[...] = jnp.zeros(run_scr.shape, F32)

        sel = jnp.where(lane == i1, 1.0, jnp.where((lane == i2) & (w2 > 0.0), 1.0, 0.0))
        rb = 256
        before = (lax.broadcasted_iota(jnp.int32, (rb, rb), 1)
                  < lax.broadcasted_iota(jnp.int32, (rb, rb), 0)).astype(BF16)
        run = run_scr[...]
        ranks = []
        for k in range(TM // rb):
            part = sel[k * rb:(k + 1) * rb]
            ranks.append(run + _dot(before, part.astype(BF16)))
            run = run + jnp.sum(part, axis=0, keepdims=True)
        rank = jnp.concatenate(ranks, axis=0)
        r1 = jnp.sum(jnp.where(lane == i1, rank, 0.0), axis=-1, keepdims=True)
        r2 = jnp.sum(jnp.where(lane == i2, rank, 0.0), axis=-1, keepdims=True)
        run_scr[...] = run
        cnt_o[...] = run_scr[...]
        fields = [i1.astype(F32), i2.astype(F32), r1, r2, w1, w2]
        info = jnp.zeros(lane.shape, F32)
        for k, v in enumerate(fields):
            info = jnp.where(lane == k, v, info)
        route_o[...] = info.T


def _out_call(x_p, x_s, sample_off, oa, ob, oc, w_out_b, mod3, l, n2g, router):
    moe = router is not None
    row = lambda k: pl.BlockSpec((1, 1, D), lambda i: ((l * 8 + _mod_row(i)) * 6 + k, 0, 0))
    tok = lambda n: pl.BlockSpec((TM, n), lambda i: (i, 0))
    in_specs = (_pair_specs(D, sample_off) + _pair_specs(384, 0) + _pair_specs(256, 0) + _pair_specs(384, 0)
                + [pl.BlockSpec((1, D, D), lambda i: (0, 0, 0)),
                   row(2), row(3), row(4), pl.BlockSpec((1, D), lambda i: (0, 0))])
    args = [x_p, x_s, *oa, *ob, *oc, w_out_b, mod3, mod3, mod3, n2g]
    out_specs = [tok(D), tok(D)]
    out_shape = [jax.ShapeDtypeStruct((T, D), F32), jax.ShapeDtypeStruct((T, D), BF16)]
    if moe:
        out_shape[1] = jax.ShapeDtypeStruct((T, D), F32)
        rh, rl = router
        in_specs += [pl.BlockSpec((1, D, LANES), lambda i: (0, 0, 0))] * 2
        args += [rh, rl]
        out_specs += [pl.BlockSpec((LANES, TM), lambda i: (0, i)), pl.BlockSpec((1, LANES), lambda i: (0, 0))]
        out_shape += [jax.ShapeDtypeStruct((LANES, T), F32), jax.ShapeDtypeStruct((1, LANES), F32)]
    return pl.pallas_call(
        functools.partial(_out_kernel, moe),
        grid=(NT,),
        in_specs=in_specs, out_specs=out_specs, out_shape=out_shape,
        scratch_shapes=[pltpu.VMEM((1, LANES), F32)] if moe else [],
        compiler_params=_cparams(("arbitrary",)),
        name=f"out_proj_l{l}",
    )(*args)


FFN_TM = 1024
FFN_TF = 256
FFN_TN = 256


def _hidden_tile(x, w1_ref, w3_ref):
    a = _dot(x, w1_ref[...].astype(BF16))
    b = _dot(x, w3_ref[...].astype(BF16))
    return (a * jax.nn.sigmoid(a) * b).astype(BF16)


def _down_tile(h_scr, w2n, m):
    out = None
    for f in range(h_scr.shape[0]):
        o = _dot(h_scr[f, 0:m, :], w2n[f])
        out = o if out is None else out + o
    return out


def _keep_weights(s, w1_ref, w3_ref, w2_ref, w1b, w3b, w2b, tn):
    w1b[s] = w1_ref[...].astype(BF16)
    w3b[s] = w3_ref[...].astype(BF16)
    w2 = w2_ref[...].astype(BF16)
    for n in range(w2b.shape[0]):
        w2b[n, s] = w2[:, n * tn:(n + 1) * tn]


def _ffn_kernel(nf, xn_ref, x1_ref, g2_ref, w1_ref, w3_ref, w2_ref, y_ref, h_scr, w1b, w3b, w2b):
    i = pl.program_id(0)
    s = pl.program_id(1)

    @pl.when((i == 0) & (s < nf))
    def _():
        _keep_weights(s, w1_ref.at[0], w3_ref.at[0], w2_ref.at[0], w1b, w3b, w2b, FFN_TN)

    @pl.when(s < nf)
    def _():
        h_scr[s] = _hidden_tile(xn_ref[...], w1b.at[s], w3b.at[s])

    @pl.when(s >= nf)
    def _():
        y_ref[...] = x1_ref[...] + g2_ref[0] * _down_tile(h_scr, w2b.at[s - nf], FFN_TM)


def _ffn_call(xn, x1, mod3, l, w1, w3, w2, i_layer):
    nf = D_FF // FFN_TF
    nn = D // FFN_TN
    tile = lambda i, s: jnp.where(i == 0, jnp.minimum(s, nf - 1), nf - 1)
    out = lambda s: jnp.maximum(s - nf, 0)

    def g2_idx(i, s):
        r = jnp.where(i < T_P // FFN_TM, 0, 1 + (i - T_P // FFN_TM) // (S_S // FFN_TM))
        return ((l * 8 + r) * 6 + 5, 0, out(s))

    return pl.pallas_call(
        functools.partial(_ffn_kernel, nf),
        grid=(T // FFN_TM, nf + nn),
        in_specs=[pl.BlockSpec((FFN_TM, D), lambda i, s: (i, 0)),
                  pl.BlockSpec((FFN_TM, FFN_TN), lambda i, s: (i, out(s))),
                  pl.BlockSpec((1, 1, FFN_TN), g2_idx),
                  pl.BlockSpec((1, D, FFN_TF), lambda i, s: (i_layer, 0, tile(i, s))),
                  pl.BlockSpec((1, D, FFN_TF), lambda i, s: (i_layer, 0, tile(i, s))),
                  pl.BlockSpec((1, FFN_TF, D), lambda i, s: (i_layer, tile(i, s), 0))],
        out_specs=pl.BlockSpec((FFN_TM, FFN_TN), lambda i, s: (i, out(s))),
        out_shape=jax.ShapeDtypeStruct((T, D), F32),
        scratch_shapes=[pltpu.VMEM((nf, FFN_TM, FFN_TF), BF16),
                        pltpu.VMEM((nf, D, FFN_TF), BF16), pltpu.VMEM((nf, D, FFN_TF), BF16),
                        pltpu.VMEM((nn, nf, FFN_TF, FFN_TN), BF16)],
        compiler_params=_cparams(("arbitrary", "arbitrary")),
        name="ffn_dense",
    )(xn, x1, mod3, w1, w3, w2)


MOE_BM = 1024
MOE_SUB = 256
MOE_NBLK = 2 * T // MOE_BM + N_EXP
MOE_NR = MOE_NBLK * MOE_BM
MOE_TF = 512
MOE_TN = 256
FIN_TM = 512


def _route_kernel(i1_ref, i2_ref, r1_ref, r2_ref, cnt_ref, w2_ref,
                  src_ref, eb_ref, rows_ref, nvb_ref, off_scr):
    def clear(r, carry):
        src_ref[r] = 0
        return carry

    lax.fori_loop(0, MOE_NR + 8, clear, 0, unroll=8)

    def clear_blk(b, carry):
        eb_ref[b] = 0
        rows_ref[b] = 0
        return carry

    lax.fori_loop(0, MOE_NBLK, clear_blk, 0)

    nblk = jnp.int32(0)
    for e in range(N_EXP):
        c = cnt_ref[e]
        nbe = (c + (MOE_BM - 1)) // MOE_BM
        off_scr[e] = nblk * MOE_BM

        def fill(j, carry, e=e, c=c, nblk=nblk):
            eb_ref[nblk + j] = e
            rows_ref[nblk + j] = jnp.minimum(c - j * MOE_BM, MOE_BM)
            return carry

        lax.fori_loop(0, nbe, fill, 0)
        nblk = nblk + nbe
    nvb_ref[0] = nblk

    def place(t, carry):
        src_ref[off_scr[i1_ref[t]] + r1_ref[t]] = t
        src_ref[jnp.where(w2_ref[t] > 0.0, off_scr[i2_ref[t]] + r2_ref[t], MOE_NR)] = t
        return carry

    lax.fori_loop(0, T, place, 0, unroll=4)


def _route_call(route, cnt):
    ints = [route[k].astype(jnp.int32) for k in range(4)]
    cnt8 = cnt[0, :N_EXP].astype(jnp.int32)
    smem = pl.BlockSpec(memory_space=pltpu.SMEM)
    outs = pl.pallas_call(
        _route_kernel,
        grid_spec=pltpu.PrefetchScalarGridSpec(
            num_scalar_prefetch=5, grid=(1,),
            in_specs=[smem],
            out_specs=[smem, smem, smem, smem],
            scratch_shapes=[pltpu.SMEM((N_EXP,), jnp.int32)]),
        out_shape=[jax.ShapeDtypeStruct((MOE_NR + 8,), jnp.int32),
                   jax.ShapeDtypeStruct((MOE_NBLK,), jnp.int32), jax.ShapeDtypeStruct((MOE_NBLK,), jnp.int32),
                   jax.ShapeDtypeStruct((1,), jnp.int32)],
        compiler_params=_cparams(("arbitrary",)),
        name="moe_route",
    )(*ints, cnt8, route[5])
    return tuple(outs) + (ints[0], route[4], route[5])


def _moe_gather_kernel(src_ref, rows_ref, x_ref, o_ref, rows_scr):
    k = pl.program_id(0)
    base = k * MOE_SUB
    used = rows_ref[k // (MOE_BM // MOE_SUB)] > (k % (MOE_BM // MOE_SUB)) * MOE_SUB

    @pl.when(used)
    def _():
        def body(i, carry):
            for j in range(8):
                r = i * 8 + j
                rows_scr[pl.ds(r, 1), :] = x_ref[pl.ds(src_ref[base + r], 1), :]
            return carry

        lax.fori_loop(0, MOE_SUB // 8, body, 0)
        o_ref[...] = rows_scr[...].astype(BF16)

    @pl.when(jnp.logical_not(used))
    def _():
        o_ref[...] = jnp.zeros(o_ref.shape, o_ref.dtype)


def _moe_gather_call(src, rows_b, xn):
    return pl.pallas_call(
        _moe_gather_kernel,
        grid_spec=pltpu.PrefetchScalarGridSpec(
            num_scalar_prefetch=2, grid=(MOE_NR // MOE_SUB,),
            in_specs=[pl.BlockSpec((T, D), lambda k, s, rw: (0, 0), pipeline_mode=pl.Buffered(1))],
            out_specs=pl.BlockSpec((MOE_SUB, D), lambda k, s, rw: (k, 0)),
            scratch_shapes=[pltpu.VMEM((MOE_SUB, D), F32)]),
        out_shape=jax.ShapeDtypeStruct((MOE_NR, D), BF16),
        compiler_params=_cparams(("arbitrary",)),
        name="moe_gather",
    )(src, rows_b, xn)


def _moe_first(b, eb_ref):
    return (b == 0) | (eb_ref[b] != eb_ref[jnp.maximum(b - 1, 0)])


def _moe_kernel(nf, eb_ref, rows_ref, nvb_ref, xs_ref, w1_ref, w3_ref, w2_ref, y_ref, h_scr, w1b, w3b, w2b):
    b = pl.program_id(0)
    s = pl.program_id(1)
    nsub = jnp.where(b < nvb_ref[0], (rows_ref[b] + MOE_SUB - 1) // MOE_SUB, 0)
    sizes = [k * MOE_SUB for k in range(1, MOE_BM // MOE_SUB + 1)]

    @pl.when((s < nf) & (nsub > 0) & _moe_first(b, eb_ref))
    def _():
        _keep_weights(s, w1_ref.at[0, 0], w3_ref.at[0, 0], w2_ref.at[0, 0], w1b, w3b, w2b, MOE_TN)

    @pl.when((s < nf) & (nsub > 0))
    def _():
        for k, m in enumerate(sizes, start=1):
            @pl.when(nsub == k)
            def _(m=m):
                h_scr[s, 0:m, :] = _hidden_tile(xs_ref[0:m, :], w1b.at[s], w3b.at[s])

    @pl.when((s >= nf) & (nsub > 0))
    def _():
        for k, m in enumerate(sizes, start=1):
            @pl.when(nsub == k)
            def _(m=m):
                y_ref[0:m, :] = _down_tile(h_scr, w2b.at[s - nf], m)
                if m < MOE_BM:
                    y_ref[m:MOE_BM, :] = jnp.zeros((MOE_BM - m, MOE_TN), F32)

    @pl.when((s >= nf) & (nsub == 0))
    def _():
        y_ref[...] = jnp.zeros(y_ref.shape, F32)


def _moe_call(xs, e_b, rows_b, nvb, w1, w3, w2, i_layer):
    nf = D_FFE // MOE_TF
    nn = D // MOE_TN

    def blk(b, nv):
        return jnp.minimum(b, nv[0] - 1)

    def tile(b, s, eb, nv):
        stream = (b < nv[0]) & _moe_first(b, eb)
        return jnp.where(stream, jnp.minimum(s, nf - 1), nf - 1)

    return pl.pallas_call(
        functools.partial(_moe_kernel, nf),
        grid_spec=pltpu.PrefetchScalarGridSpec(
            num_scalar_prefetch=3, grid=(MOE_NBLK, nf + nn),
            in_specs=[pl.BlockSpec((MOE_BM, D), lambda b, s, eb, rw, nv: (blk(b, nv), 0)),
                      pl.BlockSpec((1, 1, D, MOE_TF),
                                   lambda b, s, eb, rw, nv: (i_layer, eb[blk(b, nv)], 0, tile(b, s, eb, nv))),
                      pl.BlockSpec((1, 1, D, MOE_TF),
                                   lambda b, s, eb, rw, nv: (i_layer, eb[blk(b, nv)], 0, tile(b, s, eb, nv))),
                      pl.BlockSpec((1, 1, MOE_TF, D),
                                   lambda b, s, eb, rw, nv: (i_layer, eb[blk(b, nv)], tile(b, s, eb, nv), 0))],
            out_specs=pl.BlockSpec((MOE_BM, MOE_TN), lambda b, s, eb, rw, nv: (b, jnp.maximum(s - nf, 0))),
            scratch_shapes=[pltpu.VMEM((nf, MOE_BM, MOE_TF), BF16),
                            pltpu.VMEM((nf, D, MOE_TF), BF16), pltpu.VMEM((nf, D, MOE_TF), BF16),
                            pltpu.VMEM((nn, nf, MOE_TF, MOE_TN), BF16)]),
        out_shape=jax.ShapeDtypeStruct((MOE_NR, D), F32),
        compiler_params=_cparams(("arbitrary", "arbitrary"), vmem=56 * 1024 * 1024),
        name="moe_experts",
    )(e_b, rows_b, nvb, xs, w1, w3, w2)


def _moe_combine_kernel(src_ref, eb_ref, rows_ref, nvb_ref, i1_ref, w1_ref, w2_ref,
                        ys_ref, x1_ref, g2_ref, yp_ref, ys_out_ref, acc):
    i = pl.program_id(0)

    @pl.when(i == 0)
    def _():
        acc[...] = jnp.zeros(acc.shape, F32)

    @pl.when(i < nvb_ref[0])
    def _():
        base = i * MOE_BM
        n = rows_ref[i]
        e = eb_ref[i]

        def gate(t):
            return jnp.where(i1_ref[t] == e, w1_ref[t], w2_ref[t])

        def add_rows(r0, cnt):
            toks = [src_ref[base + r0 + j] for j in range(cnt)]
            vals = [acc[pl.ds(toks[j], 1), :] + gate(toks[j]) * ys_ref[pl.ds(r0 + j, 1), :]
                    for j in range(cnt)]
            for j in range(cnt):
                acc[pl.ds(toks[j], 1), :] = vals[j]

        def body4(q, carry):
            add_rows(q * 4, 4)
            return carry

        lax.fori_loop(0, n // 4, body4, 0)

        def body1(r, carry):
            add_rows(r, 1)
            return carry

        lax.fori_loop((n // 4) * 4, n, body1, 0)

    @pl.when(i >= MOE_NBLK)
    def _():
        t0 = pl.multiple_of((i - MOE_NBLK) * FIN_TM, FIN_TM)
        y = x1_ref[...] + g2_ref[0] * acc[pl.ds(t0, FIN_TM), :]

        @pl.when(i - MOE_NBLK < T_P // FIN_TM)
        def _():
            yp_ref[...] = y

        @pl.when(i - MOE_NBLK >= T_P // FIN_TM)
        def _():
            ys_out_ref[...] = y


def _moe_combine_call(src, e_b, rows_b, nvb, i1, w1, w2, ys, x1, mod3, l):
    nfin = T // FIN_TM
    smem = pl.BlockSpec(memory_space=pltpu.SMEM)

    def g2_idx(i, *_):
        j = jnp.maximum(i - MOE_NBLK, 0)
        r = jnp.where(j < T_P // FIN_TM, 0, 1 + (j - T_P // FIN_TM) // (S_S // FIN_TM))
        return ((l * 8 + r) * 6 + 5, 0, 0)

    return pl.pallas_call(
        _moe_combine_kernel,
        grid_spec=pltpu.PrefetchScalarGridSpec(
            num_scalar_prefetch=5, grid=(MOE_NBLK + nfin,),
            in_specs=[smem, smem,
                      pl.BlockSpec((MOE_BM, D),
                                   lambda i, s, eb, rw, nv, t1: (jnp.minimum(jnp.minimum(i, MOE_NBLK - 1), nv[0] - 1), 0)),
                      pl.BlockSpec((FIN_TM, D), lambda i, *_: (jnp.maximum(i - MOE_NBLK, 0), 0)),
                      pl.BlockSpec((1, 1, D), g2_idx)],
            out_specs=[pl.BlockSpec((FIN_TM, D),
                                    lambda i, *_: (jnp.clip(i - MOE_NBLK, 0, T_P // FIN_TM - 1), 0)),
                       pl.BlockSpec((FIN_TM, D),
                                    lambda i, *_: (jnp.maximum(i - MOE_NBLK - T_P // FIN_TM, 0), 0))],
            scratch_shapes=[pltpu.VMEM((T, D), F32)]),
        out_shape=[jax.ShapeDtypeStruct((T_P, D), F32), jax.ShapeDtypeStruct((T_S, D), F32)],
        compiler_params=_cparams(("arbitrary",), vmem=52 * 1024 * 1024),
        name="moe_combine",
    )(src, e_b, rows_b, nvb, i1, w1, w2, ys, x1, mod3)


def _moe_layer(xp, x1, route, cnt, mod3, l, w1, w3, w2, i_layer):
    src, e_b, rows_b, nvb, i1, g1, g2 = _route_call(route, cnt)
    xs = _moe_gather_call(src, rows_b, xp)
    ys = _moe_call(xs, e_b, rows_b, nvb, w1, w3, w2, i_layer)
    return _moe_combine_call(src, e_b, rows_b, nvb, i1, g1, g2, ys, x1, mod3, l)


def _pad_cols(w, n):
    return jnp.pad(w, ((0, 0), (0, n - w.shape[1])))


def _tail_selector():
    sel = np.zeros((432, 1024), np.float32)
    for d in range(2):
        for gate in range(2):
            for head in range(B_HEADS):
                sel[d * 8 + gate * 4 + head, gate * 256 + (head // 2) * LANES + d * 2 + head % 2] = 1.0
    for c in range(C_QRANK):
        sel[16 + c, 512 + c] = 1.0
    for c in range(C_KVRANK):
        sel[16 + C_QRANK + c, 768 + c] = 1.0
    for c in range(C_ROPE):
        sel[16 + C_QRANK + C_KVRANK + c, 896 + C_NOPE + c] = 1.0
    return sel


def _relayout_w_tail(w):
    sel = jnp.asarray(_tail_selector(), BF16)
    return jnp.dot(w[:, IN_MAIN:].astype(BF16), sel, preferred_element_type=BF16)


def _relayout_gate_b(b):
    return jnp.dot(b[None, :], jnp.asarray(_tail_selector()[:16, :512]), precision=lax.Precision.HIGHEST)


def _pad_heads(w, width):
    r = w.shape[0]
    h = w.shape[1] // width
    return jnp.pad(w.reshape(r, h, width), ((0, 0), (0, 0), (0, LANES - width))).reshape(r, h * LANES)


def _rope_tables(half, span_start, period):
    rows = S_S // 64
    r = jnp.repeat(jnp.arange(rows), 64).astype(F32)
    c = jnp.tile(jnp.arange(64), rows).astype(F32)
    n_freq = half // 2
    freq = 10000.0 ** (-jnp.arange(n_freq, dtype=F32) / n_freq)
    ang = jnp.concatenate([r[:, None] * freq, c[:, None] * freq], axis=-1)
    cos, sin = jnp.cos(ang), jnp.sin(ang)
    d = (jnp.arange(LANES) - span_start) % period
    inside = d < 2 * half
    p = jnp.where(inside, d % half, 0)
    first = inside & (d < half)
    second = inside & (d >= half)
    ct = jnp.where(inside[None, :], cos[:, p], 1.0)
    sa = jnp.where(first[None, :], -sin[:, p], 0.0)
    sb = jnp.where(second[None, :], sin[:, p], 0.0)
    ident = (jnp.ones((TM, LANES), F32), jnp.zeros((TM, LANES), F32), jnp.zeros((TM, LANES), F32))
    return tuple(jnp.concatenate([i0, t], axis=0) for i0, t in zip(ident, (ct, sa, sb)))


def kernel(x_prompt, x_sample, c, cache_swa_k, cache_swa_v, cache_mla_ckv, cache_mla_krope, state_mlstm_C, state_mlstm_n, state_mlstm_m, c_ctx, ada_w, ada_b, norm1_g, norm2_g, w_in, a_qn_g, a_kn_g, a_sink, b_gate_b, b_hn_g, c_qa_g, c_kva_g, c_wuq, c_wukv, c_qn_g, c_kn_g, w_out, ffn_w1, ffn_w3, ffn_w2, moe_router, moe_w1, moe_w3, moe_w2):
    x_p, x_s, sample_off = x_prompt.reshape(T_P, D), x_sample.reshape(T_S, D), 0
    cv = jnp.concatenate([c_ctx[None, :], c, jnp.zeros((5, D), F32)], axis=0)
    mod = _ada_call(cv, ada_w, ada_b)
    mod3 = mod.reshape(DEPTH * 8 * 6, 1, D)

    rope_a = _rope_tables(32, 0, HD)
    rope_c = _rope_tables(16, C_NOPE, LANES)

    cache_k4 = cache_swa_k.reshape(NB_S, DEPTH, PAST, A_KV * HD)
    cache_v4 = cache_swa_v.reshape(NB_S, DEPTH, PAST, A_KV * HD)

    news = []
    for l in range(DEPTH):
        w_tail = _relayout_w_tail(w_in[l])
        wuq_p = _pad_heads(c_wuq[l], C_QK).astype(BF16)[None]
        wukv = c_wukv[l].reshape(C_KVRANK, C_HEADS, C_NOPE + C_V)
        wk_p = jnp.pad(wukv[..., :C_NOPE], ((0, 0), (0, 0), (0, LANES - C_NOPE)))
        wk_p = wk_p.reshape(1, C_KVRANK, C_HEADS * LANES).astype(BF16)
        wv_p = wukv[..., C_NOPE:].reshape(1, C_KVRANK, C_HEADS * C_V).astype(BF16)
        w_out_b = w_out[l].astype(BF16)[None]
        gqa = jnp.tile(a_qn_g[l], A_HEADS)[None, :]
        gka = jnp.tile(a_kn_g[l], A_KV)[None, :]
        gb = _relayout_gate_b(b_gate_b[l])
        gqn = jnp.tile(jnp.pad(c_qn_g[l], (0, LANES - C_QK)), C_HEADS)[None, :]
        gkn = jnp.tile(jnp.pad(c_kn_g[l], (0, LANES - C_QK)), C_HEADS)[None, :]
        (qa, ka, va, kaf, vaf, bq, bk, bv, bo, li, lf, qc, ckv_n, ckr) = _in_call(
            x_p, x_s, sample_off, mod3, l, norm1_g[l][None, :], w_in, w_tail, gqa, gka, gb, c_qa_g[l][None, :],
            c_kva_g[l][None, :], wuq_p, gqn, rope_a, rope_c)

        ckr_cache = jnp.pad(cache_mla_krope[:, l].reshape(NB_S * PAST, C_ROPE),
                            ((0, 0), (C_NOPE, LANES - C_NOPE - C_ROPE)))
        kc, vc = _mlakv_call(ckv_n, cache_mla_ckv[:, l].reshape(NB_S * PAST, C_KVRANK), ckr, ckr_cache,
                             l, wk_p, wv_p, gkn, rope_c)

        sink = a_sink[l]
        oa_p = _attn_a_prompt_call(sink, qa, ka, va)
        oa_s = _attn_a_sample_call(sink, qa, ka, va, cache_k4, cache_v4, l)
        oc_p = _mla_prompt_call(qc, kc, vc)
        oc_s = _mla_sample_call(qc, kc, vc)

        hn2 = jnp.tile(b_hn_g[l], 2)[None, :]
        ob_p, cn, nn, mn = _mlstm_call(bq, bk, bv, bo, li, lf, hn2, None, l)
        c0, n0 = state_mlstm_C[:, l], state_mlstm_n[:, l]
        m_st = state_mlstm_m[:, l].reshape(NB_S, 2, 2, 2)
        m0 = jnp.transpose(m_st, (0, 2, 1, 3)).reshape(NB_S * 2, 1, 4)
        m0 = jnp.pad(m0, ((0, 0), (0, 0), (0, LANES - 4)))
        ob_s = _mlstm_call(bq, bk, bv, bo, li, lf, hn2, (c0, n0, m0), l)

        oa, ob, oc = (oa_p, oa_s), (ob_p, ob_s), (oc_p, oc_s)

        if l % 2 == 0:
            x1, xn = _out_call(x_p, x_s, sample_off, oa, ob, oc, w_out_b, mod3, l, norm2_g[l][None, :], None)
            x = _ffn_call(xn, x1, mod3, l, ffn_w1, ffn_w3, ffn_w2, l // 2)
            x_p, x_s, sample_off = x, x, NT_P
        else:
            r = _pad_cols(moe_router[l // 2], LANES)
            rh = r.astype(BF16)
            rl = (r - rh.astype(F32)).astype(BF16)
            x1, xn, route, cnt = _out_call(x_p, x_s, sample_off, oa, ob, oc, w_out_b, mod3, l,
                                           norm2_g[l][None, :], (rh[None], rl[None]))
            x_p, x_s = _moe_layer(xn, x1, route, cnt, mod3, l, moe_w1, moe_w3, moe_w2, l // 2)
            sample_off = 0

        new_k = kaf.reshape(NB_P, S_P, A_KV, HD)
        new_v = vaf.reshape(NB_P, S_P, A_KV, HD)
        new_ckv = ckv_n[:T_P].reshape(NB_P, S_P, C_KVRANK)
        new_kr = ckr[:T_P, C_NOPE:C_NOPE + C_ROPE].reshape(NB_P, S_P, C_ROPE)
        new_c, new_n = cn, nn
        mn4 = mn.reshape(NB_P, 2, LANES)[:, :, :4].reshape(NB_P, 2, 2, 2)
        new_m = jnp.transpose(mn4, (0, 2, 1, 3)).reshape(NB_P, 2, B_HEADS)
        news.append((new_k, new_v, new_ckv, new_kr, new_c, new_n, new_m))

    y_prompt = x_p[:T_P].reshape(NB_P, S_P, D)
    y_sample = x_s[sample_off * TM:sample_off * TM + T_S].reshape(NB_S, S_S, D)
    stacked = tuple(jnp.stack([nw[j] for nw in news], axis=1) for j in range(7))
    return (y_prompt, y_sample) + stacked
```

```python
import functools

import jax
import jax.numpy as jnp
import numpy as np
from jax import lax
from jax.experimental import pallas as pl
from jax.experimental.pallas import tpu as pltpu

F32 = jnp.float32
BF16 = jnp.bfloat16

D = 1024
NB_P, S_P = 16, 256
NB_S, S_S = 2, 1024
PAST = 512
DEPTH = 2
T_P = NB_P * S_P
T_S = NB_S * S_S
T = T_P + T_S
TM = 512
NT = T // TM
NT_P = T_P // TM
HD = 64
A_HEADS, A_KV = 6, 2
B_HEADS = 4
CHUNK = 64
C_HEADS = 6
C_QRANK, C_KVRANK, C_NOPE, C_ROPE, C_V = 256, 128, 64, 32, 64
C_QK = C_NOPE + C_ROPE
D_FF = 2816
N_EXP = 8
D_FFE = 3584
EPS = 1e-6
NEG = -1e30
LANES = 128
VMEM_LIMIT = 48 * 1024 * 1024

IN_MAIN = 1664
SEG = dict(QA=(0, 384), KA=(384, 128), VA=(512, 128), BQ=(640, 256), BK=(896, 256), BV=(1152, 256),
           BO=(1408, 256), GI=(1664, 256), GF=(1920, 256), CQ=(2176, 256), CKV=(2432, 128), CKR=(2560, 128))
NP_IN = 2688


def _cparams(sem, vmem=VMEM_LIMIT):
    return pltpu.CompilerParams(dimension_semantics=sem, vmem_limit_bytes=vmem)


def _dot(a, b):
    return jnp.dot(a, b, preferred_element_type=F32)


def _dot_nt(a, b):
    return lax.dot_general(a, b, (((1,), (1,)), ((), ())), preferred_element_type=F32)


def _lane(shape):
    return lax.broadcasted_iota(jnp.int32, shape, len(shape) - 1)


def _mod_row(i):
    return jnp.where(i < NT_P, 0, 1 + (i - NT_P) // (S_S // TM))


def _pair_specs(n, sample_off):
    return [pl.BlockSpec((TM, n), lambda i: (jnp.minimum(i, NT_P - 1), 0)),
            pl.BlockSpec((TM, n), lambda i: (sample_off + jnp.maximum(i - NT_P, 0), 0))]


def _pick(i, p_ref, s_ref):
    return jnp.where(i < NT_P, p_ref[...], s_ref[...])


def _rope_blk(i):
    return jnp.where((i >= NT_P) & (i < NT), 1 + (i - NT_P) % (S_S // TM), 0)


def _ada_kernel(cv_ref, w_ref, b_ref, o_ref):
    s = cv_ref[...]
    s = s * jax.nn.sigmoid(s)
    o_ref[0] = _dot(s.astype(BF16), w_ref[0].astype(BF16)) + b_ref[0]


def _ada_call(cv, ada_w, ada_b):
    tn = 1536
    return pl.pallas_call(
        _ada_kernel,
        grid=(DEPTH, 6 * D // tn),
        in_specs=[pl.BlockSpec((8, D), lambda l, j: (0, 0)),
                  pl.BlockSpec((1, D, tn), lambda l, j: (l, 0, j)),
                  pl.BlockSpec((1, 1, tn), lambda l, j: (l, 0, j))],
        out_specs=pl.BlockSpec((1, 8, tn), lambda l, j: (l, 0, j)),
        out_shape=jax.ShapeDtypeStruct((DEPTH, 8, 6 * D), F32),
        compiler_params=_cparams(("arbitrary", "arbitrary")),
        name="ada_mod",
    )(cv, ada_w, ada_b.reshape(DEPTH, 1, 6 * D))


def _half_norm(x, g, n):
    outs = []
    for j in range(x.shape[1] // LANES):
        xj = x[:, j * LANES:(j + 1) * LANES]
        lo = _lane(xj.shape) < HD
        s = xj * xj
        s_lo = jnp.sum(jnp.where(lo, s, 0.0), axis=-1, keepdims=True)
        s_hi = jnp.sum(jnp.where(lo, 0.0, s), axis=-1, keepdims=True)
        r = lax.rsqrt(jnp.where(lo, s_lo, s_hi) * (1.0 / n) + EPS)
        outs.append(xj * r)
    return jnp.concatenate(outs, axis=-1) * g


def _group_norm(x, g, n):
    outs = []
    for j in range(x.shape[1] // LANES):
        xj = x[:, j * LANES:(j + 1) * LANES]
        r = lax.rsqrt(jnp.sum(xj * xj, axis=-1, keepdims=True) * (1.0 / n) + EPS)
        outs.append(xj * r)
    return jnp.concatenate(outs, axis=-1) * g


def _rope(x, c, sa, sb, shift):
    outs = []
    for j in range(x.shape[1] // LANES):
        xj = x[:, j * LANES:(j + 1) * LANES]
        outs.append(xj * c + pltpu.roll(xj, LANES - shift, 1) * sa + pltpu.roll(xj, shift, 1) * sb)
    return jnp.concatenate(outs, axis=-1)


def _in_kernel(xp_ref, xs_ref, sh_ref, sc_ref, n1_ref, w_ref, wt_ref, gqa_ref, gka_ref, gb_ref, gcq_ref, gckv_ref,
               wuq_ref, gqn_ref, ra_c, ra_a, ra_b, rc_c, rc_a, rc_b,
               qa_o, ka_o, va_o, kaf_o, vaf_o, bq_o, bk_o, bv_o, bo_o, li_o, lf_o, qc_o, ckv_o, ckr_o, wb_scr):
    i = pl.program_id(0)

    @pl.when(i == 0)
    def _():
        wb_scr[...] = w_ref[0].astype(BF16)

    x = _pick(i, xp_ref, xs_ref)
    xn = x * lax.rsqrt(jnp.mean(x * x, axis=-1, keepdims=True) + EPS) * n1_ref[...]
    xn = xn * (1.0 + sc_ref[0]) + sh_ref[0]
    xb = xn.astype(BF16)
    p_main = _dot(xb, wb_scr[...])
    p_tail = _dot(xb, wt_ref[...])

    def seg(name):
        o, n = SEG[name]
        return p_main[:, o:o + n] if o < IN_MAIN else p_tail[:, o - IN_MAIN:o - IN_MAIN + n]

    qa = _half_norm(seg("QA"), gqa_ref[...], HD)
    ka = _half_norm(seg("KA"), gka_ref[...], HD)
    va = seg("VA")

    @pl.when(i < NT_P)
    def _():
        kaf_o[...] = ka
        vaf_o[...] = va

    def dup(x):
        lo, hi = _half(x, False), _half(x, True)
        return jnp.concatenate([lo + pltpu.roll(lo, HD, 1), hi + pltpu.roll(hi, HD, 1)], axis=-1)

    va_o[...] = dup(va).astype(BF16)

    bq_o[...] = seg("BQ").astype(BF16)
    bk_o[...] = (seg("BK") * (HD ** -0.5)).astype(BF16)
    bv_o[...] = seg("BV").astype(BF16)
    bo_o[...] = seg("BO")
    gb = gb_ref[...]
    li_o[...] = seg("GI") + gb[:, 0:256]
    lf_o[...] = jax.nn.log_sigmoid(seg("GF") + gb[:, 256:512])

    cq = seg("CQ")
    cqn = cq * lax.rsqrt(jnp.mean(cq * cq, axis=-1, keepdims=True) + EPS) * gcq_ref[...]
    qc = _group_norm(_dot(cqn.astype(BF16), wuq_ref[0]), gqn_ref[...], C_QK)

    @pl.when(i < NT_P)
    def _():
        qa_o[...] = qa.astype(BF16)
        ka_o[...] = dup(ka).astype(BF16)
        qc_o[...] = qc.astype(BF16)

    @pl.when(i >= NT_P)
    def _():
        qa_o[...] = _rope(qa, ra_c[...], ra_a[...], ra_b[...], 32).astype(BF16)
        ka_o[...] = dup(_rope(ka, ra_c[...], ra_a[...], ra_b[...], 32)).astype(BF16)
        qc_o[...] = _rope(qc, rc_c[...], rc_a[...], rc_b[...], 16).astype(BF16)

    ckv = seg("CKV")
    ckv_o[...] = ckv * lax.rsqrt(jnp.mean(ckv * ckv, axis=-1, keepdims=True) + EPS) * gckv_ref[...]
    ckr_o[...] = seg("CKR")


def _in_call(x_p, x_s, sample_off, mod3, l, n1g, w_in, w_tail, gqa, gka, gb, gcq, gckv, wuq_p, gqn, rope_a, rope_c):
    row = lambda k: pl.BlockSpec((1, 1, D), lambda i: ((l * 8 + _mod_row(i)) * 6 + k, 0, 0))
    vec = lambda n: pl.BlockSpec((1, n), lambda i: (0, 0))
    tab = pl.BlockSpec((TM, LANES), lambda i: (_rope_blk(i), 0))
    tok = lambda n: pl.BlockSpec((TM, n), lambda i: (i, 0))
    tokp = pl.BlockSpec((TM, LANES), lambda i: (jnp.minimum(i, NT_P - 1), 0))
    o = lambda n, dt: jax.ShapeDtypeStruct((T, n), dt)
    return pl.pallas_call(
        _in_kernel,
        grid=(NT,),
        in_specs=_pair_specs(D, sample_off) + [row(0), row(1), vec(D),
                  pl.BlockSpec((1, D, IN_MAIN), lambda i: (l, 0, 0), pipeline_mode=pl.Buffered(1)),
                  pl.BlockSpec((D, NP_IN - IN_MAIN), lambda i: (0, 0)),
                  vec(384), vec(128), vec(512), vec(256), vec(128),
                  pl.BlockSpec((1, C_QRANK, 768), lambda i: (0, 0, 0)), vec(768),
                  tab, tab, tab, tab, tab, tab],
        out_specs=[tok(384), tok(256), tok(256), tokp, tokp, tok(256), tok(256), tok(256), tok(256),
                   tok(256), tok(256), tok(768), tok(128), tok(128)],
        out_shape=[o(384, BF16), o(256, BF16), o(256, BF16),
                   jax.ShapeDtypeStruct((T_P, LANES), F32), jax.ShapeDtypeStruct((T_P, LANES), F32),
                   o(256, BF16), o(256, BF16), o(256, BF16), o(256, F32),
                   o(256, F32), o(256, F32), o(768, BF16), o(128, F32), o(128, F32)],
        scratch_shapes=[pltpu.VMEM((D, IN_MAIN), BF16)],
        compiler_params=_cparams(("arbitrary",)),
        name=f"in_proj_l{l}",
    )(x_p, x_s, mod3, mod3, n1g, w_in, w_tail, gqa, gka, gb, gcq, gckv, wuq_p, gqn, *rope_a, *rope_c)


def _mlakv_kernel(ckv_ref, ckvc_ref, ckr_ref, ckrc_ref, wk_ref, wv_ref, gkn_ref, rc_c, rc_a, rc_b, k_o, v_o):
    i = pl.program_id(0)
    tok = i < NT
    c = jnp.where(tok, ckv_ref[...], ckvc_ref[...]).astype(BF16)
    k = _dot(c, wk_ref[0])
    kr = jnp.where(tok, ckr_ref[...], ckrc_ref[...])
    k = k + jnp.concatenate([kr] * C_HEADS, axis=-1)
    k = _group_norm(k, gkn_ref[...], C_QK)
    sample = (i >= NT_P) & tok

    @pl.when(sample)
    def _():
        k_o[...] = _rope(k, rc_c[...], rc_a[...], rc_b[...], 16).astype(BF16)

    @pl.when(jnp.logical_not(sample))
    def _():
        k_o[...] = k.astype(BF16)

    v_o[...] = _dot(c, wv_ref[0]).astype(BF16)


def _mlakv_call(ckv_n, ckv_cache, ckr, ckr_cache, l, wk_p, wv_p, gkn, rope_c):
    r = T + NB_S * PAST
    tab = pl.BlockSpec((TM, LANES), lambda i: (_rope_blk(i), 0))
    tok = lambda n: pl.BlockSpec((TM, n), lambda i: (i, 0))
    tokens = pl.BlockSpec((TM, LANES), lambda i: (jnp.minimum(i, NT - 1), 0))
    cached = pl.BlockSpec((TM, LANES), lambda i: (jnp.maximum(i - NT, 0), 0))
    return pl.pallas_call(
        _mlakv_kernel,
        grid=(r // TM,),
        in_specs=[tokens, cached, tokens, cached,
                  pl.BlockSpec((1, C_KVRANK, 768), lambda i: (0, 0, 0)),
                  pl.BlockSpec((1, C_KVRANK, 384), lambda i: (0, 0, 0)),
                  pl.BlockSpec((1, 768), lambda i: (0, 0)), tab, tab, tab],
        out_specs=[tok(768), tok(384)],
        out_shape=[jax.ShapeDtypeStruct((r, 768), BF16), jax.ShapeDtypeStruct((r, 384), BF16)],
        compiler_params=_cparams(("arbitrary",)),
        name=f"mla_kv_l{l}",
    )(ckv_n, ckv_cache, ckr, ckr_cache, wk_p, wv_p, gkn, *rope_c)


def _bmm(a, b):
    return lax.dot_general(a, b, (((2,), (1,)), ((0,), (0,))), preferred_element_type=F32)


def _bmm_nt(a, b):
    return lax.dot_general(a, b, (((2,), (2,)), ((0,), (0,))), preferred_element_type=F32)


def _bmm_tn(a, b):
    return lax.dot_general(a, b, (((1,), (1,)), ((0,), (0,))), preferred_element_type=F32)


def _softmax_pv(scores, values, sink):
    batched = scores[0].ndim == 3
    heads = range(scores[0].shape[0]) if batched else ()
    m = scores[0].max(axis=-1, keepdims=True)
    for s in scores[1:]:
        m = jnp.maximum(m, s.max(axis=-1, keepdims=True))
    if sink is not None:
        m = jnp.stack([jnp.maximum(m[h], sink[h]) for h in heads]) if batched else jnp.maximum(m, sink)
    es = [jnp.exp(s - m) for s in scores]
    den = es[0].sum(axis=-1, keepdims=True)
    for e in es[1:]:
        den = den + e.sum(axis=-1, keepdims=True)
    if sink is not None:
        den = (jnp.stack([den[h] + jnp.exp(sink[h] - m[h]) for h in heads]) if batched
               else den + jnp.exp(sink - m))
    inv = 1.0 / den
    out = None
    for e, v in zip(es, values):
        o = (_bmm if batched else _dot)((e * inv).astype(BF16), v)
        out = o if out is None else out + o
    return out


def _scores(q3, k3, scale):
    return jnp.stack([_dot_nt(q3[h], k3[h]) for h in range(q3.shape[0])]) * scale


def _half(x, hi):
    lo = _lane(x.shape) < HD
    return jnp.where(lo != hi, x, jnp.zeros_like(x))


def _pair_sum_store(o3, o_ref):
    for j in range(o3.shape[0] // 2):
        o_ref[:, j * LANES:(j + 1) * LANES] = (o3[2 * j] + o3[2 * j + 1]).astype(BF16)


def _a_heads(q, kdups, vdups):
    q3 = jnp.stack([_half(q[:, (h // 2) * LANES:(h // 2 + 1) * LANES], h % 2 == 1) for h in range(A_HEADS)])
    k3 = [kdups[h // (A_HEADS // A_KV)] for h in range(A_HEADS)]
    v3 = jnp.stack([_half(vdups[h // (A_HEADS // A_KV)], h % 2 == 1) for h in range(A_HEADS)])
    return q3, k3, v3


def _sink3(sink_ref):
    return [sink_ref[h] for h in range(A_HEADS)]


PROMPT_SEQS = 2


def _attn_a_prompt_kernel(sink_ref, q_ref, k_ref, v_ref, o_ref):
    for b in range(PROMPT_SEQS):
        rows = slice(b * S_P, (b + 1) * S_P)
        q = q_ref[rows, :]
        k = k_ref[rows, :]
        v = v_ref[rows, :]
        for j in range(A_HEADS // 2):
            acc = None
            for c in range(2):
                h = 2 * j + c
                g = h // (A_HEADS // A_KV)
                qh = _half(q[:, j * LANES:(j + 1) * LANES], c == 1)
                s = _dot_nt(qh, k[:, g * LANES:(g + 1) * LANES]) * (HD ** -0.5)
                vh = _half(v[:, g * LANES:(g + 1) * LANES], c == 1)
                o = _softmax_pv([s], [vh], sink_ref[h])
                acc = o if acc is None else acc + o
            o_ref[rows, j * LANES:(j + 1) * LANES] = acc.astype(BF16)


def _attn_a_prompt_call(sink, qa, ka, va):
    return pl.pallas_call(
        _attn_a_prompt_kernel,
        grid_spec=pltpu.PrefetchScalarGridSpec(
            num_scalar_prefetch=1, grid=(NB_P // PROMPT_SEQS,),
            in_specs=[pl.BlockSpec((PROMPT_SEQS * S_P, 384), lambda b, s: (b, 0)),
                      pl.BlockSpec((PROMPT_SEQS * S_P, 256), lambda b, s: (b, 0)),
                      pl.BlockSpec((PROMPT_SEQS * S_P, 256), lambda b, s: (b, 0))],
            out_specs=pl.BlockSpec((PROMPT_SEQS * S_P, 384), lambda b, s: (b, 0))),
        out_shape=jax.ShapeDtypeStruct((T_P, 384), BF16),
        compiler_params=_cparams(("arbitrary",)),
        name="attn_a_prompt",
    )(sink, qa, ka, va)


QB = 128
WIN = 128
BAND = QB + 2 * WIN


def _attn_a_sample_kernel(sink_ref, q_ref, k_ref, v_ref, kc_ref, vc_ref, o_ref):
    n = pl.program_id(1)
    ws = pl.multiple_of(jnp.clip(n * QB - WIN, 0, S_S - BAND), QB)
    kb = k_ref[pl.ds(ws, BAND), :]
    vb = v_ref[pl.ds(ws, BAND), :]
    qpos = n * QB + lax.broadcasted_iota(jnp.int32, (QB, BAND), 0)
    kpos = ws + lax.broadcasted_iota(jnp.int32, (QB, BAND), 1)
    ok = (jnp.abs(qpos - kpos) <= WIN)[None]
    groups = range(A_KV)
    q3, k3, v3 = _a_heads(q_ref[...], [kb[:, g * LANES:(g + 1) * LANES] for g in groups],
                          [vb[:, g * LANES:(g + 1) * LANES] for g in groups])

    def dup(x, g):
        xg = _half(x, g == 1)
        return (xg + pltpu.roll(xg, HD, 1)).astype(BF16)

    kc = kc_ref[0, 0]
    vc = vc_ref[0, 0]
    _, kc3, vc3 = _a_heads(q_ref[...], [dup(kc, g) for g in groups], [dup(vc, g) for g in groups])
    s_b = jnp.where(ok, _scores(q3, k3, HD ** -0.5), NEG)
    s_c = _scores(q3, kc3, HD ** -0.5)
    _pair_sum_store(_softmax_pv([s_b, s_c], [v3, vc3], _sink3(sink_ref)), o_ref)


def _attn_a_sample_call(sink, qa, ka, va, cache_k, cache_v, l):
    nqb = S_S // QB
    off = T_P // S_S
    return pl.pallas_call(
        _attn_a_sample_kernel,
        grid_spec=pltpu.PrefetchScalarGridSpec(
            num_scalar_prefetch=1, grid=(NB_S, nqb),
            in_specs=[pl.BlockSpec((QB, 384), lambda b, n, s: (T_P // QB + b * nqb + n, 0)),
                      pl.BlockSpec((S_S, 256), lambda b, n, s: (off + b, 0)),
                      pl.BlockSpec((S_S, 256), lambda b, n, s: (off + b, 0)),
                      pl.BlockSpec((1, 1, PAST, LANES), lambda b, n, s: (b, l, 0, 0)),
                      pl.BlockSpec((1, 1, PAST, LANES), lambda b, n, s: (b, l, 0, 0))],
            out_specs=pl.BlockSpec((QB, 384), lambda b, n, s: (b * nqb + n, 0))),
        out_shape=jax.ShapeDtypeStruct((T_S, 384), BF16),
        compiler_params=_cparams(("arbitrary", "arbitrary")),
        name=f"attn_a_sample_l{l}",
    )(sink, qa, ka, va, cache_k, cache_v)


def _mla_heads(q, ks, vs, o_ref):
    heads = range(C_HEADS)
    q3 = jnp.stack([q[:, h * LANES:(h + 1) * LANES] for h in heads])
    scores = [_scores(q3, [k[:, h * LANES:(h + 1) * LANES] for h in heads], C_QK ** -0.5) for k in ks]
    vals = [jnp.stack([_half(v[:, (h // 2) * LANES:(h // 2 + 1) * LANES], h % 2 == 1) for h in heads])
            for v in vs]
    _pair_sum_store(_softmax_pv(scores, vals, None), o_ref)


def _mla_prompt_kernel(q_ref, k_ref, v_ref, o_ref):
    for b in range(PROMPT_SEQS):
        rows = pl.ds(b * S_P, S_P)
        _mla_heads(q_ref[rows, :], [k_ref[rows, :]], [v_ref[rows, :]], o_ref.at[rows])


def _mla_prompt_call(qc, kc, vc):
    return pl.pallas_call(
        _mla_prompt_kernel,
        grid=(NB_P // PROMPT_SEQS,),
        in_specs=[pl.BlockSpec((PROMPT_SEQS * S_P, 768), lambda b: (b, 0)),
                  pl.BlockSpec((PROMPT_SEQS * S_P, 768), lambda b: (b, 0)),
                  pl.BlockSpec((PROMPT_SEQS * S_P, 384), lambda b: (b, 0))],
        out_specs=pl.BlockSpec((PROMPT_SEQS * S_P, 384), lambda b: (b, 0)),
        out_shape=jax.ShapeDtypeStruct((T_P, 384), BF16),
        compiler_params=_cparams(("arbitrary",)),
        name="mla_prompt",
    )(qc, kc, vc)


def _mla_sample_kernel(q_ref, kc_ref, vc_ref, kl_ref, vl_ref, o_ref):
    _mla_heads(q_ref[...], [kc_ref[...], kl_ref[...]], [vc_ref[...], vl_ref[...]], o_ref)


def _mla_sample_call(qc, kc, vc):
    tq = 256
    nq = S_S // tq
    return pl.pallas_call(
        _mla_sample_kernel,
        grid=(NB_S, nq),
        in_specs=[pl.BlockSpec((tq, 768), lambda b, n: (T_P // tq + b * nq + n, 0)),
                  pl.BlockSpec((PAST, 768), lambda b, n: (T // PAST + b, 0)),
                  pl.BlockSpec((PAST, 384), lambda b, n: (T // PAST + b, 0)),
                  pl.BlockSpec((S_S, 768), lambda b, n: (T_P // S_S + b, 0)),
                  pl.BlockSpec((S_S, 384), lambda b, n: (T_P // S_S + b, 0))],
        out_specs=pl.BlockSpec((tq, 384), lambda b, n: (b * nq + n, 0)),
        out_shape=jax.ShapeDtypeStruct((T_S, 384), BF16),
        compiler_params=_cparams(("arbitrary", "arbitrary")),
        name="mla_sample",
    )(qc, kc, vc, kc, vc)


def _split3(x):
    x1 = x.astype(BF16)
    r = x - x1.astype(F32)
    x2 = r.astype(BF16)
    x3 = (r - x2.astype(F32)).astype(BF16)
    return x1, x2, x3


def _mlstm_kernel(nc, has_state, *refs):
    for j in range(B_HEADS // 2):
        _mlstm_pair(nc, has_state, j, *refs)


def _mlstm_pair(nc, has_state, j, *refs):
    if has_state:
        (q_ref, k_ref, v_ref, bo_ref, li_ref, lf_ref, hn_ref, c0_ref, n0_ref, m0_ref, ob_ref) = refs
    else:
        (q_ref, k_ref, v_ref, bo_ref, li_ref, lf_ref, hn_ref, ob_ref, cn_ref, nn_ref, mn_ref) = refs
    s_len = nc * CHUNK
    pair = slice(j * LANES, (j + 1) * LANES)
    c3 = lambda x: x.reshape(nc, CHUNK, x.shape[-1])
    li3 = c3(li_ref[:, pair])
    lf3 = c3(lf_ref[:, pair])
    row = lax.broadcasted_iota(jnp.int32, (CHUNK, CHUNK), 0)
    colv = lax.broadcasted_iota(jnp.int32, (CHUNK, CHUNK), 1)
    tri_f = colv <= row
    tri_b = colv >= row
    bcast = lambda m: jnp.broadcast_to(m[None], (nc, CHUNK, CHUNK))
    lane1 = _lane((1, LANES))
    fwd_lane = lane1 < 2

    lf_parts = _split3(lf3)
    tf3 = bcast(tri_f.astype(BF16))
    tb3 = bcast(tri_b.astype(BF16))
    bc_f = _bmm(tf3, lf_parts[0]) + _bmm(tf3, lf_parts[1]) + _bmm(tf3, lf_parts[2])
    bc_b = _bmm(tb3, lf_parts[0]) + _bmm(tb3, lf_parts[1]) + _bmm(tb3, lf_parts[2])
    bc3 = jnp.where(fwd_lane, bc_f, bc_b)
    row3 = lax.broadcasted_iota(jnp.int32, (nc, CHUNK, LANES), 1)
    edge = jnp.where(row3 == jnp.where(fwd_lane, CHUNK - 1, 0), bc3, 0.0)
    bl3 = jnp.sum(edge, axis=1, keepdims=True)
    gg3 = bl3 - bc3 + li3
    bl2 = jnp.sum(edge, axis=1)
    mg2 = gg3.max(axis=1)

    m0 = m0_ref[j] if has_state else jnp.zeros((1, LANES), F32)
    mf = m0
    mb = m0
    mf_prev, mf_next, mb_prev, mb_next = {}, {}, {}, {}
    for i in range(nc):
        mf_prev[i] = mf
        mf = jnp.maximum(bl2[i:i + 1] + mf, mg2[i:i + 1])
        mf_next[i] = mf
        cb = nc - 1 - i
        mb_prev[cb] = mb
        mb = jnp.maximum(bl2[cb:cb + 1] + mb, mg2[cb:cb + 1])
        mb_next[cb] = mb
    m_prev = [jnp.where(fwd_lane, mf_prev[c], mb_prev[c]) for c in range(nc)]
    m_next = [jnp.where(fwd_lane, mf_next[c], mb_next[c]) for c in range(nc)]
    m_prev3 = jnp.stack(m_prev)
    m_next3 = jnp.stack(m_next)
    dec2d = jnp.exp(bl2 + jnp.concatenate(m_prev, axis=0) - jnp.concatenate(m_next, axis=0))
    if not has_state:
        mn_ref[j] = jnp.where(fwd_lane, mf, mb)

    u_parts = _split3(li3 - bc3)
    ws3 = jnp.exp(gg3 - m_next3)
    inter3 = bc3 + m_prev3

    q3 = c3(q_ref[:, pair])
    k3 = c3(k_ref[:, pair])
    v3 = c3(v_ref[:, pair])
    q3f = q3.astype(F32)
    k3f = k3.astype(F32)
    lane3 = _lane((nc, CHUNK, LANES))
    lo3 = lane3 < HD
    rr = lax.broadcasted_iota(jnp.int32, (LANES, LANES), 0)
    cc = lax.broadcasted_iota(jnp.int32, (LANES, LANES), 1)
    blockdiag = (rr < HD) == (cc < HD)
    col = lambda x, k: x[:, :, k:k + 1]

    hsum = None
    for d in range(2):
        causal3 = bcast(tri_f if d == 0 else tri_b)
        a_c, ws_sum, mt_c, ws_c, intra = [], [], [], [], None
        for e in range(2):
            kk = 2 * d + e
            pick = jnp.broadcast_to(jnp.where(_lane((CHUNK, LANES)) == kk, 1.0, 0.0).astype(BF16)[None],
                                    (nc, CHUNK, LANES))
            ub = _bmm_nt(pick, u_parts[0]) + _bmm_nt(pick, u_parts[1]) + _bmm_nt(pick, u_parts[2])
            bc_col = col(bc3, kk)
            d_mat = jnp.where(causal3, bc_col + ub, NEG)
            inter = col(inter3, kk)
            mt = jnp.maximum(inter, d_mat.max(axis=-1, keepdims=True))
            a = jnp.exp(inter - mt)
            w = jnp.exp(d_mat - mt) * _bmm_nt(_half(q3, e == 1), k3)
            o = _bmm(w.astype(BF16), _half(v3, e == 1))
            intra = o if intra is None else intra + o
            a_c.append(a)
            mt_c.append(mt)
            ws_sum.append(w.sum(axis=-1, keepdims=True))
            ws_c.append(col(ws3, kk))
        kw3 = k3f * jnp.where(lo3, ws_c[0], ws_c[1])
        dec2 = jnp.where(_lane((nc, LANES)) < HD, dec2d[:, 2 * d:2 * d + 1], dec2d[:, 2 * d + 1:2 * d + 2])
        u_all = jnp.where(blockdiag, _bmm_tn(kw3.astype(BF16), v3), 0.0)
        kwsum = kw3.sum(axis=1)
        if has_state:
            zero = jnp.zeros((HD, HD), F32)
            cst = jnp.concatenate([jnp.concatenate([c0_ref[0, d, 2 * j], zero], axis=1),
                                   jnp.concatenate([zero, c0_ref[0, d, 2 * j + 1]], axis=1)], axis=0)
            nst = jnp.concatenate([n0_ref[0, d, 2 * j:2 * j + 1, :], n0_ref[0, d, 2 * j + 1:2 * j + 2, :]],
                                  axis=1)
        else:
            cst = jnp.zeros((LANES, LANES), F32)
            nst = jnp.zeros((1, LANES), F32)
        cs, ns = [None] * nc, [None] * nc
        for c in (range(nc) if d == 0 else range(nc - 1, -1, -1)):
            cs[c] = cst
            ns[c] = nst
            cst = dec2[c:c + 1] * cst + u_all[c]
            nst = dec2[c:c + 1] * nst + kwsum[c:c + 1]
        if not has_state:
            for e in range(2):
                half = slice(e * HD, (e + 1) * HD)
                cn_ref[0, d, 2 * j + e] = cst[half, half]
                nn_ref[0, d, 2 * j + e:2 * j + e + 1, :] = nst[:, half]
        qc = _bmm(q3, jnp.stack(cs).astype(BF16))
        qn_all = q3f * jnp.stack(ns)
        dn = []
        for e in range(2):
            qn = jnp.sum(jnp.where(lo3 != (e == 1), qn_all, 0.0), axis=-1, keepdims=True)
            den = a_c[e] * qn + ws_sum[e]
            dn.append(jnp.maximum(jnp.abs(den), jnp.exp(-mt_c[e])))
        h2 = (jnp.where(lo3, a_c[0], a_c[1]) * qc + intra) / jnp.where(lo3, dn[0], dn[1])
        hsum = h2 if hsum is None else hsum + h2

    hs = hsum.reshape(s_len, LANES)
    lo = _lane(hs.shape) < HD
    s = hs * hs
    s_lo = jnp.sum(jnp.where(lo, s, 0.0), axis=-1, keepdims=True)
    s_hi = jnp.sum(jnp.where(lo, 0.0, s), axis=-1, keepdims=True)
    r = lax.rsqrt(jnp.where(lo, s_lo, s_hi) * (1.0 / HD) + EPS)
    ob_ref[:, pair] = (hs * r * hn_ref[...] * jax.nn.sigmoid(bo_ref[:, pair])).astype(BF16)


def _mlstm_call(bq, bk, bv, bo, li, lf, hn2, state, l):
    has_state = state is not None
    if has_state:
        nb, s_len, base = NB_S, S_S, T_P // S_S
    else:
        nb, s_len, base = NB_P, S_P, 0
    nc = s_len // CHUNK
    tokp = pl.BlockSpec((s_len, 2 * LANES), lambda b: (base + b, 0))
    in_specs = [tokp, tokp, tokp, tokp, tokp, tokp, pl.BlockSpec((1, LANES), lambda b: (0, 0))]
    args = [bq, bk, bv, bo, li, lf, hn2]
    ob_spec = pl.BlockSpec((s_len, 2 * LANES), lambda b: (b, 0))
    ob_shape = jax.ShapeDtypeStruct((nb * s_len, 256), BF16)
    state_specs = [pl.BlockSpec((1, 2, B_HEADS, HD, HD), lambda b: (b, 0, 0, 0, 0)),
                   pl.BlockSpec((1, 2, B_HEADS, HD), lambda b: (b, 0, 0, 0)),
                   pl.BlockSpec((2, 1, LANES), lambda b: (b, 0, 0))]
    if has_state:
        in_specs += state_specs
        args += list(state)
        out_specs = ob_spec
        out_shape = ob_shape
    else:
        out_specs = [ob_spec] + state_specs
        out_shape = [ob_shape,
                     jax.ShapeDtypeStruct((nb, 2, B_HEADS, HD, HD), F32),
                     jax.ShapeDtypeStruct((nb, 2, B_HEADS, HD), F32),
                     jax.ShapeDtypeStruct((nb * 2, 1, LANES), F32)]
    return pl.pallas_call(
        functools.partial(_mlstm_kernel, nc, has_state),
        grid=(nb,),
        in_specs=in_specs,
        out_specs=out_specs,
        out_shape=out_shape,
        compiler_params=_cparams(("arbitrary",)),
        name=f"mlstm_{'sample' if has_state else 'prompt'}_l{l}",
    )(*args)


def _out_kernel(moe, xp_ref, xs_ref, oap_ref, oas_ref, obp_ref, obs_ref, ocp_ref, ocs_ref,
                w_ref, g1_ref, sh_ref, sc_ref, n2_ref, *rest):
    if moe:
        rh_ref, rl_ref, x1_o, xn_o, route_o, cnt_o, run_scr = rest
    else:
        x1_o, xn_o = rest
    i = pl.program_id(0)
    o = (_dot(_pick(i, oap_ref, oas_ref), w_ref[0, 0:384, :])
         + _dot(_pick(i, obp_ref, obs_ref), w_ref[0, 384:640, :])
         + _dot(_pick(i, ocp_ref, ocs_ref), w_ref[0, 640:1024, :]))
    x1 = _pick(i, xp_ref, xs_ref) + g1_ref[0] * o
    x1_o[...] = x1
    xn = x1 * lax.rsqrt(jnp.mean(x1 * x1, axis=-1, keepdims=True) + EPS) * n2_ref[...]
    xn = xn * (1.0 + sc_ref[0]) + sh_ref[0]
    xb = xn.astype(BF16)
    if not moe:
        xn_o[...] = xb
    else:
        xn_o[...] = xn
        xl = (xn - xb.astype(F32)).astype(BF16)
        logits = _dot(xb, rh_ref[0]) + (_dot(xl, rh_ref[0]) + _dot(xb, rl_ref[0]))
        lane = _lane(logits.shape)
        logits = jnp.where(lane < N_EXP, logits, -jnp.inf)
        m1 = logits.max(axis=-1, keepdims=True)
        i1 = jnp.min(jnp.where(logits == m1, lane, LANES), axis=-1, keepdims=True)
        rest_l = jnp.where(lane == i1, -jnp.inf, logits)
        m2 = rest_l.max(axis=-1, keepdims=True)
        i2 = jnp.min(jnp.where(rest_l == m2, lane, LANES), axis=-1, keepdims=True)
        e2 = jnp.exp(m2 - m1)
        den = 1.0 + e2
        w1 = 1.0 / den
        w2 = e2 / den

        @pl.when(pl.program_id(0) == 0)
        def _():
            run_scr[...] = jnp.zeros(run_scr.shape, F32)

        sel = jnp.where(lane == i1, 1.0, jnp.where((lane == i2) & (w2 > 0.0), 1.0, 0.0))
        rb = 256
        before = (lax.broadcasted_iota(jnp.int32, (rb, rb), 1)
                  < lax.broadcasted_iota(jnp.int32, (rb, rb), 0)).astype(BF16)
        run = run_scr[...]
        ranks = []
        for k in range(TM // rb):
            part = sel[k * rb:(k + 1) * rb]
            ranks.append(run + _dot(before, part.astype(BF16)))
            run = run + jnp.sum(part, axis=0, keepdims=True)
        rank = jnp.concatenate(ranks, axis=0)
        r1 = jnp.sum(jnp.where(lane == i1, rank, 0.0), axis=-1, keepdims=True)
        r2 = jnp.sum(jnp.where(lane == i2, rank, 0.0), axis=-1, keepdims=True)
        run_scr[...] = run
        cnt_o[...] = run_scr[...]
        fields = [i1.astype(F32), i2.astype(F32), r1, r2, w1, w2]
        info = jnp.zeros(lane.shape, F32)
        for k, v in enumerate(fields):
            info = jnp.where(lane == k, v, info)
        route_o[...] = info.T


def _out_call(x_p, x_s, sample_off, oa, ob, oc, w_out_b, mod3, l, n2g, router):
    moe = router is not None
    row = lambda k: pl.BlockSpec((1, 1, D), lambda i: ((l * 8 + _mod_row(i)) * 6 + k, 0, 0))
    tok = lambda n: pl.BlockSpec((TM, n), lambda i: (i, 0))
    in_specs = (_pair_specs(D, sample_off) + _pair_specs(384, 0) + _pair_specs(256, 0) + _pair_specs(384, 0)
                + [pl.BlockSpec((1, D, D), lambda i: (0, 0, 0)),
                   row(2), row(3), row(4), pl.BlockSpec((1, D), lambda i: (0, 0))])
    args = [x_p, x_s, *oa, *ob, *oc, w_out_b, mod3, mod3, mod3, n2g]
    out_specs = [tok(D), tok(D)]
    out_shape = [jax.ShapeDtypeStruct((T, D), F32), jax.ShapeDtypeStruct((T, D), BF16)]
    if moe:
        out_shape[1] = jax.ShapeDtypeStruct((T, D), F32)
        rh, rl = router
        in_specs += [pl.BlockSpec((1, D, LANES), lambda i: (0, 0, 0))] * 2
        args += [rh, rl]
        out_specs += [pl.BlockSpec((LANES, TM), lambda i: (0, i)), pl.BlockSpec((1, LANES), lambda i: (0, 0))]
        out_shape += [jax.ShapeDtypeStruct((LANES, T), F32), jax.ShapeDtypeStruct((1, LANES), F32)]
    return pl.pallas_call(
        functools.partial(_out_kernel, moe),
        grid=(NT,),
        in_specs=in_specs, out_specs=out_specs, out_shape=out_shape,
        scratch_shapes=[pltpu.VMEM((1, LANES), F32)] if moe else [],
        compiler_params=_cparams(("arbitrary",)),
        name=f"out_proj_l{l}",
    )(*args)


FFN_TM = 1024
FFN_TF = 256
FFN_TN = 512


def _hidden_tile(x, w1_ref, w3_ref):
    a = _dot(x, w1_ref[...].astype(BF16))
    b = _dot(x, w3_ref[...].astype(BF16))
    return (a * jax.nn.sigmoid(a) * b).astype(BF16)


def _down_tile(h_scr, w2n, m):
    out = None
    for f in range(h_scr.shape[0]):
        o = _dot(h_scr[f, 0:m, :], w2n[f])
        out = o if out is None else out + o
    return out


def _keep_weights(s, w1_ref, w3_ref, w2_ref, w1b, w3b, w2b, tn):
    w1b[s] = w1_ref[...].astype(BF16)
    w3b[s] = w3_ref[...].astype(BF16)
    w2 = w2_ref[...].astype(BF16)
    for n in range(w2b.shape[0]):
        w2b[n, s] = w2[:, n * tn:(n + 1) * tn]


def _ffn_kernel(nf, xn_ref, x1_ref, g2_ref, w1_ref, w3_ref, w2_ref, y_ref, h_scr, w1b, w3b, w2b):
    i = pl.program_id(0)
    s = pl.program_id(1)

    @pl.when((i == 0) & (s < nf))
    def _():
        _keep_weights(s, w1_ref.at[0], w3_ref.at[0], w2_ref.at[0], w1b, w3b, w2b, FFN_TN)

    @pl.when(s < nf)
    def _():
        h_scr[s] = _hidden_tile(xn_ref[...], w1b.at[s], w3b.at[s])

    @pl.when(s >= nf)
    def _():
        y_ref[...] = x1_ref[...] + g2_ref[0] * _down_tile(h_scr, w2b.at[s - nf], FFN_TM)


def _ffn_call(xn, x1, mod3, l, w1, w3, w2, i_layer):
    nf = D_FF // FFN_TF
    nn = D // FFN_TN
    tile = lambda i, s: jnp.where(i == 0, jnp.minimum(s, nf - 1), nf - 1)
    out = lambda s: jnp.maximum(s - nf, 0)

    def g2_idx(i, s):
        r = jnp.where(i < T_P // FFN_TM, 0, 1 + (i - T_P // FFN_TM) // (S_S // FFN_TM))
        return ((l * 8 + r) * 6 + 5, 0, out(s))

    return pl.pallas_call(
        functools.partial(_ffn_kernel, nf),
        grid=(T // FFN_TM, nf + nn),
        in_specs=[pl.BlockSpec((FFN_TM, D), lambda i, s: (i, 0)),
                  pl.BlockSpec((FFN_TM, FFN_TN), lambda i, s: (i, out(s))),
                  pl.BlockSpec((1, 1, FFN_TN), g2_idx),
                  pl.BlockSpec((1, D, FFN_TF), lambda i, s: (i_layer, 0, tile(i, s))),
                  pl.BlockSpec((1, D, FFN_TF), lambda i, s: (i_layer, 0, tile(i, s))),
                  pl.BlockSpec((1, FFN_TF, D), lambda i, s: (i_layer, tile(i, s), 0))],
        out_specs=pl.BlockSpec((FFN_TM, FFN_TN), lambda i, s: (i, out(s))),
        out_shape=jax.ShapeDtypeStruct((T, D), F32),
        scratch_shapes=[pltpu.VMEM((nf, FFN_TM, FFN_TF), BF16),
                        pltpu.VMEM((nf, D, FFN_TF), BF16), pltpu.VMEM((nf, D, FFN_TF), BF16),
                        pltpu.VMEM((nn, nf, FFN_TF, FFN_TN), BF16)],
        compiler_params=_cparams(("arbitrary", "arbitrary")),
        name="ffn_dense",
    )(xn, x1, mod3, w1, w3, w2)


MOE_BM = 1024
MOE_SUB = 256
MOE_NBLK = 2 * T // MOE_BM + N_EXP
MOE_NR = MOE_NBLK * MOE_BM
MOE_TF = 512
MOE_TN = 512
FIN_TM = 512


def _route_kernel(i1_ref, i2_ref, r1_ref, r2_ref, cnt_ref, w2_ref,
                  src_ref, eb_ref, rows_ref, nvb_ref, off_scr):
    def clear(r, carry):
        src_ref[r] = 0
        return carry

    lax.fori_loop(0, MOE_NR + 8, clear, 0, unroll=8)

    def clear_blk(b, carry):
        eb_ref[b] = 0
        rows_ref[b] = 0
        return carry

    lax.fori_loop(0, MOE_NBLK, clear_blk, 0)

    nblk = jnp.int32(0)
    for e in range(N_EXP):
        c = cnt_ref[e]
        nbe = (c + (MOE_BM - 1)) // MOE_BM
        off_scr[e] = nblk * MOE_BM

        def fill(j, carry, e=e, c=c, nblk=nblk):
            eb_ref[nblk + j] = e
            rows_ref[nblk + j] = jnp.minimum(c - j * MOE_BM, MOE_BM)
            return carry

        lax.fori_loop(0, nbe, fill, 0)
        nblk = nblk + nbe
    nvb_ref[0] = nblk

    def place(t, carry):
        src_ref[off_scr[i1_ref[t]] + r1_ref[t]] = t
        src_ref[jnp.where(w2_ref[t] > 0.0, off_scr[i2_ref[t]] + r2_ref[t], MOE_NR)] = t
        return carry

    lax.fori_loop(0, T, place, 0, unroll=4)


def _route_call(route, cnt):
    ints = [route[k].astype(jnp.int32) for k in range(4)]
    cnt8 = cnt[0, :N_EXP].astype(jnp.int32)
    smem = pl.BlockSpec(memory_space=pltpu.SMEM)
    outs = pl.pallas_call(
        _route_kernel,
        grid_spec=pltpu.PrefetchScalarGridSpec(
            num_scalar_prefetch=5, grid=(1,),
            in_specs=[smem],
            out_specs=[smem, smem, smem, smem],
            scratch_shapes=[pltpu.SMEM((N_EXP,), jnp.int32)]),
        out_shape=[jax.ShapeDtypeStruct((MOE_NR + 8,), jnp.int32),
                   jax.ShapeDtypeStruct((MOE_NBLK,), jnp.int32), jax.ShapeDtypeStruct((MOE_NBLK,), jnp.int32),
                   jax.ShapeDtypeStruct((1,), jnp.int32)],
        compiler_params=_cparams(("arbitrary",)),
        name="moe_route",
    )(*ints, cnt8, route[5])
    return tuple(outs) + (ints[0], route[4], route[5])


def _moe_gather_kernel(src_ref, rows_ref, x_ref, o_ref, rows_scr):
    b = pl.program_id(0)
    for q in range(MOE_BM // MOE_SUB):
        base = b * MOE_BM + q * MOE_SUB
        used = rows_ref[b] > q * MOE_SUB
        sub = pl.ds(q * MOE_SUB, MOE_SUB)

        @pl.when(used)
        def _(base=base, sub=sub):
            def body(i, carry):
                for j in range(8):
                    r = i * 8 + j
                    rows_scr[pl.ds(r, 1), :] = x_ref[pl.ds(src_ref[base + r], 1), :]
                return carry

            lax.fori_loop(0, MOE_SUB // 8, body, 0)
            o_ref[sub, :] = rows_scr[...].astype(BF16)

        @pl.when(jnp.logical_not(used))
        def _(sub=sub):
            o_ref[sub, :] = jnp.zeros((MOE_SUB, D), BF16)


def _moe_gather_call(src, rows_b, xn):
    return pl.pallas_call(
        _moe_gather_kernel,
        grid_spec=pltpu.PrefetchScalarGridSpec(
            num_scalar_prefetch=2, grid=(MOE_NBLK,),
            in_specs=[pl.BlockSpec((T, D), lambda b, s, rw: (0, 0), pipeline_mode=pl.Buffered(1))],
            out_specs=pl.BlockSpec((MOE_BM, D), lambda b, s, rw: (b, 0)),
            scratch_shapes=[pltpu.VMEM((MOE_SUB, D), F32)]),
        out_shape=jax.ShapeDtypeStruct((MOE_NR, D), BF16),
        compiler_params=_cparams(("arbitrary",)),
        name="moe_gather",
    )(src, rows_b, xn)


def _moe_first(b, eb_ref):
    return (b == 0) | (eb_ref[b] != eb_ref[jnp.maximum(b - 1, 0)])


def _moe_kernel(nf, eb_ref, rows_ref, nvb_ref, xs_ref, w1_ref, w3_ref, w2_ref, y_ref, h_scr, w1b, w3b, w2b):
    b = pl.program_id(0)
    s = pl.program_id(1)
    nsub = jnp.where(b < nvb_ref[0], (rows_ref[b] + MOE_SUB - 1) // MOE_SUB, 0)
    sizes = [k * MOE_SUB for k in range(1, MOE_BM // MOE_SUB + 1)]

    @pl.when((s < nf) & (nsub > 0) & _moe_first(b, eb_ref))
    def _():
        _keep_weights(s, w1_ref.at[0, 0], w3_ref.at[0, 0], w2_ref.at[0, 0], w1b, w3b, w2b, MOE_TN)

    @pl.when((s < nf) & (nsub > 0))
    def _():
        for k, m in enumerate(sizes, start=1):
            @pl.when(nsub == k)
            def _(m=m):
                h_scr[s, 0:m, :] = _hidden_tile(xs_ref[0:m, :], w1b.at[s], w3b.at[s])

    @pl.when((s >= nf) & (nsub > 0))
    def _():
        for k, m in enumerate(sizes, start=1):
            @pl.when(nsub == k)
            def _(m=m):
                y_ref[0:m, :] = _down_tile(h_scr, w2b.at[s - nf], m)
                if m < MOE_BM:
                    y_ref[m:MOE_BM, :] = jnp.zeros((MOE_BM - m, MOE_TN), F32)

    @pl.when((s >= nf) & (nsub == 0))
    def _():
        y_ref[...] = jnp.zeros(y_ref.shape, F32)


def _moe_call(xs, e_b, rows_b, nvb, w1, w3, w2, i_layer):
    nf = D_FFE // MOE_TF
    nn = D // MOE_TN

    def blk(b, nv):
        return jnp.minimum(b, nv[0] - 1)

    def tile(b, s, eb, nv):
        stream = (b < nv[0]) & _moe_first(b, eb)
        return jnp.where(stream, jnp.minimum(s, nf - 1), nf - 1)

    return pl.pallas_call(
        functools.partial(_moe_kernel, nf),
        grid_spec=pltpu.PrefetchScalarGridSpec(
            num_scalar_prefetch=3, grid=(MOE_NBLK, nf + nn),
            in_specs=[pl.BlockSpec((MOE_BM, D), lambda b, s, eb, rw, nv: (blk(b, nv), 0)),
                      pl.BlockSpec((1, 1, D, MOE_TF),
                                   lambda b, s, eb, rw, nv: (i_layer, eb[blk(b, nv)], 0, tile(b, s, eb, nv))),
                      pl.BlockSpec((1, 1, D, MOE_TF),
                                   lambda b, s, eb, rw, nv: (i_layer, eb[blk(b, nv)], 0, tile(b, s, eb, nv))),
                      pl.BlockSpec((1, 1, MOE_TF, D),
                                   lambda b, s, eb, rw, nv: (i_layer, eb[blk(b, nv)], tile(b, s, eb, nv), 0))],
            out_specs=pl.BlockSpec((MOE_BM, MOE_TN), lambda b, s, eb, rw, nv: (b, jnp.maximum(s - nf, 0))),
            scratch_shapes=[pltpu.VMEM((nf, MOE_BM, MOE_TF), BF16),
                            pltpu.VMEM((nf, D, MOE_TF), BF16), pltpu.VMEM((nf, D, MOE_TF), BF16),
                            pltpu.VMEM((nn, nf, MOE_TF, MOE_TN), BF16)]),
        out_shape=jax.ShapeDtypeStruct((MOE_NR, D), F32),
        compiler_params=_cparams(("arbitrary", "arbitrary"), vmem=56 * 1024 * 1024),
        name="moe_experts",
    )(e_b, rows_b, nvb, xs, w1, w3, w2)


def _moe_combine_kernel(src_ref, eb_ref, rows_ref, nvb_ref, i1_ref, w1_ref, w2_ref,
                        ys_ref, x1_ref, g2_ref, yp_ref, ys_out_ref, acc):
    i = pl.program_id(0)

    @pl.when(i == 0)
    def _():
        acc[...] = jnp.zeros(acc.shape, F32)

    @pl.when(i < nvb_ref[0])
    def _():
        base = i * MOE_BM
        n = rows_ref[i]
        e = eb_ref[i]

        def gate(t):
            return jnp.where(i1_ref[t] == e, w1_ref[t], w2_ref[t])

        def add_rows(r0, cnt):
            toks = [src_ref[base + r0 + j] for j in range(cnt)]
            vals = [acc[pl.ds(toks[j], 1), :] + gate(toks[j]) * ys_ref[pl.ds(r0 + j, 1), :]
                    for j in range(cnt)]
            for j in range(cnt):
                acc[pl.ds(toks[j], 1), :] = vals[j]

        def body4(q, carry):
            add_rows(q * 4, 4)
            return carry

        lax.fori_loop(0, n // 4, body4, 0)

        def body1(r, carry):
            add_rows(r, 1)
            return carry

        lax.fori_loop((n // 4) * 4, n, body1, 0)

    @pl.when(i >= MOE_NBLK)
    def _():
        t0 = pl.multiple_of((i - MOE_NBLK) * FIN_TM, FIN_TM)
        y = x1_ref[...] + g2_ref[0] * acc[pl.ds(t0, FIN_TM), :]

        @pl.when(i - MOE_NBLK < T_P // FIN_TM)
        def _():
            yp_ref[...] = y

        @pl.when(i - MOE_NBLK >= T_P // FIN_TM)
        def _():
            ys_out_ref[...] = y


def _moe_combine_call(src, e_b, rows_b, nvb, i1, w1, w2, ys, x1, mod3, l):
    nfin = T // FIN_TM
    smem = pl.BlockSpec(memory_space=pltpu.SMEM)

    def g2_idx(i, *_):
        j = jnp.maximum(i - MOE_NBLK, 0)
        r = jnp.where(j < T_P // FIN_TM, 0, 1 + (j - T_P // FIN_TM) // (S_S // FIN_TM))
        return ((l * 8 + r) * 6 + 5, 0, 0)

    return pl.pallas_call(
        _moe_combine_kernel,
        grid_spec=pltpu.PrefetchScalarGridSpec(
            num_scalar_prefetch=5, grid=(MOE_NBLK + nfin,),
            in_specs=[smem, smem,
                      pl.BlockSpec((MOE_BM, D),
                                   lambda i, s, eb, rw, nv, t1: (jnp.minimum(jnp.minimum(i, MOE_NBLK - 1), nv[0] - 1), 0)),
                      pl.BlockSpec((FIN_TM, D), lambda i, *_: (jnp.maximum(i - MOE_NBLK, 0), 0)),
                      pl.BlockSpec((1, 1, D), g2_idx)],
            out_specs=[pl.BlockSpec((FIN_TM, D),
                                    lambda i, *_: (jnp.clip(i - MOE_NBLK, 0, T_P // FIN_TM - 1), 0)),
                       pl.BlockSpec((FIN_TM, D),
                                    lambda i, *_: (jnp.maximum(i - MOE_NBLK - T_P // FIN_TM, 0), 0))],
            scratch_shapes=[pltpu.VMEM((T, D), F32)]),
        out_shape=[jax.ShapeDtypeStruct((T_P, D), F32), jax.ShapeDtypeStruct((T_S, D), F32)],
        compiler_params=_cparams(("arbitrary",), vmem=52 * 1024 * 1024),
        name="moe_combine",
    )(src, e_b, rows_b, nvb, i1, w1, w2, ys, x1, mod3)


def _moe_layer(xp, x1, route, cnt, mod3, l, w1, w3, w2, i_layer):
    src, e_b, rows_b, nvb, i1, g1, g2 = _route_call(route, cnt)
    xs = _moe_gather_call(src, rows_b, xp)
    ys = _moe_call(xs, e_b, rows_b, nvb, w1, w3, w2, i_layer)
    return _moe_combine_call(src, e_b, rows_b, nvb, i1, g1, g2, ys, x1, mod3, l)


def _pad_cols(w, n):
    return jnp.pad(w, ((0, 0), (0, n - w.shape[1])))


def _tail_selector():
    sel = np.zeros((432, 1024), np.float32)
    for d in range(2):
        for gate in range(2):
            for head in range(B_HEADS):
                sel[d * 8 + gate * 4 + head, gate * 256 + (head // 2) * LANES + d * 2 + head % 2] = 1.0
    for c in range(C_QRANK):
        sel[16 + c, 512 + c] = 1.0
    for c in range(C_KVRANK):
        sel[16 + C_QRANK + c, 768 + c] = 1.0
    for c in range(C_ROPE):
        sel[16 + C_QRANK + C_KVRANK + c, 896 + C_NOPE + c] = 1.0
    return sel


def _relayout_w_tail(w):
    sel = jnp.asarray(_tail_selector(), BF16)
    return jnp.dot(w[:, IN_MAIN:].astype(BF16), sel, preferred_element_type=BF16)


def _relayout_gate_b(b):
    return jnp.dot(b[None, :], jnp.asarray(_tail_selector()[:16, :512]), precision=lax.Precision.HIGHEST)


def _pad_heads(w, width):
    r = w.shape[0]
    h = w.shape[1] // width
    return jnp.pad(w.reshape(r, h, width), ((0, 0), (0, 0), (0, LANES - width))).reshape(r, h * LANES)


def _rope_tables(half, span_start, period):
    rows = S_S // 64
    r = jnp.repeat(jnp.arange(rows), 64).astype(F32)
    c = jnp.tile(jnp.arange(64), rows).astype(F32)
    n_freq = half // 2
    freq = 10000.0 ** (-jnp.arange(n_freq, dtype=F32) / n_freq)
    ang = jnp.concatenate([r[:, None] * freq, c[:, None] * freq], axis=-1)
    cos, sin = jnp.cos(ang), jnp.sin(ang)
    d = (jnp.arange(LANES) - span_start) % period
    inside = d < 2 * half
    p = jnp.where(inside, d % half, 0)
    first = inside & (d < half)
    second = inside & (d >= half)
    ct = jnp.where(inside[None, :], cos[:, p], 1.0)
    sa = jnp.where(first[None, :], -sin[:, p], 0.0)
    sb = jnp.where(second[None, :], sin[:, p], 0.0)
    ident = (jnp.ones((TM, LANES), F32), jnp.zeros((TM, LANES), F32), jnp.zeros((TM, LANES), F32))
    return tuple(jnp.concatenate([i0, t], axis=0) for i0, t in zip(ident, (ct, sa, sb)))


def kernel(x_prompt, x_sample, c, cache_swa_k, cache_swa_v, cache_mla_ckv, cache_mla_krope, state_mlstm_C, state_mlstm_n, state_mlstm_m, c_ctx, ada_w, ada_b, norm1_g, norm2_g, w_in, a_qn_g, a_kn_g, a_sink, b_gate_b, b_hn_g, c_qa_g, c_kva_g, c_wuq, c_wukv, c_qn_g, c_kn_g, w_out, ffn_w1, ffn_w3, ffn_w2, moe_router, moe_w1, moe_w3, moe_w2):
    x_p, x_s, sample_off = x_prompt.reshape(T_P, D), x_sample.reshape(T_S, D), 0
    cv = jnp.concatenate([c_ctx[None, :], c, jnp.zeros((5, D), F32)], axis=0)
    mod = _ada_call(cv, ada_w, ada_b)
    mod3 = mod.reshape(DEPTH * 8 * 6, 1, D)

    rope_a = _rope_tables(32, 0, HD)
    rope_c = _rope_tables(16, C_NOPE, LANES)

    cache_k4 = cache_swa_k.reshape(NB_S, DEPTH, PAST, A_KV * HD)
    cache_v4 = cache_swa_v.reshape(NB_S, DEPTH, PAST, A_KV * HD)

    news = []
    for l in range(DEPTH):
        w_tail = _relayout_w_tail(w_in[l])
        wuq_p = _pad_heads(c_wuq[l], C_QK).astype(BF16)[None]
        wukv = c_wukv[l].reshape(C_KVRANK, C_HEADS, C_NOPE + C_V)
        wk_p = jnp.pad(wukv[..., :C_NOPE], ((0, 0), (0, 0), (0, LANES - C_NOPE)))
        wk_p = wk_p.reshape(1, C_KVRANK, C_HEADS * LANES).astype(BF16)
        wv_p = wukv[..., C_NOPE:].reshape(1, C_KVRANK, C_HEADS * C_V).astype(BF16)
        w_out_b = w_out[l].astype(BF16)[None]
        gqa = jnp.tile(a_qn_g[l], A_HEADS)[None, :]
        gka = jnp.tile(a_kn_g[l], A_KV)[None, :]
        gb = _relayout_gate_b(b_gate_b[l])
        gqn = jnp.tile(jnp.pad(c_qn_g[l], (0, LANES - C_QK)), C_HEADS)[None, :]
        gkn = jnp.tile(jnp.pad(c_kn_g[l], (0, LANES - C_QK)), C_HEADS)[None, :]
        (qa, ka, va, kaf, vaf, bq, bk, bv, bo, li, lf, qc, ckv_n, ckr) = _in_call(
            x_p, x_s, sample_off, mod3, l, norm1_g[l][None, :], w_in, w_tail, gqa, gka, gb, c_qa_g[l][None, :],
            c_kva_g[l][None, :], wuq_p, gqn, rope_a, rope_c)

        ckr_cache = jnp.pad(cache_mla_krope[:, l].reshape(NB_S * PAST, C_ROPE),
                            ((0, 0), (C_NOPE, LANES - C_NOPE - C_ROPE)))
        kc, vc = _mlakv_call(ckv_n, cache_mla_ckv[:, l].reshape(NB_S * PAST, C_KVRANK), ckr, ckr_cache,
                             l, wk_p, wv_p, gkn, rope_c)

        sink = a_sink[l]
        oa_p = _attn_a_prompt_call(sink, qa, ka, va)
        oa_s = _attn_a_sample_call(sink, qa, ka, va, cache_k4, cache_v4, l)
        oc_p = _mla_prompt_call(qc, kc, vc)
        oc_s = _mla_sample_call(qc, kc, vc)

        hn2 = jnp.tile(b_hn_g[l], 2)[None, :]
        ob_p, cn, nn, mn = _mlstm_call(bq, bk, bv, bo, li, lf, hn2, None, l)
        c0, n0 = state_mlstm_C[:, l], state_mlstm_n[:, l]
        m_st = state_mlstm_m[:, l].reshape(NB_S, 2, 2, 2)
        m0 = jnp.transpose(m_st, (0, 2, 1, 3)).reshape(NB_S * 2, 1, 4)
        m0 = jnp.pad(m0, ((0, 0), (0, 0), (0, LANES - 4)))
        ob_s = _mlstm_call(bq, bk, bv, bo, li, lf, hn2, (c0, n0, m0), l)

        oa, ob, oc = (oa_p, oa_s), (ob_p, ob_s), (oc_p, oc_s)

        if l % 2 == 0:
            x1, xn = _out_call(x_p, x_s, sample_off, oa, ob, oc, w_out_b, mod3, l, norm2_g[l][None, :], None)
            x = _ffn_call(xn, x1, mod3, l, ffn_w1, ffn_w3, ffn_w2, l // 2)
            x_p, x_s, sample_off = x, x, NT_P
        else:
            r = _pad_cols(moe_router[l // 2], LANES)
            rh = r.astype(BF16)
            rl = (r - rh.astype(F32)).astype(BF16)
            x1, xn, route, cnt = _out_call(x_p, x_s, sample_off, oa, ob, oc, w_out_b, mod3, l,
                                           norm2_g[l][None, :], (rh[None], rl[None]))
            x_p, x_s = _moe_layer(xn, x1, route, cnt, mod3, l, moe_w1, moe_w3, moe_w2, l // 2)
            sample_off = 0

        new_k = kaf.reshape(NB_P, S_P, A_KV, HD)
        new_v = vaf.reshape(NB_P, S_P, A_KV, HD)
        new_ckv = ckv_n[:T_P].reshape(NB_P, S_P, C_KVRANK)
        new_kr = ckr[:T_P, C_NOPE:C_NOPE + C_ROPE].reshape(NB_P, S_P, C_ROPE)
        new_c, new_n = cn, nn
        mn4 = mn.reshape(NB_P, 2, LANES)[:, :, :4].reshape(NB_P, 2, 2, 2)
        new_m = jnp.transpose(mn4, (0, 2, 1, 3)).reshape(NB_P, 2, B_HEADS)
        news.append((new_k, new_v, new_ckv, new_kr, new_c, new_n, new_m))

    y_prompt = x_p[:T_P].reshape(NB_P, S_P, D)
    y_sample = x_s[sample_off * TM:sample_off * TM + T_S].reshape(NB_S, S_S, D)
    stacked = tuple(jnp.stack([nw[j] for nw in news], axis=1) for j in range(7))
    return (y_prompt, y_sample) + stacked
```

```python
import functools

import jax
import jax.numpy as jnp
import numpy as np
from jax import lax
from jax.experimental import pallas as pl
from jax.experimental.pallas import tpu as pltpu

F32 = jnp.float32
BF16 = jnp.bfloat16

D = 1024
NB_P, S_P = 16, 256
NB_S, S_S = 2, 1024
PAST = 512
DEPTH = 2
T_P = NB_P * S_P
T_S = NB_S * S_S
T = T_P + T_S
TM = 512
NT = T // TM
NT_P = T_P // TM
HD = 64
A_HEADS, A_KV = 6, 2
B_HEADS = 4
CHUNK = 64
C_HEADS = 6
C_QRANK, C_KVRANK, C_NOPE, C_ROPE, C_V = 256, 128, 64, 32, 64
C_QK = C_NOPE + C_ROPE
D_FF = 2816
N_EXP = 8
D_FFE = 3584
EPS = 1e-6
NEG = -1e30
LANES = 128
VMEM_LIMIT = 48 * 1024 * 1024

IN_MAIN = 1664
IN_TAIL = 432
SEG = dict(QA=(0, 384), KA=(384, 128), VA=(512, 128), BQ=(640, 256), BK=(896, 256), BV=(1152, 256),
           BO=(1408, 256), GI=(1664, 256), GF=(1920, 256), CQ=(2176, 256), CKV=(2432, 128), CKR=(2560, 128))
NP_IN = 2688


def _cparams(sem, vmem=VMEM_LIMIT):
    return pltpu.CompilerParams(dimension_semantics=sem, vmem_limit_bytes=vmem)


def _dot(a, b):
    return jnp.dot(a, b, preferred_element_type=F32)


def _dot_nt(a, b):
    return lax.dot_general(a, b, (((1,), (1,)), ((), ())), preferred_element_type=F32)


def _lane(shape):
    return lax.broadcasted_iota(jnp.int32, shape, len(shape) - 1)


def _mod_row(i):
    return jnp.where(i < NT_P, 0, 1 + (i - NT_P) // (S_S // TM))


def _pair_specs(n, sample_off):
    return [pl.BlockSpec((TM, n), lambda i: (jnp.minimum(i, NT_P - 1), 0)),
            pl.BlockSpec((TM, n), lambda i: (sample_off + jnp.maximum(i - NT_P, 0), 0))]


def _pick(i, p_ref, s_ref):
    return jnp.where(i < NT_P, p_ref[...], s_ref[...])


def _rope_blk(i):
    return jnp.where((i >= NT_P) & (i < NT), 1 + (i - NT_P) % (S_S // TM), 0)


def _ada_kernel(cv_ref, w_ref, b_ref, o_ref):
    s = cv_ref[...]
    s = s * jax.nn.sigmoid(s)
    o_ref[0] = _dot(s.astype(BF16), w_ref[0].astype(BF16)) + b_ref[0]


def _ada_call(cv, ada_w, ada_b):
    tn = 1536
    return pl.pallas_call(
        _ada_kernel,
        grid=(DEPTH, 6 * D // tn),
        in_specs=[pl.BlockSpec((8, D), lambda l, j: (0, 0)),
                  pl.BlockSpec((1, D, tn), lambda l, j: (l, 0, j)),
                  pl.BlockSpec((1, 1, tn), lambda l, j: (l, 0, j))],
        out_specs=pl.BlockSpec((1, 8, tn), lambda l, j: (l, 0, j)),
        out_shape=jax.ShapeDtypeStruct((DEPTH, 8, 6 * D), F32),
        compiler_params=_cparams(("arbitrary", "arbitrary")),
        name="ada_mod",
    )(cv, ada_w, ada_b.reshape(DEPTH, 1, 6 * D))


def _half_norm(x, g, n):
    outs = []
    for j in range(x.shape[1] // LANES):
        xj = x[:, j * LANES:(j + 1) * LANES]
        lo = _lane(xj.shape) < HD
        s = xj * xj
        s_lo = jnp.sum(jnp.where(lo, s, 0.0), axis=-1, keepdims=True)
        s_hi = jnp.sum(jnp.where(lo, 0.0, s), axis=-1, keepdims=True)
        r = lax.rsqrt(jnp.where(lo, s_lo, s_hi) * (1.0 / n) + EPS)
        outs.append(xj * r)
    return jnp.concatenate(outs, axis=-1) * g


def _group_norm(x, g, n):
    outs = []
    for j in range(x.shape[1] // LANES):
        xj = x[:, j * LANES:(j + 1) * LANES]
        r = lax.rsqrt(jnp.sum(xj * xj, axis=-1, keepdims=True) * (1.0 / n) + EPS)
        outs.append(xj * r)
    return jnp.concatenate(outs, axis=-1) * g


def _rope(x, c, sa, sb, shift):
    outs = []
    for j in range(x.shape[1] // LANES):
        xj = x[:, j * LANES:(j + 1) * LANES]
        outs.append(xj * c + pltpu.roll(xj, LANES - shift, 1) * sa + pltpu.roll(xj, shift, 1) * sb)
    return jnp.concatenate(outs, axis=-1)


def _in_kernel(xp_ref, xs_ref, sh_ref, sc_ref, n1_ref, w_ref, sel_ref, gqa_ref, gka_ref, gb_ref, gcq_ref, gckv_ref,
               wuq_ref, gqn_ref, ra_c, ra_a, ra_b, rc_c, rc_a, rc_b,
               qa_o, ka_o, va_o, kaf_o, vaf_o, bq_o, bk_o, bv_o, bo_o, li_o, lf_o, qc_o, ckv_o, ckr_o,
               wb_scr, wt_scr):
    i = pl.program_id(0)

    @pl.when(i == 0)
    def _():
        wb_scr[...] = w_ref[0, :, 0:IN_MAIN].astype(BF16)
        tail = w_ref[0, :, IN_MAIN:IN_MAIN + IN_TAIL].astype(BF16)
        wt_scr[...] = _dot(tail, sel_ref[...]).astype(BF16)

    x = _pick(i, xp_ref, xs_ref)
    xn = x * lax.rsqrt(jnp.mean(x * x, axis=-1, keepdims=True) + EPS) * n1_ref[...]
    xn = xn * (1.0 + sc_ref[0]) + sh_ref[0]
    xb = xn.astype(BF16)
    p_main = _dot(xb, wb_scr[...])
    p_tail = _dot(xb, wt_scr[...])

    def seg(name):
        o, n = SEG[name]
        return p_main[:, o:o + n] if o < IN_MAIN else p_tail[:, o - IN_MAIN:o - IN_MAIN + n]

    qa = _half_norm(seg("QA"), gqa_ref[...], HD)
    ka = _half_norm(seg("KA"), gka_ref[...], HD)
    va = seg("VA")

    @pl.when(i < NT_P)
    def _():
        kaf_o[...] = ka
        vaf_o[...] = va

    def dup(x):
        lo, hi = _half(x, False), _half(x, True)
        return jnp.concatenate([lo + pltpu.roll(lo, HD, 1), hi + pltpu.roll(hi, HD, 1)], axis=-1)

    va_o[...] = dup(va).astype(BF16)

    bq_o[...] = seg("BQ").astype(BF16)
    bk_o[...] = (seg("BK") * (HD ** -0.5)).astype(BF16)
    bv_o[...] = seg("BV").astype(BF16)
    bo_o[...] = seg("BO")
    gb = gb_ref[...]
    li_o[...] = seg("GI") + gb[:, 0:256]
    lf_o[...] = jax.nn.log_sigmoid(seg("GF") + gb[:, 256:512])

    cq = seg("CQ")
    cqn = cq * lax.rsqrt(jnp.mean(cq * cq, axis=-1, keepdims=True) + EPS) * gcq_ref[...]
    qc = _group_norm(_dot(cqn.astype(BF16), wuq_ref[0]), gqn_ref[...], C_QK)

    @pl.when(i < NT_P)
    def _():
        qa_o[...] = qa.astype(BF16)
        ka_o[...] = dup(ka).astype(BF16)
        qc_o[...] = qc.astype(BF16)

    @pl.when(i >= NT_P)
    def _():
        qa_o[...] = _rope(qa, ra_c[...], ra_a[...], ra_b[...], 32).astype(BF16)
        ka_o[...] = dup(_rope(ka, ra_c[...], ra_a[...], ra_b[...], 32)).astype(BF16)
        qc_o[...] = _rope(qc, rc_c[...], rc_a[...], rc_b[...], 16).astype(BF16)

    ckv = seg("CKV")
    ckv_o[...] = ckv * lax.rsqrt(jnp.mean(ckv * ckv, axis=-1, keepdims=True) + EPS) * gckv_ref[...]
    ckr_o[...] = seg("CKR")


def _in_call(x_p, x_s, sample_off, mod3, l, n1g, w_in, sel, gqa, gka, gb, gcq, gckv, wuq_p, gqn, rope_a, rope_c):
    row = lambda k: pl.BlockSpec((1, 1, D), lambda i: ((l * 8 + _mod_row(i)) * 6 + k, 0, 0))
    vec = lambda n: pl.BlockSpec((1, n), lambda i: (0, 0))
    tab = pl.BlockSpec((TM, LANES), lambda i: (_rope_blk(i), 0))
    tok = lambda n: pl.BlockSpec((TM, n), lambda i: (i, 0))
    tokp = pl.BlockSpec((TM, LANES), lambda i: (jnp.minimum(i, NT_P - 1), 0))
    o = lambda n, dt: jax.ShapeDtypeStruct((T, n), dt)
    return pl.pallas_call(
        _in_kernel,
        grid=(NT,),
        in_specs=_pair_specs(D, sample_off) + [row(0), row(1), vec(D),
                  pl.BlockSpec((1, D, IN_MAIN + IN_TAIL), lambda i: (l, 0, 0), pipeline_mode=pl.Buffered(1)),
                  pl.BlockSpec((IN_TAIL, NP_IN - IN_MAIN), lambda i: (0, 0)),
                  vec(384), vec(128), vec(512), vec(256), vec(128),
                  pl.BlockSpec((1, C_QRANK, 768), lambda i: (0, 0, 0)), vec(768),
                  tab, tab, tab, tab, tab, tab],
        out_specs=[tok(384), tok(256), tok(256), tokp, tokp, tok(256), tok(256), tok(256), tok(256),
                   tok(256), tok(256), tok(768), tok(128), tok(128)],
        out_shape=[o(384, BF16), o(256, BF16), o(256, BF16),
                   jax.ShapeDtypeStruct((T_P, LANES), F32), jax.ShapeDtypeStruct((T_P, LANES), F32),
                   o(256, BF16), o(256, BF16), o(256, BF16), o(256, F32),
                   o(256, F32), o(256, F32), o(768, BF16), o(128, F32), o(128, F32)],
        scratch_shapes=[pltpu.VMEM((D, IN_MAIN), BF16), pltpu.VMEM((D, NP_IN - IN_MAIN), BF16)],
        compiler_params=_cparams(("arbitrary",)),
        name=f"in_proj_l{l}",
    )(x_p, x_s, mod3, mod3, n1g, w_in, sel, gqa, gka, gb, gcq, gckv, wuq_p, gqn, *rope_a, *rope_c)


def _mlakv_kernel(ckv_ref, ckvc_ref, ckr_ref, ckrc_ref, wk_ref, wv_ref, gkn_ref, rc_c, rc_a, rc_b, k_o, v_o):
    i = pl.program_id(0)
    tok = i < NT
    c = jnp.where(tok, ckv_ref[...], ckvc_ref[...]).astype(BF16)
    k = _dot(c, wk_ref[0])
    kr = jnp.where(tok, ckr_ref[...], ckrc_ref[...])
    k = k + jnp.concatenate([kr] * C_HEADS, axis=-1)
    k = _group_norm(k, gkn_ref[...], C_QK)
    sample = (i >= NT_P) & tok

    @pl.when(sample)
    def _():
        k_o[...] = _rope(k, rc_c[...], rc_a[...], rc_b[...], 16).astype(BF16)

    @pl.when(jnp.logical_not(sample))
    def _():
        k_o[...] = k.astype(BF16)

    v_o[...] = _dot(c, wv_ref[0]).astype(BF16)


def _mlakv_call(ckv_n, ckv_cache, ckr, ckr_cache, l, wk_p, wv_p, gkn, rope_c):
    r = T + NB_S * PAST
    tab = pl.BlockSpec((TM, LANES), lambda i: (_rope_blk(i), 0))
    tok = lambda n: pl.BlockSpec((TM, n), lambda i: (i, 0))
    tokens = pl.BlockSpec((TM, LANES), lambda i: (jnp.minimum(i, NT - 1), 0))
    cached = pl.BlockSpec((TM, LANES), lambda i: (jnp.maximum(i - NT, 0), 0))
    return pl.pallas_call(
        _mlakv_kernel,
        grid=(r // TM,),
        in_specs=[tokens, cached, tokens, cached,
                  pl.BlockSpec((1, C_KVRANK, 768), lambda i: (0, 0, 0)),
                  pl.BlockSpec((1, C_KVRANK, 384), lambda i: (0, 0, 0)),
                  pl.BlockSpec((1, 768), lambda i: (0, 0)), tab, tab, tab],
        out_specs=[tok(768), tok(384)],
        out_shape=[jax.ShapeDtypeStruct((r, 768), BF16), jax.ShapeDtypeStruct((r, 384), BF16)],
        compiler_params=_cparams(("arbitrary",)),
        name=f"mla_kv_l{l}",
    )(ckv_n, ckv_cache, ckr, ckr_cache, wk_p, wv_p, gkn, *rope_c)


def _bmm(a, b):
    return lax.dot_general(a, b, (((2,), (1,)), ((0,), (0,))), preferred_element_type=F32)


def _bmm_nt(a, b):
    return lax.dot_general(a, b, (((2,), (2,)), ((0,), (0,))), preferred_element_type=F32)


def _bmm_tn(a, b):
    return lax.dot_general(a, b, (((1,), (1,)), ((0,), (0,))), preferred_element_type=F32)


def _softmax_pv(scores, values, sink):
    batched = scores[0].ndim == 3
    heads = range(scores[0].shape[0]) if batched else ()
    m = scores[0].max(axis=-1, keepdims=True)
    for s in scores[1:]:
        m = jnp.maximum(m, s.max(axis=-1, keepdims=True))
    if sink is not None:
        m = jnp.stack([jnp.maximum(m[h], sink[h]) for h in heads]) if batched else jnp.maximum(m, sink)
    es = [jnp.exp(s - m) for s in scores]
    den = es[0].sum(axis=-1, keepdims=True)
    for e in es[1:]:
        den = den + e.sum(axis=-1, keepdims=True)
    if sink is not None:
        den = (jnp.stack([den[h] + jnp.exp(sink[h] - m[h]) for h in heads]) if batched
               else den + jnp.exp(sink - m))
    inv = 1.0 / den
    out = None
    for e, v in zip(es, values):
        o = (_bmm if batched else _dot)((e * inv).astype(BF16), v)
        out = o if out is None else out + o
    return out


def _scores(q3, k3, scale):
    return jnp.stack([_dot_nt(q3[h], k3[h]) for h in range(q3.shape[0])]) * scale


def _half(x, hi):
    lo = _lane(x.shape) < HD
    return jnp.where(lo != hi, x, jnp.zeros_like(x))


def _pair_sum_store(o3, o_ref):
    for j in range(o3.shape[0] // 2):
        o_ref[:, j * LANES:(j + 1) * LANES] = (o3[2 * j] + o3[2 * j + 1]).astype(BF16)


def _a_heads(q, kdups, vdups):
    q3 = jnp.stack([_half(q[:, (h // 2) * LANES:(h // 2 + 1) * LANES], h % 2 == 1) for h in range(A_HEADS)])
    k3 = [kdups[h // (A_HEADS // A_KV)] for h in range(A_HEADS)]
    v3 = jnp.stack([_half(vdups[h // (A_HEADS // A_KV)], h % 2 == 1) for h in range(A_HEADS)])
    return q3, k3, v3


def _sink3(sink_ref):
    return [sink_ref[h] for h in range(A_HEADS)]


PROMPT_SEQS = 2


def _attn_a_prompt_kernel(sink_ref, q_ref, k_ref, v_ref, o_ref):
    for b in range(PROMPT_SEQS):
        rows = slice(b * S_P, (b + 1) * S_P)
        q = q_ref[rows, :]
        k = k_ref[rows, :]
        v = v_ref[rows, :]
        for j in range(A_HEADS // 2):
            acc = None
            for c in range(2):
                h = 2 * j + c
                g = h // (A_HEADS // A_KV)
                qh = _half(q[:, j * LANES:(j + 1) * LANES], c == 1)
                s = _dot_nt(qh, k[:, g * LANES:(g + 1) * LANES]) * (HD ** -0.5)
                vh = _half(v[:, g * LANES:(g + 1) * LANES], c == 1)
                o = _softmax_pv([s], [vh], sink_ref[h])
                acc = o if acc is None else acc + o
            o_ref[rows, j * LANES:(j + 1) * LANES] = acc.astype(BF16)


def _attn_a_prompt_call(sink, qa, ka, va):
    return pl.pallas_call(
        _attn_a_prompt_kernel,
        grid_spec=pltpu.PrefetchScalarGridSpec(
            num_scalar_prefetch=1, grid=(NB_P // PROMPT_SEQS,),
            in_specs=[pl.BlockSpec((PROMPT_SEQS * S_P, 384), lambda b, s: (b, 0)),
                      pl.BlockSpec((PROMPT_SEQS * S_P, 256), lambda b, s: (b, 0)),
                      pl.BlockSpec((PROMPT_SEQS * S_P, 256), lambda b, s: (b, 0))],
            out_specs=pl.BlockSpec((PROMPT_SEQS * S_P, 384), lambda b, s: (b, 0))),
        out_shape=jax.ShapeDtypeStruct((T_P, 384), BF16),
        compiler_params=_cparams(("arbitrary",)),
        name="attn_a_prompt",
    )(sink, qa, ka, va)


QB = 128
WIN = 128
BAND = QB + 2 * WIN


def _attn_a_sample_kernel(sink_ref, q_ref, k_ref, v_ref, kc_ref, vc_ref, o_ref):
    n = pl.program_id(1)
    ws = pl.multiple_of(jnp.clip(n * QB - WIN, 0, S_S - BAND), QB)
    kb = k_ref[pl.ds(ws, BAND), :]
    vb = v_ref[pl.ds(ws, BAND), :]
    qpos = n * QB + lax.broadcasted_iota(jnp.int32, (QB, BAND), 0)
    kpos = ws + lax.broadcasted_iota(jnp.int32, (QB, BAND), 1)
    ok = (jnp.abs(qpos - kpos) <= WIN)[None]
    groups = range(A_KV)
    q3, k3, v3 = _a_heads(q_ref[...], [kb[:, g * LANES:(g + 1) * LANES] for g in groups],
                          [vb[:, g * LANES:(g + 1) * LANES] for g in groups])

    def dup(x, g):
        xg = _half(x, g == 1)
        return (xg + pltpu.roll(xg, HD, 1)).astype(BF16)

    kc = kc_ref[0, 0]
    vc = vc_ref[0, 0]
    _, kc3, vc3 = _a_heads(q_ref[...], [dup(kc, g) for g in groups], [dup(vc, g) for g in groups])
    s_b = jnp.where(ok, _scores(q3, k3, HD ** -0.5), NEG)
    s_c = _scores(q3, kc3, HD ** -0.5)
    _pair_sum_store(_softmax_pv([s_b, s_c], [v3, vc3], _sink3(sink_ref)), o_ref)


def _attn_a_sample_call(sink, qa, ka, va, cache_k, cache_v, l):
    nqb = S_S // QB
    off = T_P // S_S
    return pl.pallas_call(
        _attn_a_sample_kernel,
        grid_spec=pltpu.PrefetchScalarGridSpec(
            num_scalar_prefetch=1, grid=(NB_S, nqb),
            in_specs=[pl.BlockSpec((QB, 384), lambda b, n, s: (T_P // QB + b * nqb + n, 0)),
                      pl.BlockSpec((S_S, 256), lambda b, n, s: (off + b, 0)),
                      pl.BlockSpec((S_S, 256), lambda b, n, s: (off + b, 0)),
                      pl.BlockSpec((1, 1, PAST, LANES), lambda b, n, s: (b, l, 0, 0)),
                      pl.BlockSpec((1, 1, PAST, LANES), lambda b, n, s: (b, l, 0, 0))],
            out_specs=pl.BlockSpec((QB, 384), lambda b, n, s: (b * nqb + n, 0))),
        out_shape=jax.ShapeDtypeStruct((T_S, 384), BF16),
        compiler_params=_cparams(("arbitrary", "arbitrary")),
        name=f"attn_a_sample_l{l}",
    )(sink, qa, ka, va, cache_k, cache_v)


def _mla_heads(q, ks, vs, o_ref):
    heads = range(C_HEADS)
    q3 = jnp.stack([q[:, h * LANES:(h + 1) * LANES] for h in heads])
    scores = [_scores(q3, [k[:, h * LANES:(h + 1) * LANES] for h in heads], C_QK ** -0.5) for k in ks]
    vals = [jnp.stack([_half(v[:, (h // 2) * LANES:(h // 2 + 1) * LANES], h % 2 == 1) for h in heads])
            for v in vs]
    _pair_sum_store(_softmax_pv(scores, vals, None), o_ref)


def _mla_prompt_kernel(q_ref, k_ref, v_ref, o_ref):
    for b in range(PROMPT_SEQS):
        rows = pl.ds(b * S_P, S_P)
        _mla_heads(q_ref[rows, :], [k_ref[rows, :]], [v_ref[rows, :]], o_ref.at[rows])


def _mla_prompt_call(qc, kc, vc):
    return pl.pallas_call(
        _mla_prompt_kernel,
        grid=(NB_P // PROMPT_SEQS,),
        in_specs=[pl.BlockSpec((PROMPT_SEQS * S_P, 768), lambda b: (b, 0)),
                  pl.BlockSpec((PROMPT_SEQS * S_P, 768), lambda b: (b, 0)),
                  pl.BlockSpec((PROMPT_SEQS * S_P, 384), lambda b: (b, 0))],
        out_specs=pl.BlockSpec((PROMPT_SEQS * S_P, 384), lambda b: (b, 0)),
        out_shape=jax.ShapeDtypeStruct((T_P, 384), BF16),
        compiler_params=_cparams(("arbitrary",)),
        name="mla_prompt",
    )(qc, kc, vc)


def _mla_sample_kernel(q_ref, kc_ref, vc_ref, kl_ref, vl_ref, o_ref):
    _mla_heads(q_ref[...], [kc_ref[...], kl_ref[...]], [vc_ref[...], vl_ref[...]], o_ref)


def _mla_sample_call(qc, kc, vc):
    tq = 256
    nq = S_S // tq
    return pl.pallas_call(
        _mla_sample_kernel,
        grid=(NB_S, nq),
        in_specs=[pl.BlockSpec((tq, 768), lambda b, n: (T_P // tq + b * nq + n, 0)),
                  pl.BlockSpec((PAST, 768), lambda b, n: (T // PAST + b, 0)),
                  pl.BlockSpec((PAST, 384), lambda b, n: (T // PAST + b, 0)),
                  pl.BlockSpec((S_S, 768), lambda b, n: (T_P // S_S + b, 0)),
                  pl.BlockSpec((S_S, 384), lambda b, n: (T_P // S_S + b, 0))],
        out_specs=pl.BlockSpec((tq, 384), lambda b, n: (b * nq + n, 0)),
        out_shape=jax.ShapeDtypeStruct((T_S, 384), BF16),
        compiler_params=_cparams(("arbitrary", "arbitrary")),
        name="mla_sample",
    )(qc, kc, vc, kc, vc)


def _split3(x):
    x1 = x.astype(BF16)
    r = x - x1.astype(F32)
    x2 = r.astype(BF16)
    x3 = (r - x2.astype(F32)).astype(BF16)
    return x1, x2, x3


def _mlstm_kernel(nc, has_state, *refs):
    for j in range(B_HEADS // 2):
        _mlstm_pair(nc, has_state, j, *refs)


def _mlstm_pair(nc, has_state, j, *refs):
    if has_state:
        (q_ref, k_ref, v_ref, bo_ref, li_ref, lf_ref, hn_ref, c0_ref, n0_ref, m0_ref, ob_ref) = refs
    else:
        (q_ref, k_ref, v_ref, bo_ref, li_ref, lf_ref, hn_ref, ob_ref, cn_ref, nn_ref, mn_ref) = refs
    s_len = nc * CHUNK
    pair = slice(j * LANES, (j + 1) * LANES)
    c3 = lambda x: x.reshape(nc, CHUNK, x.shape[-1])
    li3 = c3(li_ref[:, pair])
    lf3 = c3(lf_ref[:, pair])
    row = lax.broadcasted_iota(jnp.int32, (CHUNK, CHUNK), 0)
    colv = lax.broadcasted_iota(jnp.int32, (CHUNK, CHUNK), 1)
    tri_f = colv <= row
    tri_b = colv >= row
    bcast = lambda m: jnp.broadcast_to(m[None], (nc, CHUNK, CHUNK))
    lane1 = _lane((1, LANES))
    fwd_lane = lane1 < 2

    lf_parts = _split3(lf3)
    tf3 = bcast(tri_f.astype(BF16))
    tb3 = bcast(tri_b.astype(BF16))
    bc_f = _bmm(tf3, lf_parts[0]) + _bmm(tf3, lf_parts[1]) + _bmm(tf3, lf_parts[2])
    bc_b = _bmm(tb3, lf_parts[0]) + _bmm(tb3, lf_parts[1]) + _bmm(tb3, lf_parts[2])
    bc3 = jnp.where(fwd_lane, bc_f, bc_b)
    row3 = lax.broadcasted_iota(jnp.int32, (nc, CHUNK, LANES), 1)
    edge = jnp.where(row3 == jnp.where(fwd_lane, CHUNK - 1, 0), bc3, 0.0)
    bl3 = jnp.sum(edge, axis=1, keepdims=True)
    gg3 = bl3 - bc3 + li3
    bl2 = jnp.sum(edge, axis=1)
    mg2 = gg3.max(axis=1)

    m0 = m0_ref[j] if has_state else jnp.zeros((1, LANES), F32)
    mf = m0
    mb = m0
    mf_prev, mf_next, mb_prev, mb_next = {}, {}, {}, {}
    for i in range(nc):
        mf_prev[i] = mf
        mf = jnp.maximum(bl2[i:i + 1] + mf, mg2[i:i + 1])
        mf_next[i] = mf
        cb = nc - 1 - i
        mb_prev[cb] = mb
        mb = jnp.maximum(bl2[cb:cb + 1] + mb, mg2[cb:cb + 1])
        mb_next[cb] = mb
    m_prev = [jnp.where(fwd_lane, mf_prev[c], mb_prev[c]) for c in range(nc)]
    m_next = [jnp.where(fwd_lane, mf_next[c], mb_next[c]) for c in range(nc)]
    m_prev3 = jnp.stack(m_prev)
    m_next3 = jnp.stack(m_next)
    dec2d = jnp.exp(bl2 + jnp.concatenate(m_prev, axis=0) - jnp.concatenate(m_next, axis=0))
    if not has_state:
        mn_ref[j] = jnp.where(fwd_lane, mf, mb)

    u_parts = _split3(li3 - bc3)
    ws3 = jnp.exp(gg3 - m_next3)
    inter3 = bc3 + m_prev3

    q3 = c3(q_ref[:, pair])
    k3 = c3(k_ref[:, pair])
    v3 = c3(v_ref[:, pair])
    q3f = q3.astype(F32)
    k3f = k3.astype(F32)
    lane3 = _lane((nc, CHUNK, LANES))
    lo3 = lane3 < HD
    rr = lax.broadcasted_iota(jnp.int32, (LANES, LANES), 0)
    cc = lax.broadcasted_iota(jnp.int32, (LANES, LANES), 1)
    blockdiag = (rr < HD) == (cc < HD)
    col = lambda x, k: x[:, :, k:k + 1]

    hsum = None
    for d in range(2):
        causal3 = bcast(tri_f if d == 0 else tri_b)
        a_c, ws_sum, mt_c, ws_c, intra = [], [], [], [], None
        for e in range(2):
            kk = 2 * d + e
            pick = jnp.broadcast_to(jnp.where(_lane((CHUNK, LANES)) == kk, 1.0, 0.0).astype(BF16)[None],
                                    (nc, CHUNK, LANES))
            ub = _bmm_nt(pick, u_parts[0]) + _bmm_nt(pick, u_parts[1]) + _bmm_nt(pick, u_parts[2])
            bc_col = col(bc3, kk)
            d_mat = jnp.where(causal3, bc_col + ub, NEG)
            inter = col(inter3, kk)
            mt = jnp.maximum(inter, d_mat.max(axis=-1, keepdims=True))
            a = jnp.exp(inter - mt)
            w = jnp.exp(d_mat - mt) * _bmm_nt(_half(q3, e == 1), k3)
            o = _bmm(w.astype(BF16), _half(v3, e == 1))
            intra = o if intra is None else intra + o
            a_c.append(a)
            mt_c.append(mt)
            ws_sum.append(w.sum(axis=-1, keepdims=True))
            ws_c.append(col(ws3, kk))
        kw3 = k3f * jnp.where(lo3, ws_c[0], ws_c[1])
        dec2 = jnp.where(_lane((nc, LANES)) < HD, dec2d[:, 2 * d:2 * d + 1], dec2d[:, 2 * d + 1:2 * d + 2])
        u_all = jnp.where(blockdiag, _bmm_tn(kw3.astype(BF16), v3), 0.0)
        kwsum = kw3.sum(axis=1)
        if has_state:
            zero = jnp.zeros((HD, HD), F32)
            cst = jnp.concatenate([jnp.concatenate([c0_ref[0, d, 2 * j], zero], axis=1),
                                   jnp.concatenate([zero, c0_ref[0, d, 2 * j + 1]], axis=1)], axis=0)
            nst = jnp.concatenate([n0_ref[0, d, 2 * j:2 * j + 1, :], n0_ref[0, d, 2 * j + 1:2 * j + 2, :]],
                                  axis=1)
        else:
            cst = jnp.zeros((LANES, LANES), F32)
            nst = jnp.zeros((1, LANES), F32)
        cs, ns = [None] * nc, [None] * nc
        for c in (range(nc) if d == 0 else range(nc - 1, -1, -1)):
            cs[c] = cst
            ns[c] = nst
            cst = dec2[c:c + 1] * cst + u_all[c]
            nst = dec2[c:c + 1] * nst + kwsum[c:c + 1]
        if not has_state:
            for e in range(2):
                half = slice(e * HD, (e + 1) * HD)
                cn_ref[0, d, 2 * j + e] = cst[half, half]
                nn_ref[0, d, 2 * j + e:2 * j + e + 1, :] = nst[:, half]
        qc = _bmm(q3, jnp.stack(cs).astype(BF16))
        qn_all = q3f * jnp.stack(ns)
        dn = []
        for e in range(2):
            qn = jnp.sum(jnp.where(lo3 != (e == 1), qn_all, 0.0), axis=-1, keepdims=True)
            den = a_c[e] * qn + ws_sum[e]
            dn.append(jnp.maximum(jnp.abs(den), jnp.exp(-mt_c[e])))
        h2 = (jnp.where(lo3, a_c[0], a_c[1]) * qc + intra) / jnp.where(lo3, dn[0], dn[1])
        hsum = h2 if hsum is None else hsum + h2

    hs = hsum.reshape(s_len, LANES)
    lo = _lane(hs.shape) < HD
    s = hs * hs
    s_lo = jnp.sum(jnp.where(lo, s, 0.0), axis=-1, keepdims=True)
    s_hi = jnp.sum(jnp.where(lo, 0.0, s), axis=-1, keepdims=True)
    r = lax.rsqrt(jnp.where(lo, s_lo, s_hi) * (1.0 / HD) + EPS)
    ob_ref[:, pair] = (hs * r * hn_ref[...] * jax.nn.sigmoid(bo_ref[:, pair])).astype(BF16)


def _mlstm_call(bq, bk, bv, bo, li, lf, hn2, state, l):
    has_state = state is not None
    if has_state:
        nb, s_len, base = NB_S, S_S, T_P // S_S
    else:
        nb, s_len, base = NB_P, S_P, 0
    nc = s_len // CHUNK
    tokp = pl.BlockSpec((s_len, 2 * LANES), lambda b: (base + b, 0))
    in_specs = [tokp, tokp, tokp, tokp, tokp, tokp, pl.BlockSpec((1, LANES), lambda b: (0, 0))]
    args = [bq, bk, bv, bo, li, lf, hn2]
    ob_spec = pl.BlockSpec((s_len, 2 * LANES), lambda b: (b, 0))
    ob_shape = jax.ShapeDtypeStruct((nb * s_len, 256), BF16)
    state_specs = [pl.BlockSpec((1, 2, B_HEADS, HD, HD), lambda b: (b, 0, 0, 0, 0)),
                   pl.BlockSpec((1, 2, B_HEADS, HD), lambda b: (b, 0, 0, 0)),
                   pl.BlockSpec((2, 1, LANES), lambda b: (b, 0, 0))]
    if has_state:
        in_specs += state_specs
        args += list(state)
        out_specs = ob_spec
        out_shape = ob_shape
    else:
        out_specs = [ob_spec] + state_specs
        out_shape = [ob_shape,
                     jax.ShapeDtypeStruct((nb, 2, B_HEADS, HD, HD), F32),
                     jax.ShapeDtypeStruct((nb, 2, B_HEADS, HD), F32),
                     jax.ShapeDtypeStruct((nb * 2, 1, LANES), F32)]
    return pl.pallas_call(
        functools.partial(_mlstm_kernel, nc, has_state),
        grid=(nb,),
        in_specs=in_specs,
        out_specs=out_specs,
        out_shape=out_shape,
        compiler_params=_cparams(("arbitrary",)),
        name=f"mlstm_{'sample' if has_state else 'prompt'}_l{l}",
    )(*args)


def _out_kernel(moe, xp_ref, xs_ref, oap_ref, oas_ref, obp_ref, obs_ref, ocp_ref, ocs_ref,
                w_ref, g1_ref, sh_ref, sc_ref, n2_ref, *rest):
    if moe:
        rh_ref, rl_ref, x1_o, xn_o, route_o, cnt_o, run_scr = rest
    else:
        x1_o, xn_o = rest
    i = pl.program_id(0)
    o = (_dot(_pick(i, oap_ref, oas_ref), w_ref[0, 0:384, :])
         + _dot(_pick(i, obp_ref, obs_ref), w_ref[0, 384:640, :])
         + _dot(_pick(i, ocp_ref, ocs_ref), w_ref[0, 640:1024, :]))
    x1 = _pick(i, xp_ref, xs_ref) + g1_ref[0] * o
    x1_o[...] = x1
    xn = x1 * lax.rsqrt(jnp.mean(x1 * x1, axis=-1, keepdims=True) + EPS) * n2_ref[...]
    xn = xn * (1.0 + sc_ref[0]) + sh_ref[0]
    xb = xn.astype(BF16)
    if not moe:
        xn_o[...] = xb
    else:
        xn_o[...] = xn
        xl = (xn - xb.astype(F32)).astype(BF16)
        logits = _dot(xb, rh_ref[0]) + (_dot(xl, rh_ref[0]) + _dot(xb, rl_ref[0]))
        lane = _lane(logits.shape)
        logits = jnp.where(lane < N_EXP, logits, -jnp.inf)
        m1 = logits.max(axis=-1, keepdims=True)
        i1 = jnp.min(jnp.where(logits == m1, lane, LANES), axis=-1, keepdims=True)
        rest_l = jnp.where(lane == i1, -jnp.inf, logits)
        m2 = rest_l.max(axis=-1, keepdims=True)
        i2 = jnp.min(jnp.where(rest_l == m2, lane, LANES), axis=-1, keepdims=True)
        e2 = jnp.exp(m2 - m1)
        den = 1.0 + e2
        w1 = 1.0 / den
        w2 = e2 / den

        @pl.when(pl.program_id(0) == 0)
        def _():
            run_scr[...] = jnp.zeros(run_scr.shape, F32)

        sel = jnp.where(lane == i1, 1.0, jnp.where((lane == i2) & (w2 > 0.0), 1.0, 0.0))
        rb = 256
        before = (lax.broadcasted_iota(jnp.int32, (rb, rb), 1)
                  < lax.broadcasted_iota(jnp.int32, (rb, rb), 0)).astype(BF16)
        run = run_scr[...]
        ranks = []
        for k in range(TM // rb):
            part = sel[k * rb:(k + 1) * rb]
            ranks.append(run + _dot(before, part.astype(BF16)))
            run = run + jnp.sum(part, axis=0, keepdims=True)
        rank = jnp.concatenate(ranks, axis=0)
        r1 = jnp.sum(jnp.where(lane == i1, rank, 0.0), axis=-1, keepdims=True)
        r2 = jnp.sum(jnp.where(lane == i2, rank, 0.0), axis=-1, keepdims=True)
        run_scr[...] = run
        cnt_o[...] = run_scr[...]
        fields = [i1.astype(F32), i2.astype(F32), r1, r2, w1, w2]
        info = jnp.zeros(lane.shape, F32)
        for k, v in enumerate(fields):
            info = jnp.where(lane == k, v, info)
        route_o[...] = info.T


def _out_call(x_p, x_s, sample_off, oa, ob, oc, w_out_b, mod3, l, n2g, router):
    moe = router is not None
    row = lambda k: pl.BlockSpec((1, 1, D), lambda i: ((l * 8 + _mod_row(i)) * 6 + k, 0, 0))
    tok = lambda n: pl.BlockSpec((TM, n), lambda i: (i, 0))
    in_specs = (_pair_specs(D, sample_off) + _pair_specs(384, 0) + _pair_specs(256, 0) + _pair_specs(384, 0)
                + [pl.BlockSpec((1, D, D), lambda i: (0, 0, 0)),
                   row(2), row(3), row(4), pl.BlockSpec((1, D), lambda i: (0, 0))])
    args = [x_p, x_s, *oa, *ob, *oc, w_out_b, mod3, mod3, mod3, n2g]
    out_specs = [tok(D), tok(D)]
    out_shape = [jax.ShapeDtypeStruct((T, D), F32), jax.ShapeDtypeStruct((T, D), BF16)]
    if moe:
        out_shape[1] = jax.ShapeDtypeStruct((T, D), F32)
        rh, rl = router
        in_specs += [pl.BlockSpec((1, D, LANES), lambda i: (0, 0, 0))] * 2
        args += [rh, rl]
        out_specs += [pl.BlockSpec((LANES, TM), lambda i: (0, i)), pl.BlockSpec((1, LANES), lambda i: (0, 0))]
        out_shape += [jax.ShapeDtypeStruct((LANES, T), F32), jax.ShapeDtypeStruct((1, LANES), F32)]
    return pl.pallas_call(
        functools.partial(_out_kernel, moe),
        grid=(NT,),
        in_specs=in_specs, out_specs=out_specs, out_shape=out_shape,
        scratch_shapes=[pltpu.VMEM((1, LANES), F32)] if moe else [],
        compiler_params=_cparams(("arbitrary",)),
        name=f"out_proj_l{l}",
    )(*args)


FFN_TM = 1024
FFN_TF = 256
FFN_TN = 512


def _hidden_tile(x, w1_ref, w3_ref):
    a = _dot(x, w1_ref[...].astype(BF16))
    b = _dot(x, w3_ref[...].astype(BF16))
    return (a * jax.nn.sigmoid(a) * b).astype(BF16)


def _down_tile(h_scr, w2n, m):
    out = None
    for f in range(h_scr.shape[0]):
        o = _dot(h_scr[f, 0:m, :], w2n[f])
        out = o if out is None else out + o
    return out


def _keep_weights(s, w1_ref, w3_ref, w2_ref, w1b, w3b, w2b, tn):
    w1b[s] = w1_ref[...].astype(BF16)
    w3b[s] = w3_ref[...].astype(BF16)
    w2 = w2_ref[...].astype(BF16)
    for n in range(w2b.shape[0]):
        w2b[n, s] = w2[:, n * tn:(n + 1) * tn]


def _ffn_kernel(nf, xn_ref, x1_ref, g2_ref, w1_ref, w3_ref, w2_ref, y_ref, h_scr, w1b, w3b, w2b):
    i = pl.program_id(0)
    s = pl.program_id(1)

    @pl.when((i == 0) & (s < nf))
    def _():
        _keep_weights(s, w1_ref.at[0], w3_ref.at[0], w2_ref.at[0], w1b, w3b, w2b, FFN_TN)

    @pl.when(s < nf)
    def _():
        h_scr[s] = _hidden_tile(xn_ref[...], w1b.at[s], w3b.at[s])

    @pl.when(s >= nf)
    def _():
        y_ref[...] = x1_ref[...] + g2_ref[0] * _down_tile(h_scr, w2b.at[s - nf], FFN_TM)


def _ffn_call(xn, x1, mod3, l, w1, w3, w2, i_layer):
    nf = D_FF // FFN_TF
    nn = D // FFN_TN
    tile = lambda i, s: jnp.where(i == 0, jnp.minimum(s, nf - 1), nf - 1)
    out = lambda s: jnp.maximum(s - nf, 0)

    def g2_idx(i, s):
        r = jnp.where(i < T_P // FFN_TM, 0, 1 + (i - T_P // FFN_TM) // (S_S // FFN_TM))
        return ((l * 8 + r) * 6 + 5, 0, out(s))

    return pl.pallas_call(
        functools.partial(_ffn_kernel, nf),
        grid=(T // FFN_TM, nf + nn),
        in_specs=[pl.BlockSpec((FFN_TM, D), lambda i, s: (i, 0)),
                  pl.BlockSpec((FFN_TM, FFN_TN), lambda i, s: (i, out(s))),
                  pl.BlockSpec((1, 1, FFN_TN), g2_idx),
                  pl.BlockSpec((1, D, FFN_TF), lambda i, s: (i_layer, 0, tile(i, s))),
                  pl.BlockSpec((1, D, FFN_TF), lambda i, s: (i_layer, 0, tile(i, s))),
                  pl.BlockSpec((1, FFN_TF, D), lambda i, s: (i_layer, tile(i, s), 0))],
        out_specs=pl.BlockSpec((FFN_TM, FFN_TN), lambda i, s: (i, out(s))),
        out_shape=jax.ShapeDtypeStruct((T, D), F32),
        scratch_shapes=[pltpu.VMEM((nf, FFN_TM, FFN_TF), BF16),
                        pltpu.VMEM((nf, D, FFN_TF), BF16), pltpu.VMEM((nf, D, FFN_TF), BF16),
                        pltpu.VMEM((nn, nf, FFN_TF, FFN_TN), BF16)],
        compiler_params=_cparams(("arbitrary", "arbitrary")),
        name="ffn_dense",
    )(xn, x1, mod3, w1, w3, w2)


MOE_BM = 1024
MOE_SUB = 256
MOE_NBLK = 2 * T // MOE_BM + N_EXP
MOE_NR = MOE_NBLK * MOE_BM
MOE_TF = 512
MOE_TN = 512
FIN_TM = 512


def _route_kernel(i1_ref, i2_ref, r1_ref, r2_ref, cnt_ref, w2_ref,
                  src_ref, eb_ref, rows_ref, nvb_ref, off_scr):
    def clear(r, carry):
        src_ref[r] = 0
        return carry

    lax.fori_loop(0, MOE_NR + 8, clear, 0, unroll=8)

    def clear_blk(b, carry):
        eb_ref[b] = 0
        rows_ref[b] = 0
        return carry

    lax.fori_loop(0, MOE_NBLK, clear_blk, 0)

    nblk = jnp.int32(0)
    for e in range(N_EXP):
        c = cnt_ref[e]
        nbe = (c + (MOE_BM - 1)) // MOE_BM
        off_scr[e] = nblk * MOE_BM

        def fill(j, carry, e=e, c=c, nblk=nblk):
            eb_ref[nblk + j] = e
            rows_ref[nblk + j] = jnp.minimum(c - j * MOE_BM, MOE_BM)
            return carry

        lax.fori_loop(0, nbe, fill, 0)
        nblk = nblk + nbe
    nvb_ref[0] = nblk

    def place(t, carry):
        src_ref[off_scr[i1_ref[t]] + r1_ref[t]] = t
        src_ref[jnp.where(w2_ref[t] > 0.0, off_scr[i2_ref[t]] + r2_ref[t], MOE_NR)] = t
        return carry

    lax.fori_loop(0, T, place, 0, unroll=4)


def _route_call(route, cnt):
    ints = [route[k].astype(jnp.int32) for k in range(4)]
    cnt8 = cnt[0, :N_EXP].astype(jnp.int32)
    smem = pl.BlockSpec(memory_space=pltpu.SMEM)
    outs = pl.pallas_call(
        _route_kernel,
        grid_spec=pltpu.PrefetchScalarGridSpec(
            num_scalar_prefetch=5, grid=(1,),
            in_specs=[smem],
            out_specs=[smem, smem, smem, smem],
            scratch_shapes=[pltpu.SMEM((N_EXP,), jnp.int32)]),
        out_shape=[jax.ShapeDtypeStruct((MOE_NR + 8,), jnp.int32),
                   jax.ShapeDtypeStruct((MOE_NBLK,), jnp.int32), jax.ShapeDtypeStruct((MOE_NBLK,), jnp.int32),
                   jax.ShapeDtypeStruct((1,), jnp.int32)],
        compiler_params=_cparams(("arbitrary",)),
        name="moe_route",
    )(*ints, cnt8, route[5])
    return tuple(outs) + (ints[0], route[4], route[5])


def _moe_gather_kernel(src_ref, rows_ref, x_ref, o_ref, rows_scr):
    b = pl.program_id(0)
    for q in range(MOE_BM // MOE_SUB):
        base = b * MOE_BM + q * MOE_SUB
        used = rows_ref[b] > q * MOE_SUB
        sub = pl.ds(q * MOE_SUB, MOE_SUB)

        @pl.when(used)
        def _(base=base, sub=sub):
            def body(i, carry):
                for j in range(8):
                    r = i * 8 + j
                    rows_scr[pl.ds(r, 1), :] = x_ref[pl.ds(src_ref[base + r], 1), :]
                return carry

            lax.fori_loop(0, MOE_SUB // 8, body, 0)
            o_ref[sub, :] = rows_scr[...].astype(BF16)

        @pl.when(jnp.logical_not(used))
        def _(sub=sub):
            o_ref[sub, :] = jnp.zeros((MOE_SUB, D), BF16)


def _moe_gather_call(src, rows_b, xn):
    return pl.pallas_call(
        _moe_gather_kernel,
        grid_spec=pltpu.PrefetchScalarGridSpec(
            num_scalar_prefetch=2, grid=(MOE_NBLK,),
            in_specs=[pl.BlockSpec((T, D), lambda b, s, rw: (0, 0), pipeline_mode=pl.Buffered(1))],
            out_specs=pl.BlockSpec((MOE_BM, D), lambda b, s, rw: (b, 0)),
            scratch_shapes=[pltpu.VMEM((MOE_SUB, D), F32)]),
        out_shape=jax.ShapeDtypeStruct((MOE_NR, D), BF16),
        compiler_params=_cparams(("arbitrary",)),
        name="moe_gather",
    )(src, rows_b, xn)


def _moe_first(b, eb_ref):
    return (b == 0) | (eb_ref[b] != eb_ref[jnp.maximum(b - 1, 0)])


def _moe_kernel(nf, eb_ref, rows_ref, nvb_ref, xs_ref, w1_ref, w3_ref, w2_ref, y_ref, h_scr, w1b, w3b, w2b):
    b = pl.program_id(0)
    s = pl.program_id(1)
    nsub = jnp.where(b < nvb_ref[0], (rows_ref[b] + MOE_SUB - 1) // MOE_SUB, 0)
    sizes = [k * MOE_SUB for k in range(1, MOE_BM // MOE_SUB + 1)]

    @pl.when((s < nf) & (nsub > 0) & _moe_first(b, eb_ref))
    def _():
        _keep_weights(s, w1_ref.at[0, 0], w3_ref.at[0, 0], w2_ref.at[0, 0], w1b, w3b, w2b, MOE_TN)

    @pl.when((s < nf) & (nsub > 0))
    def _():
        for k, m in enumerate(sizes, start=1):
            @pl.when(nsub == k)
            def _(m=m):
                h_scr[s, 0:m, :] = _hidden_tile(xs_ref[0:m, :], w1b.at[s], w3b.at[s])

    @pl.when((s >= nf) & (nsub > 0))
    def _():
        for k, m in enumerate(sizes, start=1):
            @pl.when(nsub == k)
            def _(m=m):
                y_ref[0:m, :] = _down_tile(h_scr, w2b.at[s - nf], m)
                if m < MOE_BM:
                    y_ref[m:MOE_BM, :] = jnp.zeros((MOE_BM - m, MOE_TN), F32)

    @pl.when((s >= nf) & (nsub == 0))
    def _():
        y_ref[...] = jnp.zeros(y_ref.shape, F32)


def _moe_call(xs, e_b, rows_b, nvb, w1, w3, w2, i_layer):
    nf = D_FFE // MOE_TF
    nn = D // MOE_TN

    def blk(b, nv):
        return jnp.minimum(b, nv[0] - 1)

    def tile(b, s, eb, nv):
        stream = (b < nv[0]) & _moe_first(b, eb)
        return jnp.where(stream, jnp.minimum(s, nf - 1), nf - 1)

    return pl.pallas_call(
        functools.partial(_moe_kernel, nf),
        grid_spec=pltpu.PrefetchScalarGridSpec(
            num_scalar_prefetch=3, grid=(MOE_NBLK, nf + nn),
            in_specs=[pl.BlockSpec((MOE_BM, D), lambda b, s, eb, rw, nv: (blk(b, nv), 0)),
                      pl.BlockSpec((1, 1, D, MOE_TF),
                                   lambda b, s, eb, rw, nv: (i_layer, eb[blk(b, nv)], 0, tile(b, s, eb, nv))),
                      pl.BlockSpec((1, 1, D, MOE_TF),
                                   lambda b, s, eb, rw, nv: (i_layer, eb[blk(b, nv)], 0, tile(b, s, eb, nv))),
                      pl.BlockSpec((1, 1, MOE_TF, D),
                                   lambda b, s, eb, rw, nv: (i_layer, eb[blk(b, nv)], tile(b, s, eb, nv), 0))],
            out_specs=pl.BlockSpec((MOE_BM, MOE_TN), lambda b, s, eb, rw, nv: (b, jnp.maximum(s - nf, 0))),
            scratch_shapes=[pltpu.VMEM((nf, MOE_BM, MOE_TF), BF16),
                            pltpu.VMEM((nf, D, MOE_TF), BF16), pltpu.VMEM((nf, D, MOE_TF), BF16),
                            pltpu.VMEM((nn, nf, MOE_TF, MOE_TN), BF16)]),
        out_shape=jax.ShapeDtypeStruct((MOE_NR, D), F32),
        compiler_params=_cparams(("arbitrary", "arbitrary"), vmem=56 * 1024 * 1024),
        name="moe_experts",
    )(e_b, rows_b, nvb, xs, w1, w3, w2)


def _moe_combine_kernel(src_ref, eb_ref, rows_ref, nvb_ref, i1_ref, w1_ref, w2_ref,
                        ys_ref, x1_ref, g2_ref, yp_ref, ys_out_ref, acc):
    i = pl.program_id(0)

    @pl.when(i == 0)
    def _():
        acc[...] = jnp.zeros(acc.shape, F32)

    @pl.when(i < nvb_ref[0])
    def _():
        base = i * MOE_BM
        n = rows_ref[i]
        e = eb_ref[i]

        def gate(t):
            return jnp.where(i1_ref[t] == e, w1_ref[t], w2_ref[t])

        def add_rows(r0, cnt):
            toks = [src_ref[base + r0 + j] for j in range(cnt)]
            vals = [acc[pl.ds(toks[j], 1), :] + gate(toks[j]) * ys_ref[pl.ds(r0 + j, 1), :]
                    for j in range(cnt)]
            for j in range(cnt):
                acc[pl.ds(toks[j], 1), :] = vals[j]

        def body4(q, carry):
            add_rows(q * 4, 4)
            return carry

        lax.fori_loop(0, n // 4, body4, 0)

        def body1(r, carry):
            add_rows(r, 1)
            return carry

        lax.fori_loop((n // 4) * 4, n, body1, 0)

    @pl.when(i >= MOE_NBLK)
    def _():
        t0 = pl.multiple_of((i - MOE_NBLK) * FIN_TM, FIN_TM)
        y = x1_ref[...] + g2_ref[0] * acc[pl.ds(t0, FIN_TM), :]

        @pl.when(i - MOE_NBLK < T_P // FIN_TM)
        def _():
            yp_ref[...] = y

        @pl.when(i - MOE_NBLK >= T_P // FIN_TM)
        def _():
            ys_out_ref[...] = y


def _moe_combine_call(src, e_b, rows_b, nvb, i1, w1, w2, ys, x1, mod3, l):
    nfin = T // FIN_TM
    smem = pl.BlockSpec(memory_space=pltpu.SMEM)

    def g2_idx(i, *_):
        j = jnp.maximum(i - MOE_NBLK, 0)
        r = jnp.where(j < T_P // FIN_TM, 0, 1 + (j - T_P // FIN_TM) // (S_S // FIN_TM))
        return ((l * 8 + r) * 6 + 5, 0, 0)

    return pl.pallas_call(
        _moe_combine_kernel,
        grid_spec=pltpu.PrefetchScalarGridSpec(
            num_scalar_prefetch=5, grid=(MOE_NBLK + nfin,),
            in_specs=[smem, smem,
                      pl.BlockSpec((MOE_BM, D),
                                   lambda i, s, eb, rw, nv, t1: (jnp.minimum(jnp.minimum(i, MOE_NBLK - 1), nv[0] - 1), 0)),
                      pl.BlockSpec((FIN_TM, D), lambda i, *_: (jnp.maximum(i - MOE_NBLK, 0), 0)),
                      pl.BlockSpec((1, 1, D), g2_idx)],
            out_specs=[pl.BlockSpec((FIN_TM, D),
                                    lambda i, *_: (jnp.clip(i - MOE_NBLK, 0, T_P // FIN_TM - 1), 0)),
                       pl.BlockSpec((FIN_TM, D),
                                    lambda i, *_: (jnp.maximum(i - MOE_NBLK - T_P // FIN_TM, 0), 0))],
            scratch_shapes=[pltpu.VMEM((T, D), F32)]),
        out_shape=[jax.ShapeDtypeStruct((T_P, D), F32), jax.ShapeDtypeStruct((T_S, D), F32)],
        compiler_params=_cparams(("arbitrary",), vmem=52 * 1024 * 1024),
        name="moe_combine",
    )(src, e_b, rows_b, nvb, i1, w1, w2, ys, x1, mod3)


def _moe_layer(xp, x1, route, cnt, mod3, l, w1, w3, w2, i_layer):
    src, e_b, rows_b, nvb, i1, g1, g2 = _route_call(route, cnt)
    xs = _moe_gather_call(src, rows_b, xp)
    ys = _moe_call(xs, e_b, rows_b, nvb, w1, w3, w2, i_layer)
    return _moe_combine_call(src, e_b, rows_b, nvb, i1, g1, g2, ys, x1, mod3, l)


def _pad_cols(w, n):
    return jnp.pad(w, ((0, 0), (0, n - w.shape[1])))


def _tail_selector():
    sel = np.zeros((IN_TAIL, NP_IN - IN_MAIN), np.float32)
    for d in range(2):
        for gate in range(2):
            for head in range(B_HEADS):
                sel[d * 8 + gate * 4 + head, gate * 256 + (head // 2) * LANES + d * 2 + head % 2] = 1.0
    for c in range(C_QRANK):
        sel[16 + c, 512 + c] = 1.0
    for c in range(C_KVRANK):
        sel[16 + C_QRANK + c, 768 + c] = 1.0
    for c in range(C_ROPE):
        sel[16 + C_QRANK + C_KVRANK + c, 896 + C_NOPE + c] = 1.0
    return sel


def _relayout_gate_b(b):
    return jnp.dot(b[None, :], jnp.asarray(_tail_selector()[:16, :512]), precision=lax.Precision.HIGHEST)


def _pad_heads(w, width):
    r = w.shape[0]
    h = w.shape[1] // width
    return jnp.pad(w.reshape(r, h, width), ((0, 0), (0, 0), (0, LANES - width))).reshape(r, h * LANES)


def _rope_tables(half, span_start, period):
    rows = S_S // 64
    r = jnp.repeat(jnp.arange(rows), 64).astype(F32)
    c = jnp.tile(jnp.arange(64), rows).astype(F32)
    n_freq = half // 2
    freq = 10000.0 ** (-jnp.arange(n_freq, dtype=F32) / n_freq)
    ang = jnp.concatenate([r[:, None] * freq, c[:, None] * freq], axis=-1)
    cos, sin = jnp.cos(ang), jnp.sin(ang)
    d = (jnp.arange(LANES) - span_start) % period
    inside = d < 2 * half
    p = jnp.where(inside, d % half, 0)
    first = inside & (d < half)
    second = inside & (d >= half)
    ct = jnp.where(inside[None, :], cos[:, p], 1.0)
    sa = jnp.where(first[None, :], -sin[:, p], 0.0)
    sb = jnp.where(second[None, :], sin[:, p], 0.0)
    ident = (jnp.ones((TM, LANES), F32), jnp.zeros((TM, LANES), F32), jnp.zeros((TM, LANES), F32))
    return tuple(jnp.concatenate([i0, t], axis=0) for i0, t in zip(ident, (ct, sa, sb)))


def kernel(x_prompt, x_sample, c, cache_swa_k, cache_swa_v, cache_mla_ckv, cache_mla_krope, state_mlstm_C, state_mlstm_n, state_mlstm_m, c_ctx, ada_w, ada_b, norm1_g, norm2_g, w_in, a_qn_g, a_kn_g, a_sink, b_gate_b, b_hn_g, c_qa_g, c_kva_g, c_wuq, c_wukv, c_qn_g, c_kn_g, w_out, ffn_w1, ffn_w3, ffn_w2, moe_router, moe_w1, moe_w3, moe_w2):
    x_p, x_s, sample_off = x_prompt.reshape(T_P, D), x_sample.reshape(T_S, D), 0
    cv = jnp.concatenate([c_ctx[None, :], c, jnp.zeros((5, D), F32)], axis=0)
    mod = _ada_call(cv, ada_w, ada_b)
    mod3 = mod.reshape(DEPTH * 8 * 6, 1, D)

    sel = jnp.asarray(_tail_selector(), BF16)
    rope_a = _rope_tables(32, 0, HD)
    rope_c = _rope_tables(16, C_NOPE, LANES)

    cache_k4 = cache_swa_k.reshape(NB_S, DEPTH, PAST, A_KV * HD)
    cache_v4 = cache_swa_v.reshape(NB_S, DEPTH, PAST, A_KV * HD)

    news = []
    for l in range(DEPTH):
        wuq_p = _pad_heads(c_wuq[l], C_QK).astype(BF16)[None]
        wukv = c_wukv[l].reshape(C_KVRANK, C_HEADS, C_NOPE + C_V)
        wk_p = jnp.pad(wukv[..., :C_NOPE], ((0, 0), (0, 0), (0, LANES - C_NOPE)))
        wk_p = wk_p.reshape(1, C_KVRANK, C_HEADS * LANES).astype(BF16)
        wv_p = wukv[..., C_NOPE:].reshape(1, C_KVRANK, C_HEADS * C_V).astype(BF16)
        w_out_b = w_out[l].astype(BF16)[None]
        gqa = jnp.tile(a_qn_g[l], A_HEADS)[None, :]
        gka = jnp.tile(a_kn_g[l], A_KV)[None, :]
        gb = _relayout_gate_b(b_gate_b[l])
        gqn = jnp.tile(jnp.pad(c_qn_g[l], (0, LANES - C_QK)), C_HEADS)[None, :]
        gkn = jnp.tile(jnp.pad(c_kn_g[l], (0, LANES - C_QK)), C_HEADS)[None, :]
        (qa, ka, va, kaf, vaf, bq, bk, bv, bo, li, lf, qc, ckv_n, ckr) = _in_call(
            x_p, x_s, sample_off, mod3, l, norm1_g[l][None, :], w_in, sel, gqa, gka, gb, c_qa_g[l][None, :],
            c_kva_g[l][None, :], wuq_p, gqn, rope_a, rope_c)

        ckr_cache = jnp.pad(cache_mla_krope[:, l].reshape(NB_S * PAST, C_ROPE),
                            ((0, 0), (C_NOPE, LANES - C_NOPE - C_ROPE)))
        kc, vc = _mlakv_call(ckv_n, cache_mla_ckv[:, l].reshape(NB_S * PAST, C_KVRANK), ckr, ckr_cache,
                             l, wk_p, wv_p, gkn, rope_c)

        sink = a_sink[l]
        oa_p = _attn_a_prompt_call(sink, qa, ka, va)
        oa_s = _attn_a_sample_call(sink, qa, ka, va, cache_k4, cache_v4, l)
        oc_p = _mla_prompt_call(qc, kc, vc)
        oc_s = _mla_sample_call(qc, kc, vc)

        hn2 = jnp.tile(b_hn_g[l], 2)[None, :]
        ob_p, cn, nn, mn = _mlstm_call(bq, bk, bv, bo, li, lf, hn2, None, l)
        c0, n0 = state_mlstm_C[:, l], state_mlstm_n[:, l]
        m_st = state_mlstm_m[:, l].reshape(NB_S, 2, 2, 2)
        m0 = jnp.transpose(m_st, (0, 2, 1, 3)).reshape(NB_S * 2, 1, 4)
        m0 = jnp.pad(m0, ((0, 0), (0, 0), (0, LANES - 4)))
        ob_s = _mlstm_call(bq, bk, bv, bo, li, lf, hn2, (c0, n0, m0), l)

        oa, ob, oc = (oa_p, oa_s), (ob_p, ob_s), (oc_p, oc_s)

        if l % 2 == 0:
            x1, xn = _out_call(x_p, x_s, sample_off, oa, ob, oc, w_out_b, mod3, l, norm2_g[l][None, :], None)
            x = _ffn_call(xn, x1, mod3, l, ffn_w1, ffn_w3, ffn_w2, l // 2)
            x_p, x_s, sample_off = x, x, NT_P
        else:
            r = _pad_cols(moe_router[l // 2], LANES)
            rh = r.astype(BF16)
            rl = (r - rh.astype(F32)).astype(BF16)
            x1, xn, route, cnt = _out_call(x_p, x_s, sample_off, oa, ob, oc, w_out_b, mod3, l,
                                           norm2_g[l][None, :], (rh[None], rl[None]))
            x_p, x_s = _moe_layer(xn, x1, route, cnt, mod3, l, moe_w1, moe_w3, moe_w2, l // 2)
            sample_off = 0

        new_k = kaf.reshape(NB_P, S_P, A_KV, HD)
        new_v = vaf.reshape(NB_P, S_P, A_KV, HD)
        new_ckv = ckv_n[:T_P].reshape(NB_P, S_P, C_KVRANK)
        new_kr = ckr[:T_P, C_NOPE:C_NOPE + C_ROPE].reshape(NB_P, S_P, C_ROPE)
        new_c, new_n = cn, nn
        mn4 = mn.reshape(NB_P, 2, LANES)[:, :, :4].reshape(NB_P, 2, 2, 2)
        new_m = jnp.transpose(mn4, (0, 2, 1, 3)).reshape(NB_P, 2, B_HEADS)
        news.append((new_k, new_v, new_ckv, new_kr, new_c, new_n, new_m))

    y_prompt = x_p[:T_P].reshape(NB_P, S_P, D)
    y_sample = x_s[sample_off * TM:sample_off * TM + T_S].reshape(NB_S, S_S, D)
    stacked = tuple(jnp.stack([nw[j] for nw in news], axis=1) for j in range(7))
    return (y_prompt, y_sample) + stacked
```

```python
import functools

import jax
import jax.numpy as jnp
import numpy as np
from jax import lax
from jax.experimental import pallas as pl
from jax.experimental.pallas import tpu as pltpu

F32 = jnp.float32
BF16 = jnp.bfloat16

D = 1024
NB_P, S_P = 16, 256
NB_S, S_S = 2, 1024
PAST = 512
DEPTH = 2
T_P = NB_P * S_P
T_S = NB_S * S_S
T = T_P + T_S
TM = 512
NT = T // TM
NT_P = T_P // TM
HD = 64
A_HEADS, A_KV = 6, 2
B_HEADS = 4
CHUNK = 64
C_HEADS = 6
C_QRANK, C_KVRANK, C_NOPE, C_ROPE, C_V = 256, 128, 64, 32, 64
C_QK = C_NOPE + C_ROPE
D_FF = 2816
N_EXP = 8
D_FFE = 3584
EPS = 1e-6
NEG = -1e30
LANES = 128
VMEM_LIMIT = 48 * 1024 * 1024

IN_MAIN = 1664
IN_TAIL = 432
SEG = dict(QA=(0, 384), KA=(384, 128), VA=(512, 128), BQ=(640, 256), BK=(896, 256), BV=(1152, 256),
           BO=(1408, 256), GI=(1664, 256), GF=(1920, 256), CQ=(2176, 256), CKV=(2432, 128), CKR=(2560, 128))
NP_IN = 2688


def _cparams(sem, vmem=VMEM_LIMIT):
    return pltpu.CompilerParams(dimension_semantics=sem, vmem_limit_bytes=vmem)


def _dot(a, b):
    return jnp.dot(a, b, preferred_element_type=F32)


def _dot_nt(a, b):
    return lax.dot_general(a, b, (((1,), (1,)), ((), ())), preferred_element_type=F32)


def _lane(shape):
    return lax.broadcasted_iota(jnp.int32, shape, len(shape) - 1)


def _mod_row(i):
    return jnp.where(i < NT_P, 0, 1 + (i - NT_P) // (S_S // TM))


def _pair_specs(n, sample_off):
    return [pl.BlockSpec((TM, n), lambda i: (jnp.minimum(i, NT_P - 1), 0)),
            pl.BlockSpec((TM, n), lambda i: (sample_off + jnp.maximum(i - NT_P, 0), 0))]


def _pick(i, p_ref, s_ref):
    return jnp.where(i < NT_P, p_ref[...], s_ref[...])


def _rope_blk(i):
    return jnp.where((i >= NT_P) & (i < NT), 1 + (i - NT_P) % (S_S // TM), 0)


def _ada_kernel(cv_ref, w_ref, b_ref, o_ref):
    s = cv_ref[...]
    s = s * jax.nn.sigmoid(s)
    o_ref[0] = _dot(s.astype(BF16), w_ref[0].astype(BF16)) + b_ref[0]


def _ada_call(cv, ada_w, ada_b):
    tn = 1536
    return pl.pallas_call(
        _ada_kernel,
        grid=(DEPTH, 6 * D // tn),
        in_specs=[pl.BlockSpec((8, D), lambda l, j: (0, 0)),
                  pl.BlockSpec((1, D, tn), lambda l, j: (l, 0, j)),
                  pl.BlockSpec((1, 1, tn), lambda l, j: (l, 0, j))],
        out_specs=pl.BlockSpec((1, 8, tn), lambda l, j: (l, 0, j)),
        out_shape=jax.ShapeDtypeStruct((DEPTH, 8, 6 * D), F32),
        compiler_params=_cparams(("arbitrary", "arbitrary")),
        name="ada_mod",
    )(cv, ada_w, ada_b.reshape(DEPTH, 1, 6 * D))


def _half_norm(x, g, n):
    outs = []
    for j in range(x.shape[1] // LANES):
        xj = x[:, j * LANES:(j + 1) * LANES]
        lo = _lane(xj.shape) < HD
        s = xj * xj
        s_lo = jnp.sum(jnp.where(lo, s, 0.0), axis=-1, keepdims=True)
        s_hi = jnp.sum(jnp.where(lo, 0.0, s), axis=-1, keepdims=True)
        r = lax.rsqrt(jnp.where(lo, s_lo, s_hi) * (1.0 / n) + EPS)
        outs.append(xj * r)
    return jnp.concatenate(outs, axis=-1) * g


def _group_norm(x, g, n):
    outs = []
    for j in range(x.shape[1] // LANES):
        xj = x[:, j * LANES:(j + 1) * LANES]
        r = lax.rsqrt(jnp.sum(xj * xj, axis=-1, keepdims=True) * (1.0 / n) + EPS)
        outs.append(xj * r)
    return jnp.concatenate(outs, axis=-1) * g


def _rope(x, c, sa, sb, shift):
    outs = []
    for j in range(x.shape[1] // LANES):
        xj = x[:, j * LANES:(j + 1) * LANES]
        outs.append(xj * c + pltpu.roll(xj, LANES - shift, 1) * sa + pltpu.roll(xj, shift, 1) * sb)
    return jnp.concatenate(outs, axis=-1)


def _in_kernel(xp_ref, xs_ref, sh_ref, sc_ref, n1_ref, w_ref, sel_ref, gqa_ref, gka_ref, gb_ref, gcq_ref, gckv_ref,
               wuq_ref, gqn_ref, ra_c, ra_a, ra_b, rc_c, rc_a, rc_b,
               qa_o, ka_o, va_o, kaf_o, vaf_o, bq_o, bk_o, bv_o, bo_o, li_o, lf_o, qc_o, ckv_o, ckr_o,
               wb_scr, wt_scr):
    i = pl.program_id(0)

    @pl.when(i == 0)
    def _():
        wb_scr[...] = w_ref[0, 0:IN_MAIN, :].T.astype(BF16)
        tail_t = w_ref[0, IN_MAIN:IN_MAIN + IN_TAIL, :].astype(BF16)
        wt_scr[...] = lax.dot_general(tail_t, sel_ref[...], (((0,), (0,)), ((), ())),
                                      preferred_element_type=F32).astype(BF16)

    x = _pick(i, xp_ref, xs_ref)
    xn = x * lax.rsqrt(jnp.mean(x * x, axis=-1, keepdims=True) + EPS) * n1_ref[...]
    xn = xn * (1.0 + sc_ref[0]) + sh_ref[0]
    xb = xn.astype(BF16)
    p_main = _dot(xb, wb_scr[...])
    p_tail = _dot(xb, wt_scr[...])

    def seg(name):
        o, n = SEG[name]
        return p_main[:, o:o + n] if o < IN_MAIN else p_tail[:, o - IN_MAIN:o - IN_MAIN + n]

    qa = _half_norm(seg("QA"), gqa_ref[...], HD)
    ka = _half_norm(seg("KA"), gka_ref[...], HD)
    va = seg("VA")

    @pl.when(i < NT_P)
    def _():
        for b in range(TM // S_P):
            kaf_o[b] = ka[b * S_P:(b + 1) * S_P, :].T
            vaf_o[b] = va[b * S_P:(b + 1) * S_P, :].T

    def dup(x):
        lo, hi = _half(x, False), _half(x, True)
        return jnp.concatenate([lo + pltpu.roll(lo, HD, 1), hi + pltpu.roll(hi, HD, 1)], axis=-1)

    va_o[...] = dup(va).astype(BF16)

    bq_o[...] = seg("BQ").astype(BF16)
    bk_o[...] = (seg("BK") * (HD ** -0.5)).astype(BF16)
    bv_o[...] = seg("BV").astype(BF16)
    bo_o[...] = seg("BO")
    gb = gb_ref[...]
    li_o[...] = seg("GI") + gb[:, 0:256]
    lf_o[...] = jax.nn.log_sigmoid(seg("GF") + gb[:, 256:512])

    cq = seg("CQ")
    cqn = cq * lax.rsqrt(jnp.mean(cq * cq, axis=-1, keepdims=True) + EPS) * gcq_ref[...]
    qc = _group_norm(_dot(cqn.astype(BF16), wuq_ref[0]), gqn_ref[...], C_QK)

    @pl.when(i < NT_P)
    def _():
        qa_o[...] = qa.astype(BF16)
        ka_o[...] = dup(ka).astype(BF16)
        qc_o[...] = qc.astype(BF16)

    @pl.when(i >= NT_P)
    def _():
        qa_o[...] = _rope(qa, ra_c[...], ra_a[...], ra_b[...], 32).astype(BF16)
        ka_o[...] = dup(_rope(ka, ra_c[...], ra_a[...], ra_b[...], 32)).astype(BF16)
        qc_o[...] = _rope(qc, rc_c[...], rc_a[...], rc_b[...], 16).astype(BF16)

    ckv = seg("CKV")
    ckv_o[...] = ckv * lax.rsqrt(jnp.mean(ckv * ckv, axis=-1, keepdims=True) + EPS) * gckv_ref[...]
    ckr_o[...] = seg("CKR")


def _in_call(x_p, x_s, sample_off, mod3, l, n1g, w_in, sel, gqa, gka, gb, gcq, gckv, wuq_p, gqn, rope_a, rope_c):
    row = lambda k: pl.BlockSpec((1, 1, D), lambda i: ((l * 8 + _mod_row(i)) * 6 + k, 0, 0))
    vec = lambda n: pl.BlockSpec((1, n), lambda i: (0, 0))
    tab = pl.BlockSpec((TM, LANES), lambda i: (_rope_blk(i), 0))
    tok = lambda n: pl.BlockSpec((TM, n), lambda i: (i, 0))
    tokp = pl.BlockSpec((TM // S_P, LANES, S_P), lambda i: (jnp.minimum(i, NT_P - 1), 0, 0))
    o = lambda n, dt: jax.ShapeDtypeStruct((T, n), dt)
    return pl.pallas_call(
        _in_kernel,
        grid=(NT,),
        in_specs=_pair_specs(D, sample_off) + [row(0), row(1), vec(D),
                  pl.BlockSpec((1, IN_MAIN + IN_TAIL, D), lambda i: (l, 0, 0), pipeline_mode=pl.Buffered(1)),
                  pl.BlockSpec((IN_TAIL, NP_IN - IN_MAIN), lambda i: (0, 0)),
                  vec(384), vec(128), vec(512), vec(256), vec(128),
                  pl.BlockSpec((1, C_QRANK, 768), lambda i: (0, 0, 0)), vec(768),
                  tab, tab, tab, tab, tab, tab],
        out_specs=[tok(384), tok(256), tok(256), tokp, tokp, tok(256), tok(256), tok(256), tok(256),
                   tok(256), tok(256), tok(768), tok(128), tok(128)],
        out_shape=[o(384, BF16), o(256, BF16), o(256, BF16),
                   jax.ShapeDtypeStruct((NB_P, LANES, S_P), F32), jax.ShapeDtypeStruct((NB_P, LANES, S_P), F32),
                   o(256, BF16), o(256, BF16), o(256, BF16), o(256, F32),
                   o(256, F32), o(256, F32), o(768, BF16), o(128, F32), o(128, F32)],
        scratch_shapes=[pltpu.VMEM((D, IN_MAIN), BF16), pltpu.VMEM((D, NP_IN - IN_MAIN), BF16)],
        compiler_params=_cparams(("arbitrary",)),
        name=f"in_proj_l{l}",
    )(x_p, x_s, mod3, mod3, n1g, w_in, sel, gqa, gka, gb, gcq, gckv, wuq_p, gqn, *rope_a, *rope_c)


def _mlakv_kernel(ckv_ref, ckvc_ref, ckr_ref, ckrc_ref, wk_ref, wv_ref, gkn_ref, rc_c, rc_a, rc_b, k_o, v_o):
    i = pl.program_id(0)
    tok = i < NT
    c = jnp.where(tok, ckv_ref[...], ckvc_ref[...]).astype(BF16)
    k = _dot(c, wk_ref[0])
    kr = jnp.where(tok, ckr_ref[...], ckrc_ref[...])
    k = k + jnp.concatenate([kr] * C_HEADS, axis=-1)
    k = _group_norm(k, gkn_ref[...], C_QK)
    sample = (i >= NT_P) & tok

    @pl.when(sample)
    def _():
        k_o[...] = _rope(k, rc_c[...], rc_a[...], rc_b[...], 16).astype(BF16)

    @pl.when(jnp.logical_not(sample))
    def _():
        k_o[...] = k.astype(BF16)

    v_o[...] = _dot(c, wv_ref[0]).astype(BF16)


def _mlakv_call(ckv_n, ckv_cache, ckr, ckr_cache, l, wk_p, wv_p, gkn, rope_c):
    r = T + NB_S * PAST
    tab = pl.BlockSpec((TM, LANES), lambda i: (_rope_blk(i), 0))
    tok = lambda n: pl.BlockSpec((TM, n), lambda i: (i, 0))
    tokens = pl.BlockSpec((TM, LANES), lambda i: (jnp.minimum(i, NT - 1), 0))
    cached = pl.BlockSpec((TM, LANES), lambda i: (jnp.maximum(i - NT, 0), 0))
    return pl.pallas_call(
        _mlakv_kernel,
        grid=(r // TM,),
        in_specs=[tokens, cached, tokens, cached,
                  pl.BlockSpec((1, C_KVRANK, 768), lambda i: (0, 0, 0)),
                  pl.BlockSpec((1, C_KVRANK, 384), lambda i: (0, 0, 0)),
                  pl.BlockSpec((1, 768), lambda i: (0, 0)), tab, tab, tab],
        out_specs=[tok(768), tok(384)],
        out_shape=[jax.ShapeDtypeStruct((r, 768), BF16), jax.ShapeDtypeStruct((r, 384), BF16)],
        compiler_params=_cparams(("arbitrary",)),
        name=f"mla_kv_l{l}",
    )(ckv_n, ckv_cache, ckr, ckr_cache, wk_p, wv_p, gkn, *rope_c)


def _bmm(a, b):
    return lax.dot_general(a, b, (((2,), (1,)), ((0,), (0,))), preferred_element_type=F32)


def _bmm_nt(a, b):
    return lax.dot_general(a, b, (((2,), (2,)), ((0,), (0,))), preferred_element_type=F32)


def _bmm_tn(a, b):
    return lax.dot_general(a, b, (((1,), (1,)), ((0,), (0,))), preferred_element_type=F32)


def _softmax_pv(scores, values, sink):
    batched = scores[0].ndim == 3
    heads = range(scores[0].shape[0]) if batched else ()
    m = scores[0].max(axis=-1, keepdims=True)
    for s in scores[1:]:
        m = jnp.maximum(m, s.max(axis=-1, keepdims=True))
    if sink is not None:
        m = jnp.stack([jnp.maximum(m[h], sink[h]) for h in heads]) if batched else jnp.maximum(m, sink)
    es = [jnp.exp(s - m) for s in scores]
    den = es[0].sum(axis=-1, keepdims=True)
    for e in es[1:]:
        den = den + e.sum(axis=-1, keepdims=True)
    if sink is not None:
        den = (jnp.stack([den[h] + jnp.exp(sink[h] - m[h]) for h in heads]) if batched
               else den + jnp.exp(sink - m))
    inv = 1.0 / den
    out = None
    for e, v in zip(es, values):
        o = (_bmm if batched else _dot)((e * inv).astype(BF16), v)
        out = o if out is None else out + o
    return out


def _scores(q3, k3, scale):
    return jnp.stack([_dot_nt(q3[h], k3[h]) for h in range(q3.shape[0])]) * scale


def _half(x, hi):
    lo = _lane(x.shape) < HD
    return jnp.where(lo != hi, x, jnp.zeros_like(x))


def _pair_sum_store(o3, o_ref):
    for j in range(o3.shape[0] // 2):
        o_ref[:, j * LANES:(j + 1) * LANES] = (o3[2 * j] + o3[2 * j + 1]).astype(BF16)


def _a_heads(q, kdups, vdups):
    q3 = jnp.stack([_half(q[:, (h // 2) * LANES:(h // 2 + 1) * LANES], h % 2 == 1) for h in range(A_HEADS)])
    k3 = [kdups[h // (A_HEADS // A_KV)] for h in range(A_HEADS)]
    v3 = jnp.stack([_half(vdups[h // (A_HEADS // A_KV)], h % 2 == 1) for h in range(A_HEADS)])
    return q3, k3, v3


def _sink3(sink_ref):
    return [sink_ref[h] for h in range(A_HEADS)]


PROMPT_SEQS = 2


def _attn_a_prompt_kernel(sink_ref, q_ref, k_ref, v_ref, o_ref):
    for b in range(PROMPT_SEQS):
        rows = slice(b * S_P, (b + 1) * S_P)
        q = q_ref[rows, :]
        k = k_ref[rows, :]
        v = v_ref[rows, :]
        for j in range(A_HEADS // 2):
            acc = None
            for c in range(2):
                h = 2 * j + c
                g = h // (A_HEADS // A_KV)
                qh = _half(q[:, j * LANES:(j + 1) * LANES], c == 1)
                s = _dot_nt(qh, k[:, g * LANES:(g + 1) * LANES]) * (HD ** -0.5)
                vh = _half(v[:, g * LANES:(g + 1) * LANES], c == 1)
                o = _softmax_pv([s], [vh], sink_ref[h])
                acc = o if acc is None else acc + o
            o_ref[rows, j * LANES:(j + 1) * LANES] = acc.astype(BF16)


def _attn_a_prompt_call(sink, qa, ka, va):
    return pl.pallas_call(
        _attn_a_prompt_kernel,
        grid_spec=pltpu.PrefetchScalarGridSpec(
            num_scalar_prefetch=1, grid=(NB_P // PROMPT_SEQS,),
            in_specs=[pl.BlockSpec((PROMPT_SEQS * S_P, 384), lambda b, s: (b, 0)),
                      pl.BlockSpec((PROMPT_SEQS * S_P, 256), lambda b, s: (b, 0)),
                      pl.BlockSpec((PROMPT_SEQS * S_P, 256), lambda b, s: (b, 0))],
            out_specs=pl.BlockSpec((PROMPT_SEQS * S_P, 384), lambda b, s: (b, 0))),
        out_shape=jax.ShapeDtypeStruct((T_P, 384), BF16),
        compiler_params=_cparams(("arbitrary",)),
        name="attn_a_prompt",
    )(sink, qa, ka, va)


QB = 128
WIN = 128
BAND = QB + 2 * WIN


def _attn_a_sample_kernel(sink_ref, q_ref, k_ref, v_ref, kc_ref, vc_ref, o_ref):
    n = pl.program_id(1)
    ws = pl.multiple_of(jnp.clip(n * QB - WIN, 0, S_S - BAND), QB)
    kb = k_ref[pl.ds(ws, BAND), :]
    vb = v_ref[pl.ds(ws, BAND), :]
    qpos = n * QB + lax.broadcasted_iota(jnp.int32, (QB, BAND), 0)
    kpos = ws + lax.broadcasted_iota(jnp.int32, (QB, BAND), 1)
    ok = (jnp.abs(qpos - kpos) <= WIN)[None]
    groups = range(A_KV)
    q3, k3, v3 = _a_heads(q_ref[...], [kb[:, g * LANES:(g + 1) * LANES] for g in groups],
                          [vb[:, g * LANES:(g + 1) * LANES] for g in groups])

    def dup(x, g):
        xg = _half(x, g == 1)
        return (xg + pltpu.roll(xg, HD, 1)).astype(BF16)

    kc = kc_ref[0, 0]
    vc = vc_ref[0, 0]
    _, kc3, vc3 = _a_heads(q_ref[...], [dup(kc, g) for g in groups], [dup(vc, g) for g in groups])
    s_b = jnp.where(ok, _scores(q3, k3, HD ** -0.5), NEG)
    s_c = _scores(q3, kc3, HD ** -0.5)
    _pair_sum_store(_softmax_pv([s_b, s_c], [v3, vc3], _sink3(sink_ref)), o_ref)


def _attn_a_sample_call(sink, qa, ka, va, cache_k, cache_v, l):
    nqb = S_S // QB
    off = T_P // S_S
    return pl.pallas_call(
        _attn_a_sample_kernel,
        grid_spec=pltpu.PrefetchScalarGridSpec(
            num_scalar_prefetch=1, grid=(NB_S, nqb),
            in_specs=[pl.BlockSpec((QB, 384), lambda b, n, s: (T_P // QB + b * nqb + n, 0)),
                      pl.BlockSpec((S_S, 256), lambda b, n, s: (off + b, 0)),
                      pl.BlockSpec((S_S, 256), lambda b, n, s: (off + b, 0)),
                      pl.BlockSpec((1, 1, PAST, LANES), lambda b, n, s: (b, l, 0, 0)),
                      pl.BlockSpec((1, 1, PAST, LANES), lambda b, n, s: (b, l, 0, 0))],
            out_specs=pl.BlockSpec((QB, 384), lambda b, n, s: (b * nqb + n, 0))),
        out_shape=jax.ShapeDtypeStruct((T_S, 384), BF16),
        compiler_params=_cparams(("arbitrary", "arbitrary")),
        name=f"attn_a_sample_l{l}",
    )(sink, qa, ka, va, cache_k, cache_v)


def _mla_heads(q, ks, vs, o_ref):
    heads = range(C_HEADS)
    q3 = jnp.stack([q[:, h * LANES:(h + 1) * LANES] for h in heads])
    scores = [_scores(q3, [k[:, h * LANES:(h + 1) * LANES] for h in heads], C_QK ** -0.5) for k in ks]
    vals = [jnp.stack([_half(v[:, (h // 2) * LANES:(h // 2 + 1) * LANES], h % 2 == 1) for h in heads])
            for v in vs]
    _pair_sum_store(_softmax_pv(scores, vals, None), o_ref)


def _mla_prompt_kernel(q_ref, k_ref, v_ref, o_ref):
    for b in range(PROMPT_SEQS):
        rows = pl.ds(b * S_P, S_P)
        _mla_heads(q_ref[rows, :], [k_ref[rows, :]], [v_ref[rows, :]], o_ref.at[rows])


def _mla_prompt_call(qc, kc, vc):
    return pl.pallas_call(
        _mla_prompt_kernel,
        grid=(NB_P // PROMPT_SEQS,),
        in_specs=[pl.BlockSpec((PROMPT_SEQS * S_P, 768), lambda b: (b, 0)),
                  pl.BlockSpec((PROMPT_SEQS * S_P, 768), lambda b: (b, 0)),
                  pl.BlockSpec((PROMPT_SEQS * S_P, 384), lambda b: (b, 0))],
        out_specs=pl.BlockSpec((PROMPT_SEQS * S_P, 384), lambda b: (b, 0)),
        out_shape=jax.ShapeDtypeStruct((T_P, 384), BF16),
        compiler_params=_cparams(("arbitrary",)),
        name="mla_prompt",
    )(qc, kc, vc)


def _mla_sample_kernel(q_ref, kc_ref, vc_ref, kl_ref, vl_ref, o_ref):
    _mla_heads(q_ref[...], [kc_ref[...], kl_ref[...]], [vc_ref[...], vl_ref[...]], o_ref)


def _mla_sample_call(qc, kc, vc):
    tq = 256
    nq = S_S // tq
    return pl.pallas_call(
        _mla_sample_kernel,
        grid=(NB_S, nq),
        in_specs=[pl.BlockSpec((tq, 768), lambda b, n: (T_P // tq + b * nq + n, 0)),
                  pl.BlockSpec((PAST, 768), lambda b, n: (T // PAST + b, 0)),
                  pl.BlockSpec((PAST, 384), lambda b, n: (T // PAST + b, 0)),
                  pl.BlockSpec((S_S, 768), lambda b, n: (T_P // S_S + b, 0)),
                  pl.BlockSpec((S_S, 384), lambda b, n: (T_P // S_S + b, 0))],
        out_specs=pl.BlockSpec((tq, 384), lambda b, n: (b * nq + n, 0)),
        out_shape=jax.ShapeDtypeStruct((T_S, 384), BF16),
        compiler_params=_cparams(("arbitrary", "arbitrary")),
        name="mla_sample",
    )(qc, kc, vc, kc, vc)


def _split3(x):
    x1 = x.astype(BF16)
    r = x - x1.astype(F32)
    x2 = r.astype(BF16)
    x3 = (r - x2.astype(F32)).astype(BF16)
    return x1, x2, x3


def _mlstm_kernel(nc, has_state, *refs):
    for j in range(B_HEADS // 2):
        _mlstm_pair(nc, has_state, j, *refs)


def _mlstm_pair(nc, has_state, j, *refs):
    if has_state:
        (q_ref, k_ref, v_ref, bo_ref, li_ref, lf_ref, hn_ref, c0_ref, n0_ref, m0_ref, ob_ref) = refs
    else:
        (q_ref, k_ref, v_ref, bo_ref, li_ref, lf_ref, hn_ref, ob_ref, cn_ref, nn_ref, mn_ref) = refs
    s_len = nc * CHUNK
    pair = slice(j * LANES, (j + 1) * LANES)
    c3 = lambda x: x.reshape(nc, CHUNK, x.shape[-1])
    li3 = c3(li_ref[:, pair])
    lf3 = c3(lf_ref[:, pair])
    row = lax.broadcasted_iota(jnp.int32, (CHUNK, CHUNK), 0)
    colv = lax.broadcasted_iota(jnp.int32, (CHUNK, CHUNK), 1)
    tri_f = colv <= row
    tri_b = colv >= row
    bcast = lambda m: jnp.broadcast_to(m[None], (nc, CHUNK, CHUNK))
    lane1 = _lane((1, LANES))
    fwd_lane = lane1 < 2

    lf_parts = _split3(lf3)
    tf3 = bcast(tri_f.astype(BF16))
    tb3 = bcast(tri_b.astype(BF16))
    bc_f = _bmm(tf3, lf_parts[0]) + _bmm(tf3, lf_parts[1]) + _bmm(tf3, lf_parts[2])
    bc_b = _bmm(tb3, lf_parts[0]) + _bmm(tb3, lf_parts[1]) + _bmm(tb3, lf_parts[2])
    bc3 = jnp.where(fwd_lane, bc_f, bc_b)
    row3 = lax.broadcasted_iota(jnp.int32, (nc, CHUNK, LANES), 1)
    edge = jnp.where(row3 == jnp.where(fwd_lane, CHUNK - 1, 0), bc3, 0.0)
    bl3 = jnp.sum(edge, axis=1, keepdims=True)
    gg3 = bl3 - bc3 + li3
    bl2 = jnp.sum(edge, axis=1)
    mg2 = gg3.max(axis=1)

    m0 = m0_ref[j] if has_state else jnp.zeros((1, LANES), F32)
    mf = m0
    mb = m0
    mf_prev, mf_next, mb_prev, mb_next = {}, {}, {}, {}
    for i in range(nc):
        mf_prev[i] = mf
        mf = jnp.maximum(bl2[i:i + 1] + mf, mg2[i:i + 1])
        mf_next[i] = mf
        cb = nc - 1 - i
        mb_prev[cb] = mb
        mb = jnp.maximum(bl2[cb:cb + 1] + mb, mg2[cb:cb + 1])
        mb_next[cb] = mb
    m_prev = [jnp.where(fwd_lane, mf_prev[c], mb_prev[c]) for c in range(nc)]
    m_next = [jnp.where(fwd_lane, mf_next[c], mb_next[c]) for c in range(nc)]
    m_prev3 = jnp.stack(m_prev)
    m_next3 = jnp.stack(m_next)
    dec2d = jnp.exp(bl2 + jnp.concatenate(m_prev, axis=0) - jnp.concatenate(m_next, axis=0))
    if not has_state:
        mn_ref[j] = jnp.where(fwd_lane, mf, mb)

    u_parts = _split3(li3 - bc3)
    ws3 = jnp.exp(gg3 - m_next3)
    inter3 = bc3 + m_prev3

    q3 = c3(q_ref[:, pair])
    k3 = c3(k_ref[:, pair])
    v3 = c3(v_ref[:, pair])
    q3f = q3.astype(F32)
    k3f = k3.astype(F32)
    lane3 = _lane((nc, CHUNK, LANES))
    lo3 = lane3 < HD
    rr = lax.broadcasted_iota(jnp.int32, (LANES, LANES), 0)
    cc = lax.broadcasted_iota(jnp.int32, (LANES, LANES), 1)
    blockdiag = (rr < HD) == (cc < HD)
    col = lambda x, k: x[:, :, k:k + 1]

    hsum = None
    for d in range(2):
        causal3 = bcast(tri_f if d == 0 else tri_b)
        a_c, ws_sum, mt_c, ws_c, intra = [], [], [], [], None
        for e in range(2):
            kk = 2 * d + e
            pick = jnp.broadcast_to(jnp.where(_lane((CHUNK, LANES)) == kk, 1.0, 0.0).astype(BF16)[None],
                                    (nc, CHUNK, LANES))
            ub = _bmm_nt(pick, u_parts[0]) + _bmm_nt(pick, u_parts[1]) + _bmm_nt(pick, u_parts[2])
            bc_col = col(bc3, kk)
            d_mat = jnp.where(causal3, bc_col + ub, NEG)
            inter = col(inter3, kk)
            mt = jnp.maximum(inter, d_mat.max(axis=-1, keepdims=True))
            a = jnp.exp(inter - mt)
            w = jnp.exp(d_mat - mt) * _bmm_nt(_half(q3, e == 1), k3)
            o = _bmm(w.astype(BF16), _half(v3, e == 1))
            intra = o if intra is None else intra + o
            a_c.append(a)
            mt_c.append(mt)
            ws_sum.append(w.sum(axis=-1, keepdims=True))
            ws_c.append(col(ws3, kk))
        kw3 = k3f * jnp.where(lo3, ws_c[0], ws_c[1])
        dec2 = jnp.where(_lane((nc, LANES)) < HD, dec2d[:, 2 * d:2 * d + 1], dec2d[:, 2 * d + 1:2 * d + 2])
        u_all = jnp.where(blockdiag, _bmm_tn(kw3.astype(BF16), v3), 0.0)
        kwsum = kw3.sum(axis=1)
        if has_state:
            zero = jnp.zeros((HD, HD), F32)
            cst = jnp.concatenate([jnp.concatenate([c0_ref[0, d, 2 * j], zero], axis=1),
                                   jnp.concatenate([zero, c0_ref[0, d, 2 * j + 1]], axis=1)], axis=0)
            nst = jnp.concatenate([n0_ref[0, d, 2 * j:2 * j + 1, :], n0_ref[0, d, 2 * j + 1:2 * j + 2, :]],
                                  axis=1)
        else:
            cst = jnp.zeros((LANES, LANES), F32)
            nst = jnp.zeros((1, LANES), F32)
        cs, ns = [None] * nc, [None] * nc
        for c in (range(nc) if d == 0 else range(nc - 1, -1, -1)):
            cs[c] = cst
            ns[c] = nst
            cst = dec2[c:c + 1] * cst + u_all[c]
            nst = dec2[c:c + 1] * nst + kwsum[c:c + 1]
        if not has_state:
            for e in range(2):
                half = slice(e * HD, (e + 1) * HD)
                cn_ref[0, d, 2 * j + e] = cst[half, half]
                nn_ref[0, d, 2 * j + e:2 * j + e + 1, :] = nst[:, half]
        qc = _bmm(q3, jnp.stack(cs).astype(BF16))
        qn_all = q3f * jnp.stack(ns)
        dn = []
        for e in range(2):
            qn = jnp.sum(jnp.where(lo3 != (e == 1), qn_all, 0.0), axis=-1, keepdims=True)
            den = a_c[e] * qn + ws_sum[e]
            dn.append(jnp.maximum(jnp.abs(den), jnp.exp(-mt_c[e])))
        h2 = (jnp.where(lo3, a_c[0], a_c[1]) * qc + intra) / jnp.where(lo3, dn[0], dn[1])
        hsum = h2 if hsum is None else hsum + h2

    hs = hsum.reshape(s_len, LANES)
    lo = _lane(hs.shape) < HD
    s = hs * hs
    s_lo = jnp.sum(jnp.where(lo, s, 0.0), axis=-1, keepdims=True)
    s_hi = jnp.sum(jnp.where(lo, 0.0, s), axis=-1, keepdims=True)
    r = lax.rsqrt(jnp.where(lo, s_lo, s_hi) * (1.0 / HD) + EPS)
    ob_ref[:, pair] = (hs * r * hn_ref[...] * jax.nn.sigmoid(bo_ref[:, pair])).astype(BF16)


def _mlstm_call(bq, bk, bv, bo, li, lf, hn2, state, l):
    has_state = state is not None
    if has_state:
        nb, s_len, base = NB_S, S_S, T_P // S_S
    else:
        nb, s_len, base = NB_P, S_P, 0
    nc = s_len // CHUNK
    tokp = pl.BlockSpec((s_len, 2 * LANES), lambda b: (base + b, 0))
    in_specs = [tokp, tokp, tokp, tokp, tokp, tokp, pl.BlockSpec((1, LANES), lambda b: (0, 0))]
    args = [bq, bk, bv, bo, li, lf, hn2]
    ob_spec = pl.BlockSpec((s_len, 2 * LANES), lambda b: (b, 0))
    ob_shape = jax.ShapeDtypeStruct((nb * s_len, 256), BF16)
    state_specs = [pl.BlockSpec((1, 2, B_HEADS, HD, HD), lambda b: (b, 0, 0, 0, 0)),
                   pl.BlockSpec((1, 2, B_HEADS, HD), lambda b: (b, 0, 0, 0)),
                   pl.BlockSpec((2, 1, LANES), lambda b: (b, 0, 0))]
    if has_state:
        in_specs += state_specs
        args += list(state)
        out_specs = ob_spec
        out_shape = ob_shape
    else:
        out_specs = [ob_spec] + state_specs
        out_shape = [ob_shape,
                     jax.ShapeDtypeStruct((nb, 2, B_HEADS, HD, HD), F32),
                     jax.ShapeDtypeStruct((nb, 2, B_HEADS, HD), F32),
                     jax.ShapeDtypeStruct((nb * 2, 1, LANES), F32)]
    return pl.pallas_call(
        functools.partial(_mlstm_kernel, nc, has_state),
        grid=(nb,),
        in_specs=in_specs,
        out_specs=out_specs,
        out_shape=out_shape,
        compiler_params=_cparams(("arbitrary",)),
        name=f"mlstm_{'sample' if has_state else 'prompt'}_l{l}",
    )(*args)


def _out_kernel(moe, xp_ref, xs_ref, oap_ref, oas_ref, obp_ref, obs_ref, ocp_ref, ocs_ref,
                w_ref, g1_ref, sh_ref, sc_ref, n2_ref, *rest):
    if moe:
        rh_ref, rl_ref, x1_o, xn_o, route_o, cnt_o, run_scr = rest
    else:
        x1_o, xn_o = rest
    i = pl.program_id(0)
    o = (_dot(_pick(i, oap_ref, oas_ref), w_ref[0, 0:384, :])
         + _dot(_pick(i, obp_ref, obs_ref), w_ref[0, 384:640, :])
         + _dot(_pick(i, ocp_ref, ocs_ref), w_ref[0, 640:1024, :]))
    x1 = _pick(i, xp_ref, xs_ref) + g1_ref[0] * o
    x1_o[...] = x1
    xn = x1 * lax.rsqrt(jnp.mean(x1 * x1, axis=-1, keepdims=True) + EPS) * n2_ref[...]
    xn = xn * (1.0 + sc_ref[0]) + sh_ref[0]
    xb = xn.astype(BF16)
    if not moe:
        xn_o[...] = xb
    else:
        xn_o[...] = xn
        xl = (xn - xb.astype(F32)).astype(BF16)
        logits = _dot(xb, rh_ref[0]) + (_dot(xl, rh_ref[0]) + _dot(xb, rl_ref[0]))
        lane = _lane(logits.shape)
        logits = jnp.where(lane < N_EXP, logits, -jnp.inf)
        m1 = logits.max(axis=-1, keepdims=True)
        i1 = jnp.min(jnp.where(logits == m1, lane, LANES), axis=-1, keepdims=True)
        rest_l = jnp.where(lane == i1, -jnp.inf, logits)
        m2 = rest_l.max(axis=-1, keepdims=True)
        i2 = jnp.min(jnp.where(rest_l == m2, lane, LANES), axis=-1, keepdims=True)
        e2 = jnp.exp(m2 - m1)
        den = 1.0 + e2
        w1 = 1.0 / den
        w2 = e2 / den

        @pl.when(pl.program_id(0) == 0)
        def _():
            run_scr[...] = jnp.zeros(run_scr.shape, F32)

        sel = jnp.where(lane == i1, 1.0, jnp.where((lane == i2) & (w2 > 0.0), 1.0, 0.0))
        rb = 256
        before = (lax.broadcasted_iota(jnp.int32, (rb, rb), 1)
                  < lax.broadcasted_iota(jnp.int32, (rb, rb), 0)).astype(BF16)
        run = run_scr[...]
        ranks = []
        for k in range(TM // rb):
            part = sel[k * rb:(k + 1) * rb]
            ranks.append(run + _dot(before, part.astype(BF16)))
            run = run + jnp.sum(part, axis=0, keepdims=True)
        rank = jnp.concatenate(ranks, axis=0)
        r1 = jnp.sum(jnp.where(lane == i1, rank, 0.0), axis=-1, keepdims=True)
        r2 = jnp.sum(jnp.where(lane == i2, rank, 0.0), axis=-1, keepdims=True)
        run_scr[...] = run
        cnt_o[...] = run_scr[...]
        fields = [i1.astype(F32), i2.astype(F32), r1, r2, w1, w2]
        info = jnp.zeros(lane.shape, F32)
        for k, v in enumerate(fields):
            info = jnp.where(lane == k, v, info)
        route_o[...] = info.T


def _out_call(x_p, x_s, sample_off, oa, ob, oc, w_out_b, mod3, l, n2g, router):
    moe = router is not None
    row = lambda k: pl.BlockSpec((1, 1, D), lambda i: ((l * 8 + _mod_row(i)) * 6 + k, 0, 0))
    tok = lambda n: pl.BlockSpec((TM, n), lambda i: (i, 0))
    in_specs = (_pair_specs(D, sample_off) + _pair_specs(384, 0) + _pair_specs(256, 0) + _pair_specs(384, 0)
                + [pl.BlockSpec((1, D, D), lambda i: (0, 0, 0)),
                   row(2), row(3), row(4), pl.BlockSpec((1, D), lambda i: (0, 0))])
    args = [x_p, x_s, *oa, *ob, *oc, w_out_b, mod3, mod3, mod3, n2g]
    out_specs = [tok(D), tok(D)]
    out_shape = [jax.ShapeDtypeStruct((T, D), F32), jax.ShapeDtypeStruct((T, D), BF16)]
    if moe:
        out_shape[1] = jax.ShapeDtypeStruct((T, D), F32)
        rh, rl = router
        in_specs += [pl.BlockSpec((1, D, LANES), lambda i: (0, 0, 0))] * 2
        args += [rh, rl]
        out_specs += [pl.BlockSpec((LANES, TM), lambda i: (0, i)), pl.BlockSpec((1, LANES), lambda i: (0, 0))]
        out_shape += [jax.ShapeDtypeStruct((LANES, T), F32), jax.ShapeDtypeStruct((1, LANES), F32)]
    return pl.pallas_call(
        functools.partial(_out_kernel, moe),
        grid=(NT,),
        in_specs=in_specs, out_specs=out_specs, out_shape=out_shape,
        scratch_shapes=[pltpu.VMEM((1, LANES), F32)] if moe else [],
        compiler_params=_cparams(("arbitrary",)),
        name=f"out_proj_l{l}",
    )(*args)


FFN_TM = 1024
FFN_TF = 256
FFN_TN = 512


def _hidden_tile(x, w1_ref, w3_ref):
    a = _dot(x, w1_ref[...].astype(BF16))
    b = _dot(x, w3_ref[...].astype(BF16))
    return (a * jax.nn.sigmoid(a) * b).astype(BF16)


def _down_tile(h_scr, w2n, m):
    out = None
    for f in range(h_scr.shape[0]):
        o = _dot(h_scr[f, 0:m, :], w2n[f])
        out = o if out is None else out + o
    return out


def _keep_weights(s, w1_ref, w3_ref, w2_ref, w1b, w3b, w2b, tn):
    w1b[s] = w1_ref[...].astype(BF16)
    w3b[s] = w3_ref[...].astype(BF16)
    w2 = w2_ref[...].astype(BF16)
    for n in range(w2b.shape[0]):
        w2b[n, s] = w2[:, n * tn:(n + 1) * tn]


def _ffn_kernel(nf, xn_ref, x1_ref, g2_ref, w1_ref, w3_ref, w2_ref, y_ref, h_scr, w1b, w3b, w2b):
    i = pl.program_id(0)
    s = pl.program_id(1)

    @pl.when((i == 0) & (s < nf))
    def _():
        _keep_weights(s, w1_ref.at[0], w3_ref.at[0], w2_ref.at[0], w1b, w3b, w2b, FFN_TN)

    @pl.when(s < nf)
    def _():
        h_scr[s] = _hidden_tile(xn_ref[...], w1b.at[s], w3b.at[s])

    @pl.when(s >= nf)
    def _():
        y_ref[...] = x1_ref[...] + g2_ref[0] * _down_tile(h_scr, w2b.at[s - nf], FFN_TM)


def _ffn_call(xn, x1, mod3, l, w1, w3, w2, i_layer):
    nf = D_FF // FFN_TF
    nn = D // FFN_TN
    tile = lambda i, s: jnp.where(i == 0, jnp.minimum(s, nf - 1), nf - 1)
    out = lambda s: jnp.maximum(s - nf, 0)

    def g2_idx(i, s):
        r = jnp.where(i < T_P // FFN_TM, 0, 1 + (i - T_P // FFN_TM) // (S_S // FFN_TM))
        return ((l * 8 + r) * 6 + 5, 0, out(s))

    return pl.pallas_call(
        functools.partial(_ffn_kernel, nf),
        grid=(T // FFN_TM, nf + nn),
        in_specs=[pl.BlockSpec((FFN_TM, D), lambda i, s: (i, 0)),
                  pl.BlockSpec((FFN_TM, FFN_TN), lambda i, s: (i, out(s))),
                  pl.BlockSpec((1, 1, FFN_TN), g2_idx),
                  pl.BlockSpec((1, D, FFN_TF), lambda i, s: (i_layer, 0, tile(i, s))),
                  pl.BlockSpec((1, D, FFN_TF), lambda i, s: (i_layer, 0, tile(i, s))),
                  pl.BlockSpec((1, FFN_TF, D), lambda i, s: (i_layer, tile(i, s), 0))],
        out_specs=pl.BlockSpec((FFN_TM, FFN_TN), lambda i, s: (i, out(s))),
        out_shape=jax.ShapeDtypeStruct((T, D), F32),
        scratch_shapes=[pltpu.VMEM((nf, FFN_TM, FFN_TF), BF16),
                        pltpu.VMEM((nf, D, FFN_TF), BF16), pltpu.VMEM((nf, D, FFN_TF), BF16),
                        pltpu.VMEM((nn, nf, FFN_TF, FFN_TN), BF16)],
        compiler_params=_cparams(("arbitrary", "arbitrary")),
        name="ffn_dense",
    )(xn, x1, mod3, w1, w3, w2)


MOE_BM = 1024
MOE_SUB = 256
MOE_NBLK = 2 * T // MOE_BM + N_EXP
MOE_NR = MOE_NBLK * MOE_BM
MOE_TF = 512
MOE_TN = 512
FIN_TM = 512


def _route_kernel(i1_ref, i2_ref, r1_ref, r2_ref, cnt_ref, w2_ref,
                  src_ref, eb_ref, rows_ref, nvb_ref, off_scr):
    def clear(r, carry):
        src_ref[r] = 0
        return carry

    lax.fori_loop(0, MOE_NR + 8, clear, 0, unroll=8)

    def clear_blk(b, carry):
        eb_ref[b] = 0
        rows_ref[b] = 0
        return carry

    lax.fori_loop(0, MOE_NBLK, clear_blk, 0)

    nblk = jnp.int32(0)
    for e in range(N_EXP):
        c = cnt_ref[e]
        nbe = (c + (MOE_BM - 1)) // MOE_BM
        off_scr[e] = nblk * MOE_BM

        def fill(j, carry, e=e, c=c, nblk=nblk):
            eb_ref[nblk + j] = e
            rows_ref[nblk + j] = jnp.minimum(c - j * MOE_BM, MOE_BM)
            return carry

        lax.fori_loop(0, nbe, fill, 0)
        nblk = nblk + nbe
    nvb_ref[0] = nblk

    def place(t, carry):
        src_ref[off_scr[i1_ref[t]] + r1_ref[t]] = t
        src_ref[jnp.where(w2_ref[t] > 0.0, off_scr[i2_ref[t]] + r2_ref[t], MOE_NR)] = t
        return carry

    lax.fori_loop(0, T, place, 0, unroll=4)


def _route_call(route, cnt):
    ints = [route[k].astype(jnp.int32) for k in range(4)]
    cnt8 = cnt[0, :N_EXP].astype(jnp.int32)
    smem = pl.BlockSpec(memory_space=pltpu.SMEM)
    outs = pl.pallas_call(
        _route_kernel,
        grid_spec=pltpu.PrefetchScalarGridSpec(
            num_scalar_prefetch=5, grid=(1,),
            in_specs=[smem],
            out_specs=[smem, smem, smem, smem],
            scratch_shapes=[pltpu.SMEM((N_EXP,), jnp.int32)]),
        out_shape=[jax.ShapeDtypeStruct((MOE_NR + 8,), jnp.int32),
                   jax.ShapeDtypeStruct((MOE_NBLK,), jnp.int32), jax.ShapeDtypeStruct((MOE_NBLK,), jnp.int32),
                   jax.ShapeDtypeStruct((1,), jnp.int32)],
        compiler_params=_cparams(("arbitrary",)),
        name="moe_route",
    )(*ints, cnt8, route[5])
    return tuple(outs) + (ints[0], route[4], route[5])


def _moe_gather_kernel(src_ref, rows_ref, x_ref, o_ref, rows_scr):
    b = pl.program_id(0)
    for q in range(MOE_BM // MOE_SUB):
        base = b * MOE_BM + q * MOE_SUB
        used = rows_ref[b] > q * MOE_SUB
        sub = pl.ds(q * MOE_SUB, MOE_SUB)

        @pl.when(used)
        def _(base=base, sub=sub):
            def body(i, carry):
                for j in range(8):
                    r = i * 8 + j
                    rows_scr[pl.ds(r, 1), :] = x_ref[pl.ds(src_ref[base + r], 1), :]
                return carry

            lax.fori_loop(0, MOE_SUB // 8, body, 0)
            o_ref[sub, :] = rows_scr[...].astype(BF16)

        @pl.when(jnp.logical_not(used))
        def _(sub=sub):
            o_ref[sub, :] = jnp.zeros((MOE_SUB, D), BF16)


def _moe_gather_call(src, rows_b, xn):
    return pl.pallas_call(
        _moe_gather_kernel,
        grid_spec=pltpu.PrefetchScalarGridSpec(
            num_scalar_prefetch=2, grid=(MOE_NBLK,),
            in_specs=[pl.BlockSpec((T, D), lambda b, s, rw: (0, 0), pipeline_mode=pl.Buffered(1))],
            out_specs=pl.BlockSpec((MOE_BM, D), lambda b, s, rw: (b, 0)),
            scratch_shapes=[pltpu.VMEM((MOE_SUB, D), F32)]),
        out_shape=jax.ShapeDtypeStruct((MOE_NR, D), BF16),
        compiler_params=_cparams(("arbitrary",)),
        name="moe_gather",
    )(src, rows_b, xn)


def _moe_first(b, eb_ref):
    return (b == 0) | (eb_ref[b] != eb_ref[jnp.maximum(b - 1, 0)])


def _moe_kernel(nf, eb_ref, rows_ref, nvb_ref, xs_ref, w1_ref, w3_ref, w2_ref, y_ref, h_scr, w1b, w3b, w2b):
    b = pl.program_id(0)
    s = pl.program_id(1)
    nsub = jnp.where(b < nvb_ref[0], (rows_ref[b] + MOE_SUB - 1) // MOE_SUB, 0)
    sizes = [k * MOE_SUB for k in range(1, MOE_BM // MOE_SUB + 1)]

    @pl.when((s < nf) & (nsub > 0) & _moe_first(b, eb_ref))
    def _():
        _keep_weights(s, w1_ref.at[0, 0], w3_ref.at[0, 0], w2_ref.at[0, 0], w1b, w3b, w2b, MOE_TN)

    @pl.when((s < nf) & (nsub > 0))
    def _():
        for k, m in enumerate(sizes, start=1):
            @pl.when(nsub == k)
            def _(m=m):
                h_scr[s, 0:m, :] = _hidden_tile(xs_ref[0:m, :], w1b.at[s], w3b.at[s])

    @pl.when((s >= nf) & (nsub > 0))
    def _():
        for k, m in enumerate(sizes, start=1):
            @pl.when(nsub == k)
            def _(m=m):
                y_ref[0:m, :] = _down_tile(h_scr, w2b.at[s - nf], m)
                if m < MOE_BM:
                    y_ref[m:MOE_BM, :] = jnp.zeros((MOE_BM - m, MOE_TN), F32)

    @pl.when((s >= nf) & (nsub == 0))
    def _():
        y_ref[...] = jnp.zeros(y_ref.shape, F32)


def _moe_call(xs, e_b, rows_b, nvb, w1, w3, w2, i_layer):
    nf = D_FFE // MOE_TF
    nn = D // MOE_TN

    def blk(b, nv):
        return jnp.minimum(b, nv[0] - 1)

    def tile(b, s, eb, nv):
        stream = (b < nv[0]) & _moe_first(b, eb)
        return jnp.where(stream, jnp.minimum(s, nf - 1), nf - 1)

    return pl.pallas_call(
        functools.partial(_moe_kernel, nf),
        grid_spec=pltpu.PrefetchScalarGridSpec(
            num_scalar_prefetch=3, grid=(MOE_NBLK, nf + nn),
            in_specs=[pl.BlockSpec((MOE_BM, D), lambda b, s, eb, rw, nv: (blk(b, nv), 0)),
                      pl.BlockSpec((1, 1, D, MOE_TF),
                                   lambda b, s, eb, rw, nv: (i_layer, eb[blk(b, nv)], 0, tile(b, s, eb, nv))),
                      pl.BlockSpec((1, 1, D, MOE_TF),
                                   lambda b, s, eb, rw, nv: (i_layer, eb[blk(b, nv)], 0, tile(b, s, eb, nv))),
                      pl.BlockSpec((1, 1, MOE_TF, D),
                                   lambda b, s, eb, rw, nv: (i_layer, eb[blk(b, nv)], tile(b, s, eb, nv), 0))],
            out_specs=pl.BlockSpec((MOE_BM, MOE_TN), lambda b, s, eb, rw, nv: (b, jnp.maximum(s - nf, 0))),
            scratch_shapes=[pltpu.VMEM((nf, MOE_BM, MOE_TF), BF16),
                            pltpu.VMEM((nf, D, MOE_TF), BF16), pltpu.VMEM((nf, D, MOE_TF), BF16),
                            pltpu.VMEM((nn, nf, MOE_TF, MOE_TN), BF16)]),
        out_shape=jax.ShapeDtypeStruct((MOE_NR, D), F32),
        compiler_params=_cparams(("arbitrary", "arbitrary"), vmem=56 * 1024 * 1024),
        name="moe_experts",
    )(e_b, rows_b, nvb, xs, w1, w3, w2)


def _moe_combine_kernel(src_ref, eb_ref, rows_ref, nvb_ref, i1_ref, w1_ref, w2_ref,
                        ys_ref, x1_ref, g2_ref, yp_ref, ys_out_ref, acc):
    i = pl.program_id(0)

    @pl.when(i == 0)
    def _():
        acc[...] = jnp.zeros(acc.shape, F32)

    @pl.when(i < nvb_ref[0])
    def _():
        base = i * MOE_BM
        n = rows_ref[i]
        e = eb_ref[i]

        def gate(t):
            return jnp.where(i1_ref[t] == e, w1_ref[t], w2_ref[t])

        def add_rows(r0, cnt):
            toks = [src_ref[base + r0 + j] for j in range(cnt)]
            vals = [acc[pl.ds(toks[j], 1), :] + gate(toks[j]) * ys_ref[pl.ds(r0 + j, 1), :]
                    for j in range(cnt)]
            for j in range(cnt):
                acc[pl.ds(toks[j], 1), :] = vals[j]

        def body4(q, carry):
            add_rows(q * 4, 4)
            return carry

        lax.fori_loop(0, n // 4, body4, 0)

        def body1(r, carry):
            add_rows(r, 1)
            return carry

        lax.fori_loop((n // 4) * 4, n, body1, 0)

    @pl.when(i >= MOE_NBLK)
    def _():
        t0 = pl.multiple_of((i - MOE_NBLK) * FIN_TM, FIN_TM)
        y = x1_ref[...] + g2_ref[0] * acc[pl.ds(t0, FIN_TM), :]

        @pl.when(i - MOE_NBLK < T_P // FIN_TM)
        def _():
            yp_ref[...] = y

        @pl.when(i - MOE_NBLK >= T_P // FIN_TM)
        def _():
            ys_out_ref[...] = y


def _moe_combine_call(src, e_b, rows_b, nvb, i1, w1, w2, ys, x1, mod3, l):
    nfin = T // FIN_TM
    smem = pl.BlockSpec(memory_space=pltpu.SMEM)

    def g2_idx(i, *_):
        j = jnp.maximum(i - MOE_NBLK, 0)
        r = jnp.where(j < T_P // FIN_TM, 0, 1 + (j - T_P // FIN_TM) // (S_S // FIN_TM))
        return ((l * 8 + r) * 6 + 5, 0, 0)

    return pl.pallas_call(
        _moe_combine_kernel,
        grid_spec=pltpu.PrefetchScalarGridSpec(
            num_scalar_prefetch=5, grid=(MOE_NBLK + nfin,),
            in_specs=[smem, smem,
                      pl.BlockSpec((MOE_BM, D),
                                   lambda i, s, eb, rw, nv, t1: (jnp.minimum(jnp.minimum(i, MOE_NBLK - 1), nv[0] - 1), 0)),
                      pl.BlockSpec((FIN_TM, D), lambda i, *_: (jnp.maximum(i - MOE_NBLK, 0), 0)),
                      pl.BlockSpec((1, 1, D), g2_idx)],
            out_specs=[pl.BlockSpec((FIN_TM, D),
                                    lambda i, *_: (jnp.clip(i - MOE_NBLK, 0, T_P // FIN_TM - 1), 0)),
                       pl.BlockSpec((FIN_TM, D),
                                    lambda i, *_: (jnp.maximum(i - MOE_NBLK - T_P // FIN_TM, 0), 0))],
            scratch_shapes=[pltpu.VMEM((T, D), F32)]),
        out_shape=[jax.ShapeDtypeStruct((T_P, D), F32), jax.ShapeDtypeStruct((T_S, D), F32)],
        compiler_params=_cparams(("arbitrary",), vmem=52 * 1024 * 1024),
        name="moe_combine",
    )(src, e_b, rows_b, nvb, i1, w1, w2, ys, x1, mod3)


def _moe_layer(xp, x1, route, cnt, mod3, l, w1, w3, w2, i_layer):
    src, e_b, rows_b, nvb, i1, g1, g2 = _route_call(route, cnt)
    xs = _moe_gather_call(src, rows_b, xp)
    ys = _moe_call(xs, e_b, rows_b, nvb, w1, w3, w2, i_layer)
    return _moe_combine_call(src, e_b, rows_b, nvb, i1, g1, g2, ys, x1, mod3, l)


def _pad_cols(w, n):
    return jnp.pad(w, ((0, 0), (0, n - w.shape[1])))


def _tail_selector():
    sel = np.zeros((IN_TAIL, NP_IN - IN_MAIN), np.float32)
    for d in range(2):
        for gate in range(2):
            for head in range(B_HEADS):
                sel[d * 8 + gate * 4 + head, gate * 256 + (head // 2) * LANES + d * 2 + head % 2] = 1.0
    for c in range(C_QRANK):
        sel[16 + c, 512 + c] = 1.0
    for c in range(C_KVRANK):
        sel[16 + C_QRANK + c, 768 + c] = 1.0
    for c in range(C_ROPE):
        sel[16 + C_QRANK + C_KVRANK + c, 896 + C_NOPE + c] = 1.0
    return sel


def _relayout_gate_b(b):
    return jnp.dot(b[None, :], jnp.asarray(_tail_selector()[:16, :512]), precision=lax.Precision.HIGHEST)


def _pad_heads(w, width):
    r = w.shape[0]
    h = w.shape[1] // width
    return jnp.pad(w.reshape(r, h, width), ((0, 0), (0, 0), (0, LANES - width))).reshape(r, h * LANES)


def _rope_tables(half, span_start, period):
    rows = S_S // 64
    r = jnp.repeat(jnp.arange(rows), 64).astype(F32)
    c = jnp.tile(jnp.arange(64), rows).astype(F32)
    n_freq = half // 2
    freq = 10000.0 ** (-jnp.arange(n_freq, dtype=F32) / n_freq)
    ang = jnp.concatenate([r[:, None] * freq, c[:, None] * freq], axis=-1)
    cos, sin = jnp.cos(ang), jnp.sin(ang)
    d = (jnp.arange(LANES) - span_start) % period
    inside = d < 2 * half
    p = jnp.where(inside, d % half, 0)
    first = inside & (d < half)
    second = inside & (d >= half)
    ct = jnp.where(inside[None, :], cos[:, p], 1.0)
    sa = jnp.where(first[None, :], -sin[:, p], 0.0)
    sb = jnp.where(second[None, :], sin[:, p], 0.0)
    ident = (jnp.ones((TM, LANES), F32), jnp.zeros((TM, LANES), F32), jnp.zeros((TM, LANES), F32))
    return tuple(jnp.concatenate([i0, t], axis=0) for i0, t in zip(ident, (ct, sa, sb)))


def kernel(x_prompt, x_sample, c, cache_swa_k, cache_swa_v, cache_mla_ckv, cache_mla_krope, state_mlstm_C, state_mlstm_n, state_mlstm_m, c_ctx, ada_w, ada_b, norm1_g, norm2_g, w_in, a_qn_g, a_kn_g, a_sink, b_gate_b, b_hn_g, c_qa_g, c_kva_g, c_wuq, c_wukv, c_qn_g, c_kn_g, w_out, ffn_w1, ffn_w3, ffn_w2, moe_router, moe_w1, moe_w3, moe_w2):
    x_p, x_s, sample_off = x_prompt.reshape(T_P, D), x_sample.reshape(T_S, D), 0
    cv = jnp.concatenate([c_ctx[None, :], c, jnp.zeros((5, D), F32)], axis=0)
    mod = _ada_call(cv, ada_w, ada_b)
    mod3 = mod.reshape(DEPTH * 8 * 6, 1, D)

    sel = jnp.asarray(_tail_selector(), BF16)
    w_in_t = jnp.swapaxes(w_in, 1, 2)
    rope_a = _rope_tables(32, 0, HD)
    rope_c = _rope_tables(16, C_NOPE, LANES)

    cache_k4 = cache_swa_k.reshape(NB_S, DEPTH, PAST, A_KV * HD)
    cache_v4 = cache_swa_v.reshape(NB_S, DEPTH, PAST, A_KV * HD)

    news = []
    for l in range(DEPTH):
        wuq_p = _pad_heads(c_wuq[l], C_QK).astype(BF16)[None]
        wukv = c_wukv[l].reshape(C_KVRANK, C_HEADS, C_NOPE + C_V)
        wk_p = jnp.pad(wukv[..., :C_NOPE], ((0, 0), (0, 0), (0, LANES - C_NOPE)))
        wk_p = wk_p.reshape(1, C_KVRANK, C_HEADS * LANES).astype(BF16)
        wv_p = wukv[..., C_NOPE:].reshape(1, C_KVRANK, C_HEADS * C_V).astype(BF16)
        w_out_b = w_out[l].astype(BF16)[None]
        gqa = jnp.tile(a_qn_g[l], A_HEADS)[None, :]
        gka = jnp.tile(a_kn_g[l], A_KV)[None, :]
        gb = _relayout_gate_b(b_gate_b[l])
        gqn = jnp.tile(jnp.pad(c_qn_g[l], (0, LANES - C_QK)), C_HEADS)[None, :]
        gkn = jnp.tile(jnp.pad(c_kn_g[l], (0, LANES - C_QK)), C_HEADS)[None, :]
        (qa, ka, va, kaf, vaf, bq, bk, bv, bo, li, lf, qc, ckv_n, ckr) = _in_call(
            x_p, x_s, sample_off, mod3, l, norm1_g[l][None, :], w_in_t, sel, gqa, gka, gb, c_qa_g[l][None, :],
            c_kva_g[l][None, :], wuq_p, gqn, rope_a, rope_c)

        ckr_cache = jnp.pad(cache_mla_krope[:, l].reshape(NB_S * PAST, C_ROPE),
                            ((0, 0), (C_NOPE, LANES - C_NOPE - C_ROPE)))
        kc, vc = _mlakv_call(ckv_n, cache_mla_ckv[:, l].reshape(NB_S * PAST, C_KVRANK), ckr, ckr_cache,
                             l, wk_p, wv_p, gkn, rope_c)

        sink = a_sink[l]
        oa_p = _attn_a_prompt_call(sink, qa, ka, va)
        oa_s = _attn_a_sample_call(sink, qa, ka, va, cache_k4, cache_v4, l)
        oc_p = _mla_prompt_call(qc, kc, vc)
        oc_s = _mla_sample_call(qc, kc, vc)

        hn2 = jnp.tile(b_hn_g[l], 2)[None, :]
        ob_p, cn, nn, mn = _mlstm_call(bq, bk, bv, bo, li, lf, hn2, None, l)
        c0, n0 = state_mlstm_C[:, l], state_mlstm_n[:, l]
        m_st = state_mlstm_m[:, l].reshape(NB_S, 2, 2, 2)
        m0 = jnp.transpose(m_st, (0, 2, 1, 3)).reshape(NB_S * 2, 1, 4)
        m0 = jnp.pad(m0, ((0, 0), (0, 0), (0, LANES - 4)))
        ob_s = _mlstm_call(bq, bk, bv, bo, li, lf, hn2, (c0, n0, m0), l)

        oa, ob, oc = (oa_p, oa_s), (ob_p, ob_s), (oc_p, oc_s)

        if l % 2 == 0:
            x1, xn = _out_call(x_p, x_s, sample_off, oa, ob, oc, w_out_b, mod3, l, norm2_g[l][None, :], None)
            x = _ffn_call(xn, x1, mod3, l, ffn_w1, ffn_w3, ffn_w2, l // 2)
            x_p, x_s, sample_off = x, x, NT_P
        else:
            r = _pad_cols(moe_router[l // 2], LANES)
            rh = r.astype(BF16)
            rl = (r - rh.astype(F32)).astype(BF16)
            x1, xn, route, cnt = _out_call(x_p, x_s, sample_off, oa, ob, oc, w_out_b, mod3, l,
                                           norm2_g[l][None, :], (rh[None], rl[None]))
            x_p, x_s = _moe_layer(xn, x1, route, cnt, mod3, l, moe_w1, moe_w3, moe_w2, l // 2)
            sample_off = 0

        new_k = kaf.reshape(NB_P, A_KV, HD, S_P)
        new_v = vaf.reshape(NB_P, A_KV, HD, S_P)
        new_ckv = ckv_n[:T_P].reshape(NB_P, S_P, C_KVRANK)
        new_kr = ckr[:T_P, C_NOPE:C_NOPE + C_ROPE].reshape(NB_P, S_P, C_ROPE)
        new_c, new_n = cn, nn
        mn4 = mn.reshape(NB_P, 2, LANES)[:, :, :4].reshape(NB_P, 2, 2, 2)
        new_m = jnp.transpose(mn4, (0, 2, 1, 3)).reshape(NB_P, 2, B_HEADS)
        news.append((new_k, new_v, new_ckv, new_kr, new_c, new_n, new_m))

    y_prompt = x_p[:T_P].reshape(NB_P, S_P, D)
    y_sample = x_s[sample_off * TM:sample_off * TM + T_S].reshape(NB_S, S_S, D)
    stacked = [jnp.stack([nw[j] for nw in news], axis=1) for j in range(7)]
    for j in (0, 1):
        stacked[j] = jnp.transpose(stacked[j], (0, 1, 4, 2, 3))
    return (y_prompt, y_sample) + tuple(stacked)
```

```python
import functools

import jax
import jax.numpy as jnp
import numpy as np
from jax import lax
from jax.experimental import pallas as pl
from jax.experimental.pallas import tpu as pltpu

F32 = jnp.float32
BF16 = jnp.bfloat16

D = 1024
NB_P, S_P = 16, 256
NB_S, S_S = 2, 1024
PAST = 512
DEPTH = 2
T_P = NB_P * S_P
T_S = NB_S * S_S
T = T_P + T_S
TM = 512
NT = T // TM
NT_P = T_P // TM
HD = 64
A_HEADS, A_KV = 6, 2
B_HEADS = 4
CHUNK = 64
C_HEADS = 6
C_QRANK, C_KVRANK, C_NOPE, C_ROPE, C_V = 256, 128, 64, 32, 64
C_QK = C_NOPE + C_ROPE
D_FF = 2816
N_EXP = 8
D_FFE = 3584
EPS = 1e-6
NEG = -1e30
LANES = 128
VMEM_LIMIT = 48 * 1024 * 1024

IN_MAIN = 1664
IN_TAIL = 432
SEG = dict(QA=(0, 384), KA=(384, 128), VA=(512, 128), BQ=(640, 256), BK=(896, 256), BV=(1152, 256),
           BO=(1408, 256), GI=(1664, 256), GF=(1920, 256), CQ=(2176, 256), CKV=(2432, 128), CKR=(2560, 128))
NP_IN = 2688


def _cparams(sem, vmem=VMEM_LIMIT):
    return pltpu.CompilerParams(dimension_semantics=sem, vmem_limit_bytes=vmem)


def _dot(a, b):
    return jnp.dot(a, b, preferred_element_type=F32)


def _dot_nt(a, b):
    return lax.dot_general(a, b, (((1,), (1,)), ((), ())), preferred_element_type=F32)


def _lane(shape):
    return lax.broadcasted_iota(jnp.int32, shape, len(shape) - 1)


def _mod_row(i):
    return jnp.where(i < NT_P, 0, 1 + (i - NT_P) // (S_S // TM))


def _pair_specs(n, sample_off):
    return [pl.BlockSpec((TM, n), lambda i: (jnp.minimum(i, NT_P - 1), 0)),
            pl.BlockSpec((TM, n), lambda i: (sample_off + jnp.maximum(i - NT_P, 0), 0))]


def _pick(i, p_ref, s_ref):
    return jnp.where(i < NT_P, p_ref[...], s_ref[...])


def _rope_blk(i):
    return jnp.where((i >= NT_P) & (i < NT), 1 + (i - NT_P) % (S_S // TM), 0)


def _ada_kernel(cv_ref, w_ref, b_ref, o_ref):
    s = cv_ref[...]
    s = s * jax.nn.sigmoid(s)
    o_ref[0] = _dot(s.astype(BF16), w_ref[0].astype(BF16)) + b_ref[0]


def _ada_call(cv, ada_w, ada_b):
    tn = 1536
    return pl.pallas_call(
        _ada_kernel,
        grid=(DEPTH, 6 * D // tn),
        in_specs=[pl.BlockSpec((8, D), lambda l, j: (0, 0)),
                  pl.BlockSpec((1, D, tn), lambda l, j: (l, 0, j)),
                  pl.BlockSpec((1, 1, tn), lambda l, j: (l, 0, j))],
        out_specs=pl.BlockSpec((1, 8, tn), lambda l, j: (l, 0, j)),
        out_shape=jax.ShapeDtypeStruct((DEPTH, 8, 6 * D), F32),
        compiler_params=_cparams(("arbitrary", "arbitrary")),
        name="ada_mod",
    )(cv, ada_w, ada_b.reshape(DEPTH, 1, 6 * D))


def _half_norm(x, g, n):
    outs = []
    for j in range(x.shape[1] // LANES):
        xj = x[:, j * LANES:(j + 1) * LANES]
        lo = _lane(xj.shape) < HD
        s = xj * xj
        s_lo = jnp.sum(jnp.where(lo, s, 0.0), axis=-1, keepdims=True)
        s_hi = jnp.sum(jnp.where(lo, 0.0, s), axis=-1, keepdims=True)
        r = lax.rsqrt(jnp.where(lo, s_lo, s_hi) * (1.0 / n) + EPS)
        outs.append(xj * r)
    return jnp.concatenate(outs, axis=-1) * g


def _group_norm(x, g, n):
    outs = []
    for j in range(x.shape[1] // LANES):
        xj = x[:, j * LANES:(j + 1) * LANES]
        r = lax.rsqrt(jnp.sum(xj * xj, axis=-1, keepdims=True) * (1.0 / n) + EPS)
        outs.append(xj * r)
    return jnp.concatenate(outs, axis=-1) * g


def _rope(x, c, sa, sb, shift):
    outs = []
    for j in range(x.shape[1] // LANES):
        xj = x[:, j * LANES:(j + 1) * LANES]
        outs.append(xj * c + pltpu.roll(xj, LANES - shift, 1) * sa + pltpu.roll(xj, shift, 1) * sb)
    return jnp.concatenate(outs, axis=-1)


def _in_kernel(xp_ref, xs_ref, sh_ref, sc_ref, n1_ref, w_ref, sel_ref, gqa_ref, gka_ref, gb_ref, gcq_ref, gckv_ref,
               wuq_ref, gqn_ref, ra_c, ra_a, ra_b, rc_c, rc_a, rc_b,
               qa_o, ka_o, va_o, kaf_o, vaf_o, bq_o, bk_o, bv_o, bo_o, li_o, lf_o, qc_o, ckv_o, ckr_o,
               wb_scr, wt_scr):
    i = pl.program_id(0)

    @pl.when(i == 0)
    def _():
        wb_scr[...] = w_ref[0, 0:IN_MAIN, :].T.astype(BF16)
        tail_t = w_ref[0, IN_MAIN:IN_MAIN + IN_TAIL, :].astype(BF16)
        wt_scr[...] = lax.dot_general(tail_t, sel_ref[...], (((0,), (0,)), ((), ())),
                                      preferred_element_type=F32).astype(BF16)

    x = _pick(i, xp_ref, xs_ref)
    xn = x * lax.rsqrt(jnp.mean(x * x, axis=-1, keepdims=True) + EPS) * n1_ref[...]
    xn = xn * (1.0 + sc_ref[0]) + sh_ref[0]
    xb = xn.astype(BF16)
    p_main = _dot(xb, wb_scr[...])
    p_tail = _dot(xb, wt_scr[...])

    def seg(name):
        o, n = SEG[name]
        return p_main[:, o:o + n] if o < IN_MAIN else p_tail[:, o - IN_MAIN:o - IN_MAIN + n]

    qa = _half_norm(seg("QA"), gqa_ref[...], HD)
    ka = _half_norm(seg("KA"), gka_ref[...], HD)
    va = seg("VA")

    @pl.when(i < NT_P)
    def _():
        for b in range(TM // S_P):
            kaf_o[b] = ka[b * S_P:(b + 1) * S_P, :].T
            vaf_o[b] = va[b * S_P:(b + 1) * S_P, :].T

    def dup(x):
        lo, hi = _half(x, False), _half(x, True)
        return jnp.concatenate([lo + pltpu.roll(lo, HD, 1), hi + pltpu.roll(hi, HD, 1)], axis=-1)

    va_o[...] = dup(va).astype(BF16)

    bq_o[...] = seg("BQ").astype(BF16)
    bk_o[...] = (seg("BK") * (HD ** -0.5)).astype(BF16)
    bv_o[...] = seg("BV").astype(BF16)
    bo_o[...] = seg("BO")
    gb = gb_ref[...]
    li_o[...] = seg("GI") + gb[:, 0:256]
    lf_o[...] = jax.nn.log_sigmoid(seg("GF") + gb[:, 256:512])

    cq = seg("CQ")
    cqn = cq * lax.rsqrt(jnp.mean(cq * cq, axis=-1, keepdims=True) + EPS) * gcq_ref[...]
    qc = _group_norm(_dot(cqn.astype(BF16), wuq_ref[0]), gqn_ref[...], C_QK)

    @pl.when(i < NT_P)
    def _():
        qa_o[...] = qa.astype(BF16)
        ka_o[...] = dup(ka).astype(BF16)
        qc_o[...] = qc.astype(BF16)

    @pl.when(i >= NT_P)
    def _():
        qa_o[...] = _rope(qa, ra_c[...], ra_a[...], ra_b[...], 32).astype(BF16)
        ka_o[...] = dup(_rope(ka, ra_c[...], ra_a[...], ra_b[...], 32)).astype(BF16)
        qc_o[...] = _rope(qc, rc_c[...], rc_a[...], rc_b[...], 16).astype(BF16)

    ckv = seg("CKV")
    ckv_o[...] = ckv * lax.rsqrt(jnp.mean(ckv * ckv, axis=-1, keepdims=True) + EPS) * gckv_ref[...]
    ckr_o[...] = seg("CKR")


def _in_call(x_p, x_s, sample_off, mod3, l, n1g, w_in, sel, gqa, gka, gb, gcq, gckv, wuq_p, gqn, rope_a, rope_c):
    row = lambda k: pl.BlockSpec((1, 1, D), lambda i: ((l * 8 + _mod_row(i)) * 6 + k, 0, 0))
    vec = lambda n: pl.BlockSpec((1, n), lambda i: (0, 0))
    tab = pl.BlockSpec((TM, LANES), lambda i: (_rope_blk(i), 0))
    tok = lambda n: pl.BlockSpec((TM, n), lambda i: (i, 0))
    tokp = pl.BlockSpec((TM // S_P, LANES, S_P), lambda i: (jnp.minimum(i, NT_P - 1), 0, 0))
    o = lambda n, dt: jax.ShapeDtypeStruct((T, n), dt)
    return pl.pallas_call(
        _in_kernel,
        grid=(NT,),
        in_specs=_pair_specs(D, sample_off) + [row(0), row(1), vec(D),
                  pl.BlockSpec((1, IN_MAIN + IN_TAIL, D), lambda i: (l, 0, 0), pipeline_mode=pl.Buffered(1)),
                  pl.BlockSpec((IN_TAIL, NP_IN - IN_MAIN), lambda i: (0, 0)),
                  vec(384), vec(128), vec(512), vec(256), vec(128),
                  pl.BlockSpec((1, C_QRANK, 768), lambda i: (0, 0, 0)), vec(768),
                  tab, tab, tab, tab, tab, tab],
        out_specs=[tok(384), tok(256), tok(256), tokp, tokp, tok(256), tok(256), tok(256), tok(256),
                   tok(256), tok(256), tok(768), tok(128), tok(128)],
        out_shape=[o(384, BF16), o(256, BF16), o(256, BF16),
                   jax.ShapeDtypeStruct((NB_P, LANES, S_P), F32), jax.ShapeDtypeStruct((NB_P, LANES, S_P), F32),
                   o(256, BF16), o(256, BF16), o(256, BF16), o(256, F32),
                   o(256, F32), o(256, F32), o(768, BF16), o(128, F32), o(128, F32)],
        scratch_shapes=[pltpu.VMEM((D, IN_MAIN), BF16), pltpu.VMEM((D, NP_IN - IN_MAIN), BF16)],
        compiler_params=_cparams(("arbitrary",)),
        name=f"in_proj_l{l}",
    )(x_p, x_s, mod3, mod3, n1g, w_in, sel, gqa, gka, gb, gcq, gckv, wuq_p, gqn, *rope_a, *rope_c)


def _mlakv_kernel(ckv_ref, ckvc_ref, ckr_ref, ckrc_ref, wk_ref, wv_ref, gkn_ref, rc_c, rc_a, rc_b, k_o, v_o):
    i = pl.program_id(0)
    tok = i < NT
    c = jnp.where(tok, ckv_ref[...], ckvc_ref[...]).astype(BF16)
    k = _dot(c, wk_ref[0])
    kr = jnp.where(tok, ckr_ref[...], ckrc_ref[...])
    k = k + jnp.concatenate([kr] * C_HEADS, axis=-1)
    k = _group_norm(k, gkn_ref[...], C_QK)
    sample = (i >= NT_P) & tok

    @pl.when(sample)
    def _():
        k_o[...] = _rope(k, rc_c[...], rc_a[...], rc_b[...], 16).astype(BF16)

    @pl.when(jnp.logical_not(sample))
    def _():
        k_o[...] = k.astype(BF16)

    v_o[...] = _dot(c, wv_ref[0]).astype(BF16)


def _mlakv_call(ckv_n, ckv_cache, ckr, ckr_cache, l, wk_p, wv_p, gkn, rope_c):
    r = T + NB_S * PAST
    tab = pl.BlockSpec((TM, LANES), lambda i: (_rope_blk(i), 0))
    tok = lambda n: pl.BlockSpec((TM, n), lambda i: (i, 0))
    tokens = pl.BlockSpec((TM, LANES), lambda i: (jnp.minimum(i, NT - 1), 0))
    cached = pl.BlockSpec((TM, LANES), lambda i: (jnp.maximum(i - NT, 0), 0))
    return pl.pallas_call(
        _mlakv_kernel,
        grid=(r // TM,),
        in_specs=[tokens, cached, tokens, cached,
                  pl.BlockSpec((1, C_KVRANK, 768), lambda i: (0, 0, 0)),
                  pl.BlockSpec((1, C_KVRANK, 384), lambda i: (0, 0, 0)),
                  pl.BlockSpec((1, 768), lambda i: (0, 0)), tab, tab, tab],
        out_specs=[tok(768), tok(384)],
        out_shape=[jax.ShapeDtypeStruct((r, 768), BF16), jax.ShapeDtypeStruct((r, 384), BF16)],
        compiler_params=_cparams(("arbitrary",)),
        name=f"mla_kv_l{l}",
    )(ckv_n, ckv_cache, ckr, ckr_cache, wk_p, wv_p, gkn, *rope_c)


def _bmm(a, b):
    return lax.dot_general(a, b, (((2,), (1,)), ((0,), (0,))), preferred_element_type=F32)


def _bmm_nt(a, b):
    return lax.dot_general(a, b, (((2,), (2,)), ((0,), (0,))), preferred_element_type=F32)


def _bmm_tn(a, b):
    return lax.dot_general(a, b, (((1,), (1,)), ((0,), (0,))), preferred_element_type=F32)


def _softmax_pv(scores, values, sink):
    batched = scores[0].ndim == 3
    heads = range(scores[0].shape[0]) if batched else ()
    m = scores[0].max(axis=-1, keepdims=True)
    for s in scores[1:]:
        m = jnp.maximum(m, s.max(axis=-1, keepdims=True))
    if sink is not None:
        m = jnp.stack([jnp.maximum(m[h], sink[h]) for h in heads]) if batched else jnp.maximum(m, sink)
    es = [jnp.exp(s - m) for s in scores]
    den = es[0].sum(axis=-1, keepdims=True)
    for e in es[1:]:
        den = den + e.sum(axis=-1, keepdims=True)
    if sink is not None:
        den = (jnp.stack([den[h] + jnp.exp(sink[h] - m[h]) for h in heads]) if batched
               else den + jnp.exp(sink - m))
    inv = 1.0 / den
    out = None
    for e, v in zip(es, values):
        o = (_bmm if batched else _dot)((e * inv).astype(BF16), v)
        out = o if out is None else out + o
    return out


def _scores(q3, k3, scale):
    return jnp.stack([_dot_nt(q3[h], k3[h]) for h in range(q3.shape[0])]) * scale


def _half(x, hi):
    lo = _lane(x.shape) < HD
    return jnp.where(lo != hi, x, jnp.zeros_like(x))


def _pair_sum_store(o3, o_ref):
    for j in range(o3.shape[0] // 2):
        o_ref[:, j * LANES:(j + 1) * LANES] = (o3[2 * j] + o3[2 * j + 1]).astype(BF16)


def _a_heads(q, kdups, vdups):
    q3 = jnp.stack([_half(q[:, (h // 2) * LANES:(h // 2 + 1) * LANES], h % 2 == 1) for h in range(A_HEADS)])
    k3 = [kdups[h // (A_HEADS // A_KV)] for h in range(A_HEADS)]
    v3 = jnp.stack([_half(vdups[h // (A_HEADS // A_KV)], h % 2 == 1) for h in range(A_HEADS)])
    return q3, k3, v3


def _sink3(sink_ref):
    return [sink_ref[h] for h in range(A_HEADS)]


PROMPT_SEQS = 4


def _attn_a_prompt_kernel(sink_ref, q_ref, k_ref, v_ref, o_ref):
    for b in range(PROMPT_SEQS):
        rows = slice(b * S_P, (b + 1) * S_P)
        q = q_ref[rows, :]
        k = k_ref[rows, :]
        v = v_ref[rows, :]
        for j in range(A_HEADS // 2):
            acc = None
            for c in range(2):
                h = 2 * j + c
                g = h // (A_HEADS // A_KV)
                qh = _half(q[:, j * LANES:(j + 1) * LANES], c == 1)
                s = _dot_nt(qh, k[:, g * LANES:(g + 1) * LANES]) * (HD ** -0.5)
                vh = _half(v[:, g * LANES:(g + 1) * LANES], c == 1)
                o = _softmax_pv([s], [vh], sink_ref[h])
                acc = o if acc is None else acc + o
            o_ref[rows, j * LANES:(j + 1) * LANES] = acc.astype(BF16)


def _attn_a_prompt_call(sink, qa, ka, va):
    return pl.pallas_call(
        _attn_a_prompt_kernel,
        grid_spec=pltpu.PrefetchScalarGridSpec(
            num_scalar_prefetch=1, grid=(NB_P // PROMPT_SEQS,),
            in_specs=[pl.BlockSpec((PROMPT_SEQS * S_P, 384), lambda b, s: (b, 0)),
                      pl.BlockSpec((PROMPT_SEQS * S_P, 256), lambda b, s: (b, 0)),
                      pl.BlockSpec((PROMPT_SEQS * S_P, 256), lambda b, s: (b, 0))],
            out_specs=pl.BlockSpec((PROMPT_SEQS * S_P, 384), lambda b, s: (b, 0))),
        out_shape=jax.ShapeDtypeStruct((T_P, 384), BF16),
        compiler_params=_cparams(("arbitrary",)),
        name="attn_a_prompt",
    )(sink, qa, ka, va)


QB = 256
WIN = 128
BAND = QB + 2 * WIN


def _attn_a_sample_kernel(sink_ref, q_ref, k_ref, v_ref, kc_ref, vc_ref, o_ref):
    n = pl.program_id(1)
    ws = pl.multiple_of(jnp.clip(n * QB - WIN, 0, S_S - BAND), WIN)
    kb = k_ref[pl.ds(ws, BAND), :]
    vb = v_ref[pl.ds(ws, BAND), :]
    qpos = n * QB + lax.broadcasted_iota(jnp.int32, (QB, BAND), 0)
    kpos = ws + lax.broadcasted_iota(jnp.int32, (QB, BAND), 1)
    ok = (jnp.abs(qpos - kpos) <= WIN)[None]
    groups = range(A_KV)
    q3, k3, v3 = _a_heads(q_ref[...], [kb[:, g * LANES:(g + 1) * LANES] for g in groups],
                          [vb[:, g * LANES:(g + 1) * LANES] for g in groups])

    def dup(x, g):
        xg = _half(x, g == 1)
        return (xg + pltpu.roll(xg, HD, 1)).astype(BF16)

    kc = kc_ref[0, 0]
    vc = vc_ref[0, 0]
    _, kc3, vc3 = _a_heads(q_ref[...], [dup(kc, g) for g in groups], [dup(vc, g) for g in groups])
    s_b = jnp.where(ok, _scores(q3, k3, HD ** -0.5), NEG)
    s_c = _scores(q3, kc3, HD ** -0.5)
    _pair_sum_store(_softmax_pv([s_b, s_c], [v3, vc3], _sink3(sink_ref)), o_ref)


def _attn_a_sample_call(sink, qa, ka, va, cache_k, cache_v, l):
    nqb = S_S // QB
    off = T_P // S_S
    return pl.pallas_call(
        _attn_a_sample_kernel,
        grid_spec=pltpu.PrefetchScalarGridSpec(
            num_scalar_prefetch=1, grid=(NB_S, nqb),
            in_specs=[pl.BlockSpec((QB, 384), lambda b, n, s: (T_P // QB + b * nqb + n, 0)),
                      pl.BlockSpec((S_S, 256), lambda b, n, s: (off + b, 0)),
                      pl.BlockSpec((S_S, 256), lambda b, n, s: (off + b, 0)),
                      pl.BlockSpec((1, 1, PAST, LANES), lambda b, n, s: (b, l, 0, 0)),
                      pl.BlockSpec((1, 1, PAST, LANES), lambda b, n, s: (b, l, 0, 0))],
            out_specs=pl.BlockSpec((QB, 384), lambda b, n, s: (b * nqb + n, 0))),
        out_shape=jax.ShapeDtypeStruct((T_S, 384), BF16),
        compiler_params=_cparams(("arbitrary", "arbitrary")),
        name=f"attn_a_sample_l{l}",
    )(sink, qa, ka, va, cache_k, cache_v)


def _mla_heads(q, ks, vs, o_ref):
    heads = range(C_HEADS)
    q3 = jnp.stack([q[:, h * LANES:(h + 1) * LANES] for h in heads])
    scores = [_scores(q3, [k[:, h * LANES:(h + 1) * LANES] for h in heads], C_QK ** -0.5) for k in ks]
    vals = [jnp.stack([_half(v[:, (h // 2) * LANES:(h // 2 + 1) * LANES], h % 2 == 1) for h in heads])
            for v in vs]
    _pair_sum_store(_softmax_pv(scores, vals, None), o_ref)


def _mla_prompt_kernel(q_ref, k_ref, v_ref, o_ref):
    for b in range(PROMPT_SEQS):
        rows = pl.ds(b * S_P, S_P)
        _mla_heads(q_ref[rows, :], [k_ref[rows, :]], [v_ref[rows, :]], o_ref.at[rows])


def _mla_prompt_call(qc, kc, vc):
    return pl.pallas_call(
        _mla_prompt_kernel,
        grid=(NB_P // PROMPT_SEQS,),
        in_specs=[pl.BlockSpec((PROMPT_SEQS * S_P, 768), lambda b: (b, 0)),
                  pl.BlockSpec((PROMPT_SEQS * S_P, 768), lambda b: (b, 0)),
                  pl.BlockSpec((PROMPT_SEQS * S_P, 384), lambda b: (b, 0))],
        out_specs=pl.BlockSpec((PROMPT_SEQS * S_P, 384), lambda b: (b, 0)),
        out_shape=jax.ShapeDtypeStruct((T_P, 384), BF16),
        compiler_params=_cparams(("arbitrary",)),
        name="mla_prompt",
    )(qc, kc, vc)


def _mla_sample_kernel(q_ref, kc_ref, vc_ref, kl_ref, vl_ref, o_ref):
    _mla_heads(q_ref[...], [kc_ref[...], kl_ref[...]], [vc_ref[...], vl_ref[...]], o_ref)


def _mla_sample_call(qc, kc, vc):
    tq = 256
    nq = S_S // tq
    return pl.pallas_call(
        _mla_sample_kernel,
        grid=(NB_S, nq),
        in_specs=[pl.BlockSpec((tq, 768), lambda b, n: (T_P // tq + b * nq + n, 0)),
                  pl.BlockSpec((PAST, 768), lambda b, n: (T // PAST + b, 0)),
                  pl.BlockSpec((PAST, 384), lambda b, n: (T // PAST + b, 0)),
                  pl.BlockSpec((S_S, 768), lambda b, n: (T_P // S_S + b, 0)),
                  pl.BlockSpec((S_S, 384), lambda b, n: (T_P // S_S + b, 0))],
        out_specs=pl.BlockSpec((tq, 384), lambda b, n: (b * nq + n, 0)),
        out_shape=jax.ShapeDtypeStruct((T_S, 384), BF16),
        compiler_params=_cparams(("arbitrary", "arbitrary")),
        name="mla_sample",
    )(qc, kc, vc, kc, vc)


def _split3(x):
    x1 = x.astype(BF16)
    r = x - x1.astype(F32)
    x2 = r.astype(BF16)
    x3 = (r - x2.astype(F32)).astype(BF16)
    return x1, x2, x3


def _mlstm_kernel(nc, has_state, *refs):
    for j in range(B_HEADS // 2):
        _mlstm_pair(nc, has_state, j, *refs)


def _mlstm_pair(nc, has_state, j, *refs):
    if has_state:
        (q_ref, k_ref, v_ref, bo_ref, li_ref, lf_ref, hn_ref, c0_ref, n0_ref, m0_ref, ob_ref) = refs
    else:
        (q_ref, k_ref, v_ref, bo_ref, li_ref, lf_ref, hn_ref, ob_ref, cn_ref, nn_ref, mn_ref) = refs
    s_len = nc * CHUNK
    pair = slice(j * LANES, (j + 1) * LANES)
    c3 = lambda x: x.reshape(nc, CHUNK, x.shape[-1])
    li3 = c3(li_ref[:, pair])
    lf3 = c3(lf_ref[:, pair])
    row = lax.broadcasted_iota(jnp.int32, (CHUNK, CHUNK), 0)
    colv = lax.broadcasted_iota(jnp.int32, (CHUNK, CHUNK), 1)
    tri_f = colv <= row
    tri_b = colv >= row
    bcast = lambda m: jnp.broadcast_to(m[None], (nc, CHUNK, CHUNK))
    lane1 = _lane((1, LANES))
    fwd_lane = lane1 < 2

    lf_parts = _split3(lf3)
    tf3 = bcast(tri_f.astype(BF16))
    tb3 = bcast(tri_b.astype(BF16))
    bc_f = _bmm(tf3, lf_parts[0]) + _bmm(tf3, lf_parts[1]) + _bmm(tf3, lf_parts[2])
    bc_b = _bmm(tb3, lf_parts[0]) + _bmm(tb3, lf_parts[1]) + _bmm(tb3, lf_parts[2])
    bc3 = jnp.where(fwd_lane, bc_f, bc_b)
    row3 = lax.broadcasted_iota(jnp.int32, (nc, CHUNK, LANES), 1)
    edge = jnp.where(row3 == jnp.where(fwd_lane, CHUNK - 1, 0), bc3, 0.0)
    bl3 = jnp.sum(edge, axis=1, keepdims=True)
    gg3 = bl3 - bc3 + li3
    bl2 = jnp.sum(edge, axis=1)
    mg2 = gg3.max(axis=1)

    m0 = m0_ref[j] if has_state else jnp.zeros((1, LANES), F32)
    mf = m0
    mb = m0
    mf_prev, mf_next, mb_prev, mb_next = {}, {}, {}, {}
    for i in range(nc):
        mf_prev[i] = mf
        mf = jnp.maximum(bl2[i:i + 1] + mf, mg2[i:i + 1])
        mf_next[i] = mf
        cb = nc - 1 - i
        mb_prev[cb] = mb
        mb = jnp.maximum(bl2[cb:cb + 1] + mb, mg2[cb:cb + 1])
        mb_next[cb] = mb
    m_prev = [jnp.where(fwd_lane, mf_prev[c], mb_prev[c]) for c in range(nc)]
    m_next = [jnp.where(fwd_lane, mf_next[c], mb_next[c]) for c in range(nc)]
    m_prev3 = jnp.stack(m_prev)
    m_next3 = jnp.stack(m_next)
    dec2d = jnp.exp(bl2 + jnp.concatenate(m_prev, axis=0) - jnp.concatenate(m_next, axis=0))
    if not has_state:
        mn_ref[j] = jnp.where(fwd_lane, mf, mb)

    u_parts = _split3(li3 - bc3)
    ws3 = jnp.exp(gg3 - m_next3)
    inter3 = bc3 + m_prev3

    q3 = c3(q_ref[:, pair])
    k3 = c3(k_ref[:, pair])
    v3 = c3(v_ref[:, pair])
    q3f = q3.astype(F32)
    k3f = k3.astype(F32)
    lane3 = _lane((nc, CHUNK, LANES))
    lo3 = lane3 < HD
    rr = lax.broadcasted_iota(jnp.int32, (LANES, LANES), 0)
    cc = lax.broadcasted_iota(jnp.int32, (LANES, LANES), 1)
    blockdiag = (rr < HD) == (cc < HD)
    col = lambda x, k: x[:, :, k:k + 1]

    hsum = None
    for d in range(2):
        causal3 = bcast(tri_f if d == 0 else tri_b)
        a_c, ws_sum, mt_c, ws_c, intra = [], [], [], [], None
        for e in range(2):
            kk = 2 * d + e
            pick = jnp.broadcast_to(jnp.where(_lane((CHUNK, LANES)) == kk, 1.0, 0.0).astype(BF16)[None],
                                    (nc, CHUNK, LANES))
            ub = _bmm_nt(pick, u_parts[0]) + _bmm_nt(pick, u_parts[1]) + _bmm_nt(pick, u_parts[2])
            bc_col = col(bc3, kk)
            d_mat = jnp.where(causal3, bc_col + ub, NEG)
            inter = col(inter3, kk)
            mt = jnp.maximum(inter, d_mat.max(axis=-1, keepdims=True))
            a = jnp.exp(inter - mt)
            w = jnp.exp(d_mat - mt) * _bmm_nt(_half(q3, e == 1), k3)
            o = _bmm(w.astype(BF16), _half(v3, e == 1))
            intra = o if intra is None else intra + o
            a_c.append(a)
            mt_c.append(mt)
            ws_sum.append(w.sum(axis=-1, keepdims=True))
            ws_c.append(col(ws3, kk))
        kw3 = k3f * jnp.where(lo3, ws_c[0], ws_c[1])
        dec2 = jnp.where(_lane((nc, LANES)) < HD, dec2d[:, 2 * d:2 * d + 1], dec2d[:, 2 * d + 1:2 * d + 2])
        u_all = jnp.where(blockdiag, _bmm_tn(kw3.astype(BF16), v3), 0.0)
        kwsum = kw3.sum(axis=1)
        if has_state:
            zero = jnp.zeros((HD, HD), F32)
            cst = jnp.concatenate([jnp.concatenate([c0_ref[0, d, 2 * j], zero], axis=1),
                                   jnp.concatenate([zero, c0_ref[0, d, 2 * j + 1]], axis=1)], axis=0)
            nst = jnp.concatenate([n0_ref[0, d, 2 * j:2 * j + 1, :], n0_ref[0, d, 2 * j + 1:2 * j + 2, :]],
                                  axis=1)
        else:
            cst = jnp.zeros((LANES, LANES), F32)
            nst = jnp.zeros((1, LANES), F32)
        cs, ns = [None] * nc, [None] * nc
        for c in (range(nc) if d == 0 else range(nc - 1, -1, -1)):
            cs[c] = cst
            ns[c] = nst
            cst = dec2[c:c + 1] * cst + u_all[c]
            nst = dec2[c:c + 1] * nst + kwsum[c:c + 1]
        if not has_state:
            for e in range(2):
                half = slice(e * HD, (e + 1) * HD)
                cn_ref[0, d, 2 * j + e] = cst[half, half]
                nn_ref[0, d, 2 * j + e:2 * j + e + 1, :] = nst[:, half]
        qc = _bmm(q3, jnp.stack(cs).astype(BF16))
        qn_all = q3f * jnp.stack(ns)
        dn = []
        for e in range(2):
            qn = jnp.sum(jnp.where(lo3 != (e == 1), qn_all, 0.0), axis=-1, keepdims=True)
            den = a_c[e] * qn + ws_sum[e]
            dn.append(jnp.maximum(jnp.abs(den), jnp.exp(-mt_c[e])))
        h2 = (jnp.where(lo3, a_c[0], a_c[1]) * qc + intra) / jnp.where(lo3, dn[0], dn[1])
        hsum = h2 if hsum is None else hsum + h2

    hs = hsum.reshape(s_len, LANES)
    lo = _lane(hs.shape) < HD
    s = hs * hs
    s_lo = jnp.sum(jnp.where(lo, s, 0.0), axis=-1, keepdims=True)
    s_hi = jnp.sum(jnp.where(lo, 0.0, s), axis=-1, keepdims=True)
    r = lax.rsqrt(jnp.where(lo, s_lo, s_hi) * (1.0 / HD) + EPS)
    ob_ref[:, pair] = (hs * r * hn_ref[...] * jax.nn.sigmoid(bo_ref[:, pair])).astype(BF16)


def _mlstm_call(bq, bk, bv, bo, li, lf, hn2, state, l):
    has_state = state is not None
    if has_state:
        nb, s_len, base = NB_S, S_S, T_P // S_S
    else:
        nb, s_len, base = NB_P, S_P, 0
    nc = s_len // CHUNK
    tokp = pl.BlockSpec((s_len, 2 * LANES), lambda b: (base + b, 0))
    in_specs = [tokp, tokp, tokp, tokp, tokp, tokp, pl.BlockSpec((1, LANES), lambda b: (0, 0))]
    args = [bq, bk, bv, bo, li, lf, hn2]
    ob_spec = pl.BlockSpec((s_len, 2 * LANES), lambda b: (b, 0))
    ob_shape = jax.ShapeDtypeStruct((nb * s_len, 256), BF16)
    state_specs = [pl.BlockSpec((1, 2, B_HEADS, HD, HD), lambda b: (b, 0, 0, 0, 0)),
                   pl.BlockSpec((1, 2, B_HEADS, HD), lambda b: (b, 0, 0, 0)),
                   pl.BlockSpec((2, 1, LANES), lambda b: (b, 0, 0))]
    if has_state:
        in_specs += state_specs
        args += list(state)
        out_specs = ob_spec
        out_shape = ob_shape
    else:
        out_specs = [ob_spec] + state_specs
        out_shape = [ob_shape,
                     jax.ShapeDtypeStruct((nb, 2, B_HEADS, HD, HD), F32),
                     jax.ShapeDtypeStruct((nb, 2, B_HEADS, HD), F32),
                     jax.ShapeDtypeStruct((nb * 2, 1, LANES), F32)]
    return pl.pallas_call(
        functools.partial(_mlstm_kernel, nc, has_state),
        grid=(nb,),
        in_specs=in_specs,
        out_specs=out_specs,
        out_shape=out_shape,
        compiler_params=_cparams(("arbitrary",)),
        name=f"mlstm_{'sample' if has_state else 'prompt'}_l{l}",
    )(*args)


def _out_kernel(moe, xp_ref, xs_ref, oap_ref, oas_ref, obp_ref, obs_ref, ocp_ref, ocs_ref,
                w_ref, g1_ref, sh_ref, sc_ref, n2_ref, *rest):
    if moe:
        rh_ref, rl_ref, x1_o, xn_o, route_o, cnt_o, run_scr = rest
    else:
        x1_o, xn_o = rest
    i = pl.program_id(0)
    o = (_dot(_pick(i, oap_ref, oas_ref), w_ref[0, 0:384, :])
         + _dot(_pick(i, obp_ref, obs_ref), w_ref[0, 384:640, :])
         + _dot(_pick(i, ocp_ref, ocs_ref), w_ref[0, 640:1024, :]))
    x1 = _pick(i, xp_ref, xs_ref) + g1_ref[0] * o
    x1_o[...] = x1
    xn = x1 * lax.rsqrt(jnp.mean(x1 * x1, axis=-1, keepdims=True) + EPS) * n2_ref[...]
    xn = xn * (1.0 + sc_ref[0]) + sh_ref[0]
    xb = xn.astype(BF16)
    if not moe:
        xn_o[...] = xb
    else:
        xn_o[...] = xn
        xl = (xn - xb.astype(F32)).astype(BF16)
        logits = _dot(xb, rh_ref[0]) + (_dot(xl, rh_ref[0]) + _dot(xb, rl_ref[0]))
        lane = _lane(logits.shape)
        logits = jnp.where(lane < N_EXP, logits, -jnp.inf)
        m1 = logits.max(axis=-1, keepdims=True)
        i1 = jnp.min(jnp.where(logits == m1, lane, LANES), axis=-1, keepdims=True)
        rest_l = jnp.where(lane == i1, -jnp.inf, logits)
        m2 = rest_l.max(axis=-1, keepdims=True)
        i2 = jnp.min(jnp.where(rest_l == m2, lane, LANES), axis=-1, keepdims=True)
        e2 = jnp.exp(m2 - m1)
        den = 1.0 + e2
        w1 = 1.0 / den
        w2 = e2 / den

        @pl.when(pl.program_id(0) == 0)
        def _():
            run_scr[...] = jnp.zeros(run_scr.shape, F32)

        sel = jnp.where(lane == i1, 1.0, jnp.where((lane == i2) & (w2 > 0.0), 1.0, 0.0))
        rb = 256
        before = (lax.broadcasted_iota(jnp.int32, (rb, rb), 1)
                  < lax.broadcasted_iota(jnp.int32, (rb, rb), 0)).astype(BF16)
        run = run_scr[...]
        ranks = []
        for k in range(TM // rb):
            part = sel[k * rb:(k + 1) * rb]
            ranks.append(run + _dot(before, part.astype(BF16)))
            run = run + jnp.sum(part, axis=0, keepdims=True)
        rank = jnp.concatenate(ranks, axis=0)
        r1 = jnp.sum(jnp.where(lane == i1, rank, 0.0), axis=-1, keepdims=True)
        r2 = jnp.sum(jnp.where(lane == i2, rank, 0.0), axis=-1, keepdims=True)
        run_scr[...] = run
        cnt_o[...] = run_scr[...]
        fields = [i1.astype(F32), i2.astype(F32), r1, r2, w1, w2]
        info = jnp.zeros(lane.shape, F32)
        for k, v in enumerate(fields):
            info = jnp.where(lane == k, v, info)
        route_o[...] = info.T


def _out_call(x_p, x_s, sample_off, oa, ob, oc, w_out_b, mod3, l, n2g, router):
    moe = router is not None
    row = lambda k: pl.BlockSpec((1, 1, D), lambda i: ((l * 8 + _mod_row(i)) * 6 + k, 0, 0))
    tok = lambda n: pl.BlockSpec((TM, n), lambda i: (i, 0))
    in_specs = (_pair_specs(D, sample_off) + _pair_specs(384, 0) + _pair_specs(256, 0) + _pair_specs(384, 0)
                + [pl.BlockSpec((1, D, D), lambda i: (0, 0, 0)),
                   row(2), row(3), row(4), pl.BlockSpec((1, D), lambda i: (0, 0))])
    args = [x_p, x_s, *oa, *ob, *oc, w_out_b, mod3, mod3, mod3, n2g]
    out_specs = [tok(D), tok(D)]
    out_shape = [jax.ShapeDtypeStruct((T, D), F32), jax.ShapeDtypeStruct((T, D), BF16)]
    if moe:
        out_shape[1] = jax.ShapeDtypeStruct((T, D), F32)
        rh, rl = router
        in_specs += [pl.BlockSpec((1, D, LANES), lambda i: (0, 0, 0))] * 2
        args += [rh, rl]
        out_specs += [pl.BlockSpec((LANES, TM), lambda i: (0, i)), pl.BlockSpec((1, LANES), lambda i: (0, 0))]
        out_shape += [jax.ShapeDtypeStruct((LANES, T), F32), jax.ShapeDtypeStruct((1, LANES), F32)]
    return pl.pallas_call(
        functools.partial(_out_kernel, moe),
        grid=(NT,),
        in_specs=in_specs, out_specs=out_specs, out_shape=out_shape,
        scratch_shapes=[pltpu.VMEM((1, LANES), F32)] if moe else [],
        compiler_params=_cparams(("arbitrary",)),
        name=f"out_proj_l{l}",
    )(*args)


FFN_TM = 1024
FFN_TF = 256
FFN_TN = 512


def _hidden_tile(x, w1_ref, w3_ref):
    a = _dot(x, w1_ref[...].astype(BF16))
    b = _dot(x, w3_ref[...].astype(BF16))
    return (a * jax.nn.sigmoid(a) * b).astype(BF16)


def _down_tile(h_scr, w2n, m):
    out = None
    for f in range(h_scr.shape[0]):
        o = _dot(h_scr[f, 0:m, :], w2n[f])
        out = o if out is None else out + o
    return out


def _keep_weights(s, w1_ref, w3_ref, w2_ref, w1b, w3b, w2b, tn):
    w1b[s] = w1_ref[...].astype(BF16)
    w3b[s] = w3_ref[...].astype(BF16)
    w2 = w2_ref[...].astype(BF16)
    for n in range(w2b.shape[0]):
        w2b[n, s] = w2[:, n * tn:(n + 1) * tn]


def _ffn_kernel(nf, xn_ref, x1_ref, g2_ref, w1_ref, w3_ref, w2_ref, y_ref, h_scr, w1b, w3b, w2b):
    i = pl.program_id(0)
    s = pl.program_id(1)

    @pl.when((i == 0) & (s < nf))
    def _():
        _keep_weights(s, w1_ref.at[0], w3_ref.at[0], w2_ref.at[0], w1b, w3b, w2b, FFN_TN)

    @pl.when(s < nf)
    def _():
        h_scr[s] = _hidden_tile(xn_ref[...], w1b.at[s], w3b.at[s])

    @pl.when(s >= nf)
    def _():
        y_ref[...] = x1_ref[...] + g2_ref[0] * _down_tile(h_scr, w2b.at[s - nf], FFN_TM)


def _ffn_call(xn, x1, mod3, l, w1, w3, w2, i_layer):
    nf = D_FF // FFN_TF
    nn = D // FFN_TN
    tile = lambda i, s: jnp.where(i == 0, jnp.minimum(s, nf - 1), nf - 1)
    out = lambda s: jnp.maximum(s - nf, 0)

    def g2_idx(i, s):
        r = jnp.where(i < T_P // FFN_TM, 0, 1 + (i - T_P // FFN_TM) // (S_S // FFN_TM))
        return ((l * 8 + r) * 6 + 5, 0, out(s))

    return pl.pallas_call(
        functools.partial(_ffn_kernel, nf),
        grid=(T // FFN_TM, nf + nn),
        in_specs=[pl.BlockSpec((FFN_TM, D), lambda i, s: (i, 0)),
                  pl.BlockSpec((FFN_TM, FFN_TN), lambda i, s: (i, out(s))),
                  pl.BlockSpec((1, 1, FFN_TN), g2_idx),
                  pl.BlockSpec((1, D, FFN_TF), lambda i, s: (i_layer, 0, tile(i, s))),
                  pl.BlockSpec((1, D, FFN_TF), lambda i, s: (i_layer, 0, tile(i, s))),
                  pl.BlockSpec((1, FFN_TF, D), lambda i, s: (i_layer, tile(i, s), 0))],
        out_specs=pl.BlockSpec((FFN_TM, FFN_TN), lambda i, s: (i, out(s))),
        out_shape=jax.ShapeDtypeStruct((T, D), F32),
        scratch_shapes=[pltpu.VMEM((nf, FFN_TM, FFN_TF), BF16),
                        pltpu.VMEM((nf, D, FFN_TF), BF16), pltpu.VMEM((nf, D, FFN_TF), BF16),
                        pltpu.VMEM((nn, nf, FFN_TF, FFN_TN), BF16)],
        compiler_params=_cparams(("arbitrary", "arbitrary")),
        name="ffn_dense",
    )(xn, x1, mod3, w1, w3, w2)


MOE_BM = 1024
MOE_SUB = 256
MOE_NBLK = 2 * T // MOE_BM + N_EXP
MOE_NR = MOE_NBLK * MOE_BM
MOE_TF = 512
MOE_TN = 512
FIN_TM = 512


def _route_kernel(i1_ref, i2_ref, r1_ref, r2_ref, cnt_ref, w2_ref,
                  src_ref, eb_ref, rows_ref, nvb_ref, off_scr):
    def clear(r, carry):
        src_ref[r] = 0
        return carry

    lax.fori_loop(0, MOE_NR + 8, clear, 0, unroll=8)

    def clear_blk(b, carry):
        eb_ref[b] = 0
        rows_ref[b] = 0
        return carry

    lax.fori_loop(0, MOE_NBLK, clear_blk, 0)

    nblk = jnp.int32(0)
    for e in range(N_EXP):
        c = cnt_ref[e]
        nbe = (c + (MOE_BM - 1)) // MOE_BM
        off_scr[e] = nblk * MOE_BM

        def fill(j, carry, e=e, c=c, nblk=nblk):
            eb_ref[nblk + j] = e
            rows_ref[nblk + j] = jnp.minimum(c - j * MOE_BM, MOE_BM)
            return carry

        lax.fori_loop(0, nbe, fill, 0)
        nblk = nblk + nbe
    nvb_ref[0] = nblk

    def place(t, carry):
        src_ref[off_scr[i1_ref[t]] + r1_ref[t]] = t
        src_ref[jnp.where(w2_ref[t] > 0.0, off_scr[i2_ref[t]] + r2_ref[t], MOE_NR)] = t
        return carry

    lax.fori_loop(0, T, place, 0, unroll=4)


def _route_call(route, cnt):
    ints = [route[k].astype(jnp.int32) for k in range(4)]
    cnt8 = cnt[0, :N_EXP].astype(jnp.int32)
    smem = pl.BlockSpec(memory_space=pltpu.SMEM)
    outs = pl.pallas_call(
        _route_kernel,
        grid_spec=pltpu.PrefetchScalarGridSpec(
            num_scalar_prefetch=5, grid=(1,),
            in_specs=[smem],
            out_specs=[smem, smem, smem, smem],
            scratch_shapes=[pltpu.SMEM((N_EXP,), jnp.int32)]),
        out_shape=[jax.ShapeDtypeStruct((MOE_NR + 8,), jnp.int32),
                   jax.ShapeDtypeStruct((MOE_NBLK,), jnp.int32), jax.ShapeDtypeStruct((MOE_NBLK,), jnp.int32),
                   jax.ShapeDtypeStruct((1,), jnp.int32)],
        compiler_params=_cparams(("arbitrary",)),
        name="moe_route",
    )(*ints, cnt8, route[5])
    return tuple(outs) + (ints[0], route[4], route[5])


def _moe_gather_kernel(src_ref, rows_ref, x_ref, o_ref, rows_scr):
    b = pl.program_id(0)
    for q in range(MOE_BM // MOE_SUB):
        base = b * MOE_BM + q * MOE_SUB
        used = rows_ref[b] > q * MOE_SUB
        sub = pl.ds(q * MOE_SUB, MOE_SUB)

        @pl.when(used)
        def _(base=base, sub=sub):
            def body(i, carry):
                for j in range(8):
                    r = i * 8 + j
                    rows_scr[pl.ds(r, 1), :] = x_ref[pl.ds(src_ref[base + r], 1), :]
                return carry

            lax.fori_loop(0, MOE_SUB // 8, body, 0)
            o_ref[sub, :] = rows_scr[...].astype(BF16)

        @pl.when(jnp.logical_not(used))
        def _(sub=sub):
            o_ref[sub, :] = jnp.zeros((MOE_SUB, D), BF16)


def _moe_gather_call(src, rows_b, xn):
    return pl.pallas_call(
        _moe_gather_kernel,
        grid_spec=pltpu.PrefetchScalarGridSpec(
            num_scalar_prefetch=2, grid=(MOE_NBLK,),
            in_specs=[pl.BlockSpec((T, D), lambda b, s, rw: (0, 0), pipeline_mode=pl.Buffered(1))],
            out_specs=pl.BlockSpec((MOE_BM, D), lambda b, s, rw: (b, 0)),
            scratch_shapes=[pltpu.VMEM((MOE_SUB, D), F32)]),
        out_shape=jax.ShapeDtypeStruct((MOE_NR, D), BF16),
        compiler_params=_cparams(("arbitrary",)),
        name="moe_gather",
    )(src, rows_b, xn)


def _moe_first(b, eb_ref):
    return (b == 0) | (eb_ref[b] != eb_ref[jnp.maximum(b - 1, 0)])


def _moe_kernel(nf, eb_ref, rows_ref, nvb_ref, xs_ref, w1_ref, w3_ref, w2_ref, y_ref, h_scr, w1b, w3b, w2b):
    b = pl.program_id(0)
    s = pl.program_id(1)
    nsub = jnp.where(b < nvb_ref[0], (rows_ref[b] + MOE_SUB - 1) // MOE_SUB, 0)
    sizes = [k * MOE_SUB for k in range(1, MOE_BM // MOE_SUB + 1)]

    @pl.when((s < nf) & (nsub > 0) & _moe_first(b, eb_ref))
    def _():
        _keep_weights(s, w1_ref.at[0, 0], w3_ref.at[0, 0], w2_ref.at[0, 0], w1b, w3b, w2b, MOE_TN)

    @pl.when((s < nf) & (nsub > 0))
    def _():
        for k, m in enumerate(sizes, start=1):
            @pl.when(nsub == k)
            def _(m=m):
                h_scr[s, 0:m, :] = _hidden_tile(xs_ref[0:m, :], w1b.at[s], w3b.at[s])

    @pl.when((s >= nf) & (nsub > 0))
    def _():
        for k, m in enumerate(sizes, start=1):
            @pl.when(nsub == k)
            def _(m=m):
                y_ref[0:m, :] = _down_tile(h_scr, w2b.at[s - nf], m)
                if m < MOE_BM:
                    y_ref[m:MOE_BM, :] = jnp.zeros((MOE_BM - m, MOE_TN), F32)

    @pl.when((s >= nf) & (nsub == 0))
    def _():
        y_ref[...] = jnp.zeros(y_ref.shape, F32)


def _moe_call(xs, e_b, rows_b, nvb, w1, w3, w2, i_layer):
    nf = D_FFE // MOE_TF
    nn = D // MOE_TN

    def blk(b, nv):
        return jnp.minimum(b, nv[0] - 1)

    def tile(b, s, eb, nv):
        stream = (b < nv[0]) & _moe_first(b, eb)
        return jnp.where(stream, jnp.minimum(s, nf - 1), nf - 1)

    return pl.pallas_call(
        functools.partial(_moe_kernel, nf),
        grid_spec=pltpu.PrefetchScalarGridSpec(
            num_scalar_prefetch=3, grid=(MOE_NBLK, nf + nn),
            in_specs=[pl.BlockSpec((MOE_BM, D), lambda b, s, eb, rw, nv: (blk(b, nv), 0)),
                      pl.BlockSpec((1, 1, D, MOE_TF),
                                   lambda b, s, eb, rw, nv: (i_layer, eb[blk(b, nv)], 0, tile(b, s, eb, nv))),
                      pl.BlockSpec((1, 1, D, MOE_TF),
                                   lambda b, s, eb, rw, nv: (i_layer, eb[blk(b, nv)], 0, tile(b, s, eb, nv))),
                      pl.BlockSpec((1, 1, MOE_TF, D),
                                   lambda b, s, eb, rw, nv: (i_layer, eb[blk(b, nv)], tile(b, s, eb, nv), 0))],
            out_specs=pl.BlockSpec((MOE_BM, MOE_TN), lambda b, s, eb, rw, nv: (b, jnp.maximum(s - nf, 0))),
            scratch_shapes=[pltpu.VMEM((nf, MOE_BM, MOE_TF), BF16),
                            pltpu.VMEM((nf, D, MOE_TF), BF16), pltpu.VMEM((nf, D, MOE_TF), BF16),
                            pltpu.VMEM((nn, nf, MOE_TF, MOE_TN), BF16)]),
        out_shape=jax.ShapeDtypeStruct((MOE_NR, D), F32),
        compiler_params=_cparams(("arbitrary", "arbitrary"), vmem=56 * 1024 * 1024),
        name="moe_experts",
    )(e_b, rows_b, nvb, xs, w1, w3, w2)


def _moe_combine_kernel(src_ref, eb_ref, rows_ref, nvb_ref, i1_ref, w1_ref, w2_ref,
                        ys_ref, x1_ref, g2_ref, yp_ref, ys_out_ref, acc):
    i = pl.program_id(0)

    @pl.when(i == 0)
    def _():
        acc[...] = jnp.zeros(acc.shape, F32)

    @pl.when(i < nvb_ref[0])
    def _():
        base = i * MOE_BM
        n = rows_ref[i]
        e = eb_ref[i]

        def gate(t):
            return jnp.where(i1_ref[t] == e, w1_ref[t], w2_ref[t])

        def add_rows(r0, cnt):
            toks = [src_ref[base + r0 + j] for j in range(cnt)]
            vals = [acc[pl.ds(toks[j], 1), :] + gate(toks[j]) * ys_ref[pl.ds(r0 + j, 1), :]
                    for j in range(cnt)]
            for j in range(cnt):
                acc[pl.ds(toks[j], 1), :] = vals[j]

        def body4(q, carry):
            add_rows(q * 4, 4)
            return carry

        lax.fori_loop(0, n // 4, body4, 0)

        def body1(r, carry):
            add_rows(r, 1)
            return carry

        lax.fori_loop((n // 4) * 4, n, body1, 0)

    @pl.when(i >= MOE_NBLK)
    def _():
        t0 = pl.multiple_of((i - MOE_NBLK) * FIN_TM, FIN_TM)
        y = x1_ref[...] + g2_ref[0] * acc[pl.ds(t0, FIN_TM), :]

        @pl.when(i - MOE_NBLK < T_P // FIN_TM)
        def _():
            yp_ref[...] = y

        @pl.when(i - MOE_NBLK >= T_P // FIN_TM)
        def _():
            ys_out_ref[...] = y


def _moe_combine_call(src, e_b, rows_b, nvb, i1, w1, w2, ys, x1, mod3, l):
    nfin = T // FIN_TM
    smem = pl.BlockSpec(memory_space=pltpu.SMEM)

    def g2_idx(i, *_):
        j = jnp.maximum(i - MOE_NBLK, 0)
        r = jnp.where(j < T_P // FIN_TM, 0, 1 + (j - T_P // FIN_TM) // (S_S // FIN_TM))
        return ((l * 8 + r) * 6 + 5, 0, 0)

    return pl.pallas_call(
        _moe_combine_kernel,
        grid_spec=pltpu.PrefetchScalarGridSpec(
            num_scalar_prefetch=5, grid=(MOE_NBLK + nfin,),
            in_specs=[smem, smem,
                      pl.BlockSpec((MOE_BM, D),
                                   lambda i, s, eb, rw, nv, t1: (jnp.minimum(jnp.minimum(i, MOE_NBLK - 1), nv[0] - 1), 0)),
                      pl.BlockSpec((FIN_TM, D), lambda i, *_: (jnp.maximum(i - MOE_NBLK, 0), 0)),
                      pl.BlockSpec((1, 1, D), g2_idx)],
            out_specs=[pl.BlockSpec((FIN_TM, D),
                                    lambda i, *_: (jnp.clip(i - MOE_NBLK, 0, T_P // FIN_TM - 1), 0)),
                       pl.BlockSpec((FIN_TM, D),
                                    lambda i, *_: (jnp.maximum(i - MOE_NBLK - T_P // FIN_TM, 0), 0))],
            scratch_shapes=[pltpu.VMEM((T, D), F32)]),
        out_shape=[jax.ShapeDtypeStruct((T_P, D), F32), jax.ShapeDtypeStruct((T_S, D), F32)],
        compiler_params=_cparams(("arbitrary",), vmem=52 * 1024 * 1024),
        name="moe_combine",
    )(src, e_b, rows_b, nvb, i1, w1, w2, ys, x1, mod3)


def _moe_layer(xp, x1, route, cnt, mod3, l, w1, w3, w2, i_layer):
    src, e_b, rows_b, nvb, i1, g1, g2 = _route_call(route, cnt)
    xs = _moe_gather_call(src, rows_b, xp)
    ys = _moe_call(xs, e_b, rows_b, nvb, w1, w3, w2, i_layer)
    return _moe_combine_call(src, e_b, rows_b, nvb, i1, g1, g2, ys, x1, mod3, l)


def _pad_cols(w, n):
    return jnp.pad(w, ((0, 0), (0, n - w.shape[1])))


def _tail_selector():
    sel = np.zeros((IN_TAIL, NP_IN - IN_MAIN), np.float32)
    for d in range(2):
        for gate in range(2):
            for head in range(B_HEADS):
                sel[d * 8 + gate * 4 + head, gate * 256 + (head // 2) * LANES + d * 2 + head % 2] = 1.0
    for c in range(C_QRANK):
        sel[16 + c, 512 + c] = 1.0
    for c in range(C_KVRANK):
        sel[16 + C_QRANK + c, 768 + c] = 1.0
    for c in range(C_ROPE):
        sel[16 + C_QRANK + C_KVRANK + c, 896 + C_NOPE + c] = 1.0
    return sel


def _relayout_gate_b(b):
    return jnp.dot(b[None, :], jnp.asarray(_tail_selector()[:16, :512]), precision=lax.Precision.HIGHEST)


def _pad_heads(w, width):
    r = w.shape[0]
    h = w.shape[1] // width
    return jnp.pad(w.reshape(r, h, width), ((0, 0), (0, 0), (0, LANES - width))).reshape(r, h * LANES)


def _rope_tables(half, span_start, period):
    rows = S_S // 64
    r = jnp.repeat(jnp.arange(rows), 64).astype(F32)
    c = jnp.tile(jnp.arange(64), rows).astype(F32)
    n_freq = half // 2
    freq = 10000.0 ** (-jnp.arange(n_freq, dtype=F32) / n_freq)
    ang = jnp.concatenate([r[:, None] * freq, c[:, None] * freq], axis=-1)
    cos, sin = jnp.cos(ang), jnp.sin(ang)
    d = (jnp.arange(LANES) - span_start) % period
    inside = d < 2 * half
    p = jnp.where(inside, d % half, 0)
    first = inside & (d < half)
    second = inside & (d >= half)
    ct = jnp.where(inside[None, :], cos[:, p], 1.0)
    sa = jnp.where(first[None, :], -sin[:, p], 0.0)
    sb = jnp.where(second[None, :], sin[:, p], 0.0)
    ident = (jnp.ones((TM, LANES), F32), jnp.zeros((TM, LANES), F32), jnp.zeros((TM, LANES), F32))
    return tuple(jnp.concatenate([i0, t], axis=0) for i0, t in zip(ident, (ct, sa, sb)))


def kernel(x_prompt, x_sample, c, cache_swa_k, cache_swa_v, cache_mla_ckv, cache_mla_krope, state_mlstm_C, state_mlstm_n, state_mlstm_m, c_ctx, ada_w, ada_b, norm1_g, norm2_g, w_in, a_qn_g, a_kn_g, a_sink, b_gate_b, b_hn_g, c_qa_g, c_kva_g, c_wuq, c_wukv, c_qn_g, c_kn_g, w_out, ffn_w1, ffn_w3, ffn_w2, moe_router, moe_w1, moe_w3, moe_w2):
    x_p, x_s, sample_off = x_prompt.reshape(T_P, D), x_sample.reshape(T_S, D), 0
    cv = jnp.concatenate([c_ctx[None, :], c, jnp.zeros((5, D), F32)], axis=0)
    mod = _ada_call(cv, ada_w, ada_b)
    mod3 = mod.reshape(DEPTH * 8 * 6, 1, D)

    sel = jnp.asarray(_tail_selector(), BF16)
    w_in_t = jnp.swapaxes(w_in, 1, 2)
    rope_a = _rope_tables(32, 0, HD)
    rope_c = _rope_tables(16, C_NOPE, LANES)

    cache_k4 = cache_swa_k.reshape(NB_S, DEPTH, PAST, A_KV * HD)
    cache_v4 = cache_swa_v.reshape(NB_S, DEPTH, PAST, A_KV * HD)

    news = []
    for l in range(DEPTH):
        wuq_p = _pad_heads(c_wuq[l], C_QK).astype(BF16)[None]
        wukv = c_wukv[l].reshape(C_KVRANK, C_HEADS, C_NOPE + C_V)
        wk_p = jnp.pad(wukv[..., :C_NOPE], ((0, 0), (0, 0), (0, LANES - C_NOPE)))
        wk_p = wk_p.reshape(1, C_KVRANK, C_HEADS * LANES).astype(BF16)
        wv_p = wukv[..., C_NOPE:].reshape(1, C_KVRANK, C_HEADS * C_V).astype(BF16)
        w_out_b = w_out[l].astype(BF16)[None]
        gqa = jnp.tile(a_qn_g[l], A_HEADS)[None, :]
        gka = jnp.tile(a_kn_g[l], A_KV)[None, :]
        gb = _relayout_gate_b(b_gate_b[l])
        gqn = jnp.tile(jnp.pad(c_qn_g[l], (0, LANES - C_QK)), C_HEADS)[None, :]
        gkn = jnp.tile(jnp.pad(c_kn_g[l], (0, LANES - C_QK)), C_HEADS)[None, :]
        (qa, ka, va, kaf, vaf, bq, bk, bv, bo, li, lf, qc, ckv_n, ckr) = _in_call(
            x_p, x_s, sample_off, mod3, l, norm1_g[l][None, :], w_in_t, sel, gqa, gka, gb, c_qa_g[l][None, :],
            c_kva_g[l][None, :], wuq_p, gqn, rope_a, rope_c)

        ckr_cache = jnp.pad(cache_mla_krope[:, l].reshape(NB_S * PAST, C_ROPE),
                            ((0, 0), (C_NOPE, LANES - C_NOPE - C_ROPE)))
        kc, vc = _mlakv_call(ckv_n, cache_mla_ckv[:, l].reshape(NB_S * PAST, C_KVRANK), ckr, ckr_cache,
                             l, wk_p, wv_p, gkn, rope_c)

        sink = a_sink[l]
        oa_p = _attn_a_prompt_call(sink, qa, ka, va)
        oa_s = _attn_a_sample_call(sink, qa, ka, va, cache_k4, cache_v4, l)
        oc_p = _mla_prompt_call(qc, kc, vc)
        oc_s = _mla_sample_call(qc, kc, vc)

        hn2 = jnp.tile(b_hn_g[l], 2)[None, :]
        ob_p, cn, nn, mn = _mlstm_call(bq, bk, bv, bo, li, lf, hn2, None, l)
        c0, n0 = state_mlstm_C[:, l], state_mlstm_n[:, l]
        m_st = state_mlstm_m[:, l].reshape(NB_S, 2, 2, 2)
        m0 = jnp.transpose(m_st, (0, 2, 1, 3)).reshape(NB_S * 2, 1, 4)
        m0 = jnp.pad(m0, ((0, 0), (0, 0), (0, LANES - 4)))
        ob_s = _mlstm_call(bq, bk, bv, bo, li, lf, hn2, (c0, n0, m0), l)

        oa, ob, oc = (oa_p, oa_s), (ob_p, ob_s), (oc_p, oc_s)

        if l % 2 == 0:
            x1, xn = _out_call(x_p, x_s, sample_off, oa, ob, oc, w_out_b, mod3, l, norm2_g[l][None, :], None)
            x = _ffn_call(xn, x1, mod3, l, ffn_w1, ffn_w3, ffn_w2, l // 2)
            x_p, x_s, sample_off = x, x, NT_P
        else:
            r = _pad_cols(moe_router[l // 2], LANES)
            rh = r.astype(BF16)
            rl = (r - rh.astype(F32)).astype(BF16)
            x1, xn, route, cnt = _out_call(x_p, x_s, sample_off, oa, ob, oc, w_out_b, mod3, l,
                                           norm2_g[l][None, :], (rh[None], rl[None]))
            x_p, x_s = _moe_layer(xn, x1, route, cnt, mod3, l, moe_w1, moe_w3, moe_w2, l // 2)
            sample_off = 0

        new_k = kaf.reshape(NB_P, A_KV, HD, S_P)
        new_v = vaf.reshape(NB_P, A_KV, HD, S_P)
        new_ckv = ckv_n[:T_P].reshape(NB_P, S_P, C_KVRANK)
        new_kr = ckr[:T_P, C_NOPE:C_NOPE + C_ROPE].reshape(NB_P, S_P, C_ROPE)
        new_c, new_n = cn, nn
        mn4 = mn.reshape(NB_P, 2, LANES)[:, :, :4].reshape(NB_P, 2, 2, 2)
        new_m = jnp.transpose(mn4, (0, 2, 1, 3)).reshape(NB_P, 2, B_HEADS)
        news.append((new_k, new_v, new_ckv, new_kr, new_c, new_n, new_m))

    y_prompt = x_p[:T_P].reshape(NB_P, S_P, D)
    y_sample = x_s[sample_off * TM:sample_off * TM + T_S].reshape(NB_S, S_S, D)
    stacked = [jnp.stack([nw[j] for nw in news], axis=1) for j in range(7)]
    for j in (0, 1):
        stacked[j] = jnp.transpose(stacked[j], (0, 1, 4, 2, 3))
    return (y_prompt, y_sample) + tuple(stacked)
```

```python
import functools

import jax
import jax.numpy as jnp
import numpy as np
from jax import lax
from jax.experimental import pallas as pl
from jax.experimental.pallas import tpu as pltpu

F32 = jnp.float32
BF16 = jnp.bfloat16

D = 1024
NB_P, S_P = 16, 256
NB_S, S_S = 2, 1024
PAST = 512
DEPTH = 2
T_P = NB_P * S_P
T_S = NB_S * S_S
T = T_P + T_S
TM = 512
NT = T // TM
NT_P = T_P // TM
HD = 64
A_HEADS, A_KV = 6, 2
B_HEADS = 4
CHUNK = 64
C_HEADS = 6
C_QRANK, C_KVRANK, C_NOPE, C_ROPE, C_V = 256, 128, 64, 32, 64
C_QK = C_NOPE + C_ROPE
D_FF = 2816
N_EXP = 8
D_FFE = 3584
EPS = 1e-6
NEG = -1e30
LANES = 128
VMEM_LIMIT = 48 * 1024 * 1024

IN_MAIN = 1664
IN_TAIL = 432
SEG = dict(QA=(0, 384), KA=(384, 128), VA=(512, 128), BQ=(640, 256), BK=(896, 256), BV=(1152, 256),
           BO=(1408, 256), GI=(1664, 256), GF=(1920, 256), CQ=(2176, 256), CKV=(2432, 128), CKR=(2560, 128))
NP_IN = 2688


def _cparams(sem, vmem=VMEM_LIMIT):
    return pltpu.CompilerParams(dimension_semantics=sem, vmem_limit_bytes=vmem)


def _dot(a, b):
    return jnp.dot(a, b, preferred_element_type=F32)


def _dot_nt(a, b):
    return lax.dot_general(a, b, (((1,), (1,)), ((), ())), preferred_element_type=F32)


def _lane(shape):
    return lax.broadcasted_iota(jnp.int32, shape, len(shape) - 1)


def _mod_row(i):
    return jnp.where(i < NT_P, 0, 1 + (i - NT_P) // (S_S // TM))


def _pair_specs(n, sample_off):
    return [pl.BlockSpec((TM, n), lambda i: (jnp.minimum(i, NT_P - 1), 0)),
            pl.BlockSpec((TM, n), lambda i: (sample_off + jnp.maximum(i - NT_P, 0), 0))]


def _pick(i, p_ref, s_ref):
    return jnp.where(i < NT_P, p_ref[...], s_ref[...])


def _rope_blk(i):
    return jnp.where((i >= NT_P) & (i < NT), 1 + (i - NT_P) % (S_S // TM), 0)


def _ada_kernel(cv_ref, w_ref, b_ref, o_ref):
    s = cv_ref[...]
    s = s * jax.nn.sigmoid(s)
    o_ref[0] = _dot(s.astype(BF16), w_ref[0].astype(BF16)) + b_ref[0]


def _ada_call(cv, ada_w, ada_b):
    tn = 1536
    return pl.pallas_call(
        _ada_kernel,
        grid=(DEPTH, 6 * D // tn),
        in_specs=[pl.BlockSpec((8, D), lambda l, j: (0, 0)),
                  pl.BlockSpec((1, D, tn), lambda l, j: (l, 0, j)),
                  pl.BlockSpec((1, 1, tn), lambda l, j: (l, 0, j))],
        out_specs=pl.BlockSpec((1, 8, tn), lambda l, j: (l, 0, j)),
        out_shape=jax.ShapeDtypeStruct((DEPTH, 8, 6 * D), F32),
        compiler_params=_cparams(("arbitrary", "arbitrary")),
        name="ada_mod",
    )(cv, ada_w, ada_b.reshape(DEPTH, 1, 6 * D))


def _half_norm(x, g, n):
    outs = []
    for j in range(x.shape[1] // LANES):
        xj = x[:, j * LANES:(j + 1) * LANES]
        lo = _lane(xj.shape) < HD
        s = xj * xj
        s_lo = jnp.sum(jnp.where(lo, s, 0.0), axis=-1, keepdims=True)
        s_hi = jnp.sum(jnp.where(lo, 0.0, s), axis=-1, keepdims=True)
        r = lax.rsqrt(jnp.where(lo, s_lo, s_hi) * (1.0 / n) + EPS)
        outs.append(xj * r)
    return jnp.concatenate(outs, axis=-1) * g


def _group_norm(x, g, n):
    outs = []
    for j in range(x.shape[1] // LANES):
        xj = x[:, j * LANES:(j + 1) * LANES]
        r = lax.rsqrt(jnp.sum(xj * xj, axis=-1, keepdims=True) * (1.0 / n) + EPS)
        outs.append(xj * r)
    return jnp.concatenate(outs, axis=-1) * g


def _rope(x, c, sa, sb, shift):
    outs = []
    for j in range(x.shape[1] // LANES):
        xj = x[:, j * LANES:(j + 1) * LANES]
        outs.append(xj * c + pltpu.roll(xj, LANES - shift, 1) * sa + pltpu.roll(xj, shift, 1) * sb)
    return jnp.concatenate(outs, axis=-1)


def _in_kernel(xp_ref, xs_ref, sh_ref, sc_ref, n1_ref, w_ref, sel_ref, gqa_ref, gka_ref, gb_ref, gcq_ref, gckv_ref,
               wuq_ref, gqn_ref, ra_c, ra_a, ra_b, rc_c, rc_a, rc_b,
               qa_o, ka_o, va_o, kaf_o, vaf_o, bq_o, bk_o, bv_o, bo_o, li_o, lf_o, qc_o, ckv_o, ckr_o,
               wb_scr, wt_scr):
    i = pl.program_id(0)

    @pl.when(i == 0)
    def _():
        wb_scr[...] = w_ref[0, 0:IN_MAIN, :].T.astype(BF16)
        tail_t = w_ref[0, IN_MAIN:IN_MAIN + IN_TAIL, :].astype(BF16)
        wt_scr[...] = lax.dot_general(tail_t, sel_ref[...], (((0,), (0,)), ((), ())),
                                      preferred_element_type=F32).astype(BF16)

    x = _pick(i, xp_ref, xs_ref)
    xn = x * lax.rsqrt(jnp.mean(x * x, axis=-1, keepdims=True) + EPS) * n1_ref[...]
    xn = xn * (1.0 + sc_ref[0]) + sh_ref[0]
    xb = xn.astype(BF16)
    p_main = _dot(xb, wb_scr[...])
    p_tail = _dot(xb, wt_scr[...])

    def seg(name):
        o, n = SEG[name]
        return p_main[:, o:o + n] if o < IN_MAIN else p_tail[:, o - IN_MAIN:o - IN_MAIN + n]

    qa = _half_norm(seg("QA"), gqa_ref[...], HD)
    ka = _half_norm(seg("KA"), gka_ref[...], HD)
    va = seg("VA")

    @pl.when(i < NT_P)
    def _():
        for b in range(TM // S_P):
            kaf_o[b] = ka[b * S_P:(b + 1) * S_P, :].T
            vaf_o[b] = va[b * S_P:(b + 1) * S_P, :].T

    def dup(x):
        lo, hi = _half(x, False), _half(x, True)
        return jnp.concatenate([lo + pltpu.roll(lo, HD, 1), hi + pltpu.roll(hi, HD, 1)], axis=-1)

    va_o[...] = dup(va).astype(BF16)

    bq_o[...] = seg("BQ").astype(BF16)
    bk_o[...] = (seg("BK") * (HD ** -0.5)).astype(BF16)
    bv_o[...] = seg("BV").astype(BF16)
    bo_o[...] = seg("BO")
    gb = gb_ref[...]
    li_o[...] = seg("GI") + gb[:, 0:256]
    lf_o[...] = jax.nn.log_sigmoid(seg("GF") + gb[:, 256:512])

    cq = seg("CQ")
    cqn = cq * lax.rsqrt(jnp.mean(cq * cq, axis=-1, keepdims=True) + EPS) * gcq_ref[...]
    qc = _group_norm(_dot(cqn.astype(BF16), wuq_ref[0]), gqn_ref[...], C_QK)

    @pl.when(i < NT_P)
    def _():
        qa_o[...] = qa.astype(BF16)
        ka_o[...] = dup(ka).astype(BF16)
        qc_o[...] = qc.astype(BF16)

    @pl.when(i >= NT_P)
    def _():
        qa_o[...] = _rope(qa, ra_c[...], ra_a[...], ra_b[...], 32).astype(BF16)
        ka_o[...] = dup(_rope(ka, ra_c[...], ra_a[...], ra_b[...], 32)).astype(BF16)
        qc_o[...] = _rope(qc, rc_c[...], rc_a[...], rc_b[...], 16).astype(BF16)

    ckv = seg("CKV")
    ckv_o[...] = ckv * lax.rsqrt(jnp.mean(ckv * ckv, axis=-1, keepdims=True) + EPS) * gckv_ref[...]
    ckr_o[...] = seg("CKR")


def _in_call(x_p, x_s, sample_off, mod3, l, n1g, w_in, sel, gqa, gka, gb, gcq, gckv, wuq_p, gqn, rope_a, rope_c):
    row = lambda k: pl.BlockSpec((1, 1, D), lambda i: ((l * 8 + _mod_row(i)) * 6 + k, 0, 0))
    vec = lambda n: pl.BlockSpec((1, n), lambda i: (0, 0))
    tab = pl.BlockSpec((TM, LANES), lambda i: (_rope_blk(i), 0))
    tok = lambda n: pl.BlockSpec((TM, n), lambda i: (i, 0))
    tokp = pl.BlockSpec((TM // S_P, LANES, S_P), lambda i: (jnp.minimum(i, NT_P - 1), 0, 0))
    o = lambda n, dt: jax.ShapeDtypeStruct((T, n), dt)
    return pl.pallas_call(
        _in_kernel,
        grid=(NT,),
        in_specs=_pair_specs(D, sample_off) + [row(0), row(1), vec(D),
                  pl.BlockSpec((1, IN_MAIN + IN_TAIL, D), lambda i: (l, 0, 0), pipeline_mode=pl.Buffered(1)),
                  pl.BlockSpec((IN_TAIL, NP_IN - IN_MAIN), lambda i: (0, 0)),
                  vec(384), vec(128), vec(512), vec(256), vec(128),
                  pl.BlockSpec((1, C_QRANK, 768), lambda i: (0, 0, 0)), vec(768),
                  tab, tab, tab, tab, tab, tab],
        out_specs=[tok(384), tok(256), tok(256), tokp, tokp, tok(256), tok(256), tok(256), tok(256),
                   tok(256), tok(256), tok(768), tok(128), tok(128)],
        out_shape=[o(384, BF16), o(256, BF16), o(256, BF16),
                   jax.ShapeDtypeStruct((NB_P, LANES, S_P), F32), jax.ShapeDtypeStruct((NB_P, LANES, S_P), F32),
                   o(256, BF16), o(256, BF16), o(256, BF16), o(256, F32),
                   o(256, F32), o(256, F32), o(768, BF16), o(128, F32), o(128, F32)],
        scratch_shapes=[pltpu.VMEM((D, IN_MAIN), BF16), pltpu.VMEM((D, NP_IN - IN_MAIN), BF16)],
        compiler_params=_cparams(("arbitrary",)),
        name=f"in_proj_l{l}",
    )(x_p, x_s, mod3, mod3, n1g, w_in, sel, gqa, gka, gb, gcq, gckv, wuq_p, gqn, *rope_a, *rope_c)


def _mlakv_kernel(ckv_ref, ckvc_ref, ckr_ref, ckrc_ref, wk_ref, wv_ref, gkn_ref, rc_c, rc_a, rc_b, k_o, v_o):
    i = pl.program_id(0)
    tok = i < NT
    c = jnp.where(tok, ckv_ref[...], ckvc_ref[...]).astype(BF16)
    k = _dot(c, wk_ref[0])
    kr = jnp.where(tok, ckr_ref[...], ckrc_ref[...])
    k = k + jnp.concatenate([kr] * C_HEADS, axis=-1)
    k = _group_norm(k, gkn_ref[...], C_QK)
    sample = (i >= NT_P) & tok

    @pl.when(sample)
    def _():
        k_o[...] = _rope(k, rc_c[...], rc_a[...], rc_b[...], 16).astype(BF16)

    @pl.when(jnp.logical_not(sample))
    def _():
        k_o[...] = k.astype(BF16)

    v_o[...] = _dot(c, wv_ref[0]).astype(BF16)


def _mlakv_call(ckv_n, ckv_cache, ckr, ckr_cache, l, wk_p, wv_p, gkn, rope_c):
    r = T + NB_S * PAST
    tab = pl.BlockSpec((TM, LANES), lambda i: (_rope_blk(i), 0))
    tok = lambda n: pl.BlockSpec((TM, n), lambda i: (i, 0))
    tokens = pl.BlockSpec((TM, LANES), lambda i: (jnp.minimum(i, NT - 1), 0))
    cached = pl.BlockSpec((TM, LANES), lambda i: (jnp.maximum(i - NT, 0), 0))
    return pl.pallas_call(
        _mlakv_kernel,
        grid=(r // TM,),
        in_specs=[tokens, cached, tokens, cached,
                  pl.BlockSpec((1, C_KVRANK, 768), lambda i: (0, 0, 0)),
                  pl.BlockSpec((1, C_KVRANK, 384), lambda i: (0, 0, 0)),
                  pl.BlockSpec((1, 768), lambda i: (0, 0)), tab, tab, tab],
        out_specs=[tok(768), tok(384)],
        out_shape=[jax.ShapeDtypeStruct((r, 768), BF16), jax.ShapeDtypeStruct((r, 384), BF16)],
        compiler_params=_cparams(("arbitrary",)),
        name=f"mla_kv_l{l}",
    )(ckv_n, ckv_cache, ckr, ckr_cache, wk_p, wv_p, gkn, *rope_c)


def _bmm(a, b):
    return lax.dot_general(a, b, (((2,), (1,)), ((0,), (0,))), preferred_element_type=F32)


def _bmm_nt(a, b):
    return lax.dot_general(a, b, (((2,), (2,)), ((0,), (0,))), preferred_element_type=F32)


def _bmm_tn(a, b):
    return lax.dot_general(a, b, (((1,), (1,)), ((0,), (0,))), preferred_element_type=F32)


def _softmax_pv(scores, values, sink):
    batched = scores[0].ndim == 3
    heads = range(scores[0].shape[0]) if batched else ()
    m = scores[0].max(axis=-1, keepdims=True)
    for s in scores[1:]:
        m = jnp.maximum(m, s.max(axis=-1, keepdims=True))
    if sink is not None:
        m = jnp.stack([jnp.maximum(m[h], sink[h]) for h in heads]) if batched else jnp.maximum(m, sink)
    es = [jnp.exp(s - m) for s in scores]
    den = es[0].sum(axis=-1, keepdims=True)
    for e in es[1:]:
        den = den + e.sum(axis=-1, keepdims=True)
    if sink is not None:
        den = (jnp.stack([den[h] + jnp.exp(sink[h] - m[h]) for h in heads]) if batched
               else den + jnp.exp(sink - m))
    inv = 1.0 / den
    out = None
    for e, v in zip(es, values):
        o = (_bmm if batched else _dot)((e * inv).astype(BF16), v)
        out = o if out is None else out + o
    return out


def _scores(q3, k3, scale):
    return jnp.stack([_dot_nt(q3[h], k3[h]) for h in range(q3.shape[0])]) * scale


def _half(x, hi):
    lo = _lane(x.shape) < HD
    return jnp.where(lo != hi, x, jnp.zeros_like(x))


def _pair_sum_store(o3, o_ref):
    for j in range(o3.shape[0] // 2):
        o_ref[:, j * LANES:(j + 1) * LANES] = (o3[2 * j] + o3[2 * j + 1]).astype(BF16)


def _a_heads(q, kdups, vdups):
    q3 = jnp.stack([_half(q[:, (h // 2) * LANES:(h // 2 + 1) * LANES], h % 2 == 1) for h in range(A_HEADS)])
    k3 = [kdups[h // (A_HEADS // A_KV)] for h in range(A_HEADS)]
    v3 = jnp.stack([_half(vdups[h // (A_HEADS // A_KV)], h % 2 == 1) for h in range(A_HEADS)])
    return q3, k3, v3


def _sink3(sink_ref):
    return [sink_ref[h] for h in range(A_HEADS)]


PROMPT_SEQS = 2


def _attn_a_prompt_kernel(sink_ref, q_ref, k_ref, v_ref, o_ref):
    for b in range(PROMPT_SEQS):
        rows = slice(b * S_P, (b + 1) * S_P)
        q = q_ref[rows, :]
        k = k_ref[rows, :]
        v = v_ref[rows, :]
        for j in range(A_HEADS // 2):
            acc = None
            for c in range(2):
                h = 2 * j + c
                g = h // (A_HEADS // A_KV)
                qh = _half(q[:, j * LANES:(j + 1) * LANES], c == 1)
                s = _dot_nt(qh, k[:, g * LANES:(g + 1) * LANES]) * (HD ** -0.5)
                vh = _half(v[:, g * LANES:(g + 1) * LANES], c == 1)
                o = _softmax_pv([s], [vh], sink_ref[h])
                acc = o if acc is None else acc + o
            o_ref[rows, j * LANES:(j + 1) * LANES] = acc.astype(BF16)


def _attn_a_prompt_call(sink, qa, ka, va):
    return pl.pallas_call(
        _attn_a_prompt_kernel,
        grid_spec=pltpu.PrefetchScalarGridSpec(
            num_scalar_prefetch=1, grid=(NB_P // PROMPT_SEQS,),
            in_specs=[pl.BlockSpec((PROMPT_SEQS * S_P, 384), lambda b, s: (b, 0)),
                      pl.BlockSpec((PROMPT_SEQS * S_P, 256), lambda b, s: (b, 0)),
                      pl.BlockSpec((PROMPT_SEQS * S_P, 256), lambda b, s: (b, 0))],
            out_specs=pl.BlockSpec((PROMPT_SEQS * S_P, 384), lambda b, s: (b, 0))),
        out_shape=jax.ShapeDtypeStruct((T_P, 384), BF16),
        compiler_params=_cparams(("arbitrary",)),
        name="attn_a_prompt",
    )(sink, qa, ka, va)


QB = 128
WIN = 128
BAND = QB + 2 * WIN


def _attn_a_sample_kernel(sink_ref, q_ref, k_ref, v_ref, kc_ref, vc_ref, o_ref):
    n = pl.program_id(1)
    ws = pl.multiple_of(jnp.clip(n * QB - WIN, 0, S_S - BAND), QB)
    kb = k_ref[pl.ds(ws, BAND), :]
    vb = v_ref[pl.ds(ws, BAND), :]
    qpos = n * QB + lax.broadcasted_iota(jnp.int32, (QB, BAND), 0)
    kpos = ws + lax.broadcasted_iota(jnp.int32, (QB, BAND), 1)
    ok = (jnp.abs(qpos - kpos) <= WIN)[None]
    groups = range(A_KV)
    q3, k3, v3 = _a_heads(q_ref[...], [kb[:, g * LANES:(g + 1) * LANES] for g in groups],
                          [vb[:, g * LANES:(g + 1) * LANES] for g in groups])

    def dup(x, g):
        xg = _half(x, g == 1)
        return (xg + pltpu.roll(xg, HD, 1)).astype(BF16)

    kc = kc_ref[0, 0]
    vc = vc_ref[0, 0]
    _, kc3, vc3 = _a_heads(q_ref[...], [dup(kc, g) for g in groups], [dup(vc, g) for g in groups])
    s_b = jnp.where(ok, _scores(q3, k3, HD ** -0.5), NEG)
    s_c = _scores(q3, kc3, HD ** -0.5)
    _pair_sum_store(_softmax_pv([s_b, s_c], [v3, vc3], _sink3(sink_ref)), o_ref)


def _attn_a_sample_call(sink, qa, ka, va, cache_k, cache_v, l):
    nqb = S_S // QB
    off = T_P // S_S
    return pl.pallas_call(
        _attn_a_sample_kernel,
        grid_spec=pltpu.PrefetchScalarGridSpec(
            num_scalar_prefetch=1, grid=(NB_S, nqb),
            in_specs=[pl.BlockSpec((QB, 384), lambda b, n, s: (T_P // QB + b * nqb + n, 0)),
                      pl.BlockSpec((S_S, 256), lambda b, n, s: (off + b, 0)),
                      pl.BlockSpec((S_S, 256), lambda b, n, s: (off + b, 0)),
                      pl.BlockSpec((1, 1, PAST, LANES), lambda b, n, s: (b, l, 0, 0)),
                      pl.BlockSpec((1, 1, PAST, LANES), lambda b, n, s: (b, l, 0, 0))],
            out_specs=pl.BlockSpec((QB, 384), lambda b, n, s: (b * nqb + n, 0))),
        out_shape=jax.ShapeDtypeStruct((T_S, 384), BF16),
        compiler_params=_cparams(("arbitrary", "arbitrary")),
        name=f"attn_a_sample_l{l}",
    )(sink, qa, ka, va, cache_k, cache_v)


def _mla_heads(q, ks, vs, o_ref):
    heads = range(C_HEADS)
    q3 = jnp.stack([q[:, h * LANES:(h + 1) * LANES] for h in heads])
    scores = [_scores(q3, [k[:, h * LANES:(h + 1) * LANES] for h in heads], C_QK ** -0.5) for k in ks]
    vals = [jnp.stack([_half(v[:, (h // 2) * LANES:(h // 2 + 1) * LANES], h % 2 == 1) for h in heads])
            for v in vs]
    _pair_sum_store(_softmax_pv(scores, vals, None), o_ref)


def _mla_prompt_kernel(q_ref, k_ref, v_ref, o_ref):
    for b in range(PROMPT_SEQS):
        rows = pl.ds(b * S_P, S_P)
        _mla_heads(q_ref[rows, :], [k_ref[rows, :]], [v_ref[rows, :]], o_ref.at[rows])


def _mla_prompt_call(qc, kc, vc):
    return pl.pallas_call(
        _mla_prompt_kernel,
        grid=(NB_P // PROMPT_SEQS,),
        in_specs=[pl.BlockSpec((PROMPT_SEQS * S_P, 768), lambda b: (b, 0)),
                  pl.BlockSpec((PROMPT_SEQS * S_P, 768), lambda b: (b, 0)),
                  pl.BlockSpec((PROMPT_SEQS * S_P, 384), lambda b: (b, 0))],
        out_specs=pl.BlockSpec((PROMPT_SEQS * S_P, 384), lambda b: (b, 0)),
        out_shape=jax.ShapeDtypeStruct((T_P, 384), BF16),
        compiler_params=_cparams(("arbitrary",)),
        name="mla_prompt",
    )(qc, kc, vc)


def _mla_sample_kernel(q_ref, kc_ref, vc_ref, kl_ref, vl_ref, o_ref):
    _mla_heads(q_ref[...], [kc_ref[...], kl_ref[...]], [vc_ref[...], vl_ref[...]], o_ref)


def _mla_sample_call(qc, kc, vc):
    tq = 256
    nq = S_S // tq
    return pl.pallas_call(
        _mla_sample_kernel,
        grid=(NB_S, nq),
        in_specs=[pl.BlockSpec((tq, 768), lambda b, n: (T_P // tq + b * nq + n, 0)),
                  pl.BlockSpec((PAST, 768), lambda b, n: (T // PAST + b, 0)),
                  pl.BlockSpec((PAST, 384), lambda b, n: (T // PAST + b, 0)),
                  pl.BlockSpec((S_S, 768), lambda b, n: (T_P // S_S + b, 0)),
                  pl.BlockSpec((S_S, 384), lambda b, n: (T_P // S_S + b, 0))],
        out_specs=pl.BlockSpec((tq, 384), lambda b, n: (b * nq + n, 0)),
        out_shape=jax.ShapeDtypeStruct((T_S, 384), BF16),
        compiler_params=_cparams(("arbitrary", "arbitrary")),
        name="mla_sample",
    )(qc, kc, vc, kc, vc)


def _split3(x):
    x1 = x.astype(BF16)
    r = x - x1.astype(F32)
    x2 = r.astype(BF16)
    x3 = (r - x2.astype(F32)).astype(BF16)
    return x1, x2, x3


def _mlstm_kernel(nc, has_state, *refs):
    for j in range(B_HEADS // 2):
        _mlstm_pair(nc, has_state, j, *refs)


def _mlstm_pair(nc, has_state, j, *refs):
    if has_state:
        (q_ref, k_ref, v_ref, bo_ref, li_ref, lf_ref, hn_ref, c0_ref, n0_ref, m0_ref, ob_ref) = refs
    else:
        (q_ref, k_ref, v_ref, bo_ref, li_ref, lf_ref, hn_ref, ob_ref, cn_ref, nn_ref, mn_ref) = refs
    s_len = nc * CHUNK
    pair = slice(j * LANES, (j + 1) * LANES)
    c3 = lambda x: x.reshape(nc, CHUNK, x.shape[-1])
    li3 = c3(li_ref[:, pair])
    lf3 = c3(lf_ref[:, pair])
    row = lax.broadcasted_iota(jnp.int32, (CHUNK, CHUNK), 0)
    colv = lax.broadcasted_iota(jnp.int32, (CHUNK, CHUNK), 1)
    tri_f = colv <= row
    tri_b = colv >= row
    bcast = lambda m: jnp.broadcast_to(m[None], (nc, CHUNK, CHUNK))
    lane1 = _lane((1, LANES))
    fwd_lane = lane1 < 2

    lf_parts = _split3(lf3)
    tf3 = bcast(tri_f.astype(BF16))
    tb3 = bcast(tri_b.astype(BF16))
    bc_f = _bmm(tf3, lf_parts[0]) + _bmm(tf3, lf_parts[1]) + _bmm(tf3, lf_parts[2])
    bc_b = _bmm(tb3, lf_parts[0]) + _bmm(tb3, lf_parts[1]) + _bmm(tb3, lf_parts[2])
    bc3 = jnp.where(fwd_lane, bc_f, bc_b)
    row3 = lax.broadcasted_iota(jnp.int32, (nc, CHUNK, LANES), 1)
    edge = jnp.where(row3 == jnp.where(fwd_lane, CHUNK - 1, 0), bc3, 0.0)
    bl3 = jnp.sum(edge, axis=1, keepdims=True)
    gg3 = bl3 - bc3 + li3
    bl2 = jnp.sum(edge, axis=1)
    mg2 = gg3.max(axis=1)

    m0 = m0_ref[j] if has_state else jnp.zeros((1, LANES), F32)
    mf = m0
    mb = m0
    mf_prev, mf_next, mb_prev, mb_next = {}, {}, {}, {}
    for i in range(nc):
        mf_prev[i] = mf
        mf = jnp.maximum(bl2[i:i + 1] + mf, mg2[i:i + 1])
        mf_next[i] = mf
        cb = nc - 1 - i
        mb_prev[cb] = mb
        mb = jnp.maximum(bl2[cb:cb + 1] + mb, mg2[cb:cb + 1])
        mb_next[cb] = mb
    m_prev = [jnp.where(fwd_lane, mf_prev[c], mb_prev[c]) for c in range(nc)]
    m_next = [jnp.where(fwd_lane, mf_next[c], mb_next[c]) for c in range(nc)]
    m_prev3 = jnp.stack(m_prev)
    m_next3 = jnp.stack(m_next)
    dec2d = jnp.exp(bl2 + jnp.concatenate(m_prev, axis=0) - jnp.concatenate(m_next, axis=0))
    if not has_state:
        mn_ref[j] = jnp.where(fwd_lane, mf, mb)

    u_parts = _split3(li3 - bc3)
    ws3 = jnp.exp(gg3 - m_next3)
    inter3 = bc3 + m_prev3

    q3 = c3(q_ref[:, pair])
    k3 = c3(k_ref[:, pair])
    v3 = c3(v_ref[:, pair])
    q3f = q3.astype(F32)
    k3f = k3.astype(F32)
    lane3 = _lane((nc, CHUNK, LANES))
    lo3 = lane3 < HD
    rr = lax.broadcasted_iota(jnp.int32, (LANES, LANES), 0)
    cc = lax.broadcasted_iota(jnp.int32, (LANES, LANES), 1)
    blockdiag = (rr < HD) == (cc < HD)
    col = lambda x, k: x[:, :, k:k + 1]

    hsum = None
    for d in range(2):
        causal3 = bcast(tri_f if d == 0 else tri_b)
        a_c, ws_sum, mt_c, ws_c, intra = [], [], [], [], None
        for e in range(2):
            kk = 2 * d + e
            pick = jnp.broadcast_to(jnp.where(_lane((CHUNK, LANES)) == kk, 1.0, 0.0).astype(BF16)[None],
                                    (nc, CHUNK, LANES))
            ub = _bmm_nt(pick, u_parts[0]) + _bmm_nt(pick, u_parts[1]) + _bmm_nt(pick, u_parts[2])
            bc_col = col(bc3, kk)
            d_mat = jnp.where(causal3, bc_col + ub, NEG)
            inter = col(inter3, kk)
            mt = jnp.maximum(inter, d_mat.max(axis=-1, keepdims=True))
            a = jnp.exp(inter - mt)
            w = jnp.exp(d_mat - mt) * _bmm_nt(_half(q3, e == 1), k3)
            o = _bmm(w.astype(BF16), _half(v3, e == 1))
            intra = o if intra is None else intra + o
            a_c.append(a)
            mt_c.append(mt)
            ws_sum.append(w.sum(axis=-1, keepdims=True))
            ws_c.append(col(ws3, kk))
        kw3 = k3f * jnp.where(lo3, ws_c[0], ws_c[1])
        dec2 = jnp.where(_lane((nc, LANES)) < HD, dec2d[:, 2 * d:2 * d + 1], dec2d[:, 2 * d + 1:2 * d + 2])
        u_all = jnp.where(blockdiag, _bmm_tn(kw3.astype(BF16), v3), 0.0)
        kwsum = kw3.sum(axis=1)
        if has_state:
            zero = jnp.zeros((HD, HD), F32)
            cst = jnp.concatenate([jnp.concatenate([c0_ref[0, d, 2 * j], zero], axis=1),
                                   jnp.concatenate([zero, c0_ref[0, d, 2 * j + 1]], axis=1)], axis=0)
            nst = jnp.concatenate([n0_ref[0, d, 2 * j:2 * j + 1, :], n0_ref[0, d, 2 * j + 1:2 * j + 2, :]],
                                  axis=1)
        else:
            cst = jnp.zeros((LANES, LANES), F32)
            nst = jnp.zeros((1, LANES), F32)
        cs, ns = [None] * nc, [None] * nc
        for c in (range(nc) if d == 0 else range(nc - 1, -1, -1)):
            cs[c] = cst
            ns[c] = nst
            cst = dec2[c:c + 1] * cst + u_all[c]
            nst = dec2[c:c + 1] * nst + kwsum[c:c + 1]
        if not has_state:
            for e in range(2):
                half = slice(e * HD, (e + 1) * HD)
                cn_ref[0, d, 2 * j + e] = cst[half, half]
                nn_ref[0, d, 2 * j + e:2 * j + e + 1, :] = nst[:, half]
        qc = _bmm(q3, jnp.stack(cs).astype(BF16))
        qn_all = q3f * jnp.stack(ns)
        dn = []
        for e in range(2):
            qn = jnp.sum(jnp.where(lo3 != (e == 1), qn_all, 0.0), axis=-1, keepdims=True)
            den = a_c[e] * qn + ws_sum[e]
            dn.append(jnp.maximum(jnp.abs(den), jnp.exp(-mt_c[e])))
        h2 = (jnp.where(lo3, a_c[0], a_c[1]) * qc + intra) / jnp.where(lo3, dn[0], dn[1])
        hsum = h2 if hsum is None else hsum + h2

    hs = hsum.reshape(s_len, LANES)
    lo = _lane(hs.shape) < HD
    s = hs * hs
    s_lo = jnp.sum(jnp.where(lo, s, 0.0), axis=-1, keepdims=True)
    s_hi = jnp.sum(jnp.where(lo, 0.0, s), axis=-1, keepdims=True)
    r = lax.rsqrt(jnp.where(lo, s_lo, s_hi) * (1.0 / HD) + EPS)
    ob_ref[:, pair] = (hs * r * hn_ref[...] * jax.nn.sigmoid(bo_ref[:, pair])).astype(BF16)


def _mlstm_call(bq, bk, bv, bo, li, lf, hn2, state, l):
    has_state = state is not None
    if has_state:
        nb, s_len, base = NB_S, S_S, T_P // S_S
    else:
        nb, s_len, base = NB_P, S_P, 0
    nc = s_len // CHUNK
    tokp = pl.BlockSpec((s_len, 2 * LANES), lambda b: (base + b, 0))
    in_specs = [tokp, tokp, tokp, tokp, tokp, tokp, pl.BlockSpec((1, LANES), lambda b: (0, 0))]
    args = [bq, bk, bv, bo, li, lf, hn2]
    ob_spec = pl.BlockSpec((s_len, 2 * LANES), lambda b: (b, 0))
    ob_shape = jax.ShapeDtypeStruct((nb * s_len, 256), BF16)
    state_specs = [pl.BlockSpec((1, 2, B_HEADS, HD, HD), lambda b: (b, 0, 0, 0, 0)),
                   pl.BlockSpec((1, 2, B_HEADS, HD), lambda b: (b, 0, 0, 0)),
                   pl.BlockSpec((2, 1, LANES), lambda b: (b, 0, 0))]
    if has_state:
        in_specs += state_specs
        args += list(state)
        out_specs = ob_spec
        out_shape = ob_shape
    else:
        out_specs = [ob_spec] + state_specs
        out_shape = [ob_shape,
                     jax.ShapeDtypeStruct((nb, 2, B_HEADS, HD, HD), F32),
                     jax.ShapeDtypeStruct((nb, 2, B_HEADS, HD), F32),
                     jax.ShapeDtypeStruct((nb * 2, 1, LANES), F32)]
    return pl.pallas_call(
        functools.partial(_mlstm_kernel, nc, has_state),
        grid=(nb,),
        in_specs=in_specs,
        out_specs=out_specs,
        out_shape=out_shape,
        compiler_params=_cparams(("arbitrary",)),
        name=f"mlstm_{'sample' if has_state else 'prompt'}_l{l}",
    )(*args)


def _out_kernel(moe, xp_ref, xs_ref, oap_ref, oas_ref, obp_ref, obs_ref, ocp_ref, ocs_ref,
                w_ref, g1_ref, sh_ref, sc_ref, n2_ref, *rest):
    if moe:
        rh_ref, rl_ref, x1_o, xn_o, route_o, cnt_o, run_scr = rest
    else:
        x1_o, xn_o = rest
    i = pl.program_id(0)
    o = (_dot(_pick(i, oap_ref, oas_ref), w_ref[0, 0:384, :])
         + _dot(_pick(i, obp_ref, obs_ref), w_ref[0, 384:640, :])
         + _dot(_pick(i, ocp_ref, ocs_ref), w_ref[0, 640:1024, :]))
    x1 = _pick(i, xp_ref, xs_ref) + g1_ref[0] * o
    x1_o[...] = x1
    xn = x1 * lax.rsqrt(jnp.mean(x1 * x1, axis=-1, keepdims=True) + EPS) * n2_ref[...]
    xn = xn * (1.0 + sc_ref[0]) + sh_ref[0]
    xb = xn.astype(BF16)
    if not moe:
        xn_o[...] = xb
    else:
        xn_o[...] = xn
        xl = (xn - xb.astype(F32)).astype(BF16)
        logits = _dot(xb, rh_ref[0]) + (_dot(xl, rh_ref[0]) + _dot(xb, rl_ref[0]))
        lane = _lane(logits.shape)
        logits = jnp.where(lane < N_EXP, logits, -jnp.inf)
        m1 = logits.max(axis=-1, keepdims=True)
        i1 = jnp.min(jnp.where(logits == m1, lane, LANES), axis=-1, keepdims=True)
        rest_l = jnp.where(lane == i1, -jnp.inf, logits)
        m2 = rest_l.max(axis=-1, keepdims=True)
        i2 = jnp.min(jnp.where(rest_l == m2, lane, LANES), axis=-1, keepdims=True)
        e2 = jnp.exp(m2 - m1)
        den = 1.0 + e2
        w1 = 1.0 / den
        w2 = e2 / den

        @pl.when(pl.program_id(0) == 0)
        def _():
            run_scr[...] = jnp.zeros(run_scr.shape, F32)

        sel = jnp.where(lane == i1, 1.0, jnp.where((lane == i2) & (w2 > 0.0), 1.0, 0.0))
        rb = 256
        before = (lax.broadcasted_iota(jnp.int32, (rb, rb), 1)
                  < lax.broadcasted_iota(jnp.int32, (rb, rb), 0)).astype(BF16)
        run = run_scr[...]
        ranks = []
        for k in range(TM // rb):
            part = sel[k * rb:(k + 1) * rb]
            ranks.append(run + _dot(before, part.astype(BF16)))
            run = run + jnp.sum(part, axis=0, keepdims=True)
        rank = jnp.concatenate(ranks, axis=0)
        r1 = jnp.sum(jnp.where(lane == i1, rank, 0.0), axis=-1, keepdims=True)
        r2 = jnp.sum(jnp.where(lane == i2, rank, 0.0), axis=-1, keepdims=True)
        run_scr[...] = run
        cnt_o[...] = run_scr[...]
        fields = [i1.astype(F32), i2.astype(F32), r1, r2, w1, w2]
        info = jnp.zeros(lane.shape, F32)
        for k, v in enumerate(fields):
            info = jnp.where(lane == k, v, info)
        route_o[...] = info.T


def _out_call(x_p, x_s, sample_off, oa, ob, oc, w_out_b, mod3, l, n2g, router):
    moe = router is not None
    row = lambda k: pl.BlockSpec((1, 1, D), lambda i: ((l * 8 + _mod_row(i)) * 6 + k, 0, 0))
    tok = lambda n: pl.BlockSpec((TM, n), lambda i: (i, 0))
    in_specs = (_pair_specs(D, sample_off) + _pair_specs(384, 0) + _pair_specs(256, 0) + _pair_specs(384, 0)
                + [pl.BlockSpec((1, D, D), lambda i: (0, 0, 0)),
                   row(2), row(3), row(4), pl.BlockSpec((1, D), lambda i: (0, 0))])
    args = [x_p, x_s, *oa, *ob, *oc, w_out_b, mod3, mod3, mod3, n2g]
    out_specs = [tok(D), tok(D)]
    out_shape = [jax.ShapeDtypeStruct((T, D), F32), jax.ShapeDtypeStruct((T, D), BF16)]
    if moe:
        out_shape[1] = jax.ShapeDtypeStruct((T, D), F32)
        rh, rl = router
        in_specs += [pl.BlockSpec((1, D, LANES), lambda i: (0, 0, 0))] * 2
        args += [rh, rl]
        out_specs += [pl.BlockSpec((LANES, TM), lambda i: (0, i)), pl.BlockSpec((1, LANES), lambda i: (0, 0))]
        out_shape += [jax.ShapeDtypeStruct((LANES, T), F32), jax.ShapeDtypeStruct((1, LANES), F32)]
    return pl.pallas_call(
        functools.partial(_out_kernel, moe),
        grid=(NT,),
        in_specs=in_specs, out_specs=out_specs, out_shape=out_shape,
        scratch_shapes=[pltpu.VMEM((1, LANES), F32)] if moe else [],
        compiler_params=_cparams(("arbitrary",)),
        name=f"out_proj_l{l}",
    )(*args)


FFN_TM = 1024
FFN_TF = 256
FFN_TN = 512


def _hidden_tile(x, w1_ref, w3_ref):
    a = _dot(x, w1_ref[...].astype(BF16))
    b = _dot(x, w3_ref[...].astype(BF16))
    return (a * jax.nn.sigmoid(a) * b).astype(BF16)


def _down_tile(h_scr, w2n, m):
    out = None
    for f in range(h_scr.shape[0]):
        o = _dot(h_scr[f, 0:m, :], w2n[f])
        out = o if out is None else out + o
    return out


def _keep_weights(s, w1_ref, w3_ref, w2_ref, w1b, w3b, w2b, tn):
    w1b[s] = w1_ref[...].astype(BF16)
    w3b[s] = w3_ref[...].astype(BF16)
    w2 = w2_ref[...].astype(BF16)
    for n in range(w2b.shape[0]):
        w2b[n, s] = w2[:, n * tn:(n + 1) * tn]


def _ffn_kernel(nf, xn_ref, x1_ref, g2_ref, w1_ref, w3_ref, w2_ref, y_ref, h_scr, w1b, w3b, w2b):
    i = pl.program_id(0)
    s = pl.program_id(1)

    @pl.when((i == 0) & (s < nf))
    def _():
        _keep_weights(s, w1_ref.at[0], w3_ref.at[0], w2_ref.at[0], w1b, w3b, w2b, FFN_TN)

    @pl.when(s < nf)
    def _():
        h_scr[s] = _hidden_tile(xn_ref[...], w1b.at[s], w3b.at[s])

    @pl.when(s >= nf)
    def _():
        y_ref[...] = x1_ref[...] + g2_ref[0] * _down_tile(h_scr, w2b.at[s - nf], FFN_TM)


def _ffn_call(xn, x1, mod3, l, w1, w3, w2, i_layer):
    nf = D_FF // FFN_TF
    nn = D // FFN_TN
    tile = lambda i, s: jnp.where(i == 0, jnp.minimum(s, nf - 1), nf - 1)
    out = lambda s: jnp.maximum(s - nf, 0)

    def g2_idx(i, s):
        r = jnp.where(i < T_P // FFN_TM, 0, 1 + (i - T_P // FFN_TM) // (S_S // FFN_TM))
        return ((l * 8 + r) * 6 + 5, 0, out(s))

    return pl.pallas_call(
        functools.partial(_ffn_kernel, nf),
        grid=(T // FFN_TM, nf + nn),
        in_specs=[pl.BlockSpec((FFN_TM, D), lambda i, s: (i, 0)),
                  pl.BlockSpec((FFN_TM, FFN_TN), lambda i, s: (i, out(s))),
                  pl.BlockSpec((1, 1, FFN_TN), g2_idx),
                  pl.BlockSpec((1, D, FFN_TF), lambda i, s: (i_layer, 0, tile(i, s))),
                  pl.BlockSpec((1, D, FFN_TF), lambda i, s: (i_layer, 0, tile(i, s))),
                  pl.BlockSpec((1, FFN_TF, D), lambda i, s: (i_layer, tile(i, s), 0))],
        out_specs=pl.BlockSpec((FFN_TM, FFN_TN), lambda i, s: (i, out(s))),
        out_shape=jax.ShapeDtypeStruct((T, D), F32),
        scratch_shapes=[pltpu.VMEM((nf, FFN_TM, FFN_TF), BF16),
                        pltpu.VMEM((nf, D, FFN_TF), BF16), pltpu.VMEM((nf, D, FFN_TF), BF16),
                        pltpu.VMEM((nn, nf, FFN_TF, FFN_TN), BF16)],
        compiler_params=_cparams(("arbitrary", "arbitrary")),
        name="ffn_dense",
    )(xn, x1, mod3, w1, w3, w2)


MOE_BM = 1024
MOE_SUB = 256
MOE_NBLK = 2 * T // MOE_BM + N_EXP
MOE_NR = MOE_NBLK * MOE_BM
MOE_TF = 512
MOE_TN = 512
FIN_TM = 512


def _route_kernel(i1_ref, i2_ref, r1_ref, r2_ref, cnt_ref, w2_ref,
                  src_ref, eb_ref, rows_ref, nvb_ref, off_scr):
    def clear(r, carry):
        src_ref[r] = 0
        return carry

    lax.fori_loop(0, MOE_NR + 8, clear, 0, unroll=16)

    def clear_blk(b, carry):
        eb_ref[b] = 0
        rows_ref[b] = 0
        return carry

    lax.fori_loop(0, MOE_NBLK, clear_blk, 0)

    nblk = jnp.int32(0)
    for e in range(N_EXP):
        c = cnt_ref[e]
        nbe = (c + (MOE_BM - 1)) // MOE_BM
        off_scr[e] = nblk * MOE_BM

        def fill(j, carry, e=e, c=c, nblk=nblk):
            eb_ref[nblk + j] = e
            rows_ref[nblk + j] = jnp.minimum(c - j * MOE_BM, MOE_BM)
            return carry

        lax.fori_loop(0, nbe, fill, 0)
        nblk = nblk + nbe
    nvb_ref[0] = nblk

    def place(t, carry):
        src_ref[off_scr[i1_ref[t]] + r1_ref[t]] = t
        src_ref[jnp.where(w2_ref[t] > 0.0, off_scr[i2_ref[t]] + r2_ref[t], MOE_NR)] = t
        return carry

    lax.fori_loop(0, T, place, 0, unroll=8)


def _route_call(route, cnt):
    ints = [route[k].astype(jnp.int32) for k in range(4)]
    cnt8 = cnt[0, :N_EXP].astype(jnp.int32)
    smem = pl.BlockSpec(memory_space=pltpu.SMEM)
    outs = pl.pallas_call(
        _route_kernel,
        grid_spec=pltpu.PrefetchScalarGridSpec(
            num_scalar_prefetch=5, grid=(1,),
            in_specs=[smem],
            out_specs=[smem, smem, smem, smem],
            scratch_shapes=[pltpu.SMEM((N_EXP,), jnp.int32)]),
        out_shape=[jax.ShapeDtypeStruct((MOE_NR + 8,), jnp.int32),
                   jax.ShapeDtypeStruct((MOE_NBLK,), jnp.int32), jax.ShapeDtypeStruct((MOE_NBLK,), jnp.int32),
                   jax.ShapeDtypeStruct((1,), jnp.int32)],
        compiler_params=_cparams(("arbitrary",)),
        name="moe_route",
    )(*ints, cnt8, route[5])
    return tuple(outs) + (ints[0], route[4], route[5])


def _moe_gather_kernel(src_ref, rows_ref, x_ref, o_ref, rows_scr):
    b = pl.program_id(0)
    for q in range(MOE_BM // MOE_SUB):
        base = b * MOE_BM + q * MOE_SUB
        used = rows_ref[b] > q * MOE_SUB
        sub = pl.ds(q * MOE_SUB, MOE_SUB)

        @pl.when(used)
        def _(base=base, sub=sub):
            def body(i, carry):
                for j in range(8):
                    r = i * 8 + j
                    rows_scr[pl.ds(r, 1), :] = x_ref[pl.ds(src_ref[base + r], 1), :]
                return carry

            lax.fori_loop(0, MOE_SUB // 8, body, 0)
            o_ref[sub, :] = rows_scr[...].astype(BF16)

        @pl.when(jnp.logical_not(used))
        def _(sub=sub):
            o_ref[sub, :] = jnp.zeros((MOE_SUB, D), BF16)


def _moe_gather_call(src, rows_b, xn):
    return pl.pallas_call(
        _moe_gather_kernel,
        grid_spec=pltpu.PrefetchScalarGridSpec(
            num_scalar_prefetch=2, grid=(MOE_NBLK,),
            in_specs=[pl.BlockSpec((T, D), lambda b, s, rw: (0, 0), pipeline_mode=pl.Buffered(1))],
            out_specs=pl.BlockSpec((MOE_BM, D), lambda b, s, rw: (b, 0)),
            scratch_shapes=[pltpu.VMEM((MOE_SUB, D), F32)]),
        out_shape=jax.ShapeDtypeStruct((MOE_NR, D), BF16),
        compiler_params=_cparams(("arbitrary",)),
        name="moe_gather",
    )(src, rows_b, xn)


def _moe_first(b, eb_ref):
    return (b == 0) | (eb_ref[b] != eb_ref[jnp.maximum(b - 1, 0)])


def _moe_kernel(nf, eb_ref, rows_ref, nvb_ref, xs_ref, w1_ref, w3_ref, w2_ref, y_ref, h_scr, w1b, w3b, w2b):
    b = pl.program_id(0)
    s = pl.program_id(1)
    nsub = jnp.where(b < nvb_ref[0], (rows_ref[b] + MOE_SUB - 1) // MOE_SUB, 0)
    sizes = [k * MOE_SUB for k in range(1, MOE_BM // MOE_SUB + 1)]

    @pl.when((s < nf) & (nsub > 0) & _moe_first(b, eb_ref))
    def _():
        _keep_weights(s, w1_ref.at[0, 0], w3_ref.at[0, 0], w2_ref.at[0, 0], w1b, w3b, w2b, MOE_TN)

    @pl.when((s < nf) & (nsub > 0))
    def _():
        for k, m in enumerate(sizes, start=1):
            @pl.when(nsub == k)
            def _(m=m):
                h_scr[s, 0:m, :] = _hidden_tile(xs_ref[0:m, :], w1b.at[s], w3b.at[s])

    @pl.when((s >= nf) & (nsub > 0))
    def _():
        for k, m in enumerate(sizes, start=1):
            @pl.when(nsub == k)
            def _(m=m):
                y_ref[0:m, :] = _down_tile(h_scr, w2b.at[s - nf], m)
                if m < MOE_BM:
                    y_ref[m:MOE_BM, :] = jnp.zeros((MOE_BM - m, MOE_TN), F32)

    @pl.when((s >= nf) & (nsub == 0))
    def _():
        y_ref[...] = jnp.zeros(y_ref.shape, F32)


def _moe_call(xs, e_b, rows_b, nvb, w1, w3, w2, i_layer):
    nf = D_FFE // MOE_TF
    nn = D // MOE_TN

    def blk(b, nv):
        return jnp.minimum(b, nv[0] - 1)

    def tile(b, s, eb, nv):
        stream = (b < nv[0]) & _moe_first(b, eb)
        return jnp.where(stream, jnp.minimum(s, nf - 1), nf - 1)

    return pl.pallas_call(
        functools.partial(_moe_kernel, nf),
        grid_spec=pltpu.PrefetchScalarGridSpec(
            num_scalar_prefetch=3, grid=(MOE_NBLK, nf + nn),
            in_specs=[pl.BlockSpec((MOE_BM, D), lambda b, s, eb, rw, nv: (blk(b, nv), 0)),
                      pl.BlockSpec((1, 1, D, MOE_TF),
                                   lambda b, s, eb, rw, nv: (i_layer, eb[blk(b, nv)], 0, tile(b, s, eb, nv))),
                      pl.BlockSpec((1, 1, D, MOE_TF),
                                   lambda b, s, eb, rw, nv: (i_layer, eb[blk(b, nv)], 0, tile(b, s, eb, nv))),
                      pl.BlockSpec((1, 1, MOE_TF, D),
                                   lambda b, s, eb, rw, nv: (i_layer, eb[blk(b, nv)], tile(b, s, eb, nv), 0))],
            out_specs=pl.BlockSpec((MOE_BM, MOE_TN), lambda b, s, eb, rw, nv: (b, jnp.maximum(s - nf, 0))),
            scratch_shapes=[pltpu.VMEM((nf, MOE_BM, MOE_TF), BF16),
                            pltpu.VMEM((nf, D, MOE_TF), BF16), pltpu.VMEM((nf, D, MOE_TF), BF16),
                            pltpu.VMEM((nn, nf, MOE_TF, MOE_TN), BF16)]),
        out_shape=jax.ShapeDtypeStruct((MOE_NR, D), F32),
        compiler_params=_cparams(("arbitrary", "arbitrary"), vmem=56 * 1024 * 1024),
        name="moe_experts",
    )(e_b, rows_b, nvb, xs, w1, w3, w2)


def _moe_combine_kernel(src_ref, eb_ref, rows_ref, nvb_ref, i1_ref, w1_ref, w2_ref,
                        ys_ref, x1_ref, g2_ref, yp_ref, ys_out_ref, acc):
    i = pl.program_id(0)

    @pl.when(i == 0)
    def _():
        acc[...] = jnp.zeros(acc.shape, F32)

    @pl.when(i < nvb_ref[0])
    def _():
        base = i * MOE_BM
        n = rows_ref[i]
        e = eb_ref[i]

        def gate(t):
            return jnp.where(i1_ref[t] == e, w1_ref[t], w2_ref[t])

        def add_rows(r0, cnt):
            toks = [src_ref[base + r0 + j] for j in range(cnt)]
            vals = [acc[pl.ds(toks[j], 1), :] + gate(toks[j]) * ys_ref[pl.ds(r0 + j, 1), :]
                    for j in range(cnt)]
            for j in range(cnt):
                acc[pl.ds(toks[j], 1), :] = vals[j]

        def body4(q, carry):
            add_rows(q * 4, 4)
            return carry

        lax.fori_loop(0, n // 4, body4, 0)

        def body1(r, carry):
            add_rows(r, 1)
            return carry

        lax.fori_loop((n // 4) * 4, n, body1, 0)

    @pl.when(i >= MOE_NBLK)
    def _():
        t0 = pl.multiple_of((i - MOE_NBLK) * FIN_TM, FIN_TM)
        y = x1_ref[...] + g2_ref[0] * acc[pl.ds(t0, FIN_TM), :]

        @pl.when(i - MOE_NBLK < T_P // FIN_TM)
        def _():
            yp_ref[...] = y

        @pl.when(i - MOE_NBLK >= T_P // FIN_TM)
        def _():
            ys_out_ref[...] = y


def _moe_combine_call(src, e_b, rows_b, nvb, i1, w1, w2, ys, x1, mod3, l):
    nfin = T // FIN_TM
    smem = pl.BlockSpec(memory_space=pltpu.SMEM)

    def g2_idx(i, *_):
        j = jnp.maximum(i - MOE_NBLK, 0)
        r = jnp.where(j < T_P // FIN_TM, 0, 1 + (j - T_P // FIN_TM) // (S_S // FIN_TM))
        return ((l * 8 + r) * 6 + 5, 0, 0)

    return pl.pallas_call(
        _moe_combine_kernel,
        grid_spec=pltpu.PrefetchScalarGridSpec(
            num_scalar_prefetch=5, grid=(MOE_NBLK + nfin,),
            in_specs=[smem, smem,
                      pl.BlockSpec((MOE_BM, D),
                                   lambda i, s, eb, rw, nv, t1: (jnp.minimum(jnp.minimum(i, MOE_NBLK - 1), nv[0] - 1), 0)),
                      pl.BlockSpec((FIN_TM, D), lambda i, *_: (jnp.maximum(i - MOE_NBLK, 0), 0)),
                      pl.BlockSpec((1, 1, D), g2_idx)],
            out_specs=[pl.BlockSpec((FIN_TM, D),
                                    lambda i, *_: (jnp.clip(i - MOE_NBLK, 0, T_P // FIN_TM - 1), 0)),
                       pl.BlockSpec((FIN_TM, D),
                                    lambda i, *_: (jnp.maximum(i - MOE_NBLK - T_P // FIN_TM, 0), 0))],
            scratch_shapes=[pltpu.VMEM((T, D), F32)]),
        out_shape=[jax.ShapeDtypeStruct((T_P, D), F32), jax.ShapeDtypeStruct((T_S, D), F32)],
        compiler_params=_cparams(("arbitrary",), vmem=52 * 1024 * 1024),
        name="moe_combine",
    )(src, e_b, rows_b, nvb, i1, w1, w2, ys, x1, mod3)


def _moe_layer(xp, x1, route, cnt, mod3, l, w1, w3, w2, i_layer):
    src, e_b, rows_b, nvb, i1, g1, g2 = _route_call(route, cnt)
    xs = _moe_gather_call(src, rows_b, xp)
    ys = _moe_call(xs, e_b, rows_b, nvb, w1, w3, w2, i_layer)
    return _moe_combine_call(src, e_b, rows_b, nvb, i1, g1, g2, ys, x1, mod3, l)


def _pad_cols(w, n):
    return jnp.pad(w, ((0, 0), (0, n - w.shape[1])))


def _tail_selector():
    sel = np.zeros((IN_TAIL, NP_IN - IN_MAIN), np.float32)
    for d in range(2):
        for gate in range(2):
            for head in range(B_HEADS):
                sel[d * 8 + gate * 4 + head, gate * 256 + (head // 2) * LANES + d * 2 + head % 2] = 1.0
    for c in range(C_QRANK):
        sel[16 + c, 512 + c] = 1.0
    for c in range(C_KVRANK):
        sel[16 + C_QRANK + c, 768 + c] = 1.0
    for c in range(C_ROPE):
        sel[16 + C_QRANK + C_KVRANK + c, 896 + C_NOPE + c] = 1.0
    return sel


def _relayout_gate_b(b):
    return jnp.dot(b[None, :], jnp.asarray(_tail_selector()[:16, :512]), precision=lax.Precision.HIGHEST)


def _pad_heads(w, width):
    r = w.shape[0]
    h = w.shape[1] // width
    return jnp.pad(w.reshape(r, h, width), ((0, 0), (0, 0), (0, LANES - width))).reshape(r, h * LANES)


def _rope_tables(half, span_start, period):
    rows = S_S // 64
    r = jnp.repeat(jnp.arange(rows), 64).astype(F32)
    c = jnp.tile(jnp.arange(64), rows).astype(F32)
    n_freq = half // 2
    freq = 10000.0 ** (-jnp.arange(n_freq, dtype=F32) / n_freq)
    ang = jnp.concatenate([r[:, None] * freq, c[:, None] * freq], axis=-1)
    cos, sin = jnp.cos(ang), jnp.sin(ang)
    d = (jnp.arange(LANES) - span_start) % period
    inside = d < 2 * half
    p = jnp.where(inside, d % half, 0)
    first = inside & (d < half)
    second = inside & (d >= half)
    ct = jnp.where(inside[None, :], cos[:, p], 1.0)
    sa = jnp.where(first[None, :], -sin[:, p], 0.0)
    sb = jnp.where(second[None, :], sin[:, p], 0.0)
    ident = (jnp.ones((TM, LANES), F32), jnp.zeros((TM, LANES), F32), jnp.zeros((TM, LANES), F32))
    return tuple(jnp.concatenate([i0, t], axis=0) for i0, t in zip(ident, (ct, sa, sb)))


def kernel(x_prompt, x_sample, c, cache_swa_k, cache_swa_v, cache_mla_ckv, cache_mla_krope, state_mlstm_C, state_mlstm_n, state_mlstm_m, c_ctx, ada_w, ada_b, norm1_g, norm2_g, w_in, a_qn_g, a_kn_g, a_sink, b_gate_b, b_hn_g, c_qa_g, c_kva_g, c_wuq, c_wukv, c_qn_g, c_kn_g, w_out, ffn_w1, ffn_w3, ffn_w2, moe_router, moe_w1, moe_w3, moe_w2):
    x_p, x_s, sample_off = x_prompt.reshape(T_P, D), x_sample.reshape(T_S, D), 0
    cv = jnp.concatenate([c_ctx[None, :], c, jnp.zeros((5, D), F32)], axis=0)
    mod = _ada_call(cv, ada_w, ada_b)
    mod3 = mod.reshape(DEPTH * 8 * 6, 1, D)

    sel = jnp.asarray(_tail_selector(), BF16)
    w_in_t = jnp.swapaxes(w_in, 1, 2)
    rope_a = _rope_tables(32, 0, HD)
    rope_c = _rope_tables(16, C_NOPE, LANES)

    cache_k4 = cache_swa_k.reshape(NB_S, DEPTH, PAST, A_KV * HD)
    cache_v4 = cache_swa_v.reshape(NB_S, DEPTH, PAST, A_KV * HD)

    news = []
    for l in range(DEPTH):
        wuq_p = _pad_heads(c_wuq[l], C_QK).astype(BF16)[None]
        wukv = c_wukv[l].reshape(C_KVRANK, C_HEADS, C_NOPE + C_V)
        wk_p = jnp.pad(wukv[..., :C_NOPE], ((0, 0), (0, 0), (0, LANES - C_NOPE)))
        wk_p = wk_p.reshape(1, C_KVRANK, C_HEADS * LANES).astype(BF16)
        wv_p = wukv[..., C_NOPE:].reshape(1, C_KVRANK, C_HEADS * C_V).astype(BF16)
        w_out_b = w_out[l].astype(BF16)[None]
        gqa = jnp.tile(a_qn_g[l], A_HEADS)[None, :]
        gka = jnp.tile(a_kn_g[l], A_KV)[None, :]
        gb = _relayout_gate_b(b_gate_b[l])
        gqn = jnp.tile(jnp.pad(c_qn_g[l], (0, LANES - C_QK)), C_HEADS)[None, :]
        gkn = jnp.tile(jnp.pad(c_kn_g[l], (0, LANES - C_QK)), C_HEADS)[None, :]
        (qa, ka, va, kaf, vaf, bq, bk, bv, bo, li, lf, qc, ckv_n, ckr) = _in_call(
            x_p, x_s, sample_off, mod3, l, norm1_g[l][None, :], w_in_t, sel, gqa, gka, gb, c_qa_g[l][None, :],
            c_kva_g[l][None, :], wuq_p, gqn, rope_a, rope_c)

        ckr_cache = jnp.pad(cache_mla_krope[:, l].reshape(NB_S * PAST, C_ROPE),
                            ((0, 0), (C_NOPE, LANES - C_NOPE - C_ROPE)))
        kc, vc = _mlakv_call(ckv_n, cache_mla_ckv[:, l].reshape(NB_S * PAST, C_KVRANK), ckr, ckr_cache,
                             l, wk_p, wv_p, gkn, rope_c)

        sink = a_sink[l]
        oa_p = _attn_a_prompt_call(sink, qa, ka, va)
        oa_s = _attn_a_sample_call(sink, qa, ka, va, cache_k4, cache_v4, l)
        oc_p = _mla_prompt_call(qc, kc, vc)
        oc_s = _mla_sample_call(qc, kc, vc)

        hn2 = jnp.tile(b_hn_g[l], 2)[None, :]
        ob_p, cn, nn, mn = _mlstm_call(bq, bk, bv, bo, li, lf, hn2, None, l)
        c0, n0 = state_mlstm_C[:, l], state_mlstm_n[:, l]
        m_st = state_mlstm_m[:, l].reshape(NB_S, 2, 2, 2)
        m0 = jnp.transpose(m_st, (0, 2, 1, 3)).reshape(NB_S * 2, 1, 4)
        m0 = jnp.pad(m0, ((0, 0), (0, 0), (0, LANES - 4)))
        ob_s = _mlstm_call(bq, bk, bv, bo, li, lf, hn2, (c0, n0, m0), l)

        oa, ob, oc = (oa_p, oa_s), (ob_p, ob_s), (oc_p, oc_s)

        if l % 2 == 0:
            x1, xn = _out_call(x_p, x_s, sample_off, oa, ob, oc, w_out_b, mod3, l, norm2_g[l][None, :], None)
            x = _ffn_call(xn, x1, mod3, l, ffn_w1, ffn_w3, ffn_w2, l // 2)
            x_p, x_s, sample_off = x, x, NT_P
        else:
            r = _pad_cols(moe_router[l // 2], LANES)
            rh = r.astype(BF16)
            rl = (r - rh.astype(F32)).astype(BF16)
            x1, xn, route, cnt = _out_call(x_p, x_s, sample_off, oa, ob, oc, w_out_b, mod3, l,
                                           norm2_g[l][None, :], (rh[None], rl[None]))
            x_p, x_s = _moe_layer(xn, x1, route, cnt, mod3, l, moe_w1, moe_w3, moe_w2, l // 2)
            sample_off = 0

        new_k = kaf.reshape(NB_P, A_KV, HD, S_P)
        new_v = vaf.reshape(NB_P, A_KV, HD, S_P)
        new_ckv = ckv_n[:T_P].reshape(NB_P, S_P, C_KVRANK)
        new_kr = ckr[:T_P, C_NOPE:C_NOPE + C_ROPE].reshape(NB_P, S_P, C_ROPE)
        new_c, new_n = cn, nn
        mn4 = mn.reshape(NB_P, 2, LANES)[:, :, :4].reshape(NB_P, 2, 2, 2)
        new_m = jnp.transpose(mn4, (0, 2, 1, 3)).reshape(NB_P, 2, B_HEADS)
        news.append((new_k, new_v, new_ckv, new_kr, new_c, new_n, new_m))

    y_prompt = x_p[:T_P].reshape(NB_P, S_P, D)
    y_sample = x_s[sample_off * TM:sample_off * TM + T_S].reshape(NB_S, S_S, D)
    stacked = [jnp.stack([nw[j] for nw in news], axis=1) for j in range(7)]
    for j in (0, 1):
        stacked[j] = jnp.transpose(stacked[j], (0, 1, 4, 2, 3))
    return (y_prompt, y_sample) + tuple(stacked)
```
